```python
import jax, jax.numpy as jnp
from jax import lax
import numpy as np

D_MODEL = 2048
BATCH = 8
SEQ = 2048
DEPTH = 2

HEAD_DIM = 128
SB_WIDTH = D_MODEL // 2
N_SB_HEADS = SB_WIDTH // HEAD_DIM
SGU_WIDTH = D_MODEL - SB_WIDTH
N_SGU_GROUPS = 8
SGU_GROUP_DIM = SGU_WIDTH // N_SGU_GROUPS
MIX_WIDTH = SB_WIDTH + SGU_WIDTH
IN_WIDTH = 3 * SB_WIDTH + 2 * SGU_WIDTH
SB_BLOCK = 128
CHUNK = 128
D_FF = 5632
CONV_WIDTH = 3
EPS = 1e-6

kernel_name = "hybrid_stickbreaking_sgu_convffn"


def rms_norm(x, g):
    xf = x.astype(jnp.float32)
    y = xf * lax.rsqrt(jnp.mean(xf * xf, axis=-1, keepdims=True) + EPS)
    return (y * g.astype(jnp.float32)).astype(x.dtype)


def stick_breaking_attention(q, k, v):
    B, S, H, Dh = q.shape
    scale = Dh ** -0.5
    outs = []
    for i in range(S // SB_BLOCK):
        q0 = i * SB_BLOCK
        kv_len = q0 + SB_BLOCK
        q_blk = q[:, q0:kv_len]
        k_blk = k[:, :kv_len]
        v_blk = v[:, :kv_len]
        z = jnp.einsum('bthd,bshd->bhts', q_blk, k_blk,
                       preferred_element_type=jnp.float32) * scale
        t_idx = q0 + jnp.arange(SB_BLOCK)[:, None]
        s_idx = jnp.arange(kv_len)[None, :]
        mask = s_idx < t_idx
        log_beta = jax.nn.log_sigmoid(z)
        log_1m_beta = jnp.where(mask, jax.nn.log_sigmoid(-z), 0.0)
        tail = lax.cumsum(log_1m_beta, axis=3, reverse=True) - log_1m_beta
        a = jnp.where(mask, jnp.exp(log_beta + tail), 0.0)
        o = jnp.einsum('bhts,bshd->bthd', a.astype(v.dtype), v_blk)
        outs.append(o)
    return jnp.concatenate(outs, axis=1)


def chunked_spatial_gating(u, v, w_s, b_s):
    B, S, G, dg = v.shape
    n_chunks = S // CHUNK
    causal = jnp.tril(jnp.ones((CHUNK, CHUNK), dtype=bool))
    w = jnp.where(causal[None], w_s, 0.0).astype(v.dtype)
    vc = v.reshape(B, n_chunks, CHUNK, G, dg)
    mixed = jnp.einsum('gts,bcsgd->bctgd', w, vc) + b_s.T[None, None, :, :, None]
    return u * mixed.reshape(B, S, G, dg)


def causal_depthwise_conv(h, w, b):
    S = h.shape[1]
    hp = jnp.pad(h, ((0, 0), (CONV_WIDTH - 1, 0), (0, 0)))
    out = b
    for j in range(CONV_WIDTH):
        out = out + hp[:, j:j + S] * w[j]
    return out


def _fwd_setup_inputs(seed: int = 0) -> dict:
    key = jax.random.key(seed)
    ks = jax.random.split(key, 20)
    f32 = jnp.float32
    nrm = lambda k, shape, s: jax.random.normal(k, shape, f32) * s
    gain = lambda k, shape: 1.0 + 0.02 * jax.random.normal(k, shape, f32)
    return {
        "x": jax.random.normal(ks[0], (BATCH, SEQ, D_MODEL), f32),
        "attn_norm_g": gain(ks[1], (DEPTH, D_MODEL)),
        "w_in": nrm(ks[2], (DEPTH, D_MODEL, IN_WIDTH), D_MODEL ** -0.5),
        "q_norm_g": gain(ks[3], (DEPTH, HEAD_DIM)),
        "k_norm_g": gain(ks[4], (DEPTH, HEAD_DIM)),
        "sgu_norm_g": gain(ks[5], (DEPTH, N_SGU_GROUPS, SGU_GROUP_DIM)),
        "sgu_w": nrm(ks[6], (DEPTH, N_SGU_GROUPS, CHUNK, CHUNK), CHUNK ** -0.5),
        "sgu_b": 1.0 + 0.02 * jax.random.normal(ks[7], (DEPTH, N_SGU_GROUPS, CHUNK), f32),
        "out_norm_a_g": gain(ks[8], (DEPTH, N_SB_HEADS, HEAD_DIM)),
        "out_norm_b_g": gain(ks[9], (DEPTH, N_SGU_GROUPS, SGU_GROUP_DIM)),
        "w_out": nrm(ks[10], (DEPTH, MIX_WIDTH, D_MODEL), (2 * DEPTH * MIX_WIDTH) ** -0.5),
        "ffn_norm_g": gain(ks[11], (DEPTH, D_MODEL)),
        "w_up": nrm(ks[12], (DEPTH, D_MODEL, 2 * D_FF), D_MODEL ** -0.5),
        "conv_w": nrm(ks[13], (DEPTH, CONV_WIDTH, 2 * D_FF), CONV_WIDTH ** -0.5),
        "conv_b": nrm(ks[14], (DEPTH, 2 * D_FF), 0.02),
        "w_down": nrm(ks[15], (DEPTH, D_FF, D_MODEL), (2 * DEPTH * D_FF) ** -0.5),
    }


def _fwd_reference(x, attn_norm_g, w_in, q_norm_g, k_norm_g, sgu_norm_g, sgu_w, sgu_b,
              out_norm_a_g, out_norm_b_g, w_out, ffn_norm_g, w_up, conv_w, conv_b, w_down):
    B, S, _ = x.shape
    for l in range(DEPTH):
        h = rms_norm(x, attn_norm_g[l])
        p = h @ w_in[l]
        q, k, va, u_pre, v_pre = jnp.split(
            p, [SB_WIDTH, 2 * SB_WIDTH, 3 * SB_WIDTH, 3 * SB_WIDTH + SGU_WIDTH], axis=-1)
        q = rms_norm(q.reshape(B, S, N_SB_HEADS, HEAD_DIM), q_norm_g[l])
        k = rms_norm(k.reshape(B, S, N_SB_HEADS, HEAD_DIM), k_norm_g[l])
        va = va.reshape(B, S, N_SB_HEADS, HEAD_DIM)
        att = stick_breaking_attention(q, k, va)
        att = rms_norm(att, out_norm_a_g[l])

        u = jax.nn.gelu(u_pre, approximate=False).reshape(B, S, N_SGU_GROUPS, SGU_GROUP_DIM)
        vs = jax.nn.gelu(v_pre, approximate=False).reshape(B, S, N_SGU_GROUPS, SGU_GROUP_DIM)
        vs = rms_norm(vs, sgu_norm_g[l])
        sg = chunked_spatial_gating(u, vs, sgu_w[l], sgu_b[l])
        sg = rms_norm(sg, out_norm_b_g[l])

        mix = jnp.concatenate([att.reshape(B, S, SB_WIDTH),
                               sg.reshape(B, S, SGU_WIDTH)], axis=-1)
        x = x + mix @ w_out[l]

        h = rms_norm(x, ffn_norm_g[l])
        up = causal_depthwise_conv(h @ w_up[l], conv_w[l], conv_b[l])
        gate, val = jnp.split(up, 2, axis=-1)
        x = x + (jax.nn.silu(gate) * val) @ w_down[l]
    return x


import jax as _jax
import jax.numpy as _jnp

TWIN_FORMAT = 'train_step'
FWD_PARAMS = ['x', 'attn_norm_g', 'w_in', 'q_norm_g', 'k_norm_g', 'sgu_norm_g', 'sgu_w', 'sgu_b', 'out_norm_a_g', 'out_norm_b_g', 'w_out', 'ffn_norm_g', 'w_up', 'conv_w', 'conv_b', 'w_down']
TWIN_WEIGHTS = ['attn_norm_g', 'w_in', 'q_norm_g', 'k_norm_g', 'sgu_norm_g', 'sgu_w', 'sgu_b', 'out_norm_a_g', 'out_norm_b_g', 'w_out', 'ffn_norm_g', 'w_up', 'conv_w', 'conv_b', 'w_down']
TWIN_DIFF_INPUT = 'x'
TWIN_INPUTS = ['x', 'attn_norm_g', 'w_in', 'q_norm_g', 'k_norm_g', 'sgu_norm_g', 'sgu_w', 'sgu_b', 'out_norm_a_g', 'out_norm_b_g', 'w_out', 'ffn_norm_g', 'w_up', 'conv_w', 'conv_b', 'w_down', 'loss_target', 'm_attn_norm_g', 'm_w_in', 'm_q_norm_g', 'm_k_norm_g', 'm_sgu_norm_g', 'm_sgu_w', 'm_sgu_b', 'm_out_norm_a_g', 'm_out_norm_b_g', 'm_w_out', 'm_ffn_norm_g', 'm_w_up', 'm_conv_w', 'm_conv_b', 'm_w_down', 'v_attn_norm_g', 'v_w_in', 'v_q_norm_g', 'v_k_norm_g', 'v_sgu_norm_g', 'v_sgu_w', 'v_sgu_b', 'v_out_norm_a_g', 'v_out_norm_b_g', 'v_w_out', 'v_ffn_norm_g', 'v_w_up', 'v_conv_w', 'v_conv_b', 'v_w_down']
TWIN_OUTPUTS = ['loss', 'grad_x', 'grad_attn_norm_g', 'grad_w_in', 'grad_q_norm_g', 'grad_k_norm_g', 'grad_sgu_norm_g', 'grad_sgu_w', 'grad_sgu_b', 'grad_out_norm_a_g', 'grad_out_norm_b_g', 'grad_w_out', 'grad_ffn_norm_g', 'grad_w_up', 'grad_conv_w', 'grad_conv_b', 'grad_w_down', 'delta_attn_norm_g', 'delta_w_in', 'delta_q_norm_g', 'delta_k_norm_g', 'delta_sgu_norm_g', 'delta_sgu_w', 'delta_sgu_b', 'delta_out_norm_a_g', 'delta_out_norm_b_g', 'delta_w_out', 'delta_ffn_norm_g', 'delta_w_up', 'delta_conv_w', 'delta_conv_b', 'delta_w_down', 'new_m_attn_norm_g', 'new_m_w_in', 'new_m_q_norm_g', 'new_m_k_norm_g', 'new_m_sgu_norm_g', 'new_m_sgu_w', 'new_m_sgu_b', 'new_m_out_norm_a_g', 'new_m_out_norm_b_g', 'new_m_w_out', 'new_m_ffn_norm_g', 'new_m_w_up', 'new_m_conv_w', 'new_m_conv_b', 'new_m_w_down', 'new_v_attn_norm_g', 'new_v_w_in', 'new_v_q_norm_g', 'new_v_k_norm_g', 'new_v_sgu_norm_g', 'new_v_sgu_w', 'new_v_sgu_b', 'new_v_out_norm_a_g', 'new_v_out_norm_b_g', 'new_v_w_out', 'new_v_ffn_norm_g', 'new_v_w_up', 'new_v_conv_w', 'new_v_conv_b', 'new_v_w_down']
TWIN_LEAF_KINDS = {'loss': 'loss', 'grad_x': 'grad_x', 'grad_attn_norm_g': 'grad_w', 'grad_w_in': 'grad_w', 'grad_q_norm_g': 'grad_w', 'grad_k_norm_g': 'grad_w', 'grad_sgu_norm_g': 'grad_w', 'grad_sgu_w': 'grad_w', 'grad_sgu_b': 'grad_w', 'grad_out_norm_a_g': 'grad_w', 'grad_out_norm_b_g': 'grad_w', 'grad_w_out': 'grad_w', 'grad_ffn_norm_g': 'grad_w', 'grad_w_up': 'grad_w', 'grad_conv_w': 'grad_w', 'grad_conv_b': 'grad_w', 'grad_w_down': 'grad_w', 'delta_attn_norm_g': 'delta_w', 'delta_w_in': 'delta_w', 'delta_q_norm_g': 'delta_w', 'delta_k_norm_g': 'delta_w', 'delta_sgu_norm_g': 'delta_w', 'delta_sgu_w': 'delta_w', 'delta_sgu_b': 'delta_w', 'delta_out_norm_a_g': 'delta_w', 'delta_out_norm_b_g': 'delta_w', 'delta_w_out': 'delta_w', 'delta_ffn_norm_g': 'delta_w', 'delta_w_up': 'delta_w', 'delta_conv_w': 'delta_w', 'delta_conv_b': 'delta_w', 'delta_w_down': 'delta_w', 'new_m_attn_norm_g': 'new_m', 'new_m_w_in': 'new_m', 'new_m_q_norm_g': 'new_m', 'new_m_k_norm_g': 'new_m', 'new_m_sgu_norm_g': 'new_m', 'new_m_sgu_w': 'new_m', 'new_m_sgu_b': 'new_m', 'new_m_out_norm_a_g': 'new_m', 'new_m_out_norm_b_g': 'new_m', 'new_m_w_out': 'new_m', 'new_m_ffn_norm_g': 'new_m', 'new_m_w_up': 'new_m', 'new_m_conv_w': 'new_m', 'new_m_conv_b': 'new_m', 'new_m_w_down': 'new_m', 'new_v_attn_norm_g': 'new_v', 'new_v_w_in': 'new_v', 'new_v_q_norm_g': 'new_v', 'new_v_k_norm_g': 'new_v', 'new_v_sgu_norm_g': 'new_v', 'new_v_sgu_w': 'new_v', 'new_v_sgu_b': 'new_v', 'new_v_out_norm_a_g': 'new_v', 'new_v_out_norm_b_g': 'new_v', 'new_v_w_out': 'new_v', 'new_v_ffn_norm_g': 'new_v', 'new_v_w_up': 'new_v', 'new_v_conv_w': 'new_v', 'new_v_conv_b': 'new_v', 'new_v_w_down': 'new_v'}


def _forward(args):
    return _fwd_reference(*[args[k] for k in FWD_PARAMS])


def _output_shape():
    out = _jax.eval_shape(lambda: _forward(_fwd_setup_inputs(0)))
    return out.shape, out.dtype

N_MICROBATCH = 1
ADAM_LR = 0.001
ADAM_B1 = 0.9
ADAM_B2 = 0.999
ADAM_EPS = 1e-08
ADAM_WD = 0.01
ADAM_STEP = 10
PER_EXAMPLE_BATCH_AXIS = {'x': 0, 'loss_target': 0}
SHARED_INPUTS = []
_WEIGHT_DTYPES = {'attn_norm_g': _jnp.float32, 'w_in': _jnp.float32, 'q_norm_g': _jnp.float32, 'k_norm_g': _jnp.float32, 'sgu_norm_g': _jnp.float32, 'sgu_w': _jnp.float32, 'sgu_b': _jnp.float32, 'out_norm_a_g': _jnp.float32, 'out_norm_b_g': _jnp.float32, 'w_out': _jnp.float32, 'ffn_norm_g': _jnp.float32, 'w_up': _jnp.float32, 'conv_w': _jnp.float32, 'conv_b': _jnp.float32, 'w_down': _jnp.float32}
MOMENT_SCALE = {'attn_norm_g': 1.011258e-01, 'w_in': 6.418776e-02, 'q_norm_g': 8.518756e-02, 'k_norm_g': 8.527731e-02, 'sgu_norm_g': 4.617190e-02, 'sgu_w': 3.608463e-02, 'sgu_b': 3.387744e-02, 'out_norm_a_g': 1.994095e+00, 'out_norm_b_g': 2.017897e+00, 'w_out': 4.382300e-01, 'ffn_norm_g': 1.659039e+00, 'w_up': 4.107872e-02, 'conv_w': 2.321114e-01, 'conv_b': 2.219239e-01, 'w_down': 8.774648e-02}


def _to_microbatches(a, axis):
    t = _jnp.moveaxis(a, axis, 0)
    t = t.reshape((N_MICROBATCH, t.shape[0] // N_MICROBATCH) + t.shape[1:])
    return _jnp.moveaxis(t, 1, axis + 1)


def setup_inputs(seed: int = 0) -> dict:
    inp = _fwd_setup_inputs(seed)
    key = _jax.random.fold_in(_jax.random.key(seed), 7919)
    shape, _ = _output_shape()
    out = dict(inp)
    out["loss_target"] = _jax.random.normal(_jax.random.fold_in(key, 0), shape, _jnp.float32)
    for i, name in enumerate(TWIN_WEIGHTS):
        w = inp[name].astype(_jnp.float32)
        if MOMENT_SCALE is None:
            s = _jnp.sqrt(_jnp.mean(_jnp.square(w)) + 1e-30)
        else:
            s = MOMENT_SCALE[name]
        km, kv = _jax.random.split(_jax.random.fold_in(key, i + 1))
        out[name] = w
        out["m_" + name] = s * _jax.random.normal(km, w.shape, _jnp.float32)
        out["v_" + name] = (s * s) * _jax.random.uniform(kv, w.shape, _jnp.float32, 0.5, 1.5)
    if N_MICROBATCH > 1:
        for name, axis in PER_EXAMPLE_BATCH_AXIS.items():
            out[name] = _to_microbatches(out[name], axis)
    return {'x': out['x'], 'attn_norm_g': out['attn_norm_g'], 'w_in': out['w_in'], 'q_norm_g': out['q_norm_g'], 'k_norm_g': out['k_norm_g'], 'sgu_norm_g': out['sgu_norm_g'], 'sgu_w': out['sgu_w'], 'sgu_b': out['sgu_b'], 'out_norm_a_g': out['out_norm_a_g'], 'out_norm_b_g': out['out_norm_b_g'], 'w_out': out['w_out'], 'ffn_norm_g': out['ffn_norm_g'], 'w_up': out['w_up'], 'conv_w': out['conv_w'], 'conv_b': out['conv_b'], 'w_down': out['w_down'], 'loss_target': out['loss_target'], 'm_attn_norm_g': out['m_attn_norm_g'], 'm_w_in': out['m_w_in'], 'm_q_norm_g': out['m_q_norm_g'], 'm_k_norm_g': out['m_k_norm_g'], 'm_sgu_norm_g': out['m_sgu_norm_g'], 'm_sgu_w': out['m_sgu_w'], 'm_sgu_b': out['m_sgu_b'], 'm_out_norm_a_g': out['m_out_norm_a_g'], 'm_out_norm_b_g': out['m_out_norm_b_g'], 'm_w_out': out['m_w_out'], 'm_ffn_norm_g': out['m_ffn_norm_g'], 'm_w_up': out['m_w_up'], 'm_conv_w': out['m_conv_w'], 'm_conv_b': out['m_conv_b'], 'm_w_down': out['m_w_down'], 'v_attn_norm_g': out['v_attn_norm_g'], 'v_w_in': out['v_w_in'], 'v_q_norm_g': out['v_q_norm_g'], 'v_k_norm_g': out['v_k_norm_g'], 'v_sgu_norm_g': out['v_sgu_norm_g'], 'v_sgu_w': out['v_sgu_w'], 'v_sgu_b': out['v_sgu_b'], 'v_out_norm_a_g': out['v_out_norm_a_g'], 'v_out_norm_b_g': out['v_out_norm_b_g'], 'v_w_out': out['v_w_out'], 'v_ffn_norm_g': out['v_ffn_norm_g'], 'v_w_up': out['v_w_up'], 'v_conv_w': out['v_conv_w'], 'v_conv_b': out['v_conv_b'], 'v_w_down': out['v_w_down']}


def _loss(weights, diff, rest, loss_target):
    with _jax.named_scope("forward"):
        args = {**rest, TWIN_DIFF_INPUT: diff, **{k: w.astype(_WEIGHT_DTYPES[k]) for k, w in weights.items()}}
        y = _forward(args)
    with _jax.named_scope("loss_head"):
        err = _jnp.square(y.astype(_jnp.float32) - loss_target)
        return 0.5 * _jnp.sum(_jnp.mean(err, axis=-1)) if err.ndim else 0.5 * err


def _adamw(w, g, m, v):
    m = ADAM_B1 * m + (1.0 - ADAM_B1) * g
    v = ADAM_B2 * v + (1.0 - ADAM_B2) * _jnp.square(g)
    m_hat = m / (1.0 - ADAM_B1 ** ADAM_STEP)
    v_hat = v / (1.0 - ADAM_B2 ** ADAM_STEP)
    delta = -ADAM_LR * (m_hat / (_jnp.sqrt(v_hat) + ADAM_EPS) + ADAM_WD * w)
    return delta, m, v


def reference(x, attn_norm_g, w_in, q_norm_g, k_norm_g, sgu_norm_g, sgu_w, sgu_b, out_norm_a_g, out_norm_b_g, w_out, ffn_norm_g, w_up, conv_w, conv_b, w_down, loss_target, m_attn_norm_g, m_w_in, m_q_norm_g, m_k_norm_g, m_sgu_norm_g, m_sgu_w, m_sgu_b, m_out_norm_a_g, m_out_norm_b_g, m_w_out, m_ffn_norm_g, m_w_up, m_conv_w, m_conv_b, m_w_down, v_attn_norm_g, v_w_in, v_q_norm_g, v_k_norm_g, v_sgu_norm_g, v_sgu_w, v_sgu_b, v_out_norm_a_g, v_out_norm_b_g, v_w_out, v_ffn_norm_g, v_w_up, v_conv_w, v_conv_b, v_w_down):
    given = dict(x=x, attn_norm_g=attn_norm_g, w_in=w_in, q_norm_g=q_norm_g, k_norm_g=k_norm_g, sgu_norm_g=sgu_norm_g, sgu_w=sgu_w, sgu_b=sgu_b, out_norm_a_g=out_norm_a_g, out_norm_b_g=out_norm_b_g, w_out=w_out, ffn_norm_g=ffn_norm_g, w_up=w_up, conv_w=conv_w, conv_b=conv_b, w_down=w_down, loss_target=loss_target, m_attn_norm_g=m_attn_norm_g, m_w_in=m_w_in, m_q_norm_g=m_q_norm_g, m_k_norm_g=m_k_norm_g, m_sgu_norm_g=m_sgu_norm_g, m_sgu_w=m_sgu_w, m_sgu_b=m_sgu_b, m_out_norm_a_g=m_out_norm_a_g, m_out_norm_b_g=m_out_norm_b_g, m_w_out=m_w_out, m_ffn_norm_g=m_ffn_norm_g, m_w_up=m_w_up, m_conv_w=m_conv_w, m_conv_b=m_conv_b, m_w_down=m_w_down, v_attn_norm_g=v_attn_norm_g, v_w_in=v_w_in, v_q_norm_g=v_q_norm_g, v_k_norm_g=v_k_norm_g, v_sgu_norm_g=v_sgu_norm_g, v_sgu_w=v_sgu_w, v_sgu_b=v_sgu_b, v_out_norm_a_g=v_out_norm_a_g, v_out_norm_b_g=v_out_norm_b_g, v_w_out=v_w_out, v_ffn_norm_g=v_ffn_norm_g, v_w_up=v_w_up, v_conv_w=v_conv_w, v_conv_b=v_conv_b, v_w_down=v_w_down)
    weights = {n: given[n] for n in TWIN_WEIGHTS}
    shared = {n: given[n] for n in SHARED_INPUTS}
    per_example = {n: given[n] for n in ['x']}
    grad_fn = _jax.value_and_grad(_loss, argnums=(0, 1))

    def one_microbatch(ex, loss_target):
        ex = dict(ex)
        diff = ex.pop(TWIN_DIFF_INPUT)
        return grad_fn(weights, diff, {**shared, **ex}, loss_target)

    if N_MICROBATCH == 1:
        loss, (grad_w, grad_x) = one_microbatch(per_example, given["loss_target"])
    else:
        def body(carry, xs):
            loss_sum, grad_sum = carry
            l_k, (gw_k, gx_k) = one_microbatch(xs[0], xs[1])
            with _jax.named_scope("update"):
                return (loss_sum + l_k, _jax.tree.map(_jnp.add, grad_sum, gw_k)), gx_k

        init = (_jnp.zeros((), _jnp.float32), _jax.tree.map(_jnp.zeros_like, weights))
        (loss, grad_w), grad_x = _jax.lax.scan(body, init, (per_example, given["loss_target"]))
    with _jax.named_scope("update"):
        delta_w, new_m, new_v = {}, {}, {}
        for n in TWIN_WEIGHTS:
            delta_w[n], new_m[n], new_v[n] = _adamw(weights[n], grad_w[n], given["m_" + n], given["v_" + n])
    return (loss, grad_x, *[grad_w[n] for n in TWIN_WEIGHTS], *[delta_w[n] for n in TWIN_WEIGHTS],
            *[new_m[n] for n in TWIN_WEIGHTS], *[new_v[n] for n in TWIN_WEIGHTS])
```

```python
import functools

import jax
import jax.numpy as jnp
from jax import lax
from jax.experimental import pallas as pl
from jax.experimental.pallas import tpu as pltpu

F32 = jnp.float32
BF16 = jnp.bfloat16
EPS = 1e-6
HEAD_DIM = 128
TILE = 128
N_GROUPS = 8
CONV_WIDTH = 3
N_DEV = 8
MESH_AXES = ("x", "y", "c")
MIB = 1024 * 1024

ADAM_LR = 0.001
ADAM_B1 = 0.9
ADAM_B2 = 0.999
ADAM_EPS = 1e-08
ADAM_WD = 0.01
ADAM_STEP = 10
ADAMW_TILE_ELEMS = 160 * 1024

NT_DIMS = (((1,), (1,)), ((), ()))
NN_DIMS = (((1,), (0,)), ((), ()))
TN_DIMS = (((0,), (0,)), ((), ()))


def _cparams(sem, vmem_mb=48):
    return pltpu.CompilerParams(dimension_semantics=sem, vmem_limit_bytes=vmem_mb * MIB)


def _pick(n, cands):
    for c in cands:
        if n % c == 0:
            return c
    return n


def _mm(name, grid, ins, in_specs, out_shape, out_spec, dims, acc_tile=None, has_res=False, vmem_mb=48):
    nk = grid[2] if len(grid) == 3 else 1

    def body(*refs):
        if has_res:
            a_ref, b_ref, r_ref, o_ref = refs[:4]
            rest = refs[4:]
        else:
            a_ref, b_ref, o_ref = refs[:3]
            r_ref = None
            rest = refs[3:]
        prod = lax.dot_general(a_ref[...], b_ref[...], dims, preferred_element_type=F32)
        if nk == 1:
            if has_res:
                prod = prod + r_ref[...]
            o_ref[...] = prod.astype(o_ref.dtype)
        else:
            acc_ref = rest[0]
            k = pl.program_id(2)

            @pl.when(k == 0)
            def _():
                acc_ref[...] = prod

            @pl.when(k > 0)
            def _():
                acc_ref[...] += prod

            @pl.when(k == nk - 1)
            def _():
                o = acc_ref[...]
                if has_res:
                    o = o + r_ref[...]
                o_ref[...] = o.astype(o_ref.dtype)

    sem = ("parallel", "parallel") + (("arbitrary",) if len(grid) == 3 else ())
    scratch = [pltpu.VMEM(acc_tile, F32)] if nk > 1 else []
    return pl.pallas_call(
        body, name=name, grid=grid, in_specs=in_specs, out_specs=out_spec, out_shape=out_shape,
        scratch_shapes=scratch, compiler_params=_cparams(sem, vmem_mb),
    )(*ins)


def _mm_nn_blocked(name, a, wb, out_dtype, halves=False):
    m, k = a.shape
    nb, _, bn = wb.shape
    tm = _pick(m, (512, 256, 128))
    a_spec = pl.BlockSpec((tm, k), lambda j, i: (i, 0))
    b_spec = pl.BlockSpec((None, k, bn), lambda j, i: (j, 0, 0))
    if halves:
        hb = nb // 2
        out_shape = jax.ShapeDtypeStruct((2, m, hb * bn), out_dtype)
        o_spec = pl.BlockSpec((None, tm, bn), lambda j, i: (j // hb, i, j % hb))
    else:
        out_shape = jax.ShapeDtypeStruct((m, nb * bn), out_dtype)
        o_spec = pl.BlockSpec((tm, bn), lambda j, i: (i, j))
    return _mm(name, (nb, m // tm), (a, wb), [a_spec, b_spec], out_shape, o_spec, NN_DIMS)


def _mm_nn_res(name, a, w, res):
    m, k = a.shape
    n = w.shape[1]
    tm = _pick(m, (512, 256, 128))
    tn = _pick(n, (512, 256, 128))
    tk = k if k <= 2048 else _pick(k, (1408, 1024, 512, 256, 128))
    grid = (n // tn, m // tm, k // tk)
    a_spec = pl.BlockSpec((tm, tk), lambda j, i, kk: (i, kk))
    b_spec = pl.BlockSpec((tk, tn), lambda j, i, kk: (kk, j))
    r_spec = pl.BlockSpec((tm, tn), lambda j, i, kk: (i, j))
    o_spec = pl.BlockSpec((tm, tn), lambda j, i, kk: (i, j))
    return _mm(name, grid, (a, w, res), [a_spec, b_spec, r_spec], jax.ShapeDtypeStruct((m, n), F32), o_spec,
               NN_DIMS, acc_tile=(tm, tn), has_res=True)


def _mm_nt_blocked(name, dy, wb, halves=False):
    nb, n, bn = wb.shape
    m = dy.shape[-2]
    tm = _pick(m, (512, 256, 128))
    tn = _pick(n, (512, 256, 128))
    if halves:
        hb = nb // 2
        a_spec = pl.BlockSpec((None, tm, bn), lambda j, i, kk: (kk // hb, i, kk % hb))
    else:
        a_spec = pl.BlockSpec((tm, bn), lambda j, i, kk: (i, kk))
    b_spec = pl.BlockSpec((None, tn, bn), lambda j, i, kk: (kk, j, 0))
    o_spec = pl.BlockSpec((tm, tn), lambda j, i, kk: (i, j))
    return _mm(name, (n // tn, m // tm, nb), (dy, wb), [a_spec, b_spec], jax.ShapeDtypeStruct((m, n), F32), o_spec,
               NT_DIMS, acc_tile=(tm, tn))


def _mm_nt_plain(name, dy, w, out_dtype=F32):
    m, k = dy.shape
    n = w.shape[0]
    tm = _pick(m, (512, 256, 128))
    tn = _pick(n, (512, 256, 128))
    a_spec = pl.BlockSpec((tm, k), lambda j, i: (i, 0))
    b_spec = pl.BlockSpec((tn, k), lambda j, i: (j, 0))
    o_spec = pl.BlockSpec((tm, tn), lambda j, i: (i, j))
    return _mm(name, (n // tn, m // tm), (dy, w), [a_spec, b_spec], jax.ShapeDtypeStruct((m, n), out_dtype), o_spec,
               NT_DIMS)


def _mm_tn_blocked(name, a, dy, nb, halves=False):
    s, k1 = a.shape
    bn = (dy.shape[-1] * (2 if halves else 1)) // nb
    tm = _pick(k1, (512, 256, 128))
    a_spec = pl.BlockSpec((s, tm), lambda j, i: (0, i))
    if halves:
        hb = nb // 2
        b_spec = pl.BlockSpec((None, s, bn), lambda j, i: (j // hb, 0, j % hb))
    else:
        b_spec = pl.BlockSpec((s, bn), lambda j, i: (0, j))
    o_spec = pl.BlockSpec((None, tm, bn), lambda j, i: (j, i, 0))
    return _mm(name, (nb, k1 // tm), (a, dy), [a_spec, b_spec], jax.ShapeDtypeStruct((nb, k1, bn), BF16), o_spec,
               TN_DIMS)


def _mm_tn_plain(name, a, dy):
    s, k1 = a.shape
    n = dy.shape[1]
    tm = _pick(k1, (512, 256, 128))
    tn = _pick(n, (512, 256, 128))
    a_spec = pl.BlockSpec((s, tm), lambda i, j: (0, i))
    b_spec = pl.BlockSpec((s, tn), lambda i, j: (0, j))
    o_spec = pl.BlockSpec((tm, tn), lambda i, j: (i, j))
    return _mm(name, (k1 // tm, n // tn), (a, dy), [a_spec, b_spec], jax.ShapeDtypeStruct((k1, n), BF16), o_spec,
               TN_DIMS)


def _rstd(x):
    return lax.rsqrt(jnp.mean(x * x, axis=-1, keepdims=True) + EPS)


def _norm_bwd(dy, xhat, r, g):
    dxhat = dy * g
    return r * (dxhat - xhat * jnp.mean(dxhat * xhat, axis=-1, keepdims=True))


def _rmsnorm_fwd(name, x, g):
    s, d = x.shape
    tr = _pick(s, (256, 128))

    def body(x_ref, g_ref, h_ref):
        xv = x_ref[...]
        h_ref[...] = (xv * _rstd(xv) * g_ref[...]).astype(BF16)

    return pl.pallas_call(
        body, name=name, grid=(s // tr,),
        in_specs=[pl.BlockSpec((tr, d), lambda i: (i, 0)), pl.BlockSpec((1, d), lambda i: (0, 0))],
        out_specs=pl.BlockSpec((tr, d), lambda i: (i, 0)),
        out_shape=jax.ShapeDtypeStruct((s, d), BF16), compiler_params=_cparams(("parallel",)),
    )(x, g)


def _rmsnorm_bwd(name, dh, x, g, dres):
    s, d = x.shape
    tr = _pick(s, (256, 128))

    def body(dh_ref, x_ref, g_ref, dres_ref, dx_ref, dxb_ref, dg_ref):
        xv = x_ref[...]
        r = _rstd(xv)
        xhat = xv * r
        dhv = dh_ref[...]
        dx = dres_ref[...] + _norm_bwd(dhv, xhat, r, g_ref[...])
        dx_ref[...] = dx
        dxb_ref[...] = dx.astype(BF16)
        part = jnp.sum(dhv * xhat, axis=0, keepdims=True)

        @pl.when(pl.program_id(0) == 0)
        def _():
            dg_ref[...] = part

        @pl.when(pl.program_id(0) > 0)
        def _():
            dg_ref[...] += part

    row = pl.BlockSpec((tr, d), lambda i: (i, 0))
    vec = pl.BlockSpec((1, d), lambda i: (0, 0))
    return pl.pallas_call(
        body, name=name, grid=(s // tr,), in_specs=[row, row, vec, row], out_specs=(row, row, vec),
        out_shape=(jax.ShapeDtypeStruct((s, d), F32), jax.ShapeDtypeStruct((s, d), BF16),
                   jax.ShapeDtypeStruct((1, d), F32)),
        compiler_params=_cparams(("arbitrary",)),
    )(dh, x, g, dres)


def _loss_head(y, target):
    s, d = y.shape
    tr = _pick(s, (256, 128))

    def body(y_ref, t_ref, dy_ref, dyb_ref, loss_ref):
        err = y_ref[...] - t_ref[...]
        dy = err * (1.0 / d)
        dy_ref[...] = dy
        dyb_ref[...] = dy.astype(BF16)
        part = 0.5 * jnp.sum(jnp.mean(err * err, axis=-1, keepdims=True), axis=0, keepdims=True)
        part = jnp.broadcast_to(part, (1, 128))

        @pl.when(pl.program_id(0) == 0)
        def _():
            loss_ref[...] = part

        @pl.when(pl.program_id(0) > 0)
        def _():
            loss_ref[...] += part

    row = pl.BlockSpec((tr, d), lambda i: (i, 0))
    return pl.pallas_call(
        body, name="loss_head", grid=(s // tr,), in_specs=[row, row],
        out_specs=(row, row, pl.BlockSpec((1, 128), lambda i: (0, 0))),
        out_shape=(jax.ShapeDtypeStruct((s, d), F32), jax.ShapeDtypeStruct((s, d), BF16),
                   jax.ShapeDtypeStruct((1, 128), F32)),
        compiler_params=_cparams(("arbitrary",)),
    )(y, target)


def _split_dot(x, tri):
    hi = x.astype(BF16)
    lo = (x - hi.astype(F32)).astype(BF16)
    return (jnp.dot(hi, tri, preferred_element_type=F32) + jnp.dot(lo, tri, preferred_element_type=F32))


def _tile_iotas():
    row = lax.broadcasted_iota(jnp.int32, (TILE, TILE), 0)
    col = lax.broadcasted_iota(jnp.int32, (TILE, TILE), 1)
    return row, col


def _sb_logits(qi, kb, mask):
    z = lax.dot_general(qi, kb, NT_DIMS, preferred_element_type=F32) * (HEAD_DIM ** -0.5)
    sp = jnp.log1p(jnp.exp(-jnp.abs(z)))
    lb = jnp.minimum(z, 0.0) - sp
    l1m = jnp.where(mask, -jnp.maximum(z, 0.0) - sp, 0.0)
    return lb, l1m


def _attn_fwd(p, gq, gk, ga, n_heads):
    s = p.shape[0]
    nq = s // TILE

    def body(q_ref, k_ref, v_ref, gq_ref, gk_ref, ga_ref, att_ref, o_ref, r_ref, qn_s, kn_s, vb_s):
        qv = q_ref[...]
        qn_s[...] = (qv * _rstd(qv) * gq_ref[...]).astype(BF16)
        kv = k_ref[...]
        kn_s[...] = (kv * _rstd(kv) * gk_ref[...]).astype(BF16)
        vb_s[...] = v_ref[...].astype(BF16)
        row, col = _tile_iotas()
        upper = (row > col).astype(BF16)
        gav = ga_ref[...]

        def qblock(i, _):
            rows = pl.ds(pl.multiple_of(i * TILE, TILE), TILE)
            qi = qn_s[rows, :]

            def kblock(jj, carry):
                o_acc, c = carry
                b = i - jj
                keys = pl.ds(pl.multiple_of(b * TILE, TILE), TILE)
                mask = col < row + (i - b) * TILE
                lb, l1m = _sb_logits(qi, kn_s[keys, :], mask)
                tail = _split_dot(l1m, upper) + c
                a = jnp.where(mask, jnp.exp(lb + tail), 0.0)
                o_acc = o_acc + jnp.dot(a.astype(BF16), vb_s[keys, :], preferred_element_type=F32)
                c = c + jnp.sum(l1m, axis=1, keepdims=True)
                return o_acc, c

            o_acc, c = lax.fori_loop(0, i + 1, kblock,
                                     (jnp.zeros((TILE, HEAD_DIM), F32), jnp.zeros((TILE, 1), F32)))
            o_ref[rows, :] = o_acc
            r_ref[rows, :] = jnp.broadcast_to(c, (TILE, HEAD_DIM))
            att_ref[rows, :] = (o_acc * _rstd(o_acc) * gav).astype(BF16)
            return 0

        lax.fori_loop(0, nq, qblock, 0)

    col_blk = lambda off: pl.BlockSpec((s, HEAD_DIM), lambda h: (0, off + h))
    vec = pl.BlockSpec((1, HEAD_DIM), lambda h: (0, 0))
    hvec = pl.BlockSpec((None, 1, HEAD_DIM), lambda h: (h, 0, 0))
    out = pl.BlockSpec((s, HEAD_DIM), lambda h: (0, h))
    w = n_heads * HEAD_DIM
    return pl.pallas_call(
        body, name="attn_fwd", grid=(n_heads,),
        in_specs=[col_blk(0), col_blk(n_heads), col_blk(2 * n_heads), vec, vec, hvec],
        out_specs=(out, out, out),
        out_shape=(jax.ShapeDtypeStruct((s, w), BF16), jax.ShapeDtypeStruct((s, w), F32),
                   jax.ShapeDtypeStruct((s, w), F32)),
        scratch_shapes=[pltpu.VMEM((s, HEAD_DIM), BF16)] * 3,
        compiler_params=_cparams(("parallel",)),
    )(p, p, p, gq, gk, ga)


def _attn_bwd(p, gq, gk, ga, o, rsum, dmix, n_heads):
    s = p.shape[0]
    nq = s // TILE

    def body(q_ref, k_ref, v_ref, gq_ref, gk_ref, ga_ref, o_ref, r_ref, dm_ref,
             dq_ref, dk_ref, dv_ref, dgq_ref, dgk_ref, dga_ref,
             qn_s, kn_s, vb_s, do_s, dqn_s, dkn_s, dv_s):
        head = pl.program_id(0)
        gqv, gkv = gq_ref[...], gk_ref[...]
        qv = q_ref[...]
        qn_s[...] = (qv * _rstd(qv) * gqv).astype(BF16)
        kv = k_ref[...]
        kn_s[...] = (kv * _rstd(kv) * gkv).astype(BF16)
        vb_s[...] = v_ref[...].astype(BF16)
        ov = o_ref[...]
        ro = _rstd(ov)
        ohat = ov * ro
        dm = dm_ref[...]
        dga_ref[...] = jnp.sum(dm * ohat, axis=0, keepdims=True)
        do_s[...] = _norm_bwd(dm, ohat, ro, ga_ref[...]).astype(BF16)
        dkn_s[...] = jnp.zeros_like(dkn_s)
        dv_s[...] = jnp.zeros_like(dv_s)
        row, col = _tile_iotas()
        lower_incl = (row <= col).astype(BF16)
        lower_excl = (row < col).astype(BF16)

        def qblock(i, _):
            rows = pl.ds(pl.multiple_of(i * TILE, TILE), TILE)
            qi = qn_s[rows, :]
            doi = do_s[rows, :]
            r_i = r_ref[rows, :][:, 0:1]

            def kblock(b, carry):
                dq_acc, pfx, pc = carry
                keys = pl.ds(pl.multiple_of(b * TILE, TILE), TILE)
                kb = kn_s[keys, :]
                mask = col < row + (i - b) * TILE
                lb, l1m = _sb_logits(qi, kb, mask)
                tail = r_i - pfx - _split_dot(l1m, lower_incl)
                a = jnp.where(mask, jnp.exp(lb + tail), 0.0)
                da = lax.dot_general(doi, vb_s[keys, :], NT_DIMS, preferred_element_type=F32)
                ds = da * a
                dl1m = jnp.where(mask, pc + _split_dot(ds, lower_excl), 0.0)
                beta = jnp.exp(lb)
                dz = ((ds * (1.0 - beta) - dl1m * beta) * (HEAD_DIM ** -0.5)).astype(BF16)
                dq_acc = dq_acc + jnp.dot(dz, kb, preferred_element_type=F32)
                dkn_s[keys, :] += lax.dot_general(dz, qi, TN_DIMS, preferred_element_type=F32)
                dv_s[keys, :] += lax.dot_general(a.astype(BF16), doi, TN_DIMS, preferred_element_type=F32)
                pfx = pfx + jnp.sum(l1m, axis=1, keepdims=True)
                pc = pc + jnp.sum(ds, axis=1, keepdims=True)
                return dq_acc, pfx, pc

            zero_col = jnp.zeros((TILE, 1), F32)
            dq_acc, _, _ = lax.fori_loop(0, i + 1, kblock, (jnp.zeros((TILE, HEAD_DIM), F32), zero_col, zero_col))
            dqn_s[rows, :] = dq_acc
            return 0

        lax.fori_loop(0, nq, qblock, 0)

        def norm_in_bwd(x_ref, g, dn, dx_ref, dg_ref):
            xv = x_ref[...]
            r = _rstd(xv)
            xhat = xv * r
            dx_ref[...] = _norm_bwd(dn, xhat, r, g).astype(BF16)
            part = jnp.sum(dn * xhat, axis=0, keepdims=True)

            @pl.when(head == 0)
            def _():
                dg_ref[...] = part

            @pl.when(head > 0)
            def _():
                dg_ref[...] += part

        norm_in_bwd(q_ref, gqv, dqn_s[...], dq_ref, dgq_ref)
        norm_in_bwd(k_ref, gkv, dkn_s[...], dk_ref, dgk_ref)
        dv_ref[...] = dv_s[...].astype(BF16)

    col_blk = lambda off: pl.BlockSpec((s, HEAD_DIM), lambda h: (0, off + h))
    vec = pl.BlockSpec((1, HEAD_DIM), lambda h: (0, 0))
    hvec = pl.BlockSpec((None, 1, HEAD_DIM), lambda h: (h, 0, 0))
    blk = pl.BlockSpec((s, HEAD_DIM), lambda h: (0, h))
    w = n_heads * HEAD_DIM
    big = jax.ShapeDtypeStruct((s, w), BF16)
    return pl.pallas_call(
        body, name="attn_bwd", grid=(n_heads,),
        in_specs=[col_blk(0), col_blk(n_heads), col_blk(2 * n_heads), vec, vec, hvec, blk, blk, blk],
        out_specs=(blk, blk, blk, vec, vec, hvec),
        out_shape=(big, big, big, jax.ShapeDtypeStruct((1, HEAD_DIM), F32), jax.ShapeDtypeStruct((1, HEAD_DIM), F32),
                   jax.ShapeDtypeStruct((n_heads, 1, HEAD_DIM), F32)),
        scratch_shapes=[pltpu.VMEM((s, HEAD_DIM), BF16)] * 4 + [pltpu.VMEM((s, HEAD_DIM), F32)] * 3,
        compiler_params=_cparams(("arbitrary",)),
    )(p, p, p, gq, gk, ga, o, rsum, dmix)


_INV_SQRT2 = 0.7071067811865476
_INV_SQRT_2PI = 0.3989422804014327


def _gelu(x):
    return 0.5 * x * (1.0 + lax.erf(x * _INV_SQRT2))


def _gelu_grad(x):
    return 0.5 * (1.0 + lax.erf(x * _INV_SQRT2)) + x * (_INV_SQRT_2PI * jnp.exp(-0.5 * x * x))


def _sgu_fwd(p, gs, w_s, b_s, gb, col0):
    s = p.shape[0]
    n_chunks = s // TILE

    def body(u_ref, v_ref, gs_ref, w_ref, b_ref, gb_ref, out_ref, vs_s):
        vg = _gelu(v_ref[...])
        vs_s[...] = (vg * _rstd(vg) * gs_ref[...]).astype(BF16)
        row, col = _tile_iotas()
        wt = jnp.where(col <= row, w_ref[...], 0.0).astype(BF16)
        bcol = b_ref[...]
        gbv = gb_ref[...]

        def chunk(c, _):
            rows = pl.ds(pl.multiple_of(c * TILE, TILE), TILE)
            mixed = jnp.dot(wt, vs_s[rows, :], preferred_element_type=F32) + bcol
            sg = _gelu(u_ref[rows, :]) * mixed
            out_ref[rows, :] = (sg * _rstd(sg) * gbv).astype(BF16)
            return 0

        lax.fori_loop(0, n_chunks, chunk, 0)

    col_blk = lambda off: pl.BlockSpec((s, HEAD_DIM), lambda g: (0, off + g))
    gvec = pl.BlockSpec((None, 1, HEAD_DIM), lambda g: (g, 0, 0))
    return pl.pallas_call(
        body, name="sgu_fwd", grid=(N_GROUPS,),
        in_specs=[col_blk(col0), col_blk(col0 + N_GROUPS), gvec,
                  pl.BlockSpec((None, TILE, TILE), lambda g: (g, 0, 0)),
                  pl.BlockSpec((None, TILE, 1), lambda g: (g, 0, 0)), gvec],
        out_specs=pl.BlockSpec((s, HEAD_DIM), lambda g: (0, g)),
        out_shape=jax.ShapeDtypeStruct((s, N_GROUPS * HEAD_DIM), BF16),
        scratch_shapes=[pltpu.VMEM((s, HEAD_DIM), BF16)],
        compiler_params=_cparams(("parallel",)),
    )(p, p, gs, w_s, b_s, gb)


def _sgu_bwd(p, gs, w_s, b_s, gb, dmix, col0, dm_col0):
    s = p.shape[0]
    n_chunks = s // TILE

    def body(u_ref, v_ref, gs_ref, w_ref, b_ref, gb_ref, dm_ref,
             du_ref, dv_ref, dgs_ref, dw_ref, db_ref, dgb_ref, vs_s, dvs_s):
        gsv = gs_ref[...]
        gbv = gb_ref[...]
        vg = _gelu(v_ref[...])
        vs_s[...] = (vg * _rstd(vg) * gsv).astype(BF16)
        row, col = _tile_iotas()
        causal = col <= row
        wt = jnp.where(causal, w_ref[...], 0.0).astype(BF16)
        bcol = b_ref[...]

        def chunk(c, carry):
            dw_acc, db_acc, dgb_acc = carry
            rows = pl.ds(pl.multiple_of(c * TILE, TILE), TILE)
            vs = vs_s[rows, :]
            mixed = jnp.dot(wt, vs, preferred_element_type=F32) + bcol
            u_pre = u_ref[rows, :]
            u = _gelu(u_pre)
            sg = u * mixed
            rs = _rstd(sg)
            sghat = sg * rs
            dm = dm_ref[rows, :]
            dsg = _norm_bwd(dm, sghat, rs, gbv)
            dgb_acc = dgb_acc + jnp.sum(dm * sghat, axis=0, keepdims=True)
            du_ref[rows, :] = (dsg * mixed * _gelu_grad(u_pre)).astype(BF16)
            dmixed = dsg * u
            db_acc = db_acc + jnp.sum(dmixed, axis=1, keepdims=True)
            dmb = dmixed.astype(BF16)
            dw_acc = dw_acc + lax.dot_general(dmb, vs, NT_DIMS, preferred_element_type=F32)
            dvs_s[rows, :] = lax.dot_general(wt, dmb, TN_DIMS, preferred_element_type=F32)
            return dw_acc, db_acc, dgb_acc

        dw_acc, db_acc, dgb_acc = lax.fori_loop(
            0, n_chunks, chunk,
            (jnp.zeros((TILE, TILE), F32), jnp.zeros((TILE, 1), F32), jnp.zeros((1, HEAD_DIM), F32)))
        dw_ref[...] = jnp.where(causal, dw_acc, 0.0)
        db_ref[...] = db_acc
        dgb_ref[...] = dgb_acc
        v_pre = v_ref[...]
        vg = _gelu(v_pre)
        rv = _rstd(vg)
        vhat = vg * rv
        dvs = dvs_s[...]
        dgs_ref[...] = jnp.sum(dvs * vhat, axis=0, keepdims=True)
        dv_ref[...] = (_norm_bwd(dvs, vhat, rv, gsv) * _gelu_grad(v_pre)).astype(BF16)

    col_blk = lambda off: pl.BlockSpec((s, HEAD_DIM), lambda g: (0, off + g))
    gvec = pl.BlockSpec((None, 1, HEAD_DIM), lambda g: (g, 0, 0))
    wspec = pl.BlockSpec((None, TILE, TILE), lambda g: (g, 0, 0))
    bspec = pl.BlockSpec((None, TILE, 1), lambda g: (g, 0, 0))
    blk = pl.BlockSpec((s, HEAD_DIM), lambda g: (0, g))
    big = jax.ShapeDtypeStruct((s, N_GROUPS * HEAD_DIM), BF16)
    gshape = jax.ShapeDtypeStruct((N_GROUPS, 1, HEAD_DIM), F32)
    return pl.pallas_call(
        body, name="sgu_bwd", grid=(N_GROUPS,),
        in_specs=[col_blk(col0), col_blk(col0 + N_GROUPS), gvec, wspec, bspec, gvec, col_blk(dm_col0)],
        out_specs=(blk, blk, gvec, wspec, bspec, gvec),
        out_shape=(big, big, gshape, jax.ShapeDtypeStruct((N_GROUPS, TILE, TILE), F32),
                   jax.ShapeDtypeStruct((N_GROUPS, TILE, 1), F32), gshape),
        scratch_shapes=[pltpu.VMEM((s, HEAD_DIM), BF16), pltpu.VMEM((s, HEAD_DIM), F32)],
        compiler_params=_cparams(("parallel",)),
    )(p, p, gs, w_s, b_s, gb, dmix)


def _shift_down(x, n):
    rows = lax.broadcasted_iota(jnp.int32, x.shape, 0)
    return jnp.where(rows >= n, pltpu.roll(x, n, 0), 0.0)


def _shift_up(x, n):
    s = x.shape[0]
    rows = lax.broadcasted_iota(jnp.int32, x.shape, 0)
    return jnp.where(rows < s - n, pltpu.roll(x, s - n, 0), 0.0)


def _conv(x, w, b):
    return b + w[0:1, :] * _shift_down(x, 2) + w[1:2, :] * _shift_down(x, 1) + w[2:3, :] * x


def _conv_specs(s, tn):
    xspec = pl.BlockSpec((2, s, tn), lambda j: (0, 0, j))
    wspec = pl.BlockSpec((2, CONV_WIDTH, tn), lambda j: (0, 0, j))
    bspec = pl.BlockSpec((2, 1, tn), lambda j: (0, 0, j))
    return xspec, wspec, bspec


def _conv_gate_fwd(up, cw, cb):
    _, s, f = up.shape
    tn = _pick(f, (256, 128))

    def body(x_ref, w_ref, b_ref, act_ref):
        gate = _conv(x_ref[0], w_ref[0], b_ref[0])
        val = _conv(x_ref[1], w_ref[1], b_ref[1])
        act_ref[...] = (gate * jax.nn.sigmoid(gate) * val).astype(BF16)

    xspec, wspec, bspec = _conv_specs(s, tn)
    return pl.pallas_call(
        body, name="conv_gate_fwd", grid=(f // tn,), in_specs=[xspec, wspec, bspec],
        out_specs=pl.BlockSpec((s, tn), lambda j: (0, j)), out_shape=jax.ShapeDtypeStruct((s, f), BF16),
        compiler_params=_cparams(("parallel",)),
    )(up, cw, cb)


def _conv_gate_bwd(up, cw, cb, dact):
    _, s, f = up.shape
    tn = _pick(f, (256, 128))

    def body(x_ref, w_ref, b_ref, da_ref, dx_ref, dw_ref, db_ref):
        xg, xv = x_ref[0], x_ref[1]
        wg, wv = w_ref[0], w_ref[1]
        gate = _conv(xg, wg, b_ref[0])
        val = _conv(xv, wv, b_ref[1])
        sig = jax.nn.sigmoid(gate)
        da = da_ref[...]
        dval = da * (gate * sig)
        dgate = da * val * (sig * (1.0 + gate * (1.0 - sig)))
        for half, (x, w, dy) in enumerate(((xg, wg, dgate), (xv, wv, dval))):
            dx_ref[half] = (w[2:3, :] * dy + w[1:2, :] * _shift_up(dy, 1) + w[0:1, :] * _shift_up(dy, 2)).astype(BF16)
            dw_ref[half, 0:1, :] = jnp.sum(dy * _shift_down(x, 2), axis=0, keepdims=True)
            dw_ref[half, 1:2, :] = jnp.sum(dy * _shift_down(x, 1), axis=0, keepdims=True)
            dw_ref[half, 2:3, :] = jnp.sum(dy * x, axis=0, keepdims=True)
            db_ref[half] = jnp.sum(dy, axis=0, keepdims=True)

    xspec, wspec, bspec = _conv_specs(s, tn)
    return pl.pallas_call(
        body, name="conv_gate_bwd", grid=(f // tn,),
        in_specs=[xspec, wspec, bspec, pl.BlockSpec((s, tn), lambda j: (0, j))],
        out_specs=(xspec, wspec, bspec),
        out_shape=(jax.ShapeDtypeStruct((2, s, f), BF16), jax.ShapeDtypeStruct((2, CONV_WIDTH, f), F32),
                   jax.ShapeDtypeStruct((2, 1, f), F32)),
        compiler_params=_cparams(("parallel",)),
    )(up, cw, cb, dact)


def _mesh_pos():
    return lax.axis_index("x"), lax.axis_index("y"), lax.axis_index("c")


def _remote(src, dst, send_sem, recv_sem, to):
    return pltpu.make_async_remote_copy(src_ref=src, dst_ref=dst, send_sem=send_sem, recv_sem=recv_sem,
                                        device_id=to, device_id_type=pl.DeviceIdType.MESH)


def _all_gather(shards):
    n = len(shards)

    def body(*refs):
        ins, outs = refs[:n], refs[n:2 * n]
        send_sems, recv_sems, local_sems = refs[2 * n:]
        x, y, c = _mesh_pos()
        me, sibling = (x, y, c), (x, y, 1 - c)
        chips = [(1 - x, y), (x, 1 - y), (1 - x, 1 - y)]
        slot = lambda pos: 4 * pos[0] + 2 * pos[1] + pos[2]

        def copy(a, k, block, to, src=None):
            dst = outs[a].at[slot(block)]
            return _remote(dst if src is None else src, dst, send_sems.at[a, k], recv_sems.at[a, k], to)

        started = []
        for a in range(n):
            mine = pltpu.make_async_copy(ins[a], outs[a].at[slot(me)], local_sems.at[a])
            mine.start()
            started.append(mine)
        sends = []
        for a in range(n):
            first = [copy(a, 0, me, sibling, src=ins[a])]
            first += [copy(a, 1 + j, me, (*chip, c), src=ins[a]) for j, chip in enumerate(chips)]
            for cp in first:
                cp.start()
            sends += first
        for a in range(n):
            for j, chip in enumerate(chips):
                copy(a, 1 + j, (*chip, c), me).wait_recv()
                fwd = copy(a, 4 + j, (*chip, c), sibling)
                fwd.start()
                sends.append(fwd)
        for a in range(n):
            copy(a, 0, sibling, me).wait_recv()
            for j, chip in enumerate(chips):
                copy(a, 4 + j, (*chip, 1 - c), me).wait_recv()
        for cp in sends:
            cp.wait_send()
        for mine in started:
            mine.wait()

    anyspec = pl.BlockSpec(memory_space=pl.ANY)
    return pl.pallas_call(
        body, name="weight_all_gather",
        in_specs=[anyspec] * n, out_specs=tuple([anyspec] * n),
        out_shape=tuple(jax.ShapeDtypeStruct((N_DEV,) + w.shape, w.dtype) for w in shards),
        scratch_shapes=[pltpu.SemaphoreType.DMA((n, 7)), pltpu.SemaphoreType.DMA((n, 7)),
                        pltpu.SemaphoreType.DMA((n,))],
    )(*shards)


def _exchange_blocks(blocked):
    n = len(blocked)

    def body(*refs):
        ins, outs = refs[:n], refs[n:2 * n]
        send_sems, recv_sems, local_sems = refs[2 * n:]
        x, y, c = _mesh_pos()
        my_slot = 4 * x + 2 * y + c
        peers = [(x ^ ((k >> 2) & 1), y ^ ((k >> 1) & 1), c ^ (k & 1)) for k in range(1, N_DEV)]
        copies = []
        for a in range(n):
            mine = pltpu.make_async_copy(ins[a].at[my_slot], outs[a].at[my_slot], local_sems.at[a])
            mine.start()
            copies.append(mine)
        sends = []
        for a in range(n):
            for k, peer in enumerate(peers):
                peer_slot = 4 * peer[0] + 2 * peer[1] + peer[2]
                cp = _remote(ins[a].at[peer_slot], outs[a].at[my_slot], send_sems.at[a, k], recv_sems.at[a, k], peer)
                cp.start()
                sends.append((cp, a, k, peer_slot))
        for cp, a, k, peer_slot in sends:
            _remote(ins[a].at[peer_slot], outs[a].at[peer_slot], send_sems.at[a, k], recv_sems.at[a, k],
                    peers[k]).wait_recv()
        for cp, _, _, _ in sends:
            cp.wait_send()
        for mine in copies:
            mine.wait()

    anyspec = pl.BlockSpec(memory_space=pl.ANY)
    return pl.pallas_call(
        body, name="grad_exchange",
        in_specs=[anyspec] * n, out_specs=tuple([anyspec] * n),
        out_shape=tuple(jax.ShapeDtypeStruct(w.shape, w.dtype) for w in blocked),
        scratch_shapes=[pltpu.SemaphoreType.DMA((n, 7)), pltpu.SemaphoreType.DMA((n, 7)),
                        pltpu.SemaphoreType.DMA((n,))],
    )(*blocked)


def _adamw_math(w, g, m, v):
    m = ADAM_B1 * m + (1.0 - ADAM_B1) * g
    v = ADAM_B2 * v + (1.0 - ADAM_B2) * (g * g)
    m_hat = m / (1.0 - ADAM_B1 ** ADAM_STEP)
    v_hat = v / (1.0 - ADAM_B2 ** ADAM_STEP)
    delta = -ADAM_LR * (m_hat / (jnp.sqrt(v_hat) + ADAM_EPS) + ADAM_WD * w)
    return delta, m, v


def _adamw(name, w, m, v, parts):
    nl, r, c = w.shape
    tr = _pick(r, tuple(t for t in (256, 128, 64, 32, 16) if t * c <= ADAMW_TILE_ELEMS))

    def body(*refs):
        w_ref, m_ref, v_ref = refs[:3]
        part_refs = refs[3:3 + nl]
        g_ref, d_ref, nm_ref, nv_ref = refs[3 + nl:]
        layer = pl.program_id(0)
        for l in range(nl):
            @pl.when(layer == l)
            def _(l=l):
                g = part_refs[l][0].astype(F32)
                for src in range(1, N_DEV):
                    g = g + part_refs[l][src].astype(F32)
                delta, nm, nv = _adamw_math(w_ref[...], g, m_ref[...], v_ref[...])
                g_ref[...] = g
                d_ref[...] = delta
                nm_ref[...] = nm
                nv_ref[...] = nv

    wspec = pl.BlockSpec((None, tr, c), lambda l, i: (l, i, 0))
    pspecs = [pl.BlockSpec((N_DEV, tr, c), lambda l, i, l0=l0: (0, jnp.where(l == l0, i, 0), 0)) for l0 in range(nl)]
    shp = jax.ShapeDtypeStruct(w.shape, F32)
    return pl.pallas_call(
        body, name=name, grid=(nl, r // tr), in_specs=[wspec] * 3 + pspecs, out_specs=(wspec,) * 4,
        out_shape=(shp,) * 4, compiler_params=_cparams(("arbitrary", "arbitrary")),
    )(w, m, v, *parts)


PACK_TILE = 8 * 128


def _pack(arrays):
    flat = []
    for a in arrays:
        v = a.reshape(-1)
        pad = (-v.shape[0]) % PACK_TILE
        flat.append(jnp.pad(v, (0, pad)) if pad else v)
    return jnp.concatenate(flat).reshape(-1, 128)


def _unpack(buf, like):
    flat = buf.reshape(-1)
    out, off = [], 0
    for a in like:
        n = 1
        for dim in a.shape:
            n *= dim
        out.append(flat[off:off + n].reshape(a.shape))
        off += n + (-n) % PACK_TILE
    return out


def _sum_slots(gathered):
    _, r, c = gathered.shape

    def body(x_ref, o_ref):
        acc = x_ref[0]
        for src in range(1, N_DEV):
            acc = acc + x_ref[src]
        o_ref[...] = acc

    return pl.pallas_call(body, name="small_grad_sum", out_shape=jax.ShapeDtypeStruct((r, c), F32))(gathered)


def _adamw_small(w, g, m, v):
    shp = jax.ShapeDtypeStruct(w.shape, F32)

    def body(w_ref, g_ref, m_ref, v_ref, d_ref, nm_ref, nv_ref):
        delta, nm, nv = _adamw_math(w_ref[...], g_ref[...], m_ref[...], v_ref[...])
        d_ref[...] = delta
        nm_ref[...] = nm
        nv_ref[...] = nv

    return pl.pallas_call(body, name="adamw_small", out_shape=(shp,) * 3)(w, g, m, v)


def kernel(x, attn_norm_g, w_in, q_norm_g, k_norm_g, sgu_norm_g, sgu_w, sgu_b, out_norm_a_g, out_norm_b_g, w_out, ffn_norm_g, w_up, conv_w, conv_b, w_down, loss_target, m_attn_norm_g, m_w_in, m_q_norm_g, m_k_norm_g, m_sgu_norm_g, m_sgu_w, m_sgu_b, m_out_norm_a_g, m_out_norm_b_g, m_w_out, m_ffn_norm_g, m_w_up, m_conv_w, m_conv_b, m_w_down, v_attn_norm_g, v_w_in, v_q_norm_g, v_k_norm_g, v_sgu_norm_g, v_sgu_w, v_sgu_b, v_out_norm_a_g, v_out_norm_b_g, v_w_out, v_ffn_norm_g, v_w_up, v_conv_w, v_conv_b, v_w_down):
    depth = w_in.shape[0]
    s, d = x.shape[1], x.shape[2]
    n_heads = (d // 2) // HEAD_DIM
    sgu_col0 = 3 * n_heads
    f2 = w_up.shape[2] * N_DEV
    ff = f2 // 2
    my_slot = 4 * lax.axis_index("x") + 2 * lax.axis_index("y") + lax.axis_index("c")

    shards = []
    for l in range(depth):
        shards += [w_in[l].astype(BF16), w_out[l].astype(BF16), w_up[l].astype(BF16), w_down[l].astype(BF16)]
    shards.append(conv_w.reshape(depth * CONV_WIDTH, -1))
    gathered = _all_gather(shards)
    conv_w_all = gathered[-1].reshape(N_DEV, depth, CONV_WIDTH, -1)
    conv_w_all = jnp.transpose(conv_w_all, (1, 2, 0, 3)).reshape(depth, CONV_WIDTH, 2, ff)
    conv_w_all = jnp.transpose(conv_w_all, (0, 2, 1, 3))
    conv_b_all = conv_b.reshape(depth, 2, 1, ff)
    sgu_b_col = sgu_b[..., None]

    xs = x[0]
    saved = []
    for l in range(depth):
        win_g, wout_g, wup_g, wdown_g = gathered[4 * l:4 * l + 4]
        wout_full = wout_g.reshape(d, d)
        wdown_full = wdown_g.reshape(ff, d)
        g1 = attn_norm_g[l][None]
        g2 = ffn_norm_g[l][None]
        gq, gk = q_norm_g[l][None], k_norm_g[l][None]
        ga = out_norm_a_g[l][:, None, :]
        gs = sgu_norm_g[l][:, None, :]
        gb = out_norm_b_g[l][:, None, :]
        h1 = _rmsnorm_fwd("attn_norm_fwd", xs, g1)
        p = _mm_nn_blocked("in_proj", h1, win_g, F32)
        att, o, rsum = _attn_fwd(p, gq, gk, ga, n_heads)
        sg = _sgu_fwd(p, gs, sgu_w[l], sgu_b_col[l], gb, sgu_col0)
        mix = jnp.concatenate([att, sg], axis=-1)
        x1 = _mm_nn_res("out_proj", mix, wout_full, xs)
        h2 = _rmsnorm_fwd("ffn_norm_fwd", x1, g2)
        up = _mm_nn_blocked("up_proj", h2, wup_g, F32, halves=True)
        act = _conv_gate_fwd(up, conv_w_all[l], conv_b_all[l])
        x2 = _mm_nn_res("down_proj", act, wdown_full, x1)
        saved.append((xs, h1, p, o, rsum, mix, x1, h2, up, act))
        xs = x2

    dx, dxb, loss_vec = _loss_head(xs, loss_target[0])
    loss = lax.psum(loss_vec[0, 0], MESH_AXES)

    big_grads = [None] * depth
    small = [None] * depth
    for l in reversed(range(depth)):
        xs0, h1, p, o, rsum, mix, x1, h2, up, act = saved[l]
        win_g, wout_g, wup_g, wdown_g = gathered[4 * l:4 * l + 4]
        wout_full = wout_g.reshape(d, d)
        wdown_full = wdown_g.reshape(ff, d)
        g1 = attn_norm_g[l][None]
        g2 = ffn_norm_g[l][None]
        gq, gk = q_norm_g[l][None], k_norm_g[l][None]
        ga = out_norm_a_g[l][:, None, :]
        gs = sgu_norm_g[l][:, None, :]
        gb = out_norm_b_g[l][:, None, :]
        dact = _mm_nt_plain("down_proj_dx", dxb, wdown_full)
        d_wdown = _mm_tn_plain("down_proj_dw", act, dxb)
        dup, d_cw, d_cb = _conv_gate_bwd(up, conv_w_all[l], conv_b_all[l], dact)
        dh2 = _mm_nt_blocked("up_proj_dx", dup, wup_g, halves=True)
        d_wup = _mm_tn_blocked("up_proj_dw", h2, dup, N_DEV, halves=True)
        dx, dxb, d_g2 = _rmsnorm_bwd("ffn_norm_bwd", dh2, x1, g2, dx)
        dmix = _mm_nt_plain("out_proj_dx", dxb, wout_full)
        d_wout = _mm_tn_plain("out_proj_dw", mix, dxb)
        dq, dk, dv, d_gq, d_gk, d_ga = _attn_bwd(p, gq, gk, ga, o, rsum, dmix, n_heads)
        du, dvs, d_gs, d_sw, d_sb, d_gb = _sgu_bwd(p, gs, sgu_w[l], sgu_b_col[l], gb, dmix, sgu_col0, n_heads)
        dp = jnp.concatenate([dq, dk, dv, du, dvs], axis=-1)
        dh1 = _mm_nt_blocked("in_proj_dx", dp, win_g)
        d_win = _mm_tn_blocked("in_proj_dw", h1, dp, N_DEV)
        dx, dxb, d_g1 = _rmsnorm_bwd("attn_norm_bwd", dh1, xs0, g1, dx)
        big_grads[l] = (d_win, d_wout.reshape(N_DEV, d // N_DEV, d), d_wup, d_wdown.reshape(N_DEV, ff // N_DEV, d))
        small[l] = dict(attn_norm_g=d_g1[0], q_norm_g=d_gq[0], k_norm_g=d_gk[0], sgu_norm_g=d_gs[:, 0], sgu_w=d_sw,
                        sgu_b=d_sb[..., 0], out_norm_a_g=d_ga[:, 0], out_norm_b_g=d_gb[:, 0], ffn_norm_g=d_g2[0],
                        conv_w=jnp.transpose(d_cw, (1, 0, 2)).reshape(CONV_WIDTH, f2), conv_b=d_cb.reshape(f2))
    grad_x = dx[None]

    small_names = ["attn_norm_g", "q_norm_g", "k_norm_g", "sgu_norm_g", "sgu_w", "sgu_b", "out_norm_a_g",
                   "out_norm_b_g", "ffn_norm_g", "conv_b", "conv_w"]
    small_g = [jnp.stack([small[l][n] for l in range(depth)]) for n in small_names]
    packed = _pack(small_g)
    flat_big = [g for l in range(depth) for g in big_grads[l]]
    received = _exchange_blocks(flat_big)
    small_all = _all_gather([packed])[0]
    small_sum = _unpack(_sum_slots(small_all), small_g)
    g_small = dict(zip(small_names, small_sum))
    cwn = conv_w.shape[2]
    g_small["conv_w"] = lax.dynamic_slice_in_dim(g_small["conv_w"], my_slot * cwn, cwn, axis=2)

    res = {}
    big = dict(w_in=(w_in, m_w_in, v_w_in, 0), w_out=(w_out, m_w_out, v_w_out, 1), w_up=(w_up, m_w_up, v_w_up, 2),
               w_down=(w_down, m_w_down, v_w_down, 3))
    for name, (w, m, v, idx) in big.items():
        parts = [received[4 * l + idx] for l in range(depth)]
        res[name] = _adamw("adamw_" + name, w, m, v, parts)
    small_w = dict(attn_norm_g=(attn_norm_g, m_attn_norm_g, v_attn_norm_g), q_norm_g=(q_norm_g, m_q_norm_g, v_q_norm_g),
                   k_norm_g=(k_norm_g, m_k_norm_g, v_k_norm_g), sgu_norm_g=(sgu_norm_g, m_sgu_norm_g, v_sgu_norm_g),
                   sgu_w=(sgu_w, m_sgu_w, v_sgu_w), sgu_b=(sgu_b, m_sgu_b, v_sgu_b),
                   out_norm_a_g=(out_norm_a_g, m_out_norm_a_g, v_out_norm_a_g),
                   out_norm_b_g=(out_norm_b_g, m_out_norm_b_g, v_out_norm_b_g),
                   ffn_norm_g=(ffn_norm_g, m_ffn_norm_g, v_ffn_norm_g), conv_b=(conv_b, m_conv_b, v_conv_b),
                   conv_w=(conv_w, m_conv_w, v_conv_w))
    like = [small_w[n][0] for n in small_names]
    pw = _pack([small_w[n][0] for n in small_names])
    pm = _pack([small_w[n][1] for n in small_names])
    pv = _pack([small_w[n][2] for n in small_names])
    pg = _pack([g_small[n].reshape(small_w[n][0].shape) for n in small_names])
    pd, pnm, pnv = _adamw_small(pw, pg, pm, pv)
    for n, dlt, nm, nv in zip(small_names, _unpack(pd, like), _unpack(pnm, like), _unpack(pnv, like)):
        res[n] = (g_small[n].reshape(small_w[n][0].shape), dlt, nm, nv)

    order = ["attn_norm_g", "w_in", "q_norm_g", "k_norm_g", "sgu_norm_g", "sgu_w", "sgu_b", "out_norm_a_g",
             "out_norm_b_g", "w_out", "ffn_norm_g", "w_up", "conv_w", "conv_b", "w_down"]
    outs = [loss, grad_x]
    for field in range(4):
        outs += [res[n][field] for n in order]
    return tuple(outs)
```

```python
import functools

import jax
import jax.numpy as jnp
from jax import lax
from jax.experimental import pallas as pl
from jax.experimental.pallas import tpu as pltpu

F32 = jnp.float32
BF16 = jnp.bfloat16
EPS = 1e-6
HEAD_DIM = 128
TILE = 128
N_GROUPS = 8
CONV_WIDTH = 3
N_DEV = 8
MESH_AXES = ("x", "y", "c")
MIB = 1024 * 1024

ADAM_LR = 0.001
ADAM_B1 = 0.9
ADAM_B2 = 0.999
ADAM_EPS = 1e-08
ADAM_WD = 0.01
ADAM_STEP = 10
ADAMW_TILE_ELEMS = 160 * 1024

NT_DIMS = (((1,), (1,)), ((), ()))
NN_DIMS = (((1,), (0,)), ((), ()))
TN_DIMS = (((0,), (0,)), ((), ()))


def _cparams(sem, vmem_mb=48):
    return pltpu.CompilerParams(dimension_semantics=sem, vmem_limit_bytes=vmem_mb * MIB)


def _pick(n, cands):
    for c in cands:
        if n % c == 0:
            return c
    return n


def _mm(name, grid, ins, in_specs, out_shape, out_spec, dims, acc_tile=None, has_res=False, vmem_mb=48):
    nk = grid[2] if len(grid) == 3 else 1

    def body(*refs):
        if has_res:
            a_ref, b_ref, r_ref, o_ref = refs[:4]
            rest = refs[4:]
        else:
            a_ref, b_ref, o_ref = refs[:3]
            r_ref = None
            rest = refs[3:]
        prod = lax.dot_general(a_ref[...], b_ref[...], dims, preferred_element_type=F32)
        if nk == 1:
            if has_res:
                prod = prod + r_ref[...]
            o_ref[...] = prod.astype(o_ref.dtype)
        else:
            acc_ref = rest[0]
            k = pl.program_id(2)

            @pl.when(k == 0)
            def _():
                acc_ref[...] = prod

            @pl.when(k > 0)
            def _():
                acc_ref[...] += prod

            @pl.when(k == nk - 1)
            def _():
                o = acc_ref[...]
                if has_res:
                    o = o + r_ref[...]
                o_ref[...] = o.astype(o_ref.dtype)

    sem = ("parallel", "parallel") + (("arbitrary",) if len(grid) == 3 else ())
    scratch = [pltpu.VMEM(acc_tile, F32)] if nk > 1 else []
    return pl.pallas_call(
        body, name=name, grid=grid, in_specs=in_specs, out_specs=out_spec, out_shape=out_shape,
        scratch_shapes=scratch, compiler_params=_cparams(sem, vmem_mb),
    )(*ins)


def _mm_nn_blocked(name, a, wb, out_dtype, halves=False):
    m, k = a.shape
    nb, _, bn = wb.shape
    tm = _pick(m, (512, 256, 128))
    a_spec = pl.BlockSpec((tm, k), lambda j, i: (i, 0))
    b_spec = pl.BlockSpec((None, k, bn), lambda j, i: (j, 0, 0))
    if halves:
        hb = nb // 2
        out_shape = jax.ShapeDtypeStruct((2, m, hb * bn), out_dtype)
        o_spec = pl.BlockSpec((None, tm, bn), lambda j, i: (j // hb, i, j % hb))
    else:
        out_shape = jax.ShapeDtypeStruct((m, nb * bn), out_dtype)
        o_spec = pl.BlockSpec((tm, bn), lambda j, i: (i, j))
    return _mm(name, (nb, m // tm), (a, wb), [a_spec, b_spec], out_shape, o_spec, NN_DIMS)


def _mm_nn_res(name, a, w, res):
    m, k = a.shape
    n = w.shape[1]
    tm = _pick(m, (512, 256, 128))
    tn = _pick(n, (512, 256, 128))
    tk = k if k <= 2048 else _pick(k, (1408, 1024, 512, 256, 128))
    grid = (n // tn, m // tm, k // tk)
    a_spec = pl.BlockSpec((tm, tk), lambda j, i, kk: (i, kk))
    b_spec = pl.BlockSpec((tk, tn), lambda j, i, kk: (kk, j))
    r_spec = pl.BlockSpec((tm, tn), lambda j, i, kk: (i, j))
    o_spec = pl.BlockSpec((tm, tn), lambda j, i, kk: (i, j))
    return _mm(name, grid, (a, w, res), [a_spec, b_spec, r_spec], jax.ShapeDtypeStruct((m, n), F32), o_spec,
               NN_DIMS, acc_tile=(tm, tn), has_res=True)


def _mm_nt_blocked(name, dy, wb, halves=False):
    nb, n, bn = wb.shape
    m = dy.shape[-2]
    tm = _pick(m, (512, 256, 128))
    tn = _pick(n, (512, 256, 128))
    if halves:
        hb = nb // 2
        a_spec = pl.BlockSpec((None, tm, bn), lambda j, i, kk: (kk // hb, i, kk % hb))
    else:
        a_spec = pl.BlockSpec((tm, bn), lambda j, i, kk: (i, kk))
    b_spec = pl.BlockSpec((None, tn, bn), lambda j, i, kk: (kk, j, 0))
    o_spec = pl.BlockSpec((tm, tn), lambda j, i, kk: (i, j))
    return _mm(name, (n // tn, m // tm, nb), (dy, wb), [a_spec, b_spec], jax.ShapeDtypeStruct((m, n), F32), o_spec,
               NT_DIMS, acc_tile=(tm, tn))


def _mm_nt_plain(name, dy, w, out_dtype=F32):
    m, k = dy.shape
    n = w.shape[0]
    tm = _pick(m, (512, 256, 128))
    tn = _pick(n, (512, 256, 128))
    a_spec = pl.BlockSpec((tm, k), lambda j, i: (i, 0))
    b_spec = pl.BlockSpec((tn, k), lambda j, i: (j, 0))
    o_spec = pl.BlockSpec((tm, tn), lambda j, i: (i, j))
    return _mm(name, (n // tn, m // tm), (dy, w), [a_spec, b_spec], jax.ShapeDtypeStruct((m, n), out_dtype), o_spec,
               NT_DIMS)


def _mm_tn_blocked(name, a, dy, nb, halves=False):
    s, k1 = a.shape
    bn = (dy.shape[-1] * (2 if halves else 1)) // nb
    tm = _pick(k1, (512, 256, 128))
    a_spec = pl.BlockSpec((s, tm), lambda j, i: (0, i))
    if halves:
        hb = nb // 2
        b_spec = pl.BlockSpec((None, s, bn), lambda j, i: (j // hb, 0, j % hb))
    else:
        b_spec = pl.BlockSpec((s, bn), lambda j, i: (0, j))
    o_spec = pl.BlockSpec((None, tm, bn), lambda j, i: (j, i, 0))
    return _mm(name, (nb, k1 // tm), (a, dy), [a_spec, b_spec], jax.ShapeDtypeStruct((nb, k1, bn), BF16), o_spec,
               TN_DIMS)


def _mm_tn_plain(name, a, dy):
    s, k1 = a.shape
    n = dy.shape[1]
    tm = _pick(k1, (512, 256, 128))
    tn = _pick(n, (512, 256, 128))
    a_spec = pl.BlockSpec((s, tm), lambda i, j: (0, i))
    b_spec = pl.BlockSpec((s, tn), lambda i, j: (0, j))
    o_spec = pl.BlockSpec((tm, tn), lambda i, j: (i, j))
    return _mm(name, (k1 // tm, n // tn), (a, dy), [a_spec, b_spec], jax.ShapeDtypeStruct((k1, n), BF16), o_spec,
               TN_DIMS)


def _rstd(x):
    return lax.rsqrt(jnp.mean(x * x, axis=-1, keepdims=True) + EPS)


def _norm_bwd(dy, xhat, r, g):
    dxhat = dy * g
    return r * (dxhat - xhat * jnp.mean(dxhat * xhat, axis=-1, keepdims=True))


def _rmsnorm_fwd(name, x, g):
    s, d = x.shape
    tr = _pick(s, (256, 128))

    def body(x_ref, g_ref, h_ref):
        xv = x_ref[...]
        h_ref[...] = (xv * _rstd(xv) * g_ref[...]).astype(BF16)

    return pl.pallas_call(
        body, name=name, grid=(s // tr,),
        in_specs=[pl.BlockSpec((tr, d), lambda i: (i, 0)), pl.BlockSpec((1, d), lambda i: (0, 0))],
        out_specs=pl.BlockSpec((tr, d), lambda i: (i, 0)),
        out_shape=jax.ShapeDtypeStruct((s, d), BF16), compiler_params=_cparams(("parallel",)),
    )(x, g)


def _rmsnorm_bwd(name, dh, x, g, dres):
    s, d = x.shape
    tr = _pick(s, (256, 128))

    def body(dh_ref, x_ref, g_ref, dres_ref, dx_ref, dxb_ref, dg_ref):
        xv = x_ref[...]
        r = _rstd(xv)
        xhat = xv * r
        dhv = dh_ref[...]
        dx = dres_ref[...] + _norm_bwd(dhv, xhat, r, g_ref[...])
        dx_ref[...] = dx
        dxb_ref[...] = dx.astype(BF16)
        part = jnp.sum(dhv * xhat, axis=0, keepdims=True)

        @pl.when(pl.program_id(0) == 0)
        def _():
            dg_ref[...] = part

        @pl.when(pl.program_id(0) > 0)
        def _():
            dg_ref[...] += part

    row = pl.BlockSpec((tr, d), lambda i: (i, 0))
    vec = pl.BlockSpec((1, d), lambda i: (0, 0))
    return pl.pallas_call(
        body, name=name, grid=(s // tr,), in_specs=[row, row, vec, row], out_specs=(row, row, vec),
        out_shape=(jax.ShapeDtypeStruct((s, d), F32), jax.ShapeDtypeStruct((s, d), BF16),
                   jax.ShapeDtypeStruct((1, d), F32)),
        compiler_params=_cparams(("arbitrary",)),
    )(dh, x, g, dres)


def _loss_head(y, target):
    s, d = y.shape
    tr = _pick(s, (256, 128))

    def body(y_ref, t_ref, dy_ref, dyb_ref, loss_ref):
        err = y_ref[...] - t_ref[...]
        dy = err * (1.0 / d)
        dy_ref[...] = dy
        dyb_ref[...] = dy.astype(BF16)
        part = 0.5 * jnp.sum(jnp.mean(err * err, axis=-1, keepdims=True), axis=0, keepdims=True)
        part = jnp.broadcast_to(part, (1, 128))

        @pl.when(pl.program_id(0) == 0)
        def _():
            loss_ref[...] = part

        @pl.when(pl.program_id(0) > 0)
        def _():
            loss_ref[...] += part

    row = pl.BlockSpec((tr, d), lambda i: (i, 0))
    return pl.pallas_call(
        body, name="loss_head", grid=(s // tr,), in_specs=[row, row],
        out_specs=(row, row, pl.BlockSpec((1, 128), lambda i: (0, 0))),
        out_shape=(jax.ShapeDtypeStruct((s, d), F32), jax.ShapeDtypeStruct((s, d), BF16),
                   jax.ShapeDtypeStruct((1, 128), F32)),
        compiler_params=_cparams(("arbitrary",)),
    )(y, target)


def _split_dot(x, tri):
    hi = x.astype(BF16)
    lo = (x - hi.astype(F32)).astype(BF16)
    return (jnp.dot(hi, tri, preferred_element_type=F32) + jnp.dot(lo, tri, preferred_element_type=F32))


def _tile_iotas():
    row = lax.broadcasted_iota(jnp.int32, (TILE, TILE), 0)
    col = lax.broadcasted_iota(jnp.int32, (TILE, TILE), 1)
    return row, col


def _sb_logits(qi, kb, mask):
    z = lax.dot_general(qi, kb, NT_DIMS, preferred_element_type=F32) * (HEAD_DIM ** -0.5)
    sp = jnp.log1p(jnp.exp(-jnp.abs(z)))
    lb = jnp.minimum(z, 0.0) - sp
    l1m = jnp.where(mask, -jnp.maximum(z, 0.0) - sp, 0.0)
    return lb, l1m


def _attn_fwd(p, gq, gk, ga, n_heads):
    s = p.shape[0]
    nq = s // TILE

    def body(q_ref, k_ref, v_ref, gq_ref, gk_ref, ga_ref, att_ref, o_ref, r_ref, qn_s, kn_s, vb_s):
        qv = q_ref[...]
        qn_s[...] = (qv * _rstd(qv) * gq_ref[...]).astype(BF16)
        kv = k_ref[...]
        kn_s[...] = (kv * _rstd(kv) * gk_ref[...]).astype(BF16)
        vb_s[...] = v_ref[...].astype(BF16)
        row, col = _tile_iotas()
        upper = (row > col).astype(BF16)
        gav = ga_ref[...]

        def qblock(i, _):
            rows = pl.ds(pl.multiple_of(i * TILE, TILE), TILE)
            qi = qn_s[rows, :]

            def kblock(jj, carry):
                o_acc, c = carry
                b = i - jj
                keys = pl.ds(pl.multiple_of(b * TILE, TILE), TILE)
                mask = col < row + (i - b) * TILE
                lb, l1m = _sb_logits(qi, kn_s[keys, :], mask)
                tail = _split_dot(l1m, upper) + c
                a = jnp.where(mask, jnp.exp(lb + tail), 0.0)
                o_acc = o_acc + jnp.dot(a.astype(BF16), vb_s[keys, :], preferred_element_type=F32)
                c = c + jnp.sum(l1m, axis=1, keepdims=True)
                return o_acc, c

            o_acc, c = lax.fori_loop(0, i + 1, kblock,
                                     (jnp.zeros((TILE, HEAD_DIM), F32), jnp.zeros((TILE, 1), F32)))
            o_ref[rows, :] = o_acc
            r_ref[rows, :] = jnp.broadcast_to(c, (TILE, HEAD_DIM))
            att_ref[rows, :] = (o_acc * _rstd(o_acc) * gav).astype(BF16)
            return 0

        lax.fori_loop(0, nq, qblock, 0)

    col_blk = lambda off: pl.BlockSpec((s, HEAD_DIM), lambda h: (0, off + h))
    vec = pl.BlockSpec((1, HEAD_DIM), lambda h: (0, 0))
    hvec = pl.BlockSpec((None, 1, HEAD_DIM), lambda h: (h, 0, 0))
    out = pl.BlockSpec((s, HEAD_DIM), lambda h: (0, h))
    w = n_heads * HEAD_DIM
    return pl.pallas_call(
        body, name="attn_fwd", grid=(n_heads,),
        in_specs=[col_blk(0), col_blk(n_heads), col_blk(2 * n_heads), vec, vec, hvec],
        out_specs=(out, out, out),
        out_shape=(jax.ShapeDtypeStruct((s, w), BF16), jax.ShapeDtypeStruct((s, w), F32),
                   jax.ShapeDtypeStruct((s, w), F32)),
        scratch_shapes=[pltpu.VMEM((s, HEAD_DIM), BF16)] * 3,
        compiler_params=_cparams(("parallel",)),
    )(p, p, p, gq, gk, ga)


def _attn_bwd(p, gq, gk, ga, o, rsum, dmix, n_heads):
    s = p.shape[0]
    nq = s // TILE

    def body(q_ref, k_ref, v_ref, gq_ref, gk_ref, ga_ref, o_ref, r_ref, dm_ref,
             dq_ref, dk_ref, dv_ref, dgq_ref, dgk_ref, dga_ref,
             qn_s, kn_s, vb_s, do_s, dqn_s, dkn_s, dv_s):
        head = pl.program_id(0)
        gqv, gkv = gq_ref[...], gk_ref[...]
        qv = q_ref[...]
        qn_s[...] = (qv * _rstd(qv) * gqv).astype(BF16)
        kv = k_ref[...]
        kn_s[...] = (kv * _rstd(kv) * gkv).astype(BF16)
        vb_s[...] = v_ref[...].astype(BF16)
        ov = o_ref[...]
        ro = _rstd(ov)
        ohat = ov * ro
        dm = dm_ref[...]
        dga_ref[...] = jnp.sum(dm * ohat, axis=0, keepdims=True)
        do_s[...] = _norm_bwd(dm, ohat, ro, ga_ref[...]).astype(BF16)
        dkn_s[...] = jnp.zeros_like(dkn_s)
        dv_s[...] = jnp.zeros_like(dv_s)
        row, col = _tile_iotas()
        lower_incl = (row <= col).astype(BF16)
        lower_excl = (row < col).astype(BF16)

        def qblock(i, _):
            rows = pl.ds(pl.multiple_of(i * TILE, TILE), TILE)
            qi = qn_s[rows, :]
            doi = do_s[rows, :]
            r_i = r_ref[rows, :][:, 0:1]

            def kblock(b, carry):
                dq_acc, pfx, pc = carry
                keys = pl.ds(pl.multiple_of(b * TILE, TILE), TILE)
                kb = kn_s[keys, :]
                mask = col < row + (i - b) * TILE
                lb, l1m = _sb_logits(qi, kb, mask)
                tail = r_i - pfx - _split_dot(l1m, lower_incl)
                a = jnp.where(mask, jnp.exp(lb + tail), 0.0)
                da = lax.dot_general(doi, vb_s[keys, :], NT_DIMS, preferred_element_type=F32)
                ds = da * a
                dl1m = jnp.where(mask, pc + _split_dot(ds, lower_excl), 0.0)
                beta = jnp.exp(lb)
                dz = ((ds * (1.0 - beta) - dl1m * beta) * (HEAD_DIM ** -0.5)).astype(BF16)
                dq_acc = dq_acc + jnp.dot(dz, kb, preferred_element_type=F32)
                dkn_s[keys, :] += lax.dot_general(dz, qi, TN_DIMS, preferred_element_type=F32)
                dv_s[keys, :] += lax.dot_general(a.astype(BF16), doi, TN_DIMS, preferred_element_type=F32)
                pfx = pfx + jnp.sum(l1m, axis=1, keepdims=True)
                pc = pc + jnp.sum(ds, axis=1, keepdims=True)
                return dq_acc, pfx, pc

            zero_col = jnp.zeros((TILE, 1), F32)
            dq_acc, _, _ = lax.fori_loop(0, i + 1, kblock, (jnp.zeros((TILE, HEAD_DIM), F32), zero_col, zero_col))
            dqn_s[rows, :] = dq_acc
            return 0

        lax.fori_loop(0, nq, qblock, 0)

        def norm_in_bwd(x_ref, g, dn, dx_ref, dg_ref):
            xv = x_ref[...]
            r = _rstd(xv)
            xhat = xv * r
            dx_ref[...] = _norm_bwd(dn, xhat, r, g).astype(BF16)
            part = jnp.sum(dn * xhat, axis=0, keepdims=True)

            @pl.when(head == 0)
            def _():
                dg_ref[...] = part

            @pl.when(head > 0)
            def _():
                dg_ref[...] += part

        norm_in_bwd(q_ref, gqv, dqn_s[...], dq_ref, dgq_ref)
        norm_in_bwd(k_ref, gkv, dkn_s[...], dk_ref, dgk_ref)
        dv_ref[...] = dv_s[...].astype(BF16)

    col_blk = lambda off: pl.BlockSpec((s, HEAD_DIM), lambda h: (0, off + h))
    vec = pl.BlockSpec((1, HEAD_DIM), lambda h: (0, 0))
    hvec = pl.BlockSpec((None, 1, HEAD_DIM), lambda h: (h, 0, 0))
    blk = pl.BlockSpec((s, HEAD_DIM), lambda h: (0, h))
    w = n_heads * HEAD_DIM
    big = jax.ShapeDtypeStruct((s, w), BF16)
    return pl.pallas_call(
        body, name="attn_bwd", grid=(n_heads,),
        in_specs=[col_blk(0), col_blk(n_heads), col_blk(2 * n_heads), vec, vec, hvec, blk, blk, blk],
        out_specs=(blk, blk, blk, vec, vec, hvec),
        out_shape=(big, big, big, jax.ShapeDtypeStruct((1, HEAD_DIM), F32), jax.ShapeDtypeStruct((1, HEAD_DIM), F32),
                   jax.ShapeDtypeStruct((n_heads, 1, HEAD_DIM), F32)),
        scratch_shapes=[pltpu.VMEM((s, HEAD_DIM), BF16)] * 4 + [pltpu.VMEM((s, HEAD_DIM), F32)] * 3,
        compiler_params=_cparams(("arbitrary",)),
    )(p, p, p, gq, gk, ga, o, rsum, dmix)


_INV_SQRT2 = 0.7071067811865476
_INV_SQRT_2PI = 0.3989422804014327


def _gelu(x):
    return 0.5 * x * (1.0 + lax.erf(x * _INV_SQRT2))


def _gelu_grad(x):
    return 0.5 * (1.0 + lax.erf(x * _INV_SQRT2)) + x * (_INV_SQRT_2PI * jnp.exp(-0.5 * x * x))


def _sgu_fwd(p, gs, w_s, b_s, gb, col0):
    s = p.shape[0]
    n_chunks = s // TILE

    def body(u_ref, v_ref, gs_ref, w_ref, b_ref, gb_ref, out_ref, vs_s):
        vg = _gelu(v_ref[...])
        vs_s[...] = (vg * _rstd(vg) * gs_ref[...]).astype(BF16)
        row, col = _tile_iotas()
        wt = jnp.where(col <= row, w_ref[...], 0.0).astype(BF16)
        bcol = b_ref[...]
        gbv = gb_ref[...]

        def chunk(c, _):
            rows = pl.ds(pl.multiple_of(c * TILE, TILE), TILE)
            mixed = jnp.dot(wt, vs_s[rows, :], preferred_element_type=F32) + bcol
            sg = _gelu(u_ref[rows, :]) * mixed
            out_ref[rows, :] = (sg * _rstd(sg) * gbv).astype(BF16)
            return 0

        lax.fori_loop(0, n_chunks, chunk, 0)

    col_blk = lambda off: pl.BlockSpec((s, HEAD_DIM), lambda g: (0, off + g))
    gvec = pl.BlockSpec((None, 1, HEAD_DIM), lambda g: (g, 0, 0))
    return pl.pallas_call(
        body, name="sgu_fwd", grid=(N_GROUPS,),
        in_specs=[col_blk(col0), col_blk(col0 + N_GROUPS), gvec,
                  pl.BlockSpec((None, TILE, TILE), lambda g: (g, 0, 0)),
                  pl.BlockSpec((None, TILE, 1), lambda g: (g, 0, 0)), gvec],
        out_specs=pl.BlockSpec((s, HEAD_DIM), lambda g: (0, g)),
        out_shape=jax.ShapeDtypeStruct((s, N_GROUPS * HEAD_DIM), BF16),
        scratch_shapes=[pltpu.VMEM((s, HEAD_DIM), BF16)],
        compiler_params=_cparams(("parallel",)),
    )(p, p, gs, w_s, b_s, gb)


def _sgu_bwd(p, gs, w_s, b_s, gb, dmix, col0, dm_col0):
    s = p.shape[0]
    n_chunks = s // TILE

    def body(u_ref, v_ref, gs_ref, w_ref, b_ref, gb_ref, dm_ref,
             du_ref, dv_ref, dgs_ref, dw_ref, db_ref, dgb_ref, vs_s, dvs_s):
        gsv = gs_ref[...]
        gbv = gb_ref[...]
        vg = _gelu(v_ref[...])
        vs_s[...] = (vg * _rstd(vg) * gsv).astype(BF16)
        row, col = _tile_iotas()
        causal = col <= row
        wt = jnp.where(causal, w_ref[...], 0.0).astype(BF16)
        bcol = b_ref[...]

        def chunk(c, carry):
            dw_acc, db_acc, dgb_acc = carry
            rows = pl.ds(pl.multiple_of(c * TILE, TILE), TILE)
            vs = vs_s[rows, :]
            mixed = jnp.dot(wt, vs, preferred_element_type=F32) + bcol
            u_pre = u_ref[rows, :]
            u = _gelu(u_pre)
            sg = u * mixed
            rs = _rstd(sg)
            sghat = sg * rs
            dm = dm_ref[rows, :]
            dsg = _norm_bwd(dm, sghat, rs, gbv)
            dgb_acc = dgb_acc + jnp.sum(dm * sghat, axis=0, keepdims=True)
            du_ref[rows, :] = (dsg * mixed * _gelu_grad(u_pre)).astype(BF16)
            dmixed = dsg * u
            db_acc = db_acc + jnp.sum(dmixed, axis=1, keepdims=True)
            dmb = dmixed.astype(BF16)
            dw_acc = dw_acc + lax.dot_general(dmb, vs, NT_DIMS, preferred_element_type=F32)
            dvs_s[rows, :] = lax.dot_general(wt, dmb, TN_DIMS, preferred_element_type=F32)
            return dw_acc, db_acc, dgb_acc

        dw_acc, db_acc, dgb_acc = lax.fori_loop(
            0, n_chunks, chunk,
            (jnp.zeros((TILE, TILE), F32), jnp.zeros((TILE, 1), F32), jnp.zeros((1, HEAD_DIM), F32)))
        dw_ref[...] = jnp.where(causal, dw_acc, 0.0)
        db_ref[...] = db_acc
        dgb_ref[...] = dgb_acc
        v_pre = v_ref[...]
        vg = _gelu(v_pre)
        rv = _rstd(vg)
        vhat = vg * rv
        dvs = dvs_s[...]
        dgs_ref[...] = jnp.sum(dvs * vhat, axis=0, keepdims=True)
        dv_ref[...] = (_norm_bwd(dvs, vhat, rv, gsv) * _gelu_grad(v_pre)).astype(BF16)

    col_blk = lambda off: pl.BlockSpec((s, HEAD_DIM), lambda g: (0, off + g))
    gvec = pl.BlockSpec((None, 1, HEAD_DIM), lambda g: (g, 0, 0))
    wspec = pl.BlockSpec((None, TILE, TILE), lambda g: (g, 0, 0))
    bspec = pl.BlockSpec((None, TILE, 1), lambda g: (g, 0, 0))
    blk = pl.BlockSpec((s, HEAD_DIM), lambda g: (0, g))
    big = jax.ShapeDtypeStruct((s, N_GROUPS * HEAD_DIM), BF16)
    gshape = jax.ShapeDtypeStruct((N_GROUPS, 1, HEAD_DIM), F32)
    return pl.pallas_call(
        body, name="sgu_bwd", grid=(N_GROUPS,),
        in_specs=[col_blk(col0), col_blk(col0 + N_GROUPS), gvec, wspec, bspec, gvec, col_blk(dm_col0)],
        out_specs=(blk, blk, gvec, wspec, bspec, gvec),
        out_shape=(big, big, gshape, jax.ShapeDtypeStruct((N_GROUPS, TILE, TILE), F32),
                   jax.ShapeDtypeStruct((N_GROUPS, TILE, 1), F32), gshape),
        scratch_shapes=[pltpu.VMEM((s, HEAD_DIM), BF16), pltpu.VMEM((s, HEAD_DIM), F32)],
        compiler_params=_cparams(("parallel",)),
    )(p, p, gs, w_s, b_s, gb, dmix)


def _shift_down(x, n):
    rows = lax.broadcasted_iota(jnp.int32, x.shape, 0)
    return jnp.where(rows >= n, pltpu.roll(x, n, 0), 0.0)


def _shift_up(x, n):
    s = x.shape[0]
    rows = lax.broadcasted_iota(jnp.int32, x.shape, 0)
    return jnp.where(rows < s - n, pltpu.roll(x, s - n, 0), 0.0)


def _conv(x, w, b):
    return b + w[0:1, :] * _shift_down(x, 2) + w[1:2, :] * _shift_down(x, 1) + w[2:3, :] * x


def _conv_specs(s, tn):
    xspec = pl.BlockSpec((2, s, tn), lambda j: (0, 0, j))
    wspec = pl.BlockSpec((2, CONV_WIDTH, tn), lambda j: (0, 0, j))
    bspec = pl.BlockSpec((2, 1, tn), lambda j: (0, 0, j))
    return xspec, wspec, bspec


def _conv_gate_fwd(up, cw, cb):
    _, s, f = up.shape
    tn = _pick(f, (256, 128))

    def body(x_ref, w_ref, b_ref, act_ref):
        gate = _conv(x_ref[0], w_ref[0], b_ref[0])
        val = _conv(x_ref[1], w_ref[1], b_ref[1])
        act_ref[...] = (gate * jax.nn.sigmoid(gate) * val).astype(BF16)

    xspec, wspec, bspec = _conv_specs(s, tn)
    return pl.pallas_call(
        body, name="conv_gate_fwd", grid=(f // tn,), in_specs=[xspec, wspec, bspec],
        out_specs=pl.BlockSpec((s, tn), lambda j: (0, j)), out_shape=jax.ShapeDtypeStruct((s, f), BF16),
        compiler_params=_cparams(("parallel",)),
    )(up, cw, cb)


def _conv_gate_bwd(up, cw, cb, dact):
    _, s, f = up.shape
    tn = _pick(f, (256, 128))

    def body(x_ref, w_ref, b_ref, da_ref, dx_ref, dw_ref, db_ref):
        xg, xv = x_ref[0], x_ref[1]
        wg, wv = w_ref[0], w_ref[1]
        gate = _conv(xg, wg, b_ref[0])
        val = _conv(xv, wv, b_ref[1])
        sig = jax.nn.sigmoid(gate)
        da = da_ref[...]
        dval = da * (gate * sig)
        dgate = da * val * (sig * (1.0 + gate * (1.0 - sig)))
        for half, (x, w, dy) in enumerate(((xg, wg, dgate), (xv, wv, dval))):
            dx_ref[half] = (w[2:3, :] * dy + w[1:2, :] * _shift_up(dy, 1) + w[0:1, :] * _shift_up(dy, 2)).astype(BF16)
            dw_ref[half, 0:1, :] = jnp.sum(dy * _shift_down(x, 2), axis=0, keepdims=True)
            dw_ref[half, 1:2, :] = jnp.sum(dy * _shift_down(x, 1), axis=0, keepdims=True)
            dw_ref[half, 2:3, :] = jnp.sum(dy * x, axis=0, keepdims=True)
            db_ref[half] = jnp.sum(dy, axis=0, keepdims=True)

    xspec, wspec, bspec = _conv_specs(s, tn)
    return pl.pallas_call(
        body, name="conv_gate_bwd", grid=(f // tn,),
        in_specs=[xspec, wspec, bspec, pl.BlockSpec((s, tn), lambda j: (0, j))],
        out_specs=(xspec, wspec, bspec),
        out_shape=(jax.ShapeDtypeStruct((2, s, f), BF16), jax.ShapeDtypeStruct((2, CONV_WIDTH, f), F32),
                   jax.ShapeDtypeStruct((2, 1, f), F32)),
        compiler_params=_cparams(("parallel",)),
    )(up, cw, cb, dact)


def _mesh_pos():
    return lax.axis_index("x"), lax.axis_index("y"), lax.axis_index("c")


def _remote(src, dst, send_sem, recv_sem, to):
    return pltpu.make_async_remote_copy(src_ref=src, dst_ref=dst, send_sem=send_sem, recv_sem=recv_sem,
                                        device_id=to, device_id_type=pl.DeviceIdType.MESH)


def _all_gather(shards):
    n = len(shards)

    def body(*refs):
        ins, outs = refs[:n], refs[n:2 * n]
        send_sems, recv_sems, local_sems = refs[2 * n:]
        x, y, c = _mesh_pos()
        me, sibling = (x, y, c), (x, y, 1 - c)
        chips = [(1 - x, y), (x, 1 - y), (1 - x, 1 - y)]
        slot = lambda pos: 4 * pos[0] + 2 * pos[1] + pos[2]

        def copy(a, k, block, to, src=None):
            dst = outs[a].at[slot(block)]
            return _remote(dst if src is None else src, dst, send_sems.at[a, k], recv_sems.at[a, k], to)

        started = []
        for a in range(n):
            mine = pltpu.make_async_copy(ins[a], outs[a].at[slot(me)], local_sems.at[a])
            mine.start()
            started.append(mine)
        sends = []
        for a in range(n):
            first = [copy(a, 0, me, sibling, src=ins[a])]
            first += [copy(a, 1 + j, me, (*chip, c), src=ins[a]) for j, chip in enumerate(chips)]
            for cp in first:
                cp.start()
            sends += first
        for a in range(n):
            for j, chip in enumerate(chips):
                copy(a, 1 + j, (*chip, c), me).wait_recv()
                fwd = copy(a, 4 + j, (*chip, c), sibling)
                fwd.start()
                sends.append(fwd)
        for a in range(n):
            copy(a, 0, sibling, me).wait_recv()
            for j, chip in enumerate(chips):
                copy(a, 4 + j, (*chip, 1 - c), me).wait_recv()
        for cp in sends:
            cp.wait_send()
        for mine in started:
            mine.wait()

    anyspec = pl.BlockSpec(memory_space=pl.ANY)
    return pl.pallas_call(
        body, name="weight_all_gather",
        in_specs=[anyspec] * n, out_specs=tuple([anyspec] * n),
        out_shape=tuple(jax.ShapeDtypeStruct((N_DEV,) + w.shape, w.dtype) for w in shards),
        scratch_shapes=[pltpu.SemaphoreType.DMA((n, 7)), pltpu.SemaphoreType.DMA((n, 7)),
                        pltpu.SemaphoreType.DMA((n,))],
    )(*shards)


def _exchange_blocks(blocked):
    n = len(blocked)

    def body(*refs):
        ins, outs = refs[:n], refs[n:2 * n]
        send_sems, recv_sems, local_sems = refs[2 * n:]
        x, y, c = _mesh_pos()
        my_slot = 4 * x + 2 * y + c
        peers = [(x ^ ((k >> 2) & 1), y ^ ((k >> 1) & 1), c ^ (k & 1)) for k in range(1, N_DEV)]
        copies = []
        for a in range(n):
            mine = pltpu.make_async_copy(ins[a].at[my_slot], outs[a].at[my_slot], local_sems.at[a])
            mine.start()
            copies.append(mine)
        sends = []
        for a in range(n):
            for k, peer in enumerate(peers):
                peer_slot = 4 * peer[0] + 2 * peer[1] + peer[2]
                cp = _remote(ins[a].at[peer_slot], outs[a].at[my_slot], send_sems.at[a, k], recv_sems.at[a, k], peer)
                cp.start()
                sends.append((cp, a, k, peer_slot))
        for cp, a, k, peer_slot in sends:
            _remote(ins[a].at[peer_slot], outs[a].at[peer_slot], send_sems.at[a, k], recv_sems.at[a, k],
                    peers[k]).wait_recv()
        for cp, _, _, _ in sends:
            cp.wait_send()
        for mine in copies:
            mine.wait()

    anyspec = pl.BlockSpec(memory_space=pl.ANY)
    return pl.pallas_call(
        body, name="grad_exchange",
        in_specs=[anyspec] * n, out_specs=tuple([anyspec] * n),
        out_shape=tuple(jax.ShapeDtypeStruct(w.shape, w.dtype) for w in blocked),
        scratch_shapes=[pltpu.SemaphoreType.DMA((n, 7)), pltpu.SemaphoreType.DMA((n, 7)),
                        pltpu.SemaphoreType.DMA((n,))],
    )(*blocked)


HBM_SPEC = pl.BlockSpec(memory_space=pltpu.HBM)
SEM_SPEC = pl.BlockSpec(memory_space=pltpu.SEMAPHORE)
ANY_SPEC = pl.BlockSpec(memory_space=pl.ANY)
TOKEN_SPEC = pl.BlockSpec(memory_space=pltpu.VMEM)
TOKEN_SHAPE = jax.ShapeDtypeStruct((8, 128), F32)
DATAFLOW = pltpu.SideEffectType.DATAFLOW_SIDE_EFFECTING
GATHER_PLANE = (2, 4, 6)


def _slot(pos):
    return 4 * pos[0] + 2 * pos[1] + pos[2]


def _flip(pos, k):
    return (pos[0] ^ ((k >> 2) & 1), pos[1] ^ ((k >> 1) & 1), pos[2] ^ (k & 1))


def _hbm(a):
    return pltpu.with_memory_space_constraint(a, pltpu.HBM)


def _hbm_shapes(arrays):
    return tuple(pltpu.HBM(a.shape, a.dtype) for a in arrays)


class _Split:
    def __init__(self, n, outs, has_sems):
        k = 2 * n if has_sems else 0
        self.n = n
        self.sems = list(outs[:k])
        self.bufs = list(outs[k:k + 2 * n])
        self.token = outs[-1]


def _split_call(name, body, bufs, sems_in, makes_sems, after):
    n = len(bufs) // 2
    k = 2 * n if makes_sems else 0
    m = len(sems_in)

    def wrapped(*refs):
        srcs, dsts = refs[:n], refs[n:2 * n]
        s_in = refs[2 * n:2 * n + m]
        s_out = refs[2 * n + m + 1:2 * n + m + 1 + k]
        token, local_sems = refs[-2], refs[-1]
        body(srcs, dsts, s_in, s_out, local_sems)
        token[...] = jnp.zeros_like(token)

    outs = pl.pallas_call(
        wrapped, name=name,
        out_shape=(pltpu.SemaphoreType.DMA(()),) * k + _hbm_shapes(bufs) + (TOKEN_SHAPE,),
        in_specs=[HBM_SPEC] * (2 * n) + [SEM_SPEC] * m + [ANY_SPEC],
        out_specs=(SEM_SPEC,) * k + (HBM_SPEC,) * (2 * n) + (TOKEN_SPEC,),
        input_output_aliases={i: k + i for i in range(2 * n)},
        scratch_shapes=[pltpu.SemaphoreType.DMA((n,))],
        compiler_params=pltpu.CompilerParams(has_side_effects=DATAFLOW),
    )(*[_hbm(b) for b in bufs], *sems_in, after)
    return _Split(n, outs, makes_sems)


def _wait_slots(land, count, send_sem, recv_sem, me, send=False, recv=False):
    span = land.at[pl.ds(0, count)]
    cp = _remote(span, span, send_sem, recv_sem, me)
    if send:
        cp.wait_send()
    if recv:
        cp.wait_recv()


def _gather_start(name, shards, after):
    n = len(shards)
    lands = [lax.empty((N_DEV,) + w.shape, w.dtype) for w in shards]

    def body(srcs, dsts, _, sems, local_sems):
        me = _mesh_pos()
        for a in range(n):
            pltpu.make_async_copy(srcs[a], dsts[a].at[_slot(me)], local_sems.at[a]).start()
            for k in (1,) + GATHER_PLANE:
                _remote(srcs[a], dsts[a].at[_slot(me)], sems[a], sems[n + a], _flip(me, k)).start()
        for a in range(n):
            pltpu.make_async_copy(srcs[a], dsts[a].at[_slot(me)], local_sems.at[a]).wait()

    return _split_call(name, body, list(shards) + lands, [], True, after)


def _gather_forward(name, started, after):
    n = started.n

    def body(srcs, dsts, sems_a, sems_b, local_sems):
        me = _mesh_pos()
        sibling = _flip(me, 1)
        for a in range(n):
            _wait_slots(dsts[a], 4, sems_a[a], sems_a[n + a], me, recv=True)
            for k in GATHER_PLANE:
                block = dsts[a].at[_slot(_flip(me, k))]
                _remote(block, block, sems_b[a], sems_b[n + a], sibling).start()
        for a in range(n):
            _wait_slots(dsts[a], 4, sems_a[a], sems_a[n + a], me, send=True)

    return _split_call(name, body, started.bufs, started.sems, True, after)


def _gather_finish(name, forwarded, after):
    n = forwarded.n

    def body(srcs, dsts, sems_b, _, local_sems):
        me = _mesh_pos()
        for a in range(n):
            _wait_slots(dsts[a], 3, sems_b[a], sems_b[n + a], me, send=True, recv=True)

    return _split_call(name, body, forwarded.bufs, forwarded.sems, False, after).bufs[n:]


def _exchange_start(name, blocked, after):
    n = len(blocked)
    lands = [lax.empty(w.shape, w.dtype) for w in blocked]

    def body(srcs, dsts, _, sems, local_sems):
        me = _mesh_pos()
        for a in range(n):
            pltpu.make_async_copy(srcs[a].at[_slot(me)], dsts[a].at[_slot(me)], local_sems.at[a]).start()
            for k in range(1, N_DEV):
                peer = _flip(me, k)
                _remote(srcs[a].at[_slot(peer)], dsts[a].at[_slot(me)], sems[a], sems[n + a], peer).start()
        for a in range(n):
            pltpu.make_async_copy(srcs[a].at[_slot(me)], dsts[a].at[_slot(me)], local_sems.at[a]).wait()

    return _split_call(name, body, list(blocked) + lands, [], True, after)


def _exchange_finish(name, started, after):
    n = started.n

    def body(srcs, dsts, sems, _, local_sems):
        me = _mesh_pos()
        for a in range(n):
            _wait_slots(dsts[a], N_DEV - 1, sems[a], sems[n + a], me, send=True, recv=True)

    return _split_call(name, body, started.bufs, started.sems, False, after).bufs[n:]


def _adamw_math(w, g, m, v):
    m = ADAM_B1 * m + (1.0 - ADAM_B1) * g
    v = ADAM_B2 * v + (1.0 - ADAM_B2) * (g * g)
    m_hat = m / (1.0 - ADAM_B1 ** ADAM_STEP)
    v_hat = v / (1.0 - ADAM_B2 ** ADAM_STEP)
    delta = -ADAM_LR * (m_hat / (jnp.sqrt(v_hat) + ADAM_EPS) + ADAM_WD * w)
    return delta, m, v


def _adamw(name, w, m, v, parts):
    nl, r, c = w.shape
    tr = _pick(r, tuple(t for t in (256, 128, 64, 32, 16) if t * c <= ADAMW_TILE_ELEMS))

    def body(*refs):
        w_ref, m_ref, v_ref = refs[:3]
        part_refs = refs[3:3 + nl]
        g_ref, d_ref, nm_ref, nv_ref = refs[3 + nl:]
        layer = pl.program_id(0)
        for l in range(nl):
            @pl.when(layer == l)
            def _(l=l):
                g = part_refs[l][0].astype(F32)
                for src in range(1, N_DEV):
                    g = g + part_refs[l][src].astype(F32)
                delta, nm, nv = _adamw_math(w_ref[...], g, m_ref[...], v_ref[...])
                g_ref[...] = g
                d_ref[...] = delta
                nm_ref[...] = nm
                nv_ref[...] = nv

    wspec = pl.BlockSpec((None, tr, c), lambda l, i: (l, i, 0))
    pspecs = [pl.BlockSpec((N_DEV, tr, c), lambda l, i, l0=l0: (0, jnp.where(l == l0, i, 0), 0)) for l0 in range(nl)]
    shp = jax.ShapeDtypeStruct(w.shape, F32)
    return pl.pallas_call(
        body, name=name, grid=(nl, r // tr), in_specs=[wspec] * 3 + pspecs, out_specs=(wspec,) * 4,
        out_shape=(shp,) * 4, compiler_params=_cparams(("arbitrary", "arbitrary")),
    )(w, m, v, *parts)


PACK_TILE = 8 * 128


def _pack(arrays):
    flat = []
    for a in arrays:
        v = a.reshape(-1)
        pad = (-v.shape[0]) % PACK_TILE
        flat.append(jnp.pad(v, (0, pad)) if pad else v)
    return jnp.concatenate(flat).reshape(-1, 128)


def _unpack(buf, like):
    flat = buf.reshape(-1)
    out, off = [], 0
    for a in like:
        n = 1
        for dim in a.shape:
            n *= dim
        out.append(flat[off:off + n].reshape(a.shape))
        off += n + (-n) % PACK_TILE
    return out


def _sum_slots(gathered):
    _, r, c = gathered.shape

    def body(x_ref, o_ref):
        acc = x_ref[0]
        for src in range(1, N_DEV):
            acc = acc + x_ref[src]
        o_ref[...] = acc

    return pl.pallas_call(body, name="small_grad_sum", out_shape=jax.ShapeDtypeStruct((r, c), F32))(gathered)


def _adamw_small(w, g, m, v):
    shp = jax.ShapeDtypeStruct(w.shape, F32)

    def body(w_ref, g_ref, m_ref, v_ref, d_ref, nm_ref, nv_ref):
        delta, nm, nv = _adamw_math(w_ref[...], g_ref[...], m_ref[...], v_ref[...])
        d_ref[...] = delta
        nm_ref[...] = nm
        nv_ref[...] = nv

    return pl.pallas_call(body, name="adamw_small", out_shape=(shp,) * 3)(w, g, m, v)


def kernel(x, attn_norm_g, w_in, q_norm_g, k_norm_g, sgu_norm_g, sgu_w, sgu_b, out_norm_a_g, out_norm_b_g, w_out, ffn_norm_g, w_up, conv_w, conv_b, w_down, loss_target, m_attn_norm_g, m_w_in, m_q_norm_g, m_k_norm_g, m_sgu_norm_g, m_sgu_w, m_sgu_b, m_out_norm_a_g, m_out_norm_b_g, m_w_out, m_ffn_norm_g, m_w_up, m_conv_w, m_conv_b, m_w_down, v_attn_norm_g, v_w_in, v_q_norm_g, v_k_norm_g, v_sgu_norm_g, v_sgu_w, v_sgu_b, v_out_norm_a_g, v_out_norm_b_g, v_w_out, v_ffn_norm_g, v_w_up, v_conv_w, v_conv_b, v_w_down):
    depth = w_in.shape[0]
    s, d = x.shape[1], x.shape[2]
    n_heads = (d // 2) // HEAD_DIM
    sgu_col0 = 3 * n_heads
    f2 = w_up.shape[2] * N_DEV
    ff = f2 // 2
    my_slot = 4 * lax.axis_index("x") + 2 * lax.axis_index("y") + lax.axis_index("c")

    wb = [(w_in[l].astype(BF16), w_out[l].astype(BF16), w_up[l].astype(BF16), w_down[l].astype(BF16))
          for l in range(depth)]
    groups = [[wb[0][0]], [wb[0][1], wb[0][2], wb[0][3], conv_w.reshape(depth * CONV_WIDTH, -1)]]
    groups += [list(wb[l]) for l in range(1, depth)]
    token = attn_norm_g
    started = []
    for i, group in enumerate(groups):
        started.append(_gather_start(f"gather{i}_start", group, token))
        token = started[-1].token

    conv_b_all = conv_b.reshape(depth, 2, 1, ff)
    sgu_b_col = sgu_b[..., None]
    gathered = [None] * depth
    forwarded = _gather_forward("gather0_forward", started[0], token)
    win0 = _gather_finish("gather0_finish", forwarded, forwarded.token)[0]

    xs = x[0]
    saved = []
    for l in range(depth):
        g1 = attn_norm_g[l][None]
        g2 = ffn_norm_g[l][None]
        gq, gk = q_norm_g[l][None], k_norm_g[l][None]
        ga = out_norm_a_g[l][:, None, :]
        gs = sgu_norm_g[l][:, None, :]
        gb = out_norm_b_g[l][:, None, :]
        if l == 0:
            win_g = win0
        else:
            win_g = gathered[l][0]
        h1 = _rmsnorm_fwd("attn_norm_fwd", xs, g1)
        p = _mm_nn_blocked("in_proj", h1, win_g, F32)
        att, o, rsum = _attn_fwd(p, gq, gk, ga, n_heads)
        if l == 0:
            forwarded = _gather_forward("gather1_forward", started[1], att)
        sg = _sgu_fwd(p, gs, sgu_w[l], sgu_b_col[l], gb, sgu_col0)
        mix = jnp.concatenate([att, sg], axis=-1)
        if l == 0:
            rest = _gather_finish("gather1_finish", forwarded, mix)
            gathered[0] = (win0, rest[0], rest[1], rest[2])
            conv_w_all = rest[3].reshape(N_DEV, depth, CONV_WIDTH, -1)
            conv_w_all = jnp.transpose(conv_w_all, (1, 2, 0, 3)).reshape(depth, CONV_WIDTH, 2, ff)
            conv_w_all = jnp.transpose(conv_w_all, (0, 2, 1, 3))
        _, wout_g, wup_g, wdown_g = gathered[l]
        wout_full = wout_g.reshape(d, d)
        wdown_full = wdown_g.reshape(ff, d)
        x1 = _mm_nn_res("out_proj", mix, wout_full, xs)
        h2 = _rmsnorm_fwd("ffn_norm_fwd", x1, g2)
        up = _mm_nn_blocked("up_proj", h2, wup_g, F32, halves=True)
        act = _conv_gate_fwd(up, conv_w_all[l], conv_b_all[l])
        if l + 1 < depth:
            forwarded = _gather_forward(f"gather{l + 2}_forward", started[l + 2], act)
        x2 = _mm_nn_res("down_proj", act, wdown_full, x1)
        if l + 1 < depth:
            gathered[l + 1] = tuple(_gather_finish(f"gather{l + 2}_finish", forwarded, x2))
        saved.append((xs, h1, p, o, rsum, mix, x1, h2, up, act))
        xs = x2

    dx, dxb, loss_vec = _loss_head(xs, loss_target[0])
    loss = lax.psum(loss_vec[0, 0], MESH_AXES)

    exchanges = []
    small = [None] * depth
    for l in reversed(range(depth)):
        xs0, h1, p, o, rsum, mix, x1, h2, up, act = saved[l]
        win_g, wout_g, wup_g, wdown_g = gathered[l]
        wout_full = wout_g.reshape(d, d)
        wdown_full = wdown_g.reshape(ff, d)
        g1 = attn_norm_g[l][None]
        g2 = ffn_norm_g[l][None]
        gq, gk = q_norm_g[l][None], k_norm_g[l][None]
        ga = out_norm_a_g[l][:, None, :]
        gs = sgu_norm_g[l][:, None, :]
        gb = out_norm_b_g[l][:, None, :]
        d_wdown = _mm_tn_plain("down_proj_dw", act, dxb)
        dact = _mm_nt_plain("down_proj_dx", dxb, wdown_full)
        dup, d_cw, d_cb = _conv_gate_bwd(up, conv_w_all[l], conv_b_all[l], dact)
        d_wup = _mm_tn_blocked("up_proj_dw", h2, dup, N_DEV, halves=True)
        exchanges.append((l, (2, 3), _exchange_start(
            f"grad_ffn{l}_start", [d_wup, d_wdown.reshape(N_DEV, ff // N_DEV, d)], d_wup)))
        dh2 = _mm_nt_blocked("up_proj_dx", dup, wup_g, halves=True)
        dx, dxb, d_g2 = _rmsnorm_bwd("ffn_norm_bwd", dh2, x1, g2, dx)
        d_wout = _mm_tn_plain("out_proj_dw", mix, dxb)
        dmix = _mm_nt_plain("out_proj_dx", dxb, wout_full)
        dq, dk, dv, d_gq, d_gk, d_ga = _attn_bwd(p, gq, gk, ga, o, rsum, dmix, n_heads)
        du, dvs, d_gs, d_sw, d_sb, d_gb = _sgu_bwd(p, gs, sgu_w[l], sgu_b_col[l], gb, dmix, sgu_col0, n_heads)
        dp = jnp.concatenate([dq, dk, dv, du, dvs], axis=-1)
        d_win = _mm_tn_blocked("in_proj_dw", h1, dp, N_DEV)
        exchanges.append((l, (0, 1), _exchange_start(
            f"grad_mix{l}_start", [d_win, d_wout.reshape(N_DEV, d // N_DEV, d)], d_win)))
        dh1 = _mm_nt_blocked("in_proj_dx", dp, win_g)
        dx, dxb, d_g1 = _rmsnorm_bwd("attn_norm_bwd", dh1, xs0, g1, dx)
        small[l] = dict(attn_norm_g=d_g1[0], q_norm_g=d_gq[0], k_norm_g=d_gk[0], sgu_norm_g=d_gs[:, 0], sgu_w=d_sw,
                        sgu_b=d_sb[..., 0], out_norm_a_g=d_ga[:, 0], out_norm_b_g=d_gb[:, 0], ffn_norm_g=d_g2[0],
                        conv_w=jnp.transpose(d_cw, (1, 0, 2)).reshape(CONV_WIDTH, f2), conv_b=d_cb.reshape(f2))
    grad_x = dx[None]

    small_names = ["attn_norm_g", "q_norm_g", "k_norm_g", "sgu_norm_g", "sgu_w", "sgu_b", "out_norm_a_g",
                   "out_norm_b_g", "ffn_norm_g", "conv_b", "conv_w"]
    small_g = [jnp.stack([small[l][n] for l in range(depth)]) for n in small_names]
    packed = _pack(small_g)
    received = [None] * (4 * depth)
    after = dx
    for l, idxs, ex in exchanges:
        stage = "ffn" if idxs == (2, 3) else "mix"
        landed = _exchange_finish(f"grad_{stage}{l}_finish", ex, after)
        landed = list(landed)
        after = landed[0]
        for idx, arr in zip(idxs, landed):
            received[4 * l + idx] = arr
    small_all = _all_gather([packed])[0]
    small_sum = _unpack(_sum_slots(small_all), small_g)
    g_small = dict(zip(small_names, small_sum))
    cwn = conv_w.shape[2]
    g_small["conv_w"] = lax.dynamic_slice_in_dim(g_small["conv_w"], my_slot * cwn, cwn, axis=2)

    res = {}
    big = dict(w_in=(w_in, m_w_in, v_w_in, 0), w_out=(w_out, m_w_out, v_w_out, 1), w_up=(w_up, m_w_up, v_w_up, 2),
               w_down=(w_down, m_w_down, v_w_down, 3))
    for name, (w, m, v, idx) in big.items():
        parts = [received[4 * l + idx] for l in range(depth)]
        res[name] = _adamw("adamw_" + name, w, m, v, parts)
    small_w = dict(attn_norm_g=(attn_norm_g, m_attn_norm_g, v_attn_norm_g), q_norm_g=(q_norm_g, m_q_norm_g, v_q_norm_g),
                   k_norm_g=(k_norm_g, m_k_norm_g, v_k_norm_g), sgu_norm_g=(sgu_norm_g, m_sgu_norm_g, v_sgu_norm_g),
                   sgu_w=(sgu_w, m_sgu_w, v_sgu_w), sgu_b=(sgu_b, m_sgu_b, v_sgu_b),
                   out_norm_a_g=(out_norm_a_g, m_out_norm_a_g, v_out_norm_a_g),
                   out_norm_b_g=(out_norm_b_g, m_out_norm_b_g, v_out_norm_b_g),
                   ffn_norm_g=(ffn_norm_g, m_ffn_norm_g, v_ffn_norm_g), conv_b=(conv_b, m_conv_b, v_conv_b),
                   conv_w=(conv_w, m_conv_w, v_conv_w))
    like = [small_w[n][0] for n in small_names]
    pw = _pack([small_w[n][0] for n in small_names])
    pm = _pack([small_w[n][1] for n in small_names])
    pv = _pack([small_w[n][2] for n in small_names])
    pg = _pack([g_small[n].reshape(small_w[n][0].shape) for n in small_names])
    pd, pnm, pnv = _adamw_small(pw, pg, pm, pv)
    for n, dlt, nm, nv in zip(small_names, _unpack(pd, like), _unpack(pnm, like), _unpack(pnv, like)):
        res[n] = (g_small[n].reshape(small_w[n][0].shape), dlt, nm, nv)

    order = ["attn_norm_g", "w_in", "q_norm_g", "k_norm_g", "sgu_norm_g", "sgu_w", "sgu_b", "out_norm_a_g",
             "out_norm_b_g", "w_out", "ffn_norm_g", "w_up", "conv_w", "conv_b", "w_down"]
    outs = [loss, grad_x]
    for field in range(4):
        outs += [res[n][field] for n in order]
    return tuple(outs)
```

```python
import functools

import jax
import jax.numpy as jnp
from jax import lax
from jax.experimental import pallas as pl
from jax.experimental.pallas import tpu as pltpu

F32 = jnp.float32
BF16 = jnp.bfloat16
EPS = 1e-6
HEAD_DIM = 128
TILE = 128
N_GROUPS = 8
CONV_WIDTH = 3
N_DEV = 8
MESH_AXES = ("x", "y", "c")
MIB = 1024 * 1024

ADAM_LR = 0.001
ADAM_B1 = 0.9
ADAM_B2 = 0.999
ADAM_EPS = 1e-08
ADAM_WD = 0.01
ADAM_STEP = 10
ADAMW_TILE_ELEMS = 160 * 1024

NT_DIMS = (((1,), (1,)), ((), ()))
NN_DIMS = (((1,), (0,)), ((), ()))
TN_DIMS = (((0,), (0,)), ((), ()))


def _cparams(sem, vmem_mb=48):
    return pltpu.CompilerParams(dimension_semantics=sem, vmem_limit_bytes=vmem_mb * MIB)


def _pick(n, cands):
    for c in cands:
        if n % c == 0:
            return c
    return n


def _mm(name, grid, ins, in_specs, out_shape, out_spec, dims, acc_tile=None, has_res=False, vmem_mb=48):
    nk = grid[2] if len(grid) == 3 else 1

    def body(*refs):
        if has_res:
            a_ref, b_ref, r_ref, o_ref = refs[:4]
            rest = refs[4:]
        else:
            a_ref, b_ref, o_ref = refs[:3]
            r_ref = None
            rest = refs[3:]
        prod = lax.dot_general(a_ref[...], b_ref[...], dims, preferred_element_type=F32)
        if nk == 1:
            if has_res:
                prod = prod + r_ref[...]
            o_ref[...] = prod.astype(o_ref.dtype)
        else:
            acc_ref = rest[0]
            k = pl.program_id(2)

            @pl.when(k == 0)
            def _():
                acc_ref[...] = prod

            @pl.when(k > 0)
            def _():
                acc_ref[...] += prod

            @pl.when(k == nk - 1)
            def _():
                o = acc_ref[...]
                if has_res:
                    o = o + r_ref[...]
                o_ref[...] = o.astype(o_ref.dtype)

    sem = ("parallel", "parallel") + (("arbitrary",) if len(grid) == 3 else ())
    scratch = [pltpu.VMEM(acc_tile, F32)] if nk > 1 else []
    return pl.pallas_call(
        body, name=name, grid=grid, in_specs=in_specs, out_specs=out_spec, out_shape=out_shape,
        scratch_shapes=scratch, compiler_params=_cparams(sem, vmem_mb),
    )(*ins)


def _mm_nn_blocked(name, a, wb, out_dtype, halves=False):
    m, k = a.shape
    nb, _, bn = wb.shape
    tm = _pick(m, (512, 256, 128))
    a_spec = pl.BlockSpec((tm, k), lambda j, i: (i, 0))
    b_spec = pl.BlockSpec((None, k, bn), lambda j, i: (j, 0, 0))
    if halves:
        hb = nb // 2
        out_shape = jax.ShapeDtypeStruct((2, m, hb * bn), out_dtype)
        o_spec = pl.BlockSpec((None, tm, bn), lambda j, i: (j // hb, i, j % hb))
    else:
        out_shape = jax.ShapeDtypeStruct((m, nb * bn), out_dtype)
        o_spec = pl.BlockSpec((tm, bn), lambda j, i: (i, j))
    return _mm(name, (nb, m // tm), (a, wb), [a_spec, b_spec], out_shape, o_spec, NN_DIMS)


def _mm_nn_res(name, a, w, res):
    m, k = a.shape
    n = w.shape[1]
    tm = _pick(m, (512, 256, 128))
    tn = _pick(n, (512, 256, 128))
    tk = k if k <= 2048 else _pick(k, (1408, 1024, 512, 256, 128))
    grid = (n // tn, m // tm, k // tk)
    a_spec = pl.BlockSpec((tm, tk), lambda j, i, kk: (i, kk))
    b_spec = pl.BlockSpec((tk, tn), lambda j, i, kk: (kk, j))
    r_spec = pl.BlockSpec((tm, tn), lambda j, i, kk: (i, j))
    o_spec = pl.BlockSpec((tm, tn), lambda j, i, kk: (i, j))
    return _mm(name, grid, (a, w, res), [a_spec, b_spec, r_spec], jax.ShapeDtypeStruct((m, n), F32), o_spec,
               NN_DIMS, acc_tile=(tm, tn), has_res=True)


def _mm_nt_blocked(name, dy, wb, halves=False):
    nb, n, bn = wb.shape
    m = dy.shape[-2]
    tm = _pick(m, (512, 256, 128))
    tn = _pick(n, (512, 256, 128))
    if halves:
        hb = nb // 2
        a_spec = pl.BlockSpec((None, tm, bn), lambda j, i, kk: (kk // hb, i, kk % hb))
    else:
        a_spec = pl.BlockSpec((tm, bn), lambda j, i, kk: (i, kk))
    b_spec = pl.BlockSpec((None, tn, bn), lambda j, i, kk: (kk, j, 0))
    o_spec = pl.BlockSpec((tm, tn), lambda j, i, kk: (i, j))
    return _mm(name, (n // tn, m // tm, nb), (dy, wb), [a_spec, b_spec], jax.ShapeDtypeStruct((m, n), F32), o_spec,
               NT_DIMS, acc_tile=(tm, tn))


def _mm_nt_plain(name, dy, w, out_dtype=F32):
    m, k = dy.shape
    n = w.shape[0]
    tm = _pick(m, (512, 256, 128))
    tn = _pick(n, (512, 256, 128))
    a_spec = pl.BlockSpec((tm, k), lambda j, i: (i, 0))
    b_spec = pl.BlockSpec((tn, k), lambda j, i: (j, 0))
    o_spec = pl.BlockSpec((tm, tn), lambda j, i: (i, j))
    return _mm(name, (n // tn, m // tm), (dy, w), [a_spec, b_spec], jax.ShapeDtypeStruct((m, n), out_dtype), o_spec,
               NT_DIMS)


def _mm_tn_blocked(name, a, dy, nb, halves=False):
    s, k1 = a.shape
    bn = (dy.shape[-1] * (2 if halves else 1)) // nb
    tm = _pick(k1, (512, 256, 128))
    a_spec = pl.BlockSpec((s, tm), lambda j, i: (0, i))
    if halves:
        hb = nb // 2
        b_spec = pl.BlockSpec((None, s, bn), lambda j, i: (j // hb, 0, j % hb))
    else:
        b_spec = pl.BlockSpec((s, bn), lambda j, i: (0, j))
    o_spec = pl.BlockSpec((None, tm, bn), lambda j, i: (j, i, 0))
    return _mm(name, (nb, k1 // tm), (a, dy), [a_spec, b_spec], jax.ShapeDtypeStruct((nb, k1, bn), BF16), o_spec,
               TN_DIMS)


def _mm_tn_plain(name, a, dy):
    s, k1 = a.shape
    n = dy.shape[1]
    tm = _pick(k1, (512, 256, 128))
    tn = _pick(n, (512, 256, 128))
    a_spec = pl.BlockSpec((s, tm), lambda i, j: (0, i))
    b_spec = pl.BlockSpec((s, tn), lambda i, j: (0, j))
    o_spec = pl.BlockSpec((tm, tn), lambda i, j: (i, j))
    return _mm(name, (k1 // tm, n // tn), (a, dy), [a_spec, b_spec], jax.ShapeDtypeStruct((k1, n), BF16), o_spec,
               TN_DIMS)


def _rstd(x):
    return lax.rsqrt(jnp.mean(x * x, axis=-1, keepdims=True) + EPS)


def _norm_bwd(dy, xhat, r, g):
    dxhat = dy * g
    return r * (dxhat - xhat * jnp.mean(dxhat * xhat, axis=-1, keepdims=True))


def _rmsnorm_fwd(name, x, g):
    s, d = x.shape
    tr = _pick(s, (256, 128))

    def body(x_ref, g_ref, h_ref):
        xv = x_ref[...]
        h_ref[...] = (xv * _rstd(xv) * g_ref[...]).astype(BF16)

    return pl.pallas_call(
        body, name=name, grid=(s // tr,),
        in_specs=[pl.BlockSpec((tr, d), lambda i: (i, 0)), pl.BlockSpec((1, d), lambda i: (0, 0))],
        out_specs=pl.BlockSpec((tr, d), lambda i: (i, 0)),
        out_shape=jax.ShapeDtypeStruct((s, d), BF16), compiler_params=_cparams(("parallel",)),
    )(x, g)


def _rmsnorm_bwd(name, dh, x, g, dres):
    s, d = x.shape
    tr = _pick(s, (256, 128))

    def body(dh_ref, x_ref, g_ref, dres_ref, dx_ref, dxb_ref, dg_ref):
        xv = x_ref[...]
        r = _rstd(xv)
        xhat = xv * r
        dhv = dh_ref[...]
        dx = dres_ref[...] + _norm_bwd(dhv, xhat, r, g_ref[...])
        dx_ref[...] = dx
        dxb_ref[...] = dx.astype(BF16)
        part = jnp.sum(dhv * xhat, axis=0, keepdims=True)

        @pl.when(pl.program_id(0) == 0)
        def _():
            dg_ref[...] = part

        @pl.when(pl.program_id(0) > 0)
        def _():
            dg_ref[...] += part

    row = pl.BlockSpec((tr, d), lambda i: (i, 0))
    vec = pl.BlockSpec((1, d), lambda i: (0, 0))
    return pl.pallas_call(
        body, name=name, grid=(s // tr,), in_specs=[row, row, vec, row], out_specs=(row, row, vec),
        out_shape=(jax.ShapeDtypeStruct((s, d), F32), jax.ShapeDtypeStruct((s, d), BF16),
                   jax.ShapeDtypeStruct((1, d), F32)),
        compiler_params=_cparams(("arbitrary",)),
    )(dh, x, g, dres)


def _loss_head(y, target):
    s, d = y.shape
    tr = _pick(s, (256, 128))

    def body(y_ref, t_ref, dy_ref, dyb_ref, loss_ref):
        err = y_ref[...] - t_ref[...]
        dy = err * (1.0 / d)
        dy_ref[...] = dy
        dyb_ref[...] = dy.astype(BF16)
        part = 0.5 * jnp.sum(jnp.mean(err * err, axis=-1, keepdims=True), axis=0, keepdims=True)
        part = jnp.broadcast_to(part, (1, 128))

        @pl.when(pl.program_id(0) == 0)
        def _():
            loss_ref[...] = part

        @pl.when(pl.program_id(0) > 0)
        def _():
            loss_ref[...] += part

    row = pl.BlockSpec((tr, d), lambda i: (i, 0))
    return pl.pallas_call(
        body, name="loss_head", grid=(s // tr,), in_specs=[row, row],
        out_specs=(row, row, pl.BlockSpec((1, 128), lambda i: (0, 0))),
        out_shape=(jax.ShapeDtypeStruct((s, d), F32), jax.ShapeDtypeStruct((s, d), BF16),
                   jax.ShapeDtypeStruct((1, 128), F32)),
        compiler_params=_cparams(("arbitrary",)),
    )(y, target)


def _split_dot(x, tri):
    hi = x.astype(BF16)
    lo = (x - hi.astype(F32)).astype(BF16)
    return (jnp.dot(hi, tri, preferred_element_type=F32) + jnp.dot(lo, tri, preferred_element_type=F32))


def _tile_iotas():
    row = lax.broadcasted_iota(jnp.int32, (TILE, TILE), 0)
    col = lax.broadcasted_iota(jnp.int32, (TILE, TILE), 1)
    return row, col


def _sb_logits(qi, kb, mask):
    z = lax.dot_general(qi, kb, NT_DIMS, preferred_element_type=F32) * (HEAD_DIM ** -0.5)
    sp = jnp.log1p(jnp.exp(-jnp.abs(z)))
    lb = jnp.minimum(z, 0.0) - sp
    l1m = jnp.where(mask, -jnp.maximum(z, 0.0) - sp, 0.0)
    return lb, l1m


def _attn_fwd(p, gq, gk, ga, n_heads):
    s = p.shape[0]
    nq = s // TILE

    def body(q_ref, k_ref, v_ref, gq_ref, gk_ref, ga_ref, att_ref, o_ref, r_ref, qn_s, kn_s, vb_s):
        qv = q_ref[...]
        qn_s[...] = (qv * _rstd(qv) * gq_ref[...]).astype(BF16)
        kv = k_ref[...]
        kn_s[...] = (kv * _rstd(kv) * gk_ref[...]).astype(BF16)
        vb_s[...] = v_ref[...].astype(BF16)
        row, col = _tile_iotas()
        upper = (row > col).astype(BF16)
        gav = ga_ref[...]

        def qblock(i, _):
            rows = pl.ds(pl.multiple_of(i * TILE, TILE), TILE)
            qi = qn_s[rows, :]

            def kblock(jj, carry):
                o_acc, c = carry
                b = i - jj
                keys = pl.ds(pl.multiple_of(b * TILE, TILE), TILE)
                mask = col < row + (i - b) * TILE
                lb, l1m = _sb_logits(qi, kn_s[keys, :], mask)
                tail = _split_dot(l1m, upper) + c
                a = jnp.where(mask, jnp.exp(lb + tail), 0.0)
                o_acc = o_acc + jnp.dot(a.astype(BF16), vb_s[keys, :], preferred_element_type=F32)
                c = c + jnp.sum(l1m, axis=1, keepdims=True)
                return o_acc, c

            o_acc, c = lax.fori_loop(0, i + 1, kblock,
                                     (jnp.zeros((TILE, HEAD_DIM), F32), jnp.zeros((TILE, 1), F32)))
            o_ref[rows, :] = o_acc
            r_ref[rows, :] = jnp.broadcast_to(c, (TILE, HEAD_DIM))
            att_ref[rows, :] = (o_acc * _rstd(o_acc) * gav).astype(BF16)
            return 0

        lax.fori_loop(0, nq, qblock, 0)

    col_blk = lambda off: pl.BlockSpec((s, HEAD_DIM), lambda h: (0, off + h))
    vec = pl.BlockSpec((1, HEAD_DIM), lambda h: (0, 0))
    hvec = pl.BlockSpec((None, 1, HEAD_DIM), lambda h: (h, 0, 0))
    out = pl.BlockSpec((s, HEAD_DIM), lambda h: (0, h))
    w = n_heads * HEAD_DIM
    return pl.pallas_call(
        body, name="attn_fwd", grid=(n_heads,),
        in_specs=[col_blk(0), col_blk(n_heads), col_blk(2 * n_heads), vec, vec, hvec],
        out_specs=(out, out, out),
        out_shape=(jax.ShapeDtypeStruct((s, w), BF16), jax.ShapeDtypeStruct((s, w), F32),
                   jax.ShapeDtypeStruct((s, w), F32)),
        scratch_shapes=[pltpu.VMEM((s, HEAD_DIM), BF16)] * 3,
        compiler_params=_cparams(("parallel",)),
    )(p, p, p, gq, gk, ga)


def _attn_bwd(p, gq, gk, ga, o, rsum, dmix, n_heads):
    s = p.shape[0]
    nq = s // TILE

    def body(q_ref, k_ref, v_ref, gq_ref, gk_ref, ga_ref, o_ref, r_ref, dm_ref,
             dq_ref, dk_ref, dv_ref, dgq_ref, dgk_ref, dga_ref,
             qn_s, kn_s, vb_s, do_s, dqn_s, dkn_s, dv_s):
        head = pl.program_id(0)
        gqv, gkv = gq_ref[...], gk_ref[...]
        qv = q_ref[...]
        qn_s[...] = (qv * _rstd(qv) * gqv).astype(BF16)
        kv = k_ref[...]
        kn_s[...] = (kv * _rstd(kv) * gkv).astype(BF16)
        vb_s[...] = v_ref[...].astype(BF16)
        ov = o_ref[...]
        ro = _rstd(ov)
        ohat = ov * ro
        dm = dm_ref[...]
        dga_ref[...] = jnp.sum(dm * ohat, axis=0, keepdims=True)
        do_s[...] = _norm_bwd(dm, ohat, ro, ga_ref[...]).astype(BF16)
        dkn_s[...] = jnp.zeros_like(dkn_s)
        dv_s[...] = jnp.zeros_like(dv_s)
        row, col = _tile_iotas()
        lower_incl = (row <= col).astype(BF16)
        lower_excl = (row < col).astype(BF16)

        def qblock(i, _):
            rows = pl.ds(pl.multiple_of(i * TILE, TILE), TILE)
            qi = qn_s[rows, :]
            doi = do_s[rows, :]
            r_i = r_ref[rows, :][:, 0:1]

            def kblock(b, carry):
                dq_acc, pfx, pc = carry
                keys = pl.ds(pl.multiple_of(b * TILE, TILE), TILE)
                kb = kn_s[keys, :]
                mask = col < row + (i - b) * TILE
                lb, l1m = _sb_logits(qi, kb, mask)
                tail = r_i - pfx - _split_dot(l1m, lower_incl)
                a = jnp.where(mask, jnp.exp(lb + tail), 0.0)
                da = lax.dot_general(doi, vb_s[keys, :], NT_DIMS, preferred_element_type=F32)
                ds = da * a
                dl1m = jnp.where(mask, pc + _split_dot(ds, lower_excl), 0.0)
                beta = jnp.exp(lb)
                dz = ((ds * (1.0 - beta) - dl1m * beta) * (HEAD_DIM ** -0.5)).astype(BF16)
                dq_acc = dq_acc + jnp.dot(dz, kb, preferred_element_type=F32)
                dkn_s[keys, :] += lax.dot_general(dz, qi, TN_DIMS, preferred_element_type=F32)
                dv_s[keys, :] += lax.dot_general(a.astype(BF16), doi, TN_DIMS, preferred_element_type=F32)
                pfx = pfx + jnp.sum(l1m, axis=1, keepdims=True)
                pc = pc + jnp.sum(ds, axis=1, keepdims=True)
                return dq_acc, pfx, pc

            zero_col = jnp.zeros((TILE, 1), F32)
            dq_acc, _, _ = lax.fori_loop(0, i + 1, kblock, (jnp.zeros((TILE, HEAD_DIM), F32), zero_col, zero_col))
            dqn_s[rows, :] = dq_acc
            return 0

        lax.fori_loop(0, nq, qblock, 0)

        def norm_in_bwd(x_ref, g, dn, dx_ref, dg_ref):
            xv = x_ref[...]
            r = _rstd(xv)
            xhat = xv * r
            dx_ref[...] = _norm_bwd(dn, xhat, r, g).astype(BF16)
            part = jnp.sum(dn * xhat, axis=0, keepdims=True)

            @pl.when(head == 0)
            def _():
                dg_ref[...] = part

            @pl.when(head > 0)
            def _():
                dg_ref[...] += part

        norm_in_bwd(q_ref, gqv, dqn_s[...], dq_ref, dgq_ref)
        norm_in_bwd(k_ref, gkv, dkn_s[...], dk_ref, dgk_ref)
        dv_ref[...] = dv_s[...].astype(BF16)

    col_blk = lambda off: pl.BlockSpec((s, HEAD_DIM), lambda h: (0, off + h))
    vec = pl.BlockSpec((1, HEAD_DIM), lambda h: (0, 0))
    hvec = pl.BlockSpec((None, 1, HEAD_DIM), lambda h: (h, 0, 0))
    blk = pl.BlockSpec((s, HEAD_DIM), lambda h: (0, h))
    w = n_heads * HEAD_DIM
    big = jax.ShapeDtypeStruct((s, w), BF16)
    return pl.pallas_call(
        body, name="attn_bwd", grid=(n_heads,),
        in_specs=[col_blk(0), col_blk(n_heads), col_blk(2 * n_heads), vec, vec, hvec, blk, blk, blk],
        out_specs=(blk, blk, blk, vec, vec, hvec),
        out_shape=(big, big, big, jax.ShapeDtypeStruct((1, HEAD_DIM), F32), jax.ShapeDtypeStruct((1, HEAD_DIM), F32),
                   jax.ShapeDtypeStruct((n_heads, 1, HEAD_DIM), F32)),
        scratch_shapes=[pltpu.VMEM((s, HEAD_DIM), BF16)] * 4 + [pltpu.VMEM((s, HEAD_DIM), F32)] * 3,
        compiler_params=_cparams(("arbitrary",)),
    )(p, p, p, gq, gk, ga, o, rsum, dmix)


_INV_SQRT2 = 0.7071067811865476
_INV_SQRT_2PI = 0.3989422804014327


def _gelu(x):
    return 0.5 * x * (1.0 + lax.erf(x * _INV_SQRT2))


def _gelu_grad(x):
    return 0.5 * (1.0 + lax.erf(x * _INV_SQRT2)) + x * (_INV_SQRT_2PI * jnp.exp(-0.5 * x * x))


def _sgu_fwd(p, gs, w_s, b_s, gb, col0):
    s = p.shape[0]
    n_chunks = s // TILE

    def body(u_ref, v_ref, gs_ref, w_ref, b_ref, gb_ref, out_ref, vs_s):
        vg = _gelu(v_ref[...])
        vs_s[...] = (vg * _rstd(vg) * gs_ref[...]).astype(BF16)
        row, col = _tile_iotas()
        wt = jnp.where(col <= row, w_ref[...], 0.0).astype(BF16)
        bcol = b_ref[...]
        gbv = gb_ref[...]

        def chunk(c, _):
            rows = pl.ds(pl.multiple_of(c * TILE, TILE), TILE)
            mixed = jnp.dot(wt, vs_s[rows, :], preferred_element_type=F32) + bcol
            sg = _gelu(u_ref[rows, :]) * mixed
            out_ref[rows, :] = (sg * _rstd(sg) * gbv).astype(BF16)
            return 0

        lax.fori_loop(0, n_chunks, chunk, 0)

    col_blk = lambda off: pl.BlockSpec((s, HEAD_DIM), lambda g: (0, off + g))
    gvec = pl.BlockSpec((None, 1, HEAD_DIM), lambda g: (g, 0, 0))
    return pl.pallas_call(
        body, name="sgu_fwd", grid=(N_GROUPS,),
        in_specs=[col_blk(col0), col_blk(col0 + N_GROUPS), gvec,
                  pl.BlockSpec((None, TILE, TILE), lambda g: (g, 0, 0)),
                  pl.BlockSpec((None, TILE, 1), lambda g: (g, 0, 0)), gvec],
        out_specs=pl.BlockSpec((s, HEAD_DIM), lambda g: (0, g)),
        out_shape=jax.ShapeDtypeStruct((s, N_GROUPS * HEAD_DIM), BF16),
        scratch_shapes=[pltpu.VMEM((s, HEAD_DIM), BF16)],
        compiler_params=_cparams(("parallel",)),
    )(p, p, gs, w_s, b_s, gb)


def _sgu_bwd(p, gs, w_s, b_s, gb, dmix, col0, dm_col0):
    s = p.shape[0]
    n_chunks = s // TILE

    def body(u_ref, v_ref, gs_ref, w_ref, b_ref, gb_ref, dm_ref,
             du_ref, dv_ref, dgs_ref, dw_ref, db_ref, dgb_ref, vs_s, dvs_s):
        gsv = gs_ref[...]
        gbv = gb_ref[...]
        vg = _gelu(v_ref[...])
        vs_s[...] = (vg * _rstd(vg) * gsv).astype(BF16)
        row, col = _tile_iotas()
        causal = col <= row
        wt = jnp.where(causal, w_ref[...], 0.0).astype(BF16)
        bcol = b_ref[...]

        def chunk(c, carry):
            dw_acc, db_acc, dgb_acc = carry
            rows = pl.ds(pl.multiple_of(c * TILE, TILE), TILE)
            vs = vs_s[rows, :]
            mixed = jnp.dot(wt, vs, preferred_element_type=F32) + bcol
            u_pre = u_ref[rows, :]
            u = _gelu(u_pre)
            sg = u * mixed
            rs = _rstd(sg)
            sghat = sg * rs
            dm = dm_ref[rows, :]
            dsg = _norm_bwd(dm, sghat, rs, gbv)
            dgb_acc = dgb_acc + jnp.sum(dm * sghat, axis=0, keepdims=True)
            du_ref[rows, :] = (dsg * mixed * _gelu_grad(u_pre)).astype(BF16)
            dmixed = dsg * u
            db_acc = db_acc + jnp.sum(dmixed, axis=1, keepdims=True)
            dmb = dmixed.astype(BF16)
            dw_acc = dw_acc + lax.dot_general(dmb, vs, NT_DIMS, preferred_element_type=F32)
            dvs_s[rows, :] = lax.dot_general(wt, dmb, TN_DIMS, preferred_element_type=F32)
            return dw_acc, db_acc, dgb_acc

        dw_acc, db_acc, dgb_acc = lax.fori_loop(
            0, n_chunks, chunk,
            (jnp.zeros((TILE, TILE), F32), jnp.zeros((TILE, 1), F32), jnp.zeros((1, HEAD_DIM), F32)))
        dw_ref[...] = jnp.where(causal, dw_acc, 0.0)
        db_ref[...] = db_acc
        dgb_ref[...] = dgb_acc
        v_pre = v_ref[...]
        vg = _gelu(v_pre)
        rv = _rstd(vg)
        vhat = vg * rv
        dvs = dvs_s[...]
        dgs_ref[...] = jnp.sum(dvs * vhat, axis=0, keepdims=True)
        dv_ref[...] = (_norm_bwd(dvs, vhat, rv, gsv) * _gelu_grad(v_pre)).astype(BF16)

    col_blk = lambda off: pl.BlockSpec((s, HEAD_DIM), lambda g: (0, off + g))
    gvec = pl.BlockSpec((None, 1, HEAD_DIM), lambda g: (g, 0, 0))
    wspec = pl.BlockSpec((None, TILE, TILE), lambda g: (g, 0, 0))
    bspec = pl.BlockSpec((None, TILE, 1), lambda g: (g, 0, 0))
    blk = pl.BlockSpec((s, HEAD_DIM), lambda g: (0, g))
    big = jax.ShapeDtypeStruct((s, N_GROUPS * HEAD_DIM), BF16)
    gshape = jax.ShapeDtypeStruct((N_GROUPS, 1, HEAD_DIM), F32)
    return pl.pallas_call(
        body, name="sgu_bwd", grid=(N_GROUPS,),
        in_specs=[col_blk(col0), col_blk(col0 + N_GROUPS), gvec, wspec, bspec, gvec, col_blk(dm_col0)],
        out_specs=(blk, blk, gvec, wspec, bspec, gvec),
        out_shape=(big, big, gshape, jax.ShapeDtypeStruct((N_GROUPS, TILE, TILE), F32),
                   jax.ShapeDtypeStruct((N_GROUPS, TILE, 1), F32), gshape),
        scratch_shapes=[pltpu.VMEM((s, HEAD_DIM), BF16), pltpu.VMEM((s, HEAD_DIM), F32)],
        compiler_params=_cparams(("parallel",)),
    )(p, p, gs, w_s, b_s, gb, dmix)


def _shift_down(x, n):
    rows = lax.broadcasted_iota(jnp.int32, x.shape, 0)
    return jnp.where(rows >= n, pltpu.roll(x, n, 0), 0.0)


def _shift_up(x, n):
    s = x.shape[0]
    rows = lax.broadcasted_iota(jnp.int32, x.shape, 0)
    return jnp.where(rows < s - n, pltpu.roll(x, s - n, 0), 0.0)


def _conv(x, w, b):
    return b + w[0:1, :] * _shift_down(x, 2) + w[1:2, :] * _shift_down(x, 1) + w[2:3, :] * x


def _conv_specs(s, tn):
    xspec = pl.BlockSpec((2, s, tn), lambda j: (0, 0, j))
    wspec = pl.BlockSpec((2, CONV_WIDTH, tn), lambda j: (0, 0, j))
    bspec = pl.BlockSpec((2, 1, tn), lambda j: (0, 0, j))
    return xspec, wspec, bspec


def _conv_gate_fwd(up, cw, cb):
    _, s, f = up.shape
    tn = _pick(f, (256, 128))

    def body(x_ref, w_ref, b_ref, act_ref):
        gate = _conv(x_ref[0], w_ref[0], b_ref[0])
        val = _conv(x_ref[1], w_ref[1], b_ref[1])
        act_ref[...] = (gate * jax.nn.sigmoid(gate) * val).astype(BF16)

    xspec, wspec, bspec = _conv_specs(s, tn)
    return pl.pallas_call(
        body, name="conv_gate_fwd", grid=(f // tn,), in_specs=[xspec, wspec, bspec],
        out_specs=pl.BlockSpec((s, tn), lambda j: (0, j)), out_shape=jax.ShapeDtypeStruct((s, f), BF16),
        compiler_params=_cparams(("parallel",)),
    )(up, cw, cb)


def _conv_gate_bwd(up, cw, cb, dact):
    _, s, f = up.shape
    tn = _pick(f, (256, 128))

    def body(x_ref, w_ref, b_ref, da_ref, dx_ref, dw_ref, db_ref):
        xg, xv = x_ref[0], x_ref[1]
        wg, wv = w_ref[0], w_ref[1]
        gate = _conv(xg, wg, b_ref[0])
        val = _conv(xv, wv, b_ref[1])
        sig = jax.nn.sigmoid(gate)
        da = da_ref[...]
        dval = da * (gate * sig)
        dgate = da * val * (sig * (1.0 + gate * (1.0 - sig)))
        for half, (x, w, dy) in enumerate(((xg, wg, dgate), (xv, wv, dval))):
            dx_ref[half] = (w[2:3, :] * dy + w[1:2, :] * _shift_up(dy, 1) + w[0:1, :] * _shift_up(dy, 2)).astype(BF16)
            dw_ref[half, 0:1, :] = jnp.sum(dy * _shift_down(x, 2), axis=0, keepdims=True)
            dw_ref[half, 1:2, :] = jnp.sum(dy * _shift_down(x, 1), axis=0, keepdims=True)
            dw_ref[half, 2:3, :] = jnp.sum(dy * x, axis=0, keepdims=True)
            db_ref[half] = jnp.sum(dy, axis=0, keepdims=True)

    xspec, wspec, bspec = _conv_specs(s, tn)
    return pl.pallas_call(
        body, name="conv_gate_bwd", grid=(f // tn,),
        in_specs=[xspec, wspec, bspec, pl.BlockSpec((s, tn), lambda j: (0, j))],
        out_specs=(xspec, wspec, bspec),
        out_shape=(jax.ShapeDtypeStruct((2, s, f), BF16), jax.ShapeDtypeStruct((2, CONV_WIDTH, f), F32),
                   jax.ShapeDtypeStruct((2, 1, f), F32)),
        compiler_params=_cparams(("parallel",)),
    )(up, cw, cb, dact)


def _mesh_pos():
    return lax.axis_index("x"), lax.axis_index("y"), lax.axis_index("c")


def _remote(src, dst, send_sem, recv_sem, to):
    return pltpu.make_async_remote_copy(src_ref=src, dst_ref=dst, send_sem=send_sem, recv_sem=recv_sem,
                                        device_id=to, device_id_type=pl.DeviceIdType.MESH)


def _all_gather(shards):
    n = len(shards)

    def body(*refs):
        ins, outs = refs[:n], refs[n:2 * n]
        send_sems, recv_sems, local_sems = refs[2 * n:]
        x, y, c = _mesh_pos()
        me, sibling = (x, y, c), (x, y, 1 - c)
        chips = [(1 - x, y), (x, 1 - y), (1 - x, 1 - y)]
        slot = lambda pos: 4 * pos[0] + 2 * pos[1] + pos[2]

        def copy(a, k, block, to, src=None):
            dst = outs[a].at[slot(block)]
            return _remote(dst if src is None else src, dst, send_sems.at[a, k], recv_sems.at[a, k], to)

        started = []
        for a in range(n):
            mine = pltpu.make_async_copy(ins[a], outs[a].at[slot(me)], local_sems.at[a])
            mine.start()
            started.append(mine)
        sends = []
        for a in range(n):
            first = [copy(a, 0, me, sibling, src=ins[a])]
            first += [copy(a, 1 + j, me, (*chip, c), src=ins[a]) for j, chip in enumerate(chips)]
            for cp in first:
                cp.start()
            sends += first
        for a in range(n):
            for j, chip in enumerate(chips):
                copy(a, 1 + j, (*chip, c), me).wait_recv()
                fwd = copy(a, 4 + j, (*chip, c), sibling)
                fwd.start()
                sends.append(fwd)
        for a in range(n):
            copy(a, 0, sibling, me).wait_recv()
            for j, chip in enumerate(chips):
                copy(a, 4 + j, (*chip, 1 - c), me).wait_recv()
        for cp in sends:
            cp.wait_send()
        for mine in started:
            mine.wait()

    anyspec = pl.BlockSpec(memory_space=pl.ANY)
    return pl.pallas_call(
        body, name="weight_all_gather",
        in_specs=[anyspec] * n, out_specs=tuple([anyspec] * n),
        out_shape=tuple(jax.ShapeDtypeStruct((N_DEV,) + w.shape, w.dtype) for w in shards),
        scratch_shapes=[pltpu.SemaphoreType.DMA((n, 7)), pltpu.SemaphoreType.DMA((n, 7)),
                        pltpu.SemaphoreType.DMA((n,))],
    )(*shards)


def _exchange_blocks(blocked):
    n = len(blocked)

    def body(*refs):
        ins, outs = refs[:n], refs[n:2 * n]
        send_sems, recv_sems, local_sems = refs[2 * n:]
        x, y, c = _mesh_pos()
        my_slot = 4 * x + 2 * y + c
        peers = [(x ^ ((k >> 2) & 1), y ^ ((k >> 1) & 1), c ^ (k & 1)) for k in range(1, N_DEV)]
        copies = []
        for a in range(n):
            mine = pltpu.make_async_copy(ins[a].at[my_slot], outs[a].at[my_slot], local_sems.at[a])
            mine.start()
            copies.append(mine)
        sends = []
        for a in range(n):
            for k, peer in enumerate(peers):
                peer_slot = 4 * peer[0] + 2 * peer[1] + peer[2]
                cp = _remote(ins[a].at[peer_slot], outs[a].at[my_slot], send_sems.at[a, k], recv_sems.at[a, k], peer)
                cp.start()
                sends.append((cp, a, k, peer_slot))
        for cp, a, k, peer_slot in sends:
            _remote(ins[a].at[peer_slot], outs[a].at[peer_slot], send_sems.at[a, k], recv_sems.at[a, k],
                    peers[k]).wait_recv()
        for cp, _, _, _ in sends:
            cp.wait_send()
        for mine in copies:
            mine.wait()

    anyspec = pl.BlockSpec(memory_space=pl.ANY)
    return pl.pallas_call(
        body, name="grad_exchange",
        in_specs=[anyspec] * n, out_specs=tuple([anyspec] * n),
        out_shape=tuple(jax.ShapeDtypeStruct(w.shape, w.dtype) for w in blocked),
        scratch_shapes=[pltpu.SemaphoreType.DMA((n, 7)), pltpu.SemaphoreType.DMA((n, 7)),
                        pltpu.SemaphoreType.DMA((n,))],
    )(*blocked)


HBM_SPEC = pl.BlockSpec(memory_space=pltpu.HBM)
SEM_SPEC = pl.BlockSpec(memory_space=pltpu.SEMAPHORE)
ANY_SPEC = pl.BlockSpec(memory_space=pl.ANY)
TOKEN_SPEC = pl.BlockSpec(memory_space=pltpu.VMEM)
TOKEN_SHAPE = jax.ShapeDtypeStruct((8, 128), F32)
DATAFLOW = pltpu.SideEffectType.DATAFLOW_SIDE_EFFECTING
GATHER_PLANE = (2, 4, 6)


def _slot(pos):
    return 4 * pos[0] + 2 * pos[1] + pos[2]


def _flip(pos, k):
    return (pos[0] ^ ((k >> 2) & 1), pos[1] ^ ((k >> 1) & 1), pos[2] ^ (k & 1))


def _hbm(a):
    return pltpu.with_memory_space_constraint(a, pltpu.HBM)


def _hbm_shapes(arrays):
    return tuple(pltpu.HBM(a.shape, a.dtype) for a in arrays)


class _Split:
    def __init__(self, n, outs, has_sems):
        k = 2 * n if has_sems else 0
        self.n = n
        self.sems = list(outs[:k])
        self.bufs = list(outs[k:k + 2 * n])
        self.token = outs[-1]


def _split_call(name, body, bufs, sems_in, makes_sems, after):
    n = len(bufs) // 2
    k = 2 * n if makes_sems else 0
    m = len(sems_in)

    def wrapped(*refs):
        srcs, dsts = refs[:n], refs[n:2 * n]
        s_in = refs[2 * n:2 * n + m]
        s_out = refs[2 * n + m + 1:2 * n + m + 1 + k]
        token, local_sems = refs[-2], refs[-1]
        body(srcs, dsts, s_in, s_out, local_sems)
        token[...] = jnp.zeros_like(token)

    outs = pl.pallas_call(
        wrapped, name=name,
        out_shape=(pltpu.SemaphoreType.DMA(()),) * k + _hbm_shapes(bufs) + (TOKEN_SHAPE,),
        in_specs=[HBM_SPEC] * (2 * n) + [SEM_SPEC] * m + [ANY_SPEC],
        out_specs=(SEM_SPEC,) * k + (HBM_SPEC,) * (2 * n) + (TOKEN_SPEC,),
        input_output_aliases={i: k + i for i in range(2 * n)},
        scratch_shapes=[pltpu.SemaphoreType.DMA((n,))],
        compiler_params=pltpu.CompilerParams(has_side_effects=DATAFLOW),
    )(*[_hbm(b) for b in bufs], *sems_in, after)
    return _Split(n, outs, makes_sems)


def _wait_slots(land, count, send_sem, recv_sem, me, send=False, recv=False):
    span = land.at[pl.ds(0, count)]
    cp = _remote(span, span, send_sem, recv_sem, me)
    if send:
        cp.wait_send()
    if recv:
        cp.wait_recv()


def _gather_start(name, shards, after):
    n = len(shards)
    my_slot = _slot(_mesh_pos())
    lands = [lax.dynamic_update_slice(lax.empty((N_DEV,) + w.shape, w.dtype), w[None], (my_slot, 0, 0)) for w in shards]

    def body(srcs, dsts, _, sems, local_sems):
        me = _mesh_pos()
        for a in range(n):
            for k in (1,) + GATHER_PLANE:
                _remote(srcs[a], dsts[a].at[_slot(me)], sems[a], sems[n + a], _flip(me, k)).start()

    return _split_call(name, body, list(shards) + lands, [], True, after)


def _gather_forward(name, started, after):
    n = started.n

    def body(srcs, dsts, sems_a, sems_b, local_sems):
        me = _mesh_pos()
        sibling = _flip(me, 1)
        for a in range(n):
            _wait_slots(dsts[a], 4, sems_a[a], sems_a[n + a], me, recv=True)
            for k in GATHER_PLANE:
                block = dsts[a].at[_slot(_flip(me, k))]
                _remote(block, block, sems_b[a], sems_b[n + a], sibling).start()
        for a in range(n):
            _wait_slots(dsts[a], 4, sems_a[a], sems_a[n + a], me, send=True)

    return _split_call(name, body, started.bufs, started.sems, True, after)


def _gather_finish(name, forwarded, after):
    n = forwarded.n

    def body(srcs, dsts, sems_b, _, local_sems):
        me = _mesh_pos()
        for a in range(n):
            _wait_slots(dsts[a], 3, sems_b[a], sems_b[n + a], me, send=True, recv=True)

    return _split_call(name, body, forwarded.bufs, forwarded.sems, False, after).bufs[n:]


def _exchange_start(name, blocked, after):
    n = len(blocked)
    my_slot = _slot(_mesh_pos())
    lands = [lax.dynamic_update_slice(lax.empty(w.shape, w.dtype), lax.dynamic_slice_in_dim(w, my_slot, 1, 0),
                                      (my_slot, 0, 0)) for w in blocked]

    def body(srcs, dsts, _, sems, local_sems):
        me = _mesh_pos()
        for a in range(n):
            for k in range(1, N_DEV):
                peer = _flip(me, k)
                _remote(srcs[a].at[_slot(peer)], dsts[a].at[_slot(me)], sems[a], sems[n + a], peer).start()

    return _split_call(name, body, list(blocked) + lands, [], True, after)


def _exchange_finish(name, started, after):
    n = started.n

    def body(srcs, dsts, sems, _, local_sems):
        me = _mesh_pos()
        for a in range(n):
            _wait_slots(dsts[a], N_DEV - 1, sems[a], sems[n + a], me, send=True, recv=True)

    return _split_call(name, body, started.bufs, started.sems, False, after).bufs[n:]


def _adamw_math(w, g, m, v):
    m = ADAM_B1 * m + (1.0 - ADAM_B1) * g
    v = ADAM_B2 * v + (1.0 - ADAM_B2) * (g * g)
    m_hat = m / (1.0 - ADAM_B1 ** ADAM_STEP)
    v_hat = v / (1.0 - ADAM_B2 ** ADAM_STEP)
    delta = -ADAM_LR * (m_hat / (jnp.sqrt(v_hat) + ADAM_EPS) + ADAM_WD * w)
    return delta, m, v


def _adamw(name, w, m, v, parts, layer, prev=None):
    _, r, c = w.shape
    tr = _pick(r, tuple(t for t in (256, 128, 64, 32, 16) if t * c <= ADAMW_TILE_ELEMS))
    n_prev = 0 if prev is None else 4

    def body(*refs):
        w_ref, m_ref, v_ref, p_ref = refs[:4]
        g_ref, d_ref, nm_ref, nv_ref = refs[4 + n_prev:]
        g = p_ref[0].astype(F32)
        for src in range(1, N_DEV):
            g = g + p_ref[src].astype(F32)
        delta, nm, nv = _adamw_math(w_ref[...], g, m_ref[...], v_ref[...])
        g_ref[...] = g
        d_ref[...] = delta
        nm_ref[...] = nm
        nv_ref[...] = nv

    wspec = pl.BlockSpec((None, tr, c), lambda i: (layer, i, 0))
    pspec = pl.BlockSpec((N_DEV, tr, c), lambda i: (0, i, 0))
    shp = jax.ShapeDtypeStruct(w.shape, F32)
    return pl.pallas_call(
        body, name=name, grid=(r // tr,), in_specs=[wspec] * 3 + [pspec] + [ANY_SPEC] * n_prev,
        out_specs=(wspec,) * 4, out_shape=(shp,) * 4, input_output_aliases={4 + j: j for j in range(n_prev)},
        compiler_params=_cparams(("parallel",)),
    )(w, m, v, parts, *([] if prev is None else prev))


PACK_TILE = 8 * 128


def _pack(arrays):
    flat = []
    for a in arrays:
        v = a.reshape(-1)
        pad = (-v.shape[0]) % PACK_TILE
        flat.append(jnp.pad(v, (0, pad)) if pad else v)
    return jnp.concatenate(flat).reshape(-1, 128)


def _unpack(buf, like):
    flat = buf.reshape(-1)
    out, off = [], 0
    for a in like:
        n = 1
        for dim in a.shape:
            n *= dim
        out.append(flat[off:off + n].reshape(a.shape))
        off += n + (-n) % PACK_TILE
    return out


def _sum_slots(gathered):
    _, r, c = gathered.shape

    def body(x_ref, o_ref):
        acc = x_ref[0]
        for src in range(1, N_DEV):
            acc = acc + x_ref[src]
        o_ref[...] = acc

    return pl.pallas_call(body, name="small_grad_sum", out_shape=jax.ShapeDtypeStruct((r, c), F32))(gathered)


def _adamw_small(w, g, m, v):
    shp = jax.ShapeDtypeStruct(w.shape, F32)

    def body(w_ref, g_ref, m_ref, v_ref, d_ref, nm_ref, nv_ref):
        delta, nm, nv = _adamw_math(w_ref[...], g_ref[...], m_ref[...], v_ref[...])
        d_ref[...] = delta
        nm_ref[...] = nm
        nv_ref[...] = nv

    return pl.pallas_call(body, name="adamw_small", out_shape=(shp,) * 3)(w, g, m, v)


def kernel(x, attn_norm_g, w_in, q_norm_g, k_norm_g, sgu_norm_g, sgu_w, sgu_b, out_norm_a_g, out_norm_b_g, w_out, ffn_norm_g, w_up, conv_w, conv_b, w_down, loss_target, m_attn_norm_g, m_w_in, m_q_norm_g, m_k_norm_g, m_sgu_norm_g, m_sgu_w, m_sgu_b, m_out_norm_a_g, m_out_norm_b_g, m_w_out, m_ffn_norm_g, m_w_up, m_conv_w, m_conv_b, m_w_down, v_attn_norm_g, v_w_in, v_q_norm_g, v_k_norm_g, v_sgu_norm_g, v_sgu_w, v_sgu_b, v_out_norm_a_g, v_out_norm_b_g, v_w_out, v_ffn_norm_g, v_w_up, v_conv_w, v_conv_b, v_w_down):
    depth = w_in.shape[0]
    s, d = x.shape[1], x.shape[2]
    n_heads = (d // 2) // HEAD_DIM
    sgu_col0 = 3 * n_heads
    f2 = w_up.shape[2] * N_DEV
    ff = f2 // 2
    my_slot = 4 * lax.axis_index("x") + 2 * lax.axis_index("y") + lax.axis_index("c")

    wb = [(w_in[l].astype(BF16), w_out[l].astype(BF16), w_up[l].astype(BF16), w_down[l].astype(BF16))
          for l in range(depth)]
    groups = [[wb[0][0]], [wb[0][1], wb[0][2], wb[0][3], conv_w.reshape(depth * CONV_WIDTH, -1)]]
    groups += [list(wb[l]) for l in range(1, depth)]
    token = attn_norm_g
    started = []
    for i, group in enumerate(groups):
        started.append(_gather_start(f"gather{i}_start", group, token))
        token = started[-1].token

    conv_b_all = conv_b.reshape(depth, 2, 1, ff)
    sgu_b_col = sgu_b[..., None]
    gathered = [None] * depth
    forwarded = _gather_forward("gather0_forward", started[0], token)
    win0 = _gather_finish("gather0_finish", forwarded, forwarded.token)[0]

    xs = x[0]
    saved = []
    for l in range(depth):
        g1 = attn_norm_g[l][None]
        g2 = ffn_norm_g[l][None]
        gq, gk = q_norm_g[l][None], k_norm_g[l][None]
        ga = out_norm_a_g[l][:, None, :]
        gs = sgu_norm_g[l][:, None, :]
        gb = out_norm_b_g[l][:, None, :]
        if l == 0:
            win_g = win0
        else:
            win_g = gathered[l][0]
        h1 = _rmsnorm_fwd("attn_norm_fwd", xs, g1)
        p = _mm_nn_blocked("in_proj", h1, win_g, F32)
        att, o, rsum = _attn_fwd(p, gq, gk, ga, n_heads)
        if l == 0:
            forwarded = _gather_forward("gather1_forward", started[1], att)
        sg = _sgu_fwd(p, gs, sgu_w[l], sgu_b_col[l], gb, sgu_col0)
        mix = jnp.concatenate([att, sg], axis=-1)
        if l == 0:
            rest = _gather_finish("gather1_finish", forwarded, mix)
            gathered[0] = (win0, rest[0], rest[1], rest[2])
            conv_w_all = rest[3].reshape(N_DEV, depth, CONV_WIDTH, -1)
            conv_w_all = jnp.transpose(conv_w_all, (1, 2, 0, 3)).reshape(depth, CONV_WIDTH, 2, ff)
            conv_w_all = jnp.transpose(conv_w_all, (0, 2, 1, 3))
        _, wout_g, wup_g, wdown_g = gathered[l]
        wout_full = wout_g.reshape(d, d)
        wdown_full = wdown_g.reshape(ff, d)
        x1 = _mm_nn_res("out_proj", mix, wout_full, xs)
        h2 = _rmsnorm_fwd("ffn_norm_fwd", x1, g2)
        up = _mm_nn_blocked("up_proj", h2, wup_g, F32, halves=True)
        act = _conv_gate_fwd(up, conv_w_all[l], conv_b_all[l])
        if l + 1 < depth:
            forwarded = _gather_forward(f"gather{l + 2}_forward", started[l + 2], act)
        x2 = _mm_nn_res("down_proj", act, wdown_full, x1)
        if l + 1 < depth:
            gathered[l + 1] = tuple(_gather_finish(f"gather{l + 2}_finish", forwarded, x2))
        saved.append((xs, h1, p, o, rsum, mix, x1, h2, up, act))
        xs = x2

    dx, dxb, loss_vec = _loss_head(xs, loss_target[0])
    loss = lax.psum(loss_vec[0, 0], MESH_AXES)

    exchanges = []
    small = [None] * depth
    for l in reversed(range(depth)):
        xs0, h1, p, o, rsum, mix, x1, h2, up, act = saved[l]
        win_g, wout_g, wup_g, wdown_g = gathered[l]
        wout_full = wout_g.reshape(d, d)
        wdown_full = wdown_g.reshape(ff, d)
        g1 = attn_norm_g[l][None]
        g2 = ffn_norm_g[l][None]
        gq, gk = q_norm_g[l][None], k_norm_g[l][None]
        ga = out_norm_a_g[l][:, None, :]
        gs = sgu_norm_g[l][:, None, :]
        gb = out_norm_b_g[l][:, None, :]
        d_wdown = _mm_tn_plain("down_proj_dw", act, dxb)
        dact = _mm_nt_plain("down_proj_dx", dxb, wdown_full)
        dup, d_cw, d_cb = _conv_gate_bwd(up, conv_w_all[l], conv_b_all[l], dact)
        d_wup = _mm_tn_blocked("up_proj_dw", h2, dup, N_DEV, halves=True)
        exchanges.append((l, ("w_up", "w_down"), _exchange_start(
            f"grad_ffn{l}_start", [d_wup, d_wdown.reshape(N_DEV, ff // N_DEV, d)], d_wup)))
        dh2 = _mm_nt_blocked("up_proj_dx", dup, wup_g, halves=True)
        dx, dxb, d_g2 = _rmsnorm_bwd("ffn_norm_bwd", dh2, x1, g2 + exchanges[-1][2].token[0:1, 0:1], dx)
        d_wout = _mm_tn_plain("out_proj_dw", mix, dxb)
        dmix = _mm_nt_plain("out_proj_dx", dxb, wout_full)
        dq, dk, dv, d_gq, d_gk, d_ga = _attn_bwd(p, gq, gk, ga, o, rsum, dmix, n_heads)
        du, dvs, d_gs, d_sw, d_sb, d_gb = _sgu_bwd(p, gs, sgu_w[l], sgu_b_col[l], gb, dmix, sgu_col0, n_heads)
        dp = jnp.concatenate([dq, dk, dv, du, dvs], axis=-1)
        d_win = _mm_tn_blocked("in_proj_dw", h1, dp, N_DEV)
        exchanges.append((l, ("w_in", "w_out"), _exchange_start(
            f"grad_mix{l}_start", [d_win, d_wout.reshape(N_DEV, d // N_DEV, d)], d_win)))
        dh1 = _mm_nt_blocked("in_proj_dx", dp, win_g)
        dx, dxb, d_g1 = _rmsnorm_bwd("attn_norm_bwd", dh1, xs0, g1 + exchanges[-1][2].token[0:1, 0:1], dx)
        small[l] = dict(attn_norm_g=d_g1[0], q_norm_g=d_gq[0], k_norm_g=d_gk[0], sgu_norm_g=d_gs[:, 0], sgu_w=d_sw,
                        sgu_b=d_sb[..., 0], out_norm_a_g=d_ga[:, 0], out_norm_b_g=d_gb[:, 0], ffn_norm_g=d_g2[0],
                        conv_w=jnp.transpose(d_cw, (1, 0, 2)).reshape(CONV_WIDTH, f2), conv_b=d_cb.reshape(f2))
    grad_x = dx[None]

    small_names = ["attn_norm_g", "q_norm_g", "k_norm_g", "sgu_norm_g", "sgu_w", "sgu_b", "out_norm_a_g",
                   "out_norm_b_g", "ffn_norm_g", "conv_b", "conv_w"]
    small_g = [jnp.stack([small[l][n] for l in range(depth)]) for n in small_names]
    packed = _pack(small_g)
    small_all = _all_gather([packed])[0]
    small_sum = _unpack(_sum_slots(small_all), small_g)
    g_small = dict(zip(small_names, small_sum))
    cwn = conv_w.shape[2]
    g_small["conv_w"] = lax.dynamic_slice_in_dim(g_small["conv_w"], my_slot * cwn, cwn, axis=2)

    res = {}
    big = dict(w_in=(w_in, m_w_in, v_w_in), w_out=(w_out, m_w_out, v_w_out), w_up=(w_up, m_w_up, v_w_up),
               w_down=(w_down, m_w_down, v_w_down))
    after = dx
    for l, names, ex in exchanges:
        stage = "ffn" if names[0] == "w_up" else "mix"
        landed = _exchange_finish(f"grad_{stage}{l}_finish", ex, after)
        for name, parts in zip(names, landed):
            w, m, v = big[name]
            res[name] = _adamw(f"adamw_{name}", w, m, v, parts, l, res.get(name))
            after = res[name][0]
    small_w = dict(attn_norm_g=(attn_norm_g, m_attn_norm_g, v_attn_norm_g), q_norm_g=(q_norm_g, m_q_norm_g, v_q_norm_g),
                   k_norm_g=(k_norm_g, m_k_norm_g, v_k_norm_g), sgu_norm_g=(sgu_norm_g, m_sgu_norm_g, v_sgu_norm_g),
                   sgu_w=(sgu_w, m_sgu_w, v_sgu_w), sgu_b=(sgu_b, m_sgu_b, v_sgu_b),
                   out_norm_a_g=(out_norm_a_g, m_out_norm_a_g, v_out_norm_a_g),
                   out_norm_b_g=(out_norm_b_g, m_out_norm_b_g, v_out_norm_b_g),
                   ffn_norm_g=(ffn_norm_g, m_ffn_norm_g, v_ffn_norm_g), conv_b=(conv_b, m_conv_b, v_conv_b),
                   conv_w=(conv_w, m_conv_w, v_conv_w))
    like = [small_w[n][0] for n in small_names]
    pw = _pack([small_w[n][0] for n in small_names])
    pm = _pack([small_w[n][1] for n in small_names])
    pv = _pack([small_w[n][2] for n in small_names])
    pg = _pack([g_small[n].reshape(small_w[n][0].shape) for n in small_names])
    pd, pnm, pnv = _adamw_small(pw, pg, pm, pv)
    for n, dlt, nm, nv in zip(small_names, _unpack(pd, like), _unpack(pnm, like), _unpack(pnv, like)):
        res[n] = (g_small[n].reshape(small_w[n][0].shape), dlt, nm, nv)

    order = ["attn_norm_g", "w_in", "q_norm_g", "k_norm_g", "sgu_norm_g", "sgu_w", "sgu_b", "out_norm_a_g",
             "out_norm_b_g", "w_out", "ffn_norm_g", "w_up", "conv_w", "conv_b", "w_down"]
    outs = [loss, grad_x]
    for field in range(4):
        outs += [res[n][field] for n in order]
    return tuple(outs)
```

```python
import functools

import jax
import jax.numpy as jnp
from jax import lax
from jax.experimental import pallas as pl
from jax.experimental.pallas import tpu as pltpu

F32 = jnp.float32
BF16 = jnp.bfloat16
EPS = 1e-6
HEAD_DIM = 128
TILE = 128
ATTN_VMEM_MB = 58
ATTN_HEADS_PER_STEP = 4
N_GROUPS = 8
CONV_WIDTH = 3
N_DEV = 8
MESH_AXES = ("x", "y", "c")
MIB = 1024 * 1024

ADAM_LR = 0.001
ADAM_B1 = 0.9
ADAM_B2 = 0.999
ADAM_EPS = 1e-08
ADAM_WD = 0.01
ADAM_STEP = 10
ADAMW_TILE_ELEMS = 160 * 1024

NT_DIMS = (((1,), (1,)), ((), ()))
NN_DIMS = (((1,), (0,)), ((), ()))
TN_DIMS = (((0,), (0,)), ((), ()))


def _cparams(sem, vmem_mb=48):
    return pltpu.CompilerParams(dimension_semantics=sem, vmem_limit_bytes=vmem_mb * MIB)


def _pick(n, cands):
    for c in cands:
        if n % c == 0:
            return c
    return n


def _mm(name, grid, ins, in_specs, out_shape, out_spec, dims, acc_tile=None, has_res=False, vmem_mb=48):
    nk = grid[2] if len(grid) == 3 else 1

    def body(*refs):
        if has_res:
            a_ref, b_ref, r_ref, o_ref = refs[:4]
            rest = refs[4:]
        else:
            a_ref, b_ref, o_ref = refs[:3]
            r_ref = None
            rest = refs[3:]
        prod = lax.dot_general(a_ref[...], b_ref[...], dims, preferred_element_type=F32)
        if nk == 1:
            if has_res:
                prod = prod + r_ref[...]
            o_ref[...] = prod.astype(o_ref.dtype)
        else:
            acc_ref = rest[0]
            k = pl.program_id(2)

            @pl.when(k == 0)
            def _():
                acc_ref[...] = prod

            @pl.when(k > 0)
            def _():
                acc_ref[...] += prod

            @pl.when(k == nk - 1)
            def _():
                o = acc_ref[...]
                if has_res:
                    o = o + r_ref[...]
                o_ref[...] = o.astype(o_ref.dtype)

    sem = ("parallel", "parallel") + (("arbitrary",) if len(grid) == 3 else ())
    scratch = [pltpu.VMEM(acc_tile, F32)] if nk > 1 else []
    return pl.pallas_call(
        body, name=name, grid=grid, in_specs=in_specs, out_specs=out_spec, out_shape=out_shape,
        scratch_shapes=scratch, compiler_params=_cparams(sem, vmem_mb),
    )(*ins)


def _mm_nn_blocked(name, a, wb, out_dtype, halves=False):
    m, k = a.shape
    nb, _, bn = wb.shape
    tm = _pick(m, (512, 256, 128))
    a_spec = pl.BlockSpec((tm, k), lambda j, i: (i, 0))
    b_spec = pl.BlockSpec((None, k, bn), lambda j, i: (j, 0, 0))
    if halves:
        hb = nb // 2
        out_shape = jax.ShapeDtypeStruct((2, m, hb * bn), out_dtype)
        o_spec = pl.BlockSpec((None, tm, bn), lambda j, i: (j // hb, i, j % hb))
    else:
        out_shape = jax.ShapeDtypeStruct((m, nb * bn), out_dtype)
        o_spec = pl.BlockSpec((tm, bn), lambda j, i: (i, j))
    return _mm(name, (nb, m // tm), (a, wb), [a_spec, b_spec], out_shape, o_spec, NN_DIMS)


def _mm_nn_res(name, a, w, res):
    m, k = a.shape
    n = w.shape[1]
    tm = _pick(m, (512, 256, 128))
    tn = _pick(n, (512, 256, 128))
    tk = k if k <= 2048 else _pick(k, (1408, 1024, 512, 256, 128))
    grid = (n // tn, m // tm, k // tk)
    a_spec = pl.BlockSpec((tm, tk), lambda j, i, kk: (i, kk))
    b_spec = pl.BlockSpec((tk, tn), lambda j, i, kk: (kk, j))
    r_spec = pl.BlockSpec((tm, tn), lambda j, i, kk: (i, j))
    o_spec = pl.BlockSpec((tm, tn), lambda j, i, kk: (i, j))
    return _mm(name, grid, (a, w, res), [a_spec, b_spec, r_spec], jax.ShapeDtypeStruct((m, n), F32), o_spec,
               NN_DIMS, acc_tile=(tm, tn), has_res=True)


def _mm_nt_blocked(name, dy, wb, halves=False):
    nb, n, bn = wb.shape
    m = dy.shape[-2]
    tm = _pick(m, (512, 256, 128))
    tn = _pick(n, (512, 256, 128))
    if halves:
        hb = nb // 2
        a_spec = pl.BlockSpec((None, tm, bn), lambda j, i, kk: (kk // hb, i, kk % hb))
    else:
        a_spec = pl.BlockSpec((tm, bn), lambda j, i, kk: (i, kk))
    b_spec = pl.BlockSpec((None, tn, bn), lambda j, i, kk: (kk, j, 0))
    o_spec = pl.BlockSpec((tm, tn), lambda j, i, kk: (i, j))
    return _mm(name, (n // tn, m // tm, nb), (dy, wb), [a_spec, b_spec], jax.ShapeDtypeStruct((m, n), F32), o_spec,
               NT_DIMS, acc_tile=(tm, tn))


def _mm_nt_plain(name, dy, w, out_dtype=F32):
    m, k = dy.shape
    n = w.shape[0]
    tm = _pick(m, (512, 256, 128))
    tn = _pick(n, (512, 256, 128))
    a_spec = pl.BlockSpec((tm, k), lambda j, i: (i, 0))
    b_spec = pl.BlockSpec((tn, k), lambda j, i: (j, 0))
    o_spec = pl.BlockSpec((tm, tn), lambda j, i: (i, j))
    return _mm(name, (n // tn, m // tm), (dy, w), [a_spec, b_spec], jax.ShapeDtypeStruct((m, n), out_dtype), o_spec,
               NT_DIMS)


def _mm_tn_blocked(name, a, dy, nb, halves=False):
    s, k1 = a.shape
    bn = (dy.shape[-1] * (2 if halves else 1)) // nb
    tm = _pick(k1, (512, 256, 128))
    a_spec = pl.BlockSpec((s, tm), lambda j, i: (0, i))
    if halves:
        hb = nb // 2
        b_spec = pl.BlockSpec((None, s, bn), lambda j, i: (j // hb, 0, j % hb))
    else:
        b_spec = pl.BlockSpec((s, bn), lambda j, i: (0, j))
    o_spec = pl.BlockSpec((None, tm, bn), lambda j, i: (j, i, 0))
    return _mm(name, (nb, k1 // tm), (a, dy), [a_spec, b_spec], jax.ShapeDtypeStruct((nb, k1, bn), BF16), o_spec,
               TN_DIMS)


def _mm_tn_plain(name, a, dy):
    s, k1 = a.shape
    n = dy.shape[1]
    tm = _pick(k1, (512, 256, 128))
    tn = _pick(n, (512, 256, 128))
    a_spec = pl.BlockSpec((s, tm), lambda i, j: (0, i))
    b_spec = pl.BlockSpec((s, tn), lambda i, j: (0, j))
    o_spec = pl.BlockSpec((tm, tn), lambda i, j: (i, j))
    return _mm(name, (k1 // tm, n // tn), (a, dy), [a_spec, b_spec], jax.ShapeDtypeStruct((k1, n), BF16), o_spec,
               TN_DIMS)


def _rstd(x):
    return lax.rsqrt(jnp.mean(x * x, axis=-1, keepdims=True) + EPS)


def _norm_bwd(dy, xhat, r, g):
    dxhat = dy * g
    return r * (dxhat - xhat * jnp.mean(dxhat * xhat, axis=-1, keepdims=True))


def _rmsnorm_fwd(name, x, g):
    s, d = x.shape
    tr = _pick(s, (256, 128))

    def body(x_ref, g_ref, h_ref):
        xv = x_ref[...]
        h_ref[...] = (xv * _rstd(xv) * g_ref[...]).astype(BF16)

    return pl.pallas_call(
        body, name=name, grid=(s // tr,),
        in_specs=[pl.BlockSpec((tr, d), lambda i: (i, 0)), pl.BlockSpec((1, d), lambda i: (0, 0))],
        out_specs=pl.BlockSpec((tr, d), lambda i: (i, 0)),
        out_shape=jax.ShapeDtypeStruct((s, d), BF16), compiler_params=_cparams(("parallel",)),
    )(x, g)


def _rmsnorm_bwd(name, dh, x, g, dres):
    s, d = x.shape
    tr = _pick(s, (256, 128))

    def body(dh_ref, x_ref, g_ref, dres_ref, dx_ref, dxb_ref, dg_ref):
        xv = x_ref[...]
        r = _rstd(xv)
        xhat = xv * r
        dhv = dh_ref[...]
        dx = dres_ref[...] + _norm_bwd(dhv, xhat, r, g_ref[...])
        dx_ref[...] = dx
        dxb_ref[...] = dx.astype(BF16)
        part = jnp.sum(dhv * xhat, axis=0, keepdims=True)

        @pl.when(pl.program_id(0) == 0)
        def _():
            dg_ref[...] = part

        @pl.when(pl.program_id(0) > 0)
        def _():
            dg_ref[...] += part

    row = pl.BlockSpec((tr, d), lambda i: (i, 0))
    vec = pl.BlockSpec((1, d), lambda i: (0, 0))
    return pl.pallas_call(
        body, name=name, grid=(s // tr,), in_specs=[row, row, vec, row], out_specs=(row, row, vec),
        out_shape=(jax.ShapeDtypeStruct((s, d), F32), jax.ShapeDtypeStruct((s, d), BF16),
                   jax.ShapeDtypeStruct((1, d), F32)),
        compiler_params=_cparams(("arbitrary",)),
    )(dh, x, g, dres)


def _loss_head(y, target):
    s, d = y.shape
    tr = _pick(s, (256, 128))

    def body(y_ref, t_ref, dy_ref, dyb_ref, loss_ref):
        err = y_ref[...] - t_ref[...]
        dy = err * (1.0 / d)
        dy_ref[...] = dy
        dyb_ref[...] = dy.astype(BF16)
        part = 0.5 * jnp.sum(jnp.mean(err * err, axis=-1, keepdims=True), axis=0, keepdims=True)
        part = jnp.broadcast_to(part, (1, 128))

        @pl.when(pl.program_id(0) == 0)
        def _():
            loss_ref[...] = part

        @pl.when(pl.program_id(0) > 0)
        def _():
            loss_ref[...] += part

    row = pl.BlockSpec((tr, d), lambda i: (i, 0))
    return pl.pallas_call(
        body, name="loss_head", grid=(s // tr,), in_specs=[row, row],
        out_specs=(row, row, pl.BlockSpec((1, 128), lambda i: (0, 0))),
        out_shape=(jax.ShapeDtypeStruct((s, d), F32), jax.ShapeDtypeStruct((s, d), BF16),
                   jax.ShapeDtypeStruct((1, 128), F32)),
        compiler_params=_cparams(("arbitrary",)),
    )(y, target)


def _split_dot(x, tri):
    hi = x.astype(BF16)
    lo = (x - hi.astype(F32)).astype(BF16)
    return (jnp.dot(hi, tri, preferred_element_type=F32) + jnp.dot(lo, tri, preferred_element_type=F32))


def _tile_iotas():
    row = lax.broadcasted_iota(jnp.int32, (TILE, TILE), 0)
    col = lax.broadcasted_iota(jnp.int32, (TILE, TILE), 1)
    return row, col


def _sb_logits(qi, kb, mask):
    z = lax.dot_general(qi, kb, NT_DIMS, preferred_element_type=F32) * (HEAD_DIM ** -0.5)
    sp = jnp.log1p(jnp.exp(-jnp.abs(z)))
    lb = jnp.minimum(z, 0.0) - sp
    l1m = -jnp.maximum(z, 0.0) - sp
    if mask is not None:
        l1m = jnp.where(mask, l1m, 0.0)
    return lb, l1m


def _attn_fwd(p, gq, gk, ga, n_heads):
    s = p.shape[0]
    nq = s // TILE

    hp = ATTN_HEADS_PER_STEP
    wd = hp * HEAD_DIM

    def body(q_ref, k_ref, v_ref, gq_ref, gk_ref, ga_ref, att_ref, o_ref, r_ref, qn_s, kn_s, vb_s):
        heads = [slice(hh * HEAD_DIM, (hh + 1) * HEAD_DIM) for hh in range(hp)]
        for hd in heads:
            qv = q_ref[:, hd]
            qn_s[:, hd] = (qv * _rstd(qv) * gq_ref[...]).astype(BF16)
            kv = k_ref[:, hd]
            kn_s[:, hd] = (kv * _rstd(kv) * gk_ref[...]).astype(BF16)
        vb_s[...] = v_ref[...].astype(BF16)
        row, col = _tile_iotas()
        causal = col < row
        upper_ones = jnp.concatenate([(row > col).astype(BF16), jnp.ones((TILE, TILE), BF16)], axis=1)

        def tiles(rows, keys, states, mask):
            logits = [_sb_logits(qn_s[rows, hd], kn_s[keys, hd], mask) for hd in heads]
            sums = [_split_dot(l1m, upper_ones) for _, l1m in logits]
            probs = []
            for (lb, _), sm, (_, c) in zip(logits, sums, states):
                a = jnp.exp(lb + sm[:, :TILE] + c)
                probs.append((a if mask is None else jnp.where(mask, a, 0.0)).astype(BF16))
            outs = [jnp.dot(a, vb_s[keys, hd], preferred_element_type=F32) for a, hd in zip(probs, heads)]
            return tuple((o_acc + o, c + sm[:, TILE:]) for (o_acc, c), o, sm in zip(states, outs, sums))

        def qblock(i, _):
            rows = pl.ds(pl.multiple_of(i * TILE, TILE), TILE)
            zero = jnp.zeros((TILE, HEAD_DIM), F32)
            states = tiles(rows, rows, tuple((zero, zero) for _ in heads), causal)

            def kblock(jj, states):
                keys = pl.ds(pl.multiple_of((i - 1 - jj) * TILE, TILE), TILE)
                return tiles(rows, keys, states, None)

            states = lax.fori_loop(0, i, kblock, states)
            for hh, (hd, (o_acc, c)) in enumerate(zip(heads, states)):
                o_ref[rows, hd] = o_acc
                r_ref[rows, hd] = c
                att_ref[rows, hd] = (o_acc * _rstd(o_acc) * ga_ref[hh]).astype(BF16)
            return 0

        lax.fori_loop(0, nq, qblock, 0)

    col_blk = lambda off: pl.BlockSpec((s, wd), lambda h: (0, off + h))
    vec = pl.BlockSpec((1, HEAD_DIM), lambda h: (0, 0))
    hvec = pl.BlockSpec((hp, 1, HEAD_DIM), lambda h: (h, 0, 0))
    out = pl.BlockSpec((s, wd), lambda h: (0, h))
    w = n_heads * HEAD_DIM
    steps = n_heads // hp
    return pl.pallas_call(
        body, name="attn_fwd", grid=(steps,),
        in_specs=[col_blk(0), col_blk(steps), col_blk(2 * steps), vec, vec, hvec],
        out_specs=(out, out, out),
        out_shape=(jax.ShapeDtypeStruct((s, w), BF16), jax.ShapeDtypeStruct((s, w), F32),
                   jax.ShapeDtypeStruct((s, w), F32)),
        scratch_shapes=[pltpu.VMEM((s, wd), BF16)] * 3,
        compiler_params=_cparams(("parallel",), ATTN_VMEM_MB),
    )(p, p, p, gq, gk, ga)


def _attn_bwd(p, gq, gk, ga, o, rsum, dmix, n_heads):
    s = p.shape[0]
    nq = s // TILE

    hp = ATTN_HEADS_PER_STEP
    wd = hp * HEAD_DIM
    scale = HEAD_DIM ** -0.5

    def body(q_ref, k_ref, v_ref, gq_ref, gk_ref, ga_ref, o_ref, r_ref, dm_ref,
             dq_ref, dk_ref, dv_ref, dgq_ref, dgk_ref, dga_ref,
             qn_s, kn_s, vb_s, do_s, dqn_s, dkn_s, dv_s):
        step = pl.program_id(0)
        gqv, gkv = gq_ref[...], gk_ref[...]
        heads = [slice(hh * HEAD_DIM, (hh + 1) * HEAD_DIM) for hh in range(hp)]
        for hh, hd in enumerate(heads):
            qv = q_ref[:, hd]
            qn_s[:, hd] = (qv * _rstd(qv) * gqv).astype(BF16)
            kv = k_ref[:, hd]
            kn_s[:, hd] = (kv * _rstd(kv) * gkv).astype(BF16)
            ov = o_ref[:, hd]
            ro = _rstd(ov)
            ohat = ov * ro
            dm = dm_ref[:, hd]
            dga_ref[hh] = jnp.sum(dm * ohat, axis=0, keepdims=True)
            do_s[:, hd] = _norm_bwd(dm, ohat, ro, ga_ref[hh]).astype(BF16)
        vb_s[...] = v_ref[...].astype(BF16)
        dkn_s[...] = jnp.zeros_like(dkn_s)
        dv_s[...] = jnp.zeros_like(dv_s)
        row, col = _tile_iotas()
        causal = col < row
        ones = jnp.ones((TILE, TILE), BF16)
        incl_ones = jnp.concatenate([(row <= col).astype(BF16), ones], axis=1)
        excl_ones = jnp.concatenate([(row < col).astype(BF16), ones], axis=1)

        def tiles(rows, keys, states, mask):
            qis = [qn_s[rows, hd] for hd in heads]
            dois = [do_s[rows, hd] for hd in heads]
            logits = [_sb_logits(qi, kn_s[keys, hd], mask) for qi, hd in zip(qis, heads)]
            sums = [_split_dot(l1m, incl_ones) for _, l1m in logits]
            das = [lax.dot_general(doi, vb_s[keys, hd], NT_DIMS, preferred_element_type=F32)
                   for doi, hd in zip(dois, heads)]
            probs, dss = [], []
            for (lb, _), sm, da, hd, (_, pfx, _) in zip(logits, sums, das, heads, states):
                a = jnp.exp(lb + (r_ref[rows, hd] - pfx - sm[:, :TILE]))
                a = a if mask is None else jnp.where(mask, a, 0.0)
                probs.append(a.astype(BF16))
                dss.append(da * a)
            dsums = [_split_dot(ds, excl_ones) for ds in dss]
            dzs = []
            for (lb, _), ds, dsm, (_, _, pc) in zip(logits, dss, dsums, states):
                dl1m = pc + dsm[:, :TILE]
                dl1m = dl1m if mask is None else jnp.where(mask, dl1m, 0.0)
                beta = jnp.exp(lb)
                dzs.append(((ds * (1.0 - beta) - dl1m * beta) * scale).astype(BF16))
            dqs = [jnp.dot(dz, kn_s[keys, hd], preferred_element_type=F32) for dz, hd in zip(dzs, heads)]
            for dz, qi, a, doi, hd in zip(dzs, qis, probs, dois, heads):
                dkn_s[keys, hd] += lax.dot_general(dz, qi, TN_DIMS, preferred_element_type=F32)
                dv_s[keys, hd] += lax.dot_general(a, doi, TN_DIMS, preferred_element_type=F32)
            return tuple((dq_acc + dq, pfx + sm[:, TILE:], pc + dsm[:, TILE:])
                         for (dq_acc, pfx, pc), dq, sm, dsm in zip(states, dqs, sums, dsums))

        def qblock(i, _):
            rows = pl.ds(pl.multiple_of(i * TILE, TILE), TILE)
            zero = jnp.zeros((TILE, HEAD_DIM), F32)

            def kblock(b, states):
                return tiles(rows, pl.ds(pl.multiple_of(b * TILE, TILE), TILE), states, None)

            states = lax.fori_loop(0, i, kblock, tuple((zero, zero, zero) for _ in heads))
            states = tiles(rows, rows, states, causal)
            for hd, (dq_acc, _, _) in zip(heads, states):
                dqn_s[rows, hd] = dq_acc
            return 0

        lax.fori_loop(0, nq, qblock, 0)

        def norm_in_bwd(x_ref, g, dn_s, dx_ref, dg_ref):
            part = jnp.zeros((1, HEAD_DIM), F32)
            for hd in heads:
                xv = x_ref[:, hd]
                r = _rstd(xv)
                xhat = xv * r
                dn = dn_s[:, hd]
                dx_ref[:, hd] = _norm_bwd(dn, xhat, r, g).astype(BF16)
                part = part + jnp.sum(dn * xhat, axis=0, keepdims=True)

            @pl.when(step == 0)
            def _():
                dg_ref[...] = part

            @pl.when(step > 0)
            def _():
                dg_ref[...] += part

        norm_in_bwd(q_ref, gqv, dqn_s, dq_ref, dgq_ref)
        norm_in_bwd(k_ref, gkv, dkn_s, dk_ref, dgk_ref)
        dv_ref[...] = dv_s[...].astype(BF16)

    once = pl.Buffered(1)
    steps = n_heads // hp
    col_blk = lambda off: pl.BlockSpec((s, wd), lambda h: (0, off + h), pipeline_mode=once)
    vec = pl.BlockSpec((1, HEAD_DIM), lambda h: (0, 0))
    hvec = pl.BlockSpec((hp, 1, HEAD_DIM), lambda h: (h, 0, 0))
    blk = pl.BlockSpec((s, wd), lambda h: (0, h), pipeline_mode=once)
    w = n_heads * HEAD_DIM
    big = jax.ShapeDtypeStruct((s, w), BF16)
    return pl.pallas_call(
        body, name="attn_bwd", grid=(steps,),
        in_specs=[col_blk(0), col_blk(steps), col_blk(2 * steps), vec, vec, hvec, blk, blk, blk],
        out_specs=(blk, blk, blk, vec, vec, hvec),
        out_shape=(big, big, big, jax.ShapeDtypeStruct((1, HEAD_DIM), F32), jax.ShapeDtypeStruct((1, HEAD_DIM), F32),
                   jax.ShapeDtypeStruct((n_heads, 1, HEAD_DIM), F32)),
        scratch_shapes=[pltpu.VMEM((s, wd), BF16)] * 4 + [pltpu.VMEM((s, wd), F32)] * 3,
        compiler_params=_cparams(("arbitrary",), ATTN_VMEM_MB),
    )(p, p, p, gq, gk, ga, o, rsum, dmix)


_INV_SQRT2 = 0.7071067811865476
_INV_SQRT_2PI = 0.3989422804014327


def _gelu(x):
    return 0.5 * x * (1.0 + lax.erf(x * _INV_SQRT2))


def _gelu_grad(x):
    return 0.5 * (1.0 + lax.erf(x * _INV_SQRT2)) + x * (_INV_SQRT_2PI * jnp.exp(-0.5 * x * x))


def _sgu_fwd(p, gs, w_s, b_s, gb, col0):
    s = p.shape[0]
    n_chunks = s // TILE

    def body(u_ref, v_ref, gs_ref, w_ref, b_ref, gb_ref, out_ref, vs_s):
        vg = _gelu(v_ref[...])
        vs_s[...] = (vg * _rstd(vg) * gs_ref[...]).astype(BF16)
        row, col = _tile_iotas()
        wt = jnp.where(col <= row, w_ref[...], 0.0).astype(BF16)
        bcol = b_ref[...]
        gbv = gb_ref[...]

        def chunk(c, _):
            rows = pl.ds(pl.multiple_of(c * TILE, TILE), TILE)
            mixed = jnp.dot(wt, vs_s[rows, :], preferred_element_type=F32) + bcol
            sg = _gelu(u_ref[rows, :]) * mixed
            out_ref[rows, :] = (sg * _rstd(sg) * gbv).astype(BF16)
            return 0

        lax.fori_loop(0, n_chunks, chunk, 0)

    col_blk = lambda off: pl.BlockSpec((s, HEAD_DIM), lambda g: (0, off + g))
    gvec = pl.BlockSpec((None, 1, HEAD_DIM), lambda g: (g, 0, 0))
    return pl.pallas_call(
        body, name="sgu_fwd", grid=(N_GROUPS,),
        in_specs=[col_blk(col0), col_blk(col0 + N_GROUPS), gvec,
                  pl.BlockSpec((None, TILE, TILE), lambda g: (g, 0, 0)),
                  pl.BlockSpec((None, TILE, 1), lambda g: (g, 0, 0)), gvec],
        out_specs=pl.BlockSpec((s, HEAD_DIM), lambda g: (0, g)),
        out_shape=jax.ShapeDtypeStruct((s, N_GROUPS * HEAD_DIM), BF16),
        scratch_shapes=[pltpu.VMEM((s, HEAD_DIM), BF16)],
        compiler_params=_cparams(("parallel",)),
    )(p, p, gs, w_s, b_s, gb)


def _sgu_bwd(p, gs, w_s, b_s, gb, dmix, col0, dm_col0):
    s = p.shape[0]
    n_chunks = s // TILE

    def body(u_ref, v_ref, gs_ref, w_ref, b_ref, gb_ref, dm_ref,
             du_ref, dv_ref, dgs_ref, dw_ref, db_ref, dgb_ref, vs_s, dvs_s):
        gsv = gs_ref[...]
        gbv = gb_ref[...]
        vg = _gelu(v_ref[...])
        vs_s[...] = (vg * _rstd(vg) * gsv).astype(BF16)
        row, col = _tile_iotas()
        causal = col <= row
        wt = jnp.where(causal, w_ref[...], 0.0).astype(BF16)
        bcol = b_ref[...]

        def chunk(c, carry):
            dw_acc, db_acc, dgb_acc = carry
            rows = pl.ds(pl.multiple_of(c * TILE, TILE), TILE)
            vs = vs_s[rows, :]
            mixed = jnp.dot(wt, vs, preferred_element_type=F32) + bcol
            u_pre = u_ref[rows, :]
            u = _gelu(u_pre)
            sg = u * mixed
            rs = _rstd(sg)
            sghat = sg * rs
            dm = dm_ref[rows, :]
            dsg = _norm_bwd(dm, sghat, rs, gbv)
            dgb_acc = dgb_acc + jnp.sum(dm * sghat, axis=0, keepdims=True)
            du_ref[rows, :] = (dsg * mixed * _gelu_grad(u_pre)).astype(BF16)
            dmixed = dsg * u
            db_acc = db_acc + jnp.sum(dmixed, axis=1, keepdims=True)
            dmb = dmixed.astype(BF16)
            dw_acc = dw_acc + lax.dot_general(dmb, vs, NT_DIMS, preferred_element_type=F32)
            dvs_s[rows, :] = lax.dot_general(wt, dmb, TN_DIMS, preferred_element_type=F32)
            return dw_acc, db_acc, dgb_acc

        dw_acc, db_acc, dgb_acc = lax.fori_loop(
            0, n_chunks, chunk,
            (jnp.zeros((TILE, TILE), F32), jnp.zeros((TILE, 1), F32), jnp.zeros((1, HEAD_DIM), F32)))
        dw_ref[...] = jnp.where(causal, dw_acc, 0.0)
        db_ref[...] = db_acc
        dgb_ref[...] = dgb_acc
        v_pre = v_ref[...]
        vg = _gelu(v_pre)
        rv = _rstd(vg)
        vhat = vg * rv
        dvs = dvs_s[...]
        dgs_ref[...] = jnp.sum(dvs * vhat, axis=0, keepdims=True)
        dv_ref[...] = (_norm_bwd(dvs, vhat, rv, gsv) * _gelu_grad(v_pre)).astype(BF16)

    col_blk = lambda off: pl.BlockSpec((s, HEAD_DIM), lambda g: (0, off + g))
    gvec = pl.BlockSpec((None, 1, HEAD_DIM), lambda g: (g, 0, 0))
    wspec = pl.BlockSpec((None, TILE, TILE), lambda g: (g, 0, 0))
    bspec = pl.BlockSpec((None, TILE, 1), lambda g: (g, 0, 0))
    blk = pl.BlockSpec((s, HEAD_DIM), lambda g: (0, g))
    big = jax.ShapeDtypeStruct((s, N_GROUPS * HEAD_DIM), BF16)
    gshape = jax.ShapeDtypeStruct((N_GROUPS, 1, HEAD_DIM), F32)
    return pl.pallas_call(
        body, name="sgu_bwd", grid=(N_GROUPS,),
        in_specs=[col_blk(col0), col_blk(col0 + N_GROUPS), gvec, wspec, bspec, gvec, col_blk(dm_col0)],
        out_specs=(blk, blk, gvec, wspec, bspec, gvec),
        out_shape=(big, big, gshape, jax.ShapeDtypeStruct((N_GROUPS, TILE, TILE), F32),
                   jax.ShapeDtypeStruct((N_GROUPS, TILE, 1), F32), gshape),
        scratch_shapes=[pltpu.VMEM((s, HEAD_DIM), BF16), pltpu.VMEM((s, HEAD_DIM), F32)],
        compiler_params=_cparams(("parallel",)),
    )(p, p, gs, w_s, b_s, gb, dmix)


def _shift_down(x, n):
    rows = lax.broadcasted_iota(jnp.int32, x.shape, 0)
    return jnp.where(rows >= n, pltpu.roll(x, n, 0), 0.0)


def _shift_up(x, n):
    s = x.shape[0]
    rows = lax.broadcasted_iota(jnp.int32, x.shape, 0)
    return jnp.where(rows < s - n, pltpu.roll(x, s - n, 0), 0.0)


def _conv(x, w, b):
    return b + w[0:1, :] * _shift_down(x, 2) + w[1:2, :] * _shift_down(x, 1) + w[2:3, :] * x


def _conv_specs(s, tn):
    xspec = pl.BlockSpec((2, s, tn), lambda j: (0, 0, j))
    wspec = pl.BlockSpec((2, CONV_WIDTH, tn), lambda j: (0, 0, j))
    bspec = pl.BlockSpec((2, 1, tn), lambda j: (0, 0, j))
    return xspec, wspec, bspec


def _conv_gate_fwd(up, cw, cb):
    _, s, f = up.shape
    tn = _pick(f, (256, 128))

    def body(x_ref, w_ref, b_ref, act_ref):
        gate = _conv(x_ref[0], w_ref[0], b_ref[0])
        val = _conv(x_ref[1], w_ref[1], b_ref[1])
        act_ref[...] = (gate * jax.nn.sigmoid(gate) * val).astype(BF16)

    xspec, wspec, bspec = _conv_specs(s, tn)
    return pl.pallas_call(
        body, name="conv_gate_fwd", grid=(f // tn,), in_specs=[xspec, wspec, bspec],
        out_specs=pl.BlockSpec((s, tn), lambda j: (0, j)), out_shape=jax.ShapeDtypeStruct((s, f), BF16),
        compiler_params=_cparams(("parallel",)),
    )(up, cw, cb)


def _conv_gate_bwd(up, cw, cb, dact):
    _, s, f = up.shape
    tn = _pick(f, (256, 128))

    def body(x_ref, w_ref, b_ref, da_ref, dx_ref, dw_ref, db_ref):
        xg, xv = x_ref[0], x_ref[1]
        wg, wv = w_ref[0], w_ref[1]
        gate = _conv(xg, wg, b_ref[0])
        val = _conv(xv, wv, b_ref[1])
        sig = jax.nn.sigmoid(gate)
        da = da_ref[...]
        dval = da * (gate * sig)
        dgate = da * val * (sig * (1.0 + gate * (1.0 - sig)))
        for half, (x, w, dy) in enumerate(((xg, wg, dgate), (xv, wv, dval))):
            dx_ref[half] = (w[2:3, :] * dy + w[1:2, :] * _shift_up(dy, 1) + w[0:1, :] * _shift_up(dy, 2)).astype(BF16)
            dw_ref[half, 0:1, :] = jnp.sum(dy * _shift_down(x, 2), axis=0, keepdims=True)
            dw_ref[half, 1:2, :] = jnp.sum(dy * _shift_down(x, 1), axis=0, keepdims=True)
            dw_ref[half, 2:3, :] = jnp.sum(dy * x, axis=0, keepdims=True)
            db_ref[half] = jnp.sum(dy, axis=0, keepdims=True)

    xspec, wspec, bspec = _conv_specs(s, tn)
    return pl.pallas_call(
        body, name="conv_gate_bwd", grid=(f // tn,),
        in_specs=[xspec, wspec, bspec, pl.BlockSpec((s, tn), lambda j: (0, j))],
        out_specs=(xspec, wspec, bspec),
        out_shape=(jax.ShapeDtypeStruct((2, s, f), BF16), jax.ShapeDtypeStruct((2, CONV_WIDTH, f), F32),
                   jax.ShapeDtypeStruct((2, 1, f), F32)),
        compiler_params=_cparams(("parallel",)),
    )(up, cw, cb, dact)


def _mesh_pos():
    return lax.axis_index("x"), lax.axis_index("y"), lax.axis_index("c")


def _remote(src, dst, send_sem, recv_sem, to):
    return pltpu.make_async_remote_copy(src_ref=src, dst_ref=dst, send_sem=send_sem, recv_sem=recv_sem,
                                        device_id=to, device_id_type=pl.DeviceIdType.MESH)


def _all_gather(shards):
    n = len(shards)

    def body(*refs):
        ins, outs = refs[:n], refs[n:2 * n]
        send_sems, recv_sems, local_sems = refs[2 * n:]
        x, y, c = _mesh_pos()
        me, sibling = (x, y, c), (x, y, 1 - c)
        chips = [(1 - x, y), (x, 1 - y), (1 - x, 1 - y)]
        slot = lambda pos: 4 * pos[0] + 2 * pos[1] + pos[2]

        def copy(a, k, block, to, src=None):
            dst = outs[a].at[slot(block)]
            return _remote(dst if src is None else src, dst, send_sems.at[a, k], recv_sems.at[a, k], to)

        started = []
        for a in range(n):
            mine = pltpu.make_async_copy(ins[a], outs[a].at[slot(me)], local_sems.at[a])
            mine.start()
            started.append(mine)
        sends = []
        for a in range(n):
            first = [copy(a, 0, me, sibling, src=ins[a])]
            first += [copy(a, 1 + j, me, (*chip, c), src=ins[a]) for j, chip in enumerate(chips)]
            for cp in first:
                cp.start()
            sends += first
        for a in range(n):
            for j, chip in enumerate(chips):
                copy(a, 1 + j, (*chip, c), me).wait_recv()
                fwd = copy(a, 4 + j, (*chip, c), sibling)
                fwd.start()
                sends.append(fwd)
        for a in range(n):
            copy(a, 0, sibling, me).wait_recv()
            for j, chip in enumerate(chips):
                copy(a, 4 + j, (*chip, 1 - c), me).wait_recv()
        for cp in sends:
            cp.wait_send()
        for mine in started:
            mine.wait()

    anyspec = pl.BlockSpec(memory_space=pl.ANY)
    return pl.pallas_call(
        body, name="weight_all_gather",
        in_specs=[anyspec] * n, out_specs=tuple([anyspec] * n),
        out_shape=tuple(jax.ShapeDtypeStruct((N_DEV,) + w.shape, w.dtype) for w in shards),
        scratch_shapes=[pltpu.SemaphoreType.DMA((n, 7)), pltpu.SemaphoreType.DMA((n, 7)),
                        pltpu.SemaphoreType.DMA((n,))],
    )(*shards)


def _exchange_blocks(blocked):
    n = len(blocked)

    def body(*refs):
        ins, outs = refs[:n], refs[n:2 * n]
        send_sems, recv_sems, local_sems = refs[2 * n:]
        x, y, c = _mesh_pos()
        my_slot = 4 * x + 2 * y + c
        peers = [(x ^ ((k >> 2) & 1), y ^ ((k >> 1) & 1), c ^ (k & 1)) for k in range(1, N_DEV)]
        copies = []
        for a in range(n):
            mine = pltpu.make_async_copy(ins[a].at[my_slot], outs[a].at[my_slot], local_sems.at[a])
            mine.start()
            copies.append(mine)
        sends = []
        for a in range(n):
            for k, peer in enumerate(peers):
                peer_slot = 4 * peer[0] + 2 * peer[1] + peer[2]
                cp = _remote(ins[a].at[peer_slot], outs[a].at[my_slot], send_sems.at[a, k], recv_sems.at[a, k], peer)
                cp.start()
                sends.append((cp, a, k, peer_slot))
        for cp, a, k, peer_slot in sends:
            _remote(ins[a].at[peer_slot], outs[a].at[peer_slot], send_sems.at[a, k], recv_sems.at[a, k],
                    peers[k]).wait_recv()
        for cp, _, _, _ in sends:
            cp.wait_send()
        for mine in copies:
            mine.wait()

    anyspec = pl.BlockSpec(memory_space=pl.ANY)
    return pl.pallas_call(
        body, name="grad_exchange",
        in_specs=[anyspec] * n, out_specs=tuple([anyspec] * n),
        out_shape=tuple(jax.ShapeDtypeStruct(w.shape, w.dtype) for w in blocked),
        scratch_shapes=[pltpu.SemaphoreType.DMA((n, 7)), pltpu.SemaphoreType.DMA((n, 7)),
                        pltpu.SemaphoreType.DMA((n,))],
    )(*blocked)


HBM_SPEC = pl.BlockSpec(memory_space=pltpu.HBM)
SEM_SPEC = pl.BlockSpec(memory_space=pltpu.SEMAPHORE)
ANY_SPEC = pl.BlockSpec(memory_space=pl.ANY)
TOKEN_SPEC = pl.BlockSpec(memory_space=pltpu.VMEM)
TOKEN_SHAPE = jax.ShapeDtypeStruct((8, 128), F32)
DATAFLOW = pltpu.SideEffectType.DATAFLOW_SIDE_EFFECTING
GATHER_PLANE = (2, 4, 6)


def _slot(pos):
    return 4 * pos[0] + 2 * pos[1] + pos[2]


def _flip(pos, k):
    return (pos[0] ^ ((k >> 2) & 1), pos[1] ^ ((k >> 1) & 1), pos[2] ^ (k & 1))


def _hbm(a):
    return pltpu.with_memory_space_constraint(a, pltpu.HBM)


def _hbm_shapes(arrays):
    return tuple(pltpu.HBM(a.shape, a.dtype) for a in arrays)


class _Split:
    def __init__(self, n, outs, has_sems):
        k = 2 * n if has_sems else 0
        self.n = n
        self.sems = list(outs[:k])
        self.bufs = list(outs[k:k + 2 * n])
        self.token = outs[-1]


def _split_call(name, body, bufs, sems_in, makes_sems, after):
    n = len(bufs) // 2
    k = 2 * n if makes_sems else 0
    m = len(sems_in)

    def wrapped(*refs):
        srcs, dsts = refs[:n], refs[n:2 * n]
        s_in = refs[2 * n:2 * n + m]
        s_out = refs[2 * n + m + 1:2 * n + m + 1 + k]
        token, local_sems = refs[-2], refs[-1]
        body(srcs, dsts, s_in, s_out, local_sems)
        token[...] = jnp.zeros_like(token)

    outs = pl.pallas_call(
        wrapped, name=name,
        out_shape=(pltpu.SemaphoreType.DMA(()),) * k + _hbm_shapes(bufs) + (TOKEN_SHAPE,),
        in_specs=[HBM_SPEC] * (2 * n) + [SEM_SPEC] * m + [ANY_SPEC],
        out_specs=(SEM_SPEC,) * k + (HBM_SPEC,) * (2 * n) + (TOKEN_SPEC,),
        input_output_aliases={i: k + i for i in range(2 * n)},
        scratch_shapes=[pltpu.SemaphoreType.DMA((n,))],
        compiler_params=pltpu.CompilerParams(has_side_effects=DATAFLOW),
    )(*[_hbm(b) for b in bufs], *sems_in, after)
    return _Split(n, outs, makes_sems)


def _wait_slots(land, count, send_sem, recv_sem, me, send=False, recv=False):
    span = land.at[pl.ds(0, count)]
    cp = _remote(span, span, send_sem, recv_sem, me)
    if send:
        cp.wait_send()
    if recv:
        cp.wait_recv()


def _gather_start(name, shards, after):
    n = len(shards)
    my_slot = _slot(_mesh_pos())
    lands = [lax.dynamic_update_slice(lax.empty((N_DEV,) + w.shape, w.dtype), w[None], (my_slot, 0, 0)) for w in shards]

    def body(srcs, dsts, _, sems, local_sems):
        me = _mesh_pos()
        for a in range(n):
            for k in (1,) + GATHER_PLANE:
                _remote(srcs[a], dsts[a].at[_slot(me)], sems[a], sems[n + a], _flip(me, k)).start()

    return _split_call(name, body, list(shards) + lands, [], True, after)


def _gather_forward(name, started, after):
    n = started.n

    def body(srcs, dsts, sems_a, sems_b, local_sems):
        me = _mesh_pos()
        sibling = _flip(me, 1)
        for a in range(n):
            _wait_slots(dsts[a], 4, sems_a[a], sems_a[n + a], me, recv=True)
            for k in GATHER_PLANE:
                block = dsts[a].at[_slot(_flip(me, k))]
                _remote(block, block, sems_b[a], sems_b[n + a], sibling).start()
        for a in range(n):
            _wait_slots(dsts[a], 4, sems_a[a], sems_a[n + a], me, send=True)

    return _split_call(name, body, started.bufs, started.sems, True, after)


def _gather_finish(name, forwarded, after):
    n = forwarded.n

    def body(srcs, dsts, sems_b, _, local_sems):
        me = _mesh_pos()
        for a in range(n):
            _wait_slots(dsts[a], 3, sems_b[a], sems_b[n + a], me, send=True, recv=True)

    return _split_call(name, body, forwarded.bufs, forwarded.sems, False, after).bufs[n:]


def _exchange_start(name, blocked, after):
    n = len(blocked)
    my_slot = _slot(_mesh_pos())
    lands = [lax.dynamic_update_slice(lax.empty(w.shape, w.dtype), lax.dynamic_slice_in_dim(w, my_slot, 1, 0),
                                      (my_slot, 0, 0)) for w in blocked]

    def body(srcs, dsts, _, sems, local_sems):
        me = _mesh_pos()
        for a in range(n):
            for k in range(1, N_DEV):
                peer = _flip(me, k)
                _remote(srcs[a].at[_slot(peer)], dsts[a].at[_slot(me)], sems[a], sems[n + a], peer).start()

    return _split_call(name, body, list(blocked) + lands, [], True, after)


def _exchange_finish(name, started, after):
    n = started.n

    def body(srcs, dsts, sems, _, local_sems):
        me = _mesh_pos()
        for a in range(n):
            _wait_slots(dsts[a], N_DEV - 1, sems[a], sems[n + a], me, send=True, recv=True)

    return _split_call(name, body, started.bufs, started.sems, False, after).bufs[n:]


def _adamw_math(w, g, m, v):
    m = ADAM_B1 * m + (1.0 - ADAM_B1) * g
    v = ADAM_B2 * v + (1.0 - ADAM_B2) * (g * g)
    m_hat = m / (1.0 - ADAM_B1 ** ADAM_STEP)
    v_hat = v / (1.0 - ADAM_B2 ** ADAM_STEP)
    delta = -ADAM_LR * (m_hat / (jnp.sqrt(v_hat) + ADAM_EPS) + ADAM_WD * w)
    return delta, m, v


def _adamw(name, w, m, v, parts, layer, prev=None):
    _, r, c = w.shape
    tr = _pick(r, tuple(t for t in (256, 128, 64, 32, 16) if t * c <= ADAMW_TILE_ELEMS))
    n_prev = 0 if prev is None else 4

    def body(*refs):
        w_ref, m_ref, v_ref, p_ref = refs[:4]
        g_ref, d_ref, nm_ref, nv_ref = refs[4 + n_prev:]
        g = p_ref[0].astype(F32)
        for src in range(1, N_DEV):
            g = g + p_ref[src].astype(F32)
        delta, nm, nv = _adamw_math(w_ref[...], g, m_ref[...], v_ref[...])
        g_ref[...] = g
        d_ref[...] = delta
        nm_ref[...] = nm
        nv_ref[...] = nv

    wspec = pl.BlockSpec((None, tr, c), lambda i: (layer, i, 0))
    pspec = pl.BlockSpec((N_DEV, tr, c), lambda i: (0, i, 0))
    shp = jax.ShapeDtypeStruct(w.shape, F32)
    return pl.pallas_call(
        body, name=name, grid=(r // tr,), in_specs=[wspec] * 3 + [pspec] + [ANY_SPEC] * n_prev,
        out_specs=(wspec,) * 4, out_shape=(shp,) * 4, input_output_aliases={4 + j: j for j in range(n_prev)},
        compiler_params=_cparams(("parallel",)),
    )(w, m, v, parts, *([] if prev is None else prev))


PACK_TILE = 8 * 128


def _pack(arrays):
    flat = []
    for a in arrays:
        v = a.reshape(-1)
        pad = (-v.shape[0]) % PACK_TILE
        flat.append(jnp.pad(v, (0, pad)) if pad else v)
    return jnp.concatenate(flat).reshape(-1, 128)


def _unpack(buf, like):
    flat = buf.reshape(-1)
    out, off = [], 0
    for a in like:
        n = 1
        for dim in a.shape:
            n *= dim
        out.append(flat[off:off + n].reshape(a.shape))
        off += n + (-n) % PACK_TILE
    return out


def _sum_slots(gathered):
    _, r, c = gathered.shape

    def body(x_ref, o_ref):
        acc = x_ref[0]
        for src in range(1, N_DEV):
            acc = acc + x_ref[src]
        o_ref[...] = acc

    return pl.pallas_call(body, name="small_grad_sum", out_shape=jax.ShapeDtypeStruct((r, c), F32))(gathered)


def _adamw_small(w, g, m, v):
    shp = jax.ShapeDtypeStruct(w.shape, F32)

    def body(w_ref, g_ref, m_ref, v_ref, d_ref, nm_ref, nv_ref):
        delta, nm, nv = _adamw_math(w_ref[...], g_ref[...], m_ref[...], v_ref[...])
        d_ref[...] = delta
        nm_ref[...] = nm
        nv_ref[...] = nv

    return pl.pallas_call(body, name="adamw_small", out_shape=(shp,) * 3)(w, g, m, v)


def kernel(x, attn_norm_g, w_in, q_norm_g, k_norm_g, sgu_norm_g, sgu_w, sgu_b, out_norm_a_g, out_norm_b_g, w_out, ffn_norm_g, w_up, conv_w, conv_b, w_down, loss_target, m_attn_norm_g, m_w_in, m_q_norm_g, m_k_norm_g, m_sgu_norm_g, m_sgu_w, m_sgu_b, m_out_norm_a_g, m_out_norm_b_g, m_w_out, m_ffn_norm_g, m_w_up, m_conv_w, m_conv_b, m_w_down, v_attn_norm_g, v_w_in, v_q_norm_g, v_k_norm_g, v_sgu_norm_g, v_sgu_w, v_sgu_b, v_out_norm_a_g, v_out_norm_b_g, v_w_out, v_ffn_norm_g, v_w_up, v_conv_w, v_conv_b, v_w_down):
    depth = w_in.shape[0]
    s, d = x.shape[1], x.shape[2]
    n_heads = (d // 2) // HEAD_DIM
    sgu_col0 = 3 * n_heads
    f2 = w_up.shape[2] * N_DEV
    ff = f2 // 2
    my_slot = 4 * lax.axis_index("x") + 2 * lax.axis_index("y") + lax.axis_index("c")

    wb = [(w_in[l].astype(BF16), w_out[l].astype(BF16), w_up[l].astype(BF16), w_down[l].astype(BF16))
          for l in range(depth)]
    groups = [[wb[0][0]], [wb[0][1], wb[0][2], wb[0][3], conv_w.reshape(depth * CONV_WIDTH, -1)]]
    groups += [list(wb[l]) for l in range(1, depth)]
    token = attn_norm_g
    started = []
    for i, group in enumerate(groups):
        started.append(_gather_start(f"gather{i}_start", group, token))
        token = started[-1].token

    conv_b_all = conv_b.reshape(depth, 2, 1, ff)
    sgu_b_col = sgu_b[..., None]
    gathered = [None] * depth
    forwarded = _gather_forward("gather0_forward", started[0], token)
    win0 = _gather_finish("gather0_finish", forwarded, forwarded.token)[0]

    xs = x[0]
    saved = []
    for l in range(depth):
        g1 = attn_norm_g[l][None]
        g2 = ffn_norm_g[l][None]
        gq, gk = q_norm_g[l][None], k_norm_g[l][None]
        ga = out_norm_a_g[l][:, None, :]
        gs = sgu_norm_g[l][:, None, :]
        gb = out_norm_b_g[l][:, None, :]
        if l == 0:
            win_g = win0
        else:
            win_g = gathered[l][0]
        h1 = _rmsnorm_fwd("attn_norm_fwd", xs, g1)
        p = _mm_nn_blocked("in_proj", h1, win_g, F32)
        att, o, rsum = _attn_fwd(p, gq, gk, ga, n_heads)
        if l == 0:
            forwarded = _gather_forward("gather1_forward", started[1], att)
        sg = _sgu_fwd(p, gs, sgu_w[l], sgu_b_col[l], gb, sgu_col0)
        mix = jnp.concatenate([att, sg], axis=-1)
        if l == 0:
            rest = _gather_finish("gather1_finish", forwarded, mix)
            gathered[0] = (win0, rest[0], rest[1], rest[2])
            conv_w_all = rest[3].reshape(N_DEV, depth, CONV_WIDTH, -1)
            conv_w_all = jnp.transpose(conv_w_all, (1, 2, 0, 3)).reshape(depth, CONV_WIDTH, 2, ff)
            conv_w_all = jnp.transpose(conv_w_all, (0, 2, 1, 3))
        _, wout_g, wup_g, wdown_g = gathered[l]
        wout_full = wout_g.reshape(d, d)
        wdown_full = wdown_g.reshape(ff, d)
        x1 = _mm_nn_res("out_proj", mix, wout_full, xs)
        h2 = _rmsnorm_fwd("ffn_norm_fwd", x1, g2)
        up = _mm_nn_blocked("up_proj", h2, wup_g, F32, halves=True)
        act = _conv_gate_fwd(up, conv_w_all[l], conv_b_all[l])
        if l + 1 < depth:
            forwarded = _gather_forward(f"gather{l + 2}_forward", started[l + 2], act)
        x2 = _mm_nn_res("down_proj", act, wdown_full, x1)
        if l + 1 < depth:
            gathered[l + 1] = tuple(_gather_finish(f"gather{l + 2}_finish", forwarded, x2))
        saved.append((xs, h1, p, o, rsum, mix, x1, h2, up, act))
        xs = x2

    dx, dxb, loss_vec = _loss_head(xs, loss_target[0])
    loss = lax.psum(loss_vec[0, 0], MESH_AXES)

    exchanges = []
    small = [None] * depth
    for l in reversed(range(depth)):
        xs0, h1, p, o, rsum, mix, x1, h2, up, act = saved[l]
        win_g, wout_g, wup_g, wdown_g = gathered[l]
        wout_full = wout_g.reshape(d, d)
        wdown_full = wdown_g.reshape(ff, d)
        g1 = attn_norm_g[l][None]
        g2 = ffn_norm_g[l][None]
        gq, gk = q_norm_g[l][None], k_norm_g[l][None]
        ga = out_norm_a_g[l][:, None, :]
        gs = sgu_norm_g[l][:, None, :]
        gb = out_norm_b_g[l][:, None, :]
        d_wdown = _mm_tn_plain("down_proj_dw", act, dxb)
        dact = _mm_nt_plain("down_proj_dx", dxb, wdown_full)
        dup, d_cw, d_cb = _conv_gate_bwd(up, conv_w_all[l], conv_b_all[l], dact)
        d_wup = _mm_tn_blocked("up_proj_dw", h2, dup, N_DEV, halves=True)
        exchanges.append((l, ("w_up", "w_down"), _exchange_start(
            f"grad_ffn{l}_start", [d_wup, d_wdown.reshape(N_DEV, ff // N_DEV, d)], d_wup)))
        dh2 = _mm_nt_blocked("up_proj_dx", dup, wup_g, halves=True)
        dx, dxb, d_g2 = _rmsnorm_bwd("ffn_norm_bwd", dh2, x1, g2 + exchanges[-1][2].token[0:1, 0:1], dx)
        d_wout = _mm_tn_plain("out_proj_dw", mix, dxb)
        dmix = _mm_nt_plain("out_proj_dx", dxb, wout_full)
        dq, dk, dv, d_gq, d_gk, d_ga = _attn_bwd(p, gq, gk, ga, o, rsum, dmix, n_heads)
        du, dvs, d_gs, d_sw, d_sb, d_gb = _sgu_bwd(p, gs, sgu_w[l], sgu_b_col[l], gb, dmix, sgu_col0, n_heads)
        dp = jnp.concatenate([dq, dk, dv, du, dvs], axis=-1)
        d_win = _mm_tn_blocked("in_proj_dw", h1, dp, N_DEV)
        exchanges.append((l, ("w_in", "w_out"), _exchange_start(
            f"grad_mix{l}_start", [d_win, d_wout.reshape(N_DEV, d // N_DEV, d)], d_win)))
        dh1 = _mm_nt_blocked("in_proj_dx", dp, win_g)
        dx, dxb, d_g1 = _rmsnorm_bwd("attn_norm_bwd", dh1, xs0, g1 + exchanges[-1][2].token[0:1, 0:1], dx)
        small[l] = dict(attn_norm_g=d_g1[0], q_norm_g=d_gq[0], k_norm_g=d_gk[0], sgu_norm_g=d_gs[:, 0], sgu_w=d_sw,
                        sgu_b=d_sb[..., 0], out_norm_a_g=d_ga[:, 0], out_norm_b_g=d_gb[:, 0], ffn_norm_g=d_g2[0],
                        conv_w=jnp.transpose(d_cw, (1, 0, 2)).reshape(CONV_WIDTH, f2), conv_b=d_cb.reshape(f2))
    grad_x = dx[None]

    small_names = ["attn_norm_g", "q_norm_g", "k_norm_g", "sgu_norm_g", "sgu_w", "sgu_b", "out_norm_a_g",
                   "out_norm_b_g", "ffn_norm_g", "conv_b", "conv_w"]
    small_g = [jnp.stack([small[l][n] for l in range(depth)]) for n in small_names]
    packed = _pack(small_g)
    small_all = _all_gather([packed])[0]
    small_sum = _unpack(_sum_slots(small_all), small_g)
    g_small = dict(zip(small_names, small_sum))
    cwn = conv_w.shape[2]
    g_small["conv_w"] = lax.dynamic_slice_in_dim(g_small["conv_w"], my_slot * cwn, cwn, axis=2)

    res = {}
    big = dict(w_in=(w_in, m_w_in, v_w_in), w_out=(w_out, m_w_out, v_w_out), w_up=(w_up, m_w_up, v_w_up),
               w_down=(w_down, m_w_down, v_w_down))
    after = dx
    for l, names, ex in exchanges:
        stage = "ffn" if names[0] == "w_up" else "mix"
        landed = _exchange_finish(f"grad_{stage}{l}_finish", ex, after)
        for name, parts in zip(names, landed):
            w, m, v = big[name]
            res[name] = _adamw(f"adamw_{name}", w, m, v, parts, l, res.get(name))
            after = res[name][0]
    small_w = dict(attn_norm_g=(attn_norm_g, m_attn_norm_g, v_attn_norm_g), q_norm_g=(q_norm_g, m_q_norm_g, v_q_norm_g),
                   k_norm_g=(k_norm_g, m_k_norm_g, v_k_norm_g), sgu_norm_g=(sgu_norm_g, m_sgu_norm_g, v_sgu_norm_g),
                   sgu_w=(sgu_w, m_sgu_w, v_sgu_w), sgu_b=(sgu_b, m_sgu_b, v_sgu_b),
                   out_norm_a_g=(out_norm_a_g, m_out_norm_a_g, v_out_norm_a_g),
                   out_norm_b_g=(out_norm_b_g, m_out_norm_b_g, v_out_norm_b_g),
                   ffn_norm_g=(ffn_norm_g, m_ffn_norm_g, v_ffn_norm_g), conv_b=(conv_b, m_conv_b, v_conv_b),
                   conv_w=(conv_w, m_conv_w, v_conv_w))
    like = [small_w[n][0] for n in small_names]
    pw = _pack([small_w[n][0] for n in small_names])
    pm = _pack([small_w[n][1] for n in small_names])
    pv = _pack([small_w[n][2] for n in small_names])
    pg = _pack([g_small[n].reshape(small_w[n][0].shape) for n in small_names])
    pd, pnm, pnv = _adamw_small(pw, pg, pm, pv)
    for n, dlt, nm, nv in zip(small_names, _unpack(pd, like), _unpack(pnm, like), _unpack(pnv, like)):
        res[n] = (g_small[n].reshape(small_w[n][0].shape), dlt, nm, nv)

    order = ["attn_norm_g", "w_in", "q_norm_g", "k_norm_g", "sgu_norm_g", "sgu_w", "sgu_b", "out_norm_a_g",
             "out_norm_b_g", "w_out", "ffn_norm_g", "w_up", "conv_w", "conv_b", "w_down"]
    outs = [loss, grad_x]
    for field in range(4):
        outs += [res[n][field] for n in order]
    return tuple(outs)
```

```python
import functools

import jax
import jax.numpy as jnp
from jax import lax
from jax.experimental import pallas as pl
from jax.experimental.pallas import tpu as pltpu

F32 = jnp.float32
BF16 = jnp.bfloat16
EPS = 1e-6
HEAD_DIM = 128
TILE = 128
ATTN_VMEM_MB = 58
ATTN_HEADS_PER_STEP = 4
N_GROUPS = 8
CONV_WIDTH = 3
N_DEV = 8
MESH_AXES = ("x", "y", "c")
MIB = 1024 * 1024

ADAM_LR = 0.001
ADAM_B1 = 0.9
ADAM_B2 = 0.999
ADAM_EPS = 1e-08
ADAM_WD = 0.01
ADAM_STEP = 10
ADAMW_TILE_ELEMS = 160 * 1024

NT_DIMS = (((1,), (1,)), ((), ()))
NN_DIMS = (((1,), (0,)), ((), ()))
TN_DIMS = (((0,), (0,)), ((), ()))


def _cparams(sem, vmem_mb=48):
    return pltpu.CompilerParams(dimension_semantics=sem, vmem_limit_bytes=vmem_mb * MIB)


def _pick(n, cands):
    for c in cands:
        if n % c == 0:
            return c
    return n


def _mm(name, grid, ins, in_specs, out_shape, out_spec, dims, has_res=False, parts=None, vmem_mb=56, after=None):
    n_in = 2 + has_res + (after is not None)
    if after is not None:
        ins = tuple(ins) + (after,)
        in_specs = list(in_specs) + [pl.BlockSpec(after.shape, lambda *_: (0, 0))]

    def body(*refs):
        a_ref, b_ref = refs[:2]
        o_ref = refs[n_in]
        if parts is None:
            acc = lax.dot_general(a_ref[...], b_ref[...], dims, preferred_element_type=F32)
        else:
            acc = None
            for part in parts:
                a, b = part(a_ref, b_ref)
                prod = lax.dot_general(a, b, dims, preferred_element_type=F32)
                acc = prod if acc is None else acc + prod
        if has_res:
            acc = acc + refs[2][...]
        o_ref[...] = acc.astype(o_ref.dtype)

    return pl.pallas_call(
        body, name=name, grid=grid, in_specs=in_specs, out_specs=out_spec, out_shape=out_shape,
        compiler_params=_cparams(("parallel",) * len(grid), vmem_mb),
    )(*ins)


def _rows_for(m, row_bytes, budget):
    return _pick(m, tuple(t for t in (2048, 1024, 512, 256, 128) if t * row_bytes <= budget))


def _mm_nn_blocked(name, a, wb, out_dtype, halves=False):
    m, k = a.shape
    nb, _, bn = wb.shape
    tm = _rows_for(m, bn * jnp.dtype(out_dtype).itemsize, 6 * MIB)
    a_spec = pl.BlockSpec((tm, k), lambda j, i: (i, 0))
    b_spec = pl.BlockSpec((None, k, bn), lambda j, i: (j, 0, 0))
    if halves:
        hb = nb // 2
        out_shape = jax.ShapeDtypeStruct((2, m, hb * bn), out_dtype)
        o_spec = pl.BlockSpec((None, tm, bn), lambda j, i: (j // hb, i, j % hb))
    else:
        out_shape = jax.ShapeDtypeStruct((m, nb * bn), out_dtype)
        o_spec = pl.BlockSpec((tm, bn), lambda j, i: (i, j))
    return _mm(name, (nb, m // tm), (a, wb), [a_spec, b_spec], out_shape, o_spec, NN_DIMS)


def _mm_nn_res(name, a, w, res):
    m, k = a.shape
    n = w.shape[1]
    tm = _pick(m, (512, 256, 128))
    tn = _rows_for(n, k * 2, 12 * MIB)
    a_spec = pl.BlockSpec((tm, k), lambda j, i: (i, 0))
    b_spec = pl.BlockSpec((k, tn), lambda j, i: (0, j))
    r_spec = pl.BlockSpec((tm, tn), lambda j, i: (i, j))
    o_spec = pl.BlockSpec((tm, tn), lambda j, i: (i, j))
    return _mm(name, (n // tn, m // tm), (a, w, res), [a_spec, b_spec, r_spec], jax.ShapeDtypeStruct((m, n), F32),
               o_spec, NN_DIMS, has_res=True)


def _mm_nt_blocked(name, dy, wb, halves=False, after=None):
    nb, n, bn = wb.shape
    m = dy.shape[-2]
    tm = _pick(m, (512, 256, 128))
    tn = _rows_for(n, nb * bn * 2, 12 * MIB)
    if halves:
        hb = nb // 2
        a_spec = pl.BlockSpec((2, tm, hb * bn), lambda j, i: (0, i, 0))
        a_part = lambda kk: (lambda a_ref: a_ref[kk // hb, :, (kk % hb) * bn:(kk % hb + 1) * bn])
    else:
        a_spec = pl.BlockSpec((tm, nb * bn), lambda j, i: (i, 0))
        a_part = lambda kk: (lambda a_ref: a_ref[:, kk * bn:(kk + 1) * bn])
    parts = [(lambda a_ref, b_ref, kk=kk, sel=a_part(kk): (sel(a_ref), b_ref[kk])) for kk in range(nb)]
    b_spec = pl.BlockSpec((nb, tn, bn), lambda j, i: (0, j, 0))
    o_spec = pl.BlockSpec((tm, tn), lambda j, i: (i, j))
    return _mm(name, (n // tn, m // tm), (dy, wb), [a_spec, b_spec], jax.ShapeDtypeStruct((m, n), F32), o_spec,
               NT_DIMS, parts=parts, after=after)


def _mm_nt_plain(name, dy, w, out_dtype=F32):
    m, k = dy.shape
    n = w.shape[0]
    tm = _rows_for(m, k * 2, 8 * MIB)
    tn = _pick(n, (512, 256, 128))
    a_spec = pl.BlockSpec((tm, k), lambda j, i: (i, 0))
    b_spec = pl.BlockSpec((tn, k), lambda j, i: (j, 0))
    o_spec = pl.BlockSpec((tm, tn), lambda j, i: (i, j))
    return _mm(name, (n // tn, m // tm), (dy, w), [a_spec, b_spec], jax.ShapeDtypeStruct((m, n), out_dtype), o_spec,
               NT_DIMS)


def _mm_tn_blocked(name, a, dy, nb, halves=False):
    s, k1 = a.shape
    bn = (dy.shape[-1] * (2 if halves else 1)) // nb
    tm = _rows_for(k1, bn * 2, 6 * MIB)
    a_spec = pl.BlockSpec((s, tm), lambda j, i: (0, i))
    if halves:
        hb = nb // 2
        b_spec = pl.BlockSpec((None, s, bn), lambda j, i: (j // hb, 0, j % hb))
    else:
        b_spec = pl.BlockSpec((s, bn), lambda j, i: (0, j))
    o_spec = pl.BlockSpec((None, tm, bn), lambda j, i: (j, i, 0))
    return _mm(name, (nb, k1 // tm), (a, dy), [a_spec, b_spec], jax.ShapeDtypeStruct((nb, k1, bn), BF16), o_spec,
               TN_DIMS)


def _mm_tn_plain(name, a, dy):
    s, k1 = a.shape
    n = dy.shape[1]
    tm = _pick(k1, (512, 256, 128))
    tn = _rows_for(n, s * 2, 8 * MIB)
    a_spec = pl.BlockSpec((s, tm), lambda i, j: (0, i))
    b_spec = pl.BlockSpec((s, tn), lambda i, j: (0, j))
    o_spec = pl.BlockSpec((tm, tn), lambda i, j: (i, j))
    return _mm(name, (k1 // tm, n // tn), (a, dy), [a_spec, b_spec], jax.ShapeDtypeStruct((k1, n), BF16), o_spec,
               TN_DIMS)


def _rstd(x):
    return lax.rsqrt(jnp.mean(x * x, axis=-1, keepdims=True) + EPS)


def _norm_bwd(dy, xhat, r, g):
    dxhat = dy * g
    return r * (dxhat - xhat * jnp.mean(dxhat * xhat, axis=-1, keepdims=True))


def _rmsnorm_fwd(name, x, g):
    s, d = x.shape
    tr = _pick(s, (256, 128))

    def body(x_ref, g_ref, h_ref):
        xv = x_ref[...]
        h_ref[...] = (xv * _rstd(xv) * g_ref[...]).astype(BF16)

    return pl.pallas_call(
        body, name=name, grid=(s // tr,),
        in_specs=[pl.BlockSpec((tr, d), lambda i: (i, 0)), pl.BlockSpec((1, d), lambda i: (0, 0))],
        out_specs=pl.BlockSpec((tr, d), lambda i: (i, 0)),
        out_shape=jax.ShapeDtypeStruct((s, d), BF16), compiler_params=_cparams(("parallel",)),
    )(x, g)


def _rmsnorm_bwd(name, dh, x, g, dres):
    s, d = x.shape
    tr = _pick(s, (256, 128))

    def body(dh_ref, x_ref, g_ref, dres_ref, dx_ref, dxb_ref, dg_ref):
        xv = x_ref[...]
        r = _rstd(xv)
        xhat = xv * r
        dhv = dh_ref[...]
        dx = dres_ref[...] + _norm_bwd(dhv, xhat, r, g_ref[...])
        dx_ref[...] = dx
        dxb_ref[...] = dx.astype(BF16)
        part = jnp.sum(dhv * xhat, axis=0, keepdims=True)

        @pl.when(pl.program_id(0) == 0)
        def _():
            dg_ref[...] = part

        @pl.when(pl.program_id(0) > 0)
        def _():
            dg_ref[...] += part

    row = pl.BlockSpec((tr, d), lambda i: (i, 0))
    vec = pl.BlockSpec((1, d), lambda i: (0, 0))
    return pl.pallas_call(
        body, name=name, grid=(s // tr,), in_specs=[row, row, vec, row], out_specs=(row, row, vec),
        out_shape=(jax.ShapeDtypeStruct((s, d), F32), jax.ShapeDtypeStruct((s, d), BF16),
                   jax.ShapeDtypeStruct((1, d), F32)),
        compiler_params=_cparams(("arbitrary",)),
    )(dh, x, g, dres)


def _loss_head(y, target):
    s, d = y.shape
    tr = _pick(s, (256, 128))

    def body(y_ref, t_ref, dy_ref, dyb_ref, loss_ref):
        err = y_ref[...] - t_ref[...]
        dy = err * (1.0 / d)
        dy_ref[...] = dy
        dyb_ref[...] = dy.astype(BF16)
        part = 0.5 * jnp.sum(jnp.mean(err * err, axis=-1, keepdims=True), axis=0, keepdims=True)
        part = jnp.broadcast_to(part, (1, 128))

        @pl.when(pl.program_id(0) == 0)
        def _():
            loss_ref[...] = part

        @pl.when(pl.program_id(0) > 0)
        def _():
            loss_ref[...] += part

    row = pl.BlockSpec((tr, d), lambda i: (i, 0))
    return pl.pallas_call(
        body, name="loss_head", grid=(s // tr,), in_specs=[row, row],
        out_specs=(row, row, pl.BlockSpec((1, 128), lambda i: (0, 0))),
        out_shape=(jax.ShapeDtypeStruct((s, d), F32), jax.ShapeDtypeStruct((s, d), BF16),
                   jax.ShapeDtypeStruct((1, 128), F32)),
        compiler_params=_cparams(("arbitrary",)),
    )(y, target)


def _split_dot(x, tri):
    hi = x.astype(BF16)
    lo = (x - hi.astype(F32)).astype(BF16)
    return (jnp.dot(hi, tri, preferred_element_type=F32) + jnp.dot(lo, tri, preferred_element_type=F32))


def _tile_iotas():
    row = lax.broadcasted_iota(jnp.int32, (TILE, TILE), 0)
    col = lax.broadcasted_iota(jnp.int32, (TILE, TILE), 1)
    return row, col


def _sb_logits(qi, kb, mask):
    z = lax.dot_general(qi, kb, NT_DIMS, preferred_element_type=F32) * (HEAD_DIM ** -0.5)
    sp = jnp.log1p(jnp.exp(-jnp.abs(z)))
    lb = jnp.minimum(z, 0.0) - sp
    l1m = -jnp.maximum(z, 0.0) - sp
    if mask is not None:
        l1m = jnp.where(mask, l1m, 0.0)
    return lb, l1m


def _attn_fwd(p, gq, gk, ga, n_heads):
    s = p.shape[0]
    nq = s // TILE

    hp = ATTN_HEADS_PER_STEP
    wd = hp * HEAD_DIM

    def body(q_ref, k_ref, v_ref, gq_ref, gk_ref, ga_ref, att_ref, o_ref, r_ref, qn_s, kn_s, vb_s):
        heads = [slice(hh * HEAD_DIM, (hh + 1) * HEAD_DIM) for hh in range(hp)]
        for hd in heads:
            qv = q_ref[:, hd]
            qn_s[:, hd] = (qv * _rstd(qv) * gq_ref[...]).astype(BF16)
            kv = k_ref[:, hd]
            kn_s[:, hd] = (kv * _rstd(kv) * gk_ref[...]).astype(BF16)
        vb_s[...] = v_ref[...].astype(BF16)
        row, col = _tile_iotas()
        causal = col < row
        upper_ones = jnp.concatenate([(row > col).astype(BF16), jnp.ones((TILE, TILE), BF16)], axis=1)

        def tiles(rows, keys, states, mask):
            logits = [_sb_logits(qn_s[rows, hd], kn_s[keys, hd], mask) for hd in heads]
            sums = [_split_dot(l1m, upper_ones) for _, l1m in logits]
            probs = []
            for (lb, _), sm, (_, c) in zip(logits, sums, states):
                a = jnp.exp(lb + sm[:, :TILE] + c)
                probs.append((a if mask is None else jnp.where(mask, a, 0.0)).astype(BF16))
            outs = [jnp.dot(a, vb_s[keys, hd], preferred_element_type=F32) for a, hd in zip(probs, heads)]
            return tuple((o_acc + o, c + sm[:, TILE:]) for (o_acc, c), o, sm in zip(states, outs, sums))

        def qblock(i, _):
            rows = pl.ds(pl.multiple_of(i * TILE, TILE), TILE)
            zero = jnp.zeros((TILE, HEAD_DIM), F32)
            states = tiles(rows, rows, tuple((zero, zero) for _ in heads), causal)

            def kblock(jj, states):
                keys = pl.ds(pl.multiple_of((i - 1 - jj) * TILE, TILE), TILE)
                return tiles(rows, keys, states, None)

            states = lax.fori_loop(0, i, kblock, states)
            for hh, (hd, (o_acc, c)) in enumerate(zip(heads, states)):
                o_ref[rows, hd] = o_acc
                r_ref[rows, hd] = c
                att_ref[rows, hd] = (o_acc * _rstd(o_acc) * ga_ref[hh]).astype(BF16)
            return 0

        lax.fori_loop(0, nq, qblock, 0)

    col_blk = lambda off: pl.BlockSpec((s, wd), lambda h: (0, off + h))
    vec = pl.BlockSpec((1, HEAD_DIM), lambda h: (0, 0))
    hvec = pl.BlockSpec((hp, 1, HEAD_DIM), lambda h: (h, 0, 0))
    out = pl.BlockSpec((s, wd), lambda h: (0, h))
    w = n_heads * HEAD_DIM
    steps = n_heads // hp
    return pl.pallas_call(
        body, name="attn_fwd", grid=(steps,),
        in_specs=[col_blk(0), col_blk(steps), col_blk(2 * steps), vec, vec, hvec],
        out_specs=(out, out, out),
        out_shape=(jax.ShapeDtypeStruct((s, w), BF16), jax.ShapeDtypeStruct((s, w), F32),
                   jax.ShapeDtypeStruct((s, w), F32)),
        scratch_shapes=[pltpu.VMEM((s, wd), BF16)] * 3,
        compiler_params=_cparams(("parallel",), ATTN_VMEM_MB),
    )(p, p, p, gq, gk, ga)


def _attn_bwd(p, gq, gk, ga, o, rsum, dmix, n_heads):
    s = p.shape[0]
    nq = s // TILE

    hp = ATTN_HEADS_PER_STEP
    wd = hp * HEAD_DIM
    scale = HEAD_DIM ** -0.5

    def body(q_ref, k_ref, v_ref, gq_ref, gk_ref, ga_ref, o_ref, r_ref, dm_ref,
             dq_ref, dk_ref, dv_ref, dgq_ref, dgk_ref, dga_ref,
             qn_s, kn_s, vb_s, do_s, dqn_s, dkn_s, dv_s):
        step = pl.program_id(0)
        gqv, gkv = gq_ref[...], gk_ref[...]
        heads = [slice(hh * HEAD_DIM, (hh + 1) * HEAD_DIM) for hh in range(hp)]
        for hh, hd in enumerate(heads):
            qv = q_ref[:, hd]
            qn_s[:, hd] = (qv * _rstd(qv) * gqv).astype(BF16)
            kv = k_ref[:, hd]
            kn_s[:, hd] = (kv * _rstd(kv) * gkv).astype(BF16)
            ov = o_ref[:, hd]
            ro = _rstd(ov)
            ohat = ov * ro
            dm = dm_ref[:, hd]
            dga_ref[hh] = jnp.sum(dm * ohat, axis=0, keepdims=True)
            do_s[:, hd] = _norm_bwd(dm, ohat, ro, ga_ref[hh]).astype(BF16)
        vb_s[...] = v_ref[...].astype(BF16)
        dkn_s[...] = jnp.zeros_like(dkn_s)
        dv_s[...] = jnp.zeros_like(dv_s)
        row, col = _tile_iotas()
        causal = col < row
        ones = jnp.ones((TILE, TILE), BF16)
        incl_ones = jnp.concatenate([(row <= col).astype(BF16), ones], axis=1)
        excl_ones = jnp.concatenate([(row < col).astype(BF16), ones], axis=1)

        def tiles(rows, keys, states, mask):
            qis = [qn_s[rows, hd] for hd in heads]
            dois = [do_s[rows, hd] for hd in heads]
            logits = [_sb_logits(qi, kn_s[keys, hd], mask) for qi, hd in zip(qis, heads)]
            sums = [_split_dot(l1m, incl_ones) for _, l1m in logits]
            das = [lax.dot_general(doi, vb_s[keys, hd], NT_DIMS, preferred_element_type=F32)
                   for doi, hd in zip(dois, heads)]
            probs, dss = [], []
            for (lb, _), sm, da, hd, (_, pfx, _) in zip(logits, sums, das, heads, states):
                a = jnp.exp(lb + (r_ref[rows, hd] - pfx - sm[:, :TILE]))
                a = a if mask is None else jnp.where(mask, a, 0.0)
                probs.append(a.astype(BF16))
                dss.append(da * a)
            dsums = [_split_dot(ds, excl_ones) for ds in dss]
            dzs = []
            for (lb, _), ds, dsm, (_, _, pc) in zip(logits, dss, dsums, states):
                dl1m = pc + dsm[:, :TILE]
                dl1m = dl1m if mask is None else jnp.where(mask, dl1m, 0.0)
                beta = jnp.exp(lb)
                dzs.append(((ds * (1.0 - beta) - dl1m * beta) * scale).astype(BF16))
            dqs = [jnp.dot(dz, kn_s[keys, hd], preferred_element_type=F32) for dz, hd in zip(dzs, heads)]
            for dz, qi, a, doi, hd in zip(dzs, qis, probs, dois, heads):
                dkn_s[keys, hd] += lax.dot_general(dz, qi, TN_DIMS, preferred_element_type=F32)
                dv_s[keys, hd] += lax.dot_general(a, doi, TN_DIMS, preferred_element_type=F32)
            return tuple((dq_acc + dq, pfx + sm[:, TILE:], pc + dsm[:, TILE:])
                         for (dq_acc, pfx, pc), dq, sm, dsm in zip(states, dqs, sums, dsums))

        def qblock(i, _):
            rows = pl.ds(pl.multiple_of(i * TILE, TILE), TILE)
            zero = jnp.zeros((TILE, HEAD_DIM), F32)

            def kblock(b, states):
                return tiles(rows, pl.ds(pl.multiple_of(b * TILE, TILE), TILE), states, None)

            states = lax.fori_loop(0, i, kblock, tuple((zero, zero, zero) for _ in heads))
            states = tiles(rows, rows, states, causal)
            for hd, (dq_acc, _, _) in zip(heads, states):
                dqn_s[rows, hd] = dq_acc
            return 0

        lax.fori_loop(0, nq, qblock, 0)

        def norm_in_bwd(x_ref, g, dn_s, dx_ref, dg_ref):
            part = jnp.zeros((1, HEAD_DIM), F32)
            for hd in heads:
                xv = x_ref[:, hd]
                r = _rstd(xv)
                xhat = xv * r
                dn = dn_s[:, hd]
                dx_ref[:, hd] = _norm_bwd(dn, xhat, r, g).astype(BF16)
                part = part + jnp.sum(dn * xhat, axis=0, keepdims=True)

            @pl.when(step == 0)
            def _():
                dg_ref[...] = part

            @pl.when(step > 0)
            def _():
                dg_ref[...] += part

        norm_in_bwd(q_ref, gqv, dqn_s, dq_ref, dgq_ref)
        norm_in_bwd(k_ref, gkv, dkn_s, dk_ref, dgk_ref)
        dv_ref[...] = dv_s[...].astype(BF16)

    once = pl.Buffered(1)
    steps = n_heads // hp
    col_blk = lambda off: pl.BlockSpec((s, wd), lambda h: (0, off + h), pipeline_mode=once)
    vec = pl.BlockSpec((1, HEAD_DIM), lambda h: (0, 0))
    hvec = pl.BlockSpec((hp, 1, HEAD_DIM), lambda h: (h, 0, 0))
    blk = pl.BlockSpec((s, wd), lambda h: (0, h), pipeline_mode=once)
    w = n_heads * HEAD_DIM
    big = jax.ShapeDtypeStruct((s, w), BF16)
    return pl.pallas_call(
        body, name="attn_bwd", grid=(steps,),
        in_specs=[col_blk(0), col_blk(steps), col_blk(2 * steps), vec, vec, hvec, blk, blk, blk],
        out_specs=(blk, blk, blk, vec, vec, hvec),
        out_shape=(big, big, big, jax.ShapeDtypeStruct((1, HEAD_DIM), F32), jax.ShapeDtypeStruct((1, HEAD_DIM), F32),
                   jax.ShapeDtypeStruct((n_heads, 1, HEAD_DIM), F32)),
        scratch_shapes=[pltpu.VMEM((s, wd), BF16)] * 4 + [pltpu.VMEM((s, wd), F32)] * 3,
        compiler_params=_cparams(("arbitrary",), ATTN_VMEM_MB),
    )(p, p, p, gq, gk, ga, o, rsum, dmix)


_INV_SQRT2 = 0.7071067811865476
_INV_SQRT_2PI = 0.3989422804014327


def _gelu(x):
    return 0.5 * x * (1.0 + lax.erf(x * _INV_SQRT2))


def _gelu_grad(x):
    return 0.5 * (1.0 + lax.erf(x * _INV_SQRT2)) + x * (_INV_SQRT_2PI * jnp.exp(-0.5 * x * x))


def _sgu_fwd(p, gs, w_s, b_s, gb, col0):
    s = p.shape[0]
    n_chunks = s // TILE

    def body(u_ref, v_ref, gs_ref, w_ref, b_ref, gb_ref, out_ref, vs_s):
        vg = _gelu(v_ref[...])
        vs_s[...] = (vg * _rstd(vg) * gs_ref[...]).astype(BF16)
        row, col = _tile_iotas()
        wt = jnp.where(col <= row, w_ref[...], 0.0).astype(BF16)
        bcol = b_ref[...]
        gbv = gb_ref[...]

        def chunk(c, _):
            rows = pl.ds(pl.multiple_of(c * TILE, TILE), TILE)
            mixed = jnp.dot(wt, vs_s[rows, :], preferred_element_type=F32) + bcol
            sg = _gelu(u_ref[rows, :]) * mixed
            out_ref[rows, :] = (sg * _rstd(sg) * gbv).astype(BF16)
            return 0

        lax.fori_loop(0, n_chunks, chunk, 0)

    col_blk = lambda off: pl.BlockSpec((s, HEAD_DIM), lambda g: (0, off + g))
    gvec = pl.BlockSpec((None, 1, HEAD_DIM), lambda g: (g, 0, 0))
    return pl.pallas_call(
        body, name="sgu_fwd", grid=(N_GROUPS,),
        in_specs=[col_blk(col0), col_blk(col0 + N_GROUPS), gvec,
                  pl.BlockSpec((None, TILE, TILE), lambda g: (g, 0, 0)),
                  pl.BlockSpec((None, TILE, 1), lambda g: (g, 0, 0)), gvec],
        out_specs=pl.BlockSpec((s, HEAD_DIM), lambda g: (0, g)),
        out_shape=jax.ShapeDtypeStruct((s, N_GROUPS * HEAD_DIM), BF16),
        scratch_shapes=[pltpu.VMEM((s, HEAD_DIM), BF16)],
        compiler_params=_cparams(("parallel",)),
    )(p, p, gs, w_s, b_s, gb)


def _sgu_bwd(p, gs, w_s, b_s, gb, dmix, col0, dm_col0):
    s = p.shape[0]
    n_chunks = s // TILE

    def body(u_ref, v_ref, gs_ref, w_ref, b_ref, gb_ref, dm_ref,
             du_ref, dv_ref, dgs_ref, dw_ref, db_ref, dgb_ref, vs_s, dvs_s):
        gsv = gs_ref[...]
        gbv = gb_ref[...]
        vg = _gelu(v_ref[...])
        vs_s[...] = (vg * _rstd(vg) * gsv).astype(BF16)
        row, col = _tile_iotas()
        causal = col <= row
        wt = jnp.where(causal, w_ref[...], 0.0).astype(BF16)
        bcol = b_ref[...]

        def chunk(c, carry):
            dw_acc, db_acc, dgb_acc = carry
            rows = pl.ds(pl.multiple_of(c * TILE, TILE), TILE)
            vs = vs_s[rows, :]
            mixed = jnp.dot(wt, vs, preferred_element_type=F32) + bcol
            u_pre = u_ref[rows, :]
            u = _gelu(u_pre)
            sg = u * mixed
            rs = _rstd(sg)
            sghat = sg * rs
            dm = dm_ref[rows, :]
            dsg = _norm_bwd(dm, sghat, rs, gbv)
            dgb_acc = dgb_acc + jnp.sum(dm * sghat, axis=0, keepdims=True)
            du_ref[rows, :] = (dsg * mixed * _gelu_grad(u_pre)).astype(BF16)
            dmixed = dsg * u
            db_acc = db_acc + jnp.sum(dmixed, axis=1, keepdims=True)
            dmb = dmixed.astype(BF16)
            dw_acc = dw_acc + lax.dot_general(dmb, vs, NT_DIMS, preferred_element_type=F32)
            dvs_s[rows, :] = lax.dot_general(wt, dmb, TN_DIMS, preferred_element_type=F32)
            return dw_acc, db_acc, dgb_acc

        dw_acc, db_acc, dgb_acc = lax.fori_loop(
            0, n_chunks, chunk,
            (jnp.zeros((TILE, TILE), F32), jnp.zeros((TILE, 1), F32), jnp.zeros((1, HEAD_DIM), F32)))
        dw_ref[...] = jnp.where(causal, dw_acc, 0.0)
        db_ref[...] = db_acc
        dgb_ref[...] = dgb_acc
        v_pre = v_ref[...]
        vg = _gelu(v_pre)
        rv = _rstd(vg)
        vhat = vg * rv
        dvs = dvs_s[...]
        dgs_ref[...] = jnp.sum(dvs * vhat, axis=0, keepdims=True)
        dv_ref[...] = (_norm_bwd(dvs, vhat, rv, gsv) * _gelu_grad(v_pre)).astype(BF16)

    col_blk = lambda off: pl.BlockSpec((s, HEAD_DIM), lambda g: (0, off + g))
    gvec = pl.BlockSpec((None, 1, HEAD_DIM), lambda g: (g, 0, 0))
    wspec = pl.BlockSpec((None, TILE, TILE), lambda g: (g, 0, 0))
    bspec = pl.BlockSpec((None, TILE, 1), lambda g: (g, 0, 0))
    blk = pl.BlockSpec((s, HEAD_DIM), lambda g: (0, g))
    big = jax.ShapeDtypeStruct((s, N_GROUPS * HEAD_DIM), BF16)
    gshape = jax.ShapeDtypeStruct((N_GROUPS, 1, HEAD_DIM), F32)
    return pl.pallas_call(
        body, name="sgu_bwd", grid=(N_GROUPS,),
        in_specs=[col_blk(col0), col_blk(col0 + N_GROUPS), gvec, wspec, bspec, gvec, col_blk(dm_col0)],
        out_specs=(blk, blk, gvec, wspec, bspec, gvec),
        out_shape=(big, big, gshape, jax.ShapeDtypeStruct((N_GROUPS, TILE, TILE), F32),
                   jax.ShapeDtypeStruct((N_GROUPS, TILE, 1), F32), gshape),
        scratch_shapes=[pltpu.VMEM((s, HEAD_DIM), BF16), pltpu.VMEM((s, HEAD_DIM), F32)],
        compiler_params=_cparams(("parallel",)),
    )(p, p, gs, w_s, b_s, gb, dmix)


def _shift_down(x, n):
    rows = lax.broadcasted_iota(jnp.int32, x.shape, 0)
    return jnp.where(rows >= n, pltpu.roll(x, n, 0), 0.0)


def _shift_up(x, n):
    s = x.shape[0]
    rows = lax.broadcasted_iota(jnp.int32, x.shape, 0)
    return jnp.where(rows < s - n, pltpu.roll(x, s - n, 0), 0.0)


def _conv(x, w, b):
    return b + w[0:1, :] * _shift_down(x, 2) + w[1:2, :] * _shift_down(x, 1) + w[2:3, :] * x


def _conv_specs(s, tn):
    xspec = pl.BlockSpec((2, s, tn), lambda j: (0, 0, j))
    wspec = pl.BlockSpec((2, CONV_WIDTH, tn), lambda j: (0, 0, j))
    bspec = pl.BlockSpec((2, 1, tn), lambda j: (0, 0, j))
    return xspec, wspec, bspec


def _conv_gate_fwd(up, cw, cb):
    _, s, f = up.shape
    tn = _pick(f, (256, 128))

    def body(x_ref, w_ref, b_ref, act_ref):
        gate = _conv(x_ref[0], w_ref[0], b_ref[0])
        val = _conv(x_ref[1], w_ref[1], b_ref[1])
        act_ref[...] = (gate * jax.nn.sigmoid(gate) * val).astype(BF16)

    xspec, wspec, bspec = _conv_specs(s, tn)
    return pl.pallas_call(
        body, name="conv_gate_fwd", grid=(f // tn,), in_specs=[xspec, wspec, bspec],
        out_specs=pl.BlockSpec((s, tn), lambda j: (0, j)), out_shape=jax.ShapeDtypeStruct((s, f), BF16),
        compiler_params=_cparams(("parallel",)),
    )(up, cw, cb)


def _conv_gate_bwd(up, cw, cb, dact):
    _, s, f = up.shape
    tn = _pick(f, (256, 128))

    def body(x_ref, w_ref, b_ref, da_ref, dx_ref, dw_ref, db_ref):
        xg, xv = x_ref[0], x_ref[1]
        wg, wv = w_ref[0], w_ref[1]
        gate = _conv(xg, wg, b_ref[0])
        val = _conv(xv, wv, b_ref[1])
        sig = jax.nn.sigmoid(gate)
        da = da_ref[...]
        dval = da * (gate * sig)
        dgate = da * val * (sig * (1.0 + gate * (1.0 - sig)))
        for half, (x, w, dy) in enumerate(((xg, wg, dgate), (xv, wv, dval))):
            dx_ref[half] = (w[2:3, :] * dy + w[1:2, :] * _shift_up(dy, 1) + w[0:1, :] * _shift_up(dy, 2)).astype(BF16)
            dw_ref[half, 0:1, :] = jnp.sum(dy * _shift_down(x, 2), axis=0, keepdims=True)
            dw_ref[half, 1:2, :] = jnp.sum(dy * _shift_down(x, 1), axis=0, keepdims=True)
            dw_ref[half, 2:3, :] = jnp.sum(dy * x, axis=0, keepdims=True)
            db_ref[half] = jnp.sum(dy, axis=0, keepdims=True)

    xspec, wspec, bspec = _conv_specs(s, tn)
    return pl.pallas_call(
        body, name="conv_gate_bwd", grid=(f // tn,),
        in_specs=[xspec, wspec, bspec, pl.BlockSpec((s, tn), lambda j: (0, j))],
        out_specs=(xspec, wspec, bspec),
        out_shape=(jax.ShapeDtypeStruct((2, s, f), BF16), jax.ShapeDtypeStruct((2, CONV_WIDTH, f), F32),
                   jax.ShapeDtypeStruct((2, 1, f), F32)),
        compiler_params=_cparams(("parallel",)),
    )(up, cw, cb, dact)


def _mesh_pos():
    return lax.axis_index("x"), lax.axis_index("y"), lax.axis_index("c")


def _remote(src, dst, send_sem, recv_sem, to):
    return pltpu.make_async_remote_copy(src_ref=src, dst_ref=dst, send_sem=send_sem, recv_sem=recv_sem,
                                        device_id=to, device_id_type=pl.DeviceIdType.MESH)


HBM_SPEC = pl.BlockSpec(memory_space=pltpu.HBM)
SEM_SPEC = pl.BlockSpec(memory_space=pltpu.SEMAPHORE)
ANY_SPEC = pl.BlockSpec(memory_space=pl.ANY)
TOKEN_SPEC = pl.BlockSpec(memory_space=pltpu.VMEM)
TOKEN_SHAPE = jax.ShapeDtypeStruct((8, 128), F32)
DATAFLOW = pltpu.SideEffectType.DATAFLOW_SIDE_EFFECTING
GATHER_PLANE = (2, 4, 6)


def _slot(pos):
    return 4 * pos[0] + 2 * pos[1] + pos[2]


def _flip(pos, k):
    return (pos[0] ^ ((k >> 2) & 1), pos[1] ^ ((k >> 1) & 1), pos[2] ^ (k & 1))


def _hbm(a):
    return pltpu.with_memory_space_constraint(a, pltpu.HBM)


def _hbm_shapes(arrays):
    return tuple(pltpu.HBM(a.shape, a.dtype) for a in arrays)


class _Split:
    def __init__(self, n, outs, has_sems):
        k = 2 * n if has_sems else 0
        self.n = n
        self.sems = list(outs[:k])
        self.bufs = list(outs[k:k + 2 * n])
        self.token = outs[-1]


def _split_call(name, body, bufs, sems_in, makes_sems, after):
    n = len(bufs) // 2
    k = 2 * n if makes_sems else 0
    m = len(sems_in)
    afters = list(after) if isinstance(after, (list, tuple)) else [after]
    na = len(afters)

    def wrapped(*refs):
        srcs, dsts = refs[:n], refs[n:2 * n]
        s_in = refs[2 * n:2 * n + m]
        s_out = refs[2 * n + m + na:2 * n + m + na + k]
        token, local_sems = refs[-2], refs[-1]
        body(srcs, dsts, s_in, s_out, local_sems)
        token[...] = jnp.zeros_like(token)

    outs = pl.pallas_call(
        wrapped, name=name,
        out_shape=(pltpu.SemaphoreType.DMA(()),) * k + _hbm_shapes(bufs) + (TOKEN_SHAPE,),
        in_specs=[HBM_SPEC] * (2 * n) + [SEM_SPEC] * m + [ANY_SPEC] * na,
        out_specs=(SEM_SPEC,) * k + (HBM_SPEC,) * (2 * n) + (TOKEN_SPEC,),
        input_output_aliases={i: k + i for i in range(2 * n)},
        scratch_shapes=[pltpu.SemaphoreType.DMA((n,))],
        compiler_params=pltpu.CompilerParams(has_side_effects=DATAFLOW),
    )(*[_hbm(b) for b in bufs], *sems_in, *afters)
    return _Split(n, outs, makes_sems)


def _wait_slots(land, count, send_sem, recv_sem, me, send=False, recv=False):
    span = land.at[pl.ds(0, count)]
    cp = _remote(span, span, send_sem, recv_sem, me)
    if send:
        cp.wait_send()
    if recv:
        cp.wait_recv()


def _gather_start(name, shards, after):
    n = len(shards)
    my_slot = _slot(_mesh_pos())
    lands = [lax.dynamic_update_slice(lax.empty((N_DEV,) + w.shape, w.dtype), w[None], (my_slot, 0, 0)) for w in shards]

    def body(srcs, dsts, _, sems, local_sems):
        me = _mesh_pos()
        for a in range(n):
            for k in (1,) + GATHER_PLANE:
                _remote(srcs[a], dsts[a].at[_slot(me)], sems[a], sems[n + a], _flip(me, k)).start()

    return _split_call(name, body, list(shards) + lands, [], True, after)


def _gather_forward(name, started, after):
    n = started.n

    def body(srcs, dsts, sems_a, sems_b, local_sems):
        me = _mesh_pos()
        sibling = _flip(me, 1)
        for a in range(n):
            _wait_slots(dsts[a], 4, sems_a[a], sems_a[n + a], me, recv=True)
            for k in GATHER_PLANE:
                block = dsts[a].at[_slot(_flip(me, k))]
                _remote(block, block, sems_b[a], sems_b[n + a], sibling).start()
        for a in range(n):
            _wait_slots(dsts[a], 4, sems_a[a], sems_a[n + a], me, send=True)

    return _split_call(name, body, started.bufs, started.sems, True, after)


def _gather_finish(name, forwarded, after):
    n = forwarded.n

    def body(srcs, dsts, sems_b, _, local_sems):
        me = _mesh_pos()
        for a in range(n):
            _wait_slots(dsts[a], 3, sems_b[a], sems_b[n + a], me, send=True, recv=True)

    return _split_call(name, body, forwarded.bufs, forwarded.sems, False, after).bufs[n:]


def _exchange_start(name, blocked, after):
    n = len(blocked)
    my_slot = _slot(_mesh_pos())
    lands = [lax.dynamic_update_slice(lax.empty(w.shape, w.dtype), lax.dynamic_slice_in_dim(w, my_slot, 1, 0),
                                      (my_slot, 0, 0)) for w in blocked]

    def body(srcs, dsts, _, sems, local_sems):
        me = _mesh_pos()
        for a in range(n):
            for k in range(1, N_DEV):
                peer = _flip(me, k)
                _remote(srcs[a].at[_slot(peer)], dsts[a].at[_slot(me)], sems[a], sems[n + a], peer).start()

    return _split_call(name, body, list(blocked) + lands, [], True, after)


def _exchange_finish(name, started, after):
    n = started.n

    def body(srcs, dsts, sems, _, local_sems):
        me = _mesh_pos()
        for a in range(n):
            _wait_slots(dsts[a], N_DEV - 1, sems[a], sems[n + a], me, send=True, recv=True)

    return _split_call(name, body, started.bufs, started.sems, False, after).bufs[n:]


def _broadcast_start(name, arrays, after):
    n = len(arrays)
    my_slot = _slot(_mesh_pos())
    lands = [lax.dynamic_update_slice(lax.empty((N_DEV,) + w.shape, w.dtype), w[None], (my_slot, 0, 0)) for w in arrays]

    def body(srcs, dsts, _, sems, local_sems):
        me = _mesh_pos()
        for a in range(n):
            for k in range(1, N_DEV):
                _remote(srcs[a], dsts[a].at[_slot(me)], sems[a], sems[n + a], _flip(me, k)).start()

    return _split_call(name, body, list(arrays) + lands, [], True, after)


def _adamw_math(w, g, m, v):
    m = ADAM_B1 * m + (1.0 - ADAM_B1) * g
    v = ADAM_B2 * v + (1.0 - ADAM_B2) * (g * g)
    m_hat = m / (1.0 - ADAM_B1 ** ADAM_STEP)
    v_hat = v / (1.0 - ADAM_B2 ** ADAM_STEP)
    delta = -ADAM_LR * (m_hat / (jnp.sqrt(v_hat) + ADAM_EPS) + ADAM_WD * w)
    return delta, m, v


def _adamw(name, w, m, v, parts, layer, prev=None):
    _, r, c = w.shape
    tr = _pick(r, tuple(t for t in (256, 128, 64, 32, 16) if t * c <= ADAMW_TILE_ELEMS))
    n_prev = 0 if prev is None else 4

    def body(*refs):
        w_ref, m_ref, v_ref, p_ref = refs[:4]
        g_ref, d_ref, nm_ref, nv_ref = refs[4 + n_prev:]
        g = p_ref[0].astype(F32)
        for src in range(1, N_DEV):
            g = g + p_ref[src].astype(F32)
        delta, nm, nv = _adamw_math(w_ref[...], g, m_ref[...], v_ref[...])
        g_ref[...] = g
        d_ref[...] = delta
        nm_ref[...] = nm
        nv_ref[...] = nv

    wspec = pl.BlockSpec((None, tr, c), lambda i: (layer, i, 0))
    pspec = pl.BlockSpec((N_DEV, tr, c), lambda i: (0, i, 0))
    shp = jax.ShapeDtypeStruct(w.shape, F32)
    return pl.pallas_call(
        body, name=name, grid=(r // tr,), in_specs=[wspec] * 3 + [pspec] + [ANY_SPEC] * n_prev,
        out_specs=(wspec,) * 4, out_shape=(shp,) * 4, input_output_aliases={4 + j: j for j in range(n_prev)},
        compiler_params=_cparams(("parallel",)),
    )(w, m, v, parts, *([] if prev is None else prev))


PACK_TILE = 8 * 128


def _pack(arrays):
    flat = []
    for a in arrays:
        v = a.reshape(-1)
        pad = (-v.shape[0]) % PACK_TILE
        flat.append(jnp.pad(v, (0, pad)) if pad else v)
    return jnp.concatenate(flat).reshape(-1, 128)


def _unpack(buf, like):
    flat = buf.reshape(-1)
    out, off = [], 0
    for a in like:
        n = 1
        for dim in a.shape:
            n *= dim
        out.append(flat[off:off + n].reshape(a.shape))
        off += n + (-n) % PACK_TILE
    return out


def _sum_slots(gathered):
    _, r, c = gathered.shape

    def body(x_ref, o_ref):
        acc = x_ref[0]
        for src in range(1, N_DEV):
            acc = acc + x_ref[src]
        o_ref[...] = acc

    return pl.pallas_call(body, name="small_grad_sum", out_shape=jax.ShapeDtypeStruct((r, c), F32))(gathered)


def _adamw_small(w, g, m, v):
    shp = jax.ShapeDtypeStruct(w.shape, F32)

    def body(w_ref, g_ref, m_ref, v_ref, d_ref, nm_ref, nv_ref):
        delta, nm, nv = _adamw_math(w_ref[...], g_ref[...], m_ref[...], v_ref[...])
        d_ref[...] = delta
        nm_ref[...] = nm
        nv_ref[...] = nv

    return pl.pallas_call(body, name="adamw_small", out_shape=(shp,) * 3)(w, g, m, v)


def kernel(x, attn_norm_g, w_in, q_norm_g, k_norm_g, sgu_norm_g, sgu_w, sgu_b, out_norm_a_g, out_norm_b_g, w_out, ffn_norm_g, w_up, conv_w, conv_b, w_down, loss_target, m_attn_norm_g, m_w_in, m_q_norm_g, m_k_norm_g, m_sgu_norm_g, m_sgu_w, m_sgu_b, m_out_norm_a_g, m_out_norm_b_g, m_w_out, m_ffn_norm_g, m_w_up, m_conv_w, m_conv_b, m_w_down, v_attn_norm_g, v_w_in, v_q_norm_g, v_k_norm_g, v_sgu_norm_g, v_sgu_w, v_sgu_b, v_out_norm_a_g, v_out_norm_b_g, v_w_out, v_ffn_norm_g, v_w_up, v_conv_w, v_conv_b, v_w_down):
    depth = w_in.shape[0]
    s, d = x.shape[1], x.shape[2]
    n_heads = (d // 2) // HEAD_DIM
    sgu_col0 = 3 * n_heads
    f2 = w_up.shape[2] * N_DEV
    ff = f2 // 2
    my_slot = 4 * lax.axis_index("x") + 2 * lax.axis_index("y") + lax.axis_index("c")

    wb = [(w_in[l].astype(BF16), w_out[l].astype(BF16), w_up[l].astype(BF16), w_down[l].astype(BF16))
          for l in range(depth)]
    groups = {"in0": [wb[0][0]], "mix0": [wb[0][1], conv_w.reshape(depth * CONV_WIDTH, -1), wb[0][2]], "down0": [wb[0][3]]}
    for l in range(1, depth):
        groups[f"in{l}"] = [wb[l][0], wb[l][1]]
        groups[f"mix{l}"] = [wb[l][2], wb[l][3]]
    token = attn_norm_g
    started = {}
    for gname, group in groups.items():
        started[gname] = _gather_start(f"gather_{gname}_start", group, token)
        token = started[gname].token

    conv_b_all = conv_b.reshape(depth, 2, 1, ff)
    sgu_b_col = sgu_b[..., None]
    forwarded = _gather_forward("gather_in0_forward", started["in0"], token)
    win_g = _gather_finish("gather_in0_finish", forwarded, forwarded.token)[0]
    wout_g = None

    xs = x[0]
    saved = []
    gathered = []
    for l in range(depth):
        g1 = attn_norm_g[l][None]
        g2 = ffn_norm_g[l][None]
        gq, gk = q_norm_g[l][None], k_norm_g[l][None]
        ga = out_norm_a_g[l][:, None, :]
        gs = sgu_norm_g[l][:, None, :]
        gb = out_norm_b_g[l][:, None, :]
        h1 = _rmsnorm_fwd("attn_norm_fwd", xs, g1)
        p = _mm_nn_blocked("in_proj", h1, win_g, F32)
        att, o, rsum = _attn_fwd(p, gq, gk, ga, n_heads)
        forwarded = _gather_forward(f"gather_mix{l}_forward", started[f"mix{l}"], att)
        sg = _sgu_fwd(p, gs, sgu_w[l], sgu_b_col[l], gb, sgu_col0)
        mix = jnp.concatenate([att, sg], axis=-1)
        landed = _gather_finish(f"gather_mix{l}_finish", forwarded, mix)
        if l == 0:
            wout_g, cw, wup_g = landed
            cw = jnp.transpose(cw.reshape(N_DEV, depth, CONV_WIDTH, -1), (1, 2, 0, 3)).reshape(depth, CONV_WIDTH, 2, ff)
            conv_w_all = jnp.transpose(cw, (0, 2, 1, 3))
        else:
            wup_g, wdown_g = landed
        x1 = _mm_nn_res("out_proj", mix, wout_g.reshape(d, d), xs)
        h2 = _rmsnorm_fwd("ffn_norm_fwd", x1, g2)
        up = _mm_nn_blocked("up_proj", h2, wup_g, F32, halves=True)
        if l == 0:
            forwarded = _gather_forward("gather_down0_forward", started["down0"], up)
        act = _conv_gate_fwd(up, conv_w_all[l], conv_b_all[l])
        if l == 0:
            wdown_g = _gather_finish("gather_down0_finish", forwarded, act)[0]
        if l + 1 < depth:
            forwarded = _gather_forward(f"gather_in{l + 1}_forward", started[f"in{l + 1}"], act)
        x2 = _mm_nn_res("down_proj", act, wdown_g.reshape(ff, d), x1)
        saved.append((xs, h1, p, o, rsum, mix, x1, h2, up, act))
        gathered.append((win_g, wout_g, wup_g, wdown_g))
        if l + 1 < depth:
            win_g, wout_g = _gather_finish(f"gather_in{l + 1}_finish", forwarded, x2)
        xs = x2

    dx, dxb, loss_vec = _loss_head(xs, loss_target[0])
    loss = lax.psum(loss_vec[0, 0], MESH_AXES)

    exchanges = []
    small = [None] * depth
    for l in reversed(range(depth)):
        xs0, h1, p, o, rsum, mix, x1, h2, up, act = saved[l]
        win_g, wout_g, wup_g, wdown_g = gathered[l]
        wout_full = wout_g.reshape(d, d)
        wdown_full = wdown_g.reshape(ff, d)
        g1 = attn_norm_g[l][None]
        g2 = ffn_norm_g[l][None]
        gq, gk = q_norm_g[l][None], k_norm_g[l][None]
        ga = out_norm_a_g[l][:, None, :]
        gs = sgu_norm_g[l][:, None, :]
        gb = out_norm_b_g[l][:, None, :]
        d_wdown = _mm_tn_plain("down_proj_dw", act, dxb)
        dact = _mm_nt_plain("down_proj_dx", dxb, wdown_full)
        dup, d_cw, d_cb = _conv_gate_bwd(up, conv_w_all[l], conv_b_all[l], dact)
        d_wup = _mm_tn_blocked("up_proj_dw", h2, dup, N_DEV, halves=True)
        exchanges.append((l, ("w_up", "w_down"), _exchange_start(
            f"grad_ffn{l}_start", [d_wup, d_wdown.reshape(N_DEV, ff // N_DEV, d)], d_wup)))
        dh2 = _mm_nt_blocked("up_proj_dx", dup, wup_g, halves=True, after=exchanges[-1][2].token)
        dx, dxb, d_g2 = _rmsnorm_bwd("ffn_norm_bwd", dh2, x1, g2, dx)
        d_wout = _mm_tn_plain("out_proj_dw", mix, dxb)
        dmix = _mm_nt_plain("out_proj_dx", dxb, wout_full)
        dq, dk, dv, d_gq, d_gk, d_ga = _attn_bwd(p, gq, gk, ga, o, rsum, dmix, n_heads)
        du, dvs, d_gs, d_sw, d_sb, d_gb = _sgu_bwd(p, gs, sgu_w[l], sgu_b_col[l], gb, dmix, sgu_col0, n_heads)
        dp = jnp.concatenate([dq, dk, dv, du, dvs], axis=-1)
        d_win = _mm_tn_blocked("in_proj_dw", h1, dp, N_DEV)
        exchanges.append((l, ("w_in", "w_out"), _exchange_start(
            f"grad_mix{l}_start", [d_win, d_wout.reshape(N_DEV, d // N_DEV, d)], d_win)))
        dh1 = _mm_nt_blocked("in_proj_dx", dp, win_g, after=exchanges[-1][2].token)
        dx, dxb, d_g1 = _rmsnorm_bwd("attn_norm_bwd", dh1, xs0, g1, dx)
        small[l] = dict(attn_norm_g=d_g1[0], q_norm_g=d_gq[0], k_norm_g=d_gk[0], sgu_norm_g=d_gs[:, 0], sgu_w=d_sw,
                        sgu_b=d_sb[..., 0], out_norm_a_g=d_ga[:, 0], out_norm_b_g=d_gb[:, 0], ffn_norm_g=d_g2[0],
                        conv_w=jnp.transpose(d_cw, (1, 0, 2)).reshape(CONV_WIDTH, f2), conv_b=d_cb.reshape(f2))
    grad_x = dx[None]

    small_names = ["attn_norm_g", "q_norm_g", "k_norm_g", "sgu_norm_g", "sgu_w", "sgu_b", "out_norm_a_g",
                   "out_norm_b_g", "ffn_norm_g", "conv_b", "conv_w"]
    small_g = [jnp.stack([small[l][n] for l in range(depth)]) for n in small_names]
    small_sent = _broadcast_start("grad_small_start", [_pack(small_g)], dx)

    res = {}
    big = dict(w_in=(w_in, m_w_in, v_w_in), w_out=(w_out, m_w_out, v_w_out), w_up=(w_up, m_w_up, v_w_up),
               w_down=(w_down, m_w_down, v_w_down))
    after = [small_sent.token]
    for l, names, ex in exchanges:
        stage = "ffn" if names[0] == "w_up" else "mix"
        landed = _exchange_finish(f"grad_{stage}{l}_finish", ex, after)
        after = []
        for name, parts in zip(names, landed):
            w, m, v = big[name]
            res[name] = _adamw(f"adamw_{name}", w, m, v, parts, l, res.get(name))
            after.append(res[name][0])
    small_all = _exchange_finish("grad_small_finish", small_sent, after)[0]
    small_sum = _unpack(_sum_slots(small_all), small_g)
    g_small = dict(zip(small_names, small_sum))
    cwn = conv_w.shape[2]
    g_small["conv_w"] = lax.dynamic_slice_in_dim(g_small["conv_w"], my_slot * cwn, cwn, axis=2)
    small_w = dict(attn_norm_g=(attn_norm_g, m_attn_norm_g, v_attn_norm_g), q_norm_g=(q_norm_g, m_q_norm_g, v_q_norm_g),
                   k_norm_g=(k_norm_g, m_k_norm_g, v_k_norm_g), sgu_norm_g=(sgu_norm_g, m_sgu_norm_g, v_sgu_norm_g),
                   sgu_w=(sgu_w, m_sgu_w, v_sgu_w), sgu_b=(sgu_b, m_sgu_b, v_sgu_b),
                   out_norm_a_g=(out_norm_a_g, m_out_norm_a_g, v_out_norm_a_g),
                   out_norm_b_g=(out_norm_b_g, m_out_norm_b_g, v_out_norm_b_g),
                   ffn_norm_g=(ffn_norm_g, m_ffn_norm_g, v_ffn_norm_g), conv_b=(conv_b, m_conv_b, v_conv_b),
                   conv_w=(conv_w, m_conv_w, v_conv_w))
    like = [small_w[n][0] for n in small_names]
    pw = _pack([small_w[n][0] for n in small_names])
    pm = _pack([small_w[n][1] for n in small_names])
    pv = _pack([small_w[n][2] for n in small_names])
    pg = _pack([g_small[n].reshape(small_w[n][0].shape) for n in small_names])
    pd, pnm, pnv = _adamw_small(pw, pg, pm, pv)
    for n, dlt, nm, nv in zip(small_names, _unpack(pd, like), _unpack(pnm, like), _unpack(pnv, like)):
        res[n] = (g_small[n].reshape(small_w[n][0].shape), dlt, nm, nv)

    order = ["attn_norm_g", "w_in", "q_norm_g", "k_norm_g", "sgu_norm_g", "sgu_w", "sgu_b", "out_norm_a_g",
             "out_norm_b_g", "w_out", "ffn_norm_g", "w_up", "conv_w", "conv_b", "w_down"]
    outs = [loss, grad_x]
    for field in range(4):
        outs += [res[n][field] for n in order]
    return tuple(outs)
```

```python
import functools

import jax
import jax.numpy as jnp
from jax import lax
from jax.experimental import pallas as pl
from jax.experimental.pallas import tpu as pltpu

F32 = jnp.float32
BF16 = jnp.bfloat16
EPS = 1e-6
HEAD_DIM = 128
TILE = 128
ATTN_VMEM_MB = 58
ATTN_HEADS_PER_STEP = 4
N_GROUPS = 8
CONV_WIDTH = 3
N_DEV = 8
MESH_AXES = ("x", "y", "c")
MIB = 1024 * 1024

ADAM_LR = 0.001
ADAM_B1 = 0.9
ADAM_B2 = 0.999
ADAM_EPS = 1e-08
ADAM_WD = 0.01
ADAM_STEP = 10
ADAMW_TILE_ELEMS = 160 * 1024

NT_DIMS = (((1,), (1,)), ((), ()))
NN_DIMS = (((1,), (0,)), ((), ()))
TN_DIMS = (((0,), (0,)), ((), ()))


def _cparams(sem, vmem_mb=48):
    return pltpu.CompilerParams(dimension_semantics=sem, vmem_limit_bytes=vmem_mb * MIB)


def _pick(n, cands):
    for c in cands:
        if n % c == 0:
            return c
    return n


def _mm(name, grid, ins, in_specs, out_shape, out_spec, dims, has_res=False, parts=None, vmem_mb=56, after=None):
    n_in = 2 + has_res + (after is not None)
    if after is not None:
        ins = tuple(ins) + (after,)
        in_specs = list(in_specs) + [pl.BlockSpec(after.shape, lambda *_: (0, 0))]

    def body(*refs):
        a_ref, b_ref = refs[:2]
        o_ref = refs[n_in]
        if parts is None:
            acc = lax.dot_general(a_ref[...], b_ref[...], dims, preferred_element_type=F32)
        else:
            acc = None
            for part in parts:
                a, b = part(a_ref, b_ref)
                prod = lax.dot_general(a, b, dims, preferred_element_type=F32)
                acc = prod if acc is None else acc + prod
        if has_res:
            acc = acc + refs[2][...]
        o_ref[...] = acc.astype(o_ref.dtype)

    return pl.pallas_call(
        body, name=name, grid=grid, in_specs=in_specs, out_specs=out_spec, out_shape=out_shape,
        compiler_params=_cparams(("parallel",) * len(grid), vmem_mb),
    )(*ins)


def _rows_for(m, row_bytes, budget):
    return _pick(m, tuple(t for t in (2048, 1024, 512, 256, 128) if t * row_bytes <= budget))


def _mm_nn_blocked(name, a, wb, out_dtype, halves=False):
    m, k = a.shape
    nb, _, bn = wb.shape
    tm = _rows_for(m, bn * jnp.dtype(out_dtype).itemsize, 6 * MIB)
    a_spec = pl.BlockSpec((tm, k), lambda j, i: (i, 0))
    b_spec = pl.BlockSpec((None, k, bn), lambda j, i: (j, 0, 0))
    if halves:
        hb = nb // 2
        out_shape = jax.ShapeDtypeStruct((2, m, hb * bn), out_dtype)
        o_spec = pl.BlockSpec((None, tm, bn), lambda j, i: (j // hb, i, j % hb))
    else:
        out_shape = jax.ShapeDtypeStruct((m, nb * bn), out_dtype)
        o_spec = pl.BlockSpec((tm, bn), lambda j, i: (i, j))
    return _mm(name, (nb, m // tm), (a, wb), [a_spec, b_spec], out_shape, o_spec, NN_DIMS)


def _mm_nn_res(name, a, w, res):
    m, k = a.shape
    n = w.shape[1]
    tm = _pick(m, (512, 256, 128))
    tn = _rows_for(n, k * 2, 12 * MIB)
    a_spec = pl.BlockSpec((tm, k), lambda j, i: (i, 0))
    b_spec = pl.BlockSpec((k, tn), lambda j, i: (0, j))
    r_spec = pl.BlockSpec((tm, tn), lambda j, i: (i, j))
    o_spec = pl.BlockSpec((tm, tn), lambda j, i: (i, j))
    return _mm(name, (n // tn, m // tm), (a, w, res), [a_spec, b_spec, r_spec], jax.ShapeDtypeStruct((m, n), F32),
               o_spec, NN_DIMS, has_res=True)


def _mm_nt_blocked(name, dy, wb, halves=False, after=None):
    nb, n, bn = wb.shape
    m = dy.shape[-2]
    tm = _pick(m, (512, 256, 128))
    tn = _rows_for(n, nb * bn * 2, 12 * MIB)
    if halves:
        hb = nb // 2
        a_spec = pl.BlockSpec((2, tm, hb * bn), lambda j, i: (0, i, 0))
        a_part = lambda kk: (lambda a_ref: a_ref[kk // hb, :, (kk % hb) * bn:(kk % hb + 1) * bn])
    else:
        a_spec = pl.BlockSpec((tm, nb * bn), lambda j, i: (i, 0))
        a_part = lambda kk: (lambda a_ref: a_ref[:, kk * bn:(kk + 1) * bn])
    parts = [(lambda a_ref, b_ref, kk=kk, sel=a_part(kk): (sel(a_ref), b_ref[kk])) for kk in range(nb)]
    b_spec = pl.BlockSpec((nb, tn, bn), lambda j, i: (0, j, 0))
    o_spec = pl.BlockSpec((tm, tn), lambda j, i: (i, j))
    return _mm(name, (n // tn, m // tm), (dy, wb), [a_spec, b_spec], jax.ShapeDtypeStruct((m, n), F32), o_spec,
               NT_DIMS, parts=parts, after=after)


def _mm_nt_plain(name, dy, w, out_dtype=F32):
    m, k = dy.shape
    n = w.shape[0]
    tm = _rows_for(m, k * 2, 8 * MIB)
    tn = _pick(n, (512, 256, 128))
    a_spec = pl.BlockSpec((tm, k), lambda j, i: (i, 0))
    b_spec = pl.BlockSpec((tn, k), lambda j, i: (j, 0))
    o_spec = pl.BlockSpec((tm, tn), lambda j, i: (i, j))
    return _mm(name, (n // tn, m // tm), (dy, w), [a_spec, b_spec], jax.ShapeDtypeStruct((m, n), out_dtype), o_spec,
               NT_DIMS)


def _mm_tn_blocked(name, a, dy, nb, halves=False):
    s, k1 = a.shape
    bn = (dy.shape[-1] * (2 if halves else 1)) // nb
    tm = _rows_for(k1, bn * 2, 6 * MIB)
    a_spec = pl.BlockSpec((s, tm), lambda j, i: (0, i))
    if halves:
        hb = nb // 2
        b_spec = pl.BlockSpec((None, s, bn), lambda j, i: (j // hb, 0, j % hb))
    else:
        b_spec = pl.BlockSpec((s, bn), lambda j, i: (0, j))
    o_spec = pl.BlockSpec((None, tm, bn), lambda j, i: (j, i, 0))
    return _mm(name, (nb, k1 // tm), (a, dy), [a_spec, b_spec], jax.ShapeDtypeStruct((nb, k1, bn), BF16), o_spec,
               TN_DIMS)


def _mm_tn_plain(name, a, dy):
    s, k1 = a.shape
    n = dy.shape[1]
    tm = _pick(k1, (512, 256, 128))
    tn = _rows_for(n, s * 2, 8 * MIB)
    a_spec = pl.BlockSpec((s, tm), lambda i, j: (0, i))
    b_spec = pl.BlockSpec((s, tn), lambda i, j: (0, j))
    o_spec = pl.BlockSpec((tm, tn), lambda i, j: (i, j))
    return _mm(name, (k1 // tm, n // tn), (a, dy), [a_spec, b_spec], jax.ShapeDtypeStruct((k1, n), BF16), o_spec,
               TN_DIMS)


def _rstd(x):
    return lax.rsqrt(jnp.mean(x * x, axis=-1, keepdims=True) + EPS)


def _norm_bwd(dy, xhat, r, g):
    dxhat = dy * g
    return r * (dxhat - xhat * jnp.mean(dxhat * xhat, axis=-1, keepdims=True))


def _rmsnorm_fwd(name, x, g):
    s, d = x.shape
    tr = _pick(s, (256, 128))

    def body(x_ref, g_ref, h_ref):
        xv = x_ref[...]
        h_ref[...] = (xv * _rstd(xv) * g_ref[...]).astype(BF16)

    return pl.pallas_call(
        body, name=name, grid=(s // tr,),
        in_specs=[pl.BlockSpec((tr, d), lambda i: (i, 0)), pl.BlockSpec((1, d), lambda i: (0, 0))],
        out_specs=pl.BlockSpec((tr, d), lambda i: (i, 0)),
        out_shape=jax.ShapeDtypeStruct((s, d), BF16), compiler_params=_cparams(("parallel",)),
    )(x, g)


def _rmsnorm_bwd(name, dh, x, g, dres):
    s, d = x.shape
    tr = _pick(s, (256, 128))

    def body(dh_ref, x_ref, g_ref, dres_ref, dx_ref, dxb_ref, dg_ref):
        xv = x_ref[...]
        r = _rstd(xv)
        xhat = xv * r
        dhv = dh_ref[...]
        dx = dres_ref[...] + _norm_bwd(dhv, xhat, r, g_ref[...])
        dx_ref[...] = dx
        dxb_ref[...] = dx.astype(BF16)
        part = jnp.sum(dhv * xhat, axis=0, keepdims=True)

        @pl.when(pl.program_id(0) == 0)
        def _():
            dg_ref[...] = part

        @pl.when(pl.program_id(0) > 0)
        def _():
            dg_ref[...] += part

    row = pl.BlockSpec((tr, d), lambda i: (i, 0))
    vec = pl.BlockSpec((1, d), lambda i: (0, 0))
    return pl.pallas_call(
        body, name=name, grid=(s // tr,), in_specs=[row, row, vec, row], out_specs=(row, row, vec),
        out_shape=(jax.ShapeDtypeStruct((s, d), F32), jax.ShapeDtypeStruct((s, d), BF16),
                   jax.ShapeDtypeStruct((1, d), F32)),
        compiler_params=_cparams(("arbitrary",)),
    )(dh, x, g, dres)


def _loss_head(y, target):
    s, d = y.shape
    tr = _pick(s, (256, 128))

    def body(y_ref, t_ref, dy_ref, dyb_ref, loss_ref):
        err = y_ref[...] - t_ref[...]
        dy = err * (1.0 / d)
        dy_ref[...] = dy
        dyb_ref[...] = dy.astype(BF16)
        part = 0.5 * jnp.sum(jnp.mean(err * err, axis=-1, keepdims=True), axis=0, keepdims=True)
        part = jnp.broadcast_to(part, (1, 128))

        @pl.when(pl.program_id(0) == 0)
        def _():
            loss_ref[...] = part

        @pl.when(pl.program_id(0) > 0)
        def _():
            loss_ref[...] += part

    row = pl.BlockSpec((tr, d), lambda i: (i, 0))
    return pl.pallas_call(
        body, name="loss_head", grid=(s // tr,), in_specs=[row, row],
        out_specs=(row, row, pl.BlockSpec((1, 128), lambda i: (0, 0))),
        out_shape=(jax.ShapeDtypeStruct((s, d), F32), jax.ShapeDtypeStruct((s, d), BF16),
                   jax.ShapeDtypeStruct((1, 128), F32)),
        compiler_params=_cparams(("arbitrary",)),
    )(y, target)


def _split_dot(x, tri):
    hi = x.astype(BF16)
    lo = (x - hi.astype(F32)).astype(BF16)
    return (jnp.dot(hi, tri, preferred_element_type=F32) + jnp.dot(lo, tri, preferred_element_type=F32))


def _tile_iotas():
    row = lax.broadcasted_iota(jnp.int32, (TILE, TILE), 0)
    col = lax.broadcasted_iota(jnp.int32, (TILE, TILE), 1)
    return row, col


def _sb_logits(qi, kb, mask):
    z = lax.dot_general(qi, kb, NT_DIMS, preferred_element_type=F32) * (HEAD_DIM ** -0.5)
    sp = jnp.log1p(jnp.exp(-jnp.abs(z)))
    lb = jnp.minimum(z, 0.0) - sp
    l1m = -jnp.maximum(z, 0.0) - sp
    if mask is not None:
        l1m = jnp.where(mask, l1m, 0.0)
    return lb, l1m


def _attn_fwd(p, gq, gk, ga, n_heads):
    s = p.shape[0]
    nq = s // TILE

    hp = ATTN_HEADS_PER_STEP
    wd = hp * HEAD_DIM

    def body(q_ref, k_ref, v_ref, gq_ref, gk_ref, ga_ref, att_ref, o_ref, r_ref, qn_s, kn_s, vb_s):
        heads = [slice(hh * HEAD_DIM, (hh + 1) * HEAD_DIM) for hh in range(hp)]
        for hd in heads:
            qv = q_ref[:, hd]
            qn_s[:, hd] = (qv * _rstd(qv) * gq_ref[...]).astype(BF16)
            kv = k_ref[:, hd]
            kn_s[:, hd] = (kv * _rstd(kv) * gk_ref[...]).astype(BF16)
        vb_s[...] = v_ref[...].astype(BF16)
        row, col = _tile_iotas()
        causal = col < row
        upper_ones = jnp.concatenate([(row > col).astype(BF16), jnp.ones((TILE, TILE), BF16)], axis=1)

        def tiles(rows, key_blocks, states, mask):
            chains = [(hi, hd, keys) for hi, hd in enumerate(heads) for keys in key_blocks]
            logits = [_sb_logits(qn_s[rows, hd], kn_s[keys, hd], mask) for _, hd, keys in chains]
            sums = [_split_dot(l1m, upper_ones) for _, l1m in logits]
            carry = [c for _, c in states]
            probs = []
            for (hi, _, _), (lb, _), sm in zip(chains, logits, sums):
                a = jnp.exp(lb + sm[:, :TILE] + carry[hi])
                carry[hi] = carry[hi] + sm[:, TILE:]
                probs.append((a if mask is None else jnp.where(mask, a, 0.0)).astype(BF16))
            outs = [jnp.dot(a, vb_s[keys, hd], preferred_element_type=F32) for a, (_, hd, keys) in zip(probs, chains)]
            acc = [o_acc for o_acc, _ in states]
            for (hi, _, _), o in zip(chains, outs):
                acc[hi] = acc[hi] + o
            return tuple(zip(acc, carry))

        def key_block(b):
            return pl.ds(pl.multiple_of(b * TILE, TILE), TILE)

        def qblock(i, _):
            rows = pl.ds(pl.multiple_of(i * TILE, TILE), TILE)
            zero = jnp.zeros((TILE, HEAD_DIM), F32)
            states = tiles(rows, [rows], tuple((zero, zero) for _ in heads), causal)
            states = lax.cond(i % 2 == 1, lambda st: tiles(rows, [key_block(i - 1)], st, None), lambda st: st, states)
            top = i - i % 2

            def kblocks(jj, states):
                return tiles(rows, [key_block(top - 1 - 2 * jj), key_block(top - 2 - 2 * jj)], states, None)

            states = lax.fori_loop(0, i // 2, kblocks, states)
            for hh, (hd, (o_acc, c)) in enumerate(zip(heads, states)):
                o_ref[rows, hd] = o_acc
                r_ref[rows, hd] = c
                att_ref[rows, hd] = (o_acc * _rstd(o_acc) * ga_ref[hh]).astype(BF16)
            return 0

        lax.fori_loop(0, nq, qblock, 0)

    col_blk = lambda off: pl.BlockSpec((s, wd), lambda h: (0, off + h))
    vec = pl.BlockSpec((1, HEAD_DIM), lambda h: (0, 0))
    hvec = pl.BlockSpec((hp, 1, HEAD_DIM), lambda h: (h, 0, 0))
    out = pl.BlockSpec((s, wd), lambda h: (0, h))
    w = n_heads * HEAD_DIM
    steps = n_heads // hp
    return pl.pallas_call(
        body, name="attn_fwd", grid=(steps,),
        in_specs=[col_blk(0), col_blk(steps), col_blk(2 * steps), vec, vec, hvec],
        out_specs=(out, out, out),
        out_shape=(jax.ShapeDtypeStruct((s, w), BF16), jax.ShapeDtypeStruct((s, w), F32),
                   jax.ShapeDtypeStruct((s, w), F32)),
        scratch_shapes=[pltpu.VMEM((s, wd), BF16)] * 3,
        compiler_params=_cparams(("parallel",), ATTN_VMEM_MB),
    )(p, p, p, gq, gk, ga)


def _attn_bwd(p, gq, gk, ga, o, rsum, dmix, n_heads):
    s = p.shape[0]
    nq = s // TILE

    hp = ATTN_HEADS_PER_STEP
    wd = hp * HEAD_DIM
    scale = HEAD_DIM ** -0.5

    def body(q_ref, k_ref, v_ref, gq_ref, gk_ref, ga_ref, o_ref, r_ref, dm_ref,
             dq_ref, dk_ref, dv_ref, dgq_ref, dgk_ref, dga_ref,
             qn_s, kn_s, vb_s, do_s, dqn_s, dkn_s, dv_s):
        step = pl.program_id(0)
        gqv, gkv = gq_ref[...], gk_ref[...]
        heads = [slice(hh * HEAD_DIM, (hh + 1) * HEAD_DIM) for hh in range(hp)]
        for hh, hd in enumerate(heads):
            qv = q_ref[:, hd]
            qn_s[:, hd] = (qv * _rstd(qv) * gqv).astype(BF16)
            kv = k_ref[:, hd]
            kn_s[:, hd] = (kv * _rstd(kv) * gkv).astype(BF16)
            ov = o_ref[:, hd]
            ro = _rstd(ov)
            ohat = ov * ro
            dm = dm_ref[:, hd]
            dga_ref[hh] = jnp.sum(dm * ohat, axis=0, keepdims=True)
            do_s[:, hd] = _norm_bwd(dm, ohat, ro, ga_ref[hh]).astype(BF16)
        vb_s[...] = v_ref[...].astype(BF16)
        dkn_s[...] = jnp.zeros_like(dkn_s)
        dv_s[...] = jnp.zeros_like(dv_s)
        row, col = _tile_iotas()
        causal = col < row
        ones = jnp.ones((TILE, TILE), BF16)
        incl_ones = jnp.concatenate([(row <= col).astype(BF16), ones], axis=1)
        excl_ones = jnp.concatenate([(row < col).astype(BF16), ones], axis=1)

        def tiles(rows, key_blocks, states, mask):
            chains = [(hi, hd, keys) for hi, hd in enumerate(heads) for keys in key_blocks]
            qis = [qn_s[rows, hd] for hd in heads]
            dois = [do_s[rows, hd] for hd in heads]
            logits = [_sb_logits(qis[hi], kn_s[keys, hd], mask) for hi, hd, keys in chains]
            sums = [_split_dot(l1m, incl_ones) for _, l1m in logits]
            das = [lax.dot_general(dois[hi], vb_s[keys, hd], NT_DIMS, preferred_element_type=F32)
                   for hi, hd, keys in chains]
            pfx = [st[1] for st in states]
            probs, dss = [], []
            for (hi, hd, _), (lb, _), sm, da in zip(chains, logits, sums, das):
                a = jnp.exp(lb + (r_ref[rows, hd] - pfx[hi] - sm[:, :TILE]))
                pfx[hi] = pfx[hi] + sm[:, TILE:]
                a = a if mask is None else jnp.where(mask, a, 0.0)
                probs.append(a.astype(BF16))
                dss.append(da * a)
            dsums = [_split_dot(ds, excl_ones) for ds in dss]
            pc = [st[2] for st in states]
            dzs = []
            for (hi, _, _), (lb, _), ds, dsm in zip(chains, logits, dss, dsums):
                dl1m = pc[hi] + dsm[:, :TILE]
                pc[hi] = pc[hi] + dsm[:, TILE:]
                dl1m = dl1m if mask is None else jnp.where(mask, dl1m, 0.0)
                beta = jnp.exp(lb)
                dzs.append(((ds * (1.0 - beta) - dl1m * beta) * scale).astype(BF16))
            dqs = [jnp.dot(dz, kn_s[keys, hd], preferred_element_type=F32) for dz, (_, hd, keys) in zip(dzs, chains)]
            for dz, a, (hi, hd, keys) in zip(dzs, probs, chains):
                dkn_s[keys, hd] += lax.dot_general(dz, qis[hi], TN_DIMS, preferred_element_type=F32)
                dv_s[keys, hd] += lax.dot_general(a, dois[hi], TN_DIMS, preferred_element_type=F32)
            dq_acc = [st[0] for st in states]
            for (hi, _, _), dq in zip(chains, dqs):
                dq_acc[hi] = dq_acc[hi] + dq
            return tuple(zip(dq_acc, pfx, pc))

        def key_block(b):
            return pl.ds(pl.multiple_of(b * TILE, TILE), TILE)

        def qblock(i, _):
            rows = pl.ds(pl.multiple_of(i * TILE, TILE), TILE)
            zero = jnp.zeros((TILE, HEAD_DIM), F32)

            def kblocks(jj, states):
                return tiles(rows, [key_block(2 * jj), key_block(2 * jj + 1)], states, None)

            states = lax.fori_loop(0, i // 2, kblocks, tuple((zero, zero, zero) for _ in heads))
            states = lax.cond(i % 2 == 1, lambda st: tiles(rows, [key_block(i - 1)], st, None), lambda st: st, states)
            states = tiles(rows, [rows], states, causal)
            for hd, (dq_acc, _, _) in zip(heads, states):
                dqn_s[rows, hd] = dq_acc
            return 0

        lax.fori_loop(0, nq, qblock, 0)

        def norm_in_bwd(x_ref, g, dn_s, dx_ref, dg_ref):
            part = jnp.zeros((1, HEAD_DIM), F32)
            for hd in heads:
                xv = x_ref[:, hd]
                r = _rstd(xv)
                xhat = xv * r
                dn = dn_s[:, hd]
                dx_ref[:, hd] = _norm_bwd(dn, xhat, r, g).astype(BF16)
                part = part + jnp.sum(dn * xhat, axis=0, keepdims=True)

            @pl.when(step == 0)
            def _():
                dg_ref[...] = part

            @pl.when(step > 0)
            def _():
                dg_ref[...] += part

        norm_in_bwd(q_ref, gqv, dqn_s, dq_ref, dgq_ref)
        norm_in_bwd(k_ref, gkv, dkn_s, dk_ref, dgk_ref)
        dv_ref[...] = dv_s[...].astype(BF16)

    once = pl.Buffered(1)
    steps = n_heads // hp
    col_blk = lambda off: pl.BlockSpec((s, wd), lambda h: (0, off + h), pipeline_mode=once)
    vec = pl.BlockSpec((1, HEAD_DIM), lambda h: (0, 0))
    hvec = pl.BlockSpec((hp, 1, HEAD_DIM), lambda h: (h, 0, 0))
    blk = pl.BlockSpec((s, wd), lambda h: (0, h), pipeline_mode=once)
    w = n_heads * HEAD_DIM
    big = jax.ShapeDtypeStruct((s, w), BF16)
    return pl.pallas_call(
        body, name="attn_bwd", grid=(steps,),
        in_specs=[col_blk(0), col_blk(steps), col_blk(2 * steps), vec, vec, hvec, blk, blk, blk],
        out_specs=(blk, blk, blk, vec, vec, hvec),
        out_shape=(big, big, big, jax.ShapeDtypeStruct((1, HEAD_DIM), F32), jax.ShapeDtypeStruct((1, HEAD_DIM), F32),
                   jax.ShapeDtypeStruct((n_heads, 1, HEAD_DIM), F32)),
        scratch_shapes=[pltpu.VMEM((s, wd), BF16)] * 4 + [pltpu.VMEM((s, wd), F32)] * 3,
        compiler_params=_cparams(("arbitrary",), ATTN_VMEM_MB),
    )(p, p, p, gq, gk, ga, o, rsum, dmix)


_INV_SQRT2 = 0.7071067811865476
_INV_SQRT_2PI = 0.3989422804014327


def _gelu(x):
    return 0.5 * x * (1.0 + lax.erf(x * _INV_SQRT2))


def _gelu_grad(x):
    return 0.5 * (1.0 + lax.erf(x * _INV_SQRT2)) + x * (_INV_SQRT_2PI * jnp.exp(-0.5 * x * x))


def _sgu_fwd(p, gs, w_s, b_s, gb, col0):
    s = p.shape[0]
    n_chunks = s // TILE

    def body(u_ref, v_ref, gs_ref, w_ref, b_ref, gb_ref, out_ref, vs_s):
        vg = _gelu(v_ref[...])
        vs_s[...] = (vg * _rstd(vg) * gs_ref[...]).astype(BF16)
        row, col = _tile_iotas()
        wt = jnp.where(col <= row, w_ref[...], 0.0).astype(BF16)
        bcol = b_ref[...]
        gbv = gb_ref[...]

        def chunk(c, _):
            rows = pl.ds(pl.multiple_of(c * TILE, TILE), TILE)
            mixed = jnp.dot(wt, vs_s[rows, :], preferred_element_type=F32) + bcol
            sg = _gelu(u_ref[rows, :]) * mixed
            out_ref[rows, :] = (sg * _rstd(sg) * gbv).astype(BF16)
            return 0

        lax.fori_loop(0, n_chunks, chunk, 0)

    col_blk = lambda off: pl.BlockSpec((s, HEAD_DIM), lambda g: (0, off + g))
    gvec = pl.BlockSpec((None, 1, HEAD_DIM), lambda g: (g, 0, 0))
    return pl.pallas_call(
        body, name="sgu_fwd", grid=(N_GROUPS,),
        in_specs=[col_blk(col0), col_blk(col0 + N_GROUPS), gvec,
                  pl.BlockSpec((None, TILE, TILE), lambda g: (g, 0, 0)),
                  pl.BlockSpec((None, TILE, 1), lambda g: (g, 0, 0)), gvec],
        out_specs=pl.BlockSpec((s, HEAD_DIM), lambda g: (0, g)),
        out_shape=jax.ShapeDtypeStruct((s, N_GROUPS * HEAD_DIM), BF16),
        scratch_shapes=[pltpu.VMEM((s, HEAD_DIM), BF16)],
        compiler_params=_cparams(("parallel",)),
    )(p, p, gs, w_s, b_s, gb)


def _sgu_bwd(p, gs, w_s, b_s, gb, dmix, col0, dm_col0):
    s = p.shape[0]
    n_chunks = s // TILE

    def body(u_ref, v_ref, gs_ref, w_ref, b_ref, gb_ref, dm_ref,
             du_ref, dv_ref, dgs_ref, dw_ref, db_ref, dgb_ref, vs_s, dvs_s):
        gsv = gs_ref[...]
        gbv = gb_ref[...]
        vg = _gelu(v_ref[...])
        vs_s[...] = (vg * _rstd(vg) * gsv).astype(BF16)
        row, col = _tile_iotas()
        causal = col <= row
        wt = jnp.where(causal, w_ref[...], 0.0).astype(BF16)
        bcol = b_ref[...]

        def chunk(c, carry):
            dw_acc, db_acc, dgb_acc = carry
            rows = pl.ds(pl.multiple_of(c * TILE, TILE), TILE)
            vs = vs_s[rows, :]
            mixed = jnp.dot(wt, vs, preferred_element_type=F32) + bcol
            u_pre = u_ref[rows, :]
            u = _gelu(u_pre)
            sg = u * mixed
            rs = _rstd(sg)
            sghat = sg * rs
            dm = dm_ref[rows, :]
            dsg = _norm_bwd(dm, sghat, rs, gbv)
            dgb_acc = dgb_acc + jnp.sum(dm * sghat, axis=0, keepdims=True)
            du_ref[rows, :] = (dsg * mixed * _gelu_grad(u_pre)).astype(BF16)
            dmixed = dsg * u
            db_acc = db_acc + jnp.sum(dmixed, axis=1, keepdims=True)
            dmb = dmixed.astype(BF16)
            dw_acc = dw_acc + lax.dot_general(dmb, vs, NT_DIMS, preferred_element_type=F32)
            dvs_s[rows, :] = lax.dot_general(wt, dmb, TN_DIMS, preferred_element_type=F32)
            return dw_acc, db_acc, dgb_acc

        dw_acc, db_acc, dgb_acc = lax.fori_loop(
            0, n_chunks, chunk,
            (jnp.zeros((TILE, TILE), F32), jnp.zeros((TILE, 1), F32), jnp.zeros((1, HEAD_DIM), F32)))
        dw_ref[...] = jnp.where(causal, dw_acc, 0.0)
        db_ref[...] = db_acc
        dgb_ref[...] = dgb_acc
        v_pre = v_ref[...]
        vg = _gelu(v_pre)
        rv = _rstd(vg)
        vhat = vg * rv
        dvs = dvs_s[...]
        dgs_ref[...] = jnp.sum(dvs * vhat, axis=0, keepdims=True)
        dv_ref[...] = (_norm_bwd(dvs, vhat, rv, gsv) * _gelu_grad(v_pre)).astype(BF16)

    col_blk = lambda off: pl.BlockSpec((s, HEAD_DIM), lambda g: (0, off + g))
    gvec = pl.BlockSpec((None, 1, HEAD_DIM), lambda g: (g, 0, 0))
    wspec = pl.BlockSpec((None, TILE, TILE), lambda g: (g, 0, 0))
    bspec = pl.BlockSpec((None, TILE, 1), lambda g: (g, 0, 0))
    blk = pl.BlockSpec((s, HEAD_DIM), lambda g: (0, g))
    big = jax.ShapeDtypeStruct((s, N_GROUPS * HEAD_DIM), BF16)
    gshape = jax.ShapeDtypeStruct((N_GROUPS, 1, HEAD_DIM), F32)
    return pl.pallas_call(
        body, name="sgu_bwd", grid=(N_GROUPS,),
        in_specs=[col_blk(col0), col_blk(col0 + N_GROUPS), gvec, wspec, bspec, gvec, col_blk(dm_col0)],
        out_specs=(blk, blk, gvec, wspec, bspec, gvec),
        out_shape=(big, big, gshape, jax.ShapeDtypeStruct((N_GROUPS, TILE, TILE), F32),
                   jax.ShapeDtypeStruct((N_GROUPS, TILE, 1), F32), gshape),
        scratch_shapes=[pltpu.VMEM((s, HEAD_DIM), BF16), pltpu.VMEM((s, HEAD_DIM), F32)],
        compiler_params=_cparams(("parallel",)),
    )(p, p, gs, w_s, b_s, gb, dmix)


def _shift_down(x, n):
    rows = lax.broadcasted_iota(jnp.int32, x.shape, 0)
    return jnp.where(rows >= n, pltpu.roll(x, n, 0), 0.0)


def _shift_up(x, n):
    s = x.shape[0]
    rows = lax.broadcasted_iota(jnp.int32, x.shape, 0)
    return jnp.where(rows < s - n, pltpu.roll(x, s - n, 0), 0.0)


def _conv(x, w, b):
    return b + w[0:1, :] * _shift_down(x, 2) + w[1:2, :] * _shift_down(x, 1) + w[2:3, :] * x


def _conv_specs(s, tn):
    xspec = pl.BlockSpec((2, s, tn), lambda j: (0, 0, j))
    wspec = pl.BlockSpec((2, CONV_WIDTH, tn), lambda j: (0, 0, j))
    bspec = pl.BlockSpec((2, 1, tn), lambda j: (0, 0, j))
    return xspec, wspec, bspec


def _conv_gate_fwd(up, cw, cb):
    _, s, f = up.shape
    tn = _pick(f, (256, 128))

    def body(x_ref, w_ref, b_ref, act_ref):
        gate = _conv(x_ref[0], w_ref[0], b_ref[0])
        val = _conv(x_ref[1], w_ref[1], b_ref[1])
        act_ref[...] = (gate * jax.nn.sigmoid(gate) * val).astype(BF16)

    xspec, wspec, bspec = _conv_specs(s, tn)
    return pl.pallas_call(
        body, name="conv_gate_fwd", grid=(f // tn,), in_specs=[xspec, wspec, bspec],
        out_specs=pl.BlockSpec((s, tn), lambda j: (0, j)), out_shape=jax.ShapeDtypeStruct((s, f), BF16),
        compiler_params=_cparams(("parallel",)),
    )(up, cw, cb)


def _conv_gate_bwd(up, cw, cb, dact):
    _, s, f = up.shape
    tn = _pick(f, (256, 128))

    def body(x_ref, w_ref, b_ref, da_ref, dx_ref, dw_ref, db_ref):
        xg, xv = x_ref[0], x_ref[1]
        wg, wv = w_ref[0], w_ref[1]
        gate = _conv(xg, wg, b_ref[0])
        val = _conv(xv, wv, b_ref[1])
        sig = jax.nn.sigmoid(gate)
        da = da_ref[...]
        dval = da * (gate * sig)
        dgate = da * val * (sig * (1.0 + gate * (1.0 - sig)))
        for half, (x, w, dy) in enumerate(((xg, wg, dgate), (xv, wv, dval))):
            dx_ref[half] = (w[2:3, :] * dy + w[1:2, :] * _shift_up(dy, 1) + w[0:1, :] * _shift_up(dy, 2)).astype(BF16)
            dw_ref[half, 0:1, :] = jnp.sum(dy * _shift_down(x, 2), axis=0, keepdims=True)
            dw_ref[half, 1:2, :] = jnp.sum(dy * _shift_down(x, 1), axis=0, keepdims=True)
            dw_ref[half, 2:3, :] = jnp.sum(dy * x, axis=0, keepdims=True)
            db_ref[half] = jnp.sum(dy, axis=0, keepdims=True)

    xspec, wspec, bspec = _conv_specs(s, tn)
    return pl.pallas_call(
        body, name="conv_gate_bwd", grid=(f // tn,),
        in_specs=[xspec, wspec, bspec, pl.BlockSpec((s, tn), lambda j: (0, j))],
        out_specs=(xspec, wspec, bspec),
        out_shape=(jax.ShapeDtypeStruct((2, s, f), BF16), jax.ShapeDtypeStruct((2, CONV_WIDTH, f), F32),
                   jax.ShapeDtypeStruct((2, 1, f), F32)),
        compiler_params=_cparams(("parallel",)),
    )(up, cw, cb, dact)


def _mesh_pos():
    return lax.axis_index("x"), lax.axis_index("y"), lax.axis_index("c")


def _remote(src, dst, send_sem, recv_sem, to):
    return pltpu.make_async_remote_copy(src_ref=src, dst_ref=dst, send_sem=send_sem, recv_sem=recv_sem,
                                        device_id=to, device_id_type=pl.DeviceIdType.MESH)


HBM_SPEC = pl.BlockSpec(memory_space=pltpu.HBM)
SEM_SPEC = pl.BlockSpec(memory_space=pltpu.SEMAPHORE)
ANY_SPEC = pl.BlockSpec(memory_space=pl.ANY)
TOKEN_SPEC = pl.BlockSpec(memory_space=pltpu.VMEM)
TOKEN_SHAPE = jax.ShapeDtypeStruct((8, 128), F32)
DATAFLOW = pltpu.SideEffectType.DATAFLOW_SIDE_EFFECTING
GATHER_PLANE = (2, 4, 6)


def _slot(pos):
    return 4 * pos[0] + 2 * pos[1] + pos[2]


def _flip(pos, k):
    return (pos[0] ^ ((k >> 2) & 1), pos[1] ^ ((k >> 1) & 1), pos[2] ^ (k & 1))


def _hbm(a):
    return pltpu.with_memory_space_constraint(a, pltpu.HBM)


def _hbm_shapes(arrays):
    return tuple(pltpu.HBM(a.shape, a.dtype) for a in arrays)


class _Split:
    def __init__(self, n, outs, has_sems):
        k = 2 * n if has_sems else 0
        self.n = n
        self.sems = list(outs[:k])
        self.bufs = list(outs[k:k + 2 * n])
        self.token = outs[-1]


def _split_call(name, body, bufs, sems_in, makes_sems, after):
    n = len(bufs) // 2
    k = 2 * n if makes_sems else 0
    m = len(sems_in)
    afters = list(after) if isinstance(after, (list, tuple)) else [after]
    na = len(afters)

    def wrapped(*refs):
        srcs, dsts = refs[:n], refs[n:2 * n]
        s_in = refs[2 * n:2 * n + m]
        s_out = refs[2 * n + m + na:2 * n + m + na + k]
        token, local_sems = refs[-2], refs[-1]
        body(srcs, dsts, s_in, s_out, local_sems)
        token[...] = jnp.zeros_like(token)

    outs = pl.pallas_call(
        wrapped, name=name,
        out_shape=(pltpu.SemaphoreType.DMA(()),) * k + _hbm_shapes(bufs) + (TOKEN_SHAPE,),
        in_specs=[HBM_SPEC] * (2 * n) + [SEM_SPEC] * m + [ANY_SPEC] * na,
        out_specs=(SEM_SPEC,) * k + (HBM_SPEC,) * (2 * n) + (TOKEN_SPEC,),
        input_output_aliases={i: k + i for i in range(2 * n)},
        scratch_shapes=[pltpu.SemaphoreType.DMA((n,))],
        compiler_params=pltpu.CompilerParams(has_side_effects=DATAFLOW),
    )(*[_hbm(b) for b in bufs], *sems_in, *afters)
    return _Split(n, outs, makes_sems)


def _wait_slots(land, count, send_sem, recv_sem, me, send=False, recv=False):
    span = land.at[pl.ds(0, count)]
    cp = _remote(span, span, send_sem, recv_sem, me)
    if send:
        cp.wait_send()
    if recv:
        cp.wait_recv()


def _gather_start(name, shards, after):
    n = len(shards)
    my_slot = _slot(_mesh_pos())
    lands = [lax.dynamic_update_slice(lax.empty((N_DEV,) + w.shape, w.dtype), w[None], (my_slot, 0, 0)) for w in shards]

    def body(srcs, dsts, _, sems, local_sems):
        me = _mesh_pos()
        for a in range(n):
            for k in (1,) + GATHER_PLANE:
                _remote(srcs[a], dsts[a].at[_slot(me)], sems[a], sems[n + a], _flip(me, k)).start()

    return _split_call(name, body, list(shards) + lands, [], True, after)


def _gather_forward(name, started, after):
    n = started.n

    def body(srcs, dsts, sems_a, sems_b, local_sems):
        me = _mesh_pos()
        sibling = _flip(me, 1)
        for a in range(n):
            _wait_slots(dsts[a], 4, sems_a[a], sems_a[n + a], me, recv=True)
            for k in GATHER_PLANE:
                block = dsts[a].at[_slot(_flip(me, k))]
                _remote(block, block, sems_b[a], sems_b[n + a], sibling).start()
        for a in range(n):
            _wait_slots(dsts[a], 4, sems_a[a], sems_a[n + a], me, send=True)

    return _split_call(name, body, started.bufs, started.sems, True, after)


def _gather_finish(name, forwarded, after):
    n = forwarded.n

    def body(srcs, dsts, sems_b, _, local_sems):
        me = _mesh_pos()
        for a in range(n):
            _wait_slots(dsts[a], 3, sems_b[a], sems_b[n + a], me, send=True, recv=True)

    return _split_call(name, body, forwarded.bufs, forwarded.sems, False, after).bufs[n:]


def _exchange_start(name, blocked, after):
    n = len(blocked)
    my_slot = _slot(_mesh_pos())
    lands = [lax.dynamic_update_slice(lax.empty(w.shape, w.dtype), lax.dynamic_slice_in_dim(w, my_slot, 1, 0),
                                      (my_slot, 0, 0)) for w in blocked]

    def body(srcs, dsts, _, sems, local_sems):
        me = _mesh_pos()
        for a in range(n):
            for k in range(1, N_DEV):
                peer = _flip(me, k)
                _remote(srcs[a].at[_slot(peer)], dsts[a].at[_slot(me)], sems[a], sems[n + a], peer).start()

    return _split_call(name, body, list(blocked) + lands, [], True, after)


def _exchange_finish(name, started, after):
    n = started.n

    def body(srcs, dsts, sems, _, local_sems):
        me = _mesh_pos()
        for a in range(n):
            _wait_slots(dsts[a], N_DEV - 1, sems[a], sems[n + a], me, send=True, recv=True)

    return _split_call(name, body, started.bufs, started.sems, False, after).bufs[n:]


def _broadcast_start(name, arrays, after):
    n = len(arrays)
    my_slot = _slot(_mesh_pos())
    lands = [lax.dynamic_update_slice(lax.empty((N_DEV,) + w.shape, w.dtype), w[None], (my_slot, 0, 0)) for w in arrays]

    def body(srcs, dsts, _, sems, local_sems):
        me = _mesh_pos()
        for a in range(n):
            for k in range(1, N_DEV):
                _remote(srcs[a], dsts[a].at[_slot(me)], sems[a], sems[n + a], _flip(me, k)).start()

    return _split_call(name, body, list(arrays) + lands, [], True, after)


def _adamw_math(w, g, m, v):
    m = ADAM_B1 * m + (1.0 - ADAM_B1) * g
    v = ADAM_B2 * v + (1.0 - ADAM_B2) * (g * g)
    m_hat = m / (1.0 - ADAM_B1 ** ADAM_STEP)
    v_hat = v / (1.0 - ADAM_B2 ** ADAM_STEP)
    delta = -ADAM_LR * (m_hat / (jnp.sqrt(v_hat) + ADAM_EPS) + ADAM_WD * w)
    return delta, m, v


def _adamw(name, w, m, v, parts, layer, prev=None):
    _, r, c = w.shape
    tr = _pick(r, tuple(t for t in (256, 128, 64, 32, 16) if t * c <= ADAMW_TILE_ELEMS))
    n_prev = 0 if prev is None else 4

    def body(*refs):
        w_ref, m_ref, v_ref, p_ref = refs[:4]
        g_ref, d_ref, nm_ref, nv_ref = refs[4 + n_prev:]
        g = p_ref[0].astype(F32)
        for src in range(1, N_DEV):
            g = g + p_ref[src].astype(F32)
        delta, nm, nv = _adamw_math(w_ref[...], g, m_ref[...], v_ref[...])
        g_ref[...] = g
        d_ref[...] = delta
        nm_ref[...] = nm
        nv_ref[...] = nv

    wspec = pl.BlockSpec((None, tr, c), lambda i: (layer, i, 0))
    pspec = pl.BlockSpec((N_DEV, tr, c), lambda i: (0, i, 0))
    shp = jax.ShapeDtypeStruct(w.shape, F32)
    return pl.pallas_call(
        body, name=name, grid=(r // tr,), in_specs=[wspec] * 3 + [pspec] + [ANY_SPEC] * n_prev,
        out_specs=(wspec,) * 4, out_shape=(shp,) * 4, input_output_aliases={4 + j: j for j in range(n_prev)},
        compiler_params=_cparams(("parallel",)),
    )(w, m, v, parts, *([] if prev is None else prev))


PACK_TILE = 8 * 128


def _pack(arrays):
    flat = []
    for a in arrays:
        v = a.reshape(-1)
        pad = (-v.shape[0]) % PACK_TILE
        flat.append(jnp.pad(v, (0, pad)) if pad else v)
    return jnp.concatenate(flat).reshape(-1, 128)


def _unpack(buf, like):
    flat = buf.reshape(-1)
    out, off = [], 0
    for a in like:
        n = 1
        for dim in a.shape:
            n *= dim
        out.append(flat[off:off + n].reshape(a.shape))
        off += n + (-n) % PACK_TILE
    return out


def _sum_slots(gathered):
    _, r, c = gathered.shape

    def body(x_ref, o_ref):
        acc = x_ref[0]
        for src in range(1, N_DEV):
            acc = acc + x_ref[src]
        o_ref[...] = acc

    return pl.pallas_call(body, name="small_grad_sum", out_shape=jax.ShapeDtypeStruct((r, c), F32))(gathered)


def _adamw_small(w, g, m, v):
    shp = jax.ShapeDtypeStruct(w.shape, F32)

    def body(w_ref, g_ref, m_ref, v_ref, d_ref, nm_ref, nv_ref):
        delta, nm, nv = _adamw_math(w_ref[...], g_ref[...], m_ref[...], v_ref[...])
        d_ref[...] = delta
        nm_ref[...] = nm
        nv_ref[...] = nv

    return pl.pallas_call(body, name="adamw_small", out_shape=(shp,) * 3)(w, g, m, v)


def kernel(x, attn_norm_g, w_in, q_norm_g, k_norm_g, sgu_norm_g, sgu_w, sgu_b, out_norm_a_g, out_norm_b_g, w_out, ffn_norm_g, w_up, conv_w, conv_b, w_down, loss_target, m_attn_norm_g, m_w_in, m_q_norm_g, m_k_norm_g, m_sgu_norm_g, m_sgu_w, m_sgu_b, m_out_norm_a_g, m_out_norm_b_g, m_w_out, m_ffn_norm_g, m_w_up, m_conv_w, m_conv_b, m_w_down, v_attn_norm_g, v_w_in, v_q_norm_g, v_k_norm_g, v_sgu_norm_g, v_sgu_w, v_sgu_b, v_out_norm_a_g, v_out_norm_b_g, v_w_out, v_ffn_norm_g, v_w_up, v_conv_w, v_conv_b, v_w_down):
    depth = w_in.shape[0]
    s, d = x.shape[1], x.shape[2]
    n_heads = (d // 2) // HEAD_DIM
    sgu_col0 = 3 * n_heads
    f2 = w_up.shape[2] * N_DEV
    ff = f2 // 2
    my_slot = 4 * lax.axis_index("x") + 2 * lax.axis_index("y") + lax.axis_index("c")

    wb = [(w_in[l].astype(BF16), w_out[l].astype(BF16), w_up[l].astype(BF16), w_down[l].astype(BF16))
          for l in range(depth)]
    groups = {"in0": [wb[0][0]], "mix0": [wb[0][1], conv_w.reshape(depth * CONV_WIDTH, -1), wb[0][2]], "down0": [wb[0][3]]}
    for l in range(1, depth):
        groups[f"in{l}"] = [wb[l][0], wb[l][1]]
        groups[f"mix{l}"] = [wb[l][2], wb[l][3]]
    token = attn_norm_g
    started = {}
    for gname, group in groups.items():
        started[gname] = _gather_start(f"gather_{gname}_start", group, token)
        token = started[gname].token

    conv_b_all = conv_b.reshape(depth, 2, 1, ff)
    sgu_b_col = sgu_b[..., None]
    forwarded = _gather_forward("gather_in0_forward", started["in0"], token)
    win_g = _gather_finish("gather_in0_finish", forwarded, forwarded.token)[0]
    wout_g = None

    xs = x[0]
    saved = []
    gathered = []
    for l in range(depth):
        g1 = attn_norm_g[l][None]
        g2 = ffn_norm_g[l][None]
        gq, gk = q_norm_g[l][None], k_norm_g[l][None]
        ga = out_norm_a_g[l][:, None, :]
        gs = sgu_norm_g[l][:, None, :]
        gb = out_norm_b_g[l][:, None, :]
        h1 = _rmsnorm_fwd("attn_norm_fwd", xs, g1)
        p = _mm_nn_blocked("in_proj", h1, win_g, F32)
        att, o, rsum = _attn_fwd(p, gq, gk, ga, n_heads)
        forwarded = _gather_forward(f"gather_mix{l}_forward", started[f"mix{l}"], att)
        sg = _sgu_fwd(p, gs, sgu_w[l], sgu_b_col[l], gb, sgu_col0)
        mix = jnp.concatenate([att, sg], axis=-1)
        landed = _gather_finish(f"gather_mix{l}_finish", forwarded, mix)
        if l == 0:
            wout_g, cw, wup_g = landed
            cw = jnp.transpose(cw.reshape(N_DEV, depth, CONV_WIDTH, -1), (1, 2, 0, 3)).reshape(depth, CONV_WIDTH, 2, ff)
            conv_w_all = jnp.transpose(cw, (0, 2, 1, 3))
        else:
            wup_g, wdown_g = landed
        x1 = _mm_nn_res("out_proj", mix, wout_g.reshape(d, d), xs)
        h2 = _rmsnorm_fwd("ffn_norm_fwd", x1, g2)
        up = _mm_nn_blocked("up_proj", h2, wup_g, F32, halves=True)
        if l == 0:
            forwarded = _gather_forward("gather_down0_forward", started["down0"], up)
        act = _conv_gate_fwd(up, conv_w_all[l], conv_b_all[l])
        if l == 0:
            wdown_g = _gather_finish("gather_down0_finish", forwarded, act)[0]
        if l + 1 < depth:
            forwarded = _gather_forward(f"gather_in{l + 1}_forward", started[f"in{l + 1}"], act)
        x2 = _mm_nn_res("down_proj", act, wdown_g.reshape(ff, d), x1)
        saved.append((xs, h1, p, o, rsum, mix, x1, h2, up, act))
        gathered.append((win_g, wout_g, wup_g, wdown_g))
        if l + 1 < depth:
            win_g, wout_g = _gather_finish(f"gather_in{l + 1}_finish", forwarded, x2)
        xs = x2

    dx, dxb, loss_vec = _loss_head(xs, loss_target[0])
    loss = lax.psum(loss_vec[0, 0], MESH_AXES)

    exchanges = []
    small = [None] * depth
    for l in reversed(range(depth)):
        xs0, h1, p, o, rsum, mix, x1, h2, up, act = saved[l]
        win_g, wout_g, wup_g, wdown_g = gathered[l]
        wout_full = wout_g.reshape(d, d)
        wdown_full = wdown_g.reshape(ff, d)
        g1 = attn_norm_g[l][None]
        g2 = ffn_norm_g[l][None]
        gq, gk = q_norm_g[l][None], k_norm_g[l][None]
        ga = out_norm_a_g[l][:, None, :]
        gs = sgu_norm_g[l][:, None, :]
        gb = out_norm_b_g[l][:, None, :]
        d_wdown = _mm_tn_plain("down_proj_dw", act, dxb)
        dact = _mm_nt_plain("down_proj_dx", dxb, wdown_full)
        dup, d_cw, d_cb = _conv_gate_bwd(up, conv_w_all[l], conv_b_all[l], dact)
        d_wup = _mm_tn_blocked("up_proj_dw", h2, dup, N_DEV, halves=True)
        exchanges.append((l, ("w_up", "w_down"), _exchange_start(
            f"grad_ffn{l}_start", [d_wup, d_wdown.reshape(N_DEV, ff // N_DEV, d)], d_wup)))
        dh2 = _mm_nt_blocked("up_proj_dx", dup, wup_g, halves=True, after=exchanges[-1][2].token)
        dx, dxb, d_g2 = _rmsnorm_bwd("ffn_norm_bwd", dh2, x1, g2, dx)
        d_wout = _mm_tn_plain("out_proj_dw", mix, dxb)
        dmix = _mm_nt_plain("out_proj_dx", dxb, wout_full)
        dq, dk, dv, d_gq, d_gk, d_ga = _attn_bwd(p, gq, gk, ga, o, rsum, dmix, n_heads)
        du, dvs, d_gs, d_sw, d_sb, d_gb = _sgu_bwd(p, gs, sgu_w[l], sgu_b_col[l], gb, dmix, sgu_col0, n_heads)
        dp = jnp.concatenate([dq, dk, dv, du, dvs], axis=-1)
        d_win = _mm_tn_blocked("in_proj_dw", h1, dp, N_DEV)
        exchanges.append((l, ("w_in", "w_out"), _exchange_start(
            f"grad_mix{l}_start", [d_win, d_wout.reshape(N_DEV, d // N_DEV, d)], d_win)))
        dh1 = _mm_nt_blocked("in_proj_dx", dp, win_g, after=exchanges[-1][2].token)
        dx, dxb, d_g1 = _rmsnorm_bwd("attn_norm_bwd", dh1, xs0, g1, dx)
        small[l] = dict(attn_norm_g=d_g1[0], q_norm_g=d_gq[0], k_norm_g=d_gk[0], sgu_norm_g=d_gs[:, 0], sgu_w=d_sw,
                        sgu_b=d_sb[..., 0], out_norm_a_g=d_ga[:, 0], out_norm_b_g=d_gb[:, 0], ffn_norm_g=d_g2[0],
                        conv_w=jnp.transpose(d_cw, (1, 0, 2)).reshape(CONV_WIDTH, f2), conv_b=d_cb.reshape(f2))
    grad_x = dx[None]

    small_names = ["attn_norm_g", "q_norm_g", "k_norm_g", "sgu_norm_g", "sgu_w", "sgu_b", "out_norm_a_g",
                   "out_norm_b_g", "ffn_norm_g", "conv_b", "conv_w"]
    small_g = [jnp.stack([small[l][n] for l in range(depth)]) for n in small_names]
    small_sent = _broadcast_start("grad_small_start", [_pack(small_g)], dx)

    res = {}
    big = dict(w_in=(w_in, m_w_in, v_w_in), w_out=(w_out, m_w_out, v_w_out), w_up=(w_up, m_w_up, v_w_up),
               w_down=(w_down, m_w_down, v_w_down))
    after = [small_sent.token]
    for l, names, ex in exchanges:
        stage = "ffn" if names[0] == "w_up" else "mix"
        landed = _exchange_finish(f"grad_{stage}{l}_finish", ex, after)
        after = []
        for name, parts in zip(names, landed):
            w, m, v = big[name]
            res[name] = _adamw(f"adamw_{name}", w, m, v, parts, l, res.get(name))
            after.append(res[name][0])
    small_all = _exchange_finish("grad_small_finish", small_sent, after)[0]
    small_sum = _unpack(_sum_slots(small_all), small_g)
    g_small = dict(zip(small_names, small_sum))
    cwn = conv_w.shape[2]
    g_small["conv_w"] = lax.dynamic_slice_in_dim(g_small["conv_w"], my_slot * cwn, cwn, axis=2)
    small_w = dict(attn_norm_g=(attn_norm_g, m_attn_norm_g, v_attn_norm_g), q_norm_g=(q_norm_g, m_q_norm_g, v_q_norm_g),
                   k_norm_g=(k_norm_g, m_k_norm_g, v_k_norm_g), sgu_norm_g=(sgu_norm_g, m_sgu_norm_g, v_sgu_norm_g),
                   sgu_w=(sgu_w, m_sgu_w, v_sgu_w), sgu_b=(sgu_b, m_sgu_b, v_sgu_b),
                   out_norm_a_g=(out_norm_a_g, m_out_norm_a_g, v_out_norm_a_g),
                   out_norm_b_g=(out_norm_b_g, m_out_norm_b_g, v_out_norm_b_g),
                   ffn_norm_g=(ffn_norm_g, m_ffn_norm_g, v_ffn_norm_g), conv_b=(conv_b, m_conv_b, v_conv_b),
                   conv_w=(conv_w, m_conv_w, v_conv_w))
    like = [small_w[n][0] for n in small_names]
    pw = _pack([small_w[n][0] for n in small_names])
    pm = _pack([small_w[n][1] for n in small_names])
    pv = _pack([small_w[n][2] for n in small_names])
    pg = _pack([g_small[n].reshape(small_w[n][0].shape) for n in small_names])
    pd, pnm, pnv = _adamw_small(pw, pg, pm, pv)
    for n, dlt, nm, nv in zip(small_names, _unpack(pd, like), _unpack(pnm, like), _unpack(pnv, like)):
        res[n] = (g_small[n].reshape(small_w[n][0].shape), dlt, nm, nv)

    order = ["attn_norm_g", "w_in", "q_norm_g", "k_norm_g", "sgu_norm_g", "sgu_w", "sgu_b", "out_norm_a_g",
             "out_norm_b_g", "w_out", "ffn_norm_g", "w_up", "conv_w", "conv_b", "w_down"]
    outs = [loss, grad_x]
    for field in range(4):
        outs += [res[n][field] for n in order]
    return tuple(outs)
```

```python
import functools

import jax
import jax.numpy as jnp
from jax import lax
from jax.experimental import pallas as pl
from jax.experimental.pallas import tpu as pltpu

F32 = jnp.float32
BF16 = jnp.bfloat16
EPS = 1e-6
HEAD_DIM = 128
TILE = 128
ATTN_VMEM_MB = 58
ATTN_HEADS_PER_STEP = 4
N_GROUPS = 8
CONV_WIDTH = 3
N_DEV = 8
MESH_AXES = ("x", "y", "c")
MIB = 1024 * 1024

ADAM_LR = 0.001
ADAM_B1 = 0.9
ADAM_B2 = 0.999
ADAM_EPS = 1e-08
ADAM_WD = 0.01
ADAM_STEP = 10
ADAMW_TILE_ELEMS = 160 * 1024

NT_DIMS = (((1,), (1,)), ((), ()))
NN_DIMS = (((1,), (0,)), ((), ()))
TN_DIMS = (((0,), (0,)), ((), ()))


def _cparams(sem, vmem_mb=48):
    return pltpu.CompilerParams(dimension_semantics=sem, vmem_limit_bytes=vmem_mb * MIB)


def _pick(n, cands):
    for c in cands:
        if n % c == 0:
            return c
    return n


def _mm(name, grid, ins, in_specs, out_shape, out_spec, dims, has_res=False, parts=None, vmem_mb=56, after=None):
    n_in = 2 + has_res + (after is not None)
    if after is not None:
        ins = tuple(ins) + (after,)
        in_specs = list(in_specs) + [pl.BlockSpec(after.shape, lambda *_: (0, 0))]

    def body(*refs):
        a_ref, b_ref = refs[:2]
        o_ref = refs[n_in]
        if parts is None:
            acc = lax.dot_general(a_ref[...], b_ref[...], dims, preferred_element_type=F32)
        else:
            acc = None
            for part in parts:
                a, b = part(a_ref, b_ref)
                prod = lax.dot_general(a, b, dims, preferred_element_type=F32)
                acc = prod if acc is None else acc + prod
        if has_res:
            acc = acc + refs[2][...]
        o_ref[...] = acc.astype(o_ref.dtype)

    return pl.pallas_call(
        body, name=name, grid=grid, in_specs=in_specs, out_specs=out_spec, out_shape=out_shape,
        compiler_params=_cparams(("parallel",) * len(grid), vmem_mb),
    )(*ins)


def _rows_for(m, row_bytes, budget):
    return _pick(m, tuple(t for t in (2048, 1024, 512, 256, 128) if t * row_bytes <= budget))


def _mm_nn_blocked(name, a, wb, out_dtype, halves=False):
    m, k = a.shape
    nb, _, bn = wb.shape
    tm = _rows_for(m, bn * jnp.dtype(out_dtype).itemsize, 6 * MIB)
    a_spec = pl.BlockSpec((tm, k), lambda j, i: (i, 0))
    b_spec = pl.BlockSpec((None, k, bn), lambda j, i: (j, 0, 0))
    if halves:
        hb = nb // 2
        out_shape = jax.ShapeDtypeStruct((2, m, hb * bn), out_dtype)
        o_spec = pl.BlockSpec((None, tm, bn), lambda j, i: (j // hb, i, j % hb))
    else:
        out_shape = jax.ShapeDtypeStruct((m, nb * bn), out_dtype)
        o_spec = pl.BlockSpec((tm, bn), lambda j, i: (i, j))
    return _mm(name, (nb, m // tm), (a, wb), [a_spec, b_spec], out_shape, o_spec, NN_DIMS)


def _mm_nn_res(name, a, w, res):
    m, k = a.shape
    n = w.shape[1]
    tm = _pick(m, (512, 256, 128))
    tn = _rows_for(n, k * 2, 12 * MIB)
    a_spec = pl.BlockSpec((tm, k), lambda j, i: (i, 0))
    b_spec = pl.BlockSpec((k, tn), lambda j, i: (0, j))
    r_spec = pl.BlockSpec((tm, tn), lambda j, i: (i, j))
    o_spec = pl.BlockSpec((tm, tn), lambda j, i: (i, j))
    return _mm(name, (n // tn, m // tm), (a, w, res), [a_spec, b_spec, r_spec], jax.ShapeDtypeStruct((m, n), F32),
               o_spec, NN_DIMS, has_res=True)


def _mm_nt_blocked(name, dy, wb, halves=False, after=None):
    nb, n, bn = wb.shape
    m = dy.shape[-2]
    tm = _pick(m, (512, 256, 128))
    tn = _rows_for(n, nb * bn * 2, 12 * MIB)
    if halves:
        hb = nb // 2
        a_spec = pl.BlockSpec((2, tm, hb * bn), lambda j, i: (0, i, 0))
        a_part = lambda kk: (lambda a_ref: a_ref[kk // hb, :, (kk % hb) * bn:(kk % hb + 1) * bn])
    else:
        a_spec = pl.BlockSpec((tm, nb * bn), lambda j, i: (i, 0))
        a_part = lambda kk: (lambda a_ref: a_ref[:, kk * bn:(kk + 1) * bn])
    parts = [(lambda a_ref, b_ref, kk=kk, sel=a_part(kk): (sel(a_ref), b_ref[kk])) for kk in range(nb)]
    b_spec = pl.BlockSpec((nb, tn, bn), lambda j, i: (0, j, 0))
    o_spec = pl.BlockSpec((tm, tn), lambda j, i: (i, j))
    return _mm(name, (n // tn, m // tm), (dy, wb), [a_spec, b_spec], jax.ShapeDtypeStruct((m, n), F32), o_spec,
               NT_DIMS, parts=parts, after=after)


def _mm_nt_plain(name, dy, w, out_dtype=F32):
    m, k = dy.shape
    n = w.shape[0]
    tm = _rows_for(m, k * 2, 8 * MIB)
    tn = _pick(n, (512, 256, 128))
    a_spec = pl.BlockSpec((tm, k), lambda j, i: (i, 0))
    b_spec = pl.BlockSpec((tn, k), lambda j, i: (j, 0))
    o_spec = pl.BlockSpec((tm, tn), lambda j, i: (i, j))
    return _mm(name, (n // tn, m // tm), (dy, w), [a_spec, b_spec], jax.ShapeDtypeStruct((m, n), out_dtype), o_spec,
               NT_DIMS)


def _mm_tn_blocked(name, a, dy, nb, halves=False):
    s, k1 = a.shape
    bn = (dy.shape[-1] * (2 if halves else 1)) // nb
    tm = _rows_for(k1, bn * 2, 6 * MIB)
    a_spec = pl.BlockSpec((s, tm), lambda j, i: (0, i))
    if halves:
        hb = nb // 2
        b_spec = pl.BlockSpec((None, s, bn), lambda j, i: (j // hb, 0, j % hb))
    else:
        b_spec = pl.BlockSpec((s, bn), lambda j, i: (0, j))
    o_spec = pl.BlockSpec((None, tm, bn), lambda j, i: (j, i, 0))
    return _mm(name, (nb, k1 // tm), (a, dy), [a_spec, b_spec], jax.ShapeDtypeStruct((nb, k1, bn), BF16), o_spec,
               TN_DIMS)


def _mm_tn_plain(name, a, dy):
    s, k1 = a.shape
    n = dy.shape[1]
    tm = _pick(k1, (512, 256, 128))
    tn = _rows_for(n, s * 2, 8 * MIB)
    a_spec = pl.BlockSpec((s, tm), lambda i, j: (0, i))
    b_spec = pl.BlockSpec((s, tn), lambda i, j: (0, j))
    o_spec = pl.BlockSpec((tm, tn), lambda i, j: (i, j))
    return _mm(name, (k1 // tm, n // tn), (a, dy), [a_spec, b_spec], jax.ShapeDtypeStruct((k1, n), BF16), o_spec,
               TN_DIMS)


def _rstd(x):
    return lax.rsqrt(jnp.mean(x * x, axis=-1, keepdims=True) + EPS)


def _norm_bwd(dy, xhat, r, g):
    dxhat = dy * g
    return r * (dxhat - xhat * jnp.mean(dxhat * xhat, axis=-1, keepdims=True))


def _rmsnorm_fwd(name, x, g):
    s, d = x.shape
    tr = _pick(s, (256, 128))

    def body(x_ref, g_ref, h_ref):
        xv = x_ref[...]
        h_ref[...] = (xv * _rstd(xv) * g_ref[...]).astype(BF16)

    return pl.pallas_call(
        body, name=name, grid=(s // tr,),
        in_specs=[pl.BlockSpec((tr, d), lambda i: (i, 0)), pl.BlockSpec((1, d), lambda i: (0, 0))],
        out_specs=pl.BlockSpec((tr, d), lambda i: (i, 0)),
        out_shape=jax.ShapeDtypeStruct((s, d), BF16), compiler_params=_cparams(("parallel",)),
    )(x, g)


def _rmsnorm_bwd(name, dh, x, g, dres):
    s, d = x.shape
    tr = _pick(s, (256, 128))

    def body(dh_ref, x_ref, g_ref, dres_ref, dx_ref, dxb_ref, dg_ref):
        xv = x_ref[...]
        r = _rstd(xv)
        xhat = xv * r
        dhv = dh_ref[...]
        dx = dres_ref[...] + _norm_bwd(dhv, xhat, r, g_ref[...])
        dx_ref[...] = dx
        dxb_ref[...] = dx.astype(BF16)
        part = jnp.sum(dhv * xhat, axis=0, keepdims=True)

        @pl.when(pl.program_id(0) == 0)
        def _():
            dg_ref[...] = part

        @pl.when(pl.program_id(0) > 0)
        def _():
            dg_ref[...] += part

    row = pl.BlockSpec((tr, d), lambda i: (i, 0))
    vec = pl.BlockSpec((1, d), lambda i: (0, 0))
    return pl.pallas_call(
        body, name=name, grid=(s // tr,), in_specs=[row, row, vec, row], out_specs=(row, row, vec),
        out_shape=(jax.ShapeDtypeStruct((s, d), F32), jax.ShapeDtypeStruct((s, d), BF16),
                   jax.ShapeDtypeStruct((1, d), F32)),
        compiler_params=_cparams(("arbitrary",)),
    )(dh, x, g, dres)


def _loss_head(y, target):
    s, d = y.shape
    tr = _pick(s, (256, 128))

    def body(y_ref, t_ref, dy_ref, dyb_ref, loss_ref):
        err = y_ref[...] - t_ref[...]
        dy = err * (1.0 / d)
        dy_ref[...] = dy
        dyb_ref[...] = dy.astype(BF16)
        part = 0.5 * jnp.sum(jnp.mean(err * err, axis=-1, keepdims=True), axis=0, keepdims=True)
        part = jnp.broadcast_to(part, (1, 128))

        @pl.when(pl.program_id(0) == 0)
        def _():
            loss_ref[...] = part

        @pl.when(pl.program_id(0) > 0)
        def _():
            loss_ref[...] += part

    row = pl.BlockSpec((tr, d), lambda i: (i, 0))
    return pl.pallas_call(
        body, name="loss_head", grid=(s // tr,), in_specs=[row, row],
        out_specs=(row, row, pl.BlockSpec((1, 128), lambda i: (0, 0))),
        out_shape=(jax.ShapeDtypeStruct((s, d), F32), jax.ShapeDtypeStruct((s, d), BF16),
                   jax.ShapeDtypeStruct((1, 128), F32)),
        compiler_params=_cparams(("arbitrary",)),
    )(y, target)


def _split_dot(x, tri):
    hi = x.astype(BF16)
    lo = (x - hi.astype(F32)).astype(BF16)
    return (jnp.dot(hi, tri, preferred_element_type=F32) + jnp.dot(lo, tri, preferred_element_type=F32))


def _tile_iotas():
    row = lax.broadcasted_iota(jnp.int32, (TILE, TILE), 0)
    col = lax.broadcasted_iota(jnp.int32, (TILE, TILE), 1)
    return row, col


def _sb_logits(qi, kb, mask):
    z = lax.dot_general(qi, kb, NT_DIMS, preferred_element_type=F32) * (HEAD_DIM ** -0.5)
    sp = jnp.log1p(jnp.exp(-jnp.abs(z)))
    lb = jnp.minimum(z, 0.0) - sp
    l1m = -jnp.maximum(z, 0.0) - sp
    if mask is not None:
        l1m = jnp.where(mask, l1m, 0.0)
    return lb, l1m


def _attn_fwd(p, gq, gk, ga, n_heads):
    s = p.shape[0]
    nq = s // TILE

    hp = ATTN_HEADS_PER_STEP
    wd = hp * HEAD_DIM

    def body(q_ref, k_ref, v_ref, gq_ref, gk_ref, ga_ref, att_ref, o_ref, r_ref, qn_s, kn_s, vb_s):
        heads = [slice(hh * HEAD_DIM, (hh + 1) * HEAD_DIM) for hh in range(hp)]
        for hd in heads:
            qv = q_ref[:, hd]
            qn_s[:, hd] = (qv * _rstd(qv) * gq_ref[...]).astype(BF16)
            kv = k_ref[:, hd]
            kn_s[:, hd] = (kv * _rstd(kv) * gk_ref[...]).astype(BF16)
        vb_s[...] = v_ref[...].astype(BF16)
        row, col = _tile_iotas()
        causal = col < row
        upper_ones = jnp.concatenate([(row > col).astype(BF16), jnp.ones((TILE, TILE), BF16)], axis=1)

        def tiles(rows, key_blocks, states, mask):
            chains = [(hi, hd, keys) for hi, hd in enumerate(heads) for keys in key_blocks]
            logits = [_sb_logits(qn_s[rows, hd], kn_s[keys, hd], mask) for _, hd, keys in chains]
            sums = [_split_dot(l1m, upper_ones) for _, l1m in logits]
            carry = [c for _, c in states]
            probs = []
            for (hi, _, _), (lb, _), sm in zip(chains, logits, sums):
                a = jnp.exp(lb + sm[:, :TILE] + carry[hi])
                carry[hi] = carry[hi] + sm[:, TILE:]
                probs.append((a if mask is None else jnp.where(mask, a, 0.0)).astype(BF16))
            outs = [jnp.dot(a, vb_s[keys, hd], preferred_element_type=F32) for a, (_, hd, keys) in zip(probs, chains)]
            acc = [o_acc for o_acc, _ in states]
            for (hi, _, _), o in zip(chains, outs):
                acc[hi] = acc[hi] + o
            return tuple(zip(acc, carry))

        def key_block(b):
            return pl.ds(pl.multiple_of(b * TILE, TILE), TILE)

        def qblock(i, _):
            rows = pl.ds(pl.multiple_of(i * TILE, TILE), TILE)
            zero = jnp.zeros((TILE, HEAD_DIM), F32)
            states = tiles(rows, [rows], tuple((zero, zero) for _ in heads), causal)
            states = lax.cond(i % 2 == 1, lambda st: tiles(rows, [key_block(i - 1)], st, None), lambda st: st, states)
            top = i - i % 2

            def kblocks(jj, states):
                return tiles(rows, [key_block(top - 1 - 2 * jj), key_block(top - 2 - 2 * jj)], states, None)

            states = lax.fori_loop(0, i // 2, kblocks, states)
            for hh, (hd, (o_acc, c)) in enumerate(zip(heads, states)):
                o_ref[rows, hd] = o_acc
                r_ref[rows, hd] = c
                att_ref[rows, hd] = (o_acc * _rstd(o_acc) * ga_ref[hh]).astype(BF16)
            return 0

        lax.fori_loop(0, nq, qblock, 0)

    col_blk = lambda off: pl.BlockSpec((s, wd), lambda h: (0, off + h))
    vec = pl.BlockSpec((1, HEAD_DIM), lambda h: (0, 0))
    hvec = pl.BlockSpec((hp, 1, HEAD_DIM), lambda h: (h, 0, 0))
    out = pl.BlockSpec((s, wd), lambda h: (0, h))
    w = n_heads * HEAD_DIM
    steps = n_heads // hp
    return pl.pallas_call(
        body, name="attn_fwd", grid=(steps,),
        in_specs=[col_blk(0), col_blk(steps), col_blk(2 * steps), vec, vec, hvec],
        out_specs=(out, out, out),
        out_shape=(jax.ShapeDtypeStruct((s, w), BF16), jax.ShapeDtypeStruct((s, w), F32),
                   jax.ShapeDtypeStruct((s, w), F32)),
        scratch_shapes=[pltpu.VMEM((s, wd), BF16)] * 3,
        compiler_params=_cparams(("parallel",), ATTN_VMEM_MB),
    )(p, p, p, gq, gk, ga)


def _attn_bwd(p, gq, gk, ga, o, rsum, dmix, n_heads):
    s = p.shape[0]
    nq = s // TILE

    hp = ATTN_HEADS_PER_STEP
    wd = hp * HEAD_DIM
    scale = HEAD_DIM ** -0.5

    def body(q_ref, k_ref, v_ref, gq_ref, gk_ref, ga_ref, o_ref, r_ref, dm_ref,
             dq_ref, dk_ref, dv_ref, dgq_ref, dgk_ref, dga_ref,
             qn_s, kn_s, vb_s, do_s, dqn_s, dkn_s, dv_s):
        step = pl.program_id(0)
        gqv, gkv = gq_ref[...], gk_ref[...]
        heads = [slice(hh * HEAD_DIM, (hh + 1) * HEAD_DIM) for hh in range(hp)]
        for hh, hd in enumerate(heads):
            qv = q_ref[:, hd]
            qn_s[:, hd] = (qv * _rstd(qv) * gqv).astype(BF16)
            kv = k_ref[:, hd]
            kn_s[:, hd] = (kv * _rstd(kv) * gkv).astype(BF16)
            ov = o_ref[:, hd]
            ro = _rstd(ov)
            ohat = ov * ro
            dm = dm_ref[:, hd]
            dga_ref[hh] = jnp.sum(dm * ohat, axis=0, keepdims=True)
            do_s[:, hd] = _norm_bwd(dm, ohat, ro, ga_ref[hh]).astype(BF16)
        vb_s[...] = v_ref[...].astype(BF16)
        dkn_s[...] = jnp.zeros_like(dkn_s)
        dv_s[...] = jnp.zeros_like(dv_s)
        row, col = _tile_iotas()
        causal = col < row
        ones = jnp.ones((TILE, TILE), BF16)
        incl_ones = jnp.concatenate([(row <= col).astype(BF16), ones], axis=1)
        excl_ones = jnp.concatenate([(row < col).astype(BF16), ones], axis=1)

        def tiles(rows, key_blocks, states, mask):
            chains = [(hi, hd, keys) for hi, hd in enumerate(heads) for keys in key_blocks]
            qis = [qn_s[rows, hd] for hd in heads]
            dois = [do_s[rows, hd] for hd in heads]
            logits = [_sb_logits(qis[hi], kn_s[keys, hd], mask) for hi, hd, keys in chains]
            sums = [_split_dot(l1m, incl_ones) for _, l1m in logits]
            das = [lax.dot_general(dois[hi], vb_s[keys, hd], NT_DIMS, preferred_element_type=F32)
                   for hi, hd, keys in chains]
            pfx = [st[1] for st in states]
            probs, dss = [], []
            for (hi, hd, _), (lb, _), sm, da in zip(chains, logits, sums, das):
                a = jnp.exp(lb + (r_ref[rows, hd] - pfx[hi] - sm[:, :TILE]))
                pfx[hi] = pfx[hi] + sm[:, TILE:]
                a = a if mask is None else jnp.where(mask, a, 0.0)
                probs.append(a.astype(BF16))
                dss.append(da * a)
            dsums = [_split_dot(ds, excl_ones) for ds in dss]
            pc = [st[2] for st in states]
            dzs = []
            for (hi, _, _), (lb, _), ds, dsm in zip(chains, logits, dss, dsums):
                dl1m = pc[hi] + dsm[:, :TILE]
                pc[hi] = pc[hi] + dsm[:, TILE:]
                dl1m = dl1m if mask is None else jnp.where(mask, dl1m, 0.0)
                beta = jnp.exp(lb)
                dzs.append(((ds * (1.0 - beta) - dl1m * beta) * scale).astype(BF16))
            dqs = [jnp.dot(dz, kn_s[keys, hd], preferred_element_type=F32) for dz, (_, hd, keys) in zip(dzs, chains)]
            for dz, a, (hi, hd, keys) in zip(dzs, probs, chains):
                dkn_s[keys, hd] += lax.dot_general(dz, qis[hi], TN_DIMS, preferred_element_type=F32)
                dv_s[keys, hd] += lax.dot_general(a, dois[hi], TN_DIMS, preferred_element_type=F32)
            dq_acc = [st[0] for st in states]
            for (hi, _, _), dq in zip(chains, dqs):
                dq_acc[hi] = dq_acc[hi] + dq
            return tuple(zip(dq_acc, pfx, pc))

        def key_block(b):
            return pl.ds(pl.multiple_of(b * TILE, TILE), TILE)

        def qblock(i, _):
            rows = pl.ds(pl.multiple_of(i * TILE, TILE), TILE)
            zero = jnp.zeros((TILE, HEAD_DIM), F32)

            def kblocks(jj, states):
                return tiles(rows, [key_block(2 * jj), key_block(2 * jj + 1)], states, None)

            states = lax.fori_loop(0, i // 2, kblocks, tuple((zero, zero, zero) for _ in heads))
            states = lax.cond(i % 2 == 1, lambda st: tiles(rows, [key_block(i - 1)], st, None), lambda st: st, states)
            states = tiles(rows, [rows], states, causal)
            for hd, (dq_acc, _, _) in zip(heads, states):
                dqn_s[rows, hd] = dq_acc
            return 0

        lax.fori_loop(0, nq, qblock, 0)

        def norm_in_bwd(x_ref, g, dn_s, dx_ref, dg_ref):
            part = jnp.zeros((1, HEAD_DIM), F32)
            for hd in heads:
                xv = x_ref[:, hd]
                r = _rstd(xv)
                xhat = xv * r
                dn = dn_s[:, hd]
                dx_ref[:, hd] = _norm_bwd(dn, xhat, r, g).astype(BF16)
                part = part + jnp.sum(dn * xhat, axis=0, keepdims=True)

            @pl.when(step == 0)
            def _():
                dg_ref[...] = part

            @pl.when(step > 0)
            def _():
                dg_ref[...] += part

        norm_in_bwd(q_ref, gqv, dqn_s, dq_ref, dgq_ref)
        norm_in_bwd(k_ref, gkv, dkn_s, dk_ref, dgk_ref)
        dv_ref[...] = dv_s[...].astype(BF16)

    once = pl.Buffered(1)
    steps = n_heads // hp
    col_blk = lambda off: pl.BlockSpec((s, wd), lambda h: (0, off + h), pipeline_mode=once)
    vec = pl.BlockSpec((1, HEAD_DIM), lambda h: (0, 0))
    hvec = pl.BlockSpec((hp, 1, HEAD_DIM), lambda h: (h, 0, 0))
    blk = pl.BlockSpec((s, wd), lambda h: (0, h), pipeline_mode=once)
    w = n_heads * HEAD_DIM
    big = jax.ShapeDtypeStruct((s, w), BF16)
    return pl.pallas_call(
        body, name="attn_bwd", grid=(steps,),
        in_specs=[col_blk(0), col_blk(steps), col_blk(2 * steps), vec, vec, hvec, blk, blk, blk],
        out_specs=(blk, blk, blk, vec, vec, hvec),
        out_shape=(big, big, big, jax.ShapeDtypeStruct((1, HEAD_DIM), F32), jax.ShapeDtypeStruct((1, HEAD_DIM), F32),
                   jax.ShapeDtypeStruct((n_heads, 1, HEAD_DIM), F32)),
        scratch_shapes=[pltpu.VMEM((s, wd), BF16)] * 4 + [pltpu.VMEM((s, wd), F32)] * 3,
        compiler_params=_cparams(("arbitrary",), ATTN_VMEM_MB),
    )(p, p, p, gq, gk, ga, o, rsum, dmix)


_INV_SQRT2 = 0.7071067811865476
_INV_SQRT_2PI = 0.3989422804014327


def _gelu(x):
    return 0.5 * x * (1.0 + lax.erf(x * _INV_SQRT2))


def _gelu_grad(x):
    return 0.5 * (1.0 + lax.erf(x * _INV_SQRT2)) + x * (_INV_SQRT_2PI * jnp.exp(-0.5 * x * x))


def _sgu_fwd(p, gs, w_s, b_s, gb, col0):
    s = p.shape[0]
    n_chunks = s // TILE
    per_trip = _pick(n_chunks, (4, 2, 1))

    def body(u_ref, v_ref, gs_ref, w_ref, b_ref, gb_ref, out_ref, vs_s):
        vg = _gelu(v_ref[...])
        vs_s[...] = (vg * _rstd(vg) * gs_ref[...]).astype(BF16)
        row, col = _tile_iotas()
        wt = jnp.where(col <= row, w_ref[...], 0.0).astype(BF16)
        bcol = b_ref[...]
        gbv = gb_ref[...]

        def chunks(c, _):
            rows = [pl.ds(pl.multiple_of((c * per_trip + k) * TILE, TILE), TILE) for k in range(per_trip)]
            mixed = [jnp.dot(wt, vs_s[r, :], preferred_element_type=F32) + bcol for r in rows]
            sgs = [_gelu(u_ref[r, :]) * mx for r, mx in zip(rows, mixed)]
            for r, sg in zip(rows, sgs):
                out_ref[r, :] = (sg * _rstd(sg) * gbv).astype(BF16)
            return 0

        lax.fori_loop(0, n_chunks // per_trip, chunks, 0)

    col_blk = lambda off: pl.BlockSpec((s, HEAD_DIM), lambda g: (0, off + g))
    gvec = pl.BlockSpec((None, 1, HEAD_DIM), lambda g: (g, 0, 0))
    return pl.pallas_call(
        body, name="sgu_fwd", grid=(N_GROUPS,),
        in_specs=[col_blk(col0), col_blk(col0 + N_GROUPS), gvec,
                  pl.BlockSpec((None, TILE, TILE), lambda g: (g, 0, 0)),
                  pl.BlockSpec((None, TILE, 1), lambda g: (g, 0, 0)), gvec],
        out_specs=pl.BlockSpec((s, HEAD_DIM), lambda g: (0, g)),
        out_shape=jax.ShapeDtypeStruct((s, N_GROUPS * HEAD_DIM), BF16),
        scratch_shapes=[pltpu.VMEM((s, HEAD_DIM), BF16)],
        compiler_params=_cparams(("parallel",)),
    )(p, p, gs, w_s, b_s, gb)


def _sgu_bwd(p, gs, w_s, b_s, gb, dmix, col0, dm_col0):
    s = p.shape[0]
    n_chunks = s // TILE
    per_trip = _pick(n_chunks, (4, 2, 1))

    def body(u_ref, v_ref, gs_ref, w_ref, b_ref, gb_ref, dm_ref,
             du_ref, dv_ref, dgs_ref, dw_ref, db_ref, dgb_ref, vs_s, dvs_s):
        gsv = gs_ref[...]
        gbv = gb_ref[...]
        vg = _gelu(v_ref[...])
        vs_s[...] = (vg * _rstd(vg) * gsv).astype(BF16)
        row, col = _tile_iotas()
        causal = col <= row
        wt = jnp.where(causal, w_ref[...], 0.0).astype(BF16)
        bcol = b_ref[...]

        def chunks(c, carry):
            dw_acc, db_acc, dgb_acc = carry
            rows = [pl.ds(pl.multiple_of((c * per_trip + k) * TILE, TILE), TILE) for k in range(per_trip)]
            vss = [vs_s[r, :] for r in rows]
            mixed = [jnp.dot(wt, vs, preferred_element_type=F32) + bcol for vs in vss]
            dmbs = []
            for r, mx in zip(rows, mixed):
                u_pre = u_ref[r, :]
                u = _gelu(u_pre)
                sg = u * mx
                rs = _rstd(sg)
                sghat = sg * rs
                dm = dm_ref[r, :]
                dsg = _norm_bwd(dm, sghat, rs, gbv)
                dgb_acc = dgb_acc + jnp.sum(dm * sghat, axis=0, keepdims=True)
                du_ref[r, :] = (dsg * mx * _gelu_grad(u_pre)).astype(BF16)
                dmixed = dsg * u
                db_acc = db_acc + jnp.sum(dmixed, axis=1, keepdims=True)
                dmbs.append(dmixed.astype(BF16))
            for dmb, vs in zip(dmbs, vss):
                dw_acc = dw_acc + lax.dot_general(dmb, vs, NT_DIMS, preferred_element_type=F32)
            for r, dmb in zip(rows, dmbs):
                dvs_s[r, :] = lax.dot_general(wt, dmb, TN_DIMS, preferred_element_type=F32)
            return dw_acc, db_acc, dgb_acc

        dw_acc, db_acc, dgb_acc = lax.fori_loop(
            0, n_chunks // per_trip, chunks,
            (jnp.zeros((TILE, TILE), F32), jnp.zeros((TILE, 1), F32), jnp.zeros((1, HEAD_DIM), F32)))
        dw_ref[...] = jnp.where(causal, dw_acc, 0.0)
        db_ref[...] = db_acc
        dgb_ref[...] = dgb_acc
        v_pre = v_ref[...]
        vg = _gelu(v_pre)
        rv = _rstd(vg)
        vhat = vg * rv
        dvs = dvs_s[...]
        dgs_ref[...] = jnp.sum(dvs * vhat, axis=0, keepdims=True)
        dv_ref[...] = (_norm_bwd(dvs, vhat, rv, gsv) * _gelu_grad(v_pre)).astype(BF16)

    col_blk = lambda off: pl.BlockSpec((s, HEAD_DIM), lambda g: (0, off + g))
    gvec = pl.BlockSpec((None, 1, HEAD_DIM), lambda g: (g, 0, 0))
    wspec = pl.BlockSpec((None, TILE, TILE), lambda g: (g, 0, 0))
    bspec = pl.BlockSpec((None, TILE, 1), lambda g: (g, 0, 0))
    blk = pl.BlockSpec((s, HEAD_DIM), lambda g: (0, g))
    big = jax.ShapeDtypeStruct((s, N_GROUPS * HEAD_DIM), BF16)
    gshape = jax.ShapeDtypeStruct((N_GROUPS, 1, HEAD_DIM), F32)
    return pl.pallas_call(
        body, name="sgu_bwd", grid=(N_GROUPS,),
        in_specs=[col_blk(col0), col_blk(col0 + N_GROUPS), gvec, wspec, bspec, gvec, col_blk(dm_col0)],
        out_specs=(blk, blk, gvec, wspec, bspec, gvec),
        out_shape=(big, big, gshape, jax.ShapeDtypeStruct((N_GROUPS, TILE, TILE), F32),
                   jax.ShapeDtypeStruct((N_GROUPS, TILE, 1), F32), gshape),
        scratch_shapes=[pltpu.VMEM((s, HEAD_DIM), BF16), pltpu.VMEM((s, HEAD_DIM), F32)],
        compiler_params=_cparams(("parallel",)),
    )(p, p, gs, w_s, b_s, gb, dmix)


SUBLANES = 8


def _shift_down(x, n):
    rolled = pltpu.roll(x, n, 0)
    edge = lax.broadcasted_iota(jnp.int32, (SUBLANES, x.shape[1]), 0)
    return jnp.concatenate([jnp.where(edge >= n, rolled[:SUBLANES], 0.0), rolled[SUBLANES:]], axis=0)


def _shift_up(x, n):
    s = x.shape[0]
    rolled = pltpu.roll(x, s - n, 0)
    edge = lax.broadcasted_iota(jnp.int32, (SUBLANES, x.shape[1]), 0)
    return jnp.concatenate([rolled[:s - SUBLANES], jnp.where(edge < SUBLANES - n, rolled[s - SUBLANES:], 0.0)], axis=0)


def _conv(x, w, b):
    x1, x2 = _shift_down(x, 1), _shift_down(x, 2)
    return b + w[0:1, :] * x2 + w[1:2, :] * x1 + w[2:3, :] * x, x1, x2


def _conv_specs(s, tn):
    xspec = pl.BlockSpec((2, s, tn), lambda j: (0, 0, j))
    wspec = pl.BlockSpec((2, CONV_WIDTH, tn), lambda j: (0, 0, j))
    bspec = pl.BlockSpec((2, 1, tn), lambda j: (0, 0, j))
    return xspec, wspec, bspec


def _conv_gate_fwd(up, cw, cb):
    _, s, f = up.shape
    tn = _pick(f, (256, 128))

    def body(x_ref, w_ref, b_ref, act_ref):
        gate = _conv(x_ref[0], w_ref[0], b_ref[0])[0]
        val = _conv(x_ref[1], w_ref[1], b_ref[1])[0]
        act_ref[...] = (gate * jax.nn.sigmoid(gate) * val).astype(BF16)

    xspec, wspec, bspec = _conv_specs(s, tn)
    return pl.pallas_call(
        body, name="conv_gate_fwd", grid=(f // tn,), in_specs=[xspec, wspec, bspec],
        out_specs=pl.BlockSpec((s, tn), lambda j: (0, j)), out_shape=jax.ShapeDtypeStruct((s, f), BF16),
        compiler_params=_cparams(("parallel",)),
    )(up, cw, cb)


def _down_dx_conv_gate_bwd(up, cw, cb, dy, wdown):
    _, s, f = up.shape
    d = dy.shape[1]
    tn = _pick(f, (256, 128))

    def body(x_ref, w_ref, b_ref, dy_ref, wd_ref, dx_ref, dw_ref, db_ref):
        da = lax.dot_general(dy_ref[...], wd_ref[...], NT_DIMS, preferred_element_type=F32)
        xg, xv = x_ref[0], x_ref[1]
        wg, wv = w_ref[0], w_ref[1]
        gate, xg1, xg2 = _conv(xg, wg, b_ref[0])
        val, xv1, xv2 = _conv(xv, wv, b_ref[1])
        sig = jax.nn.sigmoid(gate)
        dval = da * (gate * sig)
        dgate = da * val * (sig * (1.0 + gate * (1.0 - sig)))
        for half, (x, x1, x2, w, dz) in enumerate(((xg, xg1, xg2, wg, dgate), (xv, xv1, xv2, wv, dval))):
            dx_ref[half] = (w[2:3, :] * dz + w[1:2, :] * _shift_up(dz, 1) + w[0:1, :] * _shift_up(dz, 2)).astype(BF16)
            dw_ref[half, 0:1, :] = jnp.sum(dz * x2, axis=0, keepdims=True)
            dw_ref[half, 1:2, :] = jnp.sum(dz * x1, axis=0, keepdims=True)
            dw_ref[half, 2:3, :] = jnp.sum(dz * x, axis=0, keepdims=True)
            db_ref[half] = jnp.sum(dz, axis=0, keepdims=True)

    xspec, wspec, bspec = _conv_specs(s, tn)
    return pl.pallas_call(
        body, name="down_dx_conv_gate_bwd", grid=(f // tn,),
        in_specs=[xspec, wspec, bspec, pl.BlockSpec((s, d), lambda j: (0, 0)), pl.BlockSpec((tn, d), lambda j: (j, 0))],
        out_specs=(xspec, wspec, bspec),
        out_shape=(jax.ShapeDtypeStruct((2, s, f), BF16), jax.ShapeDtypeStruct((2, CONV_WIDTH, f), F32),
                   jax.ShapeDtypeStruct((2, 1, f), F32)),
        compiler_params=_cparams(("parallel",), 56),
    )(up, cw, cb, dy, wdown)


def _mesh_pos():
    return lax.axis_index("x"), lax.axis_index("y"), lax.axis_index("c")


def _remote(src, dst, send_sem, recv_sem, to):
    return pltpu.make_async_remote_copy(src_ref=src, dst_ref=dst, send_sem=send_sem, recv_sem=recv_sem,
                                        device_id=to, device_id_type=pl.DeviceIdType.MESH)


HBM_SPEC = pl.BlockSpec(memory_space=pltpu.HBM)
SEM_SPEC = pl.BlockSpec(memory_space=pltpu.SEMAPHORE)
ANY_SPEC = pl.BlockSpec(memory_space=pl.ANY)
TOKEN_SPEC = pl.BlockSpec(memory_space=pltpu.VMEM)
TOKEN_SHAPE = jax.ShapeDtypeStruct((8, 128), F32)
DATAFLOW = pltpu.SideEffectType.DATAFLOW_SIDE_EFFECTING
GATHER_PLANE = (2, 4, 6)


def _slot(pos):
    return 4 * pos[0] + 2 * pos[1] + pos[2]


def _flip(pos, k):
    return (pos[0] ^ ((k >> 2) & 1), pos[1] ^ ((k >> 1) & 1), pos[2] ^ (k & 1))


def _hbm(a):
    return pltpu.with_memory_space_constraint(a, pltpu.HBM)


def _hbm_shapes(arrays):
    return tuple(pltpu.HBM(a.shape, a.dtype) for a in arrays)


class _Split:
    def __init__(self, n, outs, has_sems):
        k = 2 * n if has_sems else 0
        self.n = n
        self.sems = list(outs[:k])
        self.bufs = list(outs[k:k + 2 * n])
        self.token = outs[-1]


def _split_call(name, body, bufs, sems_in, makes_sems, after):
    n = len(bufs) // 2
    k = 2 * n if makes_sems else 0
    m = len(sems_in)
    afters = list(after) if isinstance(after, (list, tuple)) else [after]
    na = len(afters)

    def wrapped(*refs):
        srcs, dsts = refs[:n], refs[n:2 * n]
        s_in = refs[2 * n:2 * n + m]
        s_out = refs[2 * n + m + na:2 * n + m + na + k]
        token, local_sems = refs[-2], refs[-1]
        body(srcs, dsts, s_in, s_out, local_sems)
        token[...] = jnp.zeros_like(token)

    outs = pl.pallas_call(
        wrapped, name=name,
        out_shape=(pltpu.SemaphoreType.DMA(()),) * k + _hbm_shapes(bufs) + (TOKEN_SHAPE,),
        in_specs=[HBM_SPEC] * (2 * n) + [SEM_SPEC] * m + [ANY_SPEC] * na,
        out_specs=(SEM_SPEC,) * k + (HBM_SPEC,) * (2 * n) + (TOKEN_SPEC,),
        input_output_aliases={i: k + i for i in range(2 * n)},
        scratch_shapes=[pltpu.SemaphoreType.DMA((n,))],
        compiler_params=pltpu.CompilerParams(has_side_effects=DATAFLOW),
    )(*[_hbm(b) for b in bufs], *sems_in, *afters)
    return _Split(n, outs, makes_sems)


def _wait_slots(land, count, send_sem, recv_sem, me, send=False, recv=False):
    span = land.at[pl.ds(0, count)]
    cp = _remote(span, span, send_sem, recv_sem, me)
    if send:
        cp.wait_send()
    if recv:
        cp.wait_recv()


def _gather_start(name, shards, after):
    n = len(shards)
    my_slot = _slot(_mesh_pos())
    lands = [lax.dynamic_update_slice(lax.empty((N_DEV,) + w.shape, w.dtype), w[None], (my_slot, 0, 0)) for w in shards]

    def body(srcs, dsts, _, sems, local_sems):
        me = _mesh_pos()
        for a in range(n):
            for k in (1,) + GATHER_PLANE:
                _remote(srcs[a], dsts[a].at[_slot(me)], sems[a], sems[n + a], _flip(me, k)).start()

    return _split_call(name, body, list(shards) + lands, [], True, after)


def _gather_forward(name, started, after):
    n = started.n

    def body(srcs, dsts, sems_a, sems_b, local_sems):
        me = _mesh_pos()
        sibling = _flip(me, 1)
        for a in range(n):
            _wait_slots(dsts[a], 4, sems_a[a], sems_a[n + a], me, recv=True)
            for k in GATHER_PLANE:
                block = dsts[a].at[_slot(_flip(me, k))]
                _remote(block, block, sems_b[a], sems_b[n + a], sibling).start()
        for a in range(n):
            _wait_slots(dsts[a], 4, sems_a[a], sems_a[n + a], me, send=True)

    return _split_call(name, body, started.bufs, started.sems, True, after)


def _gather_finish(name, forwarded, after):
    n = forwarded.n

    def body(srcs, dsts, sems_b, _, local_sems):
        me = _mesh_pos()
        for a in range(n):
            _wait_slots(dsts[a], 3, sems_b[a], sems_b[n + a], me, send=True, recv=True)

    return _split_call(name, body, forwarded.bufs, forwarded.sems, False, after).bufs[n:]


def _exchange_start(name, blocked, after):
    n = len(blocked)
    my_slot = _slot(_mesh_pos())
    lands = [lax.dynamic_update_slice(lax.empty(w.shape, w.dtype), lax.dynamic_slice_in_dim(w, my_slot, 1, 0),
                                      (my_slot, 0, 0)) for w in blocked]

    def body(srcs, dsts, _, sems, local_sems):
        me = _mesh_pos()
        for a in range(n):
            for k in range(1, N_DEV):
                peer = _flip(me, k)
                _remote(srcs[a].at[_slot(peer)], dsts[a].at[_slot(me)], sems[a], sems[n + a], peer).start()

    return _split_call(name, body, list(blocked) + lands, [], True, after)


def _exchange_finish(name, started, after):
    n = started.n

    def body(srcs, dsts, sems, _, local_sems):
        me = _mesh_pos()
        for a in range(n):
            _wait_slots(dsts[a], N_DEV - 1, sems[a], sems[n + a], me, send=True, recv=True)

    return _split_call(name, body, started.bufs, started.sems, False, after).bufs[n:]


def _broadcast_start(name, arrays, after):
    n = len(arrays)
    my_slot = _slot(_mesh_pos())
    lands = [lax.dynamic_update_slice(lax.empty((N_DEV,) + w.shape, w.dtype), w[None], (my_slot, 0, 0)) for w in arrays]

    def body(srcs, dsts, _, sems, local_sems):
        me = _mesh_pos()
        for a in range(n):
            for k in range(1, N_DEV):
                _remote(srcs[a], dsts[a].at[_slot(me)], sems[a], sems[n + a], _flip(me, k)).start()

    return _split_call(name, body, list(arrays) + lands, [], True, after)


def _adamw_math(w, g, m, v):
    m = ADAM_B1 * m + (1.0 - ADAM_B1) * g
    v = ADAM_B2 * v + (1.0 - ADAM_B2) * (g * g)
    m_hat = m / (1.0 - ADAM_B1 ** ADAM_STEP)
    v_hat = v / (1.0 - ADAM_B2 ** ADAM_STEP)
    delta = -ADAM_LR * (m_hat / (jnp.sqrt(v_hat) + ADAM_EPS) + ADAM_WD * w)
    return delta, m, v


def _adamw(name, w, m, v, parts, layer, prev=None):
    _, r, c = w.shape
    tr = _pick(r, tuple(t for t in (256, 128, 64, 32, 16) if t * c <= ADAMW_TILE_ELEMS))
    n_prev = 0 if prev is None else 4

    def body(*refs):
        w_ref, m_ref, v_ref, p_ref = refs[:4]
        g_ref, d_ref, nm_ref, nv_ref = refs[4 + n_prev:]
        g = p_ref[0].astype(F32)
        for src in range(1, N_DEV):
            g = g + p_ref[src].astype(F32)
        delta, nm, nv = _adamw_math(w_ref[...], g, m_ref[...], v_ref[...])
        g_ref[...] = g
        d_ref[...] = delta
        nm_ref[...] = nm
        nv_ref[...] = nv

    wspec = pl.BlockSpec((None, tr, c), lambda i: (layer, i, 0))
    pspec = pl.BlockSpec((N_DEV, tr, c), lambda i: (0, i, 0))
    shp = jax.ShapeDtypeStruct(w.shape, F32)
    return pl.pallas_call(
        body, name=name, grid=(r // tr,), in_specs=[wspec] * 3 + [pspec] + [ANY_SPEC] * n_prev,
        out_specs=(wspec,) * 4, out_shape=(shp,) * 4, input_output_aliases={4 + j: j for j in range(n_prev)},
        compiler_params=_cparams(("parallel",)),
    )(w, m, v, parts, *([] if prev is None else prev))


PACK_TILE = 8 * 128


def _pack(arrays):
    flat = []
    for a in arrays:
        v = a.reshape(-1)
        pad = (-v.shape[0]) % PACK_TILE
        flat.append(jnp.pad(v, (0, pad)) if pad else v)
    return jnp.concatenate(flat).reshape(-1, 128)


def _unpack(buf, like):
    flat = buf.reshape(-1)
    out, off = [], 0
    for a in like:
        n = 1
        for dim in a.shape:
            n *= dim
        out.append(flat[off:off + n].reshape(a.shape))
        off += n + (-n) % PACK_TILE
    return out


def _sum_slots(gathered):
    _, r, c = gathered.shape

    def body(x_ref, o_ref):
        acc = x_ref[0]
        for src in range(1, N_DEV):
            acc = acc + x_ref[src]
        o_ref[...] = acc

    return pl.pallas_call(body, name="small_grad_sum", out_shape=jax.ShapeDtypeStruct((r, c), F32))(gathered)


def _adamw_small(w, g, m, v):
    shp = jax.ShapeDtypeStruct(w.shape, F32)

    def body(w_ref, g_ref, m_ref, v_ref, d_ref, nm_ref, nv_ref):
        delta, nm, nv = _adamw_math(w_ref[...], g_ref[...], m_ref[...], v_ref[...])
        d_ref[...] = delta
        nm_ref[...] = nm
        nv_ref[...] = nv

    return pl.pallas_call(body, name="adamw_small", out_shape=(shp,) * 3)(w, g, m, v)


def kernel(x, attn_norm_g, w_in, q_norm_g, k_norm_g, sgu_norm_g, sgu_w, sgu_b, out_norm_a_g, out_norm_b_g, w_out, ffn_norm_g, w_up, conv_w, conv_b, w_down, loss_target, m_attn_norm_g, m_w_in, m_q_norm_g, m_k_norm_g, m_sgu_norm_g, m_sgu_w, m_sgu_b, m_out_norm_a_g, m_out_norm_b_g, m_w_out, m_ffn_norm_g, m_w_up, m_conv_w, m_conv_b, m_w_down, v_attn_norm_g, v_w_in, v_q_norm_g, v_k_norm_g, v_sgu_norm_g, v_sgu_w, v_sgu_b, v_out_norm_a_g, v_out_norm_b_g, v_w_out, v_ffn_norm_g, v_w_up, v_conv_w, v_conv_b, v_w_down):
    depth = w_in.shape[0]
    s, d = x.shape[1], x.shape[2]
    n_heads = (d // 2) // HEAD_DIM
    sgu_col0 = 3 * n_heads
    f2 = w_up.shape[2] * N_DEV
    ff = f2 // 2
    my_slot = 4 * lax.axis_index("x") + 2 * lax.axis_index("y") + lax.axis_index("c")

    wb = [(w_in[l].astype(BF16), w_out[l].astype(BF16), w_up[l].astype(BF16), w_down[l].astype(BF16))
          for l in range(depth)]
    groups = {"in0": [wb[0][0]], "mix0": [wb[0][1], conv_w.reshape(depth * CONV_WIDTH, -1), wb[0][2]], "down0": [wb[0][3]]}
    for l in range(1, depth):
        groups[f"in{l}"] = [wb[l][0], wb[l][1]]
        groups[f"mix{l}"] = [wb[l][2], wb[l][3]]
    token = attn_norm_g
    started = {}
    for gname, group in groups.items():
        started[gname] = _gather_start(f"gather_{gname}_start", group, token)
        token = started[gname].token

    conv_b_all = conv_b.reshape(depth, 2, 1, ff)
    sgu_b_col = sgu_b[..., None]
    forwarded = _gather_forward("gather_in0_forward", started["in0"], token)
    win_g = _gather_finish("gather_in0_finish", forwarded, forwarded.token)[0]
    wout_g = None

    xs = x[0]
    saved = []
    gathered = []
    for l in range(depth):
        g1 = attn_norm_g[l][None]
        g2 = ffn_norm_g[l][None]
        gq, gk = q_norm_g[l][None], k_norm_g[l][None]
        ga = out_norm_a_g[l][:, None, :]
        gs = sgu_norm_g[l][:, None, :]
        gb = out_norm_b_g[l][:, None, :]
        h1 = _rmsnorm_fwd("attn_norm_fwd", xs, g1)
        p = _mm_nn_blocked("in_proj", h1, win_g, F32)
        att, o, rsum = _attn_fwd(p, gq, gk, ga, n_heads)
        forwarded = _gather_forward(f"gather_mix{l}_forward", started[f"mix{l}"], att)
        sg = _sgu_fwd(p, gs, sgu_w[l], sgu_b_col[l], gb, sgu_col0)
        mix = jnp.concatenate([att, sg], axis=-1)
        landed = _gather_finish(f"gather_mix{l}_finish", forwarded, mix)
        if l == 0:
            wout_g, cw, wup_g = landed
            cw = jnp.transpose(cw.reshape(N_DEV, depth, CONV_WIDTH, -1), (1, 2, 0, 3)).reshape(depth, CONV_WIDTH, 2, ff)
            conv_w_all = jnp.transpose(cw, (0, 2, 1, 3))
        else:
            wup_g, wdown_g = landed
        x1 = _mm_nn_res("out_proj", mix, wout_g.reshape(d, d), xs)
        h2 = _rmsnorm_fwd("ffn_norm_fwd", x1, g2)
        up = _mm_nn_blocked("up_proj", h2, wup_g, F32, halves=True)
        if l == 0:
            forwarded = _gather_forward("gather_down0_forward", started["down0"], up)
        act = _conv_gate_fwd(up, conv_w_all[l], conv_b_all[l])
        if l == 0:
            wdown_g = _gather_finish("gather_down0_finish", forwarded, act)[0]
        if l + 1 < depth:
            forwarded = _gather_forward(f"gather_in{l + 1}_forward", started[f"in{l + 1}"], act)
        x2 = _mm_nn_res("down_proj", act, wdown_g.reshape(ff, d), x1)
        saved.append((xs, h1, p, o, rsum, mix, x1, h2, up, act))
        gathered.append((win_g, wout_g, wup_g, wdown_g))
        if l + 1 < depth:
            win_g, wout_g = _gather_finish(f"gather_in{l + 1}_finish", forwarded, x2)
        xs = x2

    dx, dxb, loss_vec = _loss_head(xs, loss_target[0])
    loss = lax.psum(loss_vec[0, 0], MESH_AXES)

    exchanges = []
    small = [None] * depth
    for l in reversed(range(depth)):
        xs0, h1, p, o, rsum, mix, x1, h2, up, act = saved[l]
        win_g, wout_g, wup_g, wdown_g = gathered[l]
        wout_full = wout_g.reshape(d, d)
        wdown_full = wdown_g.reshape(ff, d)
        g1 = attn_norm_g[l][None]
        g2 = ffn_norm_g[l][None]
        gq, gk = q_norm_g[l][None], k_norm_g[l][None]
        ga = out_norm_a_g[l][:, None, :]
        gs = sgu_norm_g[l][:, None, :]
        gb = out_norm_b_g[l][:, None, :]
        d_wdown = _mm_tn_plain("down_proj_dw", act, dxb)
        dup, d_cw, d_cb = _down_dx_conv_gate_bwd(up, conv_w_all[l], conv_b_all[l], dxb, wdown_full)
        d_wup = _mm_tn_blocked("up_proj_dw", h2, dup, N_DEV, halves=True)
        exchanges.append((l, ("w_up", "w_down"), _exchange_start(
            f"grad_ffn{l}_start", [d_wup, d_wdown.reshape(N_DEV, ff // N_DEV, d)], d_wup)))
        dh2 = _mm_nt_blocked("up_proj_dx", dup, wup_g, halves=True, after=exchanges[-1][2].token)
        dx, dxb, d_g2 = _rmsnorm_bwd("ffn_norm_bwd", dh2, x1, g2, dx)
        d_wout = _mm_tn_plain("out_proj_dw", mix, dxb)
        dmix = _mm_nt_plain("out_proj_dx", dxb, wout_full)
        dq, dk, dv, d_gq, d_gk, d_ga = _attn_bwd(p, gq, gk, ga, o, rsum, dmix, n_heads)
        du, dvs, d_gs, d_sw, d_sb, d_gb = _sgu_bwd(p, gs, sgu_w[l], sgu_b_col[l], gb, dmix, sgu_col0, n_heads)
        dp = jnp.concatenate([dq, dk, dv, du, dvs], axis=-1)
        d_win = _mm_tn_blocked("in_proj_dw", h1, dp, N_DEV)
        exchanges.append((l, ("w_in", "w_out"), _exchange_start(
            f"grad_mix{l}_start", [d_win, d_wout.reshape(N_DEV, d // N_DEV, d)], d_win)))
        dh1 = _mm_nt_blocked("in_proj_dx", dp, win_g, after=exchanges[-1][2].token)
        dx, dxb, d_g1 = _rmsnorm_bwd("attn_norm_bwd", dh1, xs0, g1, dx)
        small[l] = dict(attn_norm_g=d_g1[0], q_norm_g=d_gq[0], k_norm_g=d_gk[0], sgu_norm_g=d_gs[:, 0], sgu_w=d_sw,
                        sgu_b=d_sb[..., 0], out_norm_a_g=d_ga[:, 0], out_norm_b_g=d_gb[:, 0], ffn_norm_g=d_g2[0],
                        conv_w=jnp.transpose(d_cw, (1, 0, 2)).reshape(CONV_WIDTH, f2), conv_b=d_cb.reshape(f2))
    grad_x = dx[None]

    small_names = ["attn_norm_g", "q_norm_g", "k_norm_g", "sgu_norm_g", "sgu_w", "sgu_b", "out_norm_a_g",
                   "out_norm_b_g", "ffn_norm_g", "conv_b", "conv_w"]
    small_g = [jnp.stack([small[l][n] for l in range(depth)]) for n in small_names]
    small_sent = _broadcast_start("grad_small_start", [_pack(small_g)], dx)

    res = {}
    big = dict(w_in=(w_in, m_w_in, v_w_in), w_out=(w_out, m_w_out, v_w_out), w_up=(w_up, m_w_up, v_w_up),
               w_down=(w_down, m_w_down, v_w_down))
    after = [small_sent.token]
    for l, names, ex in exchanges:
        stage = "ffn" if names[0] == "w_up" else "mix"
        landed = _exchange_finish(f"grad_{stage}{l}_finish", ex, after)
        after = []
        for name, parts in zip(names, landed):
            w, m, v = big[name]
            res[name] = _adamw(f"adamw_{name}", w, m, v, parts, l, res.get(name))
            after.append(res[name][0])
    small_all = _exchange_finish("grad_small_finish", small_sent, after)[0]
    small_sum = _unpack(_sum_slots(small_all), small_g)
    g_small = dict(zip(small_names, small_sum))
    cwn = conv_w.shape[2]
    g_small["conv_w"] = lax.dynamic_slice_in_dim(g_small["conv_w"], my_slot * cwn, cwn, axis=2)
    small_w = dict(attn_norm_g=(attn_norm_g, m_attn_norm_g, v_attn_norm_g), q_norm_g=(q_norm_g, m_q_norm_g, v_q_norm_g),
                   k_norm_g=(k_norm_g, m_k_norm_g, v_k_norm_g), sgu_norm_g=(sgu_norm_g, m_sgu_norm_g, v_sgu_norm_g),
                   sgu_w=(sgu_w, m_sgu_w, v_sgu_w), sgu_b=(sgu_b, m_sgu_b, v_sgu_b),
                   out_norm_a_g=(out_norm_a_g, m_out_norm_a_g, v_out_norm_a_g),
                   out_norm_b_g=(out_norm_b_g, m_out_norm_b_g, v_out_norm_b_g),
                   ffn_norm_g=(ffn_norm_g, m_ffn_norm_g, v_ffn_norm_g), conv_b=(conv_b, m_conv_b, v_conv_b),
                   conv_w=(conv_w, m_conv_w, v_conv_w))
    like = [small_w[n][0] for n in small_names]
    pw = _pack([small_w[n][0] for n in small_names])
    pm = _pack([small_w[n][1] for n in small_names])
    pv = _pack([small_w[n][2] for n in small_names])
    pg = _pack([g_small[n].reshape(small_w[n][0].shape) for n in small_names])
    pd, pnm, pnv = _adamw_small(pw, pg, pm, pv)
    for n, dlt, nm, nv in zip(small_names, _unpack(pd, like), _unpack(pnm, like), _unpack(pnv, like)):
        res[n] = (g_small[n].reshape(small_w[n][0].shape), dlt, nm, nv)

    order = ["attn_norm_g", "w_in", "q_norm_g", "k_norm_g", "sgu_norm_g", "sgu_w", "sgu_b", "out_norm_a_g",
             "out_norm_b_g", "w_out", "ffn_norm_g", "w_up", "conv_w", "conv_b", "w_down"]
    outs = [loss, grad_x]
    for field in range(4):
        outs += [res[n][field] for n in order]
    return tuple(outs)
```

```python
import functools

import jax
import jax.numpy as jnp
from jax import lax
from jax.experimental import pallas as pl
from jax.experimental.pallas import tpu as pltpu

F32 = jnp.float32
BF16 = jnp.bfloat16
EPS = 1e-6
HEAD_DIM = 128
TILE = 128
ATTN_VMEM_MB = 58
ATTN_HEADS_PER_STEP = 4
N_GROUPS = 8
CONV_WIDTH = 3
N_DEV = 8
MESH_AXES = ("x", "y", "c")
MIB = 1024 * 1024

ADAM_LR = 0.001
ADAM_B1 = 0.9
ADAM_B2 = 0.999
ADAM_EPS = 1e-08
ADAM_WD = 0.01
ADAM_STEP = 10
ADAMW_TILE_ELEMS = 160 * 1024

NT_DIMS = (((1,), (1,)), ((), ()))
NN_DIMS = (((1,), (0,)), ((), ()))
TN_DIMS = (((0,), (0,)), ((), ()))


def _cparams(sem, vmem_mb=48):
    return pltpu.CompilerParams(dimension_semantics=sem, vmem_limit_bytes=vmem_mb * MIB)


def _pick(n, cands):
    for c in cands:
        if n % c == 0:
            return c
    return n


def _mm(name, grid, ins, in_specs, out_shape, out_spec, dims, has_res=False, parts=None, vmem_mb=56, after=None):
    n_in = 2 + has_res + (after is not None)
    if after is not None:
        ins = tuple(ins) + (after,)
        in_specs = list(in_specs) + [pl.BlockSpec(after.shape, lambda *_: (0, 0))]

    def body(*refs):
        a_ref, b_ref = refs[:2]
        o_ref = refs[n_in]
        if parts is None:
            acc = lax.dot_general(a_ref[...], b_ref[...], dims, preferred_element_type=F32)
        else:
            acc = None
            for part in parts:
                a, b = part(a_ref, b_ref)
                prod = lax.dot_general(a, b, dims, preferred_element_type=F32)
                acc = prod if acc is None else acc + prod
        if has_res:
            acc = acc + refs[2][...]
        o_ref[...] = acc.astype(o_ref.dtype)

    return pl.pallas_call(
        body, name=name, grid=grid, in_specs=in_specs, out_specs=out_spec, out_shape=out_shape,
        compiler_params=_cparams(("parallel",) * len(grid), vmem_mb),
    )(*ins)


def _rows_for(m, row_bytes, budget):
    return _pick(m, tuple(t for t in (2048, 1024, 512, 256, 128) if t * row_bytes <= budget))


def _mm_nn_blocked(name, a, wb, out_dtype, halves=False, after=None):
    m, k = a.shape
    nb, _, bn = wb.shape
    tm = _rows_for(m, bn * jnp.dtype(out_dtype).itemsize, 6 * MIB)
    a_spec = pl.BlockSpec((tm, k), lambda j, i: (i, 0))
    b_spec = pl.BlockSpec((None, k, bn), lambda j, i: (j, 0, 0))
    if halves:
        hb = nb // 2
        out_shape = jax.ShapeDtypeStruct((2, m, hb * bn), out_dtype)
        o_spec = pl.BlockSpec((None, tm, bn), lambda j, i: (j // hb, i, j % hb))
    else:
        out_shape = jax.ShapeDtypeStruct((m, nb * bn), out_dtype)
        o_spec = pl.BlockSpec((tm, bn), lambda j, i: (i, j))
    return _mm(name, (nb, m // tm), (a, wb), [a_spec, b_spec], out_shape, o_spec, NN_DIMS, after=after)


def _mm_nn_res(name, a, w, res, after=None):
    m, k = a.shape
    n = w.shape[1]
    tm = _pick(m, (512, 256, 128))
    tn = _rows_for(n, k * 2, 12 * MIB)
    a_spec = pl.BlockSpec((tm, k), lambda j, i: (i, 0))
    b_spec = pl.BlockSpec((k, tn), lambda j, i: (0, j))
    r_spec = pl.BlockSpec((tm, tn), lambda j, i: (i, j))
    o_spec = pl.BlockSpec((tm, tn), lambda j, i: (i, j))
    return _mm(name, (n // tn, m // tm), (a, w, res), [a_spec, b_spec, r_spec], jax.ShapeDtypeStruct((m, n), F32),
               o_spec, NN_DIMS, has_res=True, after=after)


def _mm_nt_blocked(name, dy, wb, halves=False, after=None):
    nb, n, bn = wb.shape
    m = dy.shape[-2]
    tm = _pick(m, (512, 256, 128))
    tn = _rows_for(n, nb * bn * 2, 12 * MIB)
    if halves:
        hb = nb // 2
        a_spec = pl.BlockSpec((2, tm, hb * bn), lambda j, i: (0, i, 0))
        a_part = lambda kk: (lambda a_ref: a_ref[kk // hb, :, (kk % hb) * bn:(kk % hb + 1) * bn])
    else:
        a_spec = pl.BlockSpec((tm, nb * bn), lambda j, i: (i, 0))
        a_part = lambda kk: (lambda a_ref: a_ref[:, kk * bn:(kk + 1) * bn])
    parts = [(lambda a_ref, b_ref, kk=kk, sel=a_part(kk): (sel(a_ref), b_ref[kk])) for kk in range(nb)]
    b_spec = pl.BlockSpec((nb, tn, bn), lambda j, i: (0, j, 0))
    o_spec = pl.BlockSpec((tm, tn), lambda j, i: (i, j))
    return _mm(name, (n // tn, m // tm), (dy, wb), [a_spec, b_spec], jax.ShapeDtypeStruct((m, n), F32), o_spec,
               NT_DIMS, parts=parts, after=after)


def _mm_nt_plain(name, dy, w, out_dtype=F32):
    m, k = dy.shape
    n = w.shape[0]
    tm = _rows_for(m, k * 2, 8 * MIB)
    tn = _pick(n, (512, 256, 128))
    a_spec = pl.BlockSpec((tm, k), lambda j, i: (i, 0))
    b_spec = pl.BlockSpec((tn, k), lambda j, i: (j, 0))
    o_spec = pl.BlockSpec((tm, tn), lambda j, i: (i, j))
    return _mm(name, (n // tn, m // tm), (dy, w), [a_spec, b_spec], jax.ShapeDtypeStruct((m, n), out_dtype), o_spec,
               NT_DIMS)


def _mm_tn_blocked(name, a, dy, nb, halves=False):
    s, k1 = a.shape
    bn = (dy.shape[-1] * (2 if halves else 1)) // nb
    tm = _rows_for(k1, bn * 2, 6 * MIB)
    a_spec = pl.BlockSpec((s, tm), lambda j, i: (0, i))
    if halves:
        hb = nb // 2
        b_spec = pl.BlockSpec((None, s, bn), lambda j, i: (j // hb, 0, j % hb))
    else:
        b_spec = pl.BlockSpec((s, bn), lambda j, i: (0, j))
    o_spec = pl.BlockSpec((None, tm, bn), lambda j, i: (j, i, 0))
    return _mm(name, (nb, k1 // tm), (a, dy), [a_spec, b_spec], jax.ShapeDtypeStruct((nb, k1, bn), BF16), o_spec,
               TN_DIMS)


def _mm_tn_plain(name, a, dy):
    s, k1 = a.shape
    n = dy.shape[1]
    tm = _pick(k1, (512, 256, 128))
    tn = _rows_for(n, s * 2, 8 * MIB)
    a_spec = pl.BlockSpec((s, tm), lambda i, j: (0, i))
    b_spec = pl.BlockSpec((s, tn), lambda i, j: (0, j))
    o_spec = pl.BlockSpec((tm, tn), lambda i, j: (i, j))
    return _mm(name, (k1 // tm, n // tn), (a, dy), [a_spec, b_spec], jax.ShapeDtypeStruct((k1, n), BF16), o_spec,
               TN_DIMS)


def _rstd(x):
    return lax.rsqrt(jnp.mean(x * x, axis=-1, keepdims=True) + EPS)


def _norm_bwd(dy, xhat, r, g):
    dxhat = dy * g
    return r * (dxhat - xhat * jnp.mean(dxhat * xhat, axis=-1, keepdims=True))


def _rmsnorm_fwd(name, x, g):
    s, d = x.shape
    tr = _pick(s, (256, 128))

    def body(x_ref, g_ref, h_ref):
        xv = x_ref[...]
        h_ref[...] = (xv * _rstd(xv) * g_ref[...]).astype(BF16)

    return pl.pallas_call(
        body, name=name, grid=(s // tr,),
        in_specs=[pl.BlockSpec((tr, d), lambda i: (i, 0)), pl.BlockSpec((1, d), lambda i: (0, 0))],
        out_specs=pl.BlockSpec((tr, d), lambda i: (i, 0)),
        out_shape=jax.ShapeDtypeStruct((s, d), BF16), compiler_params=_cparams(("parallel",)),
    )(x, g)


def _rmsnorm_bwd(name, dh, x, g, dres):
    s, d = x.shape
    tr = _pick(s, (256, 128))

    def body(dh_ref, x_ref, g_ref, dres_ref, dx_ref, dxb_ref, dg_ref):
        xv = x_ref[...]
        r = _rstd(xv)
        xhat = xv * r
        dhv = dh_ref[...]
        dx = dres_ref[...] + _norm_bwd(dhv, xhat, r, g_ref[...])
        dx_ref[...] = dx
        dxb_ref[...] = dx.astype(BF16)
        part = jnp.sum(dhv * xhat, axis=0, keepdims=True)

        @pl.when(pl.program_id(0) == 0)
        def _():
            dg_ref[...] = part

        @pl.when(pl.program_id(0) > 0)
        def _():
            dg_ref[...] += part

    row = pl.BlockSpec((tr, d), lambda i: (i, 0))
    vec = pl.BlockSpec((1, d), lambda i: (0, 0))
    return pl.pallas_call(
        body, name=name, grid=(s // tr,), in_specs=[row, row, vec, row], out_specs=(row, row, vec),
        out_shape=(jax.ShapeDtypeStruct((s, d), F32), jax.ShapeDtypeStruct((s, d), BF16),
                   jax.ShapeDtypeStruct((1, d), F32)),
        compiler_params=_cparams(("arbitrary",)),
    )(dh, x, g, dres)


def _loss_head(y, target):
    s, d = y.shape
    tr = _pick(s, (256, 128))

    def body(y_ref, t_ref, dy_ref, dyb_ref, loss_ref):
        err = y_ref[...] - t_ref[...]
        dy = err * (1.0 / d)
        dy_ref[...] = dy
        dyb_ref[...] = dy.astype(BF16)
        part = 0.5 * jnp.sum(jnp.mean(err * err, axis=-1, keepdims=True), axis=0, keepdims=True)
        part = jnp.broadcast_to(part, (1, 128))

        @pl.when(pl.program_id(0) == 0)
        def _():
            loss_ref[...] = part

        @pl.when(pl.program_id(0) > 0)
        def _():
            loss_ref[...] += part

    row = pl.BlockSpec((tr, d), lambda i: (i, 0))
    return pl.pallas_call(
        body, name="loss_head", grid=(s // tr,), in_specs=[row, row],
        out_specs=(row, row, pl.BlockSpec((1, 128), lambda i: (0, 0))),
        out_shape=(jax.ShapeDtypeStruct((s, d), F32), jax.ShapeDtypeStruct((s, d), BF16),
                   jax.ShapeDtypeStruct((1, 128), F32)),
        compiler_params=_cparams(("arbitrary",)),
    )(y, target)


def _split_dot(x, tri):
    hi = x.astype(BF16)
    lo = (x - hi.astype(F32)).astype(BF16)
    return (jnp.dot(hi, tri, preferred_element_type=F32) + jnp.dot(lo, tri, preferred_element_type=F32))


def _tile_iotas():
    row = lax.broadcasted_iota(jnp.int32, (TILE, TILE), 0)
    col = lax.broadcasted_iota(jnp.int32, (TILE, TILE), 1)
    return row, col


def _sb_logits(qi, kb, mask):
    z = lax.dot_general(qi, kb, NT_DIMS, preferred_element_type=F32) * (HEAD_DIM ** -0.5)
    sp = jnp.log1p(jnp.exp(-jnp.abs(z)))
    lb = jnp.minimum(z, 0.0) - sp
    l1m = -jnp.maximum(z, 0.0) - sp
    if mask is not None:
        l1m = jnp.where(mask, l1m, 0.0)
    return lb, l1m


def _attn_fwd(p, gq, gk, ga, n_heads):
    s = p.shape[0]
    nq = s // TILE

    hp = ATTN_HEADS_PER_STEP
    wd = hp * HEAD_DIM

    def body(q_ref, k_ref, v_ref, gq_ref, gk_ref, ga_ref, att_ref, o_ref, r_ref, qn_s, kn_s, vb_s):
        heads = [slice(hh * HEAD_DIM, (hh + 1) * HEAD_DIM) for hh in range(hp)]
        for hd in heads:
            qv = q_ref[:, hd]
            qn_s[:, hd] = (qv * _rstd(qv) * gq_ref[...]).astype(BF16)
            kv = k_ref[:, hd]
            kn_s[:, hd] = (kv * _rstd(kv) * gk_ref[...]).astype(BF16)
        vb_s[...] = v_ref[...].astype(BF16)
        row, col = _tile_iotas()
        causal = col < row
        upper_ones = jnp.concatenate([(row > col).astype(BF16), jnp.ones((TILE, TILE), BF16)], axis=1)

        def tiles(rows, key_blocks, states, mask):
            chains = [(hi, hd, keys) for hi, hd in enumerate(heads) for keys in key_blocks]
            logits = [_sb_logits(qn_s[rows, hd], kn_s[keys, hd], mask) for _, hd, keys in chains]
            sums = [_split_dot(l1m, upper_ones) for _, l1m in logits]
            carry = [c for _, c in states]
            probs = []
            for (hi, _, _), (lb, _), sm in zip(chains, logits, sums):
                a = jnp.exp(lb + sm[:, :TILE] + carry[hi])
                carry[hi] = carry[hi] + sm[:, TILE:]
                probs.append((a if mask is None else jnp.where(mask, a, 0.0)).astype(BF16))
            outs = [jnp.dot(a, vb_s[keys, hd], preferred_element_type=F32) for a, (_, hd, keys) in zip(probs, chains)]
            acc = [o_acc for o_acc, _ in states]
            for (hi, _, _), o in zip(chains, outs):
                acc[hi] = acc[hi] + o
            return tuple(zip(acc, carry))

        def key_block(b):
            return pl.ds(pl.multiple_of(b * TILE, TILE), TILE)

        def qblock(i, _):
            rows = pl.ds(pl.multiple_of(i * TILE, TILE), TILE)
            zero = jnp.zeros((TILE, HEAD_DIM), F32)
            states = tiles(rows, [rows], tuple((zero, zero) for _ in heads), causal)
            states = lax.cond(i % 2 == 1, lambda st: tiles(rows, [key_block(i - 1)], st, None), lambda st: st, states)
            top = i - i % 2

            def kblocks(jj, states):
                return tiles(rows, [key_block(top - 1 - 2 * jj), key_block(top - 2 - 2 * jj)], states, None)

            states = lax.fori_loop(0, i // 2, kblocks, states)
            for hh, (hd, (o_acc, c)) in enumerate(zip(heads, states)):
                o_ref[rows, hd] = o_acc
                r_ref[rows, hd] = c
                att_ref[rows, hd] = (o_acc * _rstd(o_acc) * ga_ref[hh]).astype(BF16)
            return 0

        lax.fori_loop(0, nq, qblock, 0)

    col_blk = lambda off: pl.BlockSpec((s, wd), lambda h: (0, off + h))
    vec = pl.BlockSpec((1, HEAD_DIM), lambda h: (0, 0))
    hvec = pl.BlockSpec((hp, 1, HEAD_DIM), lambda h: (h, 0, 0))
    out = pl.BlockSpec((s, wd), lambda h: (0, h))
    w = n_heads * HEAD_DIM
    steps = n_heads // hp
    return pl.pallas_call(
        body, name="attn_fwd", grid=(steps,),
        in_specs=[col_blk(0), col_blk(steps), col_blk(2 * steps), vec, vec, hvec],
        out_specs=(out, out, out),
        out_shape=(jax.ShapeDtypeStruct((s, w), BF16), jax.ShapeDtypeStruct((s, w), F32),
                   jax.ShapeDtypeStruct((s, w), F32)),
        scratch_shapes=[pltpu.VMEM((s, wd), BF16)] * 3,
        compiler_params=_cparams(("parallel",), ATTN_VMEM_MB),
    )(p, p, p, gq, gk, ga)


def _attn_bwd(p, gq, gk, ga, o, rsum, dmix, n_heads):
    s = p.shape[0]
    nq = s // TILE

    hp = ATTN_HEADS_PER_STEP
    wd = hp * HEAD_DIM
    scale = HEAD_DIM ** -0.5

    def body(q_ref, k_ref, v_ref, gq_ref, gk_ref, ga_ref, o_ref, r_ref, dm_ref,
             dq_ref, dk_ref, dv_ref, dgq_ref, dgk_ref, dga_ref,
             qn_s, kn_s, vb_s, do_s, dqn_s, dkn_s, dv_s):
        step = pl.program_id(0)
        gqv, gkv = gq_ref[...], gk_ref[...]
        heads = [slice(hh * HEAD_DIM, (hh + 1) * HEAD_DIM) for hh in range(hp)]
        for hh, hd in enumerate(heads):
            qv = q_ref[:, hd]
            qn_s[:, hd] = (qv * _rstd(qv) * gqv).astype(BF16)
            kv = k_ref[:, hd]
            kn_s[:, hd] = (kv * _rstd(kv) * gkv).astype(BF16)
            ov = o_ref[:, hd]
            ro = _rstd(ov)
            ohat = ov * ro
            dm = dm_ref[:, hd]
            dga_ref[hh] = jnp.sum(dm * ohat, axis=0, keepdims=True)
            do_s[:, hd] = _norm_bwd(dm, ohat, ro, ga_ref[hh]).astype(BF16)
        vb_s[...] = v_ref[...].astype(BF16)
        dkn_s[...] = jnp.zeros_like(dkn_s)
        dv_s[...] = jnp.zeros_like(dv_s)
        row, col = _tile_iotas()
        causal = col < row
        ones = jnp.ones((TILE, TILE), BF16)
        incl_ones = jnp.concatenate([(row <= col).astype(BF16), ones], axis=1)
        excl_ones = jnp.concatenate([(row < col).astype(BF16), ones], axis=1)

        def tiles(rows, key_blocks, states, mask):
            chains = [(hi, hd, keys) for hi, hd in enumerate(heads) for keys in key_blocks]
            qis = [qn_s[rows, hd] for hd in heads]
            dois = [do_s[rows, hd] for hd in heads]
            logits = [_sb_logits(qis[hi], kn_s[keys, hd], mask) for hi, hd, keys in chains]
            sums = [_split_dot(l1m, incl_ones) for _, l1m in logits]
            das = [lax.dot_general(dois[hi], vb_s[keys, hd], NT_DIMS, preferred_element_type=F32)
                   for hi, hd, keys in chains]
            pfx = [st[1] for st in states]
            probs, dss = [], []
            for (hi, hd, _), (lb, _), sm, da in zip(chains, logits, sums, das):
                a = jnp.exp(lb + (r_ref[rows, hd] - pfx[hi] - sm[:, :TILE]))
                pfx[hi] = pfx[hi] + sm[:, TILE:]
                a = a if mask is None else jnp.where(mask, a, 0.0)
                probs.append(a.astype(BF16))
                dss.append(da * a)
            dsums = [_split_dot(ds, excl_ones) for ds in dss]
            pc = [st[2] for st in states]
            dzs = []
            for (hi, _, _), (lb, _), ds, dsm in zip(chains, logits, dss, dsums):
                dl1m = pc[hi] + dsm[:, :TILE]
                pc[hi] = pc[hi] + dsm[:, TILE:]
                dl1m = dl1m if mask is None else jnp.where(mask, dl1m, 0.0)
                beta = jnp.exp(lb)
                dzs.append(((ds * (1.0 - beta) - dl1m * beta) * scale).astype(BF16))
            dqs = [jnp.dot(dz, kn_s[keys, hd], preferred_element_type=F32) for dz, (_, hd, keys) in zip(dzs, chains)]
            for dz, a, (hi, hd, keys) in zip(dzs, probs, chains):
                dkn_s[keys, hd] += lax.dot_general(dz, qis[hi], TN_DIMS, preferred_element_type=F32)
                dv_s[keys, hd] += lax.dot_general(a, dois[hi], TN_DIMS, preferred_element_type=F32)
            dq_acc = [st[0] for st in states]
            for (hi, _, _), dq in zip(chains, dqs):
                dq_acc[hi] = dq_acc[hi] + dq
            return tuple(zip(dq_acc, pfx, pc))

        def key_block(b):
            return pl.ds(pl.multiple_of(b * TILE, TILE), TILE)

        def qblock(i, _):
            rows = pl.ds(pl.multiple_of(i * TILE, TILE), TILE)
            zero = jnp.zeros((TILE, HEAD_DIM), F32)

            def kblocks(jj, states):
                return tiles(rows, [key_block(2 * jj), key_block(2 * jj + 1)], states, None)

            states = lax.fori_loop(0, i // 2, kblocks, tuple((zero, zero, zero) for _ in heads))
            states = lax.cond(i % 2 == 1, lambda st: tiles(rows, [key_block(i - 1)], st, None), lambda st: st, states)
            states = tiles(rows, [rows], states, causal)
            for hd, (dq_acc, _, _) in zip(heads, states):
                dqn_s[rows, hd] = dq_acc
            return 0

        lax.fori_loop(0, nq, qblock, 0)

        def norm_in_bwd(x_ref, g, dn_s, dx_ref, dg_ref):
            part = jnp.zeros((1, HEAD_DIM), F32)
            for hd in heads:
                xv = x_ref[:, hd]
                r = _rstd(xv)
                xhat = xv * r
                dn = dn_s[:, hd]
                dx_ref[:, hd] = _norm_bwd(dn, xhat, r, g).astype(BF16)
                part = part + jnp.sum(dn * xhat, axis=0, keepdims=True)

            @pl.when(step == 0)
            def _():
                dg_ref[...] = part

            @pl.when(step > 0)
            def _():
                dg_ref[...] += part

        norm_in_bwd(q_ref, gqv, dqn_s, dq_ref, dgq_ref)
        norm_in_bwd(k_ref, gkv, dkn_s, dk_ref, dgk_ref)
        dv_ref[...] = dv_s[...].astype(BF16)

    once = pl.Buffered(1)
    steps = n_heads // hp
    col_blk = lambda off: pl.BlockSpec((s, wd), lambda h: (0, off + h), pipeline_mode=once)
    vec = pl.BlockSpec((1, HEAD_DIM), lambda h: (0, 0))
    hvec = pl.BlockSpec((hp, 1, HEAD_DIM), lambda h: (h, 0, 0))
    blk = pl.BlockSpec((s, wd), lambda h: (0, h), pipeline_mode=once)
    w = n_heads * HEAD_DIM
    big = jax.ShapeDtypeStruct((s, w), BF16)
    return pl.pallas_call(
        body, name="attn_bwd", grid=(steps,),
        in_specs=[col_blk(0), col_blk(steps), col_blk(2 * steps), vec, vec, hvec, blk, blk, blk],
        out_specs=(blk, blk, blk, vec, vec, hvec),
        out_shape=(big, big, big, jax.ShapeDtypeStruct((1, HEAD_DIM), F32), jax.ShapeDtypeStruct((1, HEAD_DIM), F32),
                   jax.ShapeDtypeStruct((n_heads, 1, HEAD_DIM), F32)),
        scratch_shapes=[pltpu.VMEM((s, wd), BF16)] * 4 + [pltpu.VMEM((s, wd), F32)] * 3,
        compiler_params=_cparams(("arbitrary",), ATTN_VMEM_MB),
    )(p, p, p, gq, gk, ga, o, rsum, dmix)


_INV_SQRT2 = 0.7071067811865476
_INV_SQRT_2PI = 0.3989422804014327


def _gelu(x):
    return 0.5 * x * (1.0 + lax.erf(x * _INV_SQRT2))


def _gelu_grad(x):
    return 0.5 * (1.0 + lax.erf(x * _INV_SQRT2)) + x * (_INV_SQRT_2PI * jnp.exp(-0.5 * x * x))


def _sgu_fwd(p, gs, w_s, b_s, gb, col0, after):
    s = p.shape[0]
    n_chunks = s // TILE
    per_trip = _pick(n_chunks, (4, 2, 1))

    def body(u_ref, v_ref, gs_ref, w_ref, b_ref, gb_ref, _, out_ref, vs_s):
        vg = _gelu(v_ref[...])
        vs_s[...] = (vg * _rstd(vg) * gs_ref[...]).astype(BF16)
        row, col = _tile_iotas()
        wt = jnp.where(col <= row, w_ref[...], 0.0).astype(BF16)
        bcol = b_ref[...]
        gbv = gb_ref[...]

        def chunks(c, _):
            rows = [pl.ds(pl.multiple_of((c * per_trip + k) * TILE, TILE), TILE) for k in range(per_trip)]
            mixed = [jnp.dot(wt, vs_s[r, :], preferred_element_type=F32) + bcol for r in rows]
            sgs = [_gelu(u_ref[r, :]) * mx for r, mx in zip(rows, mixed)]
            for r, sg in zip(rows, sgs):
                out_ref[r, :] = (sg * _rstd(sg) * gbv).astype(BF16)
            return 0

        lax.fori_loop(0, n_chunks // per_trip, chunks, 0)

    col_blk = lambda off: pl.BlockSpec((s, HEAD_DIM), lambda g: (0, off + g))
    gvec = pl.BlockSpec((None, 1, HEAD_DIM), lambda g: (g, 0, 0))
    return pl.pallas_call(
        body, name="sgu_fwd", grid=(N_GROUPS,),
        in_specs=[col_blk(col0), col_blk(col0 + N_GROUPS), gvec,
                  pl.BlockSpec((None, TILE, TILE), lambda g: (g, 0, 0)),
                  pl.BlockSpec((None, TILE, 1), lambda g: (g, 0, 0)), gvec,
                  pl.BlockSpec(after.shape, lambda g: (0, 0))],
        out_specs=pl.BlockSpec((s, HEAD_DIM), lambda g: (0, g)),
        out_shape=jax.ShapeDtypeStruct((s, N_GROUPS * HEAD_DIM), BF16),
        scratch_shapes=[pltpu.VMEM((s, HEAD_DIM), BF16)],
        compiler_params=_cparams(("parallel",)),
    )(p, p, gs, w_s, b_s, gb, after)


def _sgu_bwd(p, gs, w_s, b_s, gb, dmix, col0, dm_col0):
    s = p.shape[0]
    n_chunks = s // TILE
    per_trip = _pick(n_chunks, (4, 2, 1))

    def body(u_ref, v_ref, gs_ref, w_ref, b_ref, gb_ref, dm_ref,
             du_ref, dv_ref, dgs_ref, dw_ref, db_ref, dgb_ref, vs_s, dvs_s):
        gsv = gs_ref[...]
        gbv = gb_ref[...]
        vg = _gelu(v_ref[...])
        vs_s[...] = (vg * _rstd(vg) * gsv).astype(BF16)
        row, col = _tile_iotas()
        causal = col <= row
        wt = jnp.where(causal, w_ref[...], 0.0).astype(BF16)
        bcol = b_ref[...]

        def chunks(c, carry):
            dw_acc, db_acc, dgb_acc = carry
            rows = [pl.ds(pl.multiple_of((c * per_trip + k) * TILE, TILE), TILE) for k in range(per_trip)]
            vss = [vs_s[r, :] for r in rows]
            mixed = [jnp.dot(wt, vs, preferred_element_type=F32) + bcol for vs in vss]
            dmbs = []
            for r, mx in zip(rows, mixed):
                u_pre = u_ref[r, :]
                u = _gelu(u_pre)
                sg = u * mx
                rs = _rstd(sg)
                sghat = sg * rs
                dm = dm_ref[r, :]
                dsg = _norm_bwd(dm, sghat, rs, gbv)
                dgb_acc = dgb_acc + jnp.sum(dm * sghat, axis=0, keepdims=True)
                du_ref[r, :] = (dsg * mx * _gelu_grad(u_pre)).astype(BF16)
                dmixed = dsg * u
                db_acc = db_acc + jnp.sum(dmixed, axis=1, keepdims=True)
                dmbs.append(dmixed.astype(BF16))
            for dmb, vs in zip(dmbs, vss):
                dw_acc = dw_acc + lax.dot_general(dmb, vs, NT_DIMS, preferred_element_type=F32)
            for r, dmb in zip(rows, dmbs):
                dvs_s[r, :] = lax.dot_general(wt, dmb, TN_DIMS, preferred_element_type=F32)
            return dw_acc, db_acc, dgb_acc

        dw_acc, db_acc, dgb_acc = lax.fori_loop(
            0, n_chunks // per_trip, chunks,
            (jnp.zeros((TILE, TILE), F32), jnp.zeros((TILE, 1), F32), jnp.zeros((1, HEAD_DIM), F32)))
        dw_ref[...] = jnp.where(causal, dw_acc, 0.0)
        db_ref[...] = db_acc
        dgb_ref[...] = dgb_acc
        v_pre = v_ref[...]
        vg = _gelu(v_pre)
        rv = _rstd(vg)
        vhat = vg * rv
        dvs = dvs_s[...]
        dgs_ref[...] = jnp.sum(dvs * vhat, axis=0, keepdims=True)
        dv_ref[...] = (_norm_bwd(dvs, vhat, rv, gsv) * _gelu_grad(v_pre)).astype(BF16)

    col_blk = lambda off: pl.BlockSpec((s, HEAD_DIM), lambda g: (0, off + g))
    gvec = pl.BlockSpec((None, 1, HEAD_DIM), lambda g: (g, 0, 0))
    wspec = pl.BlockSpec((None, TILE, TILE), lambda g: (g, 0, 0))
    bspec = pl.BlockSpec((None, TILE, 1), lambda g: (g, 0, 0))
    blk = pl.BlockSpec((s, HEAD_DIM), lambda g: (0, g))
    big = jax.ShapeDtypeStruct((s, N_GROUPS * HEAD_DIM), BF16)
    gshape = jax.ShapeDtypeStruct((N_GROUPS, 1, HEAD_DIM), F32)
    return pl.pallas_call(
        body, name="sgu_bwd", grid=(N_GROUPS,),
        in_specs=[col_blk(col0), col_blk(col0 + N_GROUPS), gvec, wspec, bspec, gvec, col_blk(dm_col0)],
        out_specs=(blk, blk, gvec, wspec, bspec, gvec),
        out_shape=(big, big, gshape, jax.ShapeDtypeStruct((N_GROUPS, TILE, TILE), F32),
                   jax.ShapeDtypeStruct((N_GROUPS, TILE, 1), F32), gshape),
        scratch_shapes=[pltpu.VMEM((s, HEAD_DIM), BF16), pltpu.VMEM((s, HEAD_DIM), F32)],
        compiler_params=_cparams(("parallel",)),
    )(p, p, gs, w_s, b_s, gb, dmix)


SUBLANES = 8


def _shift_down(x, n):
    rolled = pltpu.roll(x, n, 0)
    edge = lax.broadcasted_iota(jnp.int32, (SUBLANES, x.shape[1]), 0)
    return jnp.concatenate([jnp.where(edge >= n, rolled[:SUBLANES], 0.0), rolled[SUBLANES:]], axis=0)


def _shift_up(x, n):
    s = x.shape[0]
    rolled = pltpu.roll(x, s - n, 0)
    edge = lax.broadcasted_iota(jnp.int32, (SUBLANES, x.shape[1]), 0)
    return jnp.concatenate([rolled[:s - SUBLANES], jnp.where(edge < SUBLANES - n, rolled[s - SUBLANES:], 0.0)], axis=0)


def _conv(x, w, b):
    x1, x2 = _shift_down(x, 1), _shift_down(x, 2)
    return b + w[0:1, :] * x2 + w[1:2, :] * x1 + w[2:3, :] * x, x1, x2


def _conv_specs(s, tn):
    xspec = pl.BlockSpec((2, s, tn), lambda j: (0, 0, j))
    wspec = pl.BlockSpec((2, CONV_WIDTH, tn), lambda j: (0, 0, j))
    bspec = pl.BlockSpec((2, 1, tn), lambda j: (0, 0, j))
    return xspec, wspec, bspec


def _conv_gate_fwd(up, cw, cb):
    _, s, f = up.shape
    tn = _pick(f, (256, 128))

    def body(x_ref, w_ref, b_ref, act_ref):
        gate = _conv(x_ref[0], w_ref[0], b_ref[0])[0]
        val = _conv(x_ref[1], w_ref[1], b_ref[1])[0]
        act_ref[...] = (gate * jax.nn.sigmoid(gate) * val).astype(BF16)

    xspec, wspec, bspec = _conv_specs(s, tn)
    return pl.pallas_call(
        body, name="conv_gate_fwd", grid=(f // tn,), in_specs=[xspec, wspec, bspec],
        out_specs=pl.BlockSpec((s, tn), lambda j: (0, j)), out_shape=jax.ShapeDtypeStruct((s, f), BF16),
        compiler_params=_cparams(("parallel",)),
    )(up, cw, cb)


def _down_dx_conv_gate_bwd(up, cw, cb, dy, wdown):
    _, s, f = up.shape
    d = dy.shape[1]
    tn = _pick(f, (256, 128))

    def body(x_ref, w_ref, b_ref, dy_ref, wd_ref, dx_ref, dw_ref, db_ref):
        da = lax.dot_general(dy_ref[...], wd_ref[...], NT_DIMS, preferred_element_type=F32)
        xg, xv = x_ref[0], x_ref[1]
        wg, wv = w_ref[0], w_ref[1]
        gate, xg1, xg2 = _conv(xg, wg, b_ref[0])
        val, xv1, xv2 = _conv(xv, wv, b_ref[1])
        sig = jax.nn.sigmoid(gate)
        dval = da * (gate * sig)
        dgate = da * val * (sig * (1.0 + gate * (1.0 - sig)))
        for half, (x, x1, x2, w, dz) in enumerate(((xg, xg1, xg2, wg, dgate), (xv, xv1, xv2, wv, dval))):
            dx_ref[half] = (w[2:3, :] * dz + w[1:2, :] * _shift_up(dz, 1) + w[0:1, :] * _shift_up(dz, 2)).astype(BF16)
            dw_ref[half, 0:1, :] = jnp.sum(dz * x2, axis=0, keepdims=True)
            dw_ref[half, 1:2, :] = jnp.sum(dz * x1, axis=0, keepdims=True)
            dw_ref[half, 2:3, :] = jnp.sum(dz * x, axis=0, keepdims=True)
            db_ref[half] = jnp.sum(dz, axis=0, keepdims=True)

    xspec, wspec, bspec = _conv_specs(s, tn)
    return pl.pallas_call(
        body, name="down_dx_conv_gate_bwd", grid=(f // tn,),
        in_specs=[xspec, wspec, bspec, pl.BlockSpec((s, d), lambda j: (0, 0)), pl.BlockSpec((tn, d), lambda j: (j, 0))],
        out_specs=(xspec, wspec, bspec),
        out_shape=(jax.ShapeDtypeStruct((2, s, f), BF16), jax.ShapeDtypeStruct((2, CONV_WIDTH, f), F32),
                   jax.ShapeDtypeStruct((2, 1, f), F32)),
        compiler_params=_cparams(("parallel",), 56),
    )(up, cw, cb, dy, wdown)


def _mesh_pos():
    return lax.axis_index("x"), lax.axis_index("y"), lax.axis_index("c")


def _remote(src, dst, send_sem, recv_sem, to):
    return pltpu.make_async_remote_copy(src_ref=src, dst_ref=dst, send_sem=send_sem, recv_sem=recv_sem,
                                        device_id=to, device_id_type=pl.DeviceIdType.MESH)


HBM_SPEC = pl.BlockSpec(memory_space=pltpu.HBM)
SEM_SPEC = pl.BlockSpec(memory_space=pltpu.SEMAPHORE)
ANY_SPEC = pl.BlockSpec(memory_space=pl.ANY)
TOKEN_SPEC = pl.BlockSpec(memory_space=pltpu.VMEM)
TOKEN_SHAPE = jax.ShapeDtypeStruct((8, 128), F32)
DATAFLOW = pltpu.SideEffectType.DATAFLOW_SIDE_EFFECTING
GATHER_PLANE = (2, 4, 6)


def _slot(pos):
    return 4 * pos[0] + 2 * pos[1] + pos[2]


def _flip(pos, k):
    return (pos[0] ^ ((k >> 2) & 1), pos[1] ^ ((k >> 1) & 1), pos[2] ^ (k & 1))


def _hbm(a):
    return pltpu.with_memory_space_constraint(a, pltpu.HBM)


def _hbm_shapes(arrays):
    return tuple(pltpu.HBM(a.shape, a.dtype) for a in arrays)


class _Split:
    def __init__(self, n, outs, has_sems):
        k = 2 * n if has_sems else 0
        self.n = n
        self.sems = list(outs[:k])
        self.bufs = list(outs[k:k + 2 * n])
        self.token = outs[-1]


def _split_call(name, body, bufs, sems_in, makes_sems, after):
    n = len(bufs) // 2
    k = 2 * n if makes_sems else 0
    m = len(sems_in)
    afters = list(after) if isinstance(after, (list, tuple)) else [after]
    na = len(afters)

    def wrapped(*refs):
        srcs, dsts = refs[:n], refs[n:2 * n]
        s_in = refs[2 * n:2 * n + m]
        s_out = refs[2 * n + m + na:2 * n + m + na + k]
        token, local_sems = refs[-2], refs[-1]
        body(srcs, dsts, s_in, s_out, local_sems)
        token[...] = jnp.zeros_like(token)

    outs = pl.pallas_call(
        wrapped, name=name,
        out_shape=(pltpu.SemaphoreType.DMA(()),) * k + _hbm_shapes(bufs) + (TOKEN_SHAPE,),
        in_specs=[HBM_SPEC] * (2 * n) + [SEM_SPEC] * m + [ANY_SPEC] * na,
        out_specs=(SEM_SPEC,) * k + (HBM_SPEC,) * (2 * n) + (TOKEN_SPEC,),
        input_output_aliases={i: k + i for i in range(2 * n)},
        scratch_shapes=[pltpu.SemaphoreType.DMA((n,))],
        compiler_params=pltpu.CompilerParams(has_side_effects=DATAFLOW),
    )(*[_hbm(b) for b in bufs], *sems_in, *afters)
    return _Split(n, outs, makes_sems)


def _wait_slots(land, count, send_sem, recv_sem, me, send=False, recv=False):
    span = land.at[pl.ds(0, count)]
    cp = _remote(span, span, send_sem, recv_sem, me)
    if send:
        cp.wait_send()
    if recv:
        cp.wait_recv()


def _gather_start(name, shards, after):
    n = len(shards)
    my_slot = _slot(_mesh_pos())
    lands = [lax.dynamic_update_slice(lax.empty((N_DEV,) + w.shape, w.dtype), w[None], (my_slot, 0, 0)) for w in shards]

    def body(srcs, dsts, _, sems, local_sems):
        me = _mesh_pos()
        for a in range(n):
            for k in (1,) + GATHER_PLANE:
                _remote(srcs[a], dsts[a].at[_slot(me)], sems[a], sems[n + a], _flip(me, k)).start()

    return _split_call(name, body, list(shards) + lands, [], True, after)


def _gather_forward(name, started, after):
    n = started.n

    def body(srcs, dsts, sems_a, sems_b, local_sems):
        me = _mesh_pos()
        sibling = _flip(me, 1)
        for a in range(n):
            _wait_slots(dsts[a], 4, sems_a[a], sems_a[n + a], me, recv=True)
            for k in GATHER_PLANE:
                block = dsts[a].at[_slot(_flip(me, k))]
                _remote(block, block, sems_b[a], sems_b[n + a], sibling).start()
        for a in range(n):
            _wait_slots(dsts[a], 4, sems_a[a], sems_a[n + a], me, send=True)

    return _split_call(name, body, started.bufs, started.sems, True, after)


def _gather_finish(name, forwarded, after):
    n = forwarded.n

    def body(srcs, dsts, sems_b, _, local_sems):
        me = _mesh_pos()
        for a in range(n):
            _wait_slots(dsts[a], 3, sems_b[a], sems_b[n + a], me, send=True, recv=True)

    return _split_call(name, body, forwarded.bufs, forwarded.sems, False, after).bufs[n:]


def _exchange_start(name, blocked, after):
    n = len(blocked)
    my_slot = _slot(_mesh_pos())
    rows = [w.shape[-2] // (N_DEV if w.ndim == 2 else 1) for w in blocked]

    def block(ref, a, slot):
        if len(ref.shape) == 3:
            return ref.at[slot]
        return ref.at[pl.ds(pl.multiple_of(slot * rows[a], 16), rows[a])]

    lands = []
    for w, r in zip(blocked, rows):
        mine = lax.dynamic_slice_in_dim(w, my_slot, 1, 0) if w.ndim == 3 else lax.dynamic_slice_in_dim(w, my_slot * r, r, 0)[None]
        lands.append(lax.dynamic_update_slice(lax.empty((N_DEV, r, w.shape[-1]), w.dtype), mine, (my_slot, 0, 0)))

    def body(srcs, dsts, _, sems, local_sems):
        me = _mesh_pos()
        for a in range(n):
            for k in range(1, N_DEV):
                peer = _flip(me, k)
                _remote(block(srcs[a], a, _slot(peer)), dsts[a].at[_slot(me)], sems[a], sems[n + a], peer).start()

    return _split_call(name, body, list(blocked) + lands, [], True, after)


def _exchange_finish(name, started, after):
    n = started.n

    def body(srcs, dsts, sems, _, local_sems):
        me = _mesh_pos()
        for a in range(n):
            _wait_slots(dsts[a], N_DEV - 1, sems[a], sems[n + a], me, send=True, recv=True)

    return _split_call(name, body, started.bufs, started.sems, False, after).bufs[n:]


def _broadcast_start(name, arrays, after):
    n = len(arrays)
    my_slot = _slot(_mesh_pos())
    lands = [lax.dynamic_update_slice(lax.empty((N_DEV,) + w.shape, w.dtype), w[None], (my_slot, 0, 0)) for w in arrays]

    def body(srcs, dsts, _, sems, local_sems):
        me = _mesh_pos()
        for a in range(n):
            for k in range(1, N_DEV):
                _remote(srcs[a], dsts[a].at[_slot(me)], sems[a], sems[n + a], _flip(me, k)).start()

    return _split_call(name, body, list(arrays) + lands, [], True, after)


def _adamw_math(w, g, m, v):
    m = ADAM_B1 * m + (1.0 - ADAM_B1) * g
    v = ADAM_B2 * v + (1.0 - ADAM_B2) * (g * g)
    m_hat = m / (1.0 - ADAM_B1 ** ADAM_STEP)
    v_hat = v / (1.0 - ADAM_B2 ** ADAM_STEP)
    delta = -ADAM_LR * (m_hat / (jnp.sqrt(v_hat) + ADAM_EPS) + ADAM_WD * w)
    return delta, m, v


def _adamw(name, w, m, v, parts, layer, prev=None):
    _, r, c = w.shape
    tr = _pick(r, tuple(t for t in (256, 128, 64, 32, 16) if t * c <= ADAMW_TILE_ELEMS))
    n_prev = 0 if prev is None else 4

    def body(*refs):
        w_ref, m_ref, v_ref, p_ref = refs[:4]
        g_ref, d_ref, nm_ref, nv_ref = refs[4 + n_prev:]
        g = p_ref[0].astype(F32)
        for src in range(1, N_DEV):
            g = g + p_ref[src].astype(F32)
        delta, nm, nv = _adamw_math(w_ref[...], g, m_ref[...], v_ref[...])
        g_ref[...] = g
        d_ref[...] = delta
        nm_ref[...] = nm
        nv_ref[...] = nv

    wspec = pl.BlockSpec((None, tr, c), lambda i: (layer, i, 0))
    pspec = pl.BlockSpec((N_DEV, tr, c), lambda i: (0, i, 0))
    shp = jax.ShapeDtypeStruct(w.shape, F32)
    return pl.pallas_call(
        body, name=name, grid=(r // tr,), in_specs=[wspec] * 3 + [pspec] + [ANY_SPEC] * n_prev,
        out_specs=(wspec,) * 4, out_shape=(shp,) * 4, input_output_aliases={4 + j: j for j in range(n_prev)},
        compiler_params=_cparams(("parallel",)),
    )(w, m, v, parts, *([] if prev is None else prev))


PACK_TILE = 8 * 128


def _pack(arrays):
    flat = []
    for a in arrays:
        v = a.reshape(-1)
        pad = (-v.shape[0]) % PACK_TILE
        flat.append(jnp.pad(v, (0, pad)) if pad else v)
    return jnp.concatenate(flat).reshape(-1, 128)


def _unpack(buf, like):
    flat = buf.reshape(-1)
    out, off = [], 0
    for a in like:
        n = 1
        for dim in a.shape:
            n *= dim
        out.append(flat[off:off + n].reshape(a.shape))
        off += n + (-n) % PACK_TILE
    return out


def _sum_slots(gathered):
    _, r, c = gathered.shape

    def body(x_ref, o_ref):
        acc = x_ref[0]
        for src in range(1, N_DEV):
            acc = acc + x_ref[src]
        o_ref[...] = acc

    return pl.pallas_call(body, name="small_grad_sum", out_shape=jax.ShapeDtypeStruct((r, c), F32))(gathered)


def _adamw_small(w, g, m, v):
    shp = jax.ShapeDtypeStruct(w.shape, F32)

    def body(w_ref, g_ref, m_ref, v_ref, d_ref, nm_ref, nv_ref):
        delta, nm, nv = _adamw_math(w_ref[...], g_ref[...], m_ref[...], v_ref[...])
        d_ref[...] = delta
        nm_ref[...] = nm
        nv_ref[...] = nv

    return pl.pallas_call(body, name="adamw_small", out_shape=(shp,) * 3)(w, g, m, v)


def kernel(x, attn_norm_g, w_in, q_norm_g, k_norm_g, sgu_norm_g, sgu_w, sgu_b, out_norm_a_g, out_norm_b_g, w_out, ffn_norm_g, w_up, conv_w, conv_b, w_down, loss_target, m_attn_norm_g, m_w_in, m_q_norm_g, m_k_norm_g, m_sgu_norm_g, m_sgu_w, m_sgu_b, m_out_norm_a_g, m_out_norm_b_g, m_w_out, m_ffn_norm_g, m_w_up, m_conv_w, m_conv_b, m_w_down, v_attn_norm_g, v_w_in, v_q_norm_g, v_k_norm_g, v_sgu_norm_g, v_sgu_w, v_sgu_b, v_out_norm_a_g, v_out_norm_b_g, v_w_out, v_ffn_norm_g, v_w_up, v_conv_w, v_conv_b, v_w_down):
    depth = w_in.shape[0]
    s, d = x.shape[1], x.shape[2]
    n_heads = (d // 2) // HEAD_DIM
    sgu_col0 = 3 * n_heads
    f2 = w_up.shape[2] * N_DEV
    ff = f2 // 2
    my_slot = 4 * lax.axis_index("x") + 2 * lax.axis_index("y") + lax.axis_index("c")

    wb = [(w_in[l].astype(BF16), w_out[l].astype(BF16), w_up[l].astype(BF16), w_down[l].astype(BF16))
          for l in range(depth)]
    groups = {"in0": [wb[0][0]], "out0": [wb[0][1], conv_w.reshape(depth * CONV_WIDTH, -1)], "up0": [wb[0][2]],
              "down0": [wb[0][3]]}
    for l in range(1, depth):
        groups[f"in{l}"] = [wb[l][0], wb[l][1]]
        groups[f"ffn{l}"] = [wb[l][2], wb[l][3]]
    token = attn_norm_g
    started = {}
    for gname, group in groups.items():
        started[gname] = _gather_start(f"gather_{gname}_start", group, token)
        token = started[gname].token

    def forward(gname, after):
        return _gather_forward(f"gather_{gname}_forward", started[gname], after)

    def finish(gname, forwarded, after):
        return _gather_finish(f"gather_{gname}_finish", forwarded, after)

    conv_b_all = conv_b.reshape(depth, 2, 1, ff)
    sgu_b_col = sgu_b[..., None]
    fwd_in = forward("in0", token)
    win_g = finish("in0", fwd_in, fwd_in.token)[0]

    xs = x[0]
    saved = []
    gathered = []
    for l in range(depth):
        g1 = attn_norm_g[l][None]
        g2 = ffn_norm_g[l][None]
        gq, gk = q_norm_g[l][None], k_norm_g[l][None]
        ga = out_norm_a_g[l][:, None, :]
        gs = sgu_norm_g[l][:, None, :]
        gb = out_norm_b_g[l][:, None, :]
        h1 = _rmsnorm_fwd("attn_norm_fwd", xs, g1)
        p = _mm_nn_blocked("in_proj", h1, win_g, F32)
        att, o, rsum = _attn_fwd(p, gq, gk, ga, n_heads)
        if l == 0:
            fwd_out = forward("out0", att)
            fwd_up = forward("up0", fwd_out.token)
        else:
            fwd_up = forward(f"ffn{l}", att)
        sg = _sgu_fwd(p, gs, sgu_w[l], sgu_b_col[l], gb, sgu_col0, fwd_up.token)
        mix = jnp.concatenate([att, sg], axis=-1)
        if l == 0:
            wout_g, cw = finish("out0", fwd_out, mix)
            cw = jnp.transpose(cw.reshape(N_DEV, depth, CONV_WIDTH, -1), (1, 2, 0, 3)).reshape(depth, CONV_WIDTH, 2, ff)
            conv_w_all = jnp.transpose(cw, (0, 2, 1, 3))
        x1 = _mm_nn_res("out_proj", mix, wout_g.reshape(d, d), xs)
        h2 = _rmsnorm_fwd("ffn_norm_fwd", x1, g2)
        if l == 0:
            wup_g = finish("up0", fwd_up, h2)[0]
            fwd_down = forward("down0", wup_g)
            up = _mm_nn_blocked("up_proj", h2, wup_g, F32, halves=True, after=fwd_down.token)
            wdown_g = finish("down0", fwd_down, up)[0]
        else:
            wup_g, wdown_g = finish(f"ffn{l}", fwd_up, h2)
            up = _mm_nn_blocked("up_proj", h2, wup_g, F32, halves=True)
        act = _conv_gate_fwd(up, conv_w_all[l], conv_b_all[l])
        saved.append((xs, h1, p, o, rsum, mix, x1, h2, up, act))
        gathered.append((win_g, wout_g, wup_g, wdown_g))
        if l + 1 < depth:
            fwd_in = forward(f"in{l + 1}", act)
            x2 = _mm_nn_res("down_proj", act, wdown_g.reshape(ff, d), x1, after=fwd_in.token)
            win_g, wout_g = finish(f"in{l + 1}", fwd_in, x2)
        else:
            x2 = _mm_nn_res("down_proj", act, wdown_g.reshape(ff, d), x1)
        xs = x2

    dx, dxb, loss_vec = _loss_head(xs, loss_target[0])
    loss = lax.psum(loss_vec[0, 0], MESH_AXES)

    exchanges = []
    small = [None] * depth
    for l in reversed(range(depth)):
        xs0, h1, p, o, rsum, mix, x1, h2, up, act = saved[l]
        win_g, wout_g, wup_g, wdown_g = gathered[l]
        wout_full = wout_g.reshape(d, d)
        wdown_full = wdown_g.reshape(ff, d)
        g1 = attn_norm_g[l][None]
        g2 = ffn_norm_g[l][None]
        gq, gk = q_norm_g[l][None], k_norm_g[l][None]
        ga = out_norm_a_g[l][:, None, :]
        gs = sgu_norm_g[l][:, None, :]
        gb = out_norm_b_g[l][:, None, :]
        d_wdown = _mm_tn_plain("down_proj_dw", act, dxb)
        dup, d_cw, d_cb = _down_dx_conv_gate_bwd(up, conv_w_all[l], conv_b_all[l], dxb, wdown_full)
        d_wup = _mm_tn_blocked("up_proj_dw", h2, dup, N_DEV, halves=True)
        exchanges.append((l, ("w_up", "w_down"), _exchange_start(
            f"grad_ffn{l}_start", [d_wup, d_wdown], d_wup)))
        dh2 = _mm_nt_blocked("up_proj_dx", dup, wup_g, halves=True, after=exchanges[-1][2].token)
        dx, dxb, d_g2 = _rmsnorm_bwd("ffn_norm_bwd", dh2, x1, g2, dx)
        d_wout = _mm_tn_plain("out_proj_dw", mix, dxb)
        dmix = _mm_nt_plain("out_proj_dx", dxb, wout_full)
        dq, dk, dv, d_gq, d_gk, d_ga = _attn_bwd(p, gq, gk, ga, o, rsum, dmix, n_heads)
        du, dvs, d_gs, d_sw, d_sb, d_gb = _sgu_bwd(p, gs, sgu_w[l], sgu_b_col[l], gb, dmix, sgu_col0, n_heads)
        dp = jnp.concatenate([dq, dk, dv, du, dvs], axis=-1)
        d_win = _mm_tn_blocked("in_proj_dw", h1, dp, N_DEV)
        exchanges.append((l, ("w_in", "w_out"), _exchange_start(
            f"grad_mix{l}_start", [d_win, d_wout], d_win)))
        dh1 = _mm_nt_blocked("in_proj_dx", dp, win_g, after=exchanges[-1][2].token)
        dx, dxb, d_g1 = _rmsnorm_bwd("attn_norm_bwd", dh1, xs0, g1, dx)
        small[l] = dict(attn_norm_g=d_g1[0], q_norm_g=d_gq[0], k_norm_g=d_gk[0], sgu_norm_g=d_gs[:, 0], sgu_w=d_sw,
                        sgu_b=d_sb[..., 0], out_norm_a_g=d_ga[:, 0], out_norm_b_g=d_gb[:, 0], ffn_norm_g=d_g2[0],
                        conv_w=jnp.transpose(d_cw, (1, 0, 2)).reshape(CONV_WIDTH, f2), conv_b=d_cb.reshape(f2))
    grad_x = dx[None]

    small_names = ["attn_norm_g", "q_norm_g", "k_norm_g", "sgu_norm_g", "sgu_w", "sgu_b", "out_norm_a_g",
                   "out_norm_b_g", "ffn_norm_g", "conv_b", "conv_w"]
    small_g = [jnp.stack([small[l][n] for l in range(depth)]) for n in small_names]
    small_sent = _broadcast_start("grad_small_start", [_pack(small_g)], dx)

    res = {}
    big = dict(w_in=(w_in, m_w_in, v_w_in), w_out=(w_out, m_w_out, v_w_out), w_up=(w_up, m_w_up, v_w_up),
               w_down=(w_down, m_w_down, v_w_down))
    after = [small_sent.token]
    for l, names, ex in exchanges:
        stage = "ffn" if names[0] == "w_up" else "mix"
        landed = _exchange_finish(f"grad_{stage}{l}_finish", ex, after)
        after = []
        for name, parts in zip(names, landed):
            w, m, v = big[name]
            res[name] = _adamw(f"adamw_{name}", w, m, v, parts, l, res.get(name))
            after.append(res[name][0])
    small_all = _exchange_finish("grad_small_finish", small_sent, after)[0]
    small_sum = _unpack(_sum_slots(small_all), small_g)
    g_small = dict(zip(small_names, small_sum))
    cwn = conv_w.shape[2]
    g_small["conv_w"] = lax.dynamic_slice_in_dim(g_small["conv_w"], my_slot * cwn, cwn, axis=2)
    small_w = dict(attn_norm_g=(attn_norm_g, m_attn_norm_g, v_attn_norm_g), q_norm_g=(q_norm_g, m_q_norm_g, v_q_norm_g),
                   k_norm_g=(k_norm_g, m_k_norm_g, v_k_norm_g), sgu_norm_g=(sgu_norm_g, m_sgu_norm_g, v_sgu_norm_g),
                   sgu_w=(sgu_w, m_sgu_w, v_sgu_w), sgu_b=(sgu_b, m_sgu_b, v_sgu_b),
                   out_norm_a_g=(out_norm_a_g, m_out_norm_a_g, v_out_norm_a_g),
                   out_norm_b_g=(out_norm_b_g, m_out_norm_b_g, v_out_norm_b_g),
                   ffn_norm_g=(ffn_norm_g, m_ffn_norm_g, v_ffn_norm_g), conv_b=(conv_b, m_conv_b, v_conv_b),
                   conv_w=(conv_w, m_conv_w, v_conv_w))
    like = [small_w[n][0] for n in small_names]
    pw = _pack([small_w[n][0] for n in small_names])
    pm = _pack([small_w[n][1] for n in small_names])
    pv = _pack([small_w[n][2] for n in small_names])
    pg = _pack([g_small[n].reshape(small_w[n][0].shape) for n in small_names])
    pd, pnm, pnv = _adamw_small(pw, pg, pm, pv)
    for n, dlt, nm, nv in zip(small_names, _unpack(pd, like), _unpack(pnm, like), _unpack(pnv, like)):
        res[n] = (g_small[n].reshape(small_w[n][0].shape), dlt, nm, nv)

    order = ["attn_norm_g", "w_in", "q_norm_g", "k_norm_g", "sgu_norm_g", "sgu_w", "sgu_b", "out_norm_a_g",
             "out_norm_b_g", "w_out", "ffn_norm_g", "w_up", "conv_w", "conv_b", "w_down"]
    outs = [loss, grad_x]
    for field in range(4):
        outs += [res[n][field] for n in order]
    return tuple(outs)
```

```python
import functools

import jax
import jax.numpy as jnp
from jax import lax
from jax.experimental import pallas as pl
from jax.experimental.pallas import tpu as pltpu

F32 = jnp.float32
BF16 = jnp.bfloat16
EPS = 1e-6
HEAD_DIM = 128
TILE = 128
ATTN_VMEM_MB = 58
ATTN_HEADS_PER_STEP = 4
N_GROUPS = 8
CONV_WIDTH = 3
N_DEV = 8
MESH_AXES = ("x", "y", "c")
MIB = 1024 * 1024

ADAM_LR = 0.001
ADAM_B1 = 0.9
ADAM_B2 = 0.999
ADAM_EPS = 1e-08
ADAM_WD = 0.01
ADAM_STEP = 10
ADAMW_TILE_ELEMS = 160 * 1024

NT_DIMS = (((1,), (1,)), ((), ()))
NN_DIMS = (((1,), (0,)), ((), ()))
TN_DIMS = (((0,), (0,)), ((), ()))


def _cparams(sem, vmem_mb=48):
    return pltpu.CompilerParams(dimension_semantics=sem, vmem_limit_bytes=vmem_mb * MIB)


def _pick(n, cands):
    for c in cands:
        if n % c == 0:
            return c
    return n


def _mm(name, grid, ins, in_specs, out_shape, out_spec, dims, has_res=False, parts=None, vmem_mb=56, after=None):
    n_in = 2 + has_res + (after is not None)
    if after is not None:
        ins = tuple(ins) + (after,)
        in_specs = list(in_specs) + [pl.BlockSpec(after.shape, lambda *_: (0, 0))]

    def body(*refs):
        a_ref, b_ref = refs[:2]
        o_ref = refs[n_in]
        if parts is None:
            acc = lax.dot_general(a_ref[...], b_ref[...], dims, preferred_element_type=F32)
        else:
            acc = None
            for part in parts:
                a, b = part(a_ref, b_ref)
                prod = lax.dot_general(a, b, dims, preferred_element_type=F32)
                acc = prod if acc is None else acc + prod
        if has_res:
            acc = acc + refs[2][...]
        o_ref[...] = acc.astype(o_ref.dtype)

    return pl.pallas_call(
        body, name=name, grid=grid, in_specs=in_specs, out_specs=out_spec, out_shape=out_shape,
        compiler_params=_cparams(("parallel",) * len(grid), vmem_mb),
    )(*ins)


def _rows_for(m, row_bytes, budget):
    return _pick(m, tuple(t for t in (2048, 1024, 512, 256, 128) if t * row_bytes <= budget))


def _mm_nn_blocked(name, a, wb, out_dtype, halves=False, after=None):
    m, k = a.shape
    nb, _, bn = wb.shape
    tm = _rows_for(m, bn * jnp.dtype(out_dtype).itemsize, 6 * MIB)
    a_spec = pl.BlockSpec((tm, k), lambda j, i: (i, 0))
    b_spec = pl.BlockSpec((None, k, bn), lambda j, i: (j, 0, 0))
    if halves:
        hb = nb // 2
        out_shape = jax.ShapeDtypeStruct((2, m, hb * bn), out_dtype)
        o_spec = pl.BlockSpec((None, tm, bn), lambda j, i: (j // hb, i, j % hb))
    else:
        out_shape = jax.ShapeDtypeStruct((m, nb * bn), out_dtype)
        o_spec = pl.BlockSpec((tm, bn), lambda j, i: (i, j))
    return _mm(name, (nb, m // tm), (a, wb), [a_spec, b_spec], out_shape, o_spec, NN_DIMS, after=after)


def _mm_nn_res(name, a, w, res, after=None):
    m, k = a.shape
    n = w.shape[1]
    tm = _pick(m, (512, 256, 128))
    tn = _rows_for(n, k * 2, 12 * MIB)
    a_spec = pl.BlockSpec((tm, k), lambda j, i: (i, 0))
    b_spec = pl.BlockSpec((k, tn), lambda j, i: (0, j))
    r_spec = pl.BlockSpec((tm, tn), lambda j, i: (i, j))
    o_spec = pl.BlockSpec((tm, tn), lambda j, i: (i, j))
    return _mm(name, (n // tn, m // tm), (a, w, res), [a_spec, b_spec, r_spec], jax.ShapeDtypeStruct((m, n), F32),
               o_spec, NN_DIMS, has_res=True, after=after)


def _mm_nt_blocked(name, dy, wb, halves=False, after=None):
    nb, n, bn = wb.shape
    m = dy.shape[-2]
    tm = _pick(m, (512, 256, 128))
    tn = _rows_for(n, nb * bn * 2, 12 * MIB)
    if halves:
        hb = nb // 2
        a_spec = pl.BlockSpec((2, tm, hb * bn), lambda j, i: (0, i, 0))
        a_part = lambda kk: (lambda a_ref: a_ref[kk // hb, :, (kk % hb) * bn:(kk % hb + 1) * bn])
    else:
        a_spec = pl.BlockSpec((tm, nb * bn), lambda j, i: (i, 0))
        a_part = lambda kk: (lambda a_ref: a_ref[:, kk * bn:(kk + 1) * bn])
    parts = [(lambda a_ref, b_ref, kk=kk, sel=a_part(kk): (sel(a_ref), b_ref[kk])) for kk in range(nb)]
    b_spec = pl.BlockSpec((nb, tn, bn), lambda j, i: (0, j, 0))
    o_spec = pl.BlockSpec((tm, tn), lambda j, i: (i, j))
    return _mm(name, (n // tn, m // tm), (dy, wb), [a_spec, b_spec], jax.ShapeDtypeStruct((m, n), F32), o_spec,
               NT_DIMS, parts=parts, after=after)


def _mm_nt_plain(name, dy, w, out_dtype=F32):
    m, k = dy.shape
    n = w.shape[0]
    tm = _rows_for(m, k * 2, 8 * MIB)
    tn = _pick(n, (512, 256, 128))
    a_spec = pl.BlockSpec((tm, k), lambda j, i: (i, 0))
    b_spec = pl.BlockSpec((tn, k), lambda j, i: (j, 0))
    o_spec = pl.BlockSpec((tm, tn), lambda j, i: (i, j))
    return _mm(name, (n // tn, m // tm), (dy, w), [a_spec, b_spec], jax.ShapeDtypeStruct((m, n), out_dtype), o_spec,
               NT_DIMS)


def _mm_tn_blocked(name, a, dy, nb, halves=False):
    s, k1 = a.shape
    bn = (dy.shape[-1] * (2 if halves else 1)) // nb
    tm = _rows_for(k1, bn * 2, 6 * MIB)
    a_spec = pl.BlockSpec((s, tm), lambda j, i: (0, i))
    if halves:
        hb = nb // 2
        b_spec = pl.BlockSpec((None, s, bn), lambda j, i: (j // hb, 0, j % hb))
    else:
        b_spec = pl.BlockSpec((s, bn), lambda j, i: (0, j))
    o_spec = pl.BlockSpec((None, tm, bn), lambda j, i: (j, i, 0))
    return _mm(name, (nb, k1 // tm), (a, dy), [a_spec, b_spec], jax.ShapeDtypeStruct((nb, k1, bn), BF16), o_spec,
               TN_DIMS)


def _mm_tn_plain(name, a, dy):
    s, k1 = a.shape
    n = dy.shape[1]
    tm = _pick(k1, (512, 256, 128))
    tn = _rows_for(n, s * 2, 8 * MIB)
    a_spec = pl.BlockSpec((s, tm), lambda i, j: (0, i))
    b_spec = pl.BlockSpec((s, tn), lambda i, j: (0, j))
    o_spec = pl.BlockSpec((tm, tn), lambda i, j: (i, j))
    return _mm(name, (k1 // tm, n // tn), (a, dy), [a_spec, b_spec], jax.ShapeDtypeStruct((k1, n), BF16), o_spec,
               TN_DIMS)


def _rstd(x):
    return lax.rsqrt(jnp.mean(x * x, axis=-1, keepdims=True) + EPS)


def _norm_bwd(dy, xhat, r, g):
    dxhat = dy * g
    return r * (dxhat - xhat * jnp.mean(dxhat * xhat, axis=-1, keepdims=True))


def _rmsnorm_fwd(name, x, g):
    s, d = x.shape
    tr = _pick(s, (256, 128))

    def body(x_ref, g_ref, h_ref):
        xv = x_ref[...]
        h_ref[...] = (xv * _rstd(xv) * g_ref[...]).astype(BF16)

    return pl.pallas_call(
        body, name=name, grid=(s // tr,),
        in_specs=[pl.BlockSpec((tr, d), lambda i: (i, 0)), pl.BlockSpec((1, d), lambda i: (0, 0))],
        out_specs=pl.BlockSpec((tr, d), lambda i: (i, 0)),
        out_shape=jax.ShapeDtypeStruct((s, d), BF16), compiler_params=_cparams(("parallel",)),
    )(x, g)


def _rmsnorm_bwd(name, dh, x, g, dres):
    s, d = x.shape
    tr = _pick(s, (256, 128))

    def body(dh_ref, x_ref, g_ref, dres_ref, dx_ref, dxb_ref, dg_ref):
        xv = x_ref[...]
        r = _rstd(xv)
        xhat = xv * r
        dhv = dh_ref[...]
        dx = dres_ref[...] + _norm_bwd(dhv, xhat, r, g_ref[...])
        dx_ref[...] = dx
        dxb_ref[...] = dx.astype(BF16)
        part = jnp.sum(dhv * xhat, axis=0, keepdims=True)

        @pl.when(pl.program_id(0) == 0)
        def _():
            dg_ref[...] = part

        @pl.when(pl.program_id(0) > 0)
        def _():
            dg_ref[...] += part

    row = pl.BlockSpec((tr, d), lambda i: (i, 0))
    vec = pl.BlockSpec((1, d), lambda i: (0, 0))
    return pl.pallas_call(
        body, name=name, grid=(s // tr,), in_specs=[row, row, vec, row], out_specs=(row, row, vec),
        out_shape=(jax.ShapeDtypeStruct((s, d), F32), jax.ShapeDtypeStruct((s, d), BF16),
                   jax.ShapeDtypeStruct((1, d), F32)),
        compiler_params=_cparams(("arbitrary",)),
    )(dh, x, g, dres)


def _loss_head(y, target):
    s, d = y.shape
    tr = _pick(s, (256, 128))

    def body(y_ref, t_ref, dy_ref, dyb_ref, loss_ref):
        err = y_ref[...] - t_ref[...]
        dy = err * (1.0 / d)
        dy_ref[...] = dy
        dyb_ref[...] = dy.astype(BF16)
        part = 0.5 * jnp.sum(jnp.mean(err * err, axis=-1, keepdims=True), axis=0, keepdims=True)
        part = jnp.broadcast_to(part, (1, 128))

        @pl.when(pl.program_id(0) == 0)
        def _():
            loss_ref[...] = part

        @pl.when(pl.program_id(0) > 0)
        def _():
            loss_ref[...] += part

    row = pl.BlockSpec((tr, d), lambda i: (i, 0))
    return pl.pallas_call(
        body, name="loss_head", grid=(s // tr,), in_specs=[row, row],
        out_specs=(row, row, pl.BlockSpec((1, 128), lambda i: (0, 0))),
        out_shape=(jax.ShapeDtypeStruct((s, d), F32), jax.ShapeDtypeStruct((s, d), BF16),
                   jax.ShapeDtypeStruct((1, 128), F32)),
        compiler_params=_cparams(("arbitrary",)),
    )(y, target)


def _split_dot(x, tri):
    hi = x.astype(BF16)
    lo = (x - hi.astype(F32)).astype(BF16)
    return (jnp.dot(hi, tri, preferred_element_type=F32) + jnp.dot(lo, tri, preferred_element_type=F32))


def _tile_iotas():
    row = lax.broadcasted_iota(jnp.int32, (TILE, TILE), 0)
    col = lax.broadcasted_iota(jnp.int32, (TILE, TILE), 1)
    return row, col


def _sb_logits(qi, kb, mask):
    z = lax.dot_general(qi, kb, NT_DIMS, preferred_element_type=F32) * (HEAD_DIM ** -0.5)
    sp = jnp.log1p(jnp.exp(-jnp.abs(z)))
    lb = jnp.minimum(z, 0.0) - sp
    l1m = -jnp.maximum(z, 0.0) - sp
    if mask is not None:
        l1m = jnp.where(mask, l1m, 0.0)
    return lb, l1m


def _attn_fwd(p, gq, gk, ga, n_heads):
    s = p.shape[0]
    nq = s // TILE

    hp = ATTN_HEADS_PER_STEP
    wd = hp * HEAD_DIM

    def body(q_ref, k_ref, v_ref, gq_ref, gk_ref, ga_ref, att_ref, o_ref, r_ref, qn_s, kn_s, vb_s):
        heads = [slice(hh * HEAD_DIM, (hh + 1) * HEAD_DIM) for hh in range(hp)]
        for hd in heads:
            qv = q_ref[:, hd]
            qn_s[:, hd] = (qv * _rstd(qv) * gq_ref[...]).astype(BF16)
            kv = k_ref[:, hd]
            kn_s[:, hd] = (kv * _rstd(kv) * gk_ref[...]).astype(BF16)
        vb_s[...] = v_ref[...].astype(BF16)
        row, col = _tile_iotas()
        causal = col < row
        upper_ones = jnp.concatenate([(row > col).astype(BF16), jnp.ones((TILE, TILE), BF16)], axis=1)

        def tiles(rows, key_blocks, states, mask):
            chains = [(hi, hd, keys) for hi, hd in enumerate(heads) for keys in key_blocks]
            logits = [_sb_logits(qn_s[rows, hd], kn_s[keys, hd], mask) for _, hd, keys in chains]
            sums = [_split_dot(l1m, upper_ones) for _, l1m in logits]
            carry = [c for _, c in states]
            probs = []
            for (hi, _, _), (lb, _), sm in zip(chains, logits, sums):
                a = jnp.exp(lb + sm[:, :TILE] + carry[hi])
                carry[hi] = carry[hi] + sm[:, TILE:]
                probs.append((a if mask is None else jnp.where(mask, a, 0.0)).astype(BF16))
            outs = [jnp.dot(a, vb_s[keys, hd], preferred_element_type=F32) for a, (_, hd, keys) in zip(probs, chains)]
            acc = [o_acc for o_acc, _ in states]
            for (hi, _, _), o in zip(chains, outs):
                acc[hi] = acc[hi] + o
            return tuple(zip(acc, carry))

        def key_block(b):
            return pl.ds(pl.multiple_of(b * TILE, TILE), TILE)

        def qblock(i, _):
            rows = pl.ds(pl.multiple_of(i * TILE, TILE), TILE)
            zero = jnp.zeros((TILE, HEAD_DIM), F32)
            states = tiles(rows, [rows], tuple((zero, zero) for _ in heads), causal)
            states = lax.cond(i % 2 == 1, lambda st: tiles(rows, [key_block(i - 1)], st, None), lambda st: st, states)
            top = i - i % 2

            def kblocks(jj, states):
                return tiles(rows, [key_block(top - 1 - 2 * jj), key_block(top - 2 - 2 * jj)], states, None)

            states = lax.fori_loop(0, i // 2, kblocks, states)
            for hh, (hd, (o_acc, c)) in enumerate(zip(heads, states)):
                o_ref[rows, hd] = o_acc
                r_ref[rows, hd] = c
                att_ref[rows, hd] = (o_acc * _rstd(o_acc) * ga_ref[hh]).astype(BF16)
            return 0

        lax.fori_loop(0, nq, qblock, 0)

    col_blk = lambda off: pl.BlockSpec((s, wd), lambda h: (0, off + h))
    vec = pl.BlockSpec((1, HEAD_DIM), lambda h: (0, 0))
    hvec = pl.BlockSpec((hp, 1, HEAD_DIM), lambda h: (h, 0, 0))
    out = pl.BlockSpec((s, wd), lambda h: (0, h))
    w = n_heads * HEAD_DIM
    steps = n_heads // hp
    return pl.pallas_call(
        body, name="attn_fwd", grid=(steps,),
        in_specs=[col_blk(0), col_blk(steps), col_blk(2 * steps), vec, vec, hvec],
        out_specs=(out, out, out),
        out_shape=(jax.ShapeDtypeStruct((s, w), BF16), jax.ShapeDtypeStruct((s, w), F32),
                   jax.ShapeDtypeStruct((s, w), F32)),
        scratch_shapes=[pltpu.VMEM((s, wd), BF16)] * 3,
        compiler_params=_cparams(("parallel",), ATTN_VMEM_MB),
    )(p, p, p, gq, gk, ga)


def _attn_bwd(p, gq, gk, ga, o, rsum, dmix, n_heads):
    s = p.shape[0]
    nq = s // TILE

    hp = ATTN_HEADS_PER_STEP
    wd = hp * HEAD_DIM
    scale = HEAD_DIM ** -0.5

    def body(q_ref, k_ref, v_ref, gq_ref, gk_ref, ga_ref, o_ref, r_ref, dm_ref,
             dq_ref, dk_ref, dv_ref, dgq_ref, dgk_ref, dga_ref,
             qn_s, kn_s, vb_s, do_s, dqn_s, dkn_s, dv_s):
        step = pl.program_id(0)
        gqv, gkv = gq_ref[...], gk_ref[...]
        heads = [slice(hh * HEAD_DIM, (hh + 1) * HEAD_DIM) for hh in range(hp)]
        for hh, hd in enumerate(heads):
            qv = q_ref[:, hd]
            qn_s[:, hd] = (qv * _rstd(qv) * gqv).astype(BF16)
            kv = k_ref[:, hd]
            kn_s[:, hd] = (kv * _rstd(kv) * gkv).astype(BF16)
            ov = o_ref[:, hd]
            ro = _rstd(ov)
            ohat = ov * ro
            dm = dm_ref[:, hd]
            dga_ref[hh] = jnp.sum(dm * ohat, axis=0, keepdims=True)
            do_s[:, hd] = _norm_bwd(dm, ohat, ro, ga_ref[hh]).astype(BF16)
        vb_s[...] = v_ref[...].astype(BF16)
        dkn_s[...] = jnp.zeros_like(dkn_s)
        dv_s[...] = jnp.zeros_like(dv_s)
        row, col = _tile_iotas()
        causal = col < row
        ones = jnp.ones((TILE, TILE), BF16)
        incl_ones = jnp.concatenate([(row <= col).astype(BF16), ones], axis=1)
        excl_ones = jnp.concatenate([(row < col).astype(BF16), ones], axis=1)

        def tiles(rows, key_blocks, states, mask):
            chains = [(hi, hd, keys) for hi, hd in enumerate(heads) for keys in key_blocks]
            qis = [qn_s[rows, hd] for hd in heads]
            dois = [do_s[rows, hd] for hd in heads]
            logits = [_sb_logits(qis[hi], kn_s[keys, hd], mask) for hi, hd, keys in chains]
            sums = [_split_dot(l1m, incl_ones) for _, l1m in logits]
            das = [lax.dot_general(dois[hi], vb_s[keys, hd], NT_DIMS, preferred_element_type=F32)
                   for hi, hd, keys in chains]
            pfx = [st[1] for st in states]
            probs, dss = [], []
            for (hi, hd, _), (lb, _), sm, da in zip(chains, logits, sums, das):
                a = jnp.exp(lb + (r_ref[rows, hd] - pfx[hi] - sm[:, :TILE]))
                pfx[hi] = pfx[hi] + sm[:, TILE:]
                a = a if mask is None else jnp.where(mask, a, 0.0)
                probs.append(a.astype(BF16))
                dss.append(da * a)
            dsums = [_split_dot(ds, excl_ones) for ds in dss]
            pc = [st[2] for st in states]
            dzs = []
            for (hi, _, _), (lb, _), ds, dsm in zip(chains, logits, dss, dsums):
                dl1m = pc[hi] + dsm[:, :TILE]
                pc[hi] = pc[hi] + dsm[:, TILE:]
                dl1m = dl1m if mask is None else jnp.where(mask, dl1m, 0.0)
                beta = jnp.exp(lb)
                dzs.append(((ds * (1.0 - beta) - dl1m * beta) * scale).astype(BF16))
            dqs = [jnp.dot(dz, kn_s[keys, hd], preferred_element_type=F32) for dz, (_, hd, keys) in zip(dzs, chains)]
            for dz, a, (hi, hd, keys) in zip(dzs, probs, chains):
                dkn_s[keys, hd] += lax.dot_general(dz, qis[hi], TN_DIMS, preferred_element_type=F32)
                dv_s[keys, hd] += lax.dot_general(a, dois[hi], TN_DIMS, preferred_element_type=F32)
            dq_acc = [st[0] for st in states]
            for (hi, _, _), dq in zip(chains, dqs):
                dq_acc[hi] = dq_acc[hi] + dq
            return tuple(zip(dq_acc, pfx, pc))

        def key_block(b):
            return pl.ds(pl.multiple_of(b * TILE, TILE), TILE)

        def qblock(i, _):
            rows = pl.ds(pl.multiple_of(i * TILE, TILE), TILE)
            zero = jnp.zeros((TILE, HEAD_DIM), F32)

            def kblocks(jj, states):
                return tiles(rows, [key_block(2 * jj), key_block(2 * jj + 1)], states, None)

            states = lax.fori_loop(0, i // 2, kblocks, tuple((zero, zero, zero) for _ in heads))
            states = lax.cond(i % 2 == 1, lambda st: tiles(rows, [key_block(i - 1)], st, None), lambda st: st, states)
            states = tiles(rows, [rows], states, causal)
            for hd, (dq_acc, _, _) in zip(heads, states):
                dqn_s[rows, hd] = dq_acc
            return 0

        lax.fori_loop(0, nq, qblock, 0)

        def norm_in_bwd(x_ref, g, dn_s, dx_ref, dg_ref):
            part = jnp.zeros((1, HEAD_DIM), F32)
            for hd in heads:
                xv = x_ref[:, hd]
                r = _rstd(xv)
                xhat = xv * r
                dn = dn_s[:, hd]
                dx_ref[:, hd] = _norm_bwd(dn, xhat, r, g).astype(BF16)
                part = part + jnp.sum(dn * xhat, axis=0, keepdims=True)

            @pl.when(step == 0)
            def _():
                dg_ref[...] = part

            @pl.when(step > 0)
            def _():
                dg_ref[...] += part

        norm_in_bwd(q_ref, gqv, dqn_s, dq_ref, dgq_ref)
        norm_in_bwd(k_ref, gkv, dkn_s, dk_ref, dgk_ref)
        dv_ref[...] = dv_s[...].astype(BF16)

    once = pl.Buffered(1)
    steps = n_heads // hp
    col_blk = lambda off: pl.BlockSpec((s, wd), lambda h: (0, off + h), pipeline_mode=once)
    vec = pl.BlockSpec((1, HEAD_DIM), lambda h: (0, 0))
    hvec = pl.BlockSpec((hp, 1, HEAD_DIM), lambda h: (h, 0, 0))
    blk = pl.BlockSpec((s, wd), lambda h: (0, h), pipeline_mode=once)
    w = n_heads * HEAD_DIM
    big = jax.ShapeDtypeStruct((s, w), BF16)
    return pl.pallas_call(
        body, name="attn_bwd", grid=(steps,),
        in_specs=[col_blk(0), col_blk(steps), col_blk(2 * steps), vec, vec, hvec, blk, blk, blk],
        out_specs=(blk, blk, blk, vec, vec, hvec),
        out_shape=(big, big, big, jax.ShapeDtypeStruct((1, HEAD_DIM), F32), jax.ShapeDtypeStruct((1, HEAD_DIM), F32),
                   jax.ShapeDtypeStruct((n_heads, 1, HEAD_DIM), F32)),
        scratch_shapes=[pltpu.VMEM((s, wd), BF16)] * 4 + [pltpu.VMEM((s, wd), F32)] * 3,
        compiler_params=_cparams(("arbitrary",), ATTN_VMEM_MB),
    )(p, p, p, gq, gk, ga, o, rsum, dmix)


_INV_SQRT2 = 0.7071067811865476
_INV_SQRT_2PI = 0.3989422804014327


def _gelu(x):
    return 0.5 * x * (1.0 + lax.erf(x * _INV_SQRT2))


def _gelu_grad(x):
    return 0.5 * (1.0 + lax.erf(x * _INV_SQRT2)) + x * (_INV_SQRT_2PI * jnp.exp(-0.5 * x * x))


def _sgu_fwd(p, gs, w_s, b_s, gb, col0, after):
    s = p.shape[0]
    n_chunks = s // TILE
    per_trip = _pick(n_chunks, (4, 2, 1))

    def body(u_ref, v_ref, gs_ref, w_ref, b_ref, gb_ref, _, out_ref, vs_s):
        vg = _gelu(v_ref[...])
        vs_s[...] = (vg * _rstd(vg) * gs_ref[...]).astype(BF16)
        row, col = _tile_iotas()
        wt = jnp.where(col <= row, w_ref[...], 0.0).astype(BF16)
        bcol = b_ref[...]
        gbv = gb_ref[...]

        def chunks(c, _):
            rows = [pl.ds(pl.multiple_of((c * per_trip + k) * TILE, TILE), TILE) for k in range(per_trip)]
            mixed = [jnp.dot(wt, vs_s[r, :], preferred_element_type=F32) + bcol for r in rows]
            sgs = [_gelu(u_ref[r, :]) * mx for r, mx in zip(rows, mixed)]
            for r, sg in zip(rows, sgs):
                out_ref[r, :] = (sg * _rstd(sg) * gbv).astype(BF16)
            return 0

        lax.fori_loop(0, n_chunks // per_trip, chunks, 0)

    col_blk = lambda off: pl.BlockSpec((s, HEAD_DIM), lambda g: (0, off + g))
    gvec = pl.BlockSpec((None, 1, HEAD_DIM), lambda g: (g, 0, 0))
    return pl.pallas_call(
        body, name="sgu_fwd", grid=(N_GROUPS,),
        in_specs=[col_blk(col0), col_blk(col0 + N_GROUPS), gvec,
                  pl.BlockSpec((None, TILE, TILE), lambda g: (g, 0, 0)),
                  pl.BlockSpec((None, TILE, 1), lambda g: (g, 0, 0)), gvec,
                  pl.BlockSpec(after.shape, lambda g: (0, 0))],
        out_specs=pl.BlockSpec((s, HEAD_DIM), lambda g: (0, g)),
        out_shape=jax.ShapeDtypeStruct((s, N_GROUPS * HEAD_DIM), BF16),
        scratch_shapes=[pltpu.VMEM((s, HEAD_DIM), BF16)],
        compiler_params=_cparams(("parallel",)),
    )(p, p, gs, w_s, b_s, gb, after)


def _sgu_bwd(p, gs, w_s, b_s, gb, dmix, col0, dm_col0):
    s = p.shape[0]
    n_chunks = s // TILE
    per_trip = _pick(n_chunks, (4, 2, 1))

    def body(u_ref, v_ref, gs_ref, w_ref, b_ref, gb_ref, dm_ref,
             du_ref, dv_ref, dgs_ref, dw_ref, db_ref, dgb_ref, vs_s, dvs_s):
        gsv = gs_ref[...]
        gbv = gb_ref[...]
        vg = _gelu(v_ref[...])
        vs_s[...] = (vg * _rstd(vg) * gsv).astype(BF16)
        row, col = _tile_iotas()
        causal = col <= row
        wt = jnp.where(causal, w_ref[...], 0.0).astype(BF16)
        bcol = b_ref[...]

        def chunks(c, carry):
            dw_acc, db_acc, dgb_acc = carry
            rows = [pl.ds(pl.multiple_of((c * per_trip + k) * TILE, TILE), TILE) for k in range(per_trip)]
            vss = [vs_s[r, :] for r in rows]
            mixed = [jnp.dot(wt, vs, preferred_element_type=F32) + bcol for vs in vss]
            dmbs = []
            for r, mx in zip(rows, mixed):
                u_pre = u_ref[r, :]
                u = _gelu(u_pre)
                sg = u * mx
                rs = _rstd(sg)
                sghat = sg * rs
                dm = dm_ref[r, :]
                dsg = _norm_bwd(dm, sghat, rs, gbv)
                dgb_acc = dgb_acc + jnp.sum(dm * sghat, axis=0, keepdims=True)
                du_ref[r, :] = (dsg * mx * _gelu_grad(u_pre)).astype(BF16)
                dmixed = dsg * u
                db_acc = db_acc + jnp.sum(dmixed, axis=1, keepdims=True)
                dmbs.append(dmixed.astype(BF16))
            for dmb, vs in zip(dmbs, vss):
                dw_acc = dw_acc + lax.dot_general(dmb, vs, NT_DIMS, preferred_element_type=F32)
            for r, dmb in zip(rows, dmbs):
                dvs_s[r, :] = lax.dot_general(wt, dmb, TN_DIMS, preferred_element_type=F32)
            return dw_acc, db_acc, dgb_acc

        dw_acc, db_acc, dgb_acc = lax.fori_loop(
            0, n_chunks // per_trip, chunks,
            (jnp.zeros((TILE, TILE), F32), jnp.zeros((TILE, 1), F32), jnp.zeros((1, HEAD_DIM), F32)))
        dw_ref[...] = jnp.where(causal, dw_acc, 0.0)
        db_ref[...] = db_acc
        dgb_ref[...] = dgb_acc
        v_pre = v_ref[...]
        vg = _gelu(v_pre)
        rv = _rstd(vg)
        vhat = vg * rv
        dvs = dvs_s[...]
        dgs_ref[...] = jnp.sum(dvs * vhat, axis=0, keepdims=True)
        dv_ref[...] = (_norm_bwd(dvs, vhat, rv, gsv) * _gelu_grad(v_pre)).astype(BF16)

    col_blk = lambda off: pl.BlockSpec((s, HEAD_DIM), lambda g: (0, off + g))
    gvec = pl.BlockSpec((None, 1, HEAD_DIM), lambda g: (g, 0, 0))
    wspec = pl.BlockSpec((None, TILE, TILE), lambda g: (g, 0, 0))
    bspec = pl.BlockSpec((None, TILE, 1), lambda g: (g, 0, 0))
    blk = pl.BlockSpec((s, HEAD_DIM), lambda g: (0, g))
    big = jax.ShapeDtypeStruct((s, N_GROUPS * HEAD_DIM), BF16)
    gshape = jax.ShapeDtypeStruct((N_GROUPS, 1, HEAD_DIM), F32)
    return pl.pallas_call(
        body, name="sgu_bwd", grid=(N_GROUPS,),
        in_specs=[col_blk(col0), col_blk(col0 + N_GROUPS), gvec, wspec, bspec, gvec, col_blk(dm_col0)],
        out_specs=(blk, blk, gvec, wspec, bspec, gvec),
        out_shape=(big, big, gshape, jax.ShapeDtypeStruct((N_GROUPS, TILE, TILE), F32),
                   jax.ShapeDtypeStruct((N_GROUPS, TILE, 1), F32), gshape),
        scratch_shapes=[pltpu.VMEM((s, HEAD_DIM), BF16), pltpu.VMEM((s, HEAD_DIM), F32)],
        compiler_params=_cparams(("parallel",)),
    )(p, p, gs, w_s, b_s, gb, dmix)


SUBLANES = 8


def _shift_down(x, n):
    rolled = pltpu.roll(x, n, 0)
    edge = lax.broadcasted_iota(jnp.int32, (SUBLANES, x.shape[1]), 0)
    return jnp.concatenate([jnp.where(edge >= n, rolled[:SUBLANES], 0.0), rolled[SUBLANES:]], axis=0)


def _shift_up(x, n):
    s = x.shape[0]
    rolled = pltpu.roll(x, s - n, 0)
    edge = lax.broadcasted_iota(jnp.int32, (SUBLANES, x.shape[1]), 0)
    return jnp.concatenate([rolled[:s - SUBLANES], jnp.where(edge < SUBLANES - n, rolled[s - SUBLANES:], 0.0)], axis=0)


def _conv(x, w, b):
    x1, x2 = _shift_down(x, 1), _shift_down(x, 2)
    return b + w[0:1, :] * x2 + w[1:2, :] * x1 + w[2:3, :] * x, x1, x2


def _conv_specs(s, tn):
    xspec = pl.BlockSpec((2, s, tn), lambda j: (0, 0, j))
    wspec = pl.BlockSpec((2, CONV_WIDTH, tn), lambda j: (0, 0, j))
    bspec = pl.BlockSpec((2, 1, tn), lambda j: (0, 0, j))
    return xspec, wspec, bspec


def _conv_gate_fwd(up, cw, cb):
    _, s, f = up.shape
    tn = _pick(f, (256, 128))

    def body(x_ref, w_ref, b_ref, act_ref):
        gate = _conv(x_ref[0], w_ref[0], b_ref[0])[0]
        val = _conv(x_ref[1], w_ref[1], b_ref[1])[0]
        act_ref[...] = (gate * jax.nn.sigmoid(gate) * val).astype(BF16)

    xspec, wspec, bspec = _conv_specs(s, tn)
    return pl.pallas_call(
        body, name="conv_gate_fwd", grid=(f // tn,), in_specs=[xspec, wspec, bspec],
        out_specs=pl.BlockSpec((s, tn), lambda j: (0, j)), out_shape=jax.ShapeDtypeStruct((s, f), BF16),
        compiler_params=_cparams(("parallel",)),
    )(up, cw, cb)


def _down_dx_conv_gate_bwd(up, cw, cb, dy, wdown, after):
    _, s, f = up.shape
    d = dy.shape[1]
    tn = _pick(f, (256, 128))

    def body(x_ref, w_ref, b_ref, dy_ref, wd_ref, _, dx_ref, dw_ref, db_ref):
        da = lax.dot_general(dy_ref[...], wd_ref[...], NT_DIMS, preferred_element_type=F32)
        xg, xv = x_ref[0], x_ref[1]
        wg, wv = w_ref[0], w_ref[1]
        gate, xg1, xg2 = _conv(xg, wg, b_ref[0])
        val, xv1, xv2 = _conv(xv, wv, b_ref[1])
        sig = jax.nn.sigmoid(gate)
        dval = da * (gate * sig)
        dgate = da * val * (sig * (1.0 + gate * (1.0 - sig)))
        for half, (x, x1, x2, w, dz) in enumerate(((xg, xg1, xg2, wg, dgate), (xv, xv1, xv2, wv, dval))):
            dx_ref[half] = (w[2:3, :] * dz + w[1:2, :] * _shift_up(dz, 1) + w[0:1, :] * _shift_up(dz, 2)).astype(BF16)
            dw_ref[half, 0:1, :] = jnp.sum(dz * x2, axis=0, keepdims=True)
            dw_ref[half, 1:2, :] = jnp.sum(dz * x1, axis=0, keepdims=True)
            dw_ref[half, 2:3, :] = jnp.sum(dz * x, axis=0, keepdims=True)
            db_ref[half] = jnp.sum(dz, axis=0, keepdims=True)

    xspec, wspec, bspec = _conv_specs(s, tn)
    return pl.pallas_call(
        body, name="down_dx_conv_gate_bwd", grid=(f // tn,),
        in_specs=[xspec, wspec, bspec, pl.BlockSpec((s, d), lambda j: (0, 0)), pl.BlockSpec((tn, d), lambda j: (j, 0)),
                  pl.BlockSpec(after.shape, lambda j: (0, 0))],
        out_specs=(xspec, wspec, bspec),
        out_shape=(jax.ShapeDtypeStruct((2, s, f), BF16), jax.ShapeDtypeStruct((2, CONV_WIDTH, f), F32),
                   jax.ShapeDtypeStruct((2, 1, f), F32)),
        compiler_params=_cparams(("parallel",), 56),
    )(up, cw, cb, dy, wdown, after)


def _mesh_pos():
    return lax.axis_index("x"), lax.axis_index("y"), lax.axis_index("c")


def _remote(src, dst, send_sem, recv_sem, to):
    return pltpu.make_async_remote_copy(src_ref=src, dst_ref=dst, send_sem=send_sem, recv_sem=recv_sem,
                                        device_id=to, device_id_type=pl.DeviceIdType.MESH)


HBM_SPEC = pl.BlockSpec(memory_space=pltpu.HBM)
SEM_SPEC = pl.BlockSpec(memory_space=pltpu.SEMAPHORE)
ANY_SPEC = pl.BlockSpec(memory_space=pl.ANY)
TOKEN_SPEC = pl.BlockSpec(memory_space=pltpu.VMEM)
TOKEN_SHAPE = jax.ShapeDtypeStruct((8, 128), F32)
DATAFLOW = pltpu.SideEffectType.DATAFLOW_SIDE_EFFECTING
GATHER_PLANE = (2, 4, 6)


def _slot(pos):
    return 4 * pos[0] + 2 * pos[1] + pos[2]


def _flip(pos, k):
    return (pos[0] ^ ((k >> 2) & 1), pos[1] ^ ((k >> 1) & 1), pos[2] ^ (k & 1))


def _hbm(a):
    return pltpu.with_memory_space_constraint(a, pltpu.HBM)


def _hbm_shapes(arrays):
    return tuple(pltpu.HBM(a.shape, a.dtype) for a in arrays)


class _Split:
    def __init__(self, n, outs, has_sems):
        k = 2 * n if has_sems else 0
        self.n = n
        self.sems = list(outs[:k])
        self.bufs = list(outs[k:k + 2 * n])
        self.token = outs[-1]


def _split_call(name, body, bufs, sems_in, makes_sems, after):
    n = len(bufs) // 2
    k = 2 * n if makes_sems else 0
    m = len(sems_in)
    afters = list(after) if isinstance(after, (list, tuple)) else [after]
    na = len(afters)

    def wrapped(*refs):
        srcs, dsts = refs[:n], refs[n:2 * n]
        s_in = refs[2 * n:2 * n + m]
        s_out = refs[2 * n + m + na:2 * n + m + na + k]
        token, local_sems = refs[-2], refs[-1]
        body(srcs, dsts, s_in, s_out, local_sems)
        token[...] = jnp.zeros_like(token)

    outs = pl.pallas_call(
        wrapped, name=name,
        out_shape=(pltpu.SemaphoreType.DMA(()),) * k + _hbm_shapes(bufs) + (TOKEN_SHAPE,),
        in_specs=[HBM_SPEC] * (2 * n) + [SEM_SPEC] * m + [ANY_SPEC] * na,
        out_specs=(SEM_SPEC,) * k + (HBM_SPEC,) * (2 * n) + (TOKEN_SPEC,),
        input_output_aliases={i: k + i for i in range(2 * n)},
        scratch_shapes=[pltpu.SemaphoreType.DMA((n,))],
        compiler_params=pltpu.CompilerParams(has_side_effects=DATAFLOW),
    )(*[_hbm(b) for b in bufs], *sems_in, *afters)
    return _Split(n, outs, makes_sems)


def _wait_slots(land, count, send_sem, recv_sem, me, send=False, recv=False):
    span = land.at[pl.ds(0, count)]
    cp = _remote(span, span, send_sem, recv_sem, me)
    if send:
        cp.wait_send()
    if recv:
        cp.wait_recv()


def _gather_start(name, shards, after):
    n = len(shards)
    my_slot = _slot(_mesh_pos())
    lands = [lax.dynamic_update_slice(lax.empty((N_DEV,) + w.shape, w.dtype), w[None], (my_slot, 0, 0)) for w in shards]

    def body(srcs, dsts, _, sems, local_sems):
        me = _mesh_pos()
        for a in range(n):
            for k in (1,) + GATHER_PLANE:
                _remote(srcs[a], dsts[a].at[_slot(me)], sems[a], sems[n + a], _flip(me, k)).start()

    return _split_call(name, body, list(shards) + lands, [], True, after)


def _gather_forward(name, started, after):
    n = started.n

    def body(srcs, dsts, sems_a, sems_b, local_sems):
        me = _mesh_pos()
        sibling = _flip(me, 1)
        for a in range(n):
            _wait_slots(dsts[a], 4, sems_a[a], sems_a[n + a], me, recv=True)
            for k in GATHER_PLANE:
                block = dsts[a].at[_slot(_flip(me, k))]
                _remote(block, block, sems_b[a], sems_b[n + a], sibling).start()
        for a in range(n):
            _wait_slots(dsts[a], 4, sems_a[a], sems_a[n + a], me, send=True)

    return _split_call(name, body, started.bufs, started.sems, True, after)


def _gather_finish(name, forwarded, after):
    n = forwarded.n

    def body(srcs, dsts, sems_b, _, local_sems):
        me = _mesh_pos()
        for a in range(n):
            _wait_slots(dsts[a], 3, sems_b[a], sems_b[n + a], me, send=True, recv=True)

    return _split_call(name, body, forwarded.bufs, forwarded.sems, False, after).bufs[n:]


def _exchange_start(name, blocked, after):
    n = len(blocked)
    my_slot = _slot(_mesh_pos())
    rows = [w.shape[-2] // (N_DEV if w.ndim == 2 else 1) for w in blocked]

    def block(ref, a, slot):
        if len(ref.shape) == 3:
            return ref.at[slot]
        return ref.at[pl.ds(pl.multiple_of(slot * rows[a], 16), rows[a])]

    lands = []
    for w, r in zip(blocked, rows):
        mine = lax.dynamic_slice_in_dim(w, my_slot, 1, 0) if w.ndim == 3 else lax.dynamic_slice_in_dim(w, my_slot * r, r, 0)[None]
        lands.append(lax.dynamic_update_slice(lax.empty((N_DEV, r, w.shape[-1]), w.dtype), mine, (my_slot, 0, 0)))

    def body(srcs, dsts, _, sems, local_sems):
        me = _mesh_pos()
        for a in range(n):
            for k in range(1, N_DEV):
                peer = _flip(me, k)
                _remote(block(srcs[a], a, _slot(peer)), dsts[a].at[_slot(me)], sems[a], sems[n + a], peer).start()

    return _split_call(name, body, list(blocked) + lands, [], True, after)


def _exchange_finish(name, started, after):
    n = started.n

    def body(srcs, dsts, sems, _, local_sems):
        me = _mesh_pos()
        for a in range(n):
            _wait_slots(dsts[a], N_DEV - 1, sems[a], sems[n + a], me, send=True, recv=True)

    return _split_call(name, body, started.bufs, started.sems, False, after).bufs[n:]


def _broadcast_start(name, arrays, after):
    n = len(arrays)
    my_slot = _slot(_mesh_pos())
    lands = [lax.dynamic_update_slice(lax.empty((N_DEV,) + w.shape, w.dtype), w[None], (my_slot, 0, 0)) for w in arrays]

    def body(srcs, dsts, _, sems, local_sems):
        me = _mesh_pos()
        for a in range(n):
            for k in range(1, N_DEV):
                _remote(srcs[a], dsts[a].at[_slot(me)], sems[a], sems[n + a], _flip(me, k)).start()

    return _split_call(name, body, list(arrays) + lands, [], True, after)


def _adamw_math(w, g, m, v):
    m = ADAM_B1 * m + (1.0 - ADAM_B1) * g
    v = ADAM_B2 * v + (1.0 - ADAM_B2) * (g * g)
    m_hat = m / (1.0 - ADAM_B1 ** ADAM_STEP)
    v_hat = v / (1.0 - ADAM_B2 ** ADAM_STEP)
    delta = -ADAM_LR * (m_hat / (jnp.sqrt(v_hat) + ADAM_EPS) + ADAM_WD * w)
    return delta, m, v


def _adamw(name, w, m, v, parts, layer, prev=None):
    _, r, c = w.shape
    tr = _pick(r, tuple(t for t in (256, 128, 64, 32, 16) if t * c <= ADAMW_TILE_ELEMS))
    n_prev = 0 if prev is None else 4

    def body(*refs):
        w_ref, m_ref, v_ref, p_ref = refs[:4]
        g_ref, d_ref, nm_ref, nv_ref = refs[4 + n_prev:]
        g = p_ref[0].astype(F32)
        for src in range(1, N_DEV):
            g = g + p_ref[src].astype(F32)
        delta, nm, nv = _adamw_math(w_ref[...], g, m_ref[...], v_ref[...])
        g_ref[...] = g
        d_ref[...] = delta
        nm_ref[...] = nm
        nv_ref[...] = nv

    wspec = pl.BlockSpec((None, tr, c), lambda i: (layer, i, 0))
    pspec = pl.BlockSpec((N_DEV, tr, c), lambda i: (0, i, 0))
    shp = jax.ShapeDtypeStruct(w.shape, F32)
    return pl.pallas_call(
        body, name=name, grid=(r // tr,), in_specs=[wspec] * 3 + [pspec] + [ANY_SPEC] * n_prev,
        out_specs=(wspec,) * 4, out_shape=(shp,) * 4, input_output_aliases={4 + j: j for j in range(n_prev)},
        compiler_params=_cparams(("parallel",)),
    )(w, m, v, parts, *([] if prev is None else prev))


PACK_TILE = 8 * 128


def _pack(arrays):
    flat = []
    for a in arrays:
        v = a.reshape(-1)
        pad = (-v.shape[0]) % PACK_TILE
        flat.append(jnp.pad(v, (0, pad)) if pad else v)
    return jnp.concatenate(flat).reshape(-1, 128)


def _unpack(buf, like):
    flat = buf.reshape(-1)
    out, off = [], 0
    for a in like:
        n = 1
        for dim in a.shape:
            n *= dim
        out.append(flat[off:off + n].reshape(a.shape))
        off += n + (-n) % PACK_TILE
    return out


def _sum_slots(gathered):
    _, r, c = gathered.shape

    def body(x_ref, o_ref):
        acc = x_ref[0]
        for src in range(1, N_DEV):
            acc = acc + x_ref[src]
        o_ref[...] = acc

    return pl.pallas_call(body, name="small_grad_sum", out_shape=jax.ShapeDtypeStruct((r, c), F32))(gathered)


def _adamw_small(w, g, m, v):
    shp = jax.ShapeDtypeStruct(w.shape, F32)

    def body(w_ref, g_ref, m_ref, v_ref, d_ref, nm_ref, nv_ref):
        delta, nm, nv = _adamw_math(w_ref[...], g_ref[...], m_ref[...], v_ref[...])
        d_ref[...] = delta
        nm_ref[...] = nm
        nv_ref[...] = nv

    return pl.pallas_call(body, name="adamw_small", out_shape=(shp,) * 3)(w, g, m, v)


def kernel(x, attn_norm_g, w_in, q_norm_g, k_norm_g, sgu_norm_g, sgu_w, sgu_b, out_norm_a_g, out_norm_b_g, w_out, ffn_norm_g, w_up, conv_w, conv_b, w_down, loss_target, m_attn_norm_g, m_w_in, m_q_norm_g, m_k_norm_g, m_sgu_norm_g, m_sgu_w, m_sgu_b, m_out_norm_a_g, m_out_norm_b_g, m_w_out, m_ffn_norm_g, m_w_up, m_conv_w, m_conv_b, m_w_down, v_attn_norm_g, v_w_in, v_q_norm_g, v_k_norm_g, v_sgu_norm_g, v_sgu_w, v_sgu_b, v_out_norm_a_g, v_out_norm_b_g, v_w_out, v_ffn_norm_g, v_w_up, v_conv_w, v_conv_b, v_w_down):
    depth = w_in.shape[0]
    s, d = x.shape[1], x.shape[2]
    n_heads = (d // 2) // HEAD_DIM
    sgu_col0 = 3 * n_heads
    f2 = w_up.shape[2] * N_DEV
    ff = f2 // 2
    my_slot = 4 * lax.axis_index("x") + 2 * lax.axis_index("y") + lax.axis_index("c")

    wb = [(w_in[l].astype(BF16), w_out[l].astype(BF16), w_up[l].astype(BF16), w_down[l].astype(BF16))
          for l in range(depth)]
    groups = {"in0": [wb[0][0]], "out0": [wb[0][1], conv_w.reshape(depth * CONV_WIDTH, -1)], "up0": [wb[0][2]],
              "down0": [wb[0][3]]}
    for l in range(1, depth):
        groups[f"in{l}"] = [wb[l][0], wb[l][1]]
        groups[f"ffn{l}"] = [wb[l][2], wb[l][3]]
    token = attn_norm_g
    started = {}
    for gname, group in groups.items():
        started[gname] = _gather_start(f"gather_{gname}_start", group, token)
        token = started[gname].token

    def forward(gname, after):
        return _gather_forward(f"gather_{gname}_forward", started[gname], after)

    def finish(gname, forwarded, after):
        return _gather_finish(f"gather_{gname}_finish", forwarded, after)

    conv_b_all = conv_b.reshape(depth, 2, 1, ff)
    sgu_b_col = sgu_b[..., None]
    fwd_in = forward("in0", token)
    win_g = finish("in0", fwd_in, fwd_in.token)[0]

    xs = x[0]
    saved = []
    gathered = []
    for l in range(depth):
        g1 = attn_norm_g[l][None]
        g2 = ffn_norm_g[l][None]
        gq, gk = q_norm_g[l][None], k_norm_g[l][None]
        ga = out_norm_a_g[l][:, None, :]
        gs = sgu_norm_g[l][:, None, :]
        gb = out_norm_b_g[l][:, None, :]
        h1 = _rmsnorm_fwd("attn_norm_fwd", xs, g1)
        p = _mm_nn_blocked("in_proj", h1, win_g, F32)
        att, o, rsum = _attn_fwd(p, gq, gk, ga, n_heads)
        if l == 0:
            fwd_out = forward("out0", att)
            fwd_up = forward("up0", fwd_out.token)
        else:
            fwd_up = forward(f"ffn{l}", att)
        sg = _sgu_fwd(p, gs, sgu_w[l], sgu_b_col[l], gb, sgu_col0, fwd_up.token)
        mix = jnp.concatenate([att, sg], axis=-1)
        if l == 0:
            wout_g, cw = finish("out0", fwd_out, mix)
            cw = jnp.transpose(cw.reshape(N_DEV, depth, CONV_WIDTH, -1), (1, 2, 0, 3)).reshape(depth, CONV_WIDTH, 2, ff)
            conv_w_all = jnp.transpose(cw, (0, 2, 1, 3))
        x1 = _mm_nn_res("out_proj", mix, wout_g.reshape(d, d), xs)
        h2 = _rmsnorm_fwd("ffn_norm_fwd", x1, g2)
        if l == 0:
            wup_g = finish("up0", fwd_up, h2)[0]
            fwd_down = forward("down0", wup_g)
            up = _mm_nn_blocked("up_proj", h2, wup_g, F32, halves=True, after=fwd_down.token)
            wdown_g = finish("down0", fwd_down, up)[0]
        else:
            wup_g, wdown_g = finish(f"ffn{l}", fwd_up, h2)
            up = _mm_nn_blocked("up_proj", h2, wup_g, F32, halves=True)
        act = _conv_gate_fwd(up, conv_w_all[l], conv_b_all[l])
        saved.append((xs, h1, p, o, rsum, mix, x1, h2, up, act))
        gathered.append((win_g, wout_g, wup_g, wdown_g))
        if l + 1 < depth:
            fwd_in = forward(f"in{l + 1}", act)
            x2 = _mm_nn_res("down_proj", act, wdown_g.reshape(ff, d), x1, after=fwd_in.token)
            win_g, wout_g = finish(f"in{l + 1}", fwd_in, x2)
        else:
            x2 = _mm_nn_res("down_proj", act, wdown_g.reshape(ff, d), x1)
        xs = x2

    dx, dxb, loss_vec = _loss_head(xs, loss_target[0])
    loss = lax.psum(loss_vec[0, 0], MESH_AXES)

    exchanges = []
    small = [None] * depth
    for l in reversed(range(depth)):
        xs0, h1, p, o, rsum, mix, x1, h2, up, act = saved[l]
        win_g, wout_g, wup_g, wdown_g = gathered[l]
        wout_full = wout_g.reshape(d, d)
        wdown_full = wdown_g.reshape(ff, d)
        g1 = attn_norm_g[l][None]
        g2 = ffn_norm_g[l][None]
        gq, gk = q_norm_g[l][None], k_norm_g[l][None]
        ga = out_norm_a_g[l][:, None, :]
        gs = sgu_norm_g[l][:, None, :]
        gb = out_norm_b_g[l][:, None, :]
        d_wdown = _mm_tn_plain("down_proj_dw", act, dxb)
        exchanges.append((l, "down", ("w_down",), _exchange_start(f"grad_down{l}_start", [d_wdown], dx)))
        dup, d_cw, d_cb = _down_dx_conv_gate_bwd(up, conv_w_all[l], conv_b_all[l], dxb, wdown_full,
                                                 exchanges[-1][3].token)
        d_wup = _mm_tn_blocked("up_proj_dw", h2, dup, N_DEV, halves=True)
        exchanges.append((l, "up", ("w_up",), _exchange_start(f"grad_up{l}_start", [d_wup], d_cb)))
        dh2 = _mm_nt_blocked("up_proj_dx", dup, wup_g, halves=True, after=exchanges[-1][3].token)
        dx, dxb, d_g2 = _rmsnorm_bwd("ffn_norm_bwd", dh2, x1, g2, dx)
        d_wout = _mm_tn_plain("out_proj_dw", mix, dxb)
        dmix = _mm_nt_plain("out_proj_dx", dxb, wout_full)
        dq, dk, dv, d_gq, d_gk, d_ga = _attn_bwd(p, gq, gk, ga, o, rsum, dmix, n_heads)
        du, dvs, d_gs, d_sw, d_sb, d_gb = _sgu_bwd(p, gs, sgu_w[l], sgu_b_col[l], gb, dmix, sgu_col0, n_heads)
        dp = jnp.concatenate([dq, dk, dv, du, dvs], axis=-1)
        d_win = _mm_tn_blocked("in_proj_dw", h1, dp, N_DEV)
        exchanges.append((l, "mix", ("w_in", "w_out"), _exchange_start(f"grad_mix{l}_start", [d_win, d_wout], d_gq)))
        dh1 = _mm_nt_blocked("in_proj_dx", dp, win_g, after=exchanges[-1][3].token)
        dx, dxb, d_g1 = _rmsnorm_bwd("attn_norm_bwd", dh1, xs0, g1, dx)
        small[l] = dict(attn_norm_g=d_g1[0], q_norm_g=d_gq[0], k_norm_g=d_gk[0], sgu_norm_g=d_gs[:, 0], sgu_w=d_sw,
                        sgu_b=d_sb[..., 0], out_norm_a_g=d_ga[:, 0], out_norm_b_g=d_gb[:, 0], ffn_norm_g=d_g2[0],
                        conv_w=jnp.transpose(d_cw, (1, 0, 2)).reshape(CONV_WIDTH, f2), conv_b=d_cb.reshape(f2))
    grad_x = dx[None]

    small_names = ["attn_norm_g", "q_norm_g", "k_norm_g", "sgu_norm_g", "sgu_w", "sgu_b", "out_norm_a_g",
                   "out_norm_b_g", "ffn_norm_g", "conv_b", "conv_w"]
    small_g = [jnp.stack([small[l][n] for l in range(depth)]) for n in small_names]
    small_sent = _broadcast_start("grad_small_start", [_pack(small_g)], dx)

    res = {}
    big = dict(w_in=(w_in, m_w_in, v_w_in), w_out=(w_out, m_w_out, v_w_out), w_up=(w_up, m_w_up, v_w_up),
               w_down=(w_down, m_w_down, v_w_down))
    after = [small_sent.token]
    for l, stage, names, ex in exchanges:
        landed = _exchange_finish(f"grad_{stage}{l}_finish", ex, after)
        after = []
        for name, parts in zip(names, landed):
            w, m, v = big[name]
            res[name] = _adamw(f"adamw_{name}", w, m, v, parts, l, res.get(name))
            after.append(res[name][0])
    small_all = _exchange_finish("grad_small_finish", small_sent, after)[0]
    small_sum = _unpack(_sum_slots(small_all), small_g)
    g_small = dict(zip(small_names, small_sum))
    cwn = conv_w.shape[2]
    g_small["conv_w"] = lax.dynamic_slice_in_dim(g_small["conv_w"], my_slot * cwn, cwn, axis=2)
    small_w = dict(attn_norm_g=(attn_norm_g, m_attn_norm_g, v_attn_norm_g), q_norm_g=(q_norm_g, m_q_norm_g, v_q_norm_g),
                   k_norm_g=(k_norm_g, m_k_norm_g, v_k_norm_g), sgu_norm_g=(sgu_norm_g, m_sgu_norm_g, v_sgu_norm_g),
                   sgu_w=(sgu_w, m_sgu_w, v_sgu_w), sgu_b=(sgu_b, m_sgu_b, v_sgu_b),
                   out_norm_a_g=(out_norm_a_g, m_out_norm_a_g, v_out_norm_a_g),
                   out_norm_b_g=(out_norm_b_g, m_out_norm_b_g, v_out_norm_b_g),
                   ffn_norm_g=(ffn_norm_g, m_ffn_norm_g, v_ffn_norm_g), conv_b=(conv_b, m_conv_b, v_conv_b),
                   conv_w=(conv_w, m_conv_w, v_conv_w))
    like = [small_w[n][0] for n in small_names]
    pw = _pack([small_w[n][0] for n in small_names])
    pm = _pack([small_w[n][1] for n in small_names])
    pv = _pack([small_w[n][2] for n in small_names])
    pg = _pack([g_small[n].reshape(small_w[n][0].shape) for n in small_names])
    pd, pnm, pnv = _adamw_small(pw, pg, pm, pv)
    for n, dlt, nm, nv in zip(small_names, _unpack(pd, like), _unpack(pnm, like), _unpack(pnv, like)):
        res[n] = (g_small[n].reshape(small_w[n][0].shape), dlt, nm, nv)

    order = ["attn_norm_g", "w_in", "q_norm_g", "k_norm_g", "sgu_norm_g", "sgu_w", "sgu_b", "out_norm_a_g",
             "out_norm_b_g", "w_out", "ffn_norm_g", "w_up", "conv_w", "conv_b", "w_down"]
    outs = [loss, grad_x]
    for field in range(4):
        outs += [res[n][field] for n in order]
    return tuple(outs)
```

```python
import functools

import jax
import jax.numpy as jnp
from jax import lax
from jax.experimental import pallas as pl
from jax.experimental.pallas import tpu as pltpu

F32 = jnp.float32
BF16 = jnp.bfloat16
EPS = 1e-6
HEAD_DIM = 128
TILE = 128
ATTN_VMEM_MB = 58
ATTN_HEADS_PER_STEP = 4
N_GROUPS = 8
CONV_WIDTH = 3
N_DEV = 8
MESH_AXES = ("x", "y", "c")
MIB = 1024 * 1024

ADAM_LR = 0.001
ADAM_B1 = 0.9
ADAM_B2 = 0.999
ADAM_EPS = 1e-08
ADAM_WD = 0.01
ADAM_STEP = 10
ADAMW_TILE_ELEMS = 160 * 1024

NT_DIMS = (((1,), (1,)), ((), ()))
NN_DIMS = (((1,), (0,)), ((), ()))
TN_DIMS = (((0,), (0,)), ((), ()))


def _cparams(sem, vmem_mb=48):
    return pltpu.CompilerParams(dimension_semantics=sem, vmem_limit_bytes=vmem_mb * MIB)


def _pick(n, cands):
    for c in cands:
        if n % c == 0:
            return c
    return n


def _mm(name, grid, ins, in_specs, out_shape, out_spec, dims, has_res=False, parts=None, vmem_mb=56, after=None):
    n_in = 2 + has_res + (after is not None)
    if after is not None:
        ins = tuple(ins) + (after,)
        in_specs = list(in_specs) + [pl.BlockSpec(after.shape, lambda *_: (0, 0))]

    def body(*refs):
        a_ref, b_ref = refs[:2]
        o_ref = refs[n_in]
        if parts is None:
            acc = lax.dot_general(a_ref[...], b_ref[...], dims, preferred_element_type=F32)
        else:
            acc = None
            for part in parts:
                a, b = part(a_ref, b_ref)
                prod = lax.dot_general(a, b, dims, preferred_element_type=F32)
                acc = prod if acc is None else acc + prod
        if has_res:
            acc = acc + refs[2][...]
        o_ref[...] = acc.astype(o_ref.dtype)

    return pl.pallas_call(
        body, name=name, grid=grid, in_specs=in_specs, out_specs=out_spec, out_shape=out_shape,
        compiler_params=_cparams(("parallel",) * len(grid), vmem_mb),
    )(*ins)


def _rows_for(m, row_bytes, budget):
    return _pick(m, tuple(t for t in (2048, 1024, 512, 256, 128) if t * row_bytes <= budget))


def _mm_nn_blocked(name, a, wb, out_dtype, halves=False, after=None):
    m, k = a.shape
    nb, _, bn = wb.shape
    tm = _rows_for(m, bn * jnp.dtype(out_dtype).itemsize, 6 * MIB)
    a_spec = pl.BlockSpec((tm, k), lambda j, i: (i, 0))
    b_spec = pl.BlockSpec((None, k, bn), lambda j, i: (j, 0, 0))
    if halves:
        hb = nb // 2
        out_shape = jax.ShapeDtypeStruct((2, m, hb * bn), out_dtype)
        o_spec = pl.BlockSpec((None, tm, bn), lambda j, i: (j // hb, i, j % hb))
    else:
        out_shape = jax.ShapeDtypeStruct((m, nb * bn), out_dtype)
        o_spec = pl.BlockSpec((tm, bn), lambda j, i: (i, j))
    return _mm(name, (nb, m // tm), (a, wb), [a_spec, b_spec], out_shape, o_spec, NN_DIMS, after=after)


def _mm_nn_res(name, a, w, res, after=None):
    m, k = a.shape
    n = w.shape[1]
    tm = _pick(m, (512, 256, 128))
    tn = _rows_for(n, k * 2, 12 * MIB)
    a_spec = pl.BlockSpec((tm, k), lambda j, i: (i, 0))
    b_spec = pl.BlockSpec((k, tn), lambda j, i: (0, j))
    r_spec = pl.BlockSpec((tm, tn), lambda j, i: (i, j))
    o_spec = pl.BlockSpec((tm, tn), lambda j, i: (i, j))
    return _mm(name, (n // tn, m // tm), (a, w, res), [a_spec, b_spec, r_spec], jax.ShapeDtypeStruct((m, n), F32),
               o_spec, NN_DIMS, has_res=True, after=after)


def _mm_nt_blocked(name, dy, wb, halves=False, after=None):
    nb, n, bn = wb.shape
    m = dy.shape[-2]
    tm = _pick(m, (512, 256, 128))
    tn = _rows_for(n, nb * bn * 2, 12 * MIB)
    if halves:
        hb = nb // 2
        a_spec = pl.BlockSpec((2, tm, hb * bn), lambda j, i: (0, i, 0))
        a_part = lambda kk: (lambda a_ref: a_ref[kk // hb, :, (kk % hb) * bn:(kk % hb + 1) * bn])
    else:
        a_spec = pl.BlockSpec((tm, nb * bn), lambda j, i: (i, 0))
        a_part = lambda kk: (lambda a_ref: a_ref[:, kk * bn:(kk + 1) * bn])
    parts = [(lambda a_ref, b_ref, kk=kk, sel=a_part(kk): (sel(a_ref), b_ref[kk])) for kk in range(nb)]
    b_spec = pl.BlockSpec((nb, tn, bn), lambda j, i: (0, j, 0))
    o_spec = pl.BlockSpec((tm, tn), lambda j, i: (i, j))
    return _mm(name, (n // tn, m // tm), (dy, wb), [a_spec, b_spec], jax.ShapeDtypeStruct((m, n), F32), o_spec,
               NT_DIMS, parts=parts, after=after)


def _mm_nt_plain(name, dy, w, out_dtype=F32):
    m, k = dy.shape
    n = w.shape[0]
    tm = _rows_for(m, k * 2, 8 * MIB)
    tn = _pick(n, (512, 256, 128))
    a_spec = pl.BlockSpec((tm, k), lambda j, i: (i, 0))
    b_spec = pl.BlockSpec((tn, k), lambda j, i: (j, 0))
    o_spec = pl.BlockSpec((tm, tn), lambda j, i: (i, j))
    return _mm(name, (n // tn, m // tm), (dy, w), [a_spec, b_spec], jax.ShapeDtypeStruct((m, n), out_dtype), o_spec,
               NT_DIMS)


def _mm_tn_blocked(name, a, dy, nb, halves=False):
    s, k1 = a.shape
    bn = (dy.shape[-1] * (2 if halves else 1)) // nb
    tm = _rows_for(k1, bn * 2, 6 * MIB)
    a_spec = pl.BlockSpec((s, tm), lambda j, i: (0, i))
    if halves:
        hb = nb // 2
        b_spec = pl.BlockSpec((None, s, bn), lambda j, i: (j // hb, 0, j % hb))
    else:
        b_spec = pl.BlockSpec((s, bn), lambda j, i: (0, j))
    o_spec = pl.BlockSpec((None, tm, bn), lambda j, i: (j, i, 0))
    return _mm(name, (nb, k1 // tm), (a, dy), [a_spec, b_spec], jax.ShapeDtypeStruct((nb, k1, bn), BF16), o_spec,
               TN_DIMS)


def _mm_tn_plain(name, a, dy):
    s, k1 = a.shape
    n = dy.shape[1]
    tm = _pick(k1, (512, 256, 128))
    tn = _rows_for(n, s * 2, 8 * MIB)
    a_spec = pl.BlockSpec((s, tm), lambda i, j: (0, i))
    b_spec = pl.BlockSpec((s, tn), lambda i, j: (0, j))
    o_spec = pl.BlockSpec((tm, tn), lambda i, j: (i, j))
    return _mm(name, (k1 // tm, n // tn), (a, dy), [a_spec, b_spec], jax.ShapeDtypeStruct((k1, n), BF16), o_spec,
               TN_DIMS)


def _rstd(x):
    return lax.rsqrt(jnp.mean(x * x, axis=-1, keepdims=True) + EPS)


def _norm_bwd(dy, xhat, r, g):
    dxhat = dy * g
    return r * (dxhat - xhat * jnp.mean(dxhat * xhat, axis=-1, keepdims=True))


def _rmsnorm_fwd(name, x, g):
    s, d = x.shape
    tr = _pick(s, (256, 128))

    def body(x_ref, g_ref, h_ref):
        xv = x_ref[...]
        h_ref[...] = (xv * _rstd(xv) * g_ref[...]).astype(BF16)

    return pl.pallas_call(
        body, name=name, grid=(s // tr,),
        in_specs=[pl.BlockSpec((tr, d), lambda i: (i, 0)), pl.BlockSpec((1, d), lambda i: (0, 0))],
        out_specs=pl.BlockSpec((tr, d), lambda i: (i, 0)),
        out_shape=jax.ShapeDtypeStruct((s, d), BF16), compiler_params=_cparams(("parallel",)),
    )(x, g)


def _rmsnorm_bwd(name, dh, x, g, dres):
    s, d = x.shape
    tr = _pick(s, (256, 128))

    def body(dh_ref, x_ref, g_ref, dres_ref, dx_ref, dxb_ref, dg_ref):
        xv = x_ref[...]
        r = _rstd(xv)
        xhat = xv * r
        dhv = dh_ref[...]
        dx = dres_ref[...] + _norm_bwd(dhv, xhat, r, g_ref[...])
        dx_ref[...] = dx
        dxb_ref[...] = dx.astype(BF16)
        part = jnp.sum(dhv * xhat, axis=0, keepdims=True)

        @pl.when(pl.program_id(0) == 0)
        def _():
            dg_ref[...] = part

        @pl.when(pl.program_id(0) > 0)
        def _():
            dg_ref[...] += part

    row = pl.BlockSpec((tr, d), lambda i: (i, 0))
    vec = pl.BlockSpec((1, d), lambda i: (0, 0))
    return pl.pallas_call(
        body, name=name, grid=(s // tr,), in_specs=[row, row, vec, row], out_specs=(row, row, vec),
        out_shape=(jax.ShapeDtypeStruct((s, d), F32), jax.ShapeDtypeStruct((s, d), BF16),
                   jax.ShapeDtypeStruct((1, d), F32)),
        compiler_params=_cparams(("arbitrary",)),
    )(dh, x, g, dres)


def _loss_head(y, target):
    s, d = y.shape
    tr = _pick(s, (256, 128))

    def body(y_ref, t_ref, dy_ref, dyb_ref, loss_ref):
        err = y_ref[...] - t_ref[...]
        dy = err * (1.0 / d)
        dy_ref[...] = dy
        dyb_ref[...] = dy.astype(BF16)
        part = 0.5 * jnp.sum(jnp.mean(err * err, axis=-1, keepdims=True), axis=0, keepdims=True)
        part = jnp.broadcast_to(part, (1, 128))

        @pl.when(pl.program_id(0) == 0)
        def _():
            loss_ref[...] = part

        @pl.when(pl.program_id(0) > 0)
        def _():
            loss_ref[...] += part

    row = pl.BlockSpec((tr, d), lambda i: (i, 0))
    return pl.pallas_call(
        body, name="loss_head", grid=(s // tr,), in_specs=[row, row],
        out_specs=(row, row, pl.BlockSpec((1, 128), lambda i: (0, 0))),
        out_shape=(jax.ShapeDtypeStruct((s, d), F32), jax.ShapeDtypeStruct((s, d), BF16),
                   jax.ShapeDtypeStruct((1, 128), F32)),
        compiler_params=_cparams(("arbitrary",)),
    )(y, target)


def _split_dot(x, tri):
    hi = x.astype(BF16)
    lo = (x - hi.astype(F32)).astype(BF16)
    return (jnp.dot(hi, tri, preferred_element_type=F32) + jnp.dot(lo, tri, preferred_element_type=F32))


def _tile_iotas():
    row = lax.broadcasted_iota(jnp.int32, (TILE, TILE), 0)
    col = lax.broadcasted_iota(jnp.int32, (TILE, TILE), 1)
    return row, col


def _sb_logits(qi, kb, mask):
    z = lax.dot_general(qi, kb, NT_DIMS, preferred_element_type=F32) * (HEAD_DIM ** -0.5)
    sp = jnp.log1p(jnp.exp(-jnp.abs(z)))
    lb = jnp.minimum(z, 0.0) - sp
    l1m = -jnp.maximum(z, 0.0) - sp
    if mask is not None:
        l1m = jnp.where(mask, l1m, 0.0)
    return lb, l1m


def _attn_fwd(p, gq, gk, ga, n_heads):
    s = p.shape[0]
    nq = s // TILE

    hp = ATTN_HEADS_PER_STEP
    wd = hp * HEAD_DIM

    def body(q_ref, k_ref, v_ref, gq_ref, gk_ref, ga_ref, att_ref, o_ref, r_ref, qn_s, kn_s, vb_s):
        heads = [slice(hh * HEAD_DIM, (hh + 1) * HEAD_DIM) for hh in range(hp)]
        for hd in heads:
            qv = q_ref[:, hd]
            qn_s[:, hd] = (qv * _rstd(qv) * gq_ref[...]).astype(BF16)
            kv = k_ref[:, hd]
            kn_s[:, hd] = (kv * _rstd(kv) * gk_ref[...]).astype(BF16)
        vb_s[...] = v_ref[...].astype(BF16)
        row, col = _tile_iotas()
        causal = col < row
        upper_ones = jnp.concatenate([(row > col).astype(BF16), jnp.ones((TILE, TILE), BF16)], axis=1)

        def tiles(rows, key_blocks, states, mask):
            chains = [(hi, hd, keys) for hi, hd in enumerate(heads) for keys in key_blocks]
            logits = [_sb_logits(qn_s[rows, hd], kn_s[keys, hd], mask) for _, hd, keys in chains]
            sums = [_split_dot(l1m, upper_ones) for _, l1m in logits]
            carry = [c for _, c in states]
            probs = []
            for (hi, _, _), (lb, _), sm in zip(chains, logits, sums):
                a = jnp.exp(lb + sm[:, :TILE] + carry[hi])
                carry[hi] = carry[hi] + sm[:, TILE:]
                probs.append((a if mask is None else jnp.where(mask, a, 0.0)).astype(BF16))
            outs = [jnp.dot(a, vb_s[keys, hd], preferred_element_type=F32) for a, (_, hd, keys) in zip(probs, chains)]
            acc = [o_acc for o_acc, _ in states]
            for (hi, _, _), o in zip(chains, outs):
                acc[hi] = acc[hi] + o
            return tuple(zip(acc, carry))

        def key_block(b):
            return pl.ds(pl.multiple_of(b * TILE, TILE), TILE)

        def qblock(i, _):
            rows = pl.ds(pl.multiple_of(i * TILE, TILE), TILE)
            zero = jnp.zeros((TILE, HEAD_DIM), F32)
            states = tiles(rows, [rows], tuple((zero, zero) for _ in heads), causal)
            states = lax.cond(i % 2 == 1, lambda st: tiles(rows, [key_block(i - 1)], st, None), lambda st: st, states)
            top = i - i % 2

            def kblocks(jj, states):
                return tiles(rows, [key_block(top - 1 - 2 * jj), key_block(top - 2 - 2 * jj)], states, None)

            states = lax.fori_loop(0, i // 2, kblocks, states)
            for hh, (hd, (o_acc, c)) in enumerate(zip(heads, states)):
                o_ref[rows, hd] = o_acc
                r_ref[rows, hd] = c
                att_ref[rows, hd] = (o_acc * _rstd(o_acc) * ga_ref[hh]).astype(BF16)
            return 0

        lax.fori_loop(0, nq, qblock, 0)

    col_blk = lambda off: pl.BlockSpec((s, wd), lambda h: (0, off + h))
    vec = pl.BlockSpec((1, HEAD_DIM), lambda h: (0, 0))
    hvec = pl.BlockSpec((hp, 1, HEAD_DIM), lambda h: (h, 0, 0))
    out = pl.BlockSpec((s, wd), lambda h: (0, h))
    w = n_heads * HEAD_DIM
    steps = n_heads // hp
    return pl.pallas_call(
        body, name="attn_fwd", grid=(steps,),
        in_specs=[col_blk(0), col_blk(steps), col_blk(2 * steps), vec, vec, hvec],
        out_specs=(out, out, out),
        out_shape=(jax.ShapeDtypeStruct((s, w), BF16), jax.ShapeDtypeStruct((s, w), F32),
                   jax.ShapeDtypeStruct((s, w), F32)),
        scratch_shapes=[pltpu.VMEM((s, wd), BF16)] * 3,
        compiler_params=_cparams(("parallel",), ATTN_VMEM_MB),
    )(p, p, p, gq, gk, ga)


def _attn_bwd(p, gq, gk, ga, o, rsum, dmix, n_heads):
    s = p.shape[0]
    nq = s // TILE

    hp = ATTN_HEADS_PER_STEP
    wd = hp * HEAD_DIM
    scale = HEAD_DIM ** -0.5

    def body(q_ref, k_ref, v_ref, gq_ref, gk_ref, ga_ref, o_ref, r_ref, dm_ref,
             dq_ref, dk_ref, dv_ref, dgq_ref, dgk_ref, dga_ref,
             qn_s, kn_s, vb_s, do_s, dqn_s, dkn_s, dv_s):
        step = pl.program_id(0)
        gqv, gkv = gq_ref[...], gk_ref[...]
        heads = [slice(hh * HEAD_DIM, (hh + 1) * HEAD_DIM) for hh in range(hp)]
        for hh, hd in enumerate(heads):
            qv = q_ref[:, hd]
            qn_s[:, hd] = (qv * _rstd(qv) * gqv).astype(BF16)
            kv = k_ref[:, hd]
            kn_s[:, hd] = (kv * _rstd(kv) * gkv).astype(BF16)
            ov = o_ref[:, hd]
            ro = _rstd(ov)
            ohat = ov * ro
            dm = dm_ref[:, hd]
            dga_ref[hh] = jnp.sum(dm * ohat, axis=0, keepdims=True)
            do_s[:, hd] = _norm_bwd(dm, ohat, ro, ga_ref[hh]).astype(BF16)
        vb_s[...] = v_ref[...].astype(BF16)
        dkn_s[...] = jnp.zeros_like(dkn_s)
        dv_s[...] = jnp.zeros_like(dv_s)
        row, col = _tile_iotas()
        causal = col < row
        ones = jnp.ones((TILE, TILE), BF16)
        incl_ones = jnp.concatenate([(row <= col).astype(BF16), ones], axis=1)
        excl_ones = jnp.concatenate([(row < col).astype(BF16), ones], axis=1)

        def tiles(rows, key_blocks, states, mask):
            chains = [(hi, hd, keys) for hi, hd in enumerate(heads) for keys in key_blocks]
            qis = [qn_s[rows, hd] for hd in heads]
            dois = [do_s[rows, hd] for hd in heads]
            logits = [_sb_logits(qis[hi], kn_s[keys, hd], mask) for hi, hd, keys in chains]
            sums = [_split_dot(l1m, incl_ones) for _, l1m in logits]
            das = [lax.dot_general(dois[hi], vb_s[keys, hd], NT_DIMS, preferred_element_type=F32)
                   for hi, hd, keys in chains]
            pfx = [st[1] for st in states]
            probs, dss = [], []
            for (hi, hd, _), (lb, _), sm, da in zip(chains, logits, sums, das):
                a = jnp.exp(lb + (r_ref[rows, hd] - pfx[hi] - sm[:, :TILE]))
                pfx[hi] = pfx[hi] + sm[:, TILE:]
                a = a if mask is None else jnp.where(mask, a, 0.0)
                probs.append(a.astype(BF16))
                dss.append(da * a)
            dsums = [_split_dot(ds, excl_ones) for ds in dss]
            pc = [st[2] for st in states]
            dzs = []
            for (hi, _, _), (lb, _), ds, dsm in zip(chains, logits, dss, dsums):
                dl1m = pc[hi] + dsm[:, :TILE]
                pc[hi] = pc[hi] + dsm[:, TILE:]
                dl1m = dl1m if mask is None else jnp.where(mask, dl1m, 0.0)
                beta = jnp.exp(lb)
                dzs.append(((ds * (1.0 - beta) - dl1m * beta) * scale).astype(BF16))
            dqs = [jnp.dot(dz, kn_s[keys, hd], preferred_element_type=F32) for dz, (_, hd, keys) in zip(dzs, chains)]
            for dz, a, (hi, hd, keys) in zip(dzs, probs, chains):
                dkn_s[keys, hd] += lax.dot_general(dz, qis[hi], TN_DIMS, preferred_element_type=F32)
                dv_s[keys, hd] += lax.dot_general(a, dois[hi], TN_DIMS, preferred_element_type=F32)
            dq_acc = [st[0] for st in states]
            for (hi, _, _), dq in zip(chains, dqs):
                dq_acc[hi] = dq_acc[hi] + dq
            return tuple(zip(dq_acc, pfx, pc))

        def key_block(b):
            return pl.ds(pl.multiple_of(b * TILE, TILE), TILE)

        def qblock(i, _):
            rows = pl.ds(pl.multiple_of(i * TILE, TILE), TILE)
            zero = jnp.zeros((TILE, HEAD_DIM), F32)

            def kblocks(jj, states):
                return tiles(rows, [key_block(2 * jj), key_block(2 * jj + 1)], states, None)

            states = lax.fori_loop(0, i // 2, kblocks, tuple((zero, zero, zero) for _ in heads))
            states = lax.cond(i % 2 == 1, lambda st: tiles(rows, [key_block(i - 1)], st, None), lambda st: st, states)
            states = tiles(rows, [rows], states, causal)
            for hd, (dq_acc, _, _) in zip(heads, states):
                dqn_s[rows, hd] = dq_acc
            return 0

        lax.fori_loop(0, nq, qblock, 0)

        def norm_in_bwd(x_ref, g, dn_s, dx_ref, dg_ref):
            part = jnp.zeros((1, HEAD_DIM), F32)
            for hd in heads:
                xv = x_ref[:, hd]
                r = _rstd(xv)
                xhat = xv * r
                dn = dn_s[:, hd]
                dx_ref[:, hd] = _norm_bwd(dn, xhat, r, g).astype(BF16)
                part = part + jnp.sum(dn * xhat, axis=0, keepdims=True)

            @pl.when(step == 0)
            def _():
                dg_ref[...] = part

            @pl.when(step > 0)
            def _():
                dg_ref[...] += part

        norm_in_bwd(q_ref, gqv, dqn_s, dq_ref, dgq_ref)
        norm_in_bwd(k_ref, gkv, dkn_s, dk_ref, dgk_ref)
        dv_ref[...] = dv_s[...].astype(BF16)

    once = pl.Buffered(1)
    steps = n_heads // hp
    col_blk = lambda off: pl.BlockSpec((s, wd), lambda h: (0, off + h), pipeline_mode=once)
    vec = pl.BlockSpec((1, HEAD_DIM), lambda h: (0, 0))
    hvec = pl.BlockSpec((hp, 1, HEAD_DIM), lambda h: (h, 0, 0))
    blk = pl.BlockSpec((s, wd), lambda h: (0, h), pipeline_mode=once)
    w = n_heads * HEAD_DIM
    big = jax.ShapeDtypeStruct((s, w), BF16)
    return pl.pallas_call(
        body, name="attn_bwd", grid=(steps,),
        in_specs=[col_blk(0), col_blk(steps), col_blk(2 * steps), vec, vec, hvec, blk, blk, blk],
        out_specs=(blk, blk, blk, vec, vec, hvec),
        out_shape=(big, big, big, jax.ShapeDtypeStruct((1, HEAD_DIM), F32), jax.ShapeDtypeStruct((1, HEAD_DIM), F32),
                   jax.ShapeDtypeStruct((n_heads, 1, HEAD_DIM), F32)),
        scratch_shapes=[pltpu.VMEM((s, wd), BF16)] * 4 + [pltpu.VMEM((s, wd), F32)] * 3,
        compiler_params=_cparams(("arbitrary",), ATTN_VMEM_MB),
    )(p, p, p, gq, gk, ga, o, rsum, dmix)


_INV_SQRT2 = 0.7071067811865476
_INV_SQRT_2PI = 0.3989422804014327


def _gelu(x):
    return 0.5 * x * (1.0 + lax.erf(x * _INV_SQRT2))


def _gelu_grad(x):
    return 0.5 * (1.0 + lax.erf(x * _INV_SQRT2)) + x * (_INV_SQRT_2PI * jnp.exp(-0.5 * x * x))


def _sgu_fwd(p, gs, w_s, b_s, gb, col0, after):
    s = p.shape[0]
    n_chunks = s // TILE
    per_trip = _pick(n_chunks, (4, 2, 1))

    def body(u_ref, v_ref, gs_ref, w_ref, b_ref, gb_ref, _, out_ref, vs_s):
        vg = _gelu(v_ref[...])
        vs_s[...] = (vg * _rstd(vg) * gs_ref[...]).astype(BF16)
        row, col = _tile_iotas()
        wt = jnp.where(col <= row, w_ref[...], 0.0).astype(BF16)
        bcol = b_ref[...]
        gbv = gb_ref[...]

        def chunks(c, _):
            rows = [pl.ds(pl.multiple_of((c * per_trip + k) * TILE, TILE), TILE) for k in range(per_trip)]
            mixed = [jnp.dot(wt, vs_s[r, :], preferred_element_type=F32) + bcol for r in rows]
            sgs = [_gelu(u_ref[r, :]) * mx for r, mx in zip(rows, mixed)]
            for r, sg in zip(rows, sgs):
                out_ref[r, :] = (sg * _rstd(sg) * gbv).astype(BF16)
            return 0

        lax.fori_loop(0, n_chunks // per_trip, chunks, 0)

    col_blk = lambda off: pl.BlockSpec((s, HEAD_DIM), lambda g: (0, off + g))
    gvec = pl.BlockSpec((None, 1, HEAD_DIM), lambda g: (g, 0, 0))
    return pl.pallas_call(
        body, name="sgu_fwd", grid=(N_GROUPS,),
        in_specs=[col_blk(col0), col_blk(col0 + N_GROUPS), gvec,
                  pl.BlockSpec((None, TILE, TILE), lambda g: (g, 0, 0)),
                  pl.BlockSpec((None, TILE, 1), lambda g: (g, 0, 0)), gvec,
                  pl.BlockSpec(after.shape, lambda g: (0, 0))],
        out_specs=pl.BlockSpec((s, HEAD_DIM), lambda g: (0, g)),
        out_shape=jax.ShapeDtypeStruct((s, N_GROUPS * HEAD_DIM), BF16),
        scratch_shapes=[pltpu.VMEM((s, HEAD_DIM), BF16)],
        compiler_params=_cparams(("parallel",)),
    )(p, p, gs, w_s, b_s, gb, after)


def _sgu_bwd(p, gs, w_s, b_s, gb, dmix, col0, dm_col0):
    s = p.shape[0]
    n_chunks = s // TILE
    per_trip = _pick(n_chunks, (4, 2, 1))

    def body(u_ref, v_ref, gs_ref, w_ref, b_ref, gb_ref, dm_ref,
             du_ref, dv_ref, dgs_ref, dw_ref, db_ref, dgb_ref, vs_s, dvs_s):
        gsv = gs_ref[...]
        gbv = gb_ref[...]
        vg = _gelu(v_ref[...])
        vs_s[...] = (vg * _rstd(vg) * gsv).astype(BF16)
        row, col = _tile_iotas()
        causal = col <= row
        wt = jnp.where(causal, w_ref[...], 0.0).astype(BF16)
        bcol = b_ref[...]

        def chunks(c, carry):
            dw_acc, db_acc, dgb_acc = carry
            rows = [pl.ds(pl.multiple_of((c * per_trip + k) * TILE, TILE), TILE) for k in range(per_trip)]
            vss = [vs_s[r, :] for r in rows]
            mixed = [jnp.dot(wt, vs, preferred_element_type=F32) + bcol for vs in vss]
            dmbs = []
            for r, mx in zip(rows, mixed):
                u_pre = u_ref[r, :]
                u = _gelu(u_pre)
                sg = u * mx
                rs = _rstd(sg)
                sghat = sg * rs
                dm = dm_ref[r, :]
                dsg = _norm_bwd(dm, sghat, rs, gbv)
                dgb_acc = dgb_acc + jnp.sum(dm * sghat, axis=0, keepdims=True)
                du_ref[r, :] = (dsg * mx * _gelu_grad(u_pre)).astype(BF16)
                dmixed = dsg * u
                db_acc = db_acc + jnp.sum(dmixed, axis=1, keepdims=True)
                dmbs.append(dmixed.astype(BF16))
            for dmb, vs in zip(dmbs, vss):
                dw_acc = dw_acc + lax.dot_general(dmb, vs, NT_DIMS, preferred_element_type=F32)
            for r, dmb in zip(rows, dmbs):
                dvs_s[r, :] = lax.dot_general(wt, dmb, TN_DIMS, preferred_element_type=F32)
            return dw_acc, db_acc, dgb_acc

        dw_acc, db_acc, dgb_acc = lax.fori_loop(
            0, n_chunks // per_trip, chunks,
            (jnp.zeros((TILE, TILE), F32), jnp.zeros((TILE, 1), F32), jnp.zeros((1, HEAD_DIM), F32)))
        dw_ref[...] = jnp.where(causal, dw_acc, 0.0)
        db_ref[...] = db_acc
        dgb_ref[...] = dgb_acc
        v_pre = v_ref[...]
        vg = _gelu(v_pre)
        rv = _rstd(vg)
        vhat = vg * rv
        dvs = dvs_s[...]
        dgs_ref[...] = jnp.sum(dvs * vhat, axis=0, keepdims=True)
        dv_ref[...] = (_norm_bwd(dvs, vhat, rv, gsv) * _gelu_grad(v_pre)).astype(BF16)

    col_blk = lambda off: pl.BlockSpec((s, HEAD_DIM), lambda g: (0, off + g))
    gvec = pl.BlockSpec((None, 1, HEAD_DIM), lambda g: (g, 0, 0))
    wspec = pl.BlockSpec((None, TILE, TILE), lambda g: (g, 0, 0))
    bspec = pl.BlockSpec((None, TILE, 1), lambda g: (g, 0, 0))
    blk = pl.BlockSpec((s, HEAD_DIM), lambda g: (0, g))
    big = jax.ShapeDtypeStruct((s, N_GROUPS * HEAD_DIM), BF16)
    gshape = jax.ShapeDtypeStruct((N_GROUPS, 1, HEAD_DIM), F32)
    return pl.pallas_call(
        body, name="sgu_bwd", grid=(N_GROUPS,),
        in_specs=[col_blk(col0), col_blk(col0 + N_GROUPS), gvec, wspec, bspec, gvec, col_blk(dm_col0)],
        out_specs=(blk, blk, gvec, wspec, bspec, gvec),
        out_shape=(big, big, gshape, jax.ShapeDtypeStruct((N_GROUPS, TILE, TILE), F32),
                   jax.ShapeDtypeStruct((N_GROUPS, TILE, 1), F32), gshape),
        scratch_shapes=[pltpu.VMEM((s, HEAD_DIM), BF16), pltpu.VMEM((s, HEAD_DIM), F32)],
        compiler_params=_cparams(("parallel",)),
    )(p, p, gs, w_s, b_s, gb, dmix)


SUBLANES = 8


def _shift_down(x, n):
    rolled = pltpu.roll(x, n, 0)
    edge = lax.broadcasted_iota(jnp.int32, (SUBLANES, x.shape[1]), 0)
    return jnp.concatenate([jnp.where(edge >= n, rolled[:SUBLANES], 0.0), rolled[SUBLANES:]], axis=0)


def _shift_up(x, n):
    s = x.shape[0]
    rolled = pltpu.roll(x, s - n, 0)
    edge = lax.broadcasted_iota(jnp.int32, (SUBLANES, x.shape[1]), 0)
    return jnp.concatenate([rolled[:s - SUBLANES], jnp.where(edge < SUBLANES - n, rolled[s - SUBLANES:], 0.0)], axis=0)


def _conv(x, w, b):
    x1, x2 = _shift_down(x, 1), _shift_down(x, 2)
    return b + w[0:1, :] * x2 + w[1:2, :] * x1 + w[2:3, :] * x, x1, x2


def _conv_specs(s, tn):
    xspec = pl.BlockSpec((2, s, tn), lambda j: (0, 0, j))
    wspec = pl.BlockSpec((2, CONV_WIDTH, tn), lambda j: (0, 0, j))
    bspec = pl.BlockSpec((2, 1, tn), lambda j: (0, 0, j))
    return xspec, wspec, bspec


def _conv_gate_fwd(up, cw, cb):
    _, s, f = up.shape
    tn = _pick(f, (256, 128))

    def body(x_ref, w_ref, b_ref, act_ref):
        gate = _conv(x_ref[0], w_ref[0], b_ref[0])[0]
        val = _conv(x_ref[1], w_ref[1], b_ref[1])[0]
        act_ref[...] = (gate * jax.nn.sigmoid(gate) * val).astype(BF16)

    xspec, wspec, bspec = _conv_specs(s, tn)
    return pl.pallas_call(
        body, name="conv_gate_fwd", grid=(f // tn,), in_specs=[xspec, wspec, bspec],
        out_specs=pl.BlockSpec((s, tn), lambda j: (0, j)), out_shape=jax.ShapeDtypeStruct((s, f), BF16),
        compiler_params=_cparams(("parallel",)),
    )(up, cw, cb)


def _down_dx_conv_gate_bwd(up, cw, cb, dy, wdown, after):
    _, s, f = up.shape
    d = dy.shape[1]
    tn = _pick(f, (256, 128))

    def body(x_ref, w_ref, b_ref, dy_ref, wd_ref, _, dx_ref, dw_ref, db_ref):
        da = lax.dot_general(dy_ref[...], wd_ref[...], NT_DIMS, preferred_element_type=F32)
        xg, xv = x_ref[0], x_ref[1]
        wg, wv = w_ref[0], w_ref[1]
        gate, xg1, xg2 = _conv(xg, wg, b_ref[0])
        val, xv1, xv2 = _conv(xv, wv, b_ref[1])
        sig = jax.nn.sigmoid(gate)
        dval = da * (gate * sig)
        dgate = da * val * (sig * (1.0 + gate * (1.0 - sig)))
        for half, (x, x1, x2, w, dz) in enumerate(((xg, xg1, xg2, wg, dgate), (xv, xv1, xv2, wv, dval))):
            dx_ref[half] = (w[2:3, :] * dz + w[1:2, :] * _shift_up(dz, 1) + w[0:1, :] * _shift_up(dz, 2)).astype(BF16)
            dw_ref[half, 0:1, :] = jnp.sum(dz * x2, axis=0, keepdims=True)
            dw_ref[half, 1:2, :] = jnp.sum(dz * x1, axis=0, keepdims=True)
            dw_ref[half, 2:3, :] = jnp.sum(dz * x, axis=0, keepdims=True)
            db_ref[half] = jnp.sum(dz, axis=0, keepdims=True)

    xspec, wspec, bspec = _conv_specs(s, tn)
    return pl.pallas_call(
        body, name="down_dx_conv_gate_bwd", grid=(f // tn,),
        in_specs=[xspec, wspec, bspec, pl.BlockSpec((s, d), lambda j: (0, 0)), pl.BlockSpec((tn, d), lambda j: (j, 0)),
                  pl.BlockSpec(after.shape, lambda j: (0, 0))],
        out_specs=(xspec, wspec, bspec),
        out_shape=(jax.ShapeDtypeStruct((2, s, f), BF16), jax.ShapeDtypeStruct((2, CONV_WIDTH, f), F32),
                   jax.ShapeDtypeStruct((2, 1, f), F32)),
        compiler_params=_cparams(("parallel",), 56),
    )(up, cw, cb, dy, wdown, after)


def _mesh_pos():
    return lax.axis_index("x"), lax.axis_index("y"), lax.axis_index("c")


def _remote(src, dst, send_sem, recv_sem, to):
    return pltpu.make_async_remote_copy(src_ref=src, dst_ref=dst, send_sem=send_sem, recv_sem=recv_sem,
                                        device_id=to, device_id_type=pl.DeviceIdType.MESH)


HBM_SPEC = pl.BlockSpec(memory_space=pltpu.HBM)
SEM_SPEC = pl.BlockSpec(memory_space=pltpu.SEMAPHORE)
ANY_SPEC = pl.BlockSpec(memory_space=pl.ANY)
TOKEN_SPEC = pl.BlockSpec(memory_space=pltpu.VMEM)
TOKEN_SHAPE = jax.ShapeDtypeStruct((8, 128), F32)
DATAFLOW = pltpu.SideEffectType.DATAFLOW_SIDE_EFFECTING
GATHER_PLANE = (2, 4, 6)


def _slot(pos):
    return 4 * pos[0] + 2 * pos[1] + pos[2]


def _flip(pos, k):
    return (pos[0] ^ ((k >> 2) & 1), pos[1] ^ ((k >> 1) & 1), pos[2] ^ (k & 1))


def _hbm(a):
    return pltpu.with_memory_space_constraint(a, pltpu.HBM)


def _hbm_shapes(arrays):
    return tuple(pltpu.HBM(a.shape, a.dtype) for a in arrays)


class _Split:
    def __init__(self, n, outs, has_sems):
        k = 2 * n if has_sems else 0
        self.n = n
        self.sems = list(outs[:k])
        self.bufs = list(outs[k:k + 2 * n])
        self.token = outs[-1]


def _split_call(name, body, bufs, sems_in, makes_sems, after):
    n = len(bufs) // 2
    k = 2 * n if makes_sems else 0
    m = len(sems_in)
    afters = list(after) if isinstance(after, (list, tuple)) else [after]
    na = len(afters)

    def wrapped(*refs):
        srcs, dsts = refs[:n], refs[n:2 * n]
        s_in = refs[2 * n:2 * n + m]
        s_out = refs[2 * n + m + na:2 * n + m + na + k]
        token, local_sems = refs[-2], refs[-1]
        body(srcs, dsts, s_in, s_out, local_sems)
        token[...] = jnp.zeros_like(token)

    outs = pl.pallas_call(
        wrapped, name=name,
        out_shape=(pltpu.SemaphoreType.DMA(()),) * k + _hbm_shapes(bufs) + (TOKEN_SHAPE,),
        in_specs=[HBM_SPEC] * (2 * n) + [SEM_SPEC] * m + [ANY_SPEC] * na,
        out_specs=(SEM_SPEC,) * k + (HBM_SPEC,) * (2 * n) + (TOKEN_SPEC,),
        input_output_aliases={i: k + i for i in range(2 * n)},
        scratch_shapes=[pltpu.SemaphoreType.DMA((n,))],
        compiler_params=pltpu.CompilerParams(has_side_effects=DATAFLOW),
    )(*[_hbm(b) for b in bufs], *sems_in, *afters)
    return _Split(n, outs, makes_sems)


def _wait_slots(land, count, send_sem, recv_sem, me, send=False, recv=False):
    span = land.at[pl.ds(0, count)]
    cp = _remote(span, span, send_sem, recv_sem, me)
    if send:
        cp.wait_send()
    if recv:
        cp.wait_recv()


def _gather_start(name, shards, after):
    n = len(shards)
    my_slot = _slot(_mesh_pos())
    lands = [lax.dynamic_update_slice(lax.empty((N_DEV,) + w.shape, w.dtype), w[None], (my_slot, 0, 0)) for w in shards]

    def body(srcs, dsts, _, sems, local_sems):
        me = _mesh_pos()
        for a in range(n):
            for k in (1,) + GATHER_PLANE:
                _remote(srcs[a], dsts[a].at[_slot(me)], sems[a], sems[n + a], _flip(me, k)).start()

    return _split_call(name, body, list(shards) + lands, [], True, after)


def _gather_forward(name, started, after):
    n = started.n

    def body(srcs, dsts, sems_a, sems_b, local_sems):
        me = _mesh_pos()
        sibling = _flip(me, 1)
        for a in range(n):
            _wait_slots(dsts[a], 4, sems_a[a], sems_a[n + a], me, recv=True)
            for k in GATHER_PLANE:
                block = dsts[a].at[_slot(_flip(me, k))]
                _remote(block, block, sems_b[a], sems_b[n + a], sibling).start()
        for a in range(n):
            _wait_slots(dsts[a], 4, sems_a[a], sems_a[n + a], me, send=True)

    return _split_call(name, body, started.bufs, started.sems, True, after)


def _gather_finish(name, forwarded, after):
    n = forwarded.n

    def body(srcs, dsts, sems_b, _, local_sems):
        me = _mesh_pos()
        for a in range(n):
            _wait_slots(dsts[a], 3, sems_b[a], sems_b[n + a], me, send=True, recv=True)

    return _split_call(name, body, forwarded.bufs, forwarded.sems, False, after).bufs[n:]


def _exchange_start(name, blocked, after):
    n = len(blocked)
    my_slot = _slot(_mesh_pos())
    rows = [w.shape[-2] // (N_DEV if w.ndim == 2 else 1) for w in blocked]

    def block(ref, a, slot):
        if len(ref.shape) == 3:
            return ref.at[slot]
        return ref.at[pl.ds(pl.multiple_of(slot * rows[a], 16), rows[a])]

    lands = []
    for w, r in zip(blocked, rows):
        mine = lax.dynamic_slice_in_dim(w, my_slot, 1, 0) if w.ndim == 3 else lax.dynamic_slice_in_dim(w, my_slot * r, r, 0)[None]
        lands.append(lax.dynamic_update_slice(lax.empty((N_DEV, r, w.shape[-1]), w.dtype), mine, (my_slot, 0, 0)))

    def body(srcs, dsts, _, sems, local_sems):
        me = _mesh_pos()
        for a in range(n):
            for k in range(1, N_DEV):
                peer = _flip(me, k)
                _remote(block(srcs[a], a, _slot(peer)), dsts[a].at[_slot(me)], sems[a], sems[n + a], peer).start()

    return _split_call(name, body, list(blocked) + lands, [], True, after)


def _exchange_finish(name, started, after):
    n = started.n

    def body(srcs, dsts, sems, _, local_sems):
        me = _mesh_pos()
        for a in range(n):
            _wait_slots(dsts[a], N_DEV - 1, sems[a], sems[n + a], me, send=True, recv=True)

    return _split_call(name, body, started.bufs, started.sems, False, after).bufs[n:]


def _broadcast_start(name, arrays, after):
    n = len(arrays)
    my_slot = _slot(_mesh_pos())
    lands = [lax.dynamic_update_slice(lax.empty((N_DEV,) + w.shape, w.dtype), w[None], (my_slot, 0, 0)) for w in arrays]

    def body(srcs, dsts, _, sems, local_sems):
        me = _mesh_pos()
        for a in range(n):
            for k in range(1, N_DEV):
                _remote(srcs[a], dsts[a].at[_slot(me)], sems[a], sems[n + a], _flip(me, k)).start()

    return _split_call(name, body, list(arrays) + lands, [], True, after)


def _adamw_math(w, g, m, v):
    m = ADAM_B1 * m + (1.0 - ADAM_B1) * g
    v = ADAM_B2 * v + (1.0 - ADAM_B2) * (g * g)
    m_hat = m / (1.0 - ADAM_B1 ** ADAM_STEP)
    v_hat = v / (1.0 - ADAM_B2 ** ADAM_STEP)
    delta = -ADAM_LR * (m_hat / (jnp.sqrt(v_hat) + ADAM_EPS) + ADAM_WD * w)
    return delta, m, v


def _adamw(name, w, m, v, parts, layer, prev=None):
    _, r, c = w.shape
    tr = _pick(r, tuple(t for t in (256, 128, 64, 32, 16) if t * c <= ADAMW_TILE_ELEMS))
    n_prev = 0 if prev is None else 4

    def body(*refs):
        w_ref, m_ref, v_ref, p_ref = refs[:4]
        g_ref, d_ref, nm_ref, nv_ref = refs[4 + n_prev:]
        g = p_ref[0].astype(F32)
        for src in range(1, N_DEV):
            g = g + p_ref[src].astype(F32)
        delta, nm, nv = _adamw_math(w_ref[...], g, m_ref[...], v_ref[...])
        g_ref[...] = g
        d_ref[...] = delta
        nm_ref[...] = nm
        nv_ref[...] = nv

    wspec = pl.BlockSpec((None, tr, c), lambda i: (layer, i, 0))
    pspec = pl.BlockSpec((N_DEV, tr, c), lambda i: (0, i, 0))
    shp = jax.ShapeDtypeStruct(w.shape, F32)
    return pl.pallas_call(
        body, name=name, grid=(r // tr,), in_specs=[wspec] * 3 + [pspec] + [ANY_SPEC] * n_prev,
        out_specs=(wspec,) * 4, out_shape=(shp,) * 4, input_output_aliases={4 + j: j for j in range(n_prev)},
        compiler_params=_cparams(("parallel",)),
    )(w, m, v, parts, *([] if prev is None else prev))


PACK_TILE = 8 * 128


def _pack(arrays):
    flat = []
    for a in arrays:
        v = a.reshape(-1)
        pad = (-v.shape[0]) % PACK_TILE
        flat.append(jnp.pad(v, (0, pad)) if pad else v)
    return jnp.concatenate(flat).reshape(-1, 128)


def _unpack(buf, like):
    flat = buf.reshape(-1)
    out, off = [], 0
    for a in like:
        n = 1
        for dim in a.shape:
            n *= dim
        out.append(flat[off:off + n].reshape(a.shape))
        off += n + (-n) % PACK_TILE
    return out


def _sum_slots(name, gathered):
    _, r, c = gathered.shape

    def body(x_ref, o_ref):
        acc = x_ref[0]
        for src in range(1, N_DEV):
            acc = acc + x_ref[src]
        o_ref[...] = acc

    return pl.pallas_call(body, name=name, out_shape=jax.ShapeDtypeStruct((r, c), F32))(gathered)


def _adamw_small(w, g, m, v):
    shp = jax.ShapeDtypeStruct(w.shape, F32)

    def body(w_ref, g_ref, m_ref, v_ref, d_ref, nm_ref, nv_ref):
        delta, nm, nv = _adamw_math(w_ref[...], g_ref[...], m_ref[...], v_ref[...])
        d_ref[...] = delta
        nm_ref[...] = nm
        nv_ref[...] = nv

    return pl.pallas_call(body, name="adamw_small", out_shape=(shp,) * 3)(w, g, m, v)


def kernel(x, attn_norm_g, w_in, q_norm_g, k_norm_g, sgu_norm_g, sgu_w, sgu_b, out_norm_a_g, out_norm_b_g, w_out, ffn_norm_g, w_up, conv_w, conv_b, w_down, loss_target, m_attn_norm_g, m_w_in, m_q_norm_g, m_k_norm_g, m_sgu_norm_g, m_sgu_w, m_sgu_b, m_out_norm_a_g, m_out_norm_b_g, m_w_out, m_ffn_norm_g, m_w_up, m_conv_w, m_conv_b, m_w_down, v_attn_norm_g, v_w_in, v_q_norm_g, v_k_norm_g, v_sgu_norm_g, v_sgu_w, v_sgu_b, v_out_norm_a_g, v_out_norm_b_g, v_w_out, v_ffn_norm_g, v_w_up, v_conv_w, v_conv_b, v_w_down):
    depth = w_in.shape[0]
    s, d = x.shape[1], x.shape[2]
    n_heads = (d // 2) // HEAD_DIM
    sgu_col0 = 3 * n_heads
    f2 = w_up.shape[2] * N_DEV
    ff = f2 // 2
    my_slot = 4 * lax.axis_index("x") + 2 * lax.axis_index("y") + lax.axis_index("c")

    wb = [(w_in[l].astype(BF16), w_out[l].astype(BF16), w_up[l].astype(BF16), w_down[l].astype(BF16))
          for l in range(depth)]
    groups = {"in0": [wb[0][0]], "out0": [wb[0][1], conv_w.reshape(depth * CONV_WIDTH, -1)], "up0": [wb[0][2]],
              "down0": [wb[0][3]]}
    for l in range(1, depth):
        groups[f"in{l}"] = [wb[l][0], wb[l][1]]
        groups[f"ffn{l}"] = [wb[l][2], wb[l][3]]
    token = attn_norm_g
    started = {}
    for gname, group in groups.items():
        started[gname] = _gather_start(f"gather_{gname}_start", group, token)
        token = started[gname].token

    def forward(gname, after):
        return _gather_forward(f"gather_{gname}_forward", started[gname], after)

    def finish(gname, forwarded, after):
        return _gather_finish(f"gather_{gname}_finish", forwarded, after)

    conv_b_all = conv_b.reshape(depth, 2, 1, ff)
    sgu_b_col = sgu_b[..., None]
    fwd_in = forward("in0", token)
    win_g = finish("in0", fwd_in, fwd_in.token)[0]

    xs = x[0]
    saved = []
    gathered = []
    for l in range(depth):
        g1 = attn_norm_g[l][None]
        g2 = ffn_norm_g[l][None]
        gq, gk = q_norm_g[l][None], k_norm_g[l][None]
        ga = out_norm_a_g[l][:, None, :]
        gs = sgu_norm_g[l][:, None, :]
        gb = out_norm_b_g[l][:, None, :]
        h1 = _rmsnorm_fwd("attn_norm_fwd", xs, g1)
        p = _mm_nn_blocked("in_proj", h1, win_g, F32)
        att, o, rsum = _attn_fwd(p, gq, gk, ga, n_heads)
        if l == 0:
            fwd_out = forward("out0", att)
            fwd_up = forward("up0", fwd_out.token)
        else:
            fwd_up = forward(f"ffn{l}", att)
        sg = _sgu_fwd(p, gs, sgu_w[l], sgu_b_col[l], gb, sgu_col0, fwd_up.token)
        mix = jnp.concatenate([att, sg], axis=-1)
        if l == 0:
            wout_g, cw = finish("out0", fwd_out, mix)
            cw = jnp.transpose(cw.reshape(N_DEV, depth, CONV_WIDTH, -1), (1, 2, 0, 3)).reshape(depth, CONV_WIDTH, 2, ff)
            conv_w_all = jnp.transpose(cw, (0, 2, 1, 3))
        x1 = _mm_nn_res("out_proj", mix, wout_g.reshape(d, d), xs)
        h2 = _rmsnorm_fwd("ffn_norm_fwd", x1, g2)
        if l == 0:
            wup_g = finish("up0", fwd_up, h2)[0]
            fwd_down = forward("down0", wup_g)
            up = _mm_nn_blocked("up_proj", h2, wup_g, F32, halves=True, after=fwd_down.token)
            wdown_g = finish("down0", fwd_down, up)[0]
        else:
            wup_g, wdown_g = finish(f"ffn{l}", fwd_up, h2)
            up = _mm_nn_blocked("up_proj", h2, wup_g, F32, halves=True)
        act = _conv_gate_fwd(up, conv_w_all[l], conv_b_all[l])
        saved.append((xs, h1, p, o, rsum, mix, x1, h2, up, act))
        gathered.append((win_g, wout_g, wup_g, wdown_g))
        if l + 1 < depth:
            fwd_in = forward(f"in{l + 1}", act)
            x2 = _mm_nn_res("down_proj", act, wdown_g.reshape(ff, d), x1, after=fwd_in.token)
            win_g, wout_g = finish(f"in{l + 1}", fwd_in, x2)
        else:
            x2 = _mm_nn_res("down_proj", act, wdown_g.reshape(ff, d), x1)
        xs = x2

    dx, dxb, loss_vec = _loss_head(xs, loss_target[0])
    loss = lax.psum(loss_vec[0, 0], MESH_AXES)

    exchanges = []
    small = [None] * depth
    small_names = ["attn_norm_g", "q_norm_g", "k_norm_g", "sgu_norm_g", "sgu_w", "sgu_b", "out_norm_a_g",
                   "out_norm_b_g", "ffn_norm_g", "conv_b", "conv_w"]
    for l in reversed(range(depth)):
        xs0, h1, p, o, rsum, mix, x1, h2, up, act = saved[l]
        win_g, wout_g, wup_g, wdown_g = gathered[l]
        wout_full = wout_g.reshape(d, d)
        wdown_full = wdown_g.reshape(ff, d)
        g1 = attn_norm_g[l][None]
        g2 = ffn_norm_g[l][None]
        gq, gk = q_norm_g[l][None], k_norm_g[l][None]
        ga = out_norm_a_g[l][:, None, :]
        gs = sgu_norm_g[l][:, None, :]
        gb = out_norm_b_g[l][:, None, :]
        d_wdown = _mm_tn_plain("down_proj_dw", act, dxb)
        exchanges.append((l, "down", ("w_down",), _exchange_start(f"grad_down{l}_start", [d_wdown], dx)))
        dup, d_cw, d_cb = _down_dx_conv_gate_bwd(up, conv_w_all[l], conv_b_all[l], dxb, wdown_full,
                                                 exchanges[-1][3].token)
        d_wup = _mm_tn_blocked("up_proj_dw", h2, dup, N_DEV, halves=True)
        exchanges.append((l, "up", ("w_up",), _exchange_start(f"grad_up{l}_start", [d_wup], d_cb)))
        dh2 = _mm_nt_blocked("up_proj_dx", dup, wup_g, halves=True, after=exchanges[-1][3].token)
        dx, dxb, d_g2 = _rmsnorm_bwd("ffn_norm_bwd", dh2, x1, g2, dx)
        d_wout = _mm_tn_plain("out_proj_dw", mix, dxb)
        dmix = _mm_nt_plain("out_proj_dx", dxb, wout_full)
        dq, dk, dv, d_gq, d_gk, d_ga = _attn_bwd(p, gq, gk, ga, o, rsum, dmix, n_heads)
        du, dvs, d_gs, d_sw, d_sb, d_gb = _sgu_bwd(p, gs, sgu_w[l], sgu_b_col[l], gb, dmix, sgu_col0, n_heads)
        dp = jnp.concatenate([dq, dk, dv, du, dvs], axis=-1)
        d_win = _mm_tn_blocked("in_proj_dw", h1, dp, N_DEV)
        small[l] = dict(attn_norm_g=None, q_norm_g=d_gq[0], k_norm_g=d_gk[0], sgu_norm_g=d_gs[:, 0], sgu_w=d_sw,
                        sgu_b=d_sb[..., 0], out_norm_a_g=d_ga[:, 0], out_norm_b_g=d_gb[:, 0], ffn_norm_g=d_g2[0],
                        conv_w=jnp.transpose(d_cw, (1, 0, 2)).reshape(CONV_WIDTH, f2), conv_b=d_cb.reshape(f2))
        mix_after = d_gq
        if l == 0:
            small[0]["attn_norm_g"] = jnp.zeros_like(attn_norm_g[0])
            small_g = [jnp.stack([small[k][n] for k in range(depth)]) for n in small_names]
            small_sent = _broadcast_start("grad_small_start", [_pack(small_g)], d_gk)
            mix_after = small_sent.token
        exchanges.append((l, "mix", ("w_in", "w_out"), _exchange_start(f"grad_mix{l}_start", [d_win, d_wout], mix_after)))
        dh1 = _mm_nt_blocked("in_proj_dx", dp, win_g, after=exchanges[-1][3].token)
        dx, dxb, d_g1 = _rmsnorm_bwd("attn_norm_bwd", dh1, xs0, g1, dx)
        if l > 0:
            small[l]["attn_norm_g"] = d_g1[0]
    grad_x = dx[None]
    norm0_sent = _broadcast_start("grad_norm0_start", [_pack([d_g1[0]])], dx)

    res = {}
    big = dict(w_in=(w_in, m_w_in, v_w_in), w_out=(w_out, m_w_out, v_w_out), w_up=(w_up, m_w_up, v_w_up),
               w_down=(w_down, m_w_down, v_w_down))
    after = [norm0_sent.token]
    for l, stage, names, ex in exchanges:
        landed = _exchange_finish(f"grad_{stage}{l}_finish", ex, after)
        after = []
        for name, parts in zip(names, landed):
            w, m, v = big[name]
            res[name] = _adamw(f"adamw_{name}", w, m, v, parts, l, res.get(name))
            after.append(res[name][0])
    small_all = _exchange_finish("grad_small_finish", small_sent, after)[0]
    norm0_all = _exchange_finish("grad_norm0_finish", norm0_sent, small_all)[0]
    small_sum = _unpack(_sum_slots("small_grad_sum", small_all), small_g)
    g_small = dict(zip(small_names, small_sum))
    g_norm0 = _sum_slots("norm0_grad_sum", norm0_all).reshape(-1)[:d]
    layer_of_row = lax.broadcasted_iota(jnp.int32, (depth, d), 0)
    g_small["attn_norm_g"] = jnp.where(layer_of_row == 0, g_norm0[None], g_small["attn_norm_g"])
    cwn = conv_w.shape[2]
    g_small["conv_w"] = lax.dynamic_slice_in_dim(g_small["conv_w"], my_slot * cwn, cwn, axis=2)
    small_w = dict(attn_norm_g=(attn_norm_g, m_attn_norm_g, v_attn_norm_g), q_norm_g=(q_norm_g, m_q_norm_g, v_q_norm_g),
                   k_norm_g=(k_norm_g, m_k_norm_g, v_k_norm_g), sgu_norm_g=(sgu_norm_g, m_sgu_norm_g, v_sgu_norm_g),
                   sgu_w=(sgu_w, m_sgu_w, v_sgu_w), sgu_b=(sgu_b, m_sgu_b, v_sgu_b),
                   out_norm_a_g=(out_norm_a_g, m_out_norm_a_g, v_out_norm_a_g),
                   out_norm_b_g=(out_norm_b_g, m_out_norm_b_g, v_out_norm_b_g),
                   ffn_norm_g=(ffn_norm_g, m_ffn_norm_g, v_ffn_norm_g), conv_b=(conv_b, m_conv_b, v_conv_b),
                   conv_w=(conv_w, m_conv_w, v_conv_w))
    like = [small_w[n][0] for n in small_names]
    pw = _pack([small_w[n][0] for n in small_names])
    pm = _pack([small_w[n][1] for n in small_names])
    pv = _pack([small_w[n][2] for n in small_names])
    pg = _pack([g_small[n].reshape(small_w[n][0].shape) for n in small_names])
    pd, pnm, pnv = _adamw_small(pw, pg, pm, pv)
    for n, dlt, nm, nv in zip(small_names, _unpack(pd, like), _unpack(pnm, like), _unpack(pnv, like)):
        res[n] = (g_small[n].reshape(small_w[n][0].shape), dlt, nm, nv)

    order = ["attn_norm_g", "w_in", "q_norm_g", "k_norm_g", "sgu_norm_g", "sgu_w", "sgu_b", "out_norm_a_g",
             "out_norm_b_g", "w_out", "ffn_norm_g", "w_up", "conv_w", "conv_b", "w_down"]
    outs = [loss, grad_x]
    for field in range(4):
        outs += [res[n][field] for n in order]
    return tuple(outs)
```

```python
import functools

import jax
import jax.numpy as jnp
from jax import lax
from jax.experimental import pallas as pl
from jax.experimental.pallas import tpu as pltpu

F32 = jnp.float32
BF16 = jnp.bfloat16
EPS = 1e-6
HEAD_DIM = 128
TILE = 128
ATTN_VMEM_MB = 58
ATTN_HEADS_PER_STEP = 4
N_GROUPS = 8
CONV_WIDTH = 3
N_DEV = 8
MESH_AXES = ("x", "y", "c")
MIB = 1024 * 1024

ADAM_LR = 0.001
ADAM_B1 = 0.9
ADAM_B2 = 0.999
ADAM_EPS = 1e-08
ADAM_WD = 0.01
ADAM_STEP = 10
ADAMW_TILE_ELEMS = 160 * 1024

NT_DIMS = (((1,), (1,)), ((), ()))
NN_DIMS = (((1,), (0,)), ((), ()))
TN_DIMS = (((0,), (0,)), ((), ()))


def _cparams(sem, vmem_mb=48):
    return pltpu.CompilerParams(dimension_semantics=sem, vmem_limit_bytes=vmem_mb * MIB)


def _pick(n, cands):
    for c in cands:
        if n % c == 0:
            return c
    return n


def _mm(name, grid, ins, in_specs, out_shape, out_spec, dims, has_res=False, parts=None, vmem_mb=56, after=None):
    n_in = 2 + has_res + (after is not None)
    if after is not None:
        ins = tuple(ins) + (after,)
        in_specs = list(in_specs) + [pl.BlockSpec(after.shape, lambda *_: (0, 0))]

    def body(*refs):
        a_ref, b_ref = refs[:2]
        o_ref = refs[n_in]
        if parts is None:
            acc = lax.dot_general(a_ref[...], b_ref[...], dims, preferred_element_type=F32)
        else:
            acc = None
            for part in parts:
                a, b = part(a_ref, b_ref)
                prod = lax.dot_general(a, b, dims, preferred_element_type=F32)
                acc = prod if acc is None else acc + prod
        if has_res:
            acc = acc + refs[2][...]
        o_ref[...] = acc.astype(o_ref.dtype)

    return pl.pallas_call(
        body, name=name, grid=grid, in_specs=in_specs, out_specs=out_spec, out_shape=out_shape,
        compiler_params=_cparams(("parallel",) * len(grid), vmem_mb),
    )(*ins)


def _rows_for(m, row_bytes, budget):
    return _pick(m, tuple(t for t in (2048, 1024, 512, 256, 128) if t * row_bytes <= budget))


def _mm_nn_blocked(name, a, wb, out_dtype, halves=False, after=None):
    m, k = a.shape
    nb, _, bn = wb.shape
    tm = _rows_for(m, bn * jnp.dtype(out_dtype).itemsize, 6 * MIB)
    a_spec = pl.BlockSpec((tm, k), lambda j, i: (i, 0))
    b_spec = pl.BlockSpec((None, k, bn), lambda j, i: (j, 0, 0))
    if halves:
        hb = nb // 2
        out_shape = jax.ShapeDtypeStruct((2, m, hb * bn), out_dtype)
        o_spec = pl.BlockSpec((None, tm, bn), lambda j, i: (j // hb, i, j % hb))
    else:
        out_shape = jax.ShapeDtypeStruct((m, nb * bn), out_dtype)
        o_spec = pl.BlockSpec((tm, bn), lambda j, i: (i, j))
    return _mm(name, (nb, m // tm), (a, wb), [a_spec, b_spec], out_shape, o_spec, NN_DIMS, after=after)


def _mm_nn_res(name, a, w, res, after=None):
    m, k = a.shape
    n = w.shape[1]
    tm = _pick(m, (512, 256, 128))
    tn = _rows_for(n, k * 2, 12 * MIB)
    a_spec = pl.BlockSpec((tm, k), lambda j, i: (i, 0))
    b_spec = pl.BlockSpec((k, tn), lambda j, i: (0, j))
    r_spec = pl.BlockSpec((tm, tn), lambda j, i: (i, j))
    o_spec = pl.BlockSpec((tm, tn), lambda j, i: (i, j))
    return _mm(name, (n // tn, m // tm), (a, w, res), [a_spec, b_spec, r_spec], jax.ShapeDtypeStruct((m, n), F32),
               o_spec, NN_DIMS, has_res=True, after=after)


def _mm_nt_blocked(name, dy, wb, halves=False, after=None):
    nb, n, bn = wb.shape
    m = dy.shape[-2]
    tm = _pick(m, (512, 256, 128))
    tn = _rows_for(n, nb * bn * 2, 12 * MIB)
    if halves:
        hb = nb // 2
        a_spec = pl.BlockSpec((2, tm, hb * bn), lambda j, i: (0, i, 0))
        a_part = lambda kk: (lambda a_ref: a_ref[kk // hb, :, (kk % hb) * bn:(kk % hb + 1) * bn])
    else:
        a_spec = pl.BlockSpec((tm, nb * bn), lambda j, i: (i, 0))
        a_part = lambda kk: (lambda a_ref: a_ref[:, kk * bn:(kk + 1) * bn])
    parts = [(lambda a_ref, b_ref, kk=kk, sel=a_part(kk): (sel(a_ref), b_ref[kk])) for kk in range(nb)]
    b_spec = pl.BlockSpec((nb, tn, bn), lambda j, i: (0, j, 0))
    o_spec = pl.BlockSpec((tm, tn), lambda j, i: (i, j))
    return _mm(name, (n // tn, m // tm), (dy, wb), [a_spec, b_spec], jax.ShapeDtypeStruct((m, n), F32), o_spec,
               NT_DIMS, parts=parts, after=after)


def _mm_nt_plain(name, dy, w, out_dtype=F32):
    m, k = dy.shape
    n = w.shape[0]
    tm = _rows_for(m, k * 2, 8 * MIB)
    tn = _pick(n, (512, 256, 128))
    a_spec = pl.BlockSpec((tm, k), lambda j, i: (i, 0))
    b_spec = pl.BlockSpec((tn, k), lambda j, i: (j, 0))
    o_spec = pl.BlockSpec((tm, tn), lambda j, i: (i, j))
    return _mm(name, (n // tn, m // tm), (dy, w), [a_spec, b_spec], jax.ShapeDtypeStruct((m, n), out_dtype), o_spec,
               NT_DIMS)


def _mm_tn_blocked(name, a, dy, nb, halves=False):
    s, k1 = a.shape
    bn = (dy.shape[-1] * (2 if halves else 1)) // nb
    tm = _rows_for(k1, bn * 2, 6 * MIB)
    a_spec = pl.BlockSpec((s, tm), lambda j, i: (0, i))
    if halves:
        hb = nb // 2
        b_spec = pl.BlockSpec((None, s, bn), lambda j, i: (j // hb, 0, j % hb))
    else:
        b_spec = pl.BlockSpec((s, bn), lambda j, i: (0, j))
    o_spec = pl.BlockSpec((None, tm, bn), lambda j, i: (j, i, 0))
    return _mm(name, (nb, k1 // tm), (a, dy), [a_spec, b_spec], jax.ShapeDtypeStruct((nb, k1, bn), BF16), o_spec,
               TN_DIMS)


def _mm_tn_plain(name, a, dy):
    s, k1 = a.shape
    n = dy.shape[1]
    tm = _pick(k1, (512, 256, 128))
    tn = _rows_for(n, s * 2, 8 * MIB)
    a_spec = pl.BlockSpec((s, tm), lambda i, j: (0, i))
    b_spec = pl.BlockSpec((s, tn), lambda i, j: (0, j))
    o_spec = pl.BlockSpec((tm, tn), lambda i, j: (i, j))
    return _mm(name, (k1 // tm, n // tn), (a, dy), [a_spec, b_spec], jax.ShapeDtypeStruct((k1, n), BF16), o_spec,
               TN_DIMS)


def _rstd(x):
    return lax.rsqrt(jnp.mean(x * x, axis=-1, keepdims=True) + EPS)


def _norm_bwd(dy, xhat, r, g):
    dxhat = dy * g
    return r * (dxhat - xhat * jnp.mean(dxhat * xhat, axis=-1, keepdims=True))


def _rmsnorm_fwd(name, x, g):
    s, d = x.shape
    tr = _pick(s, (256, 128))

    def body(x_ref, g_ref, h_ref):
        xv = x_ref[...]
        h_ref[...] = (xv * _rstd(xv) * g_ref[...]).astype(BF16)

    return pl.pallas_call(
        body, name=name, grid=(s // tr,),
        in_specs=[pl.BlockSpec((tr, d), lambda i: (i, 0)), pl.BlockSpec((1, d), lambda i: (0, 0))],
        out_specs=pl.BlockSpec((tr, d), lambda i: (i, 0)),
        out_shape=jax.ShapeDtypeStruct((s, d), BF16), compiler_params=_cparams(("parallel",)),
    )(x, g)


def _rmsnorm_bwd(name, dh, x, g, dres):
    s, d = x.shape
    tr = _pick(s, (256, 128))

    def body(dh_ref, x_ref, g_ref, dres_ref, dx_ref, dxb_ref, dg_ref):
        xv = x_ref[...]
        r = _rstd(xv)
        xhat = xv * r
        dhv = dh_ref[...]
        dx = dres_ref[...] + _norm_bwd(dhv, xhat, r, g_ref[...])
        dx_ref[...] = dx
        dxb_ref[...] = dx.astype(BF16)
        part = jnp.sum(dhv * xhat, axis=0, keepdims=True)

        @pl.when(pl.program_id(0) == 0)
        def _():
            dg_ref[...] = part

        @pl.when(pl.program_id(0) > 0)
        def _():
            dg_ref[...] += part

    row = pl.BlockSpec((tr, d), lambda i: (i, 0))
    vec = pl.BlockSpec((1, d), lambda i: (0, 0))
    return pl.pallas_call(
        body, name=name, grid=(s // tr,), in_specs=[row, row, vec, row], out_specs=(row, row, vec),
        out_shape=(jax.ShapeDtypeStruct((s, d), F32), jax.ShapeDtypeStruct((s, d), BF16),
                   jax.ShapeDtypeStruct((1, d), F32)),
        compiler_params=_cparams(("arbitrary",)),
    )(dh, x, g, dres)


def _loss_head(y, target):
    s, d = y.shape
    tr = _pick(s, (256, 128))

    def body(y_ref, t_ref, dy_ref, dyb_ref, loss_ref):
        err = y_ref[...] - t_ref[...]
        dy = err * (1.0 / d)
        dy_ref[...] = dy
        dyb_ref[...] = dy.astype(BF16)
        part = 0.5 * jnp.sum(jnp.mean(err * err, axis=-1, keepdims=True), axis=0, keepdims=True)
        part = jnp.broadcast_to(part, (1, 128))

        @pl.when(pl.program_id(0) == 0)
        def _():
            loss_ref[...] = part

        @pl.when(pl.program_id(0) > 0)
        def _():
            loss_ref[...] += part

    row = pl.BlockSpec((tr, d), lambda i: (i, 0))
    return pl.pallas_call(
        body, name="loss_head", grid=(s // tr,), in_specs=[row, row],
        out_specs=(row, row, pl.BlockSpec((1, 128), lambda i: (0, 0))),
        out_shape=(jax.ShapeDtypeStruct((s, d), F32), jax.ShapeDtypeStruct((s, d), BF16),
                   jax.ShapeDtypeStruct((1, 128), F32)),
        compiler_params=_cparams(("arbitrary",)),
    )(y, target)


def _split_dot(x, tri):
    hi = x.astype(BF16)
    lo = (x - hi.astype(F32)).astype(BF16)
    return (jnp.dot(hi, tri, preferred_element_type=F32) + jnp.dot(lo, tri, preferred_element_type=F32))


def _tile_iotas():
    row = lax.broadcasted_iota(jnp.int32, (TILE, TILE), 0)
    col = lax.broadcasted_iota(jnp.int32, (TILE, TILE), 1)
    return row, col


def _sb_logits(qi, kb, mask):
    z = lax.dot_general(qi, kb, NT_DIMS, preferred_element_type=F32) * (HEAD_DIM ** -0.5)
    sp = jnp.log1p(jnp.exp(-jnp.abs(z)))
    lb = jnp.minimum(z, 0.0) - sp
    l1m = -jnp.maximum(z, 0.0) - sp
    if mask is not None:
        l1m = jnp.where(mask, l1m, 0.0)
    return lb, l1m


def _attn_fwd(p, gq, gk, ga, n_heads):
    s = p.shape[0]
    nq = s // TILE

    hp = ATTN_HEADS_PER_STEP
    wd = hp * HEAD_DIM

    def body(q_ref, k_ref, v_ref, gq_ref, gk_ref, ga_ref, att_ref, o_ref, r_ref, qn_s, kn_s, vb_s):
        heads = [slice(hh * HEAD_DIM, (hh + 1) * HEAD_DIM) for hh in range(hp)]
        for hd in heads:
            qv = q_ref[:, hd]
            qn_s[:, hd] = (qv * _rstd(qv) * gq_ref[...]).astype(BF16)
            kv = k_ref[:, hd]
            kn_s[:, hd] = (kv * _rstd(kv) * gk_ref[...]).astype(BF16)
        vb_s[...] = v_ref[...].astype(BF16)
        row, col = _tile_iotas()
        causal = col < row
        upper_ones = jnp.concatenate([(row > col).astype(BF16), jnp.ones((TILE, TILE), BF16)], axis=1)

        def tiles(rows, key_blocks, states, mask):
            chains = [(hi, hd, keys) for hi, hd in enumerate(heads) for keys in key_blocks]
            logits = [_sb_logits(qn_s[rows, hd], kn_s[keys, hd], mask) for _, hd, keys in chains]
            sums = [_split_dot(l1m, upper_ones) for _, l1m in logits]
            carry = [c for _, c in states]
            probs = []
            for (hi, _, _), (lb, _), sm in zip(chains, logits, sums):
                a = jnp.exp(lb + sm[:, :TILE] + carry[hi])
                carry[hi] = carry[hi] + sm[:, TILE:]
                probs.append((a if mask is None else jnp.where(mask, a, 0.0)).astype(BF16))
            outs = [jnp.dot(a, vb_s[keys, hd], preferred_element_type=F32) for a, (_, hd, keys) in zip(probs, chains)]
            acc = [o_acc for o_acc, _ in states]
            for (hi, _, _), o in zip(chains, outs):
                acc[hi] = acc[hi] + o
            return tuple(zip(acc, carry))

        def key_block(b):
            return pl.ds(pl.multiple_of(b * TILE, TILE), TILE)

        def qblock(i, _):
            rows = pl.ds(pl.multiple_of(i * TILE, TILE), TILE)
            zero = jnp.zeros((TILE, HEAD_DIM), F32)
            states = tiles(rows, [rows], tuple((zero, zero) for _ in heads), causal)
            states = lax.cond(i % 2 == 1, lambda st: tiles(rows, [key_block(i - 1)], st, None), lambda st: st, states)
            top = i - i % 2

            def kblocks(jj, states):
                return tiles(rows, [key_block(top - 1 - 2 * jj), key_block(top - 2 - 2 * jj)], states, None)

            states = lax.fori_loop(0, i // 2, kblocks, states)
            for hh, (hd, (o_acc, c)) in enumerate(zip(heads, states)):
                o_ref[rows, hd] = o_acc
                r_ref[rows, hd] = c
                att_ref[rows, hd] = (o_acc * _rstd(o_acc) * ga_ref[hh]).astype(BF16)
            return 0

        lax.fori_loop(0, nq, qblock, 0)

    col_blk = lambda off: pl.BlockSpec((s, wd), lambda h: (0, off + h))
    vec = pl.BlockSpec((1, HEAD_DIM), lambda h: (0, 0))
    hvec = pl.BlockSpec((hp, 1, HEAD_DIM), lambda h: (h, 0, 0))
    out = pl.BlockSpec((s, wd), lambda h: (0, h))
    w = n_heads * HEAD_DIM
    steps = n_heads // hp
    return pl.pallas_call(
        body, name="attn_fwd", grid=(steps,),
        in_specs=[col_blk(0), col_blk(steps), col_blk(2 * steps), vec, vec, hvec],
        out_specs=(out, out, out),
        out_shape=(jax.ShapeDtypeStruct((s, w), BF16), jax.ShapeDtypeStruct((s, w), F32),
                   jax.ShapeDtypeStruct((s, w), F32)),
        scratch_shapes=[pltpu.VMEM((s, wd), BF16)] * 3,
        compiler_params=_cparams(("parallel",), ATTN_VMEM_MB),
    )(p, p, p, gq, gk, ga)


def _attn_bwd(p, gq, gk, ga, o, rsum, dmix, n_heads):
    s = p.shape[0]
    nq = s // TILE

    hp = ATTN_HEADS_PER_STEP
    wd = hp * HEAD_DIM
    scale = HEAD_DIM ** -0.5

    def body(q_ref, k_ref, v_ref, gq_ref, gk_ref, ga_ref, o_ref, r_ref, dm_ref,
             dq_ref, dk_ref, dv_ref, dgq_ref, dgk_ref, dga_ref,
             qn_s, kn_s, vb_s, do_s, dqn_s, dkn_s, dv_s):
        step = pl.program_id(0)
        gqv, gkv = gq_ref[...], gk_ref[...]
        heads = [slice(hh * HEAD_DIM, (hh + 1) * HEAD_DIM) for hh in range(hp)]
        for hh, hd in enumerate(heads):
            qv = q_ref[:, hd]
            qn_s[:, hd] = (qv * _rstd(qv) * gqv).astype(BF16)
            kv = k_ref[:, hd]
            kn_s[:, hd] = (kv * _rstd(kv) * gkv).astype(BF16)
            ov = o_ref[:, hd]
            ro = _rstd(ov)
            ohat = ov * ro
            dm = dm_ref[:, hd]
            dga_ref[hh] = jnp.sum(dm * ohat, axis=0, keepdims=True)
            do_s[:, hd] = _norm_bwd(dm, ohat, ro, ga_ref[hh]).astype(BF16)
        vb_s[...] = v_ref[...].astype(BF16)
        dkn_s[...] = jnp.zeros_like(dkn_s)
        dv_s[...] = jnp.zeros_like(dv_s)
        row, col = _tile_iotas()
        causal = col < row
        ones = jnp.ones((TILE, TILE), BF16)
        incl_ones = jnp.concatenate([(row <= col).astype(BF16), ones], axis=1)
        excl_ones = jnp.concatenate([(row < col).astype(BF16), ones], axis=1)

        def tiles(rows, key_blocks, states, mask):
            chains = [(hi, hd, keys) for hi, hd in enumerate(heads) for keys in key_blocks]
            qis = [qn_s[rows, hd] for hd in heads]
            dois = [do_s[rows, hd] for hd in heads]
            logits = [_sb_logits(qis[hi], kn_s[keys, hd], mask) for hi, hd, keys in chains]
            sums = [_split_dot(l1m, incl_ones) for _, l1m in logits]
            das = [lax.dot_general(dois[hi], vb_s[keys, hd], NT_DIMS, preferred_element_type=F32)
                   for hi, hd, keys in chains]
            pfx = [st[1] for st in states]
            probs, dss = [], []
            for (hi, hd, _), (lb, _), sm, da in zip(chains, logits, sums, das):
                a = jnp.exp(lb + (r_ref[rows, hd] - pfx[hi] - sm[:, :TILE]))
                pfx[hi] = pfx[hi] + sm[:, TILE:]
                a = a if mask is None else jnp.where(mask, a, 0.0)
                probs.append(a.astype(BF16))
                dss.append(da * a)
            dsums = [_split_dot(ds, excl_ones) for ds in dss]
            pc = [st[2] for st in states]
            dzs = []
            for (hi, _, _), (lb, _), ds, dsm in zip(chains, logits, dss, dsums):
                dl1m = pc[hi] + dsm[:, :TILE]
                pc[hi] = pc[hi] + dsm[:, TILE:]
                dl1m = dl1m if mask is None else jnp.where(mask, dl1m, 0.0)
                beta = jnp.exp(lb)
                dzs.append(((ds * (1.0 - beta) - dl1m * beta) * scale).astype(BF16))
            dqs = [jnp.dot(dz, kn_s[keys, hd], preferred_element_type=F32) for dz, (_, hd, keys) in zip(dzs, chains)]
            for dz, a, (hi, hd, keys) in zip(dzs, probs, chains):
                dkn_s[keys, hd] += lax.dot_general(dz, qis[hi], TN_DIMS, preferred_element_type=F32)
                dv_s[keys, hd] += lax.dot_general(a, dois[hi], TN_DIMS, preferred_element_type=F32)
            dq_acc = [st[0] for st in states]
            for (hi, _, _), dq in zip(chains, dqs):
                dq_acc[hi] = dq_acc[hi] + dq
            return tuple(zip(dq_acc, pfx, pc))

        def key_block(b):
            return pl.ds(pl.multiple_of(b * TILE, TILE), TILE)

        def qblock(i, _):
            rows = pl.ds(pl.multiple_of(i * TILE, TILE), TILE)
            zero = jnp.zeros((TILE, HEAD_DIM), F32)

            def kblocks(jj, states):
                return tiles(rows, [key_block(2 * jj), key_block(2 * jj + 1)], states, None)

            states = lax.fori_loop(0, i // 2, kblocks, tuple((zero, zero, zero) for _ in heads))
            states = lax.cond(i % 2 == 1, lambda st: tiles(rows, [key_block(i - 1)], st, None), lambda st: st, states)
            states = tiles(rows, [rows], states, causal)
            for hd, (dq_acc, _, _) in zip(heads, states):
                dqn_s[rows, hd] = dq_acc
            return 0

        lax.fori_loop(0, nq, qblock, 0)

        def norm_in_bwd(x_ref, g, dn_s, dx_ref, dg_ref):
            part = jnp.zeros((1, HEAD_DIM), F32)
            for hd in heads:
                xv = x_ref[:, hd]
                r = _rstd(xv)
                xhat = xv * r
                dn = dn_s[:, hd]
                dx_ref[:, hd] = _norm_bwd(dn, xhat, r, g).astype(BF16)
                part = part + jnp.sum(dn * xhat, axis=0, keepdims=True)

            @pl.when(step == 0)
            def _():
                dg_ref[...] = part

            @pl.when(step > 0)
            def _():
                dg_ref[...] += part

        norm_in_bwd(q_ref, gqv, dqn_s, dq_ref, dgq_ref)
        norm_in_bwd(k_ref, gkv, dkn_s, dk_ref, dgk_ref)
        dv_ref[...] = dv_s[...].astype(BF16)

    once = pl.Buffered(1)
    steps = n_heads // hp
    col_blk = lambda off: pl.BlockSpec((s, wd), lambda h: (0, off + h), pipeline_mode=once)
    vec = pl.BlockSpec((1, HEAD_DIM), lambda h: (0, 0))
    hvec = pl.BlockSpec((hp, 1, HEAD_DIM), lambda h: (h, 0, 0))
    blk = pl.BlockSpec((s, wd), lambda h: (0, h), pipeline_mode=once)
    w = n_heads * HEAD_DIM
    big = jax.ShapeDtypeStruct((s, w), BF16)
    return pl.pallas_call(
        body, name="attn_bwd", grid=(steps,),
        in_specs=[col_blk(0), col_blk(steps), col_blk(2 * steps), vec, vec, hvec, blk, blk, blk],
        out_specs=(blk, blk, blk, vec, vec, hvec),
        out_shape=(big, big, big, jax.ShapeDtypeStruct((1, HEAD_DIM), F32), jax.ShapeDtypeStruct((1, HEAD_DIM), F32),
                   jax.ShapeDtypeStruct((n_heads, 1, HEAD_DIM), F32)),
        scratch_shapes=[pltpu.VMEM((s, wd), BF16)] * 4 + [pltpu.VMEM((s, wd), F32)] * 3,
        compiler_params=_cparams(("arbitrary",), ATTN_VMEM_MB),
    )(p, p, p, gq, gk, ga, o, rsum, dmix)


_INV_SQRT2 = 0.7071067811865476
_INV_SQRT_2PI = 0.3989422804014327


def _gelu(x):
    return 0.5 * x * (1.0 + lax.erf(x * _INV_SQRT2))


def _gelu_grad(x):
    return 0.5 * (1.0 + lax.erf(x * _INV_SQRT2)) + x * (_INV_SQRT_2PI * jnp.exp(-0.5 * x * x))


def _sgu_fwd(p, gs, w_s, b_s, gb, col0, after):
    s = p.shape[0]
    n_chunks = s // TILE
    per_trip = _pick(n_chunks, (4, 2, 1))

    def body(u_ref, v_ref, gs_ref, w_ref, b_ref, gb_ref, _, out_ref, vs_s):
        vg = _gelu(v_ref[...])
        vs_s[...] = (vg * _rstd(vg) * gs_ref[...]).astype(BF16)
        row, col = _tile_iotas()
        wt = jnp.where(col <= row, w_ref[...], 0.0).astype(BF16)
        bcol = b_ref[...]
        gbv = gb_ref[...]

        def chunks(c, _):
            rows = [pl.ds(pl.multiple_of((c * per_trip + k) * TILE, TILE), TILE) for k in range(per_trip)]
            mixed = [jnp.dot(wt, vs_s[r, :], preferred_element_type=F32) + bcol for r in rows]
            sgs = [_gelu(u_ref[r, :]) * mx for r, mx in zip(rows, mixed)]
            for r, sg in zip(rows, sgs):
                out_ref[r, :] = (sg * _rstd(sg) * gbv).astype(BF16)
            return 0

        lax.fori_loop(0, n_chunks // per_trip, chunks, 0)

    col_blk = lambda off: pl.BlockSpec((s, HEAD_DIM), lambda g: (0, off + g))
    gvec = pl.BlockSpec((None, 1, HEAD_DIM), lambda g: (g, 0, 0))
    return pl.pallas_call(
        body, name="sgu_fwd", grid=(N_GROUPS,),
        in_specs=[col_blk(col0), col_blk(col0 + N_GROUPS), gvec,
                  pl.BlockSpec((None, TILE, TILE), lambda g: (g, 0, 0)),
                  pl.BlockSpec((None, TILE, 1), lambda g: (g, 0, 0)), gvec,
                  pl.BlockSpec(after.shape, lambda g: (0, 0))],
        out_specs=pl.BlockSpec((s, HEAD_DIM), lambda g: (0, g)),
        out_shape=jax.ShapeDtypeStruct((s, N_GROUPS * HEAD_DIM), BF16),
        scratch_shapes=[pltpu.VMEM((s, HEAD_DIM), BF16)],
        compiler_params=_cparams(("parallel",)),
    )(p, p, gs, w_s, b_s, gb, after)


def _sgu_bwd(p, gs, w_s, b_s, gb, dmix, col0, dm_col0):
    s = p.shape[0]
    n_chunks = s // TILE
    per_trip = _pick(n_chunks, (4, 2, 1))

    def body(u_ref, v_ref, gs_ref, w_ref, b_ref, gb_ref, dm_ref,
             du_ref, dv_ref, dgs_ref, dw_ref, db_ref, dgb_ref, vs_s, dvs_s):
        gsv = gs_ref[...]
        gbv = gb_ref[...]
        vg = _gelu(v_ref[...])
        vs_s[...] = (vg * _rstd(vg) * gsv).astype(BF16)
        row, col = _tile_iotas()
        causal = col <= row
        wt = jnp.where(causal, w_ref[...], 0.0).astype(BF16)
        bcol = b_ref[...]

        def chunks(c, carry):
            dw_acc, db_acc, dgb_acc = carry
            rows = [pl.ds(pl.multiple_of((c * per_trip + k) * TILE, TILE), TILE) for k in range(per_trip)]
            vss = [vs_s[r, :] for r in rows]
            mixed = [jnp.dot(wt, vs, preferred_element_type=F32) + bcol for vs in vss]
            dmbs = []
            for r, mx in zip(rows, mixed):
                u_pre = u_ref[r, :]
                u = _gelu(u_pre)
                sg = u * mx
                rs = _rstd(sg)
                sghat = sg * rs
                dm = dm_ref[r, :]
                dsg = _norm_bwd(dm, sghat, rs, gbv)
                dgb_acc = dgb_acc + jnp.sum(dm * sghat, axis=0, keepdims=True)
                du_ref[r, :] = (dsg * mx * _gelu_grad(u_pre)).astype(BF16)
                dmixed = dsg * u
                db_acc = db_acc + jnp.sum(dmixed, axis=1, keepdims=True)
                dmbs.append(dmixed.astype(BF16))
            for dmb, vs in zip(dmbs, vss):
                dw_acc = dw_acc + lax.dot_general(dmb, vs, NT_DIMS, preferred_element_type=F32)
            for r, dmb in zip(rows, dmbs):
                dvs_s[r, :] = lax.dot_general(wt, dmb, TN_DIMS, preferred_element_type=F32)
            return dw_acc, db_acc, dgb_acc

        dw_acc, db_acc, dgb_acc = lax.fori_loop(
            0, n_chunks // per_trip, chunks,
            (jnp.zeros((TILE, TILE), F32), jnp.zeros((TILE, 1), F32), jnp.zeros((1, HEAD_DIM), F32)))
        dw_ref[...] = jnp.where(causal, dw_acc, 0.0)
        db_ref[...] = db_acc
        dgb_ref[...] = dgb_acc
        v_pre = v_ref[...]
        vg = _gelu(v_pre)
        rv = _rstd(vg)
        vhat = vg * rv
        dvs = dvs_s[...]
        dgs_ref[...] = jnp.sum(dvs * vhat, axis=0, keepdims=True)
        dv_ref[...] = (_norm_bwd(dvs, vhat, rv, gsv) * _gelu_grad(v_pre)).astype(BF16)

    col_blk = lambda off: pl.BlockSpec((s, HEAD_DIM), lambda g: (0, off + g))
    gvec = pl.BlockSpec((None, 1, HEAD_DIM), lambda g: (g, 0, 0))
    wspec = pl.BlockSpec((None, TILE, TILE), lambda g: (g, 0, 0))
    bspec = pl.BlockSpec((None, TILE, 1), lambda g: (g, 0, 0))
    blk = pl.BlockSpec((s, HEAD_DIM), lambda g: (0, g))
    big = jax.ShapeDtypeStruct((s, N_GROUPS * HEAD_DIM), BF16)
    gshape = jax.ShapeDtypeStruct((N_GROUPS, 1, HEAD_DIM), F32)
    return pl.pallas_call(
        body, name="sgu_bwd", grid=(N_GROUPS,),
        in_specs=[col_blk(col0), col_blk(col0 + N_GROUPS), gvec, wspec, bspec, gvec, col_blk(dm_col0)],
        out_specs=(blk, blk, gvec, wspec, bspec, gvec),
        out_shape=(big, big, gshape, jax.ShapeDtypeStruct((N_GROUPS, TILE, TILE), F32),
                   jax.ShapeDtypeStruct((N_GROUPS, TILE, 1), F32), gshape),
        scratch_shapes=[pltpu.VMEM((s, HEAD_DIM), BF16), pltpu.VMEM((s, HEAD_DIM), F32)],
        compiler_params=_cparams(("parallel",)),
    )(p, p, gs, w_s, b_s, gb, dmix)


SUBLANES = 8


def _shift_down(x, n):
    rolled = pltpu.roll(x, n, 0)
    edge = lax.broadcasted_iota(jnp.int32, (SUBLANES, x.shape[1]), 0)
    return jnp.concatenate([jnp.where(edge >= n, rolled[:SUBLANES], 0.0), rolled[SUBLANES:]], axis=0)


def _shift_up(x, n):
    s = x.shape[0]
    rolled = pltpu.roll(x, s - n, 0)
    edge = lax.broadcasted_iota(jnp.int32, (SUBLANES, x.shape[1]), 0)
    return jnp.concatenate([rolled[:s - SUBLANES], jnp.where(edge < SUBLANES - n, rolled[s - SUBLANES:], 0.0)], axis=0)


def _conv(x, w, b):
    x1, x2 = _shift_down(x, 1), _shift_down(x, 2)
    return b + w[0:1, :] * x2 + w[1:2, :] * x1 + w[2:3, :] * x, x1, x2


def _conv_specs(s, tn):
    xspec = pl.BlockSpec((2, s, tn), lambda j: (0, 0, j))
    wspec = pl.BlockSpec((2, CONV_WIDTH, tn), lambda j: (0, 0, j))
    bspec = pl.BlockSpec((2, 1, tn), lambda j: (0, 0, j))
    return xspec, wspec, bspec


def _conv_gate_fwd(up, cw, cb):
    _, s, f = up.shape
    tn = _pick(f, (256, 128))

    def body(x_ref, w_ref, b_ref, act_ref):
        gate = _conv(x_ref[0], w_ref[0], b_ref[0])[0]
        val = _conv(x_ref[1], w_ref[1], b_ref[1])[0]
        act_ref[...] = (gate * jax.nn.sigmoid(gate) * val).astype(BF16)

    xspec, wspec, bspec = _conv_specs(s, tn)
    return pl.pallas_call(
        body, name="conv_gate_fwd", grid=(f // tn,), in_specs=[xspec, wspec, bspec],
        out_specs=pl.BlockSpec((s, tn), lambda j: (0, j)), out_shape=jax.ShapeDtypeStruct((s, f), BF16),
        compiler_params=_cparams(("parallel",)),
    )(up, cw, cb)


def _down_dx_conv_gate_bwd(up, cw, cb, dy, wdown, after):
    _, s, f = up.shape
    d = dy.shape[1]
    tn = _pick(f, (256, 128))

    def body(x_ref, w_ref, b_ref, dy_ref, wd_ref, _, dx_ref, dw_ref, db_ref):
        da = lax.dot_general(dy_ref[...], wd_ref[...], NT_DIMS, preferred_element_type=F32)
        xg, xv = x_ref[0], x_ref[1]
        wg, wv = w_ref[0], w_ref[1]
        gate, xg1, xg2 = _conv(xg, wg, b_ref[0])
        val, xv1, xv2 = _conv(xv, wv, b_ref[1])
        sig = jax.nn.sigmoid(gate)
        dval = da * (gate * sig)
        dgate = da * val * (sig * (1.0 + gate * (1.0 - sig)))
        for half, (x, x1, x2, w, dz) in enumerate(((xg, xg1, xg2, wg, dgate), (xv, xv1, xv2, wv, dval))):
            dx_ref[half] = (w[2:3, :] * dz + w[1:2, :] * _shift_up(dz, 1) + w[0:1, :] * _shift_up(dz, 2)).astype(BF16)
            dw_ref[half, 0:1, :] = jnp.sum(dz * x2, axis=0, keepdims=True)
            dw_ref[half, 1:2, :] = jnp.sum(dz * x1, axis=0, keepdims=True)
            dw_ref[half, 2:3, :] = jnp.sum(dz * x, axis=0, keepdims=True)
            db_ref[half] = jnp.sum(dz, axis=0, keepdims=True)

    xspec, wspec, bspec = _conv_specs(s, tn)
    return pl.pallas_call(
        body, name="down_dx_conv_gate_bwd", grid=(f // tn,),
        in_specs=[xspec, wspec, bspec, pl.BlockSpec((s, d), lambda j: (0, 0)), pl.BlockSpec((tn, d), lambda j: (j, 0)),
                  pl.BlockSpec(after.shape, lambda j: (0, 0))],
        out_specs=(xspec, wspec, bspec),
        out_shape=(jax.ShapeDtypeStruct((2, s, f), BF16), jax.ShapeDtypeStruct((2, CONV_WIDTH, f), F32),
                   jax.ShapeDtypeStruct((2, 1, f), F32)),
        compiler_params=_cparams(("parallel",), 56),
    )(up, cw, cb, dy, wdown, after)


def _mesh_pos():
    return lax.axis_index("x"), lax.axis_index("y"), lax.axis_index("c")


def _remote(src, dst, send_sem, recv_sem, to):
    return pltpu.make_async_remote_copy(src_ref=src, dst_ref=dst, send_sem=send_sem, recv_sem=recv_sem,
                                        device_id=to, device_id_type=pl.DeviceIdType.MESH)


HBM_SPEC = pl.BlockSpec(memory_space=pltpu.HBM)
SEM_SPEC = pl.BlockSpec(memory_space=pltpu.SEMAPHORE)
ANY_SPEC = pl.BlockSpec(memory_space=pl.ANY)
TOKEN_SPEC = pl.BlockSpec(memory_space=pltpu.VMEM)
TOKEN_SHAPE = jax.ShapeDtypeStruct((8, 128), F32)
DATAFLOW = pltpu.SideEffectType.DATAFLOW_SIDE_EFFECTING
GATHER_PLANE = (2, 4, 6)


def _slot(pos):
    return 4 * pos[0] + 2 * pos[1] + pos[2]


def _flip(pos, k):
    return (pos[0] ^ ((k >> 2) & 1), pos[1] ^ ((k >> 1) & 1), pos[2] ^ (k & 1))


def _hbm(a):
    return pltpu.with_memory_space_constraint(a, pltpu.HBM)


def _hbm_shapes(arrays):
    return tuple(pltpu.HBM(a.shape, a.dtype) for a in arrays)


class _Split:
    def __init__(self, n, outs, has_sems):
        k = 2 * n if has_sems else 0
        self.n = n
        self.sems = list(outs[:k])
        self.bufs = list(outs[k:k + 2 * n])
        self.token = outs[-1]


def _split_call(name, body, bufs, sems_in, makes_sems, after):
    n = len(bufs) // 2
    k = 2 * n if makes_sems else 0
    m = len(sems_in)
    afters = list(after) if isinstance(after, (list, tuple)) else [after]
    na = len(afters)

    def wrapped(*refs):
        srcs, dsts = refs[:n], refs[n:2 * n]
        s_in = refs[2 * n:2 * n + m]
        s_out = refs[2 * n + m + na:2 * n + m + na + k]
        token, local_sems = refs[-2], refs[-1]
        body(srcs, dsts, s_in, s_out, local_sems)
        token[...] = jnp.zeros_like(token)

    outs = pl.pallas_call(
        wrapped, name=name,
        out_shape=(pltpu.SemaphoreType.DMA(()),) * k + _hbm_shapes(bufs) + (TOKEN_SHAPE,),
        in_specs=[HBM_SPEC] * (2 * n) + [SEM_SPEC] * m + [ANY_SPEC] * na,
        out_specs=(SEM_SPEC,) * k + (HBM_SPEC,) * (2 * n) + (TOKEN_SPEC,),
        input_output_aliases={i: k + i for i in range(2 * n)},
        scratch_shapes=[pltpu.SemaphoreType.DMA((n,))],
        compiler_params=pltpu.CompilerParams(has_side_effects=DATAFLOW),
    )(*[_hbm(b) for b in bufs], *sems_in, *afters)
    return _Split(n, outs, makes_sems)


def _wait_slots(land, count, send_sem, recv_sem, me, send=False, recv=False):
    span = land.at[pl.ds(0, count)]
    cp = _remote(span, span, send_sem, recv_sem, me)
    if send:
        cp.wait_send()
    if recv:
        cp.wait_recv()


def _gather_start(name, shards, after):
    n = len(shards)
    my_slot = _slot(_mesh_pos())
    lands = [lax.dynamic_update_slice(lax.empty((N_DEV,) + w.shape, w.dtype), w[None], (my_slot, 0, 0)) for w in shards]

    def body(srcs, dsts, _, sems, local_sems):
        me = _mesh_pos()
        for a in range(n):
            for k in (1,) + GATHER_PLANE:
                _remote(srcs[a], dsts[a].at[_slot(me)], sems[a], sems[n + a], _flip(me, k)).start()

    return _split_call(name, body, list(shards) + lands, [], True, after)


def _gather_forward(name, started, after):
    n = started.n

    def body(srcs, dsts, sems_a, sems_b, local_sems):
        me = _mesh_pos()
        sibling = _flip(me, 1)
        for a in range(n):
            _wait_slots(dsts[a], 4, sems_a[a], sems_a[n + a], me, recv=True)
            for k in GATHER_PLANE:
                block = dsts[a].at[_slot(_flip(me, k))]
                _remote(block, block, sems_b[a], sems_b[n + a], sibling).start()
        for a in range(n):
            _wait_slots(dsts[a], 4, sems_a[a], sems_a[n + a], me, send=True)

    return _split_call(name, body, started.bufs, started.sems, True, after)


def _gather_finish(name, forwarded, after):
    n = forwarded.n

    def body(srcs, dsts, sems_b, _, local_sems):
        me = _mesh_pos()
        for a in range(n):
            _wait_slots(dsts[a], 3, sems_b[a], sems_b[n + a], me, send=True, recv=True)

    return _split_call(name, body, forwarded.bufs, forwarded.sems, False, after).bufs[n:]


def _exchange_start(name, blocked, after):
    n = len(blocked)
    my_slot = _slot(_mesh_pos())
    rows = [w.shape[-2] // (N_DEV if w.ndim == 2 else 1) for w in blocked]

    def block(ref, a, slot):
        if len(ref.shape) == 3:
            return ref.at[slot]
        return ref.at[pl.ds(pl.multiple_of(slot * rows[a], 16), rows[a])]

    lands = []
    for w, r in zip(blocked, rows):
        mine = lax.dynamic_slice_in_dim(w, my_slot, 1, 0) if w.ndim == 3 else lax.dynamic_slice_in_dim(w, my_slot * r, r, 0)[None]
        lands.append(lax.dynamic_update_slice(lax.empty((N_DEV, r, w.shape[-1]), w.dtype), mine, (my_slot, 0, 0)))

    def body(srcs, dsts, _, sems, local_sems):
        me = _mesh_pos()
        for a in range(n):
            for k in range(1, N_DEV):
                peer = _flip(me, k)
                _remote(block(srcs[a], a, _slot(peer)), dsts[a].at[_slot(me)], sems[a], sems[n + a], peer).start()

    return _split_call(name, body, list(blocked) + lands, [], True, after)


def _exchange_finish(name, started, after):
    n = started.n

    def body(srcs, dsts, sems, _, local_sems):
        me = _mesh_pos()
        for a in range(n):
            _wait_slots(dsts[a], N_DEV - 1, sems[a], sems[n + a], me, send=True, recv=True)

    return _split_call(name, body, started.bufs, started.sems, False, after).bufs[n:]


def _broadcast_start(name, arrays, after):
    n = len(arrays)
    my_slot = _slot(_mesh_pos())
    lands = [lax.dynamic_update_slice(lax.empty((N_DEV,) + w.shape, w.dtype), w[None], (my_slot, 0, 0)) for w in arrays]

    def body(srcs, dsts, _, sems, local_sems):
        me = _mesh_pos()
        for a in range(n):
            for k in range(1, N_DEV):
                _remote(srcs[a], dsts[a].at[_slot(me)], sems[a], sems[n + a], _flip(me, k)).start()

    return _split_call(name, body, list(arrays) + lands, [], True, after)


def _adamw_math(w, g, m, v):
    m = ADAM_B1 * m + (1.0 - ADAM_B1) * g
    v = ADAM_B2 * v + (1.0 - ADAM_B2) * (g * g)
    m_hat = m / (1.0 - ADAM_B1 ** ADAM_STEP)
    v_hat = v / (1.0 - ADAM_B2 ** ADAM_STEP)
    delta = -ADAM_LR * (m_hat / (jnp.sqrt(v_hat) + ADAM_EPS) + ADAM_WD * w)
    return delta, m, v


def _adamw(name, w, m, v, parts, layer, prev=None):
    _, r, c = w.shape
    tr = _pick(r, tuple(t for t in (256, 128, 64, 32, 16) if t * c <= ADAMW_TILE_ELEMS))
    n_prev = 0 if prev is None else 4

    def body(*refs):
        w_ref, m_ref, v_ref, p_ref = refs[:4]
        g_ref, d_ref, nm_ref, nv_ref = refs[4 + n_prev:]
        g = p_ref[0].astype(F32)
        for src in range(1, N_DEV):
            g = g + p_ref[src].astype(F32)
        delta, nm, nv = _adamw_math(w_ref[...], g, m_ref[...], v_ref[...])
        g_ref[...] = g
        d_ref[...] = delta
        nm_ref[...] = nm
        nv_ref[...] = nv

    wspec = pl.BlockSpec((None, tr, c), lambda i: (layer, i, 0))
    pspec = pl.BlockSpec((N_DEV, tr, c), lambda i: (0, i, 0))
    shp = jax.ShapeDtypeStruct(w.shape, F32)
    return pl.pallas_call(
        body, name=name, grid=(r // tr,), in_specs=[wspec] * 3 + [pspec] + [ANY_SPEC] * n_prev,
        out_specs=(wspec,) * 4, out_shape=(shp,) * 4, input_output_aliases={4 + j: j for j in range(n_prev)},
        compiler_params=_cparams(("parallel",)),
    )(w, m, v, parts, *([] if prev is None else prev))


PACK_TILE = 8 * 128


def _pack(arrays):
    flat = []
    for a in arrays:
        v = a.reshape(-1)
        pad = (-v.shape[0]) % PACK_TILE
        flat.append(jnp.pad(v, (0, pad)) if pad else v)
    return jnp.concatenate(flat).reshape(-1, 128)


def _unpack(buf, like):
    flat = buf.reshape(-1)
    out, off = [], 0
    for a in like:
        n = 1
        for dim in a.shape:
            n *= dim
        out.append(flat[off:off + n].reshape(a.shape))
        off += n + (-n) % PACK_TILE
    return out


def _sum_slots(name, gathered):
    _, r, c = gathered.shape

    def body(x_ref, o_ref):
        acc = x_ref[0].astype(F32)
        for src in range(1, N_DEV):
            acc = acc + x_ref[src].astype(F32)
        o_ref[...] = acc

    return pl.pallas_call(body, name=name, out_shape=jax.ShapeDtypeStruct((r, c), F32))(gathered)


def _adamw_small(w, g, m, v):
    shp = jax.ShapeDtypeStruct(w.shape, F32)

    def body(w_ref, g_ref, m_ref, v_ref, d_ref, nm_ref, nv_ref):
        delta, nm, nv = _adamw_math(w_ref[...], g_ref[...], m_ref[...], v_ref[...])
        d_ref[...] = delta
        nm_ref[...] = nm
        nv_ref[...] = nv

    return pl.pallas_call(body, name="adamw_small", out_shape=(shp,) * 3)(w, g, m, v)


def kernel(x, attn_norm_g, w_in, q_norm_g, k_norm_g, sgu_norm_g, sgu_w, sgu_b, out_norm_a_g, out_norm_b_g, w_out, ffn_norm_g, w_up, conv_w, conv_b, w_down, loss_target, m_attn_norm_g, m_w_in, m_q_norm_g, m_k_norm_g, m_sgu_norm_g, m_sgu_w, m_sgu_b, m_out_norm_a_g, m_out_norm_b_g, m_w_out, m_ffn_norm_g, m_w_up, m_conv_w, m_conv_b, m_w_down, v_attn_norm_g, v_w_in, v_q_norm_g, v_k_norm_g, v_sgu_norm_g, v_sgu_w, v_sgu_b, v_out_norm_a_g, v_out_norm_b_g, v_w_out, v_ffn_norm_g, v_w_up, v_conv_w, v_conv_b, v_w_down):
    depth = w_in.shape[0]
    s, d = x.shape[1], x.shape[2]
    n_heads = (d // 2) // HEAD_DIM
    sgu_col0 = 3 * n_heads
    f2 = w_up.shape[2] * N_DEV
    ff = f2 // 2
    my_slot = 4 * lax.axis_index("x") + 2 * lax.axis_index("y") + lax.axis_index("c")

    wb = [(w_in[l].astype(BF16), w_out[l].astype(BF16), w_up[l].astype(BF16), w_down[l].astype(BF16))
          for l in range(depth)]
    groups = {"in0": [wb[0][0]], "out0": [wb[0][1], conv_w.reshape(depth * CONV_WIDTH, -1)], "up0": [wb[0][2]],
              "down0": [wb[0][3]]}
    for l in range(1, depth):
        groups[f"in{l}"] = [wb[l][0], wb[l][1]]
        groups[f"ffn{l}"] = [wb[l][2], wb[l][3]]
    token = attn_norm_g
    started = {}
    for gname, group in groups.items():
        started[gname] = _gather_start(f"gather_{gname}_start", group, token)
        token = started[gname].token

    def forward(gname, after):
        return _gather_forward(f"gather_{gname}_forward", started[gname], after)

    def finish(gname, forwarded, after):
        return _gather_finish(f"gather_{gname}_finish", forwarded, after)

    conv_b_all = conv_b.reshape(depth, 2, 1, ff)
    sgu_b_col = sgu_b[..., None]
    fwd_in = forward("in0", token)
    win_g = finish("in0", fwd_in, fwd_in.token)[0]

    xs = x[0]
    saved = []
    gathered = []
    for l in range(depth):
        g1 = attn_norm_g[l][None]
        g2 = ffn_norm_g[l][None]
        gq, gk = q_norm_g[l][None], k_norm_g[l][None]
        ga = out_norm_a_g[l][:, None, :]
        gs = sgu_norm_g[l][:, None, :]
        gb = out_norm_b_g[l][:, None, :]
        h1 = _rmsnorm_fwd("attn_norm_fwd", xs, g1)
        p = _mm_nn_blocked("in_proj", h1, win_g, F32)
        att, o, rsum = _attn_fwd(p, gq, gk, ga, n_heads)
        if l == 0:
            fwd_out = forward("out0", att)
            fwd_up = forward("up0", fwd_out.token)
        else:
            fwd_up = forward(f"ffn{l}", att)
        sg = _sgu_fwd(p, gs, sgu_w[l], sgu_b_col[l], gb, sgu_col0, fwd_up.token)
        mix = jnp.concatenate([att, sg], axis=-1)
        if l == 0:
            wout_g, cw = finish("out0", fwd_out, mix)
            cw = jnp.transpose(cw.reshape(N_DEV, depth, CONV_WIDTH, -1), (1, 2, 0, 3)).reshape(depth, CONV_WIDTH, 2, ff)
            conv_w_all = jnp.transpose(cw, (0, 2, 1, 3))
        x1 = _mm_nn_res("out_proj", mix, wout_g.reshape(d, d), xs)
        h2 = _rmsnorm_fwd("ffn_norm_fwd", x1, g2)
        if l == 0:
            wup_g = finish("up0", fwd_up, h2)[0]
            fwd_down = forward("down0", wup_g)
            up = _mm_nn_blocked("up_proj", h2, wup_g, F32, halves=True, after=fwd_down.token)
            wdown_g = finish("down0", fwd_down, up)[0]
        else:
            wup_g, wdown_g = finish(f"ffn{l}", fwd_up, h2)
            up = _mm_nn_blocked("up_proj", h2, wup_g, F32, halves=True)
        act = _conv_gate_fwd(up, conv_w_all[l], conv_b_all[l])
        saved.append((xs, h1, p, o, rsum, mix, x1, h2, up, act))
        gathered.append((win_g, wout_g, wup_g, wdown_g))
        if l + 1 < depth:
            fwd_in = forward(f"in{l + 1}", act)
            x2 = _mm_nn_res("down_proj", act, wdown_g.reshape(ff, d), x1, after=fwd_in.token)
            win_g, wout_g = finish(f"in{l + 1}", fwd_in, x2)
        else:
            x2 = _mm_nn_res("down_proj", act, wdown_g.reshape(ff, d), x1)
        xs = x2

    dx, dxb, loss_vec = _loss_head(xs, loss_target[0])
    loss = lax.psum(loss_vec[0, 0], MESH_AXES)

    exchanges = []
    small = [None] * depth
    small_names = ["attn_norm_g", "q_norm_g", "k_norm_g", "sgu_norm_g", "sgu_w", "sgu_b", "out_norm_a_g",
                   "out_norm_b_g", "ffn_norm_g", "conv_b", "conv_w"]
    for l in reversed(range(depth)):
        xs0, h1, p, o, rsum, mix, x1, h2, up, act = saved[l]
        win_g, wout_g, wup_g, wdown_g = gathered[l]
        wout_full = wout_g.reshape(d, d)
        wdown_full = wdown_g.reshape(ff, d)
        g1 = attn_norm_g[l][None]
        g2 = ffn_norm_g[l][None]
        gq, gk = q_norm_g[l][None], k_norm_g[l][None]
        ga = out_norm_a_g[l][:, None, :]
        gs = sgu_norm_g[l][:, None, :]
        gb = out_norm_b_g[l][:, None, :]
        d_wdown = _mm_tn_plain("down_proj_dw", act, dxb)
        exchanges.append((l, "down", ("w_down",), _exchange_start(f"grad_down{l}_start", [d_wdown], dx)))
        dup, d_cw, d_cb = _down_dx_conv_gate_bwd(up, conv_w_all[l], conv_b_all[l], dxb, wdown_full,
                                                 exchanges[-1][3].token)
        d_wup = _mm_tn_blocked("up_proj_dw", h2, dup, N_DEV, halves=True)
        exchanges.append((l, "up", ("w_up",), _exchange_start(f"grad_up{l}_start", [d_wup], d_cb)))
        dh2 = _mm_nt_blocked("up_proj_dx", dup, wup_g, halves=True, after=exchanges[-1][3].token)
        dx, dxb, d_g2 = _rmsnorm_bwd("ffn_norm_bwd", dh2, x1, g2, dx)
        d_wout = _mm_tn_plain("out_proj_dw", mix, dxb)
        dmix = _mm_nt_plain("out_proj_dx", dxb, wout_full)
        dq, dk, dv, d_gq, d_gk, d_ga = _attn_bwd(p, gq, gk, ga, o, rsum, dmix, n_heads)
        du, dvs, d_gs, d_sw, d_sb, d_gb = _sgu_bwd(p, gs, sgu_w[l], sgu_b_col[l], gb, dmix, sgu_col0, n_heads)
        dp = jnp.concatenate([dq, dk, dv, du, dvs], axis=-1)
        d_win = _mm_tn_blocked("in_proj_dw", h1, dp, N_DEV)
        exchanges.append((l, "mix", ("w_in", "w_out"), _exchange_start(f"grad_mix{l}_start", [d_win, d_wout], d_gq)))
        dh1 = _mm_nt_blocked("in_proj_dx", dp, win_g, after=exchanges[-1][3].token)
        dx, dxb, d_g1 = _rmsnorm_bwd("attn_norm_bwd", dh1, xs0, g1, dx)
        small[l] = dict(attn_norm_g=d_g1[0], q_norm_g=d_gq[0], k_norm_g=d_gk[0], sgu_norm_g=d_gs[:, 0], sgu_w=d_sw,
                        sgu_b=d_sb[..., 0], out_norm_a_g=d_ga[:, 0], out_norm_b_g=d_gb[:, 0], ffn_norm_g=d_g2[0],
                        conv_w=jnp.transpose(d_cw, (1, 0, 2)).reshape(CONV_WIDTH, f2), conv_b=d_cb.reshape(f2))
    grad_x = dx[None]

    f32_names = [n for n in small_names if n != "sgu_w"]
    small_g = [jnp.stack([small[l][n] for l in range(depth)]) for n in f32_names]
    sgu_w_g = jnp.stack([small[l]["sgu_w"] for l in range(depth)])
    small_sent = _broadcast_start("grad_small_start", [_pack(small_g), sgu_w_g.reshape(-1, TILE).astype(BF16)], dx)

    res = {}
    big = dict(w_in=(w_in, m_w_in, v_w_in), w_out=(w_out, m_w_out, v_w_out), w_up=(w_up, m_w_up, v_w_up),
               w_down=(w_down, m_w_down, v_w_down))
    after = [small_sent.token]
    for l, stage, names, ex in exchanges:
        landed = _exchange_finish(f"grad_{stage}{l}_finish", ex, after)
        after = []
        for name, parts in zip(names, landed):
            w, m, v = big[name]
            res[name] = _adamw(f"adamw_{name}", w, m, v, parts, l, res.get(name))
            after.append(res[name][0])
    small_all, sgu_w_all = _exchange_finish("grad_small_finish", small_sent, after)
    small_sum = _unpack(_sum_slots("small_grad_sum", small_all), small_g)
    g_small = dict(zip(f32_names, small_sum))
    g_small["sgu_w"] = _sum_slots("sgu_w_grad_sum", sgu_w_all).reshape(sgu_w.shape)
    cwn = conv_w.shape[2]
    g_small["conv_w"] = lax.dynamic_slice_in_dim(g_small["conv_w"], my_slot * cwn, cwn, axis=2)
    small_w = dict(attn_norm_g=(attn_norm_g, m_attn_norm_g, v_attn_norm_g), q_norm_g=(q_norm_g, m_q_norm_g, v_q_norm_g),
                   k_norm_g=(k_norm_g, m_k_norm_g, v_k_norm_g), sgu_norm_g=(sgu_norm_g, m_sgu_norm_g, v_sgu_norm_g),
                   sgu_w=(sgu_w, m_sgu_w, v_sgu_w), sgu_b=(sgu_b, m_sgu_b, v_sgu_b),
                   out_norm_a_g=(out_norm_a_g, m_out_norm_a_g, v_out_norm_a_g),
                   out_norm_b_g=(out_norm_b_g, m_out_norm_b_g, v_out_norm_b_g),
                   ffn_norm_g=(ffn_norm_g, m_ffn_norm_g, v_ffn_norm_g), conv_b=(conv_b, m_conv_b, v_conv_b),
                   conv_w=(conv_w, m_conv_w, v_conv_w))
    like = [small_w[n][0] for n in small_names]
    pw = _pack([small_w[n][0] for n in small_names])
    pm = _pack([small_w[n][1] for n in small_names])
    pv = _pack([small_w[n][2] for n in small_names])
    pg = _pack([g_small[n].reshape(small_w[n][0].shape) for n in small_names])
    pd, pnm, pnv = _adamw_small(pw, pg, pm, pv)
    for n, dlt, nm, nv in zip(small_names, _unpack(pd, like), _unpack(pnm, like), _unpack(pnv, like)):
        res[n] = (g_small[n].reshape(small_w[n][0].shape), dlt, nm, nv)

    order = ["attn_norm_g", "w_in", "q_norm_g", "k_norm_g", "sgu_norm_g", "sgu_w", "sgu_b", "out_norm_a_g",
             "out_norm_b_g", "w_out", "ffn_norm_g", "w_up", "conv_w", "conv_b", "w_down"]
    outs = [loss, grad_x]
    for field in range(4):
        outs += [res[n][field] for n in order]
    return tuple(outs)
```

```python
import functools

import jax
import jax.numpy as jnp
from jax import lax
from jax.experimental import pallas as pl
from jax.experimental.pallas import tpu as pltpu

F32 = jnp.float32
BF16 = jnp.bfloat16
EPS = 1e-6
HEAD_DIM = 128
TILE = 128
ATTN_VMEM_MB = 58
ATTN_HEADS_PER_STEP = 4
N_GROUPS = 8
CONV_WIDTH = 3
N_DEV = 8
MESH_AXES = ("x", "y", "c")
MIB = 1024 * 1024

ADAM_LR = 0.001
ADAM_B1 = 0.9
ADAM_B2 = 0.999
ADAM_EPS = 1e-08
ADAM_WD = 0.01
ADAM_STEP = 10
ADAMW_TILE_ELEMS = 160 * 1024

NT_DIMS = (((1,), (1,)), ((), ()))
NN_DIMS = (((1,), (0,)), ((), ()))
TN_DIMS = (((0,), (0,)), ((), ()))


def _cparams(sem, vmem_mb=48):
    return pltpu.CompilerParams(dimension_semantics=sem, vmem_limit_bytes=vmem_mb * MIB)


def _pick(n, cands):
    for c in cands:
        if n % c == 0:
            return c
    return n


def _mm(name, grid, ins, in_specs, out_shape, out_spec, dims, has_res=False, parts=None, vmem_mb=56, after=None):
    n_in = 2 + has_res + (after is not None)
    if after is not None:
        ins = tuple(ins) + (after,)
        in_specs = list(in_specs) + [pl.BlockSpec(after.shape, lambda *_: (0, 0))]

    def body(*refs):
        a_ref, b_ref = refs[:2]
        o_ref = refs[n_in]
        if parts is None:
            acc = lax.dot_general(a_ref[...], b_ref[...], dims, preferred_element_type=F32)
        else:
            acc = None
            for part in parts:
                a, b = part(a_ref, b_ref)
                prod = lax.dot_general(a, b, dims, preferred_element_type=F32)
                acc = prod if acc is None else acc + prod
        if has_res:
            acc = acc + refs[2][...]
        o_ref[...] = acc.astype(o_ref.dtype)

    return pl.pallas_call(
        body, name=name, grid=grid, in_specs=in_specs, out_specs=out_spec, out_shape=out_shape,
        compiler_params=_cparams(("parallel",) * len(grid), vmem_mb),
    )(*ins)


def _rows_for(m, row_bytes, budget):
    return _pick(m, tuple(t for t in (2048, 1024, 512, 256, 128) if t * row_bytes <= budget))


def _mm_nn_blocked(name, a, wb, out_dtype, halves=False, after=None):
    m, k = a.shape
    nb, _, bn = wb.shape
    tm = _rows_for(m, bn * jnp.dtype(out_dtype).itemsize, 6 * MIB)
    a_spec = pl.BlockSpec((tm, k), lambda j, i: (i, 0))
    b_spec = pl.BlockSpec((None, k, bn), lambda j, i: (j, 0, 0))
    if halves:
        hb = nb // 2
        out_shape = jax.ShapeDtypeStruct((2, m, hb * bn), out_dtype)
        o_spec = pl.BlockSpec((None, tm, bn), lambda j, i: (j // hb, i, j % hb))
    else:
        out_shape = jax.ShapeDtypeStruct((m, nb * bn), out_dtype)
        o_spec = pl.BlockSpec((tm, bn), lambda j, i: (i, j))
    return _mm(name, (nb, m // tm), (a, wb), [a_spec, b_spec], out_shape, o_spec, NN_DIMS, after=after)


def _mm_nn_res(name, a, w, res, after=None):
    m, k = a.shape
    n = w.shape[1]
    tm = _pick(m, (512, 256, 128))
    tn = _rows_for(n, k * 2, 12 * MIB)
    a_spec = pl.BlockSpec((tm, k), lambda j, i: (i, 0))
    b_spec = pl.BlockSpec((k, tn), lambda j, i: (0, j))
    r_spec = pl.BlockSpec((tm, tn), lambda j, i: (i, j))
    o_spec = pl.BlockSpec((tm, tn), lambda j, i: (i, j))
    return _mm(name, (n // tn, m // tm), (a, w, res), [a_spec, b_spec, r_spec], jax.ShapeDtypeStruct((m, n), F32),
               o_spec, NN_DIMS, has_res=True, after=after)


def _mm_nt_blocked(name, dy, wb, halves=False, after=None):
    nb, n, bn = wb.shape
    m = dy.shape[-2]
    tm = _pick(m, (512, 256, 128))
    tn = _rows_for(n, nb * bn * 2, 12 * MIB)
    if halves:
        hb = nb // 2
        a_spec = pl.BlockSpec((2, tm, hb * bn), lambda j, i: (0, i, 0))
        a_part = lambda kk: (lambda a_ref: a_ref[kk // hb, :, (kk % hb) * bn:(kk % hb + 1) * bn])
    else:
        a_spec = pl.BlockSpec((tm, nb * bn), lambda j, i: (i, 0))
        a_part = lambda kk: (lambda a_ref: a_ref[:, kk * bn:(kk + 1) * bn])
    parts = [(lambda a_ref, b_ref, kk=kk, sel=a_part(kk): (sel(a_ref), b_ref[kk])) for kk in range(nb)]
    b_spec = pl.BlockSpec((nb, tn, bn), lambda j, i: (0, j, 0))
    o_spec = pl.BlockSpec((tm, tn), lambda j, i: (i, j))
    return _mm(name, (n // tn, m // tm), (dy, wb), [a_spec, b_spec], jax.ShapeDtypeStruct((m, n), F32), o_spec,
               NT_DIMS, parts=parts, after=after)


def _mm_nt_plain(name, dy, w, out_dtype=F32):
    m, k = dy.shape
    n = w.shape[0]
    tm = _rows_for(m, k * 2, 8 * MIB)
    tn = _pick(n, (512, 256, 128))
    a_spec = pl.BlockSpec((tm, k), lambda j, i: (i, 0))
    b_spec = pl.BlockSpec((tn, k), lambda j, i: (j, 0))
    o_spec = pl.BlockSpec((tm, tn), lambda j, i: (i, j))
    return _mm(name, (n // tn, m // tm), (dy, w), [a_spec, b_spec], jax.ShapeDtypeStruct((m, n), out_dtype), o_spec,
               NT_DIMS)


def _mm_tn_blocked(name, a, dy, nb, halves=False):
    s, k1 = a.shape
    bn = (dy.shape[-1] * (2 if halves else 1)) // nb
    tm = _rows_for(k1, bn * 2, 6 * MIB)
    a_spec = pl.BlockSpec((s, tm), lambda j, i: (0, i))
    if halves:
        hb = nb // 2
        b_spec = pl.BlockSpec((None, s, bn), lambda j, i: (j // hb, 0, j % hb))
    else:
        b_spec = pl.BlockSpec((s, bn), lambda j, i: (0, j))
    o_spec = pl.BlockSpec((None, tm, bn), lambda j, i: (j, i, 0))
    return _mm(name, (nb, k1 // tm), (a, dy), [a_spec, b_spec], jax.ShapeDtypeStruct((nb, k1, bn), BF16), o_spec,
               TN_DIMS)


def _mm_tn_plain(name, a, dy):
    s, k1 = a.shape
    n = dy.shape[1]
    tm = _pick(k1, (512, 256, 128))
    tn = _rows_for(n, s * 2, 8 * MIB)
    a_spec = pl.BlockSpec((s, tm), lambda i, j: (0, i))
    b_spec = pl.BlockSpec((s, tn), lambda i, j: (0, j))
    o_spec = pl.BlockSpec((tm, tn), lambda i, j: (i, j))
    return _mm(name, (k1 // tm, n // tn), (a, dy), [a_spec, b_spec], jax.ShapeDtypeStruct((k1, n), BF16), o_spec,
               TN_DIMS)


def _rstd(x):
    return lax.rsqrt(jnp.mean(x * x, axis=-1, keepdims=True) + EPS)


def _norm_bwd(dy, xhat, r, g):
    dxhat = dy * g
    return r * (dxhat - xhat * jnp.mean(dxhat * xhat, axis=-1, keepdims=True))


def _rmsnorm_fwd(name, x, g):
    s, d = x.shape
    tr = _pick(s, (256, 128))

    def body(x_ref, g_ref, h_ref):
        xv = x_ref[...]
        h_ref[...] = (xv * _rstd(xv) * g_ref[...]).astype(BF16)

    return pl.pallas_call(
        body, name=name, grid=(s // tr,),
        in_specs=[pl.BlockSpec((tr, d), lambda i: (i, 0)), pl.BlockSpec((1, d), lambda i: (0, 0))],
        out_specs=pl.BlockSpec((tr, d), lambda i: (i, 0)),
        out_shape=jax.ShapeDtypeStruct((s, d), BF16), compiler_params=_cparams(("parallel",)),
    )(x, g)


def _rmsnorm_bwd(name, dh, x, g, dres):
    s, d = x.shape
    tr = _pick(s, (256, 128))

    def body(dh_ref, x_ref, g_ref, dres_ref, dx_ref, dxb_ref, dg_ref):
        xv = x_ref[...]
        r = _rstd(xv)
        xhat = xv * r
        dhv = dh_ref[...]
        dx = dres_ref[...] + _norm_bwd(dhv, xhat, r, g_ref[...])
        dx_ref[...] = dx
        dxb_ref[...] = dx.astype(BF16)
        part = jnp.sum(dhv * xhat, axis=0, keepdims=True)

        @pl.when(pl.program_id(0) == 0)
        def _():
            dg_ref[...] = part

        @pl.when(pl.program_id(0) > 0)
        def _():
            dg_ref[...] += part

    row = pl.BlockSpec((tr, d), lambda i: (i, 0))
    vec = pl.BlockSpec((1, d), lambda i: (0, 0))
    return pl.pallas_call(
        body, name=name, grid=(s // tr,), in_specs=[row, row, vec, row], out_specs=(row, row, vec),
        out_shape=(jax.ShapeDtypeStruct((s, d), F32), jax.ShapeDtypeStruct((s, d), BF16),
                   jax.ShapeDtypeStruct((1, d), F32)),
        compiler_params=_cparams(("arbitrary",)),
    )(dh, x, g, dres)


def _loss_head(y, target):
    s, d = y.shape
    tr = _pick(s, (256, 128))

    def body(y_ref, t_ref, dy_ref, dyb_ref, loss_ref):
        err = y_ref[...] - t_ref[...]
        dy = err * (1.0 / d)
        dy_ref[...] = dy
        dyb_ref[...] = dy.astype(BF16)
        part = 0.5 * jnp.sum(jnp.mean(err * err, axis=-1, keepdims=True), axis=0, keepdims=True)
        part = jnp.broadcast_to(part, (1, 128))

        @pl.when(pl.program_id(0) == 0)
        def _():
            loss_ref[...] = part

        @pl.when(pl.program_id(0) > 0)
        def _():
            loss_ref[...] += part

    row = pl.BlockSpec((tr, d), lambda i: (i, 0))
    return pl.pallas_call(
        body, name="loss_head", grid=(s // tr,), in_specs=[row, row],
        out_specs=(row, row, pl.BlockSpec((1, 128), lambda i: (0, 0))),
        out_shape=(jax.ShapeDtypeStruct((s, d), F32), jax.ShapeDtypeStruct((s, d), BF16),
                   jax.ShapeDtypeStruct((1, 128), F32)),
        compiler_params=_cparams(("arbitrary",)),
    )(y, target)


def _split_dot(x, tri):
    hi = x.astype(BF16)
    lo = (x - hi.astype(F32)).astype(BF16)
    return (jnp.dot(hi, tri, preferred_element_type=F32) + jnp.dot(lo, tri, preferred_element_type=F32))


def _tile_iotas():
    row = lax.broadcasted_iota(jnp.int32, (TILE, TILE), 0)
    col = lax.broadcasted_iota(jnp.int32, (TILE, TILE), 1)
    return row, col


def _sb_logits(qi, kb, mask):
    z = lax.dot_general(qi, kb, NT_DIMS, preferred_element_type=F32) * (HEAD_DIM ** -0.5)
    sp = jnp.log1p(jnp.exp(-jnp.abs(z)))
    lb = jnp.minimum(z, 0.0) - sp
    l1m = -jnp.maximum(z, 0.0) - sp
    if mask is not None:
        l1m = jnp.where(mask, l1m, 0.0)
    return lb, l1m


def _attn_fwd(p, gq, gk, ga, n_heads):
    s = p.shape[0]
    nq = s // TILE

    hp = ATTN_HEADS_PER_STEP
    wd = hp * HEAD_DIM

    def body(q_ref, k_ref, v_ref, gq_ref, gk_ref, ga_ref, att_ref, o_ref, r_ref, qn_s, kn_s, vb_s):
        heads = [slice(hh * HEAD_DIM, (hh + 1) * HEAD_DIM) for hh in range(hp)]
        for hd in heads:
            qv = q_ref[:, hd]
            qn_s[:, hd] = (qv * _rstd(qv) * gq_ref[...]).astype(BF16)
            kv = k_ref[:, hd]
            kn_s[:, hd] = (kv * _rstd(kv) * gk_ref[...]).astype(BF16)
        vb_s[...] = v_ref[...].astype(BF16)
        row, col = _tile_iotas()
        causal = col < row
        upper_ones = jnp.concatenate([(row > col).astype(BF16), jnp.ones((TILE, TILE), BF16)], axis=1)

        def tiles(rows, key_blocks, states, mask):
            chains = [(hi, hd, keys) for hi, hd in enumerate(heads) for keys in key_blocks]
            logits = [_sb_logits(qn_s[rows, hd], kn_s[keys, hd], mask) for _, hd, keys in chains]
            sums = [_split_dot(l1m, upper_ones) for _, l1m in logits]
            carry = [c for _, c in states]
            probs = []
            for (hi, _, _), (lb, _), sm in zip(chains, logits, sums):
                a = jnp.exp(lb + sm[:, :TILE] + carry[hi])
                carry[hi] = carry[hi] + sm[:, TILE:]
                probs.append((a if mask is None else jnp.where(mask, a, 0.0)).astype(BF16))
            outs = [jnp.dot(a, vb_s[keys, hd], preferred_element_type=F32) for a, (_, hd, keys) in zip(probs, chains)]
            acc = [o_acc for o_acc, _ in states]
            for (hi, _, _), o in zip(chains, outs):
                acc[hi] = acc[hi] + o
            return tuple(zip(acc, carry))

        def key_block(b):
            return pl.ds(pl.multiple_of(b * TILE, TILE), TILE)

        def qblock(i, _):
            rows = pl.ds(pl.multiple_of(i * TILE, TILE), TILE)
            zero = jnp.zeros((TILE, HEAD_DIM), F32)
            states = tiles(rows, [rows], tuple((zero, zero) for _ in heads), causal)
            states = lax.cond(i % 2 == 1, lambda st: tiles(rows, [key_block(i - 1)], st, None), lambda st: st, states)
            top = i - i % 2

            def kblocks(jj, states):
                return tiles(rows, [key_block(top - 1 - 2 * jj), key_block(top - 2 - 2 * jj)], states, None)

            states = lax.fori_loop(0, i // 2, kblocks, states)
            for hh, (hd, (o_acc, c)) in enumerate(zip(heads, states)):
                o_ref[rows, hd] = o_acc
                r_ref[rows, hd] = c
                att_ref[rows, hd] = (o_acc * _rstd(o_acc) * ga_ref[hh]).astype(BF16)
            return 0

        lax.fori_loop(0, nq, qblock, 0)

    col_blk = lambda off: pl.BlockSpec((s, wd), lambda h: (0, off + h))
    vec = pl.BlockSpec((1, HEAD_DIM), lambda h: (0, 0))
    hvec = pl.BlockSpec((hp, 1, HEAD_DIM), lambda h: (h, 0, 0))
    out = pl.BlockSpec((s, wd), lambda h: (0, h))
    w = n_heads * HEAD_DIM
    steps = n_heads // hp
    return pl.pallas_call(
        body, name="attn_fwd", grid=(steps,),
        in_specs=[col_blk(0), col_blk(steps), col_blk(2 * steps), vec, vec, hvec],
        out_specs=(out, out, out),
        out_shape=(jax.ShapeDtypeStruct((s, w), BF16), jax.ShapeDtypeStruct((s, w), F32),
                   jax.ShapeDtypeStruct((s, w), F32)),
        scratch_shapes=[pltpu.VMEM((s, wd), BF16)] * 3,
        compiler_params=_cparams(("parallel",), ATTN_VMEM_MB),
    )(p, p, p, gq, gk, ga)


def _attn_bwd(p, gq, gk, ga, o, rsum, dmix, n_heads):
    s = p.shape[0]
    nq = s // TILE

    hp = ATTN_HEADS_PER_STEP
    wd = hp * HEAD_DIM
    scale = HEAD_DIM ** -0.5

    def body(q_ref, k_ref, v_ref, gq_ref, gk_ref, ga_ref, o_ref, r_ref, dm_ref,
             dq_ref, dk_ref, dv_ref, dgq_ref, dgk_ref, dga_ref,
             qn_s, kn_s, vb_s, do_s, dqn_s, dkn_s, dv_s):
        step = pl.program_id(0)
        gqv, gkv = gq_ref[...], gk_ref[...]
        heads = [slice(hh * HEAD_DIM, (hh + 1) * HEAD_DIM) for hh in range(hp)]
        for hh, hd in enumerate(heads):
            qv = q_ref[:, hd]
            qn_s[:, hd] = (qv * _rstd(qv) * gqv).astype(BF16)
            kv = k_ref[:, hd]
            kn_s[:, hd] = (kv * _rstd(kv) * gkv).astype(BF16)
            ov = o_ref[:, hd]
            ro = _rstd(ov)
            ohat = ov * ro
            dm = dm_ref[:, hd]
            dga_ref[hh] = jnp.sum(dm * ohat, axis=0, keepdims=True)
            do_s[:, hd] = _norm_bwd(dm, ohat, ro, ga_ref[hh]).astype(BF16)
        vb_s[...] = v_ref[...].astype(BF16)
        dkn_s[...] = jnp.zeros_like(dkn_s)
        dv_s[...] = jnp.zeros_like(dv_s)
        row, col = _tile_iotas()
        causal = col < row
        ones = jnp.ones((TILE, TILE), BF16)
        incl_ones = jnp.concatenate([(row <= col).astype(BF16), ones], axis=1)
        excl_ones = jnp.concatenate([(row < col).astype(BF16), ones], axis=1)

        def tiles(rows, key_blocks, states, mask):
            chains = [(hi, hd, keys) for hi, hd in enumerate(heads) for keys in key_blocks]
            qis = [qn_s[rows, hd] for hd in heads]
            dois = [do_s[rows, hd] for hd in heads]
            logits = [_sb_logits(qis[hi], kn_s[keys, hd], mask) for hi, hd, keys in chains]
            sums = [_split_dot(l1m, incl_ones) for _, l1m in logits]
            das = [lax.dot_general(dois[hi], vb_s[keys, hd], NT_DIMS, preferred_element_type=F32)
                   for hi, hd, keys in chains]
            pfx = [st[1] for st in states]
            probs, dss = [], []
            for (hi, hd, _), (lb, _), sm, da in zip(chains, logits, sums, das):
                a = jnp.exp(lb + (r_ref[rows, hd] - pfx[hi] - sm[:, :TILE]))
                pfx[hi] = pfx[hi] + sm[:, TILE:]
                a = a if mask is None else jnp.where(mask, a, 0.0)
                probs.append(a.astype(BF16))
                dss.append(da * a)
            dsums = [_split_dot(ds, excl_ones) for ds in dss]
            pc = [st[2] for st in states]
            dzs = []
            for (hi, _, _), (lb, _), ds, dsm in zip(chains, logits, dss, dsums):
                dl1m = pc[hi] + dsm[:, :TILE]
                pc[hi] = pc[hi] + dsm[:, TILE:]
                dl1m = dl1m if mask is None else jnp.where(mask, dl1m, 0.0)
                beta = jnp.exp(lb)
                dzs.append(((ds * (1.0 - beta) - dl1m * beta) * scale).astype(BF16))
            dqs = [jnp.dot(dz, kn_s[keys, hd], preferred_element_type=F32) for dz, (_, hd, keys) in zip(dzs, chains)]
            for dz, a, (hi, hd, keys) in zip(dzs, probs, chains):
                dkn_s[keys, hd] += lax.dot_general(dz, qis[hi], TN_DIMS, preferred_element_type=F32)
                dv_s[keys, hd] += lax.dot_general(a, dois[hi], TN_DIMS, preferred_element_type=F32)
            dq_acc = [st[0] for st in states]
            for (hi, _, _), dq in zip(chains, dqs):
                dq_acc[hi] = dq_acc[hi] + dq
            return tuple(zip(dq_acc, pfx, pc))

        def key_block(b):
            return pl.ds(pl.multiple_of(b * TILE, TILE), TILE)

        def qblock(i, _):
            rows = pl.ds(pl.multiple_of(i * TILE, TILE), TILE)
            zero = jnp.zeros((TILE, HEAD_DIM), F32)

            def kblocks(jj, states):
                return tiles(rows, [key_block(2 * jj), key_block(2 * jj + 1)], states, None)

            states = lax.fori_loop(0, i // 2, kblocks, tuple((zero, zero, zero) for _ in heads))
            states = lax.cond(i % 2 == 1, lambda st: tiles(rows, [key_block(i - 1)], st, None), lambda st: st, states)
            states = tiles(rows, [rows], states, causal)
            for hd, (dq_acc, _, _) in zip(heads, states):
                dqn_s[rows, hd] = dq_acc
            return 0

        lax.fori_loop(0, nq, qblock, 0)

        def norm_in_bwd(x_ref, g, dn_s, dx_ref, dg_ref):
            part = jnp.zeros((1, HEAD_DIM), F32)
            for hd in heads:
                xv = x_ref[:, hd]
                r = _rstd(xv)
                xhat = xv * r
                dn = dn_s[:, hd]
                dx_ref[:, hd] = _norm_bwd(dn, xhat, r, g).astype(BF16)
                part = part + jnp.sum(dn * xhat, axis=0, keepdims=True)

            @pl.when(step == 0)
            def _():
                dg_ref[...] = part

            @pl.when(step > 0)
            def _():
                dg_ref[...] += part

        norm_in_bwd(q_ref, gqv, dqn_s, dq_ref, dgq_ref)
        norm_in_bwd(k_ref, gkv, dkn_s, dk_ref, dgk_ref)
        dv_ref[...] = dv_s[...].astype(BF16)

    once = pl.Buffered(1)
    steps = n_heads // hp
    col_blk = lambda off: pl.BlockSpec((s, wd), lambda h: (0, off + h), pipeline_mode=once)
    vec = pl.BlockSpec((1, HEAD_DIM), lambda h: (0, 0))
    hvec = pl.BlockSpec((hp, 1, HEAD_DIM), lambda h: (h, 0, 0))
    blk = pl.BlockSpec((s, wd), lambda h: (0, h), pipeline_mode=once)
    w = n_heads * HEAD_DIM
    big = jax.ShapeDtypeStruct((s, w), BF16)
    return pl.pallas_call(
        body, name="attn_bwd", grid=(steps,),
        in_specs=[col_blk(0), col_blk(steps), col_blk(2 * steps), vec, vec, hvec, blk, blk, blk],
        out_specs=(blk, blk, blk, vec, vec, hvec),
        out_shape=(big, big, big, jax.ShapeDtypeStruct((1, HEAD_DIM), F32), jax.ShapeDtypeStruct((1, HEAD_DIM), F32),
                   jax.ShapeDtypeStruct((n_heads, 1, HEAD_DIM), F32)),
        scratch_shapes=[pltpu.VMEM((s, wd), BF16)] * 4 + [pltpu.VMEM((s, wd), F32)] * 3,
        compiler_params=_cparams(("arbitrary",), ATTN_VMEM_MB),
    )(p, p, p, gq, gk, ga, o, rsum, dmix)


_INV_SQRT2 = 0.7071067811865476
_INV_SQRT_2PI = 0.3989422804014327


def _gelu(x):
    return 0.5 * x * (1.0 + lax.erf(x * _INV_SQRT2))


def _gelu_grad(x):
    return 0.5 * (1.0 + lax.erf(x * _INV_SQRT2)) + x * (_INV_SQRT_2PI * jnp.exp(-0.5 * x * x))


def _sgu_fwd(p, gs, w_s, b_s, gb, col0, after):
    s = p.shape[0]
    n_chunks = s // TILE
    per_trip = _pick(n_chunks, (4, 2, 1))

    def body(u_ref, v_ref, gs_ref, w_ref, b_ref, gb_ref, _, out_ref, vs_s):
        vg = _gelu(v_ref[...])
        vs_s[...] = (vg * _rstd(vg) * gs_ref[...]).astype(BF16)
        row, col = _tile_iotas()
        wt = jnp.where(col <= row, w_ref[...], 0.0).astype(BF16)
        bcol = b_ref[...]
        gbv = gb_ref[...]

        def chunks(c, _):
            rows = [pl.ds(pl.multiple_of((c * per_trip + k) * TILE, TILE), TILE) for k in range(per_trip)]
            mixed = [jnp.dot(wt, vs_s[r, :], preferred_element_type=F32) + bcol for r in rows]
            sgs = [_gelu(u_ref[r, :]) * mx for r, mx in zip(rows, mixed)]
            for r, sg in zip(rows, sgs):
                out_ref[r, :] = (sg * _rstd(sg) * gbv).astype(BF16)
            return 0

        lax.fori_loop(0, n_chunks // per_trip, chunks, 0)

    col_blk = lambda off: pl.BlockSpec((s, HEAD_DIM), lambda g: (0, off + g))
    gvec = pl.BlockSpec((None, 1, HEAD_DIM), lambda g: (g, 0, 0))
    return pl.pallas_call(
        body, name="sgu_fwd", grid=(N_GROUPS,),
        in_specs=[col_blk(col0), col_blk(col0 + N_GROUPS), gvec,
                  pl.BlockSpec((None, TILE, TILE), lambda g: (g, 0, 0)),
                  pl.BlockSpec((None, TILE, 1), lambda g: (g, 0, 0)), gvec,
                  pl.BlockSpec(after.shape, lambda g: (0, 0))],
        out_specs=pl.BlockSpec((s, HEAD_DIM), lambda g: (0, g)),
        out_shape=jax.ShapeDtypeStruct((s, N_GROUPS * HEAD_DIM), BF16),
        scratch_shapes=[pltpu.VMEM((s, HEAD_DIM), BF16)],
        compiler_params=_cparams(("parallel",)),
    )(p, p, gs, w_s, b_s, gb, after)


def _sgu_bwd(p, gs, w_s, b_s, gb, dmix, col0, dm_col0):
    s = p.shape[0]
    n_chunks = s // TILE
    per_trip = _pick(n_chunks, (4, 2, 1))

    def body(u_ref, v_ref, gs_ref, w_ref, b_ref, gb_ref, dm_ref,
             du_ref, dv_ref, dgs_ref, dw_ref, db_ref, dgb_ref, vs_s, dvs_s):
        gsv = gs_ref[...]
        gbv = gb_ref[...]
        vg = _gelu(v_ref[...])
        vs_s[...] = (vg * _rstd(vg) * gsv).astype(BF16)
        row, col = _tile_iotas()
        causal = col <= row
        wt = jnp.where(causal, w_ref[...], 0.0).astype(BF16)
        bcol = b_ref[...]

        def chunks(c, carry):
            dw_acc, db_acc, dgb_acc = carry
            rows = [pl.ds(pl.multiple_of((c * per_trip + k) * TILE, TILE), TILE) for k in range(per_trip)]
            vss = [vs_s[r, :] for r in rows]
            mixed = [jnp.dot(wt, vs, preferred_element_type=F32) + bcol for vs in vss]
            dmbs = []
            for r, mx in zip(rows, mixed):
                u_pre = u_ref[r, :]
                u = _gelu(u_pre)
                sg = u * mx
                rs = _rstd(sg)
                sghat = sg * rs
                dm = dm_ref[r, :]
                dsg = _norm_bwd(dm, sghat, rs, gbv)
                dgb_acc = dgb_acc + jnp.sum(dm * sghat, axis=0, keepdims=True)
                du_ref[r, :] = (dsg * mx * _gelu_grad(u_pre)).astype(BF16)
                dmixed = dsg * u
                db_acc = db_acc + jnp.sum(dmixed, axis=1, keepdims=True)
                dmbs.append(dmixed.astype(BF16))
            for dmb, vs in zip(dmbs, vss):
                dw_acc = dw_acc + lax.dot_general(dmb, vs, NT_DIMS, preferred_element_type=F32)
            for r, dmb in zip(rows, dmbs):
                dvs_s[r, :] = lax.dot_general(wt, dmb, TN_DIMS, preferred_element_type=F32)
            return dw_acc, db_acc, dgb_acc

        dw_acc, db_acc, dgb_acc = lax.fori_loop(
            0, n_chunks // per_trip, chunks,
            (jnp.zeros((TILE, TILE), F32), jnp.zeros((TILE, 1), F32), jnp.zeros((1, HEAD_DIM), F32)))
        dw_ref[...] = jnp.where(causal, dw_acc, 0.0)
        db_ref[...] = db_acc
        dgb_ref[...] = dgb_acc
        v_pre = v_ref[...]
        vg = _gelu(v_pre)
        rv = _rstd(vg)
        vhat = vg * rv
        dvs = dvs_s[...]
        dgs_ref[...] = jnp.sum(dvs * vhat, axis=0, keepdims=True)
        dv_ref[...] = (_norm_bwd(dvs, vhat, rv, gsv) * _gelu_grad(v_pre)).astype(BF16)

    col_blk = lambda off: pl.BlockSpec((s, HEAD_DIM), lambda g: (0, off + g))
    gvec = pl.BlockSpec((None, 1, HEAD_DIM), lambda g: (g, 0, 0))
    wspec = pl.BlockSpec((None, TILE, TILE), lambda g: (g, 0, 0))
    bspec = pl.BlockSpec((None, TILE, 1), lambda g: (g, 0, 0))
    blk = pl.BlockSpec((s, HEAD_DIM), lambda g: (0, g))
    big = jax.ShapeDtypeStruct((s, N_GROUPS * HEAD_DIM), BF16)
    gshape = jax.ShapeDtypeStruct((N_GROUPS, 1, HEAD_DIM), F32)
    return pl.pallas_call(
        body, name="sgu_bwd", grid=(N_GROUPS,),
        in_specs=[col_blk(col0), col_blk(col0 + N_GROUPS), gvec, wspec, bspec, gvec, col_blk(dm_col0)],
        out_specs=(blk, blk, gvec, wspec, bspec, gvec),
        out_shape=(big, big, gshape, jax.ShapeDtypeStruct((N_GROUPS, TILE, TILE), F32),
                   jax.ShapeDtypeStruct((N_GROUPS, TILE, 1), F32), gshape),
        scratch_shapes=[pltpu.VMEM((s, HEAD_DIM), BF16), pltpu.VMEM((s, HEAD_DIM), F32)],
        compiler_params=_cparams(("parallel",)),
    )(p, p, gs, w_s, b_s, gb, dmix)


SUBLANES = 8


def _shift_down(x, n):
    rolled = pltpu.roll(x, n, 0)
    edge = lax.broadcasted_iota(jnp.int32, (SUBLANES, x.shape[1]), 0)
    return jnp.concatenate([jnp.where(edge >= n, rolled[:SUBLANES], 0.0), rolled[SUBLANES:]], axis=0)


def _shift_up(x, n):
    s = x.shape[0]
    rolled = pltpu.roll(x, s - n, 0)
    edge = lax.broadcasted_iota(jnp.int32, (SUBLANES, x.shape[1]), 0)
    return jnp.concatenate([rolled[:s - SUBLANES], jnp.where(edge < SUBLANES - n, rolled[s - SUBLANES:], 0.0)], axis=0)


def _conv(x, w, b):
    x1, x2 = _shift_down(x, 1), _shift_down(x, 2)
    return b + w[0:1, :] * x2 + w[1:2, :] * x1 + w[2:3, :] * x, x1, x2


def _conv_specs(s, tn):
    xspec = pl.BlockSpec((2, s, tn), lambda j: (0, 0, j))
    wspec = pl.BlockSpec((2, CONV_WIDTH, tn), lambda j: (0, 0, j))
    bspec = pl.BlockSpec((2, 1, tn), lambda j: (0, 0, j))
    return xspec, wspec, bspec


def _conv_gate_fwd(up, cw, cb):
    _, s, f = up.shape
    tn = _pick(f, (256, 128))

    def body(x_ref, w_ref, b_ref, act_ref):
        gate = _conv(x_ref[0], w_ref[0], b_ref[0])[0]
        val = _conv(x_ref[1], w_ref[1], b_ref[1])[0]
        act_ref[...] = (gate * jax.nn.sigmoid(gate) * val).astype(BF16)

    xspec, wspec, bspec = _conv_specs(s, tn)
    return pl.pallas_call(
        body, name="conv_gate_fwd", grid=(f // tn,), in_specs=[xspec, wspec, bspec],
        out_specs=pl.BlockSpec((s, tn), lambda j: (0, j)), out_shape=jax.ShapeDtypeStruct((s, f), BF16),
        compiler_params=_cparams(("parallel",)),
    )(up, cw, cb)


def _up_conv_gate_fwd(h, wb, cw, cb, after=None):
    s, k = h.shape
    nb, _, bn = wb.shape
    hb = nb // 2
    f = hb * bn
    tm = _pick(s, (512, 256, 128))
    n_in = 5 + (after is not None)

    def body(*refs):
        h_ref, wg_ref, wv_ref, w_ref, b_ref = refs[:5]
        up_ref, act_ref, halo_s = refs[n_in:]
        first = pl.program_id(1) == 0
        outs = []
        for half, wt_ref in enumerate((wg_ref, wv_ref)):
            x = jnp.dot(h_ref[...], wt_ref[...], preferred_element_type=F32)
            up_ref[half] = x
            halo = jnp.where(first, 0.0, halo_s[half])
            halo_s[half] = x[tm - SUBLANES:]
            full = jnp.concatenate([halo, x], axis=0)
            x1 = pltpu.roll(full, 1, 0)[SUBLANES:]
            x2 = pltpu.roll(full, 2, 0)[SUBLANES:]
            w = w_ref[half]
            outs.append(b_ref[half] + w[0:1, :] * x2 + w[1:2, :] * x1 + w[2:3, :] * x)
        gate, val = outs
        act_ref[...] = (gate * jax.nn.sigmoid(gate) * val).astype(BF16)

    ins = [h, wb, wb, cw, cb]
    in_specs = [pl.BlockSpec((tm, k), lambda j, i: (i, 0)),
                pl.BlockSpec((None, k, bn), lambda j, i: (j, 0, 0)),
                pl.BlockSpec((None, k, bn), lambda j, i: (j + hb, 0, 0)),
                pl.BlockSpec((2, CONV_WIDTH, bn), lambda j, i: (0, 0, j)),
                pl.BlockSpec((2, 1, bn), lambda j, i: (0, 0, j))]
    if after is not None:
        ins.append(after)
        in_specs.append(pl.BlockSpec(after.shape, lambda j, i: (0, 0)))
    return pl.pallas_call(
        body, name="up_conv_gate_fwd", grid=(hb, s // tm), in_specs=in_specs,
        out_specs=(pl.BlockSpec((2, tm, bn), lambda j, i: (0, i, j)), pl.BlockSpec((tm, bn), lambda j, i: (i, j))),
        out_shape=(jax.ShapeDtypeStruct((2, s, f), F32), jax.ShapeDtypeStruct((s, f), BF16)),
        scratch_shapes=[pltpu.VMEM((2, SUBLANES, bn), F32)],
        compiler_params=_cparams(("parallel", "arbitrary"), 56),
    )(*ins)


def _down_dx_conv_gate_bwd(up, cw, cb, dy, wdown, after):
    _, s, f = up.shape
    d = dy.shape[1]
    tn = _pick(f, (256, 128))

    def body(x_ref, w_ref, b_ref, dy_ref, wd_ref, _, dx_ref, dw_ref, db_ref):
        da = lax.dot_general(dy_ref[...], wd_ref[...], NT_DIMS, preferred_element_type=F32)
        xg, xv = x_ref[0], x_ref[1]
        wg, wv = w_ref[0], w_ref[1]
        gate, xg1, xg2 = _conv(xg, wg, b_ref[0])
        val, xv1, xv2 = _conv(xv, wv, b_ref[1])
        sig = jax.nn.sigmoid(gate)
        dval = da * (gate * sig)
        dgate = da * val * (sig * (1.0 + gate * (1.0 - sig)))
        for half, (x, x1, x2, w, dz) in enumerate(((xg, xg1, xg2, wg, dgate), (xv, xv1, xv2, wv, dval))):
            dx_ref[half] = (w[2:3, :] * dz + w[1:2, :] * _shift_up(dz, 1) + w[0:1, :] * _shift_up(dz, 2)).astype(BF16)
            dw_ref[half, 0:1, :] = jnp.sum(dz * x2, axis=0, keepdims=True)
            dw_ref[half, 1:2, :] = jnp.sum(dz * x1, axis=0, keepdims=True)
            dw_ref[half, 2:3, :] = jnp.sum(dz * x, axis=0, keepdims=True)
            db_ref[half] = jnp.sum(dz, axis=0, keepdims=True)

    xspec, wspec, bspec = _conv_specs(s, tn)
    return pl.pallas_call(
        body, name="down_dx_conv_gate_bwd", grid=(f // tn,),
        in_specs=[xspec, wspec, bspec, pl.BlockSpec((s, d), lambda j: (0, 0)), pl.BlockSpec((tn, d), lambda j: (j, 0)),
                  pl.BlockSpec(after.shape, lambda j: (0, 0))],
        out_specs=(xspec, wspec, bspec),
        out_shape=(jax.ShapeDtypeStruct((2, s, f), BF16), jax.ShapeDtypeStruct((2, CONV_WIDTH, f), F32),
                   jax.ShapeDtypeStruct((2, 1, f), F32)),
        compiler_params=_cparams(("parallel",), 56),
    )(up, cw, cb, dy, wdown, after)


def _mesh_pos():
    return lax.axis_index("x"), lax.axis_index("y"), lax.axis_index("c")


def _remote(src, dst, send_sem, recv_sem, to):
    return pltpu.make_async_remote_copy(src_ref=src, dst_ref=dst, send_sem=send_sem, recv_sem=recv_sem,
                                        device_id=to, device_id_type=pl.DeviceIdType.MESH)


HBM_SPEC = pl.BlockSpec(memory_space=pltpu.HBM)
SEM_SPEC = pl.BlockSpec(memory_space=pltpu.SEMAPHORE)
ANY_SPEC = pl.BlockSpec(memory_space=pl.ANY)
TOKEN_SPEC = pl.BlockSpec(memory_space=pltpu.VMEM)
TOKEN_SHAPE = jax.ShapeDtypeStruct((8, 128), F32)
DATAFLOW = pltpu.SideEffectType.DATAFLOW_SIDE_EFFECTING
GATHER_PLANE = (2, 4, 6)


def _slot(pos):
    return 4 * pos[0] + 2 * pos[1] + pos[2]


def _flip(pos, k):
    return (pos[0] ^ ((k >> 2) & 1), pos[1] ^ ((k >> 1) & 1), pos[2] ^ (k & 1))


def _hbm(a):
    return pltpu.with_memory_space_constraint(a, pltpu.HBM)


def _hbm_shapes(arrays):
    return tuple(pltpu.HBM(a.shape, a.dtype) for a in arrays)


class _Split:
    def __init__(self, n, outs, has_sems):
        k = 2 * n if has_sems else 0
        self.n = n
        self.sems = list(outs[:k])
        self.bufs = list(outs[k:k + 2 * n])
        self.token = outs[-1]


def _split_call(name, body, bufs, sems_in, makes_sems, after):
    n = len(bufs) // 2
    k = 2 * n if makes_sems else 0
    m = len(sems_in)
    afters = list(after) if isinstance(after, (list, tuple)) else [after]
    na = len(afters)

    def wrapped(*refs):
        srcs, dsts = refs[:n], refs[n:2 * n]
        s_in = refs[2 * n:2 * n + m]
        s_out = refs[2 * n + m + na:2 * n + m + na + k]
        token, local_sems = refs[-2], refs[-1]
        body(srcs, dsts, s_in, s_out, local_sems)
        token[...] = jnp.zeros_like(token)

    outs = pl.pallas_call(
        wrapped, name=name,
        out_shape=(pltpu.SemaphoreType.DMA(()),) * k + _hbm_shapes(bufs) + (TOKEN_SHAPE,),
        in_specs=[HBM_SPEC] * (2 * n) + [SEM_SPEC] * m + [ANY_SPEC] * na,
        out_specs=(SEM_SPEC,) * k + (HBM_SPEC,) * (2 * n) + (TOKEN_SPEC,),
        input_output_aliases={i: k + i for i in range(2 * n)},
        scratch_shapes=[pltpu.SemaphoreType.DMA((n,))],
        compiler_params=pltpu.CompilerParams(has_side_effects=DATAFLOW),
    )(*[_hbm(b) for b in bufs], *sems_in, *afters)
    return _Split(n, outs, makes_sems)


def _wait_slots(land, count, send_sem, recv_sem, me, send=False, recv=False):
    span = land.at[pl.ds(0, count)]
    cp = _remote(span, span, send_sem, recv_sem, me)
    if send:
        cp.wait_send()
    if recv:
        cp.wait_recv()


def _gather_start(name, shards, after):
    n = len(shards)
    my_slot = _slot(_mesh_pos())
    lands = [lax.dynamic_update_slice(lax.empty((N_DEV,) + w.shape, w.dtype), w[None], (my_slot, 0, 0)) for w in shards]

    def body(srcs, dsts, _, sems, local_sems):
        me = _mesh_pos()
        for a in range(n):
            for k in (1,) + GATHER_PLANE:
                _remote(srcs[a], dsts[a].at[_slot(me)], sems[a], sems[n + a], _flip(me, k)).start()

    return _split_call(name, body, list(shards) + lands, [], True, after)


def _gather_forward(name, started, after):
    n = started.n

    def body(srcs, dsts, sems_a, sems_b, local_sems):
        me = _mesh_pos()
        sibling = _flip(me, 1)
        for a in range(n):
            _wait_slots(dsts[a], 4, sems_a[a], sems_a[n + a], me, recv=True)
            for k in GATHER_PLANE:
                block = dsts[a].at[_slot(_flip(me, k))]
                _remote(block, block, sems_b[a], sems_b[n + a], sibling).start()
        for a in range(n):
            _wait_slots(dsts[a], 4, sems_a[a], sems_a[n + a], me, send=True)

    return _split_call(name, body, started.bufs, started.sems, True, after)


def _gather_finish(name, forwarded, after):
    n = forwarded.n

    def body(srcs, dsts, sems_b, _, local_sems):
        me = _mesh_pos()
        for a in range(n):
            _wait_slots(dsts[a], 3, sems_b[a], sems_b[n + a], me, send=True, recv=True)

    return _split_call(name, body, forwarded.bufs, forwarded.sems, False, after).bufs[n:]


def _exchange_start(name, blocked, after):
    n = len(blocked)
    my_slot = _slot(_mesh_pos())
    rows = [w.shape[-2] // (N_DEV if w.ndim == 2 else 1) for w in blocked]

    def block(ref, a, slot):
        if len(ref.shape) == 3:
            return ref.at[slot]
        return ref.at[pl.ds(pl.multiple_of(slot * rows[a], 16), rows[a])]

    lands = []
    for w, r in zip(blocked, rows):
        mine = lax.dynamic_slice_in_dim(w, my_slot, 1, 0) if w.ndim == 3 else lax.dynamic_slice_in_dim(w, my_slot * r, r, 0)[None]
        lands.append(lax.dynamic_update_slice(lax.empty((N_DEV, r, w.shape[-1]), w.dtype), mine, (my_slot, 0, 0)))

    def body(srcs, dsts, _, sems, local_sems):
        me = _mesh_pos()
        for a in range(n):
            for k in range(1, N_DEV):
                peer = _flip(me, k)
                _remote(block(srcs[a], a, _slot(peer)), dsts[a].at[_slot(me)], sems[a], sems[n + a], peer).start()

    return _split_call(name, body, list(blocked) + lands, [], True, after)


def _exchange_finish(name, started, after):
    n = started.n

    def body(srcs, dsts, sems, _, local_sems):
        me = _mesh_pos()
        for a in range(n):
            _wait_slots(dsts[a], N_DEV - 1, sems[a], sems[n + a], me, send=True, recv=True)

    return _split_call(name, body, started.bufs, started.sems, False, after).bufs[n:]


def _broadcast_start(name, arrays, after):
    n = len(arrays)
    my_slot = _slot(_mesh_pos())
    lands = [lax.dynamic_update_slice(lax.empty((N_DEV,) + w.shape, w.dtype), w[None], (my_slot, 0, 0)) for w in arrays]

    def body(srcs, dsts, _, sems, local_sems):
        me = _mesh_pos()
        for a in range(n):
            for k in range(1, N_DEV):
                _remote(srcs[a], dsts[a].at[_slot(me)], sems[a], sems[n + a], _flip(me, k)).start()

    return _split_call(name, body, list(arrays) + lands, [], True, after)


def _adamw_math(w, g, m, v):
    m = ADAM_B1 * m + (1.0 - ADAM_B1) * g
    v = ADAM_B2 * v + (1.0 - ADAM_B2) * (g * g)
    m_hat = m / (1.0 - ADAM_B1 ** ADAM_STEP)
    v_hat = v / (1.0 - ADAM_B2 ** ADAM_STEP)
    delta = -ADAM_LR * (m_hat / (jnp.sqrt(v_hat) + ADAM_EPS) + ADAM_WD * w)
    return delta, m, v


def _adamw(name, w, m, v, parts, layer, prev=None):
    _, r, c = w.shape
    tr = _pick(r, tuple(t for t in (256, 128, 64, 32, 16) if t * c <= ADAMW_TILE_ELEMS))
    n_prev = 0 if prev is None else 4

    def body(*refs):
        w_ref, m_ref, v_ref, p_ref = refs[:4]
        g_ref, d_ref, nm_ref, nv_ref = refs[4 + n_prev:]
        g = p_ref[0].astype(F32)
        for src in range(1, N_DEV):
            g = g + p_ref[src].astype(F32)
        delta, nm, nv = _adamw_math(w_ref[...], g, m_ref[...], v_ref[...])
        g_ref[...] = g
        d_ref[...] = delta
        nm_ref[...] = nm
        nv_ref[...] = nv

    wspec = pl.BlockSpec((None, tr, c), lambda i: (layer, i, 0))
    pspec = pl.BlockSpec((N_DEV, tr, c), lambda i: (0, i, 0))
    shp = jax.ShapeDtypeStruct(w.shape, F32)
    return pl.pallas_call(
        body, name=name, grid=(r // tr,), in_specs=[wspec] * 3 + [pspec] + [ANY_SPEC] * n_prev,
        out_specs=(wspec,) * 4, out_shape=(shp,) * 4, input_output_aliases={4 + j: j for j in range(n_prev)},
        compiler_params=_cparams(("parallel",)),
    )(w, m, v, parts, *([] if prev is None else prev))


PACK_TILE = 8 * 128


def _pack(arrays):
    flat = []
    for a in arrays:
        v = a.reshape(-1)
        pad = (-v.shape[0]) % PACK_TILE
        flat.append(jnp.pad(v, (0, pad)) if pad else v)
    return jnp.concatenate(flat).reshape(-1, 128)


def _unpack(buf, like):
    flat = buf.reshape(-1)
    out, off = [], 0
    for a in like:
        n = 1
        for dim in a.shape:
            n *= dim
        out.append(flat[off:off + n].reshape(a.shape))
        off += n + (-n) % PACK_TILE
    return out


def _sum_slots(name, gathered):
    _, r, c = gathered.shape

    def body(x_ref, o_ref):
        acc = x_ref[0].astype(F32)
        for src in range(1, N_DEV):
            acc = acc + x_ref[src].astype(F32)
        o_ref[...] = acc

    return pl.pallas_call(body, name=name, out_shape=jax.ShapeDtypeStruct((r, c), F32))(gathered)


def _adamw_small(w, g, m, v):
    shp = jax.ShapeDtypeStruct(w.shape, F32)

    def body(w_ref, g_ref, m_ref, v_ref, d_ref, nm_ref, nv_ref):
        delta, nm, nv = _adamw_math(w_ref[...], g_ref[...], m_ref[...], v_ref[...])
        d_ref[...] = delta
        nm_ref[...] = nm
        nv_ref[...] = nv

    return pl.pallas_call(body, name="adamw_small", out_shape=(shp,) * 3)(w, g, m, v)


def kernel(x, attn_norm_g, w_in, q_norm_g, k_norm_g, sgu_norm_g, sgu_w, sgu_b, out_norm_a_g, out_norm_b_g, w_out, ffn_norm_g, w_up, conv_w, conv_b, w_down, loss_target, m_attn_norm_g, m_w_in, m_q_norm_g, m_k_norm_g, m_sgu_norm_g, m_sgu_w, m_sgu_b, m_out_norm_a_g, m_out_norm_b_g, m_w_out, m_ffn_norm_g, m_w_up, m_conv_w, m_conv_b, m_w_down, v_attn_norm_g, v_w_in, v_q_norm_g, v_k_norm_g, v_sgu_norm_g, v_sgu_w, v_sgu_b, v_out_norm_a_g, v_out_norm_b_g, v_w_out, v_ffn_norm_g, v_w_up, v_conv_w, v_conv_b, v_w_down):
    depth = w_in.shape[0]
    s, d = x.shape[1], x.shape[2]
    n_heads = (d // 2) // HEAD_DIM
    sgu_col0 = 3 * n_heads
    f2 = w_up.shape[2] * N_DEV
    ff = f2 // 2
    my_slot = 4 * lax.axis_index("x") + 2 * lax.axis_index("y") + lax.axis_index("c")

    wb = [(w_in[l].astype(BF16), w_out[l].astype(BF16), w_up[l].astype(BF16), w_down[l].astype(BF16))
          for l in range(depth)]
    groups = {"in0": [wb[0][0]], "out0": [wb[0][1], conv_w.reshape(depth * CONV_WIDTH, -1)], "up0": [wb[0][2]],
              "down0": [wb[0][3]]}
    for l in range(1, depth):
        groups[f"in{l}"] = [wb[l][0], wb[l][1]]
        groups[f"ffn{l}"] = [wb[l][2], wb[l][3]]
    token = attn_norm_g
    started = {}
    for gname, group in groups.items():
        started[gname] = _gather_start(f"gather_{gname}_start", group, token)
        token = started[gname].token

    def forward(gname, after):
        return _gather_forward(f"gather_{gname}_forward", started[gname], after)

    def finish(gname, forwarded, after):
        return _gather_finish(f"gather_{gname}_finish", forwarded, after)

    conv_b_all = conv_b.reshape(depth, 2, 1, ff)
    sgu_b_col = sgu_b[..., None]
    fwd_in = forward("in0", token)
    win_g = finish("in0", fwd_in, fwd_in.token)[0]

    xs = x[0]
    saved = []
    gathered = []
    for l in range(depth):
        g1 = attn_norm_g[l][None]
        g2 = ffn_norm_g[l][None]
        gq, gk = q_norm_g[l][None], k_norm_g[l][None]
        ga = out_norm_a_g[l][:, None, :]
        gs = sgu_norm_g[l][:, None, :]
        gb = out_norm_b_g[l][:, None, :]
        h1 = _rmsnorm_fwd("attn_norm_fwd", xs, g1)
        p = _mm_nn_blocked("in_proj", h1, win_g, F32)
        att, o, rsum = _attn_fwd(p, gq, gk, ga, n_heads)
        if l == 0:
            fwd_out = forward("out0", att)
            fwd_up = forward("up0", fwd_out.token)
        else:
            fwd_up = forward(f"ffn{l}", att)
        sg = _sgu_fwd(p, gs, sgu_w[l], sgu_b_col[l], gb, sgu_col0, fwd_up.token)
        mix = jnp.concatenate([att, sg], axis=-1)
        if l == 0:
            wout_g, cw = finish("out0", fwd_out, mix)
            cw = jnp.transpose(cw.reshape(N_DEV, depth, CONV_WIDTH, -1), (1, 2, 0, 3)).reshape(depth, CONV_WIDTH, 2, ff)
            conv_w_all = jnp.transpose(cw, (0, 2, 1, 3))
        x1 = _mm_nn_res("out_proj", mix, wout_g.reshape(d, d), xs)
        h2 = _rmsnorm_fwd("ffn_norm_fwd", x1, g2)
        if l == 0:
            wup_g = finish("up0", fwd_up, h2)[0]
            fwd_down = forward("down0", wup_g)
            up, act = _up_conv_gate_fwd(h2, wup_g, conv_w_all[l], conv_b_all[l], after=fwd_down.token)
            wdown_g = finish("down0", fwd_down, up)[0]
        else:
            wup_g, wdown_g = finish(f"ffn{l}", fwd_up, h2)
            up, act = _up_conv_gate_fwd(h2, wup_g, conv_w_all[l], conv_b_all[l])
        saved.append((xs, h1, p, o, rsum, mix, x1, h2, up, act))
        gathered.append((win_g, wout_g, wup_g, wdown_g))
        if l + 1 < depth:
            fwd_in = forward(f"in{l + 1}", act)
            x2 = _mm_nn_res("down_proj", act, wdown_g.reshape(ff, d), x1, after=fwd_in.token)
            win_g, wout_g = finish(f"in{l + 1}", fwd_in, x2)
        else:
            x2 = _mm_nn_res("down_proj", act, wdown_g.reshape(ff, d), x1)
        xs = x2

    dx, dxb, loss_vec = _loss_head(xs, loss_target[0])
    loss = lax.psum(loss_vec[0, 0], MESH_AXES)

    exchanges = []
    small = [None] * depth
    small_names = ["attn_norm_g", "q_norm_g", "k_norm_g", "sgu_norm_g", "sgu_w", "sgu_b", "out_norm_a_g",
                   "out_norm_b_g", "ffn_norm_g", "conv_b", "conv_w"]
    for l in reversed(range(depth)):
        xs0, h1, p, o, rsum, mix, x1, h2, up, act = saved[l]
        win_g, wout_g, wup_g, wdown_g = gathered[l]
        wout_full = wout_g.reshape(d, d)
        wdown_full = wdown_g.reshape(ff, d)
        g1 = attn_norm_g[l][None]
        g2 = ffn_norm_g[l][None]
        gq, gk = q_norm_g[l][None], k_norm_g[l][None]
        ga = out_norm_a_g[l][:, None, :]
        gs = sgu_norm_g[l][:, None, :]
        gb = out_norm_b_g[l][:, None, :]
        d_wdown = _mm_tn_plain("down_proj_dw", act, dxb)
        exchanges.append((l, "down", ("w_down",), _exchange_start(f"grad_down{l}_start", [d_wdown], dx)))
        dup, d_cw, d_cb = _down_dx_conv_gate_bwd(up, conv_w_all[l], conv_b_all[l], dxb, wdown_full,
                                                 exchanges[-1][3].token)
        d_wup = _mm_tn_blocked("up_proj_dw", h2, dup, N_DEV, halves=True)
        exchanges.append((l, "up", ("w_up",), _exchange_start(f"grad_up{l}_start", [d_wup], d_cb)))
        dh2 = _mm_nt_blocked("up_proj_dx", dup, wup_g, halves=True, after=exchanges[-1][3].token)
        dx, dxb, d_g2 = _rmsnorm_bwd("ffn_norm_bwd", dh2, x1, g2, dx)
        d_wout = _mm_tn_plain("out_proj_dw", mix, dxb)
        dmix = _mm_nt_plain("out_proj_dx", dxb, wout_full)
        dq, dk, dv, d_gq, d_gk, d_ga = _attn_bwd(p, gq, gk, ga, o, rsum, dmix, n_heads)
        du, dvs, d_gs, d_sw, d_sb, d_gb = _sgu_bwd(p, gs, sgu_w[l], sgu_b_col[l], gb, dmix, sgu_col0, n_heads)
        dp = jnp.concatenate([dq, dk, dv, du, dvs], axis=-1)
        d_win = _mm_tn_blocked("in_proj_dw", h1, dp, N_DEV)
        exchanges.append((l, "mix", ("w_in", "w_out"), _exchange_start(f"grad_mix{l}_start", [d_win, d_wout], d_gq)))
        dh1 = _mm_nt_blocked("in_proj_dx", dp, win_g, after=exchanges[-1][3].token)
        dx, dxb, d_g1 = _rmsnorm_bwd("attn_norm_bwd", dh1, xs0, g1, dx)
        small[l] = dict(attn_norm_g=d_g1[0], q_norm_g=d_gq[0], k_norm_g=d_gk[0], sgu_norm_g=d_gs[:, 0], sgu_w=d_sw,
                        sgu_b=d_sb[..., 0], out_norm_a_g=d_ga[:, 0], out_norm_b_g=d_gb[:, 0], ffn_norm_g=d_g2[0],
                        conv_w=jnp.transpose(d_cw, (1, 0, 2)).reshape(CONV_WIDTH, f2), conv_b=d_cb.reshape(f2))
    grad_x = dx[None]

    f32_names = [n for n in small_names if n != "sgu_w"]
    small_g = [jnp.stack([small[l][n] for l in range(depth)]) for n in f32_names]
    sgu_w_g = jnp.stack([small[l]["sgu_w"] for l in range(depth)])
    small_sent = _broadcast_start("grad_small_start", [_pack(small_g), sgu_w_g.reshape(-1, TILE).astype(BF16)], dx)

    res = {}
    big = dict(w_in=(w_in, m_w_in, v_w_in), w_out=(w_out, m_w_out, v_w_out), w_up=(w_up, m_w_up, v_w_up),
               w_down=(w_down, m_w_down, v_w_down))
    after = [small_sent.token]
    for l, stage, names, ex in exchanges:
        landed = _exchange_finish(f"grad_{stage}{l}_finish", ex, after)
        after = []
        for name, parts in zip(names, landed):
            w, m, v = big[name]
            res[name] = _adamw(f"adamw_{name}", w, m, v, parts, l, res.get(name))
            after.append(res[name][0])
    small_all, sgu_w_all = _exchange_finish("grad_small_finish", small_sent, after)
    small_sum = _unpack(_sum_slots("small_grad_sum", small_all), small_g)
    g_small = dict(zip(f32_names, small_sum))
    g_small["sgu_w"] = _sum_slots("sgu_w_grad_sum", sgu_w_all).reshape(sgu_w.shape)
    cwn = conv_w.shape[2]
    g_small["conv_w"] = lax.dynamic_slice_in_dim(g_small["conv_w"], my_slot * cwn, cwn, axis=2)
    small_w = dict(attn_norm_g=(attn_norm_g, m_attn_norm_g, v_attn_norm_g), q_norm_g=(q_norm_g, m_q_norm_g, v_q_norm_g),
                   k_norm_g=(k_norm_g, m_k_norm_g, v_k_norm_g), sgu_norm_g=(sgu_norm_g, m_sgu_norm_g, v_sgu_norm_g),
                   sgu_w=(sgu_w, m_sgu_w, v_sgu_w), sgu_b=(sgu_b, m_sgu_b, v_sgu_b),
                   out_norm_a_g=(out_norm_a_g, m_out_norm_a_g, v_out_norm_a_g),
                   out_norm_b_g=(out_norm_b_g, m_out_norm_b_g, v_out_norm_b_g),
                   ffn_norm_g=(ffn_norm_g, m_ffn_norm_g, v_ffn_norm_g), conv_b=(conv_b, m_conv_b, v_conv_b),
                   conv_w=(conv_w, m_conv_w, v_conv_w))
    like = [small_w[n][0] for n in small_names]
    pw = _pack([small_w[n][0] for n in small_names])
    pm = _pack([small_w[n][1] for n in small_names])
    pv = _pack([small_w[n][2] for n in small_names])
    pg = _pack([g_small[n].reshape(small_w[n][0].shape) for n in small_names])
    pd, pnm, pnv = _adamw_small(pw, pg, pm, pv)
    for n, dlt, nm, nv in zip(small_names, _unpack(pd, like), _unpack(pnm, like), _unpack(pnv, like)):
        res[n] = (g_small[n].reshape(small_w[n][0].shape), dlt, nm, nv)

    order = ["attn_norm_g", "w_in", "q_norm_g", "k_norm_g", "sgu_norm_g", "sgu_w", "sgu_b", "out_norm_a_g",
             "out_norm_b_g", "w_out", "ffn_norm_g", "w_up", "conv_w", "conv_b", "w_down"]
    outs = [loss, grad_x]
    for field in range(4):
        outs += [res[n][field] for n in order]
    return tuple(outs)
```

```python
import functools

import jax
import jax.numpy as jnp
from jax import lax
from jax.experimental import pallas as pl
from jax.experimental.pallas import tpu as pltpu

F32 = jnp.float32
BF16 = jnp.bfloat16
EPS = 1e-6
HEAD_DIM = 128
TILE = 128
ATTN_VMEM_MB = 58
ATTN_HEADS_PER_STEP = 4
N_GROUPS = 8
CONV_WIDTH = 3
N_DEV = 8
MESH_AXES = ("x", "y", "c")
MIB = 1024 * 1024

ADAM_LR = 0.001
ADAM_B1 = 0.9
ADAM_B2 = 0.999
ADAM_EPS = 1e-08
ADAM_WD = 0.01
ADAM_STEP = 10
ADAMW_TILE_ELEMS = 160 * 1024

NT_DIMS = (((1,), (1,)), ((), ()))
NN_DIMS = (((1,), (0,)), ((), ()))
TN_DIMS = (((0,), (0,)), ((), ()))


def _cparams(sem, vmem_mb=48):
    return pltpu.CompilerParams(dimension_semantics=sem, vmem_limit_bytes=vmem_mb * MIB)


def _pick(n, cands):
    for c in cands:
        if n % c == 0:
            return c
    return n


def _mm(name, grid, ins, in_specs, out_shape, out_spec, dims, has_res=False, parts=None, vmem_mb=56, after=None):
    n_in = 2 + has_res + (after is not None)
    if after is not None:
        ins = tuple(ins) + (after,)
        in_specs = list(in_specs) + [pl.BlockSpec(after.shape, lambda *_: (0, 0))]

    def body(*refs):
        a_ref, b_ref = refs[:2]
        o_ref = refs[n_in]
        if parts is None:
            acc = lax.dot_general(a_ref[...], b_ref[...], dims, preferred_element_type=F32)
        else:
            acc = None
            for part in parts:
                a, b = part(a_ref, b_ref)
                prod = lax.dot_general(a, b, dims, preferred_element_type=F32)
                acc = prod if acc is None else acc + prod
        if has_res:
            acc = acc + refs[2][...]
        o_ref[...] = acc.astype(o_ref.dtype)

    return pl.pallas_call(
        body, name=name, grid=grid, in_specs=in_specs, out_specs=out_spec, out_shape=out_shape,
        compiler_params=_cparams(("parallel",) * len(grid), vmem_mb),
    )(*ins)


def _rows_for(m, row_bytes, budget):
    return _pick(m, tuple(t for t in (2048, 1024, 512, 256, 128) if t * row_bytes <= budget))


def _mm_nn_blocked(name, a, wb, out_dtype, halves=False, after=None):
    m, k = a.shape
    nb, _, bn = wb.shape
    tm = _rows_for(m, bn * jnp.dtype(out_dtype).itemsize, 6 * MIB)
    a_spec = pl.BlockSpec((tm, k), lambda j, i: (i, 0))
    b_spec = pl.BlockSpec((None, k, bn), lambda j, i: (j, 0, 0))
    if halves:
        hb = nb // 2
        out_shape = jax.ShapeDtypeStruct((2, m, hb * bn), out_dtype)
        o_spec = pl.BlockSpec((None, tm, bn), lambda j, i: (j // hb, i, j % hb))
    else:
        out_shape = jax.ShapeDtypeStruct((m, nb * bn), out_dtype)
        o_spec = pl.BlockSpec((tm, bn), lambda j, i: (i, j))
    return _mm(name, (nb, m // tm), (a, wb), [a_spec, b_spec], out_shape, o_spec, NN_DIMS, after=after)


def _mm_nn_res(name, a, w, res, after=None):
    m, k = a.shape
    n = w.shape[1]
    tm = _pick(m, (512, 256, 128))
    tn = _rows_for(n, k * 2, 12 * MIB)
    a_spec = pl.BlockSpec((tm, k), lambda j, i: (i, 0))
    b_spec = pl.BlockSpec((k, tn), lambda j, i: (0, j))
    r_spec = pl.BlockSpec((tm, tn), lambda j, i: (i, j))
    o_spec = pl.BlockSpec((tm, tn), lambda j, i: (i, j))
    return _mm(name, (n // tn, m // tm), (a, w, res), [a_spec, b_spec, r_spec], jax.ShapeDtypeStruct((m, n), F32),
               o_spec, NN_DIMS, has_res=True, after=after)


def _mm_nt_blocked(name, dy, wb, halves=False, after=None):
    nb, n, bn = wb.shape
    m = dy.shape[-2]
    tm = _pick(m, (512, 256, 128))
    tn = _rows_for(n, nb * bn * 2, 12 * MIB)
    if halves:
        hb = nb // 2
        a_spec = pl.BlockSpec((2, tm, hb * bn), lambda j, i: (0, i, 0))
        a_part = lambda kk: (lambda a_ref: a_ref[kk // hb, :, (kk % hb) * bn:(kk % hb + 1) * bn])
    else:
        a_spec = pl.BlockSpec((tm, nb * bn), lambda j, i: (i, 0))
        a_part = lambda kk: (lambda a_ref: a_ref[:, kk * bn:(kk + 1) * bn])
    parts = [(lambda a_ref, b_ref, kk=kk, sel=a_part(kk): (sel(a_ref), b_ref[kk])) for kk in range(nb)]
    b_spec = pl.BlockSpec((nb, tn, bn), lambda j, i: (0, j, 0))
    o_spec = pl.BlockSpec((tm, tn), lambda j, i: (i, j))
    return _mm(name, (n // tn, m // tm), (dy, wb), [a_spec, b_spec], jax.ShapeDtypeStruct((m, n), F32), o_spec,
               NT_DIMS, parts=parts, after=after)


def _mm_nt_plain(name, dy, w, out_dtype=F32):
    m, k = dy.shape
    n = w.shape[0]
    tm = _rows_for(m, k * 2, 8 * MIB)
    tn = _pick(n, (512, 256, 128))
    a_spec = pl.BlockSpec((tm, k), lambda j, i: (i, 0))
    b_spec = pl.BlockSpec((tn, k), lambda j, i: (j, 0))
    o_spec = pl.BlockSpec((tm, tn), lambda j, i: (i, j))
    return _mm(name, (n // tn, m // tm), (dy, w), [a_spec, b_spec], jax.ShapeDtypeStruct((m, n), out_dtype), o_spec,
               NT_DIMS)


def _mm_tn_blocked(name, a, dy, nb, halves=False):
    s, k1 = a.shape
    bn = (dy.shape[-1] * (2 if halves else 1)) // nb
    tm = _rows_for(k1, bn * 2, 6 * MIB)
    a_spec = pl.BlockSpec((s, tm), lambda j, i: (0, i))
    if halves:
        hb = nb // 2
        b_spec = pl.BlockSpec((None, s, bn), lambda j, i: (j // hb, 0, j % hb))
    else:
        b_spec = pl.BlockSpec((s, bn), lambda j, i: (0, j))
    o_spec = pl.BlockSpec((None, tm, bn), lambda j, i: (j, i, 0))
    return _mm(name, (nb, k1 // tm), (a, dy), [a_spec, b_spec], jax.ShapeDtypeStruct((nb, k1, bn), BF16), o_spec,
               TN_DIMS)


def _mm_tn_plain(name, a, dy):
    s, k1 = a.shape
    n = dy.shape[1]
    tm = _pick(k1, (512, 256, 128))
    tn = _rows_for(n, s * 2, 8 * MIB)
    a_spec = pl.BlockSpec((s, tm), lambda i, j: (0, i))
    b_spec = pl.BlockSpec((s, tn), lambda i, j: (0, j))
    o_spec = pl.BlockSpec((tm, tn), lambda i, j: (i, j))
    return _mm(name, (k1 // tm, n // tn), (a, dy), [a_spec, b_spec], jax.ShapeDtypeStruct((k1, n), BF16), o_spec,
               TN_DIMS)


def _rstd(x):
    return lax.rsqrt(jnp.mean(x * x, axis=-1, keepdims=True) + EPS)


def _norm_bwd(dy, xhat, r, g):
    dxhat = dy * g
    return r * (dxhat - xhat * jnp.mean(dxhat * xhat, axis=-1, keepdims=True))


def _rmsnorm_fwd(name, x, g):
    s, d = x.shape
    tr = _pick(s, (256, 128))

    def body(x_ref, g_ref, h_ref):
        xv = x_ref[...]
        h_ref[...] = (xv * _rstd(xv) * g_ref[...]).astype(BF16)

    return pl.pallas_call(
        body, name=name, grid=(s // tr,),
        in_specs=[pl.BlockSpec((tr, d), lambda i: (i, 0)), pl.BlockSpec((1, d), lambda i: (0, 0))],
        out_specs=pl.BlockSpec((tr, d), lambda i: (i, 0)),
        out_shape=jax.ShapeDtypeStruct((s, d), BF16), compiler_params=_cparams(("parallel",)),
    )(x, g)


def _rmsnorm_bwd(name, dh, x, g, dres):
    s, d = x.shape
    tr = _pick(s, (256, 128))

    def body(dh_ref, x_ref, g_ref, dres_ref, dx_ref, dxb_ref, dg_ref):
        xv = x_ref[...]
        r = _rstd(xv)
        xhat = xv * r
        dhv = dh_ref[...]
        dx = dres_ref[...] + _norm_bwd(dhv, xhat, r, g_ref[...])
        dx_ref[...] = dx
        dxb_ref[...] = dx.astype(BF16)
        part = jnp.sum(dhv * xhat, axis=0, keepdims=True)

        @pl.when(pl.program_id(0) == 0)
        def _():
            dg_ref[...] = part

        @pl.when(pl.program_id(0) > 0)
        def _():
            dg_ref[...] += part

    row = pl.BlockSpec((tr, d), lambda i: (i, 0))
    vec = pl.BlockSpec((1, d), lambda i: (0, 0))
    return pl.pallas_call(
        body, name=name, grid=(s // tr,), in_specs=[row, row, vec, row], out_specs=(row, row, vec),
        out_shape=(jax.ShapeDtypeStruct((s, d), F32), jax.ShapeDtypeStruct((s, d), BF16),
                   jax.ShapeDtypeStruct((1, d), F32)),
        compiler_params=_cparams(("arbitrary",)),
    )(dh, x, g, dres)


def _loss_head(y, target):
    s, d = y.shape
    tr = _pick(s, (256, 128))

    def body(y_ref, t_ref, dy_ref, dyb_ref, loss_ref):
        err = y_ref[...] - t_ref[...]
        dy = err * (1.0 / d)
        dy_ref[...] = dy
        dyb_ref[...] = dy.astype(BF16)
        part = 0.5 * jnp.sum(jnp.mean(err * err, axis=-1, keepdims=True), axis=0, keepdims=True)
        part = jnp.broadcast_to(part, (1, 128))

        @pl.when(pl.program_id(0) == 0)
        def _():
            loss_ref[...] = part

        @pl.when(pl.program_id(0) > 0)
        def _():
            loss_ref[...] += part

    row = pl.BlockSpec((tr, d), lambda i: (i, 0))
    return pl.pallas_call(
        body, name="loss_head", grid=(s // tr,), in_specs=[row, row],
        out_specs=(row, row, pl.BlockSpec((1, 128), lambda i: (0, 0))),
        out_shape=(jax.ShapeDtypeStruct((s, d), F32), jax.ShapeDtypeStruct((s, d), BF16),
                   jax.ShapeDtypeStruct((1, 128), F32)),
        compiler_params=_cparams(("arbitrary",)),
    )(y, target)


def _split_dot(x, tri):
    hi = x.astype(BF16)
    lo = (x - hi.astype(F32)).astype(BF16)
    return (jnp.dot(hi, tri, preferred_element_type=F32) + jnp.dot(lo, tri, preferred_element_type=F32))


def _tile_iotas():
    row = lax.broadcasted_iota(jnp.int32, (TILE, TILE), 0)
    col = lax.broadcasted_iota(jnp.int32, (TILE, TILE), 1)
    return row, col


def _sb_logits(qi, kb, mask):
    z = lax.dot_general(qi, kb, NT_DIMS, preferred_element_type=F32) * (HEAD_DIM ** -0.5)
    sp = jnp.log1p(jnp.exp(-jnp.abs(z)))
    lb = jnp.minimum(z, 0.0) - sp
    l1m = -jnp.maximum(z, 0.0) - sp
    if mask is not None:
        l1m = jnp.where(mask, l1m, 0.0)
    return lb, l1m


def _attn_fwd(p, gq, gk, ga, n_heads):
    s = p.shape[0]
    nq = s // TILE

    hp = ATTN_HEADS_PER_STEP
    wd = hp * HEAD_DIM

    def body(q_ref, k_ref, v_ref, gq_ref, gk_ref, ga_ref, att_ref, o_ref, r_ref, qn_s, kn_s, vb_s):
        heads = [slice(hh * HEAD_DIM, (hh + 1) * HEAD_DIM) for hh in range(hp)]
        for hd in heads:
            qv = q_ref[:, hd]
            qn_s[:, hd] = (qv * _rstd(qv) * gq_ref[...]).astype(BF16)
            kv = k_ref[:, hd]
            kn_s[:, hd] = (kv * _rstd(kv) * gk_ref[...]).astype(BF16)
        vb_s[...] = v_ref[...].astype(BF16)
        row, col = _tile_iotas()
        causal = col < row
        upper_ones = jnp.concatenate([(row > col).astype(BF16), jnp.ones((TILE, TILE), BF16)], axis=1)

        def tiles(rows, key_blocks, states, mask):
            chains = [(hi, hd, keys) for hi, hd in enumerate(heads) for keys in key_blocks]
            logits = [_sb_logits(qn_s[rows, hd], kn_s[keys, hd], mask) for _, hd, keys in chains]
            sums = [_split_dot(l1m, upper_ones) for _, l1m in logits]
            carry = [c for _, c in states]
            probs = []
            for (hi, _, _), (lb, _), sm in zip(chains, logits, sums):
                a = jnp.exp(lb + sm[:, :TILE] + carry[hi])
                carry[hi] = carry[hi] + sm[:, TILE:]
                probs.append((a if mask is None else jnp.where(mask, a, 0.0)).astype(BF16))
            outs = [jnp.dot(a, vb_s[keys, hd], preferred_element_type=F32) for a, (_, hd, keys) in zip(probs, chains)]
            acc = [o_acc for o_acc, _ in states]
            for (hi, _, _), o in zip(chains, outs):
                acc[hi] = acc[hi] + o
            return tuple(zip(acc, carry))

        def key_block(b):
            return pl.ds(pl.multiple_of(b * TILE, TILE), TILE)

        def qblock(i, _):
            rows = pl.ds(pl.multiple_of(i * TILE, TILE), TILE)
            zero = jnp.zeros((TILE, HEAD_DIM), F32)
            states = tiles(rows, [rows], tuple((zero, zero) for _ in heads), causal)
            states = lax.cond(i % 2 == 1, lambda st: tiles(rows, [key_block(i - 1)], st, None), lambda st: st, states)
            top = i - i % 2

            def kblocks(jj, states):
                return tiles(rows, [key_block(top - 1 - 2 * jj), key_block(top - 2 - 2 * jj)], states, None)

            states = lax.fori_loop(0, i // 2, kblocks, states)
            for hh, (hd, (o_acc, c)) in enumerate(zip(heads, states)):
                o_ref[rows, hd] = o_acc
                r_ref[rows, hd] = c
                att_ref[rows, hd] = (o_acc * _rstd(o_acc) * ga_ref[hh]).astype(BF16)
            return 0

        lax.fori_loop(0, nq, qblock, 0)

    col_blk = lambda off: pl.BlockSpec((s, wd), lambda h: (0, off + h))
    vec = pl.BlockSpec((1, HEAD_DIM), lambda h: (0, 0))
    hvec = pl.BlockSpec((hp, 1, HEAD_DIM), lambda h: (h, 0, 0))
    out = pl.BlockSpec((s, wd), lambda h: (0, h))
    w = n_heads * HEAD_DIM
    steps = n_heads // hp
    return pl.pallas_call(
        body, name="attn_fwd", grid=(steps,),
        in_specs=[col_blk(0), col_blk(steps), col_blk(2 * steps), vec, vec, hvec],
        out_specs=(out, out, out),
        out_shape=(jax.ShapeDtypeStruct((s, w), BF16), jax.ShapeDtypeStruct((s, w), F32),
                   jax.ShapeDtypeStruct((s, w), F32)),
        scratch_shapes=[pltpu.VMEM((s, wd), BF16)] * 3,
        compiler_params=_cparams(("parallel",), ATTN_VMEM_MB),
    )(p, p, p, gq, gk, ga)


def _attn_bwd(p, gq, gk, ga, o, rsum, dmix, n_heads):
    s = p.shape[0]
    nq = s // TILE

    hp = ATTN_HEADS_PER_STEP
    wd = hp * HEAD_DIM
    scale = HEAD_DIM ** -0.5

    def body(q_ref, k_ref, v_ref, gq_ref, gk_ref, ga_ref, o_ref, r_ref, dm_ref,
             dq_ref, dk_ref, dv_ref, dgq_ref, dgk_ref, dga_ref,
             qn_s, kn_s, vb_s, do_s, dqn_s, dkn_s, dv_s):
        step = pl.program_id(0)
        gqv, gkv = gq_ref[...], gk_ref[...]
        heads = [slice(hh * HEAD_DIM, (hh + 1) * HEAD_DIM) for hh in range(hp)]
        for hh, hd in enumerate(heads):
            qv = q_ref[:, hd]
            qn_s[:, hd] = (qv * _rstd(qv) * gqv).astype(BF16)
            kv = k_ref[:, hd]
            kn_s[:, hd] = (kv * _rstd(kv) * gkv).astype(BF16)
            ov = o_ref[:, hd]
            ro = _rstd(ov)
            ohat = ov * ro
            dm = dm_ref[:, hd]
            dga_ref[hh] = jnp.sum(dm * ohat, axis=0, keepdims=True)
            do_s[:, hd] = _norm_bwd(dm, ohat, ro, ga_ref[hh]).astype(BF16)
        vb_s[...] = v_ref[...].astype(BF16)
        dkn_s[...] = jnp.zeros_like(dkn_s)
        dv_s[...] = jnp.zeros_like(dv_s)
        row, col = _tile_iotas()
        causal = col < row
        ones = jnp.ones((TILE, TILE), BF16)
        incl_ones = jnp.concatenate([(row <= col).astype(BF16), ones], axis=1)
        excl_ones = jnp.concatenate([(row < col).astype(BF16), ones], axis=1)

        def tiles(rows, key_blocks, states, mask):
            chains = [(hi, hd, keys) for hi, hd in enumerate(heads) for keys in key_blocks]
            qis = [qn_s[rows, hd] for hd in heads]
            dois = [do_s[rows, hd] for hd in heads]
            logits = [_sb_logits(qis[hi], kn_s[keys, hd], mask) for hi, hd, keys in chains]
            sums = [_split_dot(l1m, incl_ones) for _, l1m in logits]
            das = [lax.dot_general(dois[hi], vb_s[keys, hd], NT_DIMS, preferred_element_type=F32)
                   for hi, hd, keys in chains]
            pfx = [st[1] for st in states]
            probs, dss = [], []
            for (hi, hd, _), (lb, _), sm, da in zip(chains, logits, sums, das):
                a = jnp.exp(lb + (r_ref[rows, hd] - pfx[hi] - sm[:, :TILE]))
                pfx[hi] = pfx[hi] + sm[:, TILE:]
                a = a if mask is None else jnp.where(mask, a, 0.0)
                probs.append(a.astype(BF16))
                dss.append(da * a)
            dsums = [_split_dot(ds, excl_ones) for ds in dss]
            pc = [st[2] for st in states]
            dzs = []
            for (hi, _, _), (lb, _), ds, dsm in zip(chains, logits, dss, dsums):
                dl1m = pc[hi] + dsm[:, :TILE]
                pc[hi] = pc[hi] + dsm[:, TILE:]
                dl1m = dl1m if mask is None else jnp.where(mask, dl1m, 0.0)
                beta = jnp.exp(lb)
                dzs.append(((ds * (1.0 - beta) - dl1m * beta) * scale).astype(BF16))
            dqs = [jnp.dot(dz, kn_s[keys, hd], preferred_element_type=F32) for dz, (_, hd, keys) in zip(dzs, chains)]
            for dz, a, (hi, hd, keys) in zip(dzs, probs, chains):
                dkn_s[keys, hd] += lax.dot_general(dz, qis[hi], TN_DIMS, preferred_element_type=F32)
                dv_s[keys, hd] += lax.dot_general(a, dois[hi], TN_DIMS, preferred_element_type=F32)
            dq_acc = [st[0] for st in states]
            for (hi, _, _), dq in zip(chains, dqs):
                dq_acc[hi] = dq_acc[hi] + dq
            return tuple(zip(dq_acc, pfx, pc))

        def key_block(b):
            return pl.ds(pl.multiple_of(b * TILE, TILE), TILE)

        def qblock(i, _):
            rows = pl.ds(pl.multiple_of(i * TILE, TILE), TILE)
            zero = jnp.zeros((TILE, HEAD_DIM), F32)

            def kblocks(jj, states):
                return tiles(rows, [key_block(2 * jj), key_block(2 * jj + 1)], states, None)

            states = lax.fori_loop(0, i // 2, kblocks, tuple((zero, zero, zero) for _ in heads))
            states = lax.cond(i % 2 == 1, lambda st: tiles(rows, [key_block(i - 1)], st, None), lambda st: st, states)
            states = tiles(rows, [rows], states, causal)
            for hd, (dq_acc, _, _) in zip(heads, states):
                dqn_s[rows, hd] = dq_acc
            return 0

        lax.fori_loop(0, nq, qblock, 0)

        def norm_in_bwd(x_ref, g, dn_s, dx_ref, dg_ref):
            part = jnp.zeros((1, HEAD_DIM), F32)
            for hd in heads:
                xv = x_ref[:, hd]
                r = _rstd(xv)
                xhat = xv * r
                dn = dn_s[:, hd]
                dx_ref[:, hd] = _norm_bwd(dn, xhat, r, g).astype(BF16)
                part = part + jnp.sum(dn * xhat, axis=0, keepdims=True)

            @pl.when(step == 0)
            def _():
                dg_ref[...] = part

            @pl.when(step > 0)
            def _():
                dg_ref[...] += part

        norm_in_bwd(q_ref, gqv, dqn_s, dq_ref, dgq_ref)
        norm_in_bwd(k_ref, gkv, dkn_s, dk_ref, dgk_ref)
        dv_ref[...] = dv_s[...].astype(BF16)

    once = pl.Buffered(1)
    steps = n_heads // hp
    col_blk = lambda off: pl.BlockSpec((s, wd), lambda h: (0, off + h), pipeline_mode=once)
    vec = pl.BlockSpec((1, HEAD_DIM), lambda h: (0, 0))
    hvec = pl.BlockSpec((hp, 1, HEAD_DIM), lambda h: (h, 0, 0))
    blk = pl.BlockSpec((s, wd), lambda h: (0, h), pipeline_mode=once)
    w = n_heads * HEAD_DIM
    big = jax.ShapeDtypeStruct((s, w), BF16)
    return pl.pallas_call(
        body, name="attn_bwd", grid=(steps,),
        in_specs=[col_blk(0), col_blk(steps), col_blk(2 * steps), vec, vec, hvec, blk, blk, blk],
        out_specs=(blk, blk, blk, vec, vec, hvec),
        out_shape=(big, big, big, jax.ShapeDtypeStruct((1, HEAD_DIM), F32), jax.ShapeDtypeStruct((1, HEAD_DIM), F32),
                   jax.ShapeDtypeStruct((n_heads, 1, HEAD_DIM), F32)),
        scratch_shapes=[pltpu.VMEM((s, wd), BF16)] * 4 + [pltpu.VMEM((s, wd), F32)] * 3,
        compiler_params=_cparams(("arbitrary",), ATTN_VMEM_MB),
    )(p, p, p, gq, gk, ga, o, rsum, dmix)


_INV_SQRT2 = 0.7071067811865476
_INV_SQRT_2PI = 0.3989422804014327


def _gelu(x):
    return 0.5 * x * (1.0 + lax.erf(x * _INV_SQRT2))


def _gelu_grad(x):
    return 0.5 * (1.0 + lax.erf(x * _INV_SQRT2)) + x * (_INV_SQRT_2PI * jnp.exp(-0.5 * x * x))


def _sgu_fwd(p, gs, w_s, b_s, gb, col0, after):
    s = p.shape[0]
    n_chunks = s // TILE
    per_trip = _pick(n_chunks, (4, 2, 1))

    def body(u_ref, v_ref, gs_ref, w_ref, b_ref, gb_ref, _, out_ref, vs_s):
        vg = _gelu(v_ref[...])
        vs_s[...] = (vg * _rstd(vg) * gs_ref[...]).astype(BF16)
        row, col = _tile_iotas()
        wt = jnp.where(col <= row, w_ref[...], 0.0).astype(BF16)
        bcol = b_ref[...]
        gbv = gb_ref[...]

        def chunks(c, _):
            rows = [pl.ds(pl.multiple_of((c * per_trip + k) * TILE, TILE), TILE) for k in range(per_trip)]
            mixed = [jnp.dot(wt, vs_s[r, :], preferred_element_type=F32) + bcol for r in rows]
            sgs = [_gelu(u_ref[r, :]) * mx for r, mx in zip(rows, mixed)]
            for r, sg in zip(rows, sgs):
                out_ref[r, :] = (sg * _rstd(sg) * gbv).astype(BF16)
            return 0

        lax.fori_loop(0, n_chunks // per_trip, chunks, 0)

    col_blk = lambda off: pl.BlockSpec((s, HEAD_DIM), lambda g: (0, off + g))
    gvec = pl.BlockSpec((None, 1, HEAD_DIM), lambda g: (g, 0, 0))
    return pl.pallas_call(
        body, name="sgu_fwd", grid=(N_GROUPS,),
        in_specs=[col_blk(col0), col_blk(col0 + N_GROUPS), gvec,
                  pl.BlockSpec((None, TILE, TILE), lambda g: (g, 0, 0)),
                  pl.BlockSpec((None, TILE, 1), lambda g: (g, 0, 0)), gvec,
                  pl.BlockSpec(after.shape, lambda g: (0, 0))],
        out_specs=pl.BlockSpec((s, HEAD_DIM), lambda g: (0, g)),
        out_shape=jax.ShapeDtypeStruct((s, N_GROUPS * HEAD_DIM), BF16),
        scratch_shapes=[pltpu.VMEM((s, HEAD_DIM), BF16)],
        compiler_params=_cparams(("parallel",)),
    )(p, p, gs, w_s, b_s, gb, after)


def _sgu_bwd(p, gs, w_s, b_s, gb, dmix, col0, dm_col0):
    s = p.shape[0]
    n_chunks = s // TILE
    per_trip = _pick(n_chunks, (4, 2, 1))

    def body(u_ref, v_ref, gs_ref, w_ref, b_ref, gb_ref, dm_ref,
             du_ref, dv_ref, dgs_ref, dw_ref, db_ref, dgb_ref, vs_s, dvs_s):
        gsv = gs_ref[...]
        gbv = gb_ref[...]
        vg = _gelu(v_ref[...])
        vs_s[...] = (vg * _rstd(vg) * gsv).astype(BF16)
        row, col = _tile_iotas()
        causal = col <= row
        wt = jnp.where(causal, w_ref[...], 0.0).astype(BF16)
        bcol = b_ref[...]

        def chunks(c, carry):
            dw_acc, db_acc, dgb_acc = carry
            rows = [pl.ds(pl.multiple_of((c * per_trip + k) * TILE, TILE), TILE) for k in range(per_trip)]
            vss = [vs_s[r, :] for r in rows]
            mixed = [jnp.dot(wt, vs, preferred_element_type=F32) + bcol for vs in vss]
            dmbs = []
            for r, mx in zip(rows, mixed):
                u_pre = u_ref[r, :]
                u = _gelu(u_pre)
                sg = u * mx
                rs = _rstd(sg)
                sghat = sg * rs
                dm = dm_ref[r, :]
                dsg = _norm_bwd(dm, sghat, rs, gbv)
                dgb_acc = dgb_acc + jnp.sum(dm * sghat, axis=0, keepdims=True)
                du_ref[r, :] = (dsg * mx * _gelu_grad(u_pre)).astype(BF16)
                dmixed = dsg * u
                db_acc = db_acc + jnp.sum(dmixed, axis=1, keepdims=True)
                dmbs.append(dmixed.astype(BF16))
            for dmb, vs in zip(dmbs, vss):
                dw_acc = dw_acc + lax.dot_general(dmb, vs, NT_DIMS, preferred_element_type=F32)
            for r, dmb in zip(rows, dmbs):
                dvs_s[r, :] = lax.dot_general(wt, dmb, TN_DIMS, preferred_element_type=F32)
            return dw_acc, db_acc, dgb_acc

        dw_acc, db_acc, dgb_acc = lax.fori_loop(
            0, n_chunks // per_trip, chunks,
            (jnp.zeros((TILE, TILE), F32), jnp.zeros((TILE, 1), F32), jnp.zeros((1, HEAD_DIM), F32)))
        dw_ref[...] = jnp.where(causal, dw_acc, 0.0)
        db_ref[...] = db_acc
        dgb_ref[...] = dgb_acc
        v_pre = v_ref[...]
        vg = _gelu(v_pre)
        rv = _rstd(vg)
        vhat = vg * rv
        dvs = dvs_s[...]
        dgs_ref[...] = jnp.sum(dvs * vhat, axis=0, keepdims=True)
        dv_ref[...] = (_norm_bwd(dvs, vhat, rv, gsv) * _gelu_grad(v_pre)).astype(BF16)

    col_blk = lambda off: pl.BlockSpec((s, HEAD_DIM), lambda g: (0, off + g))
    gvec = pl.BlockSpec((None, 1, HEAD_DIM), lambda g: (g, 0, 0))
    wspec = pl.BlockSpec((None, TILE, TILE), lambda g: (g, 0, 0))
    bspec = pl.BlockSpec((None, TILE, 1), lambda g: (g, 0, 0))
    blk = pl.BlockSpec((s, HEAD_DIM), lambda g: (0, g))
    big = jax.ShapeDtypeStruct((s, N_GROUPS * HEAD_DIM), BF16)
    gshape = jax.ShapeDtypeStruct((N_GROUPS, 1, HEAD_DIM), F32)
    return pl.pallas_call(
        body, name="sgu_bwd", grid=(N_GROUPS,),
        in_specs=[col_blk(col0), col_blk(col0 + N_GROUPS), gvec, wspec, bspec, gvec, col_blk(dm_col0)],
        out_specs=(blk, blk, gvec, wspec, bspec, gvec),
        out_shape=(big, big, gshape, jax.ShapeDtypeStruct((N_GROUPS, TILE, TILE), F32),
                   jax.ShapeDtypeStruct((N_GROUPS, TILE, 1), F32), gshape),
        scratch_shapes=[pltpu.VMEM((s, HEAD_DIM), BF16), pltpu.VMEM((s, HEAD_DIM), F32)],
        compiler_params=_cparams(("parallel",)),
    )(p, p, gs, w_s, b_s, gb, dmix)


SUBLANES = 8


def _shift_down(x, n):
    rolled = pltpu.roll(x, n, 0)
    edge = lax.broadcasted_iota(jnp.int32, (SUBLANES, x.shape[1]), 0)
    return jnp.concatenate([jnp.where(edge >= n, rolled[:SUBLANES], 0.0), rolled[SUBLANES:]], axis=0)


def _shift_up(x, n):
    s = x.shape[0]
    rolled = pltpu.roll(x, s - n, 0)
    edge = lax.broadcasted_iota(jnp.int32, (SUBLANES, x.shape[1]), 0)
    return jnp.concatenate([rolled[:s - SUBLANES], jnp.where(edge < SUBLANES - n, rolled[s - SUBLANES:], 0.0)], axis=0)


def _conv(x, w, b):
    x1, x2 = _shift_down(x, 1), _shift_down(x, 2)
    return b + w[0:1, :] * x2 + w[1:2, :] * x1 + w[2:3, :] * x, x1, x2


def _conv_specs(s, tn):
    xspec = pl.BlockSpec((2, s, tn), lambda j: (0, 0, j))
    wspec = pl.BlockSpec((2, CONV_WIDTH, tn), lambda j: (0, 0, j))
    bspec = pl.BlockSpec((2, 1, tn), lambda j: (0, 0, j))
    return xspec, wspec, bspec


def _conv_gate_fwd(up, cw, cb):
    _, s, f = up.shape
    tn = _pick(f, (256, 128))

    def body(x_ref, w_ref, b_ref, act_ref):
        gate = _conv(x_ref[0], w_ref[0], b_ref[0])[0]
        val = _conv(x_ref[1], w_ref[1], b_ref[1])[0]
        act_ref[...] = (gate * jax.nn.sigmoid(gate) * val).astype(BF16)

    xspec, wspec, bspec = _conv_specs(s, tn)
    return pl.pallas_call(
        body, name="conv_gate_fwd", grid=(f // tn,), in_specs=[xspec, wspec, bspec],
        out_specs=pl.BlockSpec((s, tn), lambda j: (0, j)), out_shape=jax.ShapeDtypeStruct((s, f), BF16),
        compiler_params=_cparams(("parallel",)),
    )(up, cw, cb)


def _up_conv_gate_fwd(h, wb, cw, cb, after=None):
    s, k = h.shape
    nb, _, bn = wb.shape
    hb = nb // 2
    f = hb * bn
    tm = _pick(s, (512, 256, 128))
    n_in = 5 + (after is not None)

    def body(*refs):
        h_ref, wg_ref, wv_ref, w_ref, b_ref = refs[:5]
        up_ref, act_ref, halo_s = refs[n_in:]
        first = pl.program_id(1) == 0
        outs = []
        for half, wt_ref in enumerate((wg_ref, wv_ref)):
            x = jnp.dot(h_ref[...], wt_ref[...], preferred_element_type=F32)
            up_ref[half] = x
            halo = jnp.where(first, 0.0, halo_s[half])
            halo_s[half] = x[tm - SUBLANES:]
            full = jnp.concatenate([halo, x], axis=0)
            x1 = pltpu.roll(full, 1, 0)[SUBLANES:]
            x2 = pltpu.roll(full, 2, 0)[SUBLANES:]
            w = w_ref[half]
            outs.append(b_ref[half] + w[0:1, :] * x2 + w[1:2, :] * x1 + w[2:3, :] * x)
        gate, val = outs
        act_ref[...] = (gate * jax.nn.sigmoid(gate) * val).astype(BF16)

    ins = [h, wb, wb, cw, cb]
    in_specs = [pl.BlockSpec((tm, k), lambda j, i: (i, 0)),
                pl.BlockSpec((None, k, bn), lambda j, i: (j, 0, 0)),
                pl.BlockSpec((None, k, bn), lambda j, i: (j + hb, 0, 0)),
                pl.BlockSpec((2, CONV_WIDTH, bn), lambda j, i: (0, 0, j)),
                pl.BlockSpec((2, 1, bn), lambda j, i: (0, 0, j))]
    if after is not None:
        ins.append(after)
        in_specs.append(pl.BlockSpec(after.shape, lambda j, i: (0, 0)))
    return pl.pallas_call(
        body, name="up_conv_gate_fwd", grid=(hb, s // tm), in_specs=in_specs,
        out_specs=(pl.BlockSpec((2, tm, bn), lambda j, i: (0, i, j)), pl.BlockSpec((tm, bn), lambda j, i: (i, j))),
        out_shape=(jax.ShapeDtypeStruct((2, s, f), F32), jax.ShapeDtypeStruct((s, f), BF16)),
        scratch_shapes=[pltpu.VMEM((2, SUBLANES, bn), F32)],
        compiler_params=_cparams(("parallel", "arbitrary"), 56),
    )(*ins)


def _down_dx_conv_gate_bwd(up, cw, cb, dy, wdown, after):
    _, s, f = up.shape
    d = dy.shape[1]
    tn = _pick(f, (256, 128))

    def body(x_ref, w_ref, b_ref, dy_ref, wd_ref, _, dx_ref, dw_ref, db_ref):
        da = lax.dot_general(dy_ref[...], wd_ref[...], NT_DIMS, preferred_element_type=F32)
        xg, xv = x_ref[0], x_ref[1]
        wg, wv = w_ref[0], w_ref[1]
        gate, xg1, xg2 = _conv(xg, wg, b_ref[0])
        val, xv1, xv2 = _conv(xv, wv, b_ref[1])
        sig = jax.nn.sigmoid(gate)
        dval = da * (gate * sig)
        dgate = da * val * (sig * (1.0 + gate * (1.0 - sig)))
        for half, (x, x1, x2, w, dz) in enumerate(((xg, xg1, xg2, wg, dgate), (xv, xv1, xv2, wv, dval))):
            dx_ref[half] = (w[2:3, :] * dz + w[1:2, :] * _shift_up(dz, 1) + w[0:1, :] * _shift_up(dz, 2)).astype(BF16)
            dw_ref[half, 0:1, :] = jnp.sum(dz * x2, axis=0, keepdims=True)
            dw_ref[half, 1:2, :] = jnp.sum(dz * x1, axis=0, keepdims=True)
            dw_ref[half, 2:3, :] = jnp.sum(dz * x, axis=0, keepdims=True)
            db_ref[half] = jnp.sum(dz, axis=0, keepdims=True)

    xspec, wspec, bspec = _conv_specs(s, tn)
    return pl.pallas_call(
        body, name="down_dx_conv_gate_bwd", grid=(f // tn,),
        in_specs=[xspec, wspec, bspec, pl.BlockSpec((s, d), lambda j: (0, 0)), pl.BlockSpec((tn, d), lambda j: (j, 0)),
                  pl.BlockSpec(after.shape, lambda j: (0, 0))],
        out_specs=(xspec, wspec, bspec),
        out_shape=(jax.ShapeDtypeStruct((2, s, f), BF16), jax.ShapeDtypeStruct((2, CONV_WIDTH, f), F32),
                   jax.ShapeDtypeStruct((2, 1, f), F32)),
        compiler_params=_cparams(("parallel",), 56),
    )(up, cw, cb, dy, wdown, after)


def _mesh_pos():
    return lax.axis_index("x"), lax.axis_index("y"), lax.axis_index("c")


def _remote(src, dst, send_sem, recv_sem, to):
    return pltpu.make_async_remote_copy(src_ref=src, dst_ref=dst, send_sem=send_sem, recv_sem=recv_sem,
                                        device_id=to, device_id_type=pl.DeviceIdType.MESH)


HBM_SPEC = pl.BlockSpec(memory_space=pltpu.HBM)
SEM_SPEC = pl.BlockSpec(memory_space=pltpu.SEMAPHORE)
ANY_SPEC = pl.BlockSpec(memory_space=pl.ANY)
TOKEN_SPEC = pl.BlockSpec(memory_space=pltpu.VMEM)
TOKEN_SHAPE = jax.ShapeDtypeStruct((8, 128), F32)
DATAFLOW = pltpu.SideEffectType.DATAFLOW_SIDE_EFFECTING
BF16_SUBLANES = 16


def _slot(pos):
    return 4 * pos[0] + 2 * pos[1] + pos[2]


def _flip(pos, k):
    return (pos[0] ^ ((k >> 2) & 1), pos[1] ^ ((k >> 1) & 1), pos[2] ^ (k & 1))


def _hbm(a):
    return pltpu.with_memory_space_constraint(a, pltpu.HBM)


def _hbm_shapes(arrays):
    return tuple(pltpu.HBM(a.shape, a.dtype) for a in arrays)


class _Split:
    def __init__(self, n, outs, n_sets):
        k = 2 * n * int(n_sets)
        self.n = n
        self.sems = list(outs[:k])
        self.bufs = list(outs[k:k + 2 * n])
        self.token = outs[-1]

    def sem_set(self, i):
        return self.sems[2 * self.n * i:2 * self.n * (i + 1)]


def _split_call(name, body, bufs, sems_in, n_sets, after):
    n = len(bufs) // 2
    k = 2 * n * int(n_sets)
    m = len(sems_in)
    afters = list(after) if isinstance(after, (list, tuple)) else [after]
    na = len(afters)

    def wrapped(*refs):
        srcs, dsts = refs[:n], refs[n:2 * n]
        s_in = refs[2 * n:2 * n + m]
        s_out = refs[2 * n + m + na:2 * n + m + na + k]
        token, local_sems = refs[-2], refs[-1]
        body(srcs, dsts, s_in, s_out, local_sems)
        token[...] = jnp.zeros_like(token)

    outs = pl.pallas_call(
        wrapped, name=name,
        out_shape=(pltpu.SemaphoreType.DMA(()),) * k + _hbm_shapes(bufs) + (TOKEN_SHAPE,),
        in_specs=[HBM_SPEC] * (2 * n) + [SEM_SPEC] * m + [ANY_SPEC] * na,
        out_specs=(SEM_SPEC,) * k + (HBM_SPEC,) * (2 * n) + (TOKEN_SPEC,),
        input_output_aliases={i: k + i for i in range(2 * n)},
        scratch_shapes=[pltpu.SemaphoreType.DMA((n,))],
        compiler_params=pltpu.CompilerParams(has_side_effects=DATAFLOW),
    )(*[_hbm(b) for b in bufs], *sems_in, *afters)
    return _Split(n, outs, n_sets)


def _wait_slots(land, count, send_sem, recv_sem, me, send=False, recv=False):
    span = land.at[pl.ds(0, count)]
    cp = _remote(span, span, send_sem, recv_sem, me)
    if send:
        cp.wait_send()
    if recv:
        cp.wait_recv()


X_FLIP, Y_FLIP, DIAG_FLIP = 4, 2, 6


def _gather_start(name, shards, after):
    n = len(shards)
    my_slot = _slot(_mesh_pos())
    lands = [lax.dynamic_update_slice(lax.empty((N_DEV,) + w.shape, w.dtype), w[None], (my_slot, 0, 0)) for w in shards]

    def body(srcs, dsts, _, sems, local_sems):
        me = _mesh_pos()
        for a in range(n):
            for k in (1, X_FLIP, Y_FLIP):
                _remote(srcs[a], dsts[a].at[_slot(me)], sems[a], sems[n + a], _flip(me, k)).start()

    return _split_call(name, body, list(shards) + lands, [], 1, after)


def _gather_relay(name, started, after):
    n = started.n

    def body(srcs, dsts, sems_a, sems_out, local_sems):
        me = _mesh_pos()
        sibling = _flip(me, 1)
        pass_on, relay = sems_out[:2 * n], sems_out[2 * n:]
        for a in range(n):
            _wait_slots(dsts[a], 3, sems_a[a], sems_a[n + a], me, recv=True)
            from_x = dsts[a].at[_slot(_flip(me, X_FLIP))]
            from_y = dsts[a].at[_slot(_flip(me, Y_FLIP))]
            for block in (from_x, from_y):
                _remote(block, block, pass_on[a], pass_on[n + a], sibling).start()
            rows = srcs[a].shape[0]
            if rows % (2 * BF16_SUBLANES) == 0:
                top, bottom = pl.ds(0, rows // 2), pl.ds(rows // 2, rows // 2)
                _remote(from_y.at[top], from_y.at[top], relay[a], relay[n + a], _flip(me, X_FLIP)).start()
                _remote(from_x.at[bottom], from_x.at[bottom], relay[a], relay[n + a], _flip(me, Y_FLIP)).start()
            else:
                _remote(from_y, from_y, relay[a], relay[n + a], _flip(me, X_FLIP)).start()
        for a in range(n):
            _wait_slots(dsts[a], 3, sems_a[a], sems_a[n + a], me, send=True)

    return _split_call(name, body, started.bufs, started.sems, 2, after)


def _gather_relay_diagonal(name, relayed, after):
    n = relayed.n

    def body(srcs, dsts, relay, pass_on, local_sems):
        me = _mesh_pos()
        for a in range(n):
            _wait_slots(dsts[a], 1, relay[a], relay[n + a], me, recv=True)
            block = dsts[a].at[_slot(_flip(me, DIAG_FLIP))]
            _remote(block, block, pass_on[a], pass_on[n + a], _flip(me, 1)).start()
        for a in range(n):
            _wait_slots(dsts[a], 1, relay[a], relay[n + a], me, send=True)

    return _split_call(name, body, relayed.bufs, relayed.sem_set(1), 1, after)


def _gather_finish(name, relayed, diagonal, after):
    n = relayed.n

    def body(srcs, dsts, sems, _, local_sems):
        me = _mesh_pos()
        first, second = sems[:2 * n], sems[2 * n:]
        for a in range(n):
            _wait_slots(dsts[a], 2, first[a], first[n + a], me, send=True, recv=True)
            _wait_slots(dsts[a], 1, second[a], second[n + a], me, send=True, recv=True)

    return _split_call(name, body, diagonal.bufs, relayed.sem_set(0) + diagonal.sems, 0, after).bufs[n:]


def _exchange_start(name, blocked, after):
    n = len(blocked)
    my_slot = _slot(_mesh_pos())
    rows = [w.shape[-2] // (N_DEV if w.ndim == 2 else 1) for w in blocked]

    def block(ref, a, slot):
        if len(ref.shape) == 3:
            return ref.at[slot]
        return ref.at[pl.ds(pl.multiple_of(slot * rows[a], 16), rows[a])]

    lands = []
    for w, r in zip(blocked, rows):
        mine = lax.dynamic_slice_in_dim(w, my_slot, 1, 0) if w.ndim == 3 else lax.dynamic_slice_in_dim(w, my_slot * r, r, 0)[None]
        lands.append(lax.dynamic_update_slice(lax.empty((N_DEV, r, w.shape[-1]), w.dtype), mine, (my_slot, 0, 0)))

    def body(srcs, dsts, _, sems, local_sems):
        me = _mesh_pos()
        for a in range(n):
            for k in range(1, N_DEV):
                peer = _flip(me, k)
                _remote(block(srcs[a], a, _slot(peer)), dsts[a].at[_slot(me)], sems[a], sems[n + a], peer).start()

    return _split_call(name, body, list(blocked) + lands, [], True, after)


def _exchange_finish(name, started, after):
    n = started.n

    def body(srcs, dsts, sems, _, local_sems):
        me = _mesh_pos()
        for a in range(n):
            _wait_slots(dsts[a], N_DEV - 1, sems[a], sems[n + a], me, send=True, recv=True)

    return _split_call(name, body, started.bufs, started.sems, False, after).bufs[n:]


def _broadcast_start(name, arrays, after):
    n = len(arrays)
    my_slot = _slot(_mesh_pos())
    lands = [lax.dynamic_update_slice(lax.empty((N_DEV,) + w.shape, w.dtype), w[None], (my_slot, 0, 0)) for w in arrays]

    def body(srcs, dsts, _, sems, local_sems):
        me = _mesh_pos()
        for a in range(n):
            for k in range(1, N_DEV):
                _remote(srcs[a], dsts[a].at[_slot(me)], sems[a], sems[n + a], _flip(me, k)).start()

    return _split_call(name, body, list(arrays) + lands, [], True, after)


def _adamw_math(w, g, m, v):
    m = ADAM_B1 * m + (1.0 - ADAM_B1) * g
    v = ADAM_B2 * v + (1.0 - ADAM_B2) * (g * g)
    m_hat = m / (1.0 - ADAM_B1 ** ADAM_STEP)
    v_hat = v / (1.0 - ADAM_B2 ** ADAM_STEP)
    delta = -ADAM_LR * (m_hat / (jnp.sqrt(v_hat) + ADAM_EPS) + ADAM_WD * w)
    return delta, m, v


def _adamw(name, w, m, v, parts, layer, prev=None):
    _, r, c = w.shape
    tr = _pick(r, tuple(t for t in (256, 128, 64, 32, 16) if t * c <= ADAMW_TILE_ELEMS))
    n_prev = 0 if prev is None else 4

    def body(*refs):
        w_ref, m_ref, v_ref, p_ref = refs[:4]
        g_ref, d_ref, nm_ref, nv_ref = refs[4 + n_prev:]
        g = p_ref[0].astype(F32)
        for src in range(1, N_DEV):
            g = g + p_ref[src].astype(F32)
        delta, nm, nv = _adamw_math(w_ref[...], g, m_ref[...], v_ref[...])
        g_ref[...] = g
        d_ref[...] = delta
        nm_ref[...] = nm
        nv_ref[...] = nv

    wspec = pl.BlockSpec((None, tr, c), lambda i: (layer, i, 0))
    pspec = pl.BlockSpec((N_DEV, tr, c), lambda i: (0, i, 0))
    shp = jax.ShapeDtypeStruct(w.shape, F32)
    return pl.pallas_call(
        body, name=name, grid=(r // tr,), in_specs=[wspec] * 3 + [pspec] + [ANY_SPEC] * n_prev,
        out_specs=(wspec,) * 4, out_shape=(shp,) * 4, input_output_aliases={4 + j: j for j in range(n_prev)},
        compiler_params=_cparams(("parallel",)),
    )(w, m, v, parts, *([] if prev is None else prev))


PACK_TILE = 8 * 128


def _pack(arrays):
    flat = []
    for a in arrays:
        v = a.reshape(-1)
        pad = (-v.shape[0]) % PACK_TILE
        flat.append(jnp.pad(v, (0, pad)) if pad else v)
    return jnp.concatenate(flat).reshape(-1, 128)


def _unpack(buf, like):
    flat = buf.reshape(-1)
    out, off = [], 0
    for a in like:
        n = 1
        for dim in a.shape:
            n *= dim
        out.append(flat[off:off + n].reshape(a.shape))
        off += n + (-n) % PACK_TILE
    return out


def _sum_slots(name, gathered):
    _, r, c = gathered.shape

    def body(x_ref, o_ref):
        acc = x_ref[0].astype(F32)
        for src in range(1, N_DEV):
            acc = acc + x_ref[src].astype(F32)
        o_ref[...] = acc

    return pl.pallas_call(body, name=name, out_shape=jax.ShapeDtypeStruct((r, c), F32))(gathered)


def _adamw_small(w, g, m, v):
    shp = jax.ShapeDtypeStruct(w.shape, F32)

    def body(w_ref, g_ref, m_ref, v_ref, d_ref, nm_ref, nv_ref):
        delta, nm, nv = _adamw_math(w_ref[...], g_ref[...], m_ref[...], v_ref[...])
        d_ref[...] = delta
        nm_ref[...] = nm
        nv_ref[...] = nv

    return pl.pallas_call(body, name="adamw_small", out_shape=(shp,) * 3)(w, g, m, v)


def kernel(x, attn_norm_g, w_in, q_norm_g, k_norm_g, sgu_norm_g, sgu_w, sgu_b, out_norm_a_g, out_norm_b_g, w_out, ffn_norm_g, w_up, conv_w, conv_b, w_down, loss_target, m_attn_norm_g, m_w_in, m_q_norm_g, m_k_norm_g, m_sgu_norm_g, m_sgu_w, m_sgu_b, m_out_norm_a_g, m_out_norm_b_g, m_w_out, m_ffn_norm_g, m_w_up, m_conv_w, m_conv_b, m_w_down, v_attn_norm_g, v_w_in, v_q_norm_g, v_k_norm_g, v_sgu_norm_g, v_sgu_w, v_sgu_b, v_out_norm_a_g, v_out_norm_b_g, v_w_out, v_ffn_norm_g, v_w_up, v_conv_w, v_conv_b, v_w_down):
    depth = w_in.shape[0]
    s, d = x.shape[1], x.shape[2]
    n_heads = (d // 2) // HEAD_DIM
    sgu_col0 = 3 * n_heads
    f2 = w_up.shape[2] * N_DEV
    ff = f2 // 2
    my_slot = 4 * lax.axis_index("x") + 2 * lax.axis_index("y") + lax.axis_index("c")

    wb = [(w_in[l].astype(BF16), w_out[l].astype(BF16), w_up[l].astype(BF16), w_down[l].astype(BF16))
          for l in range(depth)]
    groups = {"in0": [wb[0][0]], "out0": [wb[0][1], conv_w.reshape(depth * CONV_WIDTH, -1)], "up0": [wb[0][2]],
              "down0": [wb[0][3]]}
    for l in range(1, depth):
        groups[f"in{l}"] = [wb[l][0], wb[l][1]]
        groups[f"ffn{l}"] = [wb[l][2], wb[l][3]]
    token = attn_norm_g
    started = {}
    for gname, group in groups.items():
        started[gname] = _gather_start(f"gather_{gname}_start", group, token)
        token = started[gname].token

    def relay(gname, after):
        return _gather_relay(f"gather_{gname}_relay", started[gname], after)

    def finish(gname, relayed, after):
        diagonal = _gather_relay_diagonal(f"gather_{gname}_diagonal", relayed, after)
        return _gather_finish(f"gather_{gname}_finish", relayed, diagonal, diagonal.token)

    conv_b_all = conv_b.reshape(depth, 2, 1, ff)
    sgu_b_col = sgu_b[..., None]
    fwd_in = relay("in0", token)
    win_g = finish("in0", fwd_in, fwd_in.token)[0]

    xs = x[0]
    saved = []
    gathered = []
    for l in range(depth):
        g1 = attn_norm_g[l][None]
        g2 = ffn_norm_g[l][None]
        gq, gk = q_norm_g[l][None], k_norm_g[l][None]
        ga = out_norm_a_g[l][:, None, :]
        gs = sgu_norm_g[l][:, None, :]
        gb = out_norm_b_g[l][:, None, :]
        h1 = _rmsnorm_fwd("attn_norm_fwd", xs, g1)
        p = _mm_nn_blocked("in_proj", h1, win_g, F32)
        att, o, rsum = _attn_fwd(p, gq, gk, ga, n_heads)
        if l == 0:
            fwd_out = relay("out0", att)
            fwd_up = relay("up0", fwd_out.token)
        else:
            fwd_up = relay(f"ffn{l}", att)
        sg = _sgu_fwd(p, gs, sgu_w[l], sgu_b_col[l], gb, sgu_col0, fwd_up.token)
        mix = jnp.concatenate([att, sg], axis=-1)
        if l == 0:
            wout_g, cw = finish("out0", fwd_out, mix)
            cw = jnp.transpose(cw.reshape(N_DEV, depth, CONV_WIDTH, -1), (1, 2, 0, 3)).reshape(depth, CONV_WIDTH, 2, ff)
            conv_w_all = jnp.transpose(cw, (0, 2, 1, 3))
        x1 = _mm_nn_res("out_proj", mix, wout_g.reshape(d, d), xs)
        h2 = _rmsnorm_fwd("ffn_norm_fwd", x1, g2)
        if l == 0:
            wup_g = finish("up0", fwd_up, h2)[0]
            fwd_down = relay("down0", wup_g)
            up, act = _up_conv_gate_fwd(h2, wup_g, conv_w_all[l], conv_b_all[l], after=fwd_down.token)
            wdown_g = finish("down0", fwd_down, up)[0]
        else:
            wup_g, wdown_g = finish(f"ffn{l}", fwd_up, h2)
            up, act = _up_conv_gate_fwd(h2, wup_g, conv_w_all[l], conv_b_all[l])
        saved.append((xs, h1, p, o, rsum, mix, x1, h2, up, act))
        gathered.append((win_g, wout_g, wup_g, wdown_g))
        if l + 1 < depth:
            fwd_in = relay(f"in{l + 1}", act)
            x2 = _mm_nn_res("down_proj", act, wdown_g.reshape(ff, d), x1, after=fwd_in.token)
            win_g, wout_g = finish(f"in{l + 1}", fwd_in, x2)
        else:
            x2 = _mm_nn_res("down_proj", act, wdown_g.reshape(ff, d), x1)
        xs = x2

    dx, dxb, loss_vec = _loss_head(xs, loss_target[0])
    loss = lax.psum(loss_vec[0, 0], MESH_AXES)

    exchanges = []
    small = [None] * depth
    small_names = ["attn_norm_g", "q_norm_g", "k_norm_g", "sgu_norm_g", "sgu_w", "sgu_b", "out_norm_a_g",
                   "out_norm_b_g", "ffn_norm_g", "conv_b", "conv_w"]
    for l in reversed(range(depth)):
        xs0, h1, p, o, rsum, mix, x1, h2, up, act = saved[l]
        win_g, wout_g, wup_g, wdown_g = gathered[l]
        wout_full = wout_g.reshape(d, d)
        wdown_full = wdown_g.reshape(ff, d)
        g1 = attn_norm_g[l][None]
        g2 = ffn_norm_g[l][None]
        gq, gk = q_norm_g[l][None], k_norm_g[l][None]
        ga = out_norm_a_g[l][:, None, :]
        gs = sgu_norm_g[l][:, None, :]
        gb = out_norm_b_g[l][:, None, :]
        d_wdown = _mm_tn_plain("down_proj_dw", act, dxb)
        exchanges.append((l, "down", ("w_down",), _exchange_start(f"grad_down{l}_start", [d_wdown], dx)))
        dup, d_cw, d_cb = _down_dx_conv_gate_bwd(up, conv_w_all[l], conv_b_all[l], dxb, wdown_full,
                                                 exchanges[-1][3].token)
        d_wup = _mm_tn_blocked("up_proj_dw", h2, dup, N_DEV, halves=True)
        exchanges.append((l, "up", ("w_up",), _exchange_start(f"grad_up{l}_start", [d_wup], d_cb)))
        dh2 = _mm_nt_blocked("up_proj_dx", dup, wup_g, halves=True, after=exchanges[-1][3].token)
        dx, dxb, d_g2 = _rmsnorm_bwd("ffn_norm_bwd", dh2, x1, g2, dx)
        d_wout = _mm_tn_plain("out_proj_dw", mix, dxb)
        dmix = _mm_nt_plain("out_proj_dx", dxb, wout_full)
        dq, dk, dv, d_gq, d_gk, d_ga = _attn_bwd(p, gq, gk, ga, o, rsum, dmix, n_heads)
        du, dvs, d_gs, d_sw, d_sb, d_gb = _sgu_bwd(p, gs, sgu_w[l], sgu_b_col[l], gb, dmix, sgu_col0, n_heads)
        dp = jnp.concatenate([dq, dk, dv, du, dvs], axis=-1)
        d_win = _mm_tn_blocked("in_proj_dw", h1, dp, N_DEV)
        exchanges.append((l, "mix", ("w_in", "w_out"), _exchange_start(f"grad_mix{l}_start", [d_win, d_wout], d_gq)))
        dh1 = _mm_nt_blocked("in_proj_dx", dp, win_g, after=exchanges[-1][3].token)
        dx, dxb, d_g1 = _rmsnorm_bwd("attn_norm_bwd", dh1, xs0, g1, dx)
        small[l] = dict(attn_norm_g=d_g1[0], q_norm_g=d_gq[0], k_norm_g=d_gk[0], sgu_norm_g=d_gs[:, 0], sgu_w=d_sw,
                        sgu_b=d_sb[..., 0], out_norm_a_g=d_ga[:, 0], out_norm_b_g=d_gb[:, 0], ffn_norm_g=d_g2[0],
                        conv_w=jnp.transpose(d_cw, (1, 0, 2)).reshape(CONV_WIDTH, f2), conv_b=d_cb.reshape(f2))
    grad_x = dx[None]

    f32_names = [n for n in small_names if n != "sgu_w"]
    small_g = [jnp.stack([small[l][n] for l in range(depth)]) for n in f32_names]
    sgu_w_g = jnp.stack([small[l]["sgu_w"] for l in range(depth)])
    small_sent = _broadcast_start("grad_small_start", [_pack(small_g), sgu_w_g.reshape(-1, TILE).astype(BF16)], dx)

    res = {}
    big = dict(w_in=(w_in, m_w_in, v_w_in), w_out=(w_out, m_w_out, v_w_out), w_up=(w_up, m_w_up, v_w_up),
               w_down=(w_down, m_w_down, v_w_down))
    after = [small_sent.token]
    for l, stage, names, ex in exchanges:
        landed = _exchange_finish(f"grad_{stage}{l}_finish", ex, after)
        after = []
        for name, parts in zip(names, landed):
            w, m, v = big[name]
            res[name] = _adamw(f"adamw_{name}", w, m, v, parts, l, res.get(name))
            after.append(res[name][0])
    small_all, sgu_w_all = _exchange_finish("grad_small_finish", small_sent, after)
    small_sum = _unpack(_sum_slots("small_grad_sum", small_all), small_g)
    g_small = dict(zip(f32_names, small_sum))
    g_small["sgu_w"] = _sum_slots("sgu_w_grad_sum", sgu_w_all).reshape(sgu_w.shape)
    cwn = conv_w.shape[2]
    g_small["conv_w"] = lax.dynamic_slice_in_dim(g_small["conv_w"], my_slot * cwn, cwn, axis=2)
    small_w = dict(attn_norm_g=(attn_norm_g, m_attn_norm_g, v_attn_norm_g), q_norm_g=(q_norm_g, m_q_norm_g, v_q_norm_g),
                   k_norm_g=(k_norm_g, m_k_norm_g, v_k_norm_g), sgu_norm_g=(sgu_norm_g, m_sgu_norm_g, v_sgu_norm_g),
                   sgu_w=(sgu_w, m_sgu_w, v_sgu_w), sgu_b=(sgu_b, m_sgu_b, v_sgu_b),
                   out_norm_a_g=(out_norm_a_g, m_out_norm_a_g, v_out_norm_a_g),
                   out_norm_b_g=(out_norm_b_g, m_out_norm_b_g, v_out_norm_b_g),
                   ffn_norm_g=(ffn_norm_g, m_ffn_norm_g, v_ffn_norm_g), conv_b=(conv_b, m_conv_b, v_conv_b),
                   conv_w=(conv_w, m_conv_w, v_conv_w))
    like = [small_w[n][0] for n in small_names]
    pw = _pack([small_w[n][0] for n in small_names])
    pm = _pack([small_w[n][1] for n in small_names])
    pv = _pack([small_w[n][2] for n in small_names])
    pg = _pack([g_small[n].reshape(small_w[n][0].shape) for n in small_names])
    pd, pnm, pnv = _adamw_small(pw, pg, pm, pv)
    for n, dlt, nm, nv in zip(small_names, _unpack(pd, like), _unpack(pnm, like), _unpack(pnv, like)):
        res[n] = (g_small[n].reshape(small_w[n][0].shape), dlt, nm, nv)

    order = ["attn_norm_g", "w_in", "q_norm_g", "k_norm_g", "sgu_norm_g", "sgu_w", "sgu_b", "out_norm_a_g",
             "out_norm_b_g", "w_out", "ffn_norm_g", "w_up", "conv_w", "conv_b", "w_down"]
    outs = [loss, grad_x]
    for field in range(4):
        outs += [res[n][field] for n in order]
    return tuple(outs)
```

```python
import functools

import jax
import jax.numpy as jnp
from jax import lax
from jax.experimental import pallas as pl
from jax.experimental.pallas import tpu as pltpu

F32 = jnp.float32
BF16 = jnp.bfloat16
EPS = 1e-6
HEAD_DIM = 128
TILE = 128
ATTN_VMEM_MB = 58
ATTN_HEADS_PER_STEP = 4
N_GROUPS = 8
CONV_WIDTH = 3
N_DEV = 8
MESH_AXES = ("x", "y", "c")
MIB = 1024 * 1024

ADAM_LR = 0.001
ADAM_B1 = 0.9
ADAM_B2 = 0.999
ADAM_EPS = 1e-08
ADAM_WD = 0.01
ADAM_STEP = 10
ADAMW_TILE_ELEMS = 160 * 1024

NT_DIMS = (((1,), (1,)), ((), ()))
NN_DIMS = (((1,), (0,)), ((), ()))
TN_DIMS = (((0,), (0,)), ((), ()))


def _cparams(sem, vmem_mb=48):
    return pltpu.CompilerParams(dimension_semantics=sem, vmem_limit_bytes=vmem_mb * MIB)


def _pick(n, cands):
    for c in cands:
        if n % c == 0:
            return c
    return n


def _mm(name, grid, ins, in_specs, out_shape, out_spec, dims, has_res=False, parts=None, vmem_mb=56, after=None):
    n_in = 2 + has_res + (after is not None)
    if after is not None:
        ins = tuple(ins) + (after,)
        in_specs = list(in_specs) + [pl.BlockSpec(after.shape, lambda *_: (0, 0))]

    def body(*refs):
        a_ref, b_ref = refs[:2]
        o_ref = refs[n_in]
        if parts is None:
            acc = lax.dot_general(a_ref[...], b_ref[...], dims, preferred_element_type=F32)
        else:
            acc = None
            for part in parts:
                a, b = part(a_ref, b_ref)
                prod = lax.dot_general(a, b, dims, preferred_element_type=F32)
                acc = prod if acc is None else acc + prod
        if has_res:
            acc = acc + refs[2][...]
        o_ref[...] = acc.astype(o_ref.dtype)

    return pl.pallas_call(
        body, name=name, grid=grid, in_specs=in_specs, out_specs=out_spec, out_shape=out_shape,
        compiler_params=_cparams(("parallel",) * len(grid), vmem_mb),
    )(*ins)


def _rows_for(m, row_bytes, budget):
    return _pick(m, tuple(t for t in (2048, 1024, 512, 256, 128) if t * row_bytes <= budget))


def _mm_nn_blocked(name, a, wb, out_dtype, halves=False, after=None):
    m, k = a.shape
    nb, _, bn = wb.shape
    tm = _rows_for(m, bn * jnp.dtype(out_dtype).itemsize, 6 * MIB)
    a_spec = pl.BlockSpec((tm, k), lambda j, i: (i, 0))
    b_spec = pl.BlockSpec((None, k, bn), lambda j, i: (j, 0, 0))
    if halves:
        hb = nb // 2
        out_shape = jax.ShapeDtypeStruct((2, m, hb * bn), out_dtype)
        o_spec = pl.BlockSpec((None, tm, bn), lambda j, i: (j // hb, i, j % hb))
    else:
        out_shape = jax.ShapeDtypeStruct((m, nb * bn), out_dtype)
        o_spec = pl.BlockSpec((tm, bn), lambda j, i: (i, j))
    return _mm(name, (nb, m // tm), (a, wb), [a_spec, b_spec], out_shape, o_spec, NN_DIMS, after=after)


def _mm_nn_res(name, a, w, res, after=None):
    m, k = a.shape
    n = w.shape[1]
    tm = _pick(m, (512, 256, 128))
    tn = _rows_for(n, k * 2, 12 * MIB)
    a_spec = pl.BlockSpec((tm, k), lambda j, i: (i, 0))
    b_spec = pl.BlockSpec((k, tn), lambda j, i: (0, j))
    r_spec = pl.BlockSpec((tm, tn), lambda j, i: (i, j))
    o_spec = pl.BlockSpec((tm, tn), lambda j, i: (i, j))
    return _mm(name, (n // tn, m // tm), (a, w, res), [a_spec, b_spec, r_spec], jax.ShapeDtypeStruct((m, n), F32),
               o_spec, NN_DIMS, has_res=True, after=after)


def _mm_nt_blocked(name, dy, wb, halves=False, after=None):
    nb, n, bn = wb.shape
    m = dy.shape[-2]
    tm = _pick(m, (512, 256, 128))
    tn = _rows_for(n, nb * bn * 2, 12 * MIB)
    if halves:
        hb = nb // 2
        a_spec = pl.BlockSpec((2, tm, hb * bn), lambda j, i: (0, i, 0))
        a_part = lambda kk: (lambda a_ref: a_ref[kk // hb, :, (kk % hb) * bn:(kk % hb + 1) * bn])
    else:
        a_spec = pl.BlockSpec((tm, nb * bn), lambda j, i: (i, 0))
        a_part = lambda kk: (lambda a_ref: a_ref[:, kk * bn:(kk + 1) * bn])
    parts = [(lambda a_ref, b_ref, kk=kk, sel=a_part(kk): (sel(a_ref), b_ref[kk])) for kk in range(nb)]
    b_spec = pl.BlockSpec((nb, tn, bn), lambda j, i: (0, j, 0))
    o_spec = pl.BlockSpec((tm, tn), lambda j, i: (i, j))
    return _mm(name, (n // tn, m // tm), (dy, wb), [a_spec, b_spec], jax.ShapeDtypeStruct((m, n), F32), o_spec,
               NT_DIMS, parts=parts, after=after)


def _mm_nt_plain(name, dy, w, out_dtype=F32):
    m, k = dy.shape
    n = w.shape[0]
    tm = _rows_for(m, k * 2, 8 * MIB)
    tn = _pick(n, (512, 256, 128))
    a_spec = pl.BlockSpec((tm, k), lambda j, i: (i, 0))
    b_spec = pl.BlockSpec((tn, k), lambda j, i: (j, 0))
    o_spec = pl.BlockSpec((tm, tn), lambda j, i: (i, j))
    return _mm(name, (n // tn, m // tm), (dy, w), [a_spec, b_spec], jax.ShapeDtypeStruct((m, n), out_dtype), o_spec,
               NT_DIMS)


def _mm_tn_blocked(name, a, dy, nb, halves=False):
    s, k1 = a.shape
    bn = (dy.shape[-1] * (2 if halves else 1)) // nb
    tm = _rows_for(k1, bn * 2, 6 * MIB)
    a_spec = pl.BlockSpec((s, tm), lambda j, i: (0, i))
    if halves:
        hb = nb // 2
        b_spec = pl.BlockSpec((None, s, bn), lambda j, i: (j // hb, 0, j % hb))
    else:
        b_spec = pl.BlockSpec((s, bn), lambda j, i: (0, j))
    o_spec = pl.BlockSpec((None, tm, bn), lambda j, i: (j, i, 0))
    return _mm(name, (nb, k1 // tm), (a, dy), [a_spec, b_spec], jax.ShapeDtypeStruct((nb, k1, bn), BF16), o_spec,
               TN_DIMS)


def _mm_tn_plain(name, a, dy):
    s, k1 = a.shape
    n = dy.shape[1]
    tm = _pick(k1, (512, 256, 128))
    tn = _rows_for(n, s * 2, 8 * MIB)
    a_spec = pl.BlockSpec((s, tm), lambda i, j: (0, i))
    b_spec = pl.BlockSpec((s, tn), lambda i, j: (0, j))
    o_spec = pl.BlockSpec((tm, tn), lambda i, j: (i, j))
    return _mm(name, (k1 // tm, n // tn), (a, dy), [a_spec, b_spec], jax.ShapeDtypeStruct((k1, n), BF16), o_spec,
               TN_DIMS)


def _rstd(x):
    return lax.rsqrt(jnp.mean(x * x, axis=-1, keepdims=True) + EPS)


def _norm_bwd(dy, xhat, r, g):
    dxhat = dy * g
    return r * (dxhat - xhat * jnp.mean(dxhat * xhat, axis=-1, keepdims=True))


def _rmsnorm_fwd(name, x, g):
    s, d = x.shape
    tr = _pick(s, (256, 128))

    def body(x_ref, g_ref, h_ref):
        xv = x_ref[...]
        h_ref[...] = (xv * _rstd(xv) * g_ref[...]).astype(BF16)

    return pl.pallas_call(
        body, name=name, grid=(s // tr,),
        in_specs=[pl.BlockSpec((tr, d), lambda i: (i, 0)), pl.BlockSpec((1, d), lambda i: (0, 0))],
        out_specs=pl.BlockSpec((tr, d), lambda i: (i, 0)),
        out_shape=jax.ShapeDtypeStruct((s, d), BF16), compiler_params=_cparams(("parallel",)),
    )(x, g)


def _rmsnorm_bwd(name, dh, x, g, dres):
    s, d = x.shape
    tr = _pick(s, (256, 128))

    def body(dh_ref, x_ref, g_ref, dres_ref, dx_ref, dxb_ref, dg_ref):
        xv = x_ref[...]
        r = _rstd(xv)
        xhat = xv * r
        dhv = dh_ref[...]
        dx = dres_ref[...] + _norm_bwd(dhv, xhat, r, g_ref[...])
        dx_ref[...] = dx
        dxb_ref[...] = dx.astype(BF16)
        part = jnp.sum(dhv * xhat, axis=0, keepdims=True)

        @pl.when(pl.program_id(0) == 0)
        def _():
            dg_ref[...] = part

        @pl.when(pl.program_id(0) > 0)
        def _():
            dg_ref[...] += part

    row = pl.BlockSpec((tr, d), lambda i: (i, 0))
    vec = pl.BlockSpec((1, d), lambda i: (0, 0))
    return pl.pallas_call(
        body, name=name, grid=(s // tr,), in_specs=[row, row, vec, row], out_specs=(row, row, vec),
        out_shape=(jax.ShapeDtypeStruct((s, d), F32), jax.ShapeDtypeStruct((s, d), BF16),
                   jax.ShapeDtypeStruct((1, d), F32)),
        compiler_params=_cparams(("arbitrary",)),
    )(dh, x, g, dres)


def _loss_head(y, target):
    s, d = y.shape
    tr = _pick(s, (256, 128))

    def body(y_ref, t_ref, dy_ref, dyb_ref, loss_ref):
        err = y_ref[...] - t_ref[...]
        dy = err * (1.0 / d)
        dy_ref[...] = dy
        dyb_ref[...] = dy.astype(BF16)
        part = 0.5 * jnp.sum(jnp.mean(err * err, axis=-1, keepdims=True), axis=0, keepdims=True)
        part = jnp.broadcast_to(part, (1, 128))

        @pl.when(pl.program_id(0) == 0)
        def _():
            loss_ref[...] = part

        @pl.when(pl.program_id(0) > 0)
        def _():
            loss_ref[...] += part

    row = pl.BlockSpec((tr, d), lambda i: (i, 0))
    return pl.pallas_call(
        body, name="loss_head", grid=(s // tr,), in_specs=[row, row],
        out_specs=(row, row, pl.BlockSpec((1, 128), lambda i: (0, 0))),
        out_shape=(jax.ShapeDtypeStruct((s, d), F32), jax.ShapeDtypeStruct((s, d), BF16),
                   jax.ShapeDtypeStruct((1, 128), F32)),
        compiler_params=_cparams(("arbitrary",)),
    )(y, target)


def _split_dot(x, tri):
    hi = x.astype(BF16)
    lo = (x - hi.astype(F32)).astype(BF16)
    return (jnp.dot(hi, tri, preferred_element_type=F32) + jnp.dot(lo, tri, preferred_element_type=F32))


def _tile_iotas():
    row = lax.broadcasted_iota(jnp.int32, (TILE, TILE), 0)
    col = lax.broadcasted_iota(jnp.int32, (TILE, TILE), 1)
    return row, col


def _sb_logits(qi, kb, mask):
    z = lax.dot_general(qi, kb, NT_DIMS, preferred_element_type=F32) * (HEAD_DIM ** -0.5)
    sp = jnp.log1p(jnp.exp(-jnp.abs(z)))
    lb = jnp.minimum(z, 0.0) - sp
    l1m = -jnp.maximum(z, 0.0) - sp
    if mask is not None:
        l1m = jnp.where(mask, l1m, 0.0)
    return lb, l1m


def _attn_fwd(p, gq, gk, ga, n_heads):
    s = p.shape[0]
    nq = s // TILE

    hp = ATTN_HEADS_PER_STEP
    wd = hp * HEAD_DIM

    def body(q_ref, k_ref, v_ref, gq_ref, gk_ref, ga_ref, att_ref, o_ref, r_ref, qn_s, kn_s, vb_s):
        heads = [slice(hh * HEAD_DIM, (hh + 1) * HEAD_DIM) for hh in range(hp)]
        for hd in heads:
            qv = q_ref[:, hd]
            qn_s[:, hd] = (qv * _rstd(qv) * gq_ref[...]).astype(BF16)
            kv = k_ref[:, hd]
            kn_s[:, hd] = (kv * _rstd(kv) * gk_ref[...]).astype(BF16)
        vb_s[...] = v_ref[...].astype(BF16)
        row, col = _tile_iotas()
        causal = col < row
        upper_ones = jnp.concatenate([(row > col).astype(BF16), jnp.ones((TILE, TILE), BF16)], axis=1)

        def tiles(rows, key_blocks, states, mask):
            chains = [(hi, hd, keys) for hi, hd in enumerate(heads) for keys in key_blocks]
            logits = [_sb_logits(qn_s[rows, hd], kn_s[keys, hd], mask) for _, hd, keys in chains]
            sums = [_split_dot(l1m, upper_ones) for _, l1m in logits]
            carry = [c for _, c in states]
            probs = []
            for (hi, _, _), (lb, _), sm in zip(chains, logits, sums):
                a = jnp.exp(lb + sm[:, :TILE] + carry[hi])
                carry[hi] = carry[hi] + sm[:, TILE:]
                probs.append((a if mask is None else jnp.where(mask, a, 0.0)).astype(BF16))
            outs = [jnp.dot(a, vb_s[keys, hd], preferred_element_type=F32) for a, (_, hd, keys) in zip(probs, chains)]
            acc = [o_acc for o_acc, _ in states]
            for (hi, _, _), o in zip(chains, outs):
                acc[hi] = acc[hi] + o
            return tuple(zip(acc, carry))

        def key_block(b):
            return pl.ds(pl.multiple_of(b * TILE, TILE), TILE)

        def qblock(i, _):
            rows = pl.ds(pl.multiple_of(i * TILE, TILE), TILE)
            zero = jnp.zeros((TILE, HEAD_DIM), F32)
            states = tiles(rows, [rows], tuple((zero, zero) for _ in heads), causal)
            states = lax.cond(i % 2 == 1, lambda st: tiles(rows, [key_block(i - 1)], st, None), lambda st: st, states)
            top = i - i % 2

            def kblocks(jj, states):
                return tiles(rows, [key_block(top - 1 - 2 * jj), key_block(top - 2 - 2 * jj)], states, None)

            states = lax.fori_loop(0, i // 2, kblocks, states)
            for hh, (hd, (o_acc, c)) in enumerate(zip(heads, states)):
                o_ref[rows, hd] = o_acc
                r_ref[rows, hd] = c
                att_ref[rows, hd] = (o_acc * _rstd(o_acc) * ga_ref[hh]).astype(BF16)
            return 0

        lax.fori_loop(0, nq, qblock, 0)

    col_blk = lambda off: pl.BlockSpec((s, wd), lambda h: (0, off + h))
    vec = pl.BlockSpec((1, HEAD_DIM), lambda h: (0, 0))
    hvec = pl.BlockSpec((hp, 1, HEAD_DIM), lambda h: (h, 0, 0))
    out = pl.BlockSpec((s, wd), lambda h: (0, h))
    w = n_heads * HEAD_DIM
    steps = n_heads // hp
    return pl.pallas_call(
        body, name="attn_fwd", grid=(steps,),
        in_specs=[col_blk(0), col_blk(steps), col_blk(2 * steps), vec, vec, hvec],
        out_specs=(out, out, out),
        out_shape=(jax.ShapeDtypeStruct((s, w), BF16), jax.ShapeDtypeStruct((s, w), F32),
                   jax.ShapeDtypeStruct((s, w), F32)),
        scratch_shapes=[pltpu.VMEM((s, wd), BF16)] * 3,
        compiler_params=_cparams(("parallel",), ATTN_VMEM_MB),
    )(p, p, p, gq, gk, ga)


def _attn_bwd(p, gq, gk, ga, o, rsum, dmix, n_heads):
    s = p.shape[0]
    nq = s // TILE

    hp = ATTN_HEADS_PER_STEP
    wd = hp * HEAD_DIM
    scale = HEAD_DIM ** -0.5

    def body(q_ref, k_ref, v_ref, gq_ref, gk_ref, ga_ref, o_ref, r_ref, dm_ref,
             dq_ref, dk_ref, dv_ref, dgq_ref, dgk_ref, dga_ref,
             qn_s, kn_s, vb_s, do_s, dqn_s, dkn_s, dv_s):
        step = pl.program_id(0)
        gqv, gkv = gq_ref[...], gk_ref[...]
        heads = [slice(hh * HEAD_DIM, (hh + 1) * HEAD_DIM) for hh in range(hp)]
        for hh, hd in enumerate(heads):
            qv = q_ref[:, hd]
            qn_s[:, hd] = (qv * _rstd(qv) * gqv).astype(BF16)
            kv = k_ref[:, hd]
            kn_s[:, hd] = (kv * _rstd(kv) * gkv).astype(BF16)
            ov = o_ref[:, hd]
            ro = _rstd(ov)
            ohat = ov * ro
            dm = dm_ref[:, hd]
            dga_ref[hh] = jnp.sum(dm * ohat, axis=0, keepdims=True)
            do_s[:, hd] = _norm_bwd(dm, ohat, ro, ga_ref[hh]).astype(BF16)
        vb_s[...] = v_ref[...].astype(BF16)
        dkn_s[...] = jnp.zeros_like(dkn_s)
        dv_s[...] = jnp.zeros_like(dv_s)
        row, col = _tile_iotas()
        causal = col < row
        ones = jnp.ones((TILE, TILE), BF16)
        incl_ones = jnp.concatenate([(row <= col).astype(BF16), ones], axis=1)
        excl_ones = jnp.concatenate([(row < col).astype(BF16), ones], axis=1)

        def tiles(rows, key_blocks, states, mask):
            chains = [(hi, hd, keys) for hi, hd in enumerate(heads) for keys in key_blocks]
            qis = [qn_s[rows, hd] for hd in heads]
            dois = [do_s[rows, hd] for hd in heads]
            logits = [_sb_logits(qis[hi], kn_s[keys, hd], mask) for hi, hd, keys in chains]
            sums = [_split_dot(l1m, incl_ones) for _, l1m in logits]
            das = [lax.dot_general(dois[hi], vb_s[keys, hd], NT_DIMS, preferred_element_type=F32)
                   for hi, hd, keys in chains]
            pfx = [st[1] for st in states]
            probs, dss = [], []
            for (hi, hd, _), (lb, _), sm, da in zip(chains, logits, sums, das):
                a = jnp.exp(lb + (r_ref[rows, hd] - pfx[hi] - sm[:, :TILE]))
                pfx[hi] = pfx[hi] + sm[:, TILE:]
                a = a if mask is None else jnp.where(mask, a, 0.0)
                probs.append(a.astype(BF16))
                dss.append(da * a)
            dsums = [_split_dot(ds, excl_ones) for ds in dss]
            pc = [st[2] for st in states]
            dzs = []
            for (hi, _, _), (lb, _), ds, dsm in zip(chains, logits, dss, dsums):
                dl1m = pc[hi] + dsm[:, :TILE]
                pc[hi] = pc[hi] + dsm[:, TILE:]
                dl1m = dl1m if mask is None else jnp.where(mask, dl1m, 0.0)
                beta = jnp.exp(lb)
                dzs.append(((ds * (1.0 - beta) - dl1m * beta) * scale).astype(BF16))
            dqs = [jnp.dot(dz, kn_s[keys, hd], preferred_element_type=F32) for dz, (_, hd, keys) in zip(dzs, chains)]
            for dz, a, (hi, hd, keys) in zip(dzs, probs, chains):
                dkn_s[keys, hd] += lax.dot_general(dz, qis[hi], TN_DIMS, preferred_element_type=F32)
                dv_s[keys, hd] += lax.dot_general(a, dois[hi], TN_DIMS, preferred_element_type=F32)
            dq_acc = [st[0] for st in states]
            for (hi, _, _), dq in zip(chains, dqs):
                dq_acc[hi] = dq_acc[hi] + dq
            return tuple(zip(dq_acc, pfx, pc))

        def key_block(b):
            return pl.ds(pl.multiple_of(b * TILE, TILE), TILE)

        def qblock(i, _):
            rows = pl.ds(pl.multiple_of(i * TILE, TILE), TILE)
            zero = jnp.zeros((TILE, HEAD_DIM), F32)

            def kblocks(jj, states):
                return tiles(rows, [key_block(2 * jj), key_block(2 * jj + 1)], states, None)

            states = lax.fori_loop(0, i // 2, kblocks, tuple((zero, zero, zero) for _ in heads))
            states = lax.cond(i % 2 == 1, lambda st: tiles(rows, [key_block(i - 1)], st, None), lambda st: st, states)
            states = tiles(rows, [rows], states, causal)
            for hd, (dq_acc, _, _) in zip(heads, states):
                dqn_s[rows, hd] = dq_acc
            return 0

        lax.fori_loop(0, nq, qblock, 0)

        def norm_in_bwd(x_ref, g, dn_s, dx_ref, dg_ref):
            part = jnp.zeros((1, HEAD_DIM), F32)
            for hd in heads:
                xv = x_ref[:, hd]
                r = _rstd(xv)
                xhat = xv * r
                dn = dn_s[:, hd]
                dx_ref[:, hd] = _norm_bwd(dn, xhat, r, g).astype(BF16)
                part = part + jnp.sum(dn * xhat, axis=0, keepdims=True)

            @pl.when(step == 0)
            def _():
                dg_ref[...] = part

            @pl.when(step > 0)
            def _():
                dg_ref[...] += part

        norm_in_bwd(q_ref, gqv, dqn_s, dq_ref, dgq_ref)
        norm_in_bwd(k_ref, gkv, dkn_s, dk_ref, dgk_ref)
        dv_ref[...] = dv_s[...].astype(BF16)

    once = pl.Buffered(1)
    steps = n_heads // hp
    col_blk = lambda off: pl.BlockSpec((s, wd), lambda h: (0, off + h), pipeline_mode=once)
    vec = pl.BlockSpec((1, HEAD_DIM), lambda h: (0, 0))
    hvec = pl.BlockSpec((hp, 1, HEAD_DIM), lambda h: (h, 0, 0))
    blk = pl.BlockSpec((s, wd), lambda h: (0, h), pipeline_mode=once)
    w = n_heads * HEAD_DIM
    big = jax.ShapeDtypeStruct((s, w), BF16)
    return pl.pallas_call(
        body, name="attn_bwd", grid=(steps,),
        in_specs=[col_blk(0), col_blk(steps), col_blk(2 * steps), vec, vec, hvec, blk, blk, blk],
        out_specs=(blk, blk, blk, vec, vec, hvec),
        out_shape=(big, big, big, jax.ShapeDtypeStruct((1, HEAD_DIM), F32), jax.ShapeDtypeStruct((1, HEAD_DIM), F32),
                   jax.ShapeDtypeStruct((n_heads, 1, HEAD_DIM), F32)),
        scratch_shapes=[pltpu.VMEM((s, wd), BF16)] * 4 + [pltpu.VMEM((s, wd), F32)] * 3,
        compiler_params=_cparams(("arbitrary",), ATTN_VMEM_MB),
    )(p, p, p, gq, gk, ga, o, rsum, dmix)


_INV_SQRT2 = 0.7071067811865476
_INV_SQRT_2PI = 0.3989422804014327


def _gelu(x):
    return 0.5 * x * (1.0 + lax.erf(x * _INV_SQRT2))


def _gelu_grad(x):
    return 0.5 * (1.0 + lax.erf(x * _INV_SQRT2)) + x * (_INV_SQRT_2PI * jnp.exp(-0.5 * x * x))


def _sgu_fwd(p, gs, w_s, b_s, gb, col0, after):
    s = p.shape[0]
    n_chunks = s // TILE
    per_trip = _pick(n_chunks, (4, 2, 1))

    def body(u_ref, v_ref, gs_ref, w_ref, b_ref, gb_ref, _, out_ref, vs_s):
        vg = _gelu(v_ref[...])
        vs_s[...] = (vg * _rstd(vg) * gs_ref[...]).astype(BF16)
        row, col = _tile_iotas()
        wt = jnp.where(col <= row, w_ref[...], 0.0).astype(BF16)
        bcol = b_ref[...]
        gbv = gb_ref[...]

        def chunks(c, _):
            rows = [pl.ds(pl.multiple_of((c * per_trip + k) * TILE, TILE), TILE) for k in range(per_trip)]
            mixed = [jnp.dot(wt, vs_s[r, :], preferred_element_type=F32) + bcol for r in rows]
            sgs = [_gelu(u_ref[r, :]) * mx for r, mx in zip(rows, mixed)]
            for r, sg in zip(rows, sgs):
                out_ref[r, :] = (sg * _rstd(sg) * gbv).astype(BF16)
            return 0

        lax.fori_loop(0, n_chunks // per_trip, chunks, 0)

    col_blk = lambda off: pl.BlockSpec((s, HEAD_DIM), lambda g: (0, off + g))
    gvec = pl.BlockSpec((None, 1, HEAD_DIM), lambda g: (g, 0, 0))
    return pl.pallas_call(
        body, name="sgu_fwd", grid=(N_GROUPS,),
        in_specs=[col_blk(col0), col_blk(col0 + N_GROUPS), gvec,
                  pl.BlockSpec((None, TILE, TILE), lambda g: (g, 0, 0)),
                  pl.BlockSpec((None, TILE, 1), lambda g: (g, 0, 0)), gvec,
                  pl.BlockSpec(after.shape, lambda g: (0, 0))],
        out_specs=pl.BlockSpec((s, HEAD_DIM), lambda g: (0, g)),
        out_shape=jax.ShapeDtypeStruct((s, N_GROUPS * HEAD_DIM), BF16),
        scratch_shapes=[pltpu.VMEM((s, HEAD_DIM), BF16)],
        compiler_params=_cparams(("parallel",)),
    )(p, p, gs, w_s, b_s, gb, after)


def _sgu_bwd(p, gs, w_s, b_s, gb, dmix, col0, dm_col0):
    s = p.shape[0]
    n_chunks = s // TILE
    per_trip = _pick(n_chunks, (4, 2, 1))

    def body(u_ref, v_ref, gs_ref, w_ref, b_ref, gb_ref, dm_ref,
             du_ref, dv_ref, dgs_ref, dw_ref, db_ref, dgb_ref, vs_s, dvs_s):
        gsv = gs_ref[...]
        gbv = gb_ref[...]
        vg = _gelu(v_ref[...])
        vs_s[...] = (vg * _rstd(vg) * gsv).astype(BF16)
        row, col = _tile_iotas()
        causal = col <= row
        wt = jnp.where(causal, w_ref[...], 0.0).astype(BF16)
        bcol = b_ref[...]

        def chunks(c, carry):
            dw_acc, db_acc, dgb_acc = carry
            rows = [pl.ds(pl.multiple_of((c * per_trip + k) * TILE, TILE), TILE) for k in range(per_trip)]
            vss = [vs_s[r, :] for r in rows]
            mixed = [jnp.dot(wt, vs, preferred_element_type=F32) + bcol for vs in vss]
            dmbs = []
            for r, mx in zip(rows, mixed):
                u_pre = u_ref[r, :]
                u = _gelu(u_pre)
                sg = u * mx
                rs = _rstd(sg)
                sghat = sg * rs
                dm = dm_ref[r, :]
                dsg = _norm_bwd(dm, sghat, rs, gbv)
                dgb_acc = dgb_acc + jnp.sum(dm * sghat, axis=0, keepdims=True)
                du_ref[r, :] = (dsg * mx * _gelu_grad(u_pre)).astype(BF16)
                dmixed = dsg * u
                db_acc = db_acc + jnp.sum(dmixed, axis=1, keepdims=True)
                dmbs.append(dmixed.astype(BF16))
            for dmb, vs in zip(dmbs, vss):
                dw_acc = dw_acc + lax.dot_general(dmb, vs, NT_DIMS, preferred_element_type=F32)
            for r, dmb in zip(rows, dmbs):
                dvs_s[r, :] = lax.dot_general(wt, dmb, TN_DIMS, preferred_element_type=F32)
            return dw_acc, db_acc, dgb_acc

        dw_acc, db_acc, dgb_acc = lax.fori_loop(
            0, n_chunks // per_trip, chunks,
            (jnp.zeros((TILE, TILE), F32), jnp.zeros((TILE, 1), F32), jnp.zeros((1, HEAD_DIM), F32)))
        dw_ref[...] = jnp.where(causal, dw_acc, 0.0)
        db_ref[...] = db_acc
        dgb_ref[...] = dgb_acc
        v_pre = v_ref[...]
        vg = _gelu(v_pre)
        rv = _rstd(vg)
        vhat = vg * rv
        dvs = dvs_s[...]
        dgs_ref[...] = jnp.sum(dvs * vhat, axis=0, keepdims=True)
        dv_ref[...] = (_norm_bwd(dvs, vhat, rv, gsv) * _gelu_grad(v_pre)).astype(BF16)

    col_blk = lambda off: pl.BlockSpec((s, HEAD_DIM), lambda g: (0, off + g))
    gvec = pl.BlockSpec((None, 1, HEAD_DIM), lambda g: (g, 0, 0))
    wspec = pl.BlockSpec((None, TILE, TILE), lambda g: (g, 0, 0))
    bspec = pl.BlockSpec((None, TILE, 1), lambda g: (g, 0, 0))
    blk = pl.BlockSpec((s, HEAD_DIM), lambda g: (0, g))
    big = jax.ShapeDtypeStruct((s, N_GROUPS * HEAD_DIM), BF16)
    gshape = jax.ShapeDtypeStruct((N_GROUPS, 1, HEAD_DIM), F32)
    return pl.pallas_call(
        body, name="sgu_bwd", grid=(N_GROUPS,),
        in_specs=[col_blk(col0), col_blk(col0 + N_GROUPS), gvec, wspec, bspec, gvec, col_blk(dm_col0)],
        out_specs=(blk, blk, gvec, wspec, bspec, gvec),
        out_shape=(big, big, gshape, jax.ShapeDtypeStruct((N_GROUPS, TILE, TILE), F32),
                   jax.ShapeDtypeStruct((N_GROUPS, TILE, 1), F32), gshape),
        scratch_shapes=[pltpu.VMEM((s, HEAD_DIM), BF16), pltpu.VMEM((s, HEAD_DIM), F32)],
        compiler_params=_cparams(("parallel",)),
    )(p, p, gs, w_s, b_s, gb, dmix)


SUBLANES = 8


def _shift_down(x, n):
    rolled = pltpu.roll(x, n, 0)
    edge = lax.broadcasted_iota(jnp.int32, (SUBLANES, x.shape[1]), 0)
    return jnp.concatenate([jnp.where(edge >= n, rolled[:SUBLANES], 0.0), rolled[SUBLANES:]], axis=0)


def _shift_up(x, n):
    s = x.shape[0]
    rolled = pltpu.roll(x, s - n, 0)
    edge = lax.broadcasted_iota(jnp.int32, (SUBLANES, x.shape[1]), 0)
    return jnp.concatenate([rolled[:s - SUBLANES], jnp.where(edge < SUBLANES - n, rolled[s - SUBLANES:], 0.0)], axis=0)


def _conv(x, w, b):
    x1, x2 = _shift_down(x, 1), _shift_down(x, 2)
    return b + w[0:1, :] * x2 + w[1:2, :] * x1 + w[2:3, :] * x, x1, x2


def _conv_specs(s, tn):
    xspec = pl.BlockSpec((2, s, tn), lambda j: (0, 0, j))
    wspec = pl.BlockSpec((2, CONV_WIDTH, tn), lambda j: (0, 0, j))
    bspec = pl.BlockSpec((2, 1, tn), lambda j: (0, 0, j))
    return xspec, wspec, bspec


def _conv_gate_fwd(up, cw, cb):
    _, s, f = up.shape
    tn = _pick(f, (256, 128))

    def body(x_ref, w_ref, b_ref, act_ref):
        gate = _conv(x_ref[0], w_ref[0], b_ref[0])[0]
        val = _conv(x_ref[1], w_ref[1], b_ref[1])[0]
        act_ref[...] = (gate * jax.nn.sigmoid(gate) * val).astype(BF16)

    xspec, wspec, bspec = _conv_specs(s, tn)
    return pl.pallas_call(
        body, name="conv_gate_fwd", grid=(f // tn,), in_specs=[xspec, wspec, bspec],
        out_specs=pl.BlockSpec((s, tn), lambda j: (0, j)), out_shape=jax.ShapeDtypeStruct((s, f), BF16),
        compiler_params=_cparams(("parallel",)),
    )(up, cw, cb)


def _up_conv_gate_fwd(h, wb, cw, cb, after=None):
    s, k = h.shape
    nb, _, bn = wb.shape
    hb = nb // 2
    f = hb * bn
    tm = _pick(s, (512, 256, 128))
    n_in = 5 + (after is not None)

    def body(*refs):
        h_ref, wg_ref, wv_ref, w_ref, b_ref = refs[:5]
        up_ref, act_ref, halo_s = refs[n_in:]
        first = pl.program_id(1) == 0
        outs = []
        for half, wt_ref in enumerate((wg_ref, wv_ref)):
            x = jnp.dot(h_ref[...], wt_ref[...], preferred_element_type=F32)
            up_ref[half] = x
            halo = jnp.where(first, 0.0, halo_s[half])
            halo_s[half] = x[tm - SUBLANES:]
            full = jnp.concatenate([halo, x], axis=0)
            x1 = pltpu.roll(full, 1, 0)[SUBLANES:]
            x2 = pltpu.roll(full, 2, 0)[SUBLANES:]
            w = w_ref[half]
            outs.append(b_ref[half] + w[0:1, :] * x2 + w[1:2, :] * x1 + w[2:3, :] * x)
        gate, val = outs
        act_ref[...] = (gate * jax.nn.sigmoid(gate) * val).astype(BF16)

    ins = [h, wb, wb, cw, cb]
    in_specs = [pl.BlockSpec((tm, k), lambda j, i: (i, 0)),
                pl.BlockSpec((None, k, bn), lambda j, i: (j, 0, 0)),
                pl.BlockSpec((None, k, bn), lambda j, i: (j + hb, 0, 0)),
                pl.BlockSpec((2, CONV_WIDTH, bn), lambda j, i: (0, 0, j)),
                pl.BlockSpec((2, 1, bn), lambda j, i: (0, 0, j))]
    if after is not None:
        ins.append(after)
        in_specs.append(pl.BlockSpec(after.shape, lambda j, i: (0, 0)))
    return pl.pallas_call(
        body, name="up_conv_gate_fwd", grid=(hb, s // tm), in_specs=in_specs,
        out_specs=(pl.BlockSpec((2, tm, bn), lambda j, i: (0, i, j)), pl.BlockSpec((tm, bn), lambda j, i: (i, j))),
        out_shape=(jax.ShapeDtypeStruct((2, s, f), F32), jax.ShapeDtypeStruct((s, f), BF16)),
        scratch_shapes=[pltpu.VMEM((2, SUBLANES, bn), F32)],
        compiler_params=_cparams(("parallel", "arbitrary"), 56),
    )(*ins)


def _down_dx_conv_gate_bwd(up, cw, cb, dy, wdown, after):
    _, s, f = up.shape
    d = dy.shape[1]
    tn = _pick(f, (256, 128))

    def body(x_ref, w_ref, b_ref, dy_ref, wd_ref, _, dx_ref, dw_ref, db_ref):
        da = lax.dot_general(dy_ref[...], wd_ref[...], NT_DIMS, preferred_element_type=F32)
        xg, xv = x_ref[0], x_ref[1]
        wg, wv = w_ref[0], w_ref[1]
        gate, xg1, xg2 = _conv(xg, wg, b_ref[0])
        val, xv1, xv2 = _conv(xv, wv, b_ref[1])
        sig = jax.nn.sigmoid(gate)
        dval = da * (gate * sig)
        dgate = da * val * (sig * (1.0 + gate * (1.0 - sig)))
        for half, (x, x1, x2, w, dz) in enumerate(((xg, xg1, xg2, wg, dgate), (xv, xv1, xv2, wv, dval))):
            dx_ref[half] = (w[2:3, :] * dz + w[1:2, :] * _shift_up(dz, 1) + w[0:1, :] * _shift_up(dz, 2)).astype(BF16)
            dw_ref[half, 0:1, :] = jnp.sum(dz * x2, axis=0, keepdims=True)
            dw_ref[half, 1:2, :] = jnp.sum(dz * x1, axis=0, keepdims=True)
            dw_ref[half, 2:3, :] = jnp.sum(dz * x, axis=0, keepdims=True)
            db_ref[half] = jnp.sum(dz, axis=0, keepdims=True)

    xspec, wspec, bspec = _conv_specs(s, tn)
    return pl.pallas_call(
        body, name="down_dx_conv_gate_bwd", grid=(f // tn,),
        in_specs=[xspec, wspec, bspec, pl.BlockSpec((s, d), lambda j: (0, 0)), pl.BlockSpec((tn, d), lambda j: (j, 0)),
                  pl.BlockSpec(after.shape, lambda j: (0, 0))],
        out_specs=(xspec, wspec, bspec),
        out_shape=(jax.ShapeDtypeStruct((2, s, f), BF16), jax.ShapeDtypeStruct((2, CONV_WIDTH, f), F32),
                   jax.ShapeDtypeStruct((2, 1, f), F32)),
        compiler_params=_cparams(("parallel",), 56),
    )(up, cw, cb, dy, wdown, after)


def _mesh_pos():
    return lax.axis_index("x"), lax.axis_index("y"), lax.axis_index("c")


def _remote(src, dst, send_sem, recv_sem, to):
    return pltpu.make_async_remote_copy(src_ref=src, dst_ref=dst, send_sem=send_sem, recv_sem=recv_sem,
                                        device_id=to, device_id_type=pl.DeviceIdType.MESH)


HBM_SPEC = pl.BlockSpec(memory_space=pltpu.HBM)
SEM_SPEC = pl.BlockSpec(memory_space=pltpu.SEMAPHORE)
ANY_SPEC = pl.BlockSpec(memory_space=pl.ANY)
TOKEN_SPEC = pl.BlockSpec(memory_space=pltpu.VMEM)
TOKEN_SHAPE = jax.ShapeDtypeStruct((8, 128), F32)
DATAFLOW = pltpu.SideEffectType.DATAFLOW_SIDE_EFFECTING
GATHER_PLANE = (2, 4, 6)


def _slot(pos):
    return 4 * pos[0] + 2 * pos[1] + pos[2]


def _flip(pos, k):
    return (pos[0] ^ ((k >> 2) & 1), pos[1] ^ ((k >> 1) & 1), pos[2] ^ (k & 1))


def _hbm(a):
    return pltpu.with_memory_space_constraint(a, pltpu.HBM)


def _hbm_shapes(arrays):
    return tuple(pltpu.HBM(a.shape, a.dtype) for a in arrays)


class _Split:
    def __init__(self, n, outs, n_sets):
        k = 2 * n * int(n_sets)
        self.n = n
        self.sems = list(outs[:k])
        self.bufs = list(outs[k:k + 2 * n])
        self.token = outs[-1]

    def sem_set(self, i):
        return self.sems[2 * self.n * i:2 * self.n * (i + 1)]


def _split_call(name, body, bufs, sems_in, n_sets, after):
    n = len(bufs) // 2
    k = 2 * n * int(n_sets)
    m = len(sems_in)
    afters = list(after) if isinstance(after, (list, tuple)) else [after]
    na = len(afters)

    def wrapped(*refs):
        srcs, dsts = refs[:n], refs[n:2 * n]
        s_in = refs[2 * n:2 * n + m]
        s_out = refs[2 * n + m + na:2 * n + m + na + k]
        token, local_sems = refs[-2], refs[-1]
        body(srcs, dsts, s_in, s_out, local_sems)
        token[...] = jnp.zeros_like(token)

    outs = pl.pallas_call(
        wrapped, name=name,
        out_shape=(pltpu.SemaphoreType.DMA(()),) * k + _hbm_shapes(bufs) + (TOKEN_SHAPE,),
        in_specs=[HBM_SPEC] * (2 * n) + [SEM_SPEC] * m + [ANY_SPEC] * na,
        out_specs=(SEM_SPEC,) * k + (HBM_SPEC,) * (2 * n) + (TOKEN_SPEC,),
        input_output_aliases={i: k + i for i in range(2 * n)},
        scratch_shapes=[pltpu.SemaphoreType.DMA((n,))],
        compiler_params=pltpu.CompilerParams(has_side_effects=DATAFLOW),
    )(*[_hbm(b) for b in bufs], *sems_in, *afters)
    return _Split(n, outs, n_sets)


def _wait_slots(land, count, send_sem, recv_sem, me, send=False, recv=False):
    span = land.at[pl.ds(0, count)]
    cp = _remote(span, span, send_sem, recv_sem, me)
    if send:
        cp.wait_send()
    if recv:
        cp.wait_recv()


X_FLIP, Y_FLIP, DIAG_FLIP = 4, 2, 6
BF16_SUBLANES = 16


def _gather_start(name, shards, after):
    n = len(shards)
    my_slot = _slot(_mesh_pos())
    lands = [lax.dynamic_update_slice(lax.empty((N_DEV,) + w.shape, w.dtype), w[None], (my_slot, 0, 0)) for w in shards]

    def body(srcs, dsts, _, sems, local_sems):
        me = _mesh_pos()
        for a in range(n):
            for k in (1, X_FLIP, Y_FLIP):
                _remote(srcs[a], dsts[a].at[_slot(me)], sems[a], sems[n + a], _flip(me, k)).start()

    return _split_call(name, body, list(shards) + lands, [], 1, after)


def _gather_relay(name, started, after):
    n = started.n

    def body(srcs, dsts, sems_a, sems_out, local_sems):
        me = _mesh_pos()
        sibling = _flip(me, 1)
        pass_on, relay = sems_out[:2 * n], sems_out[2 * n:]
        for a in range(n):
            _wait_slots(dsts[a], 3, sems_a[a], sems_a[n + a], me, recv=True)
            from_x = dsts[a].at[_slot(_flip(me, X_FLIP))]
            from_y = dsts[a].at[_slot(_flip(me, Y_FLIP))]
            rows = srcs[a].shape[0]
            if rows % (2 * BF16_SUBLANES) == 0:
                top, bottom = pl.ds(0, rows // 2), pl.ds(rows // 2, rows // 2)
                _remote(from_y.at[top], from_y.at[top], relay[a], relay[n + a], _flip(me, X_FLIP)).start()
                _remote(from_x.at[bottom], from_x.at[bottom], relay[a], relay[n + a], _flip(me, Y_FLIP)).start()
            else:
                _remote(from_y, from_y, relay[a], relay[n + a], _flip(me, X_FLIP)).start()
            for block in (from_x, from_y):
                _remote(block, block, pass_on[a], pass_on[n + a], sibling).start()
        for a in range(n):
            _wait_slots(dsts[a], 3, sems_a[a], sems_a[n + a], me, send=True)

    return _split_call(name, body, started.bufs, started.sems, 2, after)


def _gather_relay_diagonal(name, relayed, after):
    n = relayed.n

    def body(srcs, dsts, relay, pass_on, local_sems):
        me = _mesh_pos()
        for a in range(n):
            _wait_slots(dsts[a], 1, relay[a], relay[n + a], me, recv=True)
            block = dsts[a].at[_slot(_flip(me, DIAG_FLIP))]
            _remote(block, block, pass_on[a], pass_on[n + a], _flip(me, 1)).start()
        for a in range(n):
            _wait_slots(dsts[a], 1, relay[a], relay[n + a], me, send=True)

    return _split_call(name, body, relayed.bufs, relayed.sem_set(1), 1, after)


def _gather_finish(name, relayed, diagonal, after):
    n = relayed.n

    def body(srcs, dsts, sems, _, local_sems):
        me = _mesh_pos()
        first, second = sems[:2 * n], sems[2 * n:]
        for a in range(n):
            _wait_slots(dsts[a], 2, first[a], first[n + a], me, send=True, recv=True)
            _wait_slots(dsts[a], 1, second[a], second[n + a], me, send=True, recv=True)

    return _split_call(name, body, diagonal.bufs, relayed.sem_set(0) + diagonal.sems, 0, after).bufs[n:]


def _exchange_start(name, blocked, after):
    n = len(blocked)
    my_slot = _slot(_mesh_pos())
    rows = [w.shape[-2] // (N_DEV if w.ndim == 2 else 1) for w in blocked]

    def block(ref, a, slot):
        if len(ref.shape) == 3:
            return ref.at[slot]
        return ref.at[pl.ds(pl.multiple_of(slot * rows[a], 16), rows[a])]

    lands = []
    for w, r in zip(blocked, rows):
        mine = lax.dynamic_slice_in_dim(w, my_slot, 1, 0) if w.ndim == 3 else lax.dynamic_slice_in_dim(w, my_slot * r, r, 0)[None]
        lands.append(lax.dynamic_update_slice(lax.empty((N_DEV, r, w.shape[-1]), w.dtype), mine, (my_slot, 0, 0)))

    def body(srcs, dsts, _, sems, local_sems):
        me = _mesh_pos()
        for a in range(n):
            for k in range(1, N_DEV):
                peer = _flip(me, k)
                _remote(block(srcs[a], a, _slot(peer)), dsts[a].at[_slot(me)], sems[a], sems[n + a], peer).start()

    return _split_call(name, body, list(blocked) + lands, [], True, after)


def _exchange_finish(name, started, after):
    n = started.n

    def body(srcs, dsts, sems, _, local_sems):
        me = _mesh_pos()
        for a in range(n):
            _wait_slots(dsts[a], N_DEV - 1, sems[a], sems[n + a], me, send=True, recv=True)

    return _split_call(name, body, started.bufs, started.sems, False, after).bufs[n:]


def _broadcast_start(name, arrays, after):
    n = len(arrays)
    my_slot = _slot(_mesh_pos())
    lands = [lax.dynamic_update_slice(lax.empty((N_DEV,) + w.shape, w.dtype), w[None], (my_slot, 0, 0)) for w in arrays]

    def body(srcs, dsts, _, sems, local_sems):
        me = _mesh_pos()
        for a in range(n):
            for k in range(1, N_DEV):
                _remote(srcs[a], dsts[a].at[_slot(me)], sems[a], sems[n + a], _flip(me, k)).start()

    return _split_call(name, body, list(arrays) + lands, [], True, after)


def _adamw_math(w, g, m, v):
    m = ADAM_B1 * m + (1.0 - ADAM_B1) * g
    v = ADAM_B2 * v + (1.0 - ADAM_B2) * (g * g)
    m_hat = m / (1.0 - ADAM_B1 ** ADAM_STEP)
    v_hat = v / (1.0 - ADAM_B2 ** ADAM_STEP)
    delta = -ADAM_LR * (m_hat / (jnp.sqrt(v_hat) + ADAM_EPS) + ADAM_WD * w)
    return delta, m, v


def _adamw(name, w, m, v, parts, layer, prev=None):
    _, r, c = w.shape
    tr = _pick(r, tuple(t for t in (256, 128, 64, 32, 16) if t * c <= ADAMW_TILE_ELEMS))
    n_prev = 0 if prev is None else 4

    def body(*refs):
        w_ref, m_ref, v_ref, p_ref = refs[:4]
        g_ref, d_ref, nm_ref, nv_ref = refs[4 + n_prev:]
        g = p_ref[0].astype(F32)
        for src in range(1, N_DEV):
            g = g + p_ref[src].astype(F32)
        delta, nm, nv = _adamw_math(w_ref[...], g, m_ref[...], v_ref[...])
        g_ref[...] = g
        d_ref[...] = delta
        nm_ref[...] = nm
        nv_ref[...] = nv

    wspec = pl.BlockSpec((None, tr, c), lambda i: (layer, i, 0))
    pspec = pl.BlockSpec((N_DEV, tr, c), lambda i: (0, i, 0))
    shp = jax.ShapeDtypeStruct(w.shape, F32)
    return pl.pallas_call(
        body, name=name, grid=(r // tr,), in_specs=[wspec] * 3 + [pspec] + [ANY_SPEC] * n_prev,
        out_specs=(wspec,) * 4, out_shape=(shp,) * 4, input_output_aliases={4 + j: j for j in range(n_prev)},
        compiler_params=_cparams(("parallel",)),
    )(w, m, v, parts, *([] if prev is None else prev))


PACK_TILE = 8 * 128


def _pack(arrays):
    flat = []
    for a in arrays:
        v = a.reshape(-1)
        pad = (-v.shape[0]) % PACK_TILE
        flat.append(jnp.pad(v, (0, pad)) if pad else v)
    return jnp.concatenate(flat).reshape(-1, 128)


def _unpack(buf, like):
    flat = buf.reshape(-1)
    out, off = [], 0
    for a in like:
        n = 1
        for dim in a.shape:
            n *= dim
        out.append(flat[off:off + n].reshape(a.shape))
        off += n + (-n) % PACK_TILE
    return out


def _sum_slots(name, gathered):
    _, r, c = gathered.shape

    def body(x_ref, o_ref):
        acc = x_ref[0].astype(F32)
        for src in range(1, N_DEV):
            acc = acc + x_ref[src].astype(F32)
        o_ref[...] = acc

    return pl.pallas_call(body, name=name, out_shape=jax.ShapeDtypeStruct((r, c), F32))(gathered)


def _adamw_small(w, g, m, v):
    shp = jax.ShapeDtypeStruct(w.shape, F32)

    def body(w_ref, g_ref, m_ref, v_ref, d_ref, nm_ref, nv_ref):
        delta, nm, nv = _adamw_math(w_ref[...], g_ref[...], m_ref[...], v_ref[...])
        d_ref[...] = delta
        nm_ref[...] = nm
        nv_ref[...] = nv

    return pl.pallas_call(body, name="adamw_small", out_shape=(shp,) * 3)(w, g, m, v)


def kernel(x, attn_norm_g, w_in, q_norm_g, k_norm_g, sgu_norm_g, sgu_w, sgu_b, out_norm_a_g, out_norm_b_g, w_out, ffn_norm_g, w_up, conv_w, conv_b, w_down, loss_target, m_attn_norm_g, m_w_in, m_q_norm_g, m_k_norm_g, m_sgu_norm_g, m_sgu_w, m_sgu_b, m_out_norm_a_g, m_out_norm_b_g, m_w_out, m_ffn_norm_g, m_w_up, m_conv_w, m_conv_b, m_w_down, v_attn_norm_g, v_w_in, v_q_norm_g, v_k_norm_g, v_sgu_norm_g, v_sgu_w, v_sgu_b, v_out_norm_a_g, v_out_norm_b_g, v_w_out, v_ffn_norm_g, v_w_up, v_conv_w, v_conv_b, v_w_down):
    depth = w_in.shape[0]
    s, d = x.shape[1], x.shape[2]
    n_heads = (d // 2) // HEAD_DIM
    sgu_col0 = 3 * n_heads
    f2 = w_up.shape[2] * N_DEV
    ff = f2 // 2
    my_slot = 4 * lax.axis_index("x") + 2 * lax.axis_index("y") + lax.axis_index("c")

    wb = [(w_in[l].astype(BF16), w_out[l].astype(BF16), w_up[l].astype(BF16), w_down[l].astype(BF16))
          for l in range(depth)]
    groups = {"in0": [wb[0][0]], "out0": [wb[0][1], conv_w.reshape(depth * CONV_WIDTH, -1)], "up0": [wb[0][2]],
              "down0": [wb[0][3]]}
    for l in range(1, depth):
        groups[f"in{l}"] = [wb[l][0], wb[l][1]]
        groups[f"ffn{l}"] = [wb[l][2], wb[l][3]]
    order = list(groups)
    started, relayed = {}, {}

    def start(gname, after):
        started[gname] = _gather_start(f"gather_{gname}_start", groups[gname], after)
        return started[gname].token

    def relay(gname, after):
        relayed[gname] = _gather_relay(f"gather_{gname}_relay", started[gname], after)
        token = relayed[gname].token
        k = order.index(gname)
        nxt = [k + 2] if k + 2 < len(order) - 1 else []
        if k == len(order) - 2:
            nxt = [k + 1]
        for j in nxt:
            token = start(order[j], token)
        return token

    def finish(gname, after):
        diagonal = _gather_relay_diagonal(f"gather_{gname}_diagonal", relayed[gname], after)
        return _gather_finish(f"gather_{gname}_finish", relayed[gname], diagonal, diagonal.token)

    conv_b_all = conv_b.reshape(depth, 2, 1, ff)
    sgu_b_col = sgu_b[..., None]
    token = start(order[1], start(order[0], attn_norm_g))
    token = relay("out0", relay("in0", token))
    win_g = finish("in0", token)[0]

    xs = x[0]
    saved = []
    gathered = []
    for l in range(depth):
        g1 = attn_norm_g[l][None]
        g2 = ffn_norm_g[l][None]
        gq, gk = q_norm_g[l][None], k_norm_g[l][None]
        ga = out_norm_a_g[l][:, None, :]
        gs = sgu_norm_g[l][:, None, :]
        gb = out_norm_b_g[l][:, None, :]
        h1 = _rmsnorm_fwd("attn_norm_fwd", xs, g1)
        p = _mm_nn_blocked("in_proj", h1, win_g, F32)
        att, o, rsum = _attn_fwd(p, gq, gk, ga, n_heads)
        token = relay("up0" if l == 0 else f"ffn{l}", att)
        sg = _sgu_fwd(p, gs, sgu_w[l], sgu_b_col[l], gb, sgu_col0, token)
        mix = jnp.concatenate([att, sg], axis=-1)
        if l == 0:
            wout_g, cw = finish("out0", mix)
            cw = jnp.transpose(cw.reshape(N_DEV, depth, CONV_WIDTH, -1), (1, 2, 0, 3)).reshape(depth, CONV_WIDTH, 2, ff)
            conv_w_all = jnp.transpose(cw, (0, 2, 1, 3))
        x1 = _mm_nn_res("out_proj", mix, wout_g.reshape(d, d), xs)
        h2 = _rmsnorm_fwd("ffn_norm_fwd", x1, g2)
        if l == 0:
            wup_g = finish("up0", h2)[0]
            token = relay("down0", wup_g)
            up, act = _up_conv_gate_fwd(h2, wup_g, conv_w_all[l], conv_b_all[l], after=token)
            wdown_g = finish("down0", up)[0]
        else:
            wup_g, wdown_g = finish(f"ffn{l}", h2)
            up, act = _up_conv_gate_fwd(h2, wup_g, conv_w_all[l], conv_b_all[l])
        saved.append((xs, h1, p, o, rsum, mix, x1, h2, up, act))
        gathered.append((win_g, wout_g, wup_g, wdown_g))
        if l + 1 < depth:
            token = relay(f"in{l + 1}", act)
            x2 = _mm_nn_res("down_proj", act, wdown_g.reshape(ff, d), x1, after=token)
            win_g, wout_g = finish(f"in{l + 1}", x2)
        else:
            x2 = _mm_nn_res("down_proj", act, wdown_g.reshape(ff, d), x1)
        xs = x2

    dx, dxb, loss_vec = _loss_head(xs, loss_target[0])
    loss = lax.psum(loss_vec[0, 0], MESH_AXES)

    exchanges = []
    small = [None] * depth
    small_names = ["attn_norm_g", "q_norm_g", "k_norm_g", "sgu_norm_g", "sgu_w", "sgu_b", "out_norm_a_g",
                   "out_norm_b_g", "ffn_norm_g", "conv_b", "conv_w"]
    for l in reversed(range(depth)):
        xs0, h1, p, o, rsum, mix, x1, h2, up, act = saved[l]
        win_g, wout_g, wup_g, wdown_g = gathered[l]
        wout_full = wout_g.reshape(d, d)
        wdown_full = wdown_g.reshape(ff, d)
        g1 = attn_norm_g[l][None]
        g2 = ffn_norm_g[l][None]
        gq, gk = q_norm_g[l][None], k_norm_g[l][None]
        ga = out_norm_a_g[l][:, None, :]
        gs = sgu_norm_g[l][:, None, :]
        gb = out_norm_b_g[l][:, None, :]
        d_wdown = _mm_tn_plain("down_proj_dw", act, dxb)
        exchanges.append((l, "down", ("w_down",), _exchange_start(f"grad_down{l}_start", [d_wdown], dx)))
        dup, d_cw, d_cb = _down_dx_conv_gate_bwd(up, conv_w_all[l], conv_b_all[l], dxb, wdown_full,
                                                 exchanges[-1][3].token)
        d_wup = _mm_tn_blocked("up_proj_dw", h2, dup, N_DEV, halves=True)
        exchanges.append((l, "up", ("w_up",), _exchange_start(f"grad_up{l}_start", [d_wup], d_cb)))
        dh2 = _mm_nt_blocked("up_proj_dx", dup, wup_g, halves=True, after=exchanges[-1][3].token)
        dx, dxb, d_g2 = _rmsnorm_bwd("ffn_norm_bwd", dh2, x1, g2, dx)
        d_wout = _mm_tn_plain("out_proj_dw", mix, dxb)
        dmix = _mm_nt_plain("out_proj_dx", dxb, wout_full)
        dq, dk, dv, d_gq, d_gk, d_ga = _attn_bwd(p, gq, gk, ga, o, rsum, dmix, n_heads)
        du, dvs, d_gs, d_sw, d_sb, d_gb = _sgu_bwd(p, gs, sgu_w[l], sgu_b_col[l], gb, dmix, sgu_col0, n_heads)
        dp = jnp.concatenate([dq, dk, dv, du, dvs], axis=-1)
        d_win = _mm_tn_blocked("in_proj_dw", h1, dp, N_DEV)
        exchanges.append((l, "mix", ("w_in", "w_out"), _exchange_start(f"grad_mix{l}_start", [d_win, d_wout], d_gq)))
        dh1 = _mm_nt_blocked("in_proj_dx", dp, win_g, after=exchanges[-1][3].token)
        dx, dxb, d_g1 = _rmsnorm_bwd("attn_norm_bwd", dh1, xs0, g1, dx)
        small[l] = dict(attn_norm_g=d_g1[0], q_norm_g=d_gq[0], k_norm_g=d_gk[0], sgu_norm_g=d_gs[:, 0], sgu_w=d_sw,
                        sgu_b=d_sb[..., 0], out_norm_a_g=d_ga[:, 0], out_norm_b_g=d_gb[:, 0], ffn_norm_g=d_g2[0],
                        conv_w=jnp.transpose(d_cw, (1, 0, 2)).reshape(CONV_WIDTH, f2), conv_b=d_cb.reshape(f2))
    grad_x = dx[None]

    f32_names = [n for n in small_names if n != "sgu_w"]
    small_g = [jnp.stack([small[l][n] for l in range(depth)]) for n in f32_names]
    sgu_w_g = jnp.stack([small[l]["sgu_w"] for l in range(depth)])
    small_sent = _broadcast_start("grad_small_start", [_pack(small_g), sgu_w_g.reshape(-1, TILE).astype(BF16)], dx)

    res = {}
    big = dict(w_in=(w_in, m_w_in, v_w_in), w_out=(w_out, m_w_out, v_w_out), w_up=(w_up, m_w_up, v_w_up),
               w_down=(w_down, m_w_down, v_w_down))
    after = [small_sent.token]
    for l, stage, names, ex in exchanges:
        landed = _exchange_finish(f"grad_{stage}{l}_finish", ex, after)
        after = []
        for name, parts in zip(names, landed):
            w, m, v = big[name]
            res[name] = _adamw(f"adamw_{name}", w, m, v, parts, l, res.get(name))
            after.append(res[name][0])
    small_all, sgu_w_all = _exchange_finish("grad_small_finish", small_sent, after)
    small_sum = _unpack(_sum_slots("small_grad_sum", small_all), small_g)
    g_small = dict(zip(f32_names, small_sum))
    g_small["sgu_w"] = _sum_slots("sgu_w_grad_sum", sgu_w_all).reshape(sgu_w.shape)
    cwn = conv_w.shape[2]
    g_small["conv_w"] = lax.dynamic_slice_in_dim(g_small["conv_w"], my_slot * cwn, cwn, axis=2)
    small_w = dict(attn_norm_g=(attn_norm_g, m_attn_norm_g, v_attn_norm_g), q_norm_g=(q_norm_g, m_q_norm_g, v_q_norm_g),
                   k_norm_g=(k_norm_g, m_k_norm_g, v_k_norm_g), sgu_norm_g=(sgu_norm_g, m_sgu_norm_g, v_sgu_norm_g),
                   sgu_w=(sgu_w, m_sgu_w, v_sgu_w), sgu_b=(sgu_b, m_sgu_b, v_sgu_b),
                   out_norm_a_g=(out_norm_a_g, m_out_norm_a_g, v_out_norm_a_g),
                   out_norm_b_g=(out_norm_b_g, m_out_norm_b_g, v_out_norm_b_g),
                   ffn_norm_g=(ffn_norm_g, m_ffn_norm_g, v_ffn_norm_g), conv_b=(conv_b, m_conv_b, v_conv_b),
                   conv_w=(conv_w, m_conv_w, v_conv_w))
    like = [small_w[n][0] for n in small_names]
    pw = _pack([small_w[n][0] for n in small_names])
    pm = _pack([small_w[n][1] for n in small_names])
    pv = _pack([small_w[n][2] for n in small_names])
    pg = _pack([g_small[n].reshape(small_w[n][0].shape) for n in small_names])
    pd, pnm, pnv = _adamw_small(pw, pg, pm, pv)
    for n, dlt, nm, nv in zip(small_names, _unpack(pd, like), _unpack(pnm, like), _unpack(pnv, like)):
        res[n] = (g_small[n].reshape(small_w[n][0].shape), dlt, nm, nv)

    order = ["attn_norm_g", "w_in", "q_norm_g", "k_norm_g", "sgu_norm_g", "sgu_w", "sgu_b", "out_norm_a_g",
             "out_norm_b_g", "w_out", "ffn_norm_g", "w_up", "conv_w", "conv_b", "w_down"]
    outs = [loss, grad_x]
    for field in range(4):
        outs += [res[n][field] for n in order]
    return tuple(outs)
```

```python
import functools

import jax
import jax.numpy as jnp
from jax import lax
from jax.experimental import pallas as pl
from jax.experimental.pallas import tpu as pltpu

F32 = jnp.float32
BF16 = jnp.bfloat16
EPS = 1e-6
HEAD_DIM = 128
TILE = 128
ATTN_VMEM_MB = 58
ATTN_HEADS_PER_STEP = 4
N_GROUPS = 8
CONV_WIDTH = 3
N_DEV = 8
MESH_AXES = ("x", "y", "c")
MIB = 1024 * 1024

ADAM_LR = 0.001
ADAM_B1 = 0.9
ADAM_B2 = 0.999
ADAM_EPS = 1e-08
ADAM_WD = 0.01
ADAM_STEP = 10
ADAMW_TILE_ELEMS = 160 * 1024

NT_DIMS = (((1,), (1,)), ((), ()))
NN_DIMS = (((1,), (0,)), ((), ()))
TN_DIMS = (((0,), (0,)), ((), ()))


def _cparams(sem, vmem_mb=48):
    return pltpu.CompilerParams(dimension_semantics=sem, vmem_limit_bytes=vmem_mb * MIB)


def _pick(n, cands):
    for c in cands:
        if n % c == 0:
            return c
    return n


def _mm(name, grid, ins, in_specs, out_shape, out_spec, dims, has_res=False, parts=None, vmem_mb=56, after=None):
    n_in = 2 + has_res + (after is not None)
    if after is not None:
        ins = tuple(ins) + (after,)
        in_specs = list(in_specs) + [pl.BlockSpec(after.shape, lambda *_: (0, 0))]

    def body(*refs):
        a_ref, b_ref = refs[:2]
        o_ref = refs[n_in]
        if parts is None:
            acc = lax.dot_general(a_ref[...], b_ref[...], dims, preferred_element_type=F32)
        else:
            acc = None
            for part in parts:
                a, b = part(a_ref, b_ref)
                prod = lax.dot_general(a, b, dims, preferred_element_type=F32)
                acc = prod if acc is None else acc + prod
        if has_res:
            acc = acc + refs[2][...]
        o_ref[...] = acc.astype(o_ref.dtype)

    return pl.pallas_call(
        body, name=name, grid=grid, in_specs=in_specs, out_specs=out_spec, out_shape=out_shape,
        compiler_params=_cparams(("parallel",) * len(grid), vmem_mb),
    )(*ins)


def _rows_for(m, row_bytes, budget):
    return _pick(m, tuple(t for t in (2048, 1024, 512, 256, 128) if t * row_bytes <= budget))


def _mm_nn_blocked(name, a, wb, out_dtype, halves=False, after=None):
    m, k = a.shape
    nb, _, bn = wb.shape
    tm = _rows_for(m, bn * jnp.dtype(out_dtype).itemsize, 6 * MIB)
    a_spec = pl.BlockSpec((tm, k), lambda j, i: (i, 0))
    b_spec = pl.BlockSpec((None, k, bn), lambda j, i: (j, 0, 0))
    if halves:
        hb = nb // 2
        out_shape = jax.ShapeDtypeStruct((2, m, hb * bn), out_dtype)
        o_spec = pl.BlockSpec((None, tm, bn), lambda j, i: (j // hb, i, j % hb))
    else:
        out_shape = jax.ShapeDtypeStruct((m, nb * bn), out_dtype)
        o_spec = pl.BlockSpec((tm, bn), lambda j, i: (i, j))
    return _mm(name, (nb, m // tm), (a, wb), [a_spec, b_spec], out_shape, o_spec, NN_DIMS, after=after)


def _mm_nn_res(name, a, w, res, after=None):
    m, k = a.shape
    n = w.shape[1]
    tm = _pick(m, (512, 256, 128))
    tn = _rows_for(n, k * 2, 12 * MIB)
    a_spec = pl.BlockSpec((tm, k), lambda j, i: (i, 0))
    b_spec = pl.BlockSpec((k, tn), lambda j, i: (0, j))
    r_spec = pl.BlockSpec((tm, tn), lambda j, i: (i, j))
    o_spec = pl.BlockSpec((tm, tn), lambda j, i: (i, j))
    return _mm(name, (n // tn, m // tm), (a, w, res), [a_spec, b_spec, r_spec], jax.ShapeDtypeStruct((m, n), F32),
               o_spec, NN_DIMS, has_res=True, after=after)


def _mm_nt_blocked(name, dy, wb, halves=False, after=None):
    nb, n, bn = wb.shape
    m = dy.shape[-2]
    tm = _pick(m, (512, 256, 128))
    tn = _rows_for(n, nb * bn * 2, 12 * MIB)
    if halves:
        hb = nb // 2
        a_spec = pl.BlockSpec((2, tm, hb * bn), lambda j, i: (0, i, 0))
        a_part = lambda kk: (lambda a_ref: a_ref[kk // hb, :, (kk % hb) * bn:(kk % hb + 1) * bn])
    else:
        a_spec = pl.BlockSpec((tm, nb * bn), lambda j, i: (i, 0))
        a_part = lambda kk: (lambda a_ref: a_ref[:, kk * bn:(kk + 1) * bn])
    parts = [(lambda a_ref, b_ref, kk=kk, sel=a_part(kk): (sel(a_ref), b_ref[kk])) for kk in range(nb)]
    b_spec = pl.BlockSpec((nb, tn, bn), lambda j, i: (0, j, 0))
    o_spec = pl.BlockSpec((tm, tn), lambda j, i: (i, j))
    return _mm(name, (n // tn, m // tm), (dy, wb), [a_spec, b_spec], jax.ShapeDtypeStruct((m, n), F32), o_spec,
               NT_DIMS, parts=parts, after=after)


def _mm_nt_plain(name, dy, w, out_dtype=F32):
    m, k = dy.shape
    n = w.shape[0]
    tm = _rows_for(m, k * 2, 8 * MIB)
    tn = _pick(n, (512, 256, 128))
    a_spec = pl.BlockSpec((tm, k), lambda j, i: (i, 0))
    b_spec = pl.BlockSpec((tn, k), lambda j, i: (j, 0))
    o_spec = pl.BlockSpec((tm, tn), lambda j, i: (i, j))
    return _mm(name, (n // tn, m // tm), (dy, w), [a_spec, b_spec], jax.ShapeDtypeStruct((m, n), out_dtype), o_spec,
               NT_DIMS)


def _mm_tn_blocked(name, a, dy, nb, halves=False):
    s, k1 = a.shape
    bn = (dy.shape[-1] * (2 if halves else 1)) // nb
    tm = _rows_for(k1, bn * 2, 6 * MIB)
    a_spec = pl.BlockSpec((s, tm), lambda j, i: (0, i))
    if halves:
        hb = nb // 2
        b_spec = pl.BlockSpec((None, s, bn), lambda j, i: (j // hb, 0, j % hb))
    else:
        b_spec = pl.BlockSpec((s, bn), lambda j, i: (0, j))
    o_spec = pl.BlockSpec((None, tm, bn), lambda j, i: (j, i, 0))
    return _mm(name, (nb, k1 // tm), (a, dy), [a_spec, b_spec], jax.ShapeDtypeStruct((nb, k1, bn), BF16), o_spec,
               TN_DIMS)


def _mm_tn_plain(name, a, dy):
    s, k1 = a.shape
    n = dy.shape[1]
    tm = _pick(k1, (512, 256, 128))
    tn = _rows_for(n, s * 2, 8 * MIB)
    a_spec = pl.BlockSpec((s, tm), lambda i, j: (0, i))
    b_spec = pl.BlockSpec((s, tn), lambda i, j: (0, j))
    o_spec = pl.BlockSpec((tm, tn), lambda i, j: (i, j))
    return _mm(name, (k1 // tm, n // tn), (a, dy), [a_spec, b_spec], jax.ShapeDtypeStruct((k1, n), BF16), o_spec,
               TN_DIMS)


def _rstd(x):
    return lax.rsqrt(jnp.mean(x * x, axis=-1, keepdims=True) + EPS)


def _norm_bwd(dy, xhat, r, g):
    dxhat = dy * g
    return r * (dxhat - xhat * jnp.mean(dxhat * xhat, axis=-1, keepdims=True))


def _rmsnorm_fwd(name, x, g):
    s, d = x.shape
    tr = _pick(s, (256, 128))

    def body(x_ref, g_ref, h_ref):
        xv = x_ref[...]
        h_ref[...] = (xv * _rstd(xv) * g_ref[...]).astype(BF16)

    return pl.pallas_call(
        body, name=name, grid=(s // tr,),
        in_specs=[pl.BlockSpec((tr, d), lambda i: (i, 0)), pl.BlockSpec((1, d), lambda i: (0, 0))],
        out_specs=pl.BlockSpec((tr, d), lambda i: (i, 0)),
        out_shape=jax.ShapeDtypeStruct((s, d), BF16), compiler_params=_cparams(("parallel",)),
    )(x, g)


def _rmsnorm_bwd(name, dh, x, g, dres):
    s, d = x.shape
    tr = _pick(s, (256, 128))

    def body(dh_ref, x_ref, g_ref, dres_ref, dx_ref, dxb_ref, dg_ref):
        xv = x_ref[...]
        r = _rstd(xv)
        xhat = xv * r
        dhv = dh_ref[...]
        dx = dres_ref[...] + _norm_bwd(dhv, xhat, r, g_ref[...])
        dx_ref[...] = dx
        dxb_ref[...] = dx.astype(BF16)
        part = jnp.sum(dhv * xhat, axis=0, keepdims=True)

        @pl.when(pl.program_id(0) == 0)
        def _():
            dg_ref[...] = part

        @pl.when(pl.program_id(0) > 0)
        def _():
            dg_ref[...] += part

    row = pl.BlockSpec((tr, d), lambda i: (i, 0))
    vec = pl.BlockSpec((1, d), lambda i: (0, 0))
    return pl.pallas_call(
        body, name=name, grid=(s // tr,), in_specs=[row, row, vec, row], out_specs=(row, row, vec),
        out_shape=(jax.ShapeDtypeStruct((s, d), F32), jax.ShapeDtypeStruct((s, d), BF16),
                   jax.ShapeDtypeStruct((1, d), F32)),
        compiler_params=_cparams(("arbitrary",)),
    )(dh, x, g, dres)


def _loss_head(y, target):
    s, d = y.shape
    tr = _pick(s, (256, 128))

    def body(y_ref, t_ref, dy_ref, dyb_ref, loss_ref):
        err = y_ref[...] - t_ref[...]
        dy = err * (1.0 / d)
        dy_ref[...] = dy
        dyb_ref[...] = dy.astype(BF16)
        part = 0.5 * jnp.sum(jnp.mean(err * err, axis=-1, keepdims=True), axis=0, keepdims=True)
        part = jnp.broadcast_to(part, (1, 128))

        @pl.when(pl.program_id(0) == 0)
        def _():
            loss_ref[...] = part

        @pl.when(pl.program_id(0) > 0)
        def _():
            loss_ref[...] += part

    row = pl.BlockSpec((tr, d), lambda i: (i, 0))
    return pl.pallas_call(
        body, name="loss_head", grid=(s // tr,), in_specs=[row, row],
        out_specs=(row, row, pl.BlockSpec((1, 128), lambda i: (0, 0))),
        out_shape=(jax.ShapeDtypeStruct((s, d), F32), jax.ShapeDtypeStruct((s, d), BF16),
                   jax.ShapeDtypeStruct((1, 128), F32)),
        compiler_params=_cparams(("arbitrary",)),
    )(y, target)


def _split_dot(x, tri):
    hi = x.astype(BF16)
    lo = (x - hi.astype(F32)).astype(BF16)
    return (jnp.dot(hi, tri, preferred_element_type=F32) + jnp.dot(lo, tri, preferred_element_type=F32))


def _tile_iotas():
    row = lax.broadcasted_iota(jnp.int32, (TILE, TILE), 0)
    col = lax.broadcasted_iota(jnp.int32, (TILE, TILE), 1)
    return row, col


def _sb_logits(qi, kb, mask):
    z = lax.dot_general(qi, kb, NT_DIMS, preferred_element_type=F32) * (HEAD_DIM ** -0.5)
    sp = jnp.log(1.0 + jnp.exp(-jnp.abs(z)))
    lb = jnp.minimum(z, 0.0) - sp
    l1m = -jnp.maximum(z, 0.0) - sp
    if mask is not None:
        l1m = jnp.where(mask, l1m, 0.0)
    return lb, l1m


def _attn_fwd(p, gq, gk, ga, n_heads):
    s = p.shape[0]
    nq = s // TILE

    hp = ATTN_HEADS_PER_STEP
    wd = hp * HEAD_DIM

    def body(q_ref, k_ref, v_ref, gq_ref, gk_ref, ga_ref, att_ref, o_ref, r_ref, qn_s, kn_s, vb_s):
        heads = [slice(hh * HEAD_DIM, (hh + 1) * HEAD_DIM) for hh in range(hp)]
        for hd in heads:
            qv = q_ref[:, hd]
            qn_s[:, hd] = (qv * _rstd(qv) * gq_ref[...]).astype(BF16)
            kv = k_ref[:, hd]
            kn_s[:, hd] = (kv * _rstd(kv) * gk_ref[...]).astype(BF16)
        vb_s[...] = v_ref[...].astype(BF16)
        row, col = _tile_iotas()
        causal = col < row
        upper_ones = jnp.concatenate([(row > col).astype(BF16), jnp.ones((TILE, TILE), BF16)], axis=1)

        def tiles(rows, key_blocks, states):
            chains = [(hi, hd, keys, mask) for hi, hd in enumerate(heads) for keys, mask in key_blocks]
            logits = [_sb_logits(qn_s[rows, hd], kn_s[keys, hd], mask) for _, hd, keys, mask in chains]
            sums = [_split_dot(l1m, upper_ones) for _, l1m in logits]
            carry = [c for _, c in states]
            probs = []
            for (hi, _, _, mask), (lb, _), sm in zip(chains, logits, sums):
                a = jnp.exp(lb + sm[:, :TILE] + carry[hi])
                carry[hi] = carry[hi] + sm[:, TILE:]
                probs.append((a if mask is None else jnp.where(mask, a, 0.0)).astype(BF16))
            outs = [jnp.dot(a, vb_s[keys, hd], preferred_element_type=F32) for a, (_, hd, keys, _) in zip(probs, chains)]
            acc = [o_acc for o_acc, _ in states]
            for (hi, _, _, _), o in zip(chains, outs):
                acc[hi] = acc[hi] + o
            return tuple(zip(acc, carry))

        def key_block(b):
            return pl.ds(pl.multiple_of(b * TILE, TILE), TILE), None

        def qblock(i, _):
            rows = pl.ds(pl.multiple_of(i * TILE, TILE), TILE)
            zero = jnp.zeros((TILE, HEAD_DIM), F32)
            states = tuple((zero, zero) for _ in heads)
            states = lax.cond(i % 2 == 1, lambda st: tiles(rows, [(rows, causal), key_block(i - 1)], st),
                              lambda st: tiles(rows, [(rows, causal)], st), states)
            top = i - i % 2

            def kblocks(jj, states):
                return tiles(rows, [key_block(top - 1 - 2 * jj), key_block(top - 2 - 2 * jj)], states)

            states = lax.fori_loop(0, i // 2, kblocks, states)
            for hh, (hd, (o_acc, c)) in enumerate(zip(heads, states)):
                o_ref[rows, hd] = o_acc
                r_ref[rows, hd] = c
                att_ref[rows, hd] = (o_acc * _rstd(o_acc) * ga_ref[hh]).astype(BF16)
            return 0

        lax.fori_loop(0, nq, qblock, 0)

    col_blk = lambda off: pl.BlockSpec((s, wd), lambda h: (0, off + h))
    vec = pl.BlockSpec((1, HEAD_DIM), lambda h: (0, 0))
    hvec = pl.BlockSpec((hp, 1, HEAD_DIM), lambda h: (h, 0, 0))
    out = pl.BlockSpec((s, wd), lambda h: (0, h))
    w = n_heads * HEAD_DIM
    steps = n_heads // hp
    return pl.pallas_call(
        body, name="attn_fwd", grid=(steps,),
        in_specs=[col_blk(0), col_blk(steps), col_blk(2 * steps), vec, vec, hvec],
        out_specs=(out, out, out),
        out_shape=(jax.ShapeDtypeStruct((s, w), BF16), jax.ShapeDtypeStruct((s, w), F32),
                   jax.ShapeDtypeStruct((s, w), F32)),
        scratch_shapes=[pltpu.VMEM((s, wd), BF16)] * 3,
        compiler_params=_cparams(("parallel",), ATTN_VMEM_MB),
    )(p, p, p, gq, gk, ga)


def _attn_bwd(p, gq, gk, ga, o, rsum, dmix, n_heads):
    s = p.shape[0]
    nq = s // TILE

    hp = ATTN_HEADS_PER_STEP
    wd = hp * HEAD_DIM
    scale = HEAD_DIM ** -0.5

    def body(q_ref, k_ref, v_ref, gq_ref, gk_ref, ga_ref, o_ref, r_ref, dm_ref,
             dq_ref, dk_ref, dv_ref, dgq_ref, dgk_ref, dga_ref,
             qn_s, kn_s, vb_s, do_s, dqn_s, dkn_s, dv_s):
        step = pl.program_id(0)
        gqv, gkv = gq_ref[...], gk_ref[...]
        heads = [slice(hh * HEAD_DIM, (hh + 1) * HEAD_DIM) for hh in range(hp)]
        for hh, hd in enumerate(heads):
            qv = q_ref[:, hd]
            qn_s[:, hd] = (qv * _rstd(qv) * gqv).astype(BF16)
            kv = k_ref[:, hd]
            kn_s[:, hd] = (kv * _rstd(kv) * gkv).astype(BF16)
            ov = o_ref[:, hd]
            ro = _rstd(ov)
            ohat = ov * ro
            dm = dm_ref[:, hd]
            dga_ref[hh] = jnp.sum(dm * ohat, axis=0, keepdims=True)
            do_s[:, hd] = _norm_bwd(dm, ohat, ro, ga_ref[hh]).astype(BF16)
        vb_s[...] = v_ref[...].astype(BF16)
        dkn_s[...] = jnp.zeros_like(dkn_s)
        dv_s[...] = jnp.zeros_like(dv_s)
        row, col = _tile_iotas()
        causal = col < row
        ones = jnp.ones((TILE, TILE), BF16)
        incl_ones = jnp.concatenate([(row <= col).astype(BF16), ones], axis=1)
        excl_ones = jnp.concatenate([(row < col).astype(BF16), ones], axis=1)

        def tiles(rows, key_blocks, states):
            chains = [(hi, hd, keys, mask) for hi, hd in enumerate(heads) for keys, mask in key_blocks]
            qis = [qn_s[rows, hd] for hd in heads]
            dois = [do_s[rows, hd] for hd in heads]
            logits = [_sb_logits(qis[hi], kn_s[keys, hd], mask) for hi, hd, keys, mask in chains]
            sums = [_split_dot(l1m, incl_ones) for _, l1m in logits]
            das = [lax.dot_general(dois[hi], vb_s[keys, hd], NT_DIMS, preferred_element_type=F32)
                   for hi, hd, keys, _ in chains]
            pfx = [st[1] for st in states]
            probs, dss = [], []
            for (hi, hd, _, mask), (lb, _), sm, da in zip(chains, logits, sums, das):
                a = jnp.exp(lb + (r_ref[rows, hd] - pfx[hi] - sm[:, :TILE]))
                pfx[hi] = pfx[hi] + sm[:, TILE:]
                a = a if mask is None else jnp.where(mask, a, 0.0)
                probs.append(a.astype(BF16))
                dss.append(da * a)
            dsums = [_split_dot(ds, excl_ones) for ds in dss]
            pc = [st[2] for st in states]
            dzs = []
            for (hi, _, _, mask), (lb, _), ds, dsm in zip(chains, logits, dss, dsums):
                dl1m = pc[hi] + dsm[:, :TILE]
                pc[hi] = pc[hi] + dsm[:, TILE:]
                dl1m = dl1m if mask is None else jnp.where(mask, dl1m, 0.0)
                beta = jnp.exp(lb)
                dzs.append(((ds * (1.0 - beta) - dl1m * beta) * scale).astype(BF16))
            dqs = [jnp.dot(dz, kn_s[keys, hd], preferred_element_type=F32) for dz, (_, hd, keys, _) in zip(dzs, chains)]
            for dz, a, (hi, hd, keys, _) in zip(dzs, probs, chains):
                dkn_s[keys, hd] += lax.dot_general(dz, qis[hi], TN_DIMS, preferred_element_type=F32)
                dv_s[keys, hd] += lax.dot_general(a, dois[hi], TN_DIMS, preferred_element_type=F32)
            dq_acc = [st[0] for st in states]
            for (hi, _, _, _), dq in zip(chains, dqs):
                dq_acc[hi] = dq_acc[hi] + dq
            return tuple(zip(dq_acc, pfx, pc))

        def key_block(b):
            return pl.ds(pl.multiple_of(b * TILE, TILE), TILE), None

        def qblock(i, _):
            rows = pl.ds(pl.multiple_of(i * TILE, TILE), TILE)
            zero = jnp.zeros((TILE, HEAD_DIM), F32)

            def kblocks(jj, states):
                return tiles(rows, [key_block(2 * jj), key_block(2 * jj + 1)], states)

            states = lax.fori_loop(0, i // 2, kblocks, tuple((zero, zero, zero) for _ in heads))
            states = lax.cond(i % 2 == 1, lambda st: tiles(rows, [key_block(i - 1), (rows, causal)], st),
                              lambda st: tiles(rows, [(rows, causal)], st), states)
            for hd, (dq_acc, _, _) in zip(heads, states):
                dqn_s[rows, hd] = dq_acc
            return 0

        lax.fori_loop(0, nq, qblock, 0)

        def norm_in_bwd(x_ref, g, dn_s, dx_ref, dg_ref):
            part = jnp.zeros((1, HEAD_DIM), F32)
            for hd in heads:
                xv = x_ref[:, hd]
                r = _rstd(xv)
                xhat = xv * r
                dn = dn_s[:, hd]
                dx_ref[:, hd] = _norm_bwd(dn, xhat, r, g).astype(BF16)
                part = part + jnp.sum(dn * xhat, axis=0, keepdims=True)

            @pl.when(step == 0)
            def _():
                dg_ref[...] = part

            @pl.when(step > 0)
            def _():
                dg_ref[...] += part

        norm_in_bwd(q_ref, gqv, dqn_s, dq_ref, dgq_ref)
        norm_in_bwd(k_ref, gkv, dkn_s, dk_ref, dgk_ref)
        dv_ref[...] = dv_s[...].astype(BF16)

    once = pl.Buffered(1)
    steps = n_heads // hp
    col_blk = lambda off: pl.BlockSpec((s, wd), lambda h: (0, off + h), pipeline_mode=once)
    vec = pl.BlockSpec((1, HEAD_DIM), lambda h: (0, 0))
    hvec = pl.BlockSpec((hp, 1, HEAD_DIM), lambda h: (h, 0, 0))
    blk = pl.BlockSpec((s, wd), lambda h: (0, h), pipeline_mode=once)
    w = n_heads * HEAD_DIM
    big = jax.ShapeDtypeStruct((s, w), BF16)
    return pl.pallas_call(
        body, name="attn_bwd", grid=(steps,),
        in_specs=[col_blk(0), col_blk(steps), col_blk(2 * steps), vec, vec, hvec, blk, blk, blk],
        out_specs=(blk, blk, blk, vec, vec, hvec),
        out_shape=(big, big, big, jax.ShapeDtypeStruct((1, HEAD_DIM), F32), jax.ShapeDtypeStruct((1, HEAD_DIM), F32),
                   jax.ShapeDtypeStruct((n_heads, 1, HEAD_DIM), F32)),
        scratch_shapes=[pltpu.VMEM((s, wd), BF16)] * 4 + [pltpu.VMEM((s, wd), F32)] * 3,
        compiler_params=_cparams(("arbitrary",), ATTN_VMEM_MB),
    )(p, p, p, gq, gk, ga, o, rsum, dmix)


_INV_SQRT2 = 0.7071067811865476
_INV_SQRT_2PI = 0.3989422804014327


def _gelu(x):
    return 0.5 * x * (1.0 + lax.erf(x * _INV_SQRT2))


def _gelu_grad(x):
    return 0.5 * (1.0 + lax.erf(x * _INV_SQRT2)) + x * (_INV_SQRT_2PI * jnp.exp(-0.5 * x * x))


def _sgu_fwd(p, gs, w_s, b_s, gb, col0, after):
    s = p.shape[0]
    n_chunks = s // TILE
    per_trip = _pick(n_chunks, (4, 2, 1))

    def body(u_ref, v_ref, gs_ref, w_ref, b_ref, gb_ref, _, out_ref, vs_s):
        vg = _gelu(v_ref[...])
        vs_s[...] = (vg * _rstd(vg) * gs_ref[...]).astype(BF16)
        row, col = _tile_iotas()
        wt = jnp.where(col <= row, w_ref[...], 0.0).astype(BF16)
        bcol = b_ref[...]
        gbv = gb_ref[...]

        def chunks(c, _):
            rows = [pl.ds(pl.multiple_of((c * per_trip + k) * TILE, TILE), TILE) for k in range(per_trip)]
            mixed = [jnp.dot(wt, vs_s[r, :], preferred_element_type=F32) + bcol for r in rows]
            sgs = [_gelu(u_ref[r, :]) * mx for r, mx in zip(rows, mixed)]
            for r, sg in zip(rows, sgs):
                out_ref[r, :] = (sg * _rstd(sg) * gbv).astype(BF16)
            return 0

        lax.fori_loop(0, n_chunks // per_trip, chunks, 0)

    col_blk = lambda off: pl.BlockSpec((s, HEAD_DIM), lambda g: (0, off + g))
    gvec = pl.BlockSpec((None, 1, HEAD_DIM), lambda g: (g, 0, 0))
    return pl.pallas_call(
        body, name="sgu_fwd", grid=(N_GROUPS,),
        in_specs=[col_blk(col0), col_blk(col0 + N_GROUPS), gvec,
                  pl.BlockSpec((None, TILE, TILE), lambda g: (g, 0, 0)),
                  pl.BlockSpec((None, TILE, 1), lambda g: (g, 0, 0)), gvec,
                  pl.BlockSpec(after.shape, lambda g: (0, 0))],
        out_specs=pl.BlockSpec((s, HEAD_DIM), lambda g: (0, g)),
        out_shape=jax.ShapeDtypeStruct((s, N_GROUPS * HEAD_DIM), BF16),
        scratch_shapes=[pltpu.VMEM((s, HEAD_DIM), BF16)],
        compiler_params=_cparams(("parallel",)),
    )(p, p, gs, w_s, b_s, gb, after)


def _sgu_bwd(p, gs, w_s, b_s, gb, dmix, col0, dm_col0):
    s = p.shape[0]
    n_chunks = s // TILE
    per_trip = _pick(n_chunks, (4, 2, 1))

    def body(u_ref, v_ref, gs_ref, w_ref, b_ref, gb_ref, dm_ref,
             du_ref, dv_ref, dgs_ref, dw_ref, db_ref, dgb_ref, vs_s, dvs_s):
        gsv = gs_ref[...]
        gbv = gb_ref[...]
        vg = _gelu(v_ref[...])
        vs_s[...] = (vg * _rstd(vg) * gsv).astype(BF16)
        row, col = _tile_iotas()
        causal = col <= row
        wt = jnp.where(causal, w_ref[...], 0.0).astype(BF16)
        bcol = b_ref[...]

        def chunks(c, carry):
            dw_acc, db_acc, dgb_acc = carry
            rows = [pl.ds(pl.multiple_of((c * per_trip + k) * TILE, TILE), TILE) for k in range(per_trip)]
            vss = [vs_s[r, :] for r in rows]
            mixed = [jnp.dot(wt, vs, preferred_element_type=F32) + bcol for vs in vss]
            dmbs = []
            for r, mx in zip(rows, mixed):
                u_pre = u_ref[r, :]
                u = _gelu(u_pre)
                sg = u * mx
                rs = _rstd(sg)
                sghat = sg * rs
                dm = dm_ref[r, :]
                dsg = _norm_bwd(dm, sghat, rs, gbv)
                dgb_acc = dgb_acc + jnp.sum(dm * sghat, axis=0, keepdims=True)
                du_ref[r, :] = (dsg * mx * _gelu_grad(u_pre)).astype(BF16)
                dmixed = dsg * u
                db_acc = db_acc + jnp.sum(dmixed, axis=1, keepdims=True)
                dmbs.append(dmixed.astype(BF16))
            for dmb, vs in zip(dmbs, vss):
                dw_acc = dw_acc + lax.dot_general(dmb, vs, NT_DIMS, preferred_element_type=F32)
            for r, dmb in zip(rows, dmbs):
                dvs_s[r, :] = lax.dot_general(wt, dmb, TN_DIMS, preferred_element_type=F32)
            return dw_acc, db_acc, dgb_acc

        dw_acc, db_acc, dgb_acc = lax.fori_loop(
            0, n_chunks // per_trip, chunks,
            (jnp.zeros((TILE, TILE), F32), jnp.zeros((TILE, 1), F32), jnp.zeros((1, HEAD_DIM), F32)))
        dw_ref[...] = jnp.where(causal, dw_acc, 0.0)
        db_ref[...] = db_acc
        dgb_ref[...] = dgb_acc
        v_pre = v_ref[...]
        vg = _gelu(v_pre)
        rv = _rstd(vg)
        vhat = vg * rv
        dvs = dvs_s[...]
        dgs_ref[...] = jnp.sum(dvs * vhat, axis=0, keepdims=True)
        dv_ref[...] = (_norm_bwd(dvs, vhat, rv, gsv) * _gelu_grad(v_pre)).astype(BF16)

    col_blk = lambda off: pl.BlockSpec((s, HEAD_DIM), lambda g: (0, off + g))
    gvec = pl.BlockSpec((None, 1, HEAD_DIM), lambda g: (g, 0, 0))
    wspec = pl.BlockSpec((None, TILE, TILE), lambda g: (g, 0, 0))
    bspec = pl.BlockSpec((None, TILE, 1), lambda g: (g, 0, 0))
    blk = pl.BlockSpec((s, HEAD_DIM), lambda g: (0, g))
    big = jax.ShapeDtypeStruct((s, N_GROUPS * HEAD_DIM), BF16)
    gshape = jax.ShapeDtypeStruct((N_GROUPS, 1, HEAD_DIM), F32)
    return pl.pallas_call(
        body, name="sgu_bwd", grid=(N_GROUPS,),
        in_specs=[col_blk(col0), col_blk(col0 + N_GROUPS), gvec, wspec, bspec, gvec, col_blk(dm_col0)],
        out_specs=(blk, blk, gvec, wspec, bspec, gvec),
        out_shape=(big, big, gshape, jax.ShapeDtypeStruct((N_GROUPS, TILE, TILE), F32),
                   jax.ShapeDtypeStruct((N_GROUPS, TILE, 1), F32), gshape),
        scratch_shapes=[pltpu.VMEM((s, HEAD_DIM), BF16), pltpu.VMEM((s, HEAD_DIM), F32)],
        compiler_params=_cparams(("parallel",)),
    )(p, p, gs, w_s, b_s, gb, dmix)


SUBLANES = 8


def _shift_down(x, n):
    rolled = pltpu.roll(x, n, 0)
    edge = lax.broadcasted_iota(jnp.int32, (SUBLANES, x.shape[1]), 0)
    return jnp.concatenate([jnp.where(edge >= n, rolled[:SUBLANES], 0.0), rolled[SUBLANES:]], axis=0)


def _shift_up(x, n):
    s = x.shape[0]
    rolled = pltpu.roll(x, s - n, 0)
    edge = lax.broadcasted_iota(jnp.int32, (SUBLANES, x.shape[1]), 0)
    return jnp.concatenate([rolled[:s - SUBLANES], jnp.where(edge < SUBLANES - n, rolled[s - SUBLANES:], 0.0)], axis=0)


def _conv(x, w, b):
    x1, x2 = _shift_down(x, 1), _shift_down(x, 2)
    return b + w[0:1, :] * x2 + w[1:2, :] * x1 + w[2:3, :] * x, x1, x2


def _conv_specs(s, tn):
    xspec = pl.BlockSpec((2, s, tn), lambda j: (0, 0, j))
    wspec = pl.BlockSpec((2, CONV_WIDTH, tn), lambda j: (0, 0, j))
    bspec = pl.BlockSpec((2, 1, tn), lambda j: (0, 0, j))
    return xspec, wspec, bspec


def _conv_gate_fwd(up, cw, cb):
    _, s, f = up.shape
    tn = _pick(f, (256, 128))

    def body(x_ref, w_ref, b_ref, act_ref):
        gate = _conv(x_ref[0], w_ref[0], b_ref[0])[0]
        val = _conv(x_ref[1], w_ref[1], b_ref[1])[0]
        act_ref[...] = (gate * jax.nn.sigmoid(gate) * val).astype(BF16)

    xspec, wspec, bspec = _conv_specs(s, tn)
    return pl.pallas_call(
        body, name="conv_gate_fwd", grid=(f // tn,), in_specs=[xspec, wspec, bspec],
        out_specs=pl.BlockSpec((s, tn), lambda j: (0, j)), out_shape=jax.ShapeDtypeStruct((s, f), BF16),
        compiler_params=_cparams(("parallel",)),
    )(up, cw, cb)


def _up_conv_gate_fwd(h, wb, cw, cb, after=None):
    s, k = h.shape
    nb, _, bn = wb.shape
    hb = nb // 2
    f = hb * bn
    tm = _pick(s, (512, 256, 128))
    n_in = 5 + (after is not None)

    def body(*refs):
        h_ref, wg_ref, wv_ref, w_ref, b_ref = refs[:5]
        up_ref, act_ref, halo_s = refs[n_in:]
        first = pl.program_id(1) == 0
        outs = []
        for half, wt_ref in enumerate((wg_ref, wv_ref)):
            x = jnp.dot(h_ref[...], wt_ref[...], preferred_element_type=F32)
            up_ref[half] = x
            halo = jnp.where(first, 0.0, halo_s[half])
            halo_s[half] = x[tm - SUBLANES:]
            full = jnp.concatenate([halo, x], axis=0)
            x1 = pltpu.roll(full, 1, 0)[SUBLANES:]
            x2 = pltpu.roll(full, 2, 0)[SUBLANES:]
            w = w_ref[half]
            outs.append(b_ref[half] + w[0:1, :] * x2 + w[1:2, :] * x1 + w[2:3, :] * x)
        gate, val = outs
        act_ref[...] = (gate * jax.nn.sigmoid(gate) * val).astype(BF16)

    ins = [h, wb, wb, cw, cb]
    in_specs = [pl.BlockSpec((tm, k), lambda j, i: (i, 0)),
                pl.BlockSpec((None, k, bn), lambda j, i: (j, 0, 0)),
                pl.BlockSpec((None, k, bn), lambda j, i: (j + hb, 0, 0)),
                pl.BlockSpec((2, CONV_WIDTH, bn), lambda j, i: (0, 0, j)),
                pl.BlockSpec((2, 1, bn), lambda j, i: (0, 0, j))]
    if after is not None:
        ins.append(after)
        in_specs.append(pl.BlockSpec(after.shape, lambda j, i: (0, 0)))
    return pl.pallas_call(
        body, name="up_conv_gate_fwd", grid=(hb, s // tm), in_specs=in_specs,
        out_specs=(pl.BlockSpec((2, tm, bn), lambda j, i: (0, i, j)), pl.BlockSpec((tm, bn), lambda j, i: (i, j))),
        out_shape=(jax.ShapeDtypeStruct((2, s, f), F32), jax.ShapeDtypeStruct((s, f), BF16)),
        scratch_shapes=[pltpu.VMEM((2, SUBLANES, bn), F32)],
        compiler_params=_cparams(("parallel", "arbitrary"), 56),
    )(*ins)


def _down_dx_conv_gate_bwd(up, cw, cb, dy, wdown, after):
    _, s, f = up.shape
    d = dy.shape[1]
    tn = _pick(f, (256, 128))

    def body(x_ref, w_ref, b_ref, dy_ref, wd_ref, _, dx_ref, dw_ref, db_ref):
        da = lax.dot_general(dy_ref[...], wd_ref[...], NT_DIMS, preferred_element_type=F32)
        xg, xv = x_ref[0], x_ref[1]
        wg, wv = w_ref[0], w_ref[1]
        gate, xg1, xg2 = _conv(xg, wg, b_ref[0])
        val, xv1, xv2 = _conv(xv, wv, b_ref[1])
        sig = jax.nn.sigmoid(gate)
        dval = da * (gate * sig)
        dgate = da * val * (sig * (1.0 + gate * (1.0 - sig)))
        for half, (x, x1, x2, w, dz) in enumerate(((xg, xg1, xg2, wg, dgate), (xv, xv1, xv2, wv, dval))):
            dx_ref[half] = (w[2:3, :] * dz + w[1:2, :] * _shift_up(dz, 1) + w[0:1, :] * _shift_up(dz, 2)).astype(BF16)
            dw_ref[half, 0:1, :] = jnp.sum(dz * x2, axis=0, keepdims=True)
            dw_ref[half, 1:2, :] = jnp.sum(dz * x1, axis=0, keepdims=True)
            dw_ref[half, 2:3, :] = jnp.sum(dz * x, axis=0, keepdims=True)
            db_ref[half] = jnp.sum(dz, axis=0, keepdims=True)

    xspec, wspec, bspec = _conv_specs(s, tn)
    return pl.pallas_call(
        body, name="down_dx_conv_gate_bwd", grid=(f // tn,),
        in_specs=[xspec, wspec, bspec, pl.BlockSpec((s, d), lambda j: (0, 0)), pl.BlockSpec((tn, d), lambda j: (j, 0)),
                  pl.BlockSpec(after.shape, lambda j: (0, 0))],
        out_specs=(xspec, wspec, bspec),
        out_shape=(jax.ShapeDtypeStruct((2, s, f), BF16), jax.ShapeDtypeStruct((2, CONV_WIDTH, f), F32),
                   jax.ShapeDtypeStruct((2, 1, f), F32)),
        compiler_params=_cparams(("parallel",), 56),
    )(up, cw, cb, dy, wdown, after)


def _mesh_pos():
    return lax.axis_index("x"), lax.axis_index("y"), lax.axis_index("c")


def _remote(src, dst, send_sem, recv_sem, to):
    return pltpu.make_async_remote_copy(src_ref=src, dst_ref=dst, send_sem=send_sem, recv_sem=recv_sem,
                                        device_id=to, device_id_type=pl.DeviceIdType.MESH)


HBM_SPEC = pl.BlockSpec(memory_space=pltpu.HBM)
SEM_SPEC = pl.BlockSpec(memory_space=pltpu.SEMAPHORE)
ANY_SPEC = pl.BlockSpec(memory_space=pl.ANY)
TOKEN_SPEC = pl.BlockSpec(memory_space=pltpu.VMEM)
TOKEN_SHAPE = jax.ShapeDtypeStruct((8, 128), F32)
DATAFLOW = pltpu.SideEffectType.DATAFLOW_SIDE_EFFECTING
GATHER_PLANE = (2, 4, 6)


def _slot(pos):
    return 4 * pos[0] + 2 * pos[1] + pos[2]


def _flip(pos, k):
    return (pos[0] ^ ((k >> 2) & 1), pos[1] ^ ((k >> 1) & 1), pos[2] ^ (k & 1))


def _hbm(a):
    return pltpu.with_memory_space_constraint(a, pltpu.HBM)


def _hbm_shapes(arrays):
    return tuple(pltpu.HBM(a.shape, a.dtype) for a in arrays)


class _Split:
    def __init__(self, n, outs, n_sets):
        k = 2 * n * int(n_sets)
        self.n = n
        self.sems = list(outs[:k])
        self.bufs = list(outs[k:k + 2 * n])
        self.token = outs[-1]

    def sem_set(self, i):
        return self.sems[2 * self.n * i:2 * self.n * (i + 1)]


def _split_call(name, body, bufs, sems_in, n_sets, after):
    n = len(bufs) // 2
    k = 2 * n * int(n_sets)
    m = len(sems_in)
    afters = list(after) if isinstance(after, (list, tuple)) else [after]
    na = len(afters)

    def wrapped(*refs):
        srcs, dsts = refs[:n], refs[n:2 * n]
        s_in = refs[2 * n:2 * n + m]
        s_out = refs[2 * n + m + na:2 * n + m + na + k]
        token, local_sems = refs[-2], refs[-1]
        body(srcs, dsts, s_in, s_out, local_sems)
        token[...] = jnp.zeros_like(token)

    outs = pl.pallas_call(
        wrapped, name=name,
        out_shape=(pltpu.SemaphoreType.DMA(()),) * k + _hbm_shapes(bufs) + (TOKEN_SHAPE,),
        in_specs=[HBM_SPEC] * (2 * n) + [SEM_SPEC] * m + [ANY_SPEC] * na,
        out_specs=(SEM_SPEC,) * k + (HBM_SPEC,) * (2 * n) + (TOKEN_SPEC,),
        input_output_aliases={i: k + i for i in range(2 * n)},
        scratch_shapes=[pltpu.SemaphoreType.DMA((n,))],
        compiler_params=pltpu.CompilerParams(has_side_effects=DATAFLOW),
    )(*[_hbm(b) for b in bufs], *sems_in, *afters)
    return _Split(n, outs, n_sets)


def _wait_slots(land, count, send_sem, recv_sem, me, send=False, recv=False):
    span = land.at[pl.ds(0, count)]
    cp = _remote(span, span, send_sem, recv_sem, me)
    if send:
        cp.wait_send()
    if recv:
        cp.wait_recv()


X_FLIP, Y_FLIP, DIAG_FLIP = 4, 2, 6
BF16_SUBLANES = 16


def _gather_start(name, shards, after):
    n = len(shards)
    my_slot = _slot(_mesh_pos())
    lands = [lax.dynamic_update_slice(lax.empty((N_DEV,) + w.shape, w.dtype), w[None], (my_slot, 0, 0)) for w in shards]

    def body(srcs, dsts, _, sems, local_sems):
        me = _mesh_pos()
        for a in range(n):
            for k in (1, X_FLIP, Y_FLIP):
                _remote(srcs[a], dsts[a].at[_slot(me)], sems[a], sems[n + a], _flip(me, k)).start()

    return _split_call(name, body, list(shards) + lands, [], 1, after)


def _gather_relay(name, started, after):
    n = started.n

    def body(srcs, dsts, sems_a, sems_out, local_sems):
        me = _mesh_pos()
        sibling = _flip(me, 1)
        pass_on, relay = sems_out[:2 * n], sems_out[2 * n:]
        for a in range(n):
            _wait_slots(dsts[a], 3, sems_a[a], sems_a[n + a], me, recv=True)
            from_x = dsts[a].at[_slot(_flip(me, X_FLIP))]
            from_y = dsts[a].at[_slot(_flip(me, Y_FLIP))]
            rows = srcs[a].shape[0]
            if rows % (2 * BF16_SUBLANES) == 0:
                top, bottom = pl.ds(0, rows // 2), pl.ds(rows // 2, rows // 2)
                _remote(from_y.at[top], from_y.at[top], relay[a], relay[n + a], _flip(me, X_FLIP)).start()
                _remote(from_x.at[bottom], from_x.at[bottom], relay[a], relay[n + a], _flip(me, Y_FLIP)).start()
            else:
                _remote(from_y, from_y, relay[a], relay[n + a], _flip(me, X_FLIP)).start()
            for block in (from_x, from_y):
                _remote(block, block, pass_on[a], pass_on[n + a], sibling).start()
        for a in range(n):
            _wait_slots(dsts[a], 3, sems_a[a], sems_a[n + a], me, send=True)

    return _split_call(name, body, started.bufs, started.sems, 2, after)


def _gather_relay_diagonal(name, relayed, after):
    n = relayed.n

    def body(srcs, dsts, relay, pass_on, local_sems):
        me = _mesh_pos()
        for a in range(n):
            _wait_slots(dsts[a], 1, relay[a], relay[n + a], me, recv=True)
            block = dsts[a].at[_slot(_flip(me, DIAG_FLIP))]
            _remote(block, block, pass_on[a], pass_on[n + a], _flip(me, 1)).start()
        for a in range(n):
            _wait_slots(dsts[a], 1, relay[a], relay[n + a], me, send=True)

    return _split_call(name, body, relayed.bufs, relayed.sem_set(1), 1, after)


def _gather_finish(name, relayed, diagonal, after):
    n = relayed.n

    def body(srcs, dsts, sems, _, local_sems):
        me = _mesh_pos()
        first, second = sems[:2 * n], sems[2 * n:]
        for a in range(n):
            _wait_slots(dsts[a], 2, first[a], first[n + a], me, send=True, recv=True)
            _wait_slots(dsts[a], 1, second[a], second[n + a], me, send=True, recv=True)

    return _split_call(name, body, diagonal.bufs, relayed.sem_set(0) + diagonal.sems, 0, after).bufs[n:]


def _exchange_start(name, blocked, after):
    n = len(blocked)
    my_slot = _slot(_mesh_pos())
    rows = [w.shape[-2] // (N_DEV if w.ndim == 2 else 1) for w in blocked]

    def block(ref, a, slot):
        if len(ref.shape) == 3:
            return ref.at[slot]
        return ref.at[pl.ds(pl.multiple_of(slot * rows[a], 16), rows[a])]

    lands = []
    for w, r in zip(blocked, rows):
        mine = lax.dynamic_slice_in_dim(w, my_slot, 1, 0) if w.ndim == 3 else lax.dynamic_slice_in_dim(w, my_slot * r, r, 0)[None]
        lands.append(lax.dynamic_update_slice(lax.empty((N_DEV, r, w.shape[-1]), w.dtype), mine, (my_slot, 0, 0)))

    def body(srcs, dsts, _, sems, local_sems):
        me = _mesh_pos()
        for a in range(n):
            for k in range(1, N_DEV):
                peer = _flip(me, k)
                _remote(block(srcs[a], a, _slot(peer)), dsts[a].at[_slot(me)], sems[a], sems[n + a], peer).start()

    return _split_call(name, body, list(blocked) + lands, [], True, after)


def _exchange_finish(name, started, after):
    n = started.n

    def body(srcs, dsts, sems, _, local_sems):
        me = _mesh_pos()
        for a in range(n):
            _wait_slots(dsts[a], N_DEV - 1, sems[a], sems[n + a], me, send=True, recv=True)

    return _split_call(name, body, started.bufs, started.sems, False, after).bufs[n:]


def _broadcast_start(name, arrays, after):
    n = len(arrays)
    my_slot = _slot(_mesh_pos())
    lands = [lax.dynamic_update_slice(lax.empty((N_DEV,) + w.shape, w.dtype), w[None], (my_slot, 0, 0)) for w in arrays]

    def body(srcs, dsts, _, sems, local_sems):
        me = _mesh_pos()
        for a in range(n):
            for k in range(1, N_DEV):
                _remote(srcs[a], dsts[a].at[_slot(me)], sems[a], sems[n + a], _flip(me, k)).start()

    return _split_call(name, body, list(arrays) + lands, [], True, after)


def _adamw_math(w, g, m, v):
    m = ADAM_B1 * m + (1.0 - ADAM_B1) * g
    v = ADAM_B2 * v + (1.0 - ADAM_B2) * (g * g)
    m_hat = m / (1.0 - ADAM_B1 ** ADAM_STEP)
    v_hat = v / (1.0 - ADAM_B2 ** ADAM_STEP)
    delta = -ADAM_LR * (m_hat / (jnp.sqrt(v_hat) + ADAM_EPS) + ADAM_WD * w)
    return delta, m, v


def _adamw(name, w, m, v, parts, layer, prev=None):
    _, r, c = w.shape
    tr = _pick(r, tuple(t for t in (256, 128, 64, 32, 16) if t * c <= ADAMW_TILE_ELEMS))
    n_prev = 0 if prev is None else 4

    def body(*refs):
        w_ref, m_ref, v_ref, p_ref = refs[:4]
        g_ref, d_ref, nm_ref, nv_ref = refs[4 + n_prev:]
        g = p_ref[0].astype(F32)
        for src in range(1, N_DEV):
            g = g + p_ref[src].astype(F32)
        delta, nm, nv = _adamw_math(w_ref[...], g, m_ref[...], v_ref[...])
        g_ref[...] = g
        d_ref[...] = delta
        nm_ref[...] = nm
        nv_ref[...] = nv

    wspec = pl.BlockSpec((None, tr, c), lambda i: (layer, i, 0))
    pspec = pl.BlockSpec((N_DEV, tr, c), lambda i: (0, i, 0))
    shp = jax.ShapeDtypeStruct(w.shape, F32)
    return pl.pallas_call(
        body, name=name, grid=(r // tr,), in_specs=[wspec] * 3 + [pspec] + [ANY_SPEC] * n_prev,
        out_specs=(wspec,) * 4, out_shape=(shp,) * 4, input_output_aliases={4 + j: j for j in range(n_prev)},
        compiler_params=_cparams(("parallel",)),
    )(w, m, v, parts, *([] if prev is None else prev))


PACK_TILE = 8 * 128


def _pack(arrays):
    flat = []
    for a in arrays:
        v = a.reshape(-1)
        pad = (-v.shape[0]) % PACK_TILE
        flat.append(jnp.pad(v, (0, pad)) if pad else v)
    return jnp.concatenate(flat).reshape(-1, 128)


def _unpack(buf, like):
    flat = buf.reshape(-1)
    out, off = [], 0
    for a in like:
        n = 1
        for dim in a.shape:
            n *= dim
        out.append(flat[off:off + n].reshape(a.shape))
        off += n + (-n) % PACK_TILE
    return out


def _sum_slots(name, gathered):
    _, r, c = gathered.shape

    def body(x_ref, o_ref):
        acc = x_ref[0].astype(F32)
        for src in range(1, N_DEV):
            acc = acc + x_ref[src].astype(F32)
        o_ref[...] = acc

    return pl.pallas_call(body, name=name, out_shape=jax.ShapeDtypeStruct((r, c), F32))(gathered)


def _adamw_small(w, g, m, v):
    shp = jax.ShapeDtypeStruct(w.shape, F32)

    def body(w_ref, g_ref, m_ref, v_ref, d_ref, nm_ref, nv_ref):
        delta, nm, nv = _adamw_math(w_ref[...], g_ref[...], m_ref[...], v_ref[...])
        d_ref[...] = delta
        nm_ref[...] = nm
        nv_ref[...] = nv

    return pl.pallas_call(body, name="adamw_small", out_shape=(shp,) * 3)(w, g, m, v)


def kernel(x, attn_norm_g, w_in, q_norm_g, k_norm_g, sgu_norm_g, sgu_w, sgu_b, out_norm_a_g, out_norm_b_g, w_out, ffn_norm_g, w_up, conv_w, conv_b, w_down, loss_target, m_attn_norm_g, m_w_in, m_q_norm_g, m_k_norm_g, m_sgu_norm_g, m_sgu_w, m_sgu_b, m_out_norm_a_g, m_out_norm_b_g, m_w_out, m_ffn_norm_g, m_w_up, m_conv_w, m_conv_b, m_w_down, v_attn_norm_g, v_w_in, v_q_norm_g, v_k_norm_g, v_sgu_norm_g, v_sgu_w, v_sgu_b, v_out_norm_a_g, v_out_norm_b_g, v_w_out, v_ffn_norm_g, v_w_up, v_conv_w, v_conv_b, v_w_down):
    depth = w_in.shape[0]
    s, d = x.shape[1], x.shape[2]
    n_heads = (d // 2) // HEAD_DIM
    sgu_col0 = 3 * n_heads
    f2 = w_up.shape[2] * N_DEV
    ff = f2 // 2
    my_slot = 4 * lax.axis_index("x") + 2 * lax.axis_index("y") + lax.axis_index("c")

    wb = [(w_in[l].astype(BF16), w_out[l].astype(BF16), w_up[l].astype(BF16), w_down[l].astype(BF16))
          for l in range(depth)]
    groups = {"in0": [wb[0][0]], "out0": [wb[0][1], conv_w.reshape(depth * CONV_WIDTH, -1)], "up0": [wb[0][2]],
              "down0": [wb[0][3]]}
    for l in range(1, depth):
        groups[f"in{l}"] = [wb[l][0], wb[l][1]]
        groups[f"ffn{l}"] = [wb[l][2], wb[l][3]]
    order = list(groups)
    started, relayed = {}, {}

    def start(gname, after):
        started[gname] = _gather_start(f"gather_{gname}_start", groups[gname], after)
        return started[gname].token

    def relay(gname, after):
        relayed[gname] = _gather_relay(f"gather_{gname}_relay", started[gname], after)
        token = relayed[gname].token
        k = order.index(gname)
        nxt = [k + 2] if k + 2 < len(order) - 1 else []
        if k == len(order) - 2:
            nxt = [k + 1]
        for j in nxt:
            token = start(order[j], token)
        return token

    def finish(gname, after):
        diagonal = _gather_relay_diagonal(f"gather_{gname}_diagonal", relayed[gname], after)
        return _gather_finish(f"gather_{gname}_finish", relayed[gname], diagonal, diagonal.token)

    conv_b_all = conv_b.reshape(depth, 2, 1, ff)
    sgu_b_col = sgu_b[..., None]
    token = start(order[1], start(order[0], attn_norm_g))
    token = relay("out0", relay("in0", token))
    win_g = finish("in0", token)[0]

    xs = x[0]
    saved = []
    gathered = []
    for l in range(depth):
        g1 = attn_norm_g[l][None]
        g2 = ffn_norm_g[l][None]
        gq, gk = q_norm_g[l][None], k_norm_g[l][None]
        ga = out_norm_a_g[l][:, None, :]
        gs = sgu_norm_g[l][:, None, :]
        gb = out_norm_b_g[l][:, None, :]
        h1 = _rmsnorm_fwd("attn_norm_fwd", xs, g1)
        p = _mm_nn_blocked("in_proj", h1, win_g, F32)
        att, o, rsum = _attn_fwd(p, gq, gk, ga, n_heads)
        token = relay("up0" if l == 0 else f"ffn{l}", att)
        sg = _sgu_fwd(p, gs, sgu_w[l], sgu_b_col[l], gb, sgu_col0, token)
        mix = jnp.concatenate([att, sg], axis=-1)
        if l == 0:
            wout_g, cw = finish("out0", mix)
            cw = jnp.transpose(cw.reshape(N_DEV, depth, CONV_WIDTH, -1), (1, 2, 0, 3)).reshape(depth, CONV_WIDTH, 2, ff)
            conv_w_all = jnp.transpose(cw, (0, 2, 1, 3))
        x1 = _mm_nn_res("out_proj", mix, wout_g.reshape(d, d), xs)
        h2 = _rmsnorm_fwd("ffn_norm_fwd", x1, g2)
        if l == 0:
            wup_g = finish("up0", h2)[0]
            token = relay("down0", wup_g)
            up, act = _up_conv_gate_fwd(h2, wup_g, conv_w_all[l], conv_b_all[l], after=token)
            wdown_g = finish("down0", up)[0]
        else:
            wup_g, wdown_g = finish(f"ffn{l}", h2)
            up, act = _up_conv_gate_fwd(h2, wup_g, conv_w_all[l], conv_b_all[l])
        saved.append((xs, h1, p, o, rsum, mix, x1, h2, up, act))
        gathered.append((win_g, wout_g, wup_g, wdown_g))
        if l + 1 < depth:
            token = relay(f"in{l + 1}", act)
            x2 = _mm_nn_res("down_proj", act, wdown_g.reshape(ff, d), x1, after=token)
            win_g, wout_g = finish(f"in{l + 1}", x2)
        else:
            x2 = _mm_nn_res("down_proj", act, wdown_g.reshape(ff, d), x1)
        xs = x2

    dx, dxb, loss_vec = _loss_head(xs, loss_target[0])
    loss = lax.psum(loss_vec[0, 0], MESH_AXES)

    exchanges = []
    small = [None] * depth
    small_names = ["attn_norm_g", "q_norm_g", "k_norm_g", "sgu_norm_g", "sgu_w", "sgu_b", "out_norm_a_g",
                   "out_norm_b_g", "ffn_norm_g", "conv_b", "conv_w"]
    for l in reversed(range(depth)):
        xs0, h1, p, o, rsum, mix, x1, h2, up, act = saved[l]
        win_g, wout_g, wup_g, wdown_g = gathered[l]
        wout_full = wout_g.reshape(d, d)
        wdown_full = wdown_g.reshape(ff, d)
        g1 = attn_norm_g[l][None]
        g2 = ffn_norm_g[l][None]
        gq, gk = q_norm_g[l][None], k_norm_g[l][None]
        ga = out_norm_a_g[l][:, None, :]
        gs = sgu_norm_g[l][:, None, :]
        gb = out_norm_b_g[l][:, None, :]
        d_wdown = _mm_tn_plain("down_proj_dw", act, dxb)
        exchanges.append((l, "down", ("w_down",), _exchange_start(f"grad_down{l}_start", [d_wdown], dx)))
        dup, d_cw, d_cb = _down_dx_conv_gate_bwd(up, conv_w_all[l], conv_b_all[l], dxb, wdown_full,
                                                 exchanges[-1][3].token)
        d_wup = _mm_tn_blocked("up_proj_dw", h2, dup, N_DEV, halves=True)
        exchanges.append((l, "up", ("w_up",), _exchange_start(f"grad_up{l}_start", [d_wup], d_cb)))
        dh2 = _mm_nt_blocked("up_proj_dx", dup, wup_g, halves=True, after=exchanges[-1][3].token)
        dx, dxb, d_g2 = _rmsnorm_bwd("ffn_norm_bwd", dh2, x1, g2, dx)
        d_wout = _mm_tn_plain("out_proj_dw", mix, dxb)
        dmix = _mm_nt_plain("out_proj_dx", dxb, wout_full)
        dq, dk, dv, d_gq, d_gk, d_ga = _attn_bwd(p, gq, gk, ga, o, rsum, dmix, n_heads)
        du, dvs, d_gs, d_sw, d_sb, d_gb = _sgu_bwd(p, gs, sgu_w[l], sgu_b_col[l], gb, dmix, sgu_col0, n_heads)
        dp = jnp.concatenate([dq, dk, dv, du, dvs], axis=-1)
        d_win = _mm_tn_blocked("in_proj_dw", h1, dp, N_DEV)
        exchanges.append((l, "mix", ("w_in", "w_out"), _exchange_start(f"grad_mix{l}_start", [d_win, d_wout], d_gq)))
        dh1 = _mm_nt_blocked("in_proj_dx", dp, win_g, after=exchanges[-1][3].token)
        dx, dxb, d_g1 = _rmsnorm_bwd("attn_norm_bwd", dh1, xs0, g1, dx)
        small[l] = dict(attn_norm_g=d_g1[0], q_norm_g=d_gq[0], k_norm_g=d_gk[0], sgu_norm_g=d_gs[:, 0], sgu_w=d_sw,
                        sgu_b=d_sb[..., 0], out_norm_a_g=d_ga[:, 0], out_norm_b_g=d_gb[:, 0], ffn_norm_g=d_g2[0],
                        conv_w=jnp.transpose(d_cw, (1, 0, 2)).reshape(CONV_WIDTH, f2), conv_b=d_cb.reshape(f2))
    grad_x = dx[None]

    f32_names = [n for n in small_names if n != "sgu_w"]
    small_g = [jnp.stack([small[l][n] for l in range(depth)]) for n in f32_names]
    sgu_w_g = jnp.stack([small[l]["sgu_w"] for l in range(depth)])
    small_sent = _broadcast_start("grad_small_start", [_pack(small_g), sgu_w_g.reshape(-1, TILE).astype(BF16)], dx)

    res = {}
    big = dict(w_in=(w_in, m_w_in, v_w_in), w_out=(w_out, m_w_out, v_w_out), w_up=(w_up, m_w_up, v_w_up),
               w_down=(w_down, m_w_down, v_w_down))
    after = [small_sent.token]
    for l, stage, names, ex in exchanges:
        landed = _exchange_finish(f"grad_{stage}{l}_finish", ex, after)
        after = []
        for name, parts in zip(names, landed):
            w, m, v = big[name]
            res[name] = _adamw(f"adamw_{name}", w, m, v, parts, l, res.get(name))
            after.append(res[name][0])
    small_all, sgu_w_all = _exchange_finish("grad_small_finish", small_sent, after)
    small_sum = _unpack(_sum_slots("small_grad_sum", small_all), small_g)
    g_small = dict(zip(f32_names, small_sum))
    g_small["sgu_w"] = _sum_slots("sgu_w_grad_sum", sgu_w_all).reshape(sgu_w.shape)
    cwn = conv_w.shape[2]
    g_small["conv_w"] = lax.dynamic_slice_in_dim(g_small["conv_w"], my_slot * cwn, cwn, axis=2)
    small_w = dict(attn_norm_g=(attn_norm_g, m_attn_norm_g, v_attn_norm_g), q_norm_g=(q_norm_g, m_q_norm_g, v_q_norm_g),
                   k_norm_g=(k_norm_g, m_k_norm_g, v_k_norm_g), sgu_norm_g=(sgu_norm_g, m_sgu_norm_g, v_sgu_norm_g),
                   sgu_w=(sgu_w, m_sgu_w, v_sgu_w), sgu_b=(sgu_b, m_sgu_b, v_sgu_b),
                   out_norm_a_g=(out_norm_a_g, m_out_norm_a_g, v_out_norm_a_g),
                   out_norm_b_g=(out_norm_b_g, m_out_norm_b_g, v_out_norm_b_g),
                   ffn_norm_g=(ffn_norm_g, m_ffn_norm_g, v_ffn_norm_g), conv_b=(conv_b, m_conv_b, v_conv_b),
                   conv_w=(conv_w, m_conv_w, v_conv_w))
    like = [small_w[n][0] for n in small_names]
    pw = _pack([small_w[n][0] for n in small_names])
    pm = _pack([small_w[n][1] for n in small_names])
    pv = _pack([small_w[n][2] for n in small_names])
    pg = _pack([g_small[n].reshape(small_w[n][0].shape) for n in small_names])
    pd, pnm, pnv = _adamw_small(pw, pg, pm, pv)
    for n, dlt, nm, nv in zip(small_names, _unpack(pd, like), _unpack(pnm, like), _unpack(pnv, like)):
        res[n] = (g_small[n].reshape(small_w[n][0].shape), dlt, nm, nv)

    order = ["attn_norm_g", "w_in", "q_norm_g", "k_norm_g", "sgu_norm_g", "sgu_w", "sgu_b", "out_norm_a_g",
             "out_norm_b_g", "w_out", "ffn_norm_g", "w_up", "conv_w", "conv_b", "w_down"]
    outs = [loss, grad_x]
    for field in range(4):
        outs += [res[n][field] for n in order]
    return tuple(outs)
```

```python
import functools

import jax
import jax.numpy as jnp
from jax import lax
from jax.experimental import pallas as pl
from jax.experimental.pallas import tpu as pltpu

F32 = jnp.float32
BF16 = jnp.bfloat16
EPS = 1e-6
HEAD_DIM = 128
TILE = 128
ATTN_VMEM_MB = 58
ATTN_HEADS_PER_STEP = 4
N_GROUPS = 8
CONV_WIDTH = 3
N_DEV = 8
MESH_AXES = ("x", "y", "c")
MIB = 1024 * 1024

ADAM_LR = 0.001
ADAM_B1 = 0.9
ADAM_B2 = 0.999
ADAM_EPS = 1e-08
ADAM_WD = 0.01
ADAM_STEP = 10
ADAMW_TILE_ELEMS = 160 * 1024

NT_DIMS = (((1,), (1,)), ((), ()))
NN_DIMS = (((1,), (0,)), ((), ()))
TN_DIMS = (((0,), (0,)), ((), ()))


def _cparams(sem, vmem_mb=48):
    return pltpu.CompilerParams(dimension_semantics=sem, vmem_limit_bytes=vmem_mb * MIB)


def _pick(n, cands):
    for c in cands:
        if n % c == 0:
            return c
    return n


def _mm(name, grid, ins, in_specs, out_shape, out_spec, dims, has_res=False, parts=None, vmem_mb=56, after=None):
    n_in = 2 + has_res + (after is not None)
    if after is not None:
        ins = tuple(ins) + (after,)
        in_specs = list(in_specs) + [pl.BlockSpec(after.shape, lambda *_: (0, 0))]

    def body(*refs):
        a_ref, b_ref = refs[:2]
        o_ref = refs[n_in]
        if parts is None:
            acc = lax.dot_general(a_ref[...], b_ref[...], dims, preferred_element_type=F32)
        else:
            acc = None
            for part in parts:
                a, b = part(a_ref, b_ref)
                prod = lax.dot_general(a, b, dims, preferred_element_type=F32)
                acc = prod if acc is None else acc + prod
        if has_res:
            acc = acc + refs[2][...]
        o_ref[...] = acc.astype(o_ref.dtype)

    return pl.pallas_call(
        body, name=name, grid=grid, in_specs=in_specs, out_specs=out_spec, out_shape=out_shape,
        compiler_params=_cparams(("parallel",) * len(grid), vmem_mb),
    )(*ins)


def _rows_for(m, row_bytes, budget):
    return _pick(m, tuple(t for t in (2048, 1024, 512, 256, 128) if t * row_bytes <= budget))


def _mm_nn_blocked(name, a, wb, out_dtype, halves=False, after=None):
    m, k = a.shape
    nb, _, bn = wb.shape
    tm = _rows_for(m, bn * jnp.dtype(out_dtype).itemsize, 6 * MIB)
    a_spec = pl.BlockSpec((tm, k), lambda j, i: (i, 0))
    b_spec = pl.BlockSpec((None, k, bn), lambda j, i: (j, 0, 0))
    if halves:
        hb = nb // 2
        out_shape = jax.ShapeDtypeStruct((2, m, hb * bn), out_dtype)
        o_spec = pl.BlockSpec((None, tm, bn), lambda j, i: (j // hb, i, j % hb))
    else:
        out_shape = jax.ShapeDtypeStruct((m, nb * bn), out_dtype)
        o_spec = pl.BlockSpec((tm, bn), lambda j, i: (i, j))
    return _mm(name, (nb, m // tm), (a, wb), [a_spec, b_spec], out_shape, o_spec, NN_DIMS, after=after)


def _mm_nn_res(name, a, w, res, after=None):
    m, k = a.shape
    n = w.shape[1]
    tm = _pick(m, (512, 256, 128))
    tn = _rows_for(n, k * 2, 12 * MIB)
    a_spec = pl.BlockSpec((tm, k), lambda j, i: (i, 0))
    b_spec = pl.BlockSpec((k, tn), lambda j, i: (0, j))
    r_spec = pl.BlockSpec((tm, tn), lambda j, i: (i, j))
    o_spec = pl.BlockSpec((tm, tn), lambda j, i: (i, j))
    return _mm(name, (n // tn, m // tm), (a, w, res), [a_spec, b_spec, r_spec], jax.ShapeDtypeStruct((m, n), F32),
               o_spec, NN_DIMS, has_res=True, after=after)


def _mm_nt_blocked(name, dy, wb, halves=False, after=None):
    nb, n, bn = wb.shape
    m = dy.shape[-2]
    tm = _pick(m, (512, 256, 128))
    tn = _rows_for(n, nb * bn * 2, 12 * MIB)
    if halves:
        hb = nb // 2
        a_spec = pl.BlockSpec((2, tm, hb * bn), lambda j, i: (0, i, 0))
        a_part = lambda kk: (lambda a_ref: a_ref[kk // hb, :, (kk % hb) * bn:(kk % hb + 1) * bn])
    else:
        a_spec = pl.BlockSpec((tm, nb * bn), lambda j, i: (i, 0))
        a_part = lambda kk: (lambda a_ref: a_ref[:, kk * bn:(kk + 1) * bn])
    parts = [(lambda a_ref, b_ref, kk=kk, sel=a_part(kk): (sel(a_ref), b_ref[kk])) for kk in range(nb)]
    b_spec = pl.BlockSpec((nb, tn, bn), lambda j, i: (0, j, 0))
    o_spec = pl.BlockSpec((tm, tn), lambda j, i: (i, j))
    return _mm(name, (n // tn, m // tm), (dy, wb), [a_spec, b_spec], jax.ShapeDtypeStruct((m, n), F32), o_spec,
               NT_DIMS, parts=parts, after=after)


def _mm_nt_plain(name, dy, w, out_dtype=F32, after=None):
    m, k = dy.shape
    n = w.shape[0]
    tm = _rows_for(m, k * 2, 8 * MIB)
    tn = _pick(n, (512, 256, 128))
    a_spec = pl.BlockSpec((tm, k), lambda j, i: (i, 0))
    b_spec = pl.BlockSpec((tn, k), lambda j, i: (j, 0))
    o_spec = pl.BlockSpec((tm, tn), lambda j, i: (i, j))
    return _mm(name, (n // tn, m // tm), (dy, w), [a_spec, b_spec], jax.ShapeDtypeStruct((m, n), out_dtype), o_spec,
               NT_DIMS, after=after)


def _mm_tn_blocked(name, a, dy, nb, halves=False):
    s, k1 = a.shape
    bn = (dy.shape[-1] * (2 if halves else 1)) // nb
    tm = _rows_for(k1, bn * 2, 6 * MIB)
    a_spec = pl.BlockSpec((s, tm), lambda j, i: (0, i))
    if halves:
        hb = nb // 2
        b_spec = pl.BlockSpec((None, s, bn), lambda j, i: (j // hb, 0, j % hb))
    else:
        b_spec = pl.BlockSpec((s, bn), lambda j, i: (0, j))
    o_spec = pl.BlockSpec((None, tm, bn), lambda j, i: (j, i, 0))
    return _mm(name, (nb, k1 // tm), (a, dy), [a_spec, b_spec], jax.ShapeDtypeStruct((nb, k1, bn), BF16), o_spec,
               TN_DIMS)


def _mm_tn_plain(name, a, dy):
    s, k1 = a.shape
    n = dy.shape[1]
    tm = _pick(k1, (512, 256, 128))
    tn = _rows_for(n, s * 2, 8 * MIB)
    a_spec = pl.BlockSpec((s, tm), lambda i, j: (0, i))
    b_spec = pl.BlockSpec((s, tn), lambda i, j: (0, j))
    o_spec = pl.BlockSpec((tm, tn), lambda i, j: (i, j))
    return _mm(name, (k1 // tm, n // tn), (a, dy), [a_spec, b_spec], jax.ShapeDtypeStruct((k1, n), BF16), o_spec,
               TN_DIMS)


def _rstd(x):
    return lax.rsqrt(jnp.mean(x * x, axis=-1, keepdims=True) + EPS)


def _norm_bwd(dy, xhat, r, g):
    dxhat = dy * g
    return r * (dxhat - xhat * jnp.mean(dxhat * xhat, axis=-1, keepdims=True))


def _rmsnorm_fwd(name, x, g):
    s, d = x.shape
    tr = _pick(s, (256, 128))

    def body(x_ref, g_ref, h_ref):
        xv = x_ref[...]
        h_ref[...] = (xv * _rstd(xv) * g_ref[...]).astype(BF16)

    return pl.pallas_call(
        body, name=name, grid=(s // tr,),
        in_specs=[pl.BlockSpec((tr, d), lambda i: (i, 0)), pl.BlockSpec((1, d), lambda i: (0, 0))],
        out_specs=pl.BlockSpec((tr, d), lambda i: (i, 0)),
        out_shape=jax.ShapeDtypeStruct((s, d), BF16), compiler_params=_cparams(("parallel",)),
    )(x, g)


def _rmsnorm_bwd(name, dh, x, g, dres):
    s, d = x.shape
    tr = _pick(s, (256, 128))

    def body(dh_ref, x_ref, g_ref, dres_ref, dx_ref, dxb_ref, dg_ref):
        xv = x_ref[...]
        r = _rstd(xv)
        xhat = xv * r
        dhv = dh_ref[...]
        dx = dres_ref[...] + _norm_bwd(dhv, xhat, r, g_ref[...])
        dx_ref[...] = dx
        dxb_ref[...] = dx.astype(BF16)
        part = jnp.sum(dhv * xhat, axis=0, keepdims=True)

        @pl.when(pl.program_id(0) == 0)
        def _():
            dg_ref[...] = part

        @pl.when(pl.program_id(0) > 0)
        def _():
            dg_ref[...] += part

    row = pl.BlockSpec((tr, d), lambda i: (i, 0))
    vec = pl.BlockSpec((1, d), lambda i: (0, 0))
    return pl.pallas_call(
        body, name=name, grid=(s // tr,), in_specs=[row, row, vec, row], out_specs=(row, row, vec),
        out_shape=(jax.ShapeDtypeStruct((s, d), F32), jax.ShapeDtypeStruct((s, d), BF16),
                   jax.ShapeDtypeStruct((1, d), F32)),
        compiler_params=_cparams(("arbitrary",)),
    )(dh, x, g, dres)


def _loss_head(y, target):
    s, d = y.shape
    tr = _pick(s, (256, 128))

    def body(y_ref, t_ref, dy_ref, dyb_ref, loss_ref):
        err = y_ref[...] - t_ref[...]
        dy = err * (1.0 / d)
        dy_ref[...] = dy
        dyb_ref[...] = dy.astype(BF16)
        part = 0.5 * jnp.sum(jnp.mean(err * err, axis=-1, keepdims=True), axis=0, keepdims=True)
        part = jnp.broadcast_to(part, (1, 128))

        @pl.when(pl.program_id(0) == 0)
        def _():
            loss_ref[...] = part

        @pl.when(pl.program_id(0) > 0)
        def _():
            loss_ref[...] += part

    row = pl.BlockSpec((tr, d), lambda i: (i, 0))
    return pl.pallas_call(
        body, name="loss_head", grid=(s // tr,), in_specs=[row, row],
        out_specs=(row, row, pl.BlockSpec((1, 128), lambda i: (0, 0))),
        out_shape=(jax.ShapeDtypeStruct((s, d), F32), jax.ShapeDtypeStruct((s, d), BF16),
                   jax.ShapeDtypeStruct((1, 128), F32)),
        compiler_params=_cparams(("arbitrary",)),
    )(y, target)


def _split_dot(x, tri):
    hi = x.astype(BF16)
    lo = (x - hi.astype(F32)).astype(BF16)
    return (jnp.dot(hi, tri, preferred_element_type=F32) + jnp.dot(lo, tri, preferred_element_type=F32))


def _tile_iotas():
    row = lax.broadcasted_iota(jnp.int32, (TILE, TILE), 0)
    col = lax.broadcasted_iota(jnp.int32, (TILE, TILE), 1)
    return row, col


def _sb_logits(qi, kb, mask):
    z = lax.dot_general(qi, kb, NT_DIMS, preferred_element_type=F32) * (HEAD_DIM ** -0.5)
    sp = jnp.log(1.0 + jnp.exp(-jnp.abs(z)))
    lb = jnp.minimum(z, 0.0) - sp
    l1m = -jnp.maximum(z, 0.0) - sp
    if mask is not None:
        l1m = jnp.where(mask, l1m, 0.0)
    return lb, l1m


def _attn_fwd(p, gq, gk, ga, n_heads):
    s = p.shape[0]
    nq = s // TILE

    hp = ATTN_HEADS_PER_STEP
    wd = hp * HEAD_DIM

    def body(q_ref, k_ref, v_ref, gq_ref, gk_ref, ga_ref, att_ref, o_ref, r_ref, qn_s, kn_s, vb_s):
        heads = [slice(hh * HEAD_DIM, (hh + 1) * HEAD_DIM) for hh in range(hp)]
        for hd in heads:
            qv = q_ref[:, hd]
            qn_s[:, hd] = (qv * _rstd(qv) * gq_ref[...]).astype(BF16)
            kv = k_ref[:, hd]
            kn_s[:, hd] = (kv * _rstd(kv) * gk_ref[...]).astype(BF16)
        vb_s[...] = v_ref[...].astype(BF16)
        row, col = _tile_iotas()
        causal = col < row
        upper_ones = jnp.concatenate([(row > col).astype(BF16), jnp.ones((TILE, TILE), BF16)], axis=1)

        def tiles(rows, key_blocks, states):
            chains = [(hi, hd, keys, mask) for hi, hd in enumerate(heads) for keys, mask in key_blocks]
            logits = [_sb_logits(qn_s[rows, hd], kn_s[keys, hd], mask) for _, hd, keys, mask in chains]
            sums = [_split_dot(l1m, upper_ones) for _, l1m in logits]
            carry = [c for _, c in states]
            probs = []
            for (hi, _, _, mask), (lb, _), sm in zip(chains, logits, sums):
                a = jnp.exp(lb + sm[:, :TILE] + carry[hi])
                carry[hi] = carry[hi] + sm[:, TILE:]
                probs.append((a if mask is None else jnp.where(mask, a, 0.0)).astype(BF16))
            outs = [jnp.dot(a, vb_s[keys, hd], preferred_element_type=F32) for a, (_, hd, keys, _) in zip(probs, chains)]
            acc = [o_acc for o_acc, _ in states]
            for (hi, _, _, _), o in zip(chains, outs):
                acc[hi] = acc[hi] + o
            return tuple(zip(acc, carry))

        def key_block(b):
            return pl.ds(pl.multiple_of(b * TILE, TILE), TILE), None

        def qblock(i, _):
            rows = pl.ds(pl.multiple_of(i * TILE, TILE), TILE)
            zero = jnp.zeros((TILE, HEAD_DIM), F32)
            states = tuple((zero, zero) for _ in heads)
            states = lax.cond(i % 2 == 1, lambda st: tiles(rows, [(rows, causal), key_block(i - 1)], st),
                              lambda st: tiles(rows, [(rows, causal)], st), states)
            top = i - i % 2

            def kblocks(jj, states):
                return tiles(rows, [key_block(top - 1 - 2 * jj), key_block(top - 2 - 2 * jj)], states)

            states = lax.fori_loop(0, i // 2, kblocks, states)
            for hh, (hd, (o_acc, c)) in enumerate(zip(heads, states)):
                o_ref[rows, hd] = o_acc
                r_ref[rows, hd] = c
                att_ref[rows, hd] = (o_acc * _rstd(o_acc) * ga_ref[hh]).astype(BF16)
            return 0

        lax.fori_loop(0, nq, qblock, 0)

    col_blk = lambda off: pl.BlockSpec((s, wd), lambda h: (0, off + h))
    vec = pl.BlockSpec((1, HEAD_DIM), lambda h: (0, 0))
    hvec = pl.BlockSpec((hp, 1, HEAD_DIM), lambda h: (h, 0, 0))
    out = pl.BlockSpec((s, wd), lambda h: (0, h))
    w = n_heads * HEAD_DIM
    steps = n_heads // hp
    return pl.pallas_call(
        body, name="attn_fwd", grid=(steps,),
        in_specs=[col_blk(0), col_blk(steps), col_blk(2 * steps), vec, vec, hvec],
        out_specs=(out, out, out),
        out_shape=(jax.ShapeDtypeStruct((s, w), BF16), jax.ShapeDtypeStruct((s, w), F32),
                   jax.ShapeDtypeStruct((s, w), F32)),
        scratch_shapes=[pltpu.VMEM((s, wd), BF16)] * 3,
        compiler_params=_cparams(("parallel",), ATTN_VMEM_MB),
    )(p, p, p, gq, gk, ga)


def _attn_bwd(p, gq, gk, ga, o, rsum, dmix, n_heads):
    s = p.shape[0]
    nq = s // TILE

    hp = ATTN_HEADS_PER_STEP
    wd = hp * HEAD_DIM
    scale = HEAD_DIM ** -0.5

    def body(q_ref, k_ref, v_ref, gq_ref, gk_ref, ga_ref, o_ref, r_ref, dm_ref,
             dq_ref, dk_ref, dv_ref, dgq_ref, dgk_ref, dga_ref,
             qn_s, kn_s, vb_s, do_s, dqn_s, dkn_s, dv_s):
        step = pl.program_id(0)
        gqv, gkv = gq_ref[...], gk_ref[...]
        heads = [slice(hh * HEAD_DIM, (hh + 1) * HEAD_DIM) for hh in range(hp)]
        for hh, hd in enumerate(heads):
            qv = q_ref[:, hd]
            qn_s[:, hd] = (qv * _rstd(qv) * gqv).astype(BF16)
            kv = k_ref[:, hd]
            kn_s[:, hd] = (kv * _rstd(kv) * gkv).astype(BF16)
            ov = o_ref[:, hd]
            ro = _rstd(ov)
            ohat = ov * ro
            dm = dm_ref[:, hd]
            dga_ref[hh] = jnp.sum(dm * ohat, axis=0, keepdims=True)
            do_s[:, hd] = _norm_bwd(dm, ohat, ro, ga_ref[hh]).astype(BF16)
        vb_s[...] = v_ref[...].astype(BF16)
        dkn_s[...] = jnp.zeros_like(dkn_s)
        dv_s[...] = jnp.zeros_like(dv_s)
        row, col = _tile_iotas()
        causal = col < row
        ones = jnp.ones((TILE, TILE), BF16)
        incl_ones = jnp.concatenate([(row <= col).astype(BF16), ones], axis=1)
        excl_ones = jnp.concatenate([(row < col).astype(BF16), ones], axis=1)

        def tiles(rows, key_blocks, states):
            chains = [(hi, hd, keys, mask) for hi, hd in enumerate(heads) for keys, mask in key_blocks]
            qis = [qn_s[rows, hd] for hd in heads]
            dois = [do_s[rows, hd] for hd in heads]
            logits = [_sb_logits(qis[hi], kn_s[keys, hd], mask) for hi, hd, keys, mask in chains]
            sums = [_split_dot(l1m, incl_ones) for _, l1m in logits]
            das = [lax.dot_general(dois[hi], vb_s[keys, hd], NT_DIMS, preferred_element_type=F32)
                   for hi, hd, keys, _ in chains]
            pfx = [st[1] for st in states]
            probs, dss = [], []
            for (hi, hd, _, mask), (lb, _), sm, da in zip(chains, logits, sums, das):
                a = jnp.exp(lb + (r_ref[rows, hd] - pfx[hi] - sm[:, :TILE]))
                pfx[hi] = pfx[hi] + sm[:, TILE:]
                a = a if mask is None else jnp.where(mask, a, 0.0)
                probs.append(a.astype(BF16))
                dss.append(da * a)
            dsums = [_split_dot(ds, excl_ones) for ds in dss]
            pc = [st[2] for st in states]
            dzs = []
            for (hi, _, _, mask), (lb, _), ds, dsm in zip(chains, logits, dss, dsums):
                dl1m = pc[hi] + dsm[:, :TILE]
                pc[hi] = pc[hi] + dsm[:, TILE:]
                dl1m = dl1m if mask is None else jnp.where(mask, dl1m, 0.0)
                beta = jnp.exp(lb)
                dzs.append(((ds * (1.0 - beta) - dl1m * beta) * scale).astype(BF16))
            dqs = [jnp.dot(dz, kn_s[keys, hd], preferred_element_type=F32) for dz, (_, hd, keys, _) in zip(dzs, chains)]
            for dz, a, (hi, hd, keys, _) in zip(dzs, probs, chains):
                dkn_s[keys, hd] += lax.dot_general(dz, qis[hi], TN_DIMS, preferred_element_type=F32)
                dv_s[keys, hd] += lax.dot_general(a, dois[hi], TN_DIMS, preferred_element_type=F32)
            dq_acc = [st[0] for st in states]
            for (hi, _, _, _), dq in zip(chains, dqs):
                dq_acc[hi] = dq_acc[hi] + dq
            return tuple(zip(dq_acc, pfx, pc))

        def key_block(b):
            return pl.ds(pl.multiple_of(b * TILE, TILE), TILE), None

        def qblock(i, _):
            rows = pl.ds(pl.multiple_of(i * TILE, TILE), TILE)
            zero = jnp.zeros((TILE, HEAD_DIM), F32)

            def kblocks(jj, states):
                return tiles(rows, [key_block(2 * jj), key_block(2 * jj + 1)], states)

            states = lax.fori_loop(0, i // 2, kblocks, tuple((zero, zero, zero) for _ in heads))
            states = lax.cond(i % 2 == 1, lambda st: tiles(rows, [key_block(i - 1), (rows, causal)], st),
                              lambda st: tiles(rows, [(rows, causal)], st), states)
            for hd, (dq_acc, _, _) in zip(heads, states):
                dqn_s[rows, hd] = dq_acc
            return 0

        lax.fori_loop(0, nq, qblock, 0)

        def norm_in_bwd(x_ref, g, dn_s, dx_ref, dg_ref):
            part = jnp.zeros((1, HEAD_DIM), F32)
            for hd in heads:
                xv = x_ref[:, hd]
                r = _rstd(xv)
                xhat = xv * r
                dn = dn_s[:, hd]
                dx_ref[:, hd] = _norm_bwd(dn, xhat, r, g).astype(BF16)
                part = part + jnp.sum(dn * xhat, axis=0, keepdims=True)

            @pl.when(step == 0)
            def _():
                dg_ref[...] = part

            @pl.when(step > 0)
            def _():
                dg_ref[...] += part

        norm_in_bwd(q_ref, gqv, dqn_s, dq_ref, dgq_ref)
        norm_in_bwd(k_ref, gkv, dkn_s, dk_ref, dgk_ref)
        dv_ref[...] = dv_s[...].astype(BF16)

    once = pl.Buffered(1)
    steps = n_heads // hp
    col_blk = lambda off: pl.BlockSpec((s, wd), lambda h: (0, off + h), pipeline_mode=once)
    vec = pl.BlockSpec((1, HEAD_DIM), lambda h: (0, 0))
    hvec = pl.BlockSpec((hp, 1, HEAD_DIM), lambda h: (h, 0, 0))
    blk = pl.BlockSpec((s, wd), lambda h: (0, h), pipeline_mode=once)
    w = n_heads * HEAD_DIM
    big = jax.ShapeDtypeStruct((s, w), BF16)
    return pl.pallas_call(
        body, name="attn_bwd", grid=(steps,),
        in_specs=[col_blk(0), col_blk(steps), col_blk(2 * steps), vec, vec, hvec, blk, blk, blk],
        out_specs=(blk, blk, blk, vec, vec, hvec),
        out_shape=(big, big, big, jax.ShapeDtypeStruct((1, HEAD_DIM), F32), jax.ShapeDtypeStruct((1, HEAD_DIM), F32),
                   jax.ShapeDtypeStruct((n_heads, 1, HEAD_DIM), F32)),
        scratch_shapes=[pltpu.VMEM((s, wd), BF16)] * 4 + [pltpu.VMEM((s, wd), F32)] * 3,
        compiler_params=_cparams(("arbitrary",), ATTN_VMEM_MB),
    )(p, p, p, gq, gk, ga, o, rsum, dmix)


_INV_SQRT2 = 0.7071067811865476
_INV_SQRT_2PI = 0.3989422804014327


def _gelu(x):
    return 0.5 * x * (1.0 + lax.erf(x * _INV_SQRT2))


def _gelu_grad(x):
    return 0.5 * (1.0 + lax.erf(x * _INV_SQRT2)) + x * (_INV_SQRT_2PI * jnp.exp(-0.5 * x * x))


def _sgu_fwd(p, gs, w_s, b_s, gb, col0, after):
    s = p.shape[0]
    n_chunks = s // TILE
    per_trip = _pick(n_chunks, (4, 2, 1))

    def body(u_ref, v_ref, gs_ref, w_ref, b_ref, gb_ref, _, out_ref, vs_s):
        vg = _gelu(v_ref[...])
        vs_s[...] = (vg * _rstd(vg) * gs_ref[...]).astype(BF16)
        row, col = _tile_iotas()
        wt = jnp.where(col <= row, w_ref[...], 0.0).astype(BF16)
        bcol = b_ref[...]
        gbv = gb_ref[...]

        def chunks(c, _):
            rows = [pl.ds(pl.multiple_of((c * per_trip + k) * TILE, TILE), TILE) for k in range(per_trip)]
            mixed = [jnp.dot(wt, vs_s[r, :], preferred_element_type=F32) + bcol for r in rows]
            sgs = [_gelu(u_ref[r, :]) * mx for r, mx in zip(rows, mixed)]
            for r, sg in zip(rows, sgs):
                out_ref[r, :] = (sg * _rstd(sg) * gbv).astype(BF16)
            return 0

        lax.fori_loop(0, n_chunks // per_trip, chunks, 0)

    col_blk = lambda off: pl.BlockSpec((s, HEAD_DIM), lambda g: (0, off + g))
    gvec = pl.BlockSpec((None, 1, HEAD_DIM), lambda g: (g, 0, 0))
    return pl.pallas_call(
        body, name="sgu_fwd", grid=(N_GROUPS,),
        in_specs=[col_blk(col0), col_blk(col0 + N_GROUPS), gvec,
                  pl.BlockSpec((None, TILE, TILE), lambda g: (g, 0, 0)),
                  pl.BlockSpec((None, TILE, 1), lambda g: (g, 0, 0)), gvec,
                  pl.BlockSpec(after.shape, lambda g: (0, 0))],
        out_specs=pl.BlockSpec((s, HEAD_DIM), lambda g: (0, g)),
        out_shape=jax.ShapeDtypeStruct((s, N_GROUPS * HEAD_DIM), BF16),
        scratch_shapes=[pltpu.VMEM((s, HEAD_DIM), BF16)],
        compiler_params=_cparams(("parallel",)),
    )(p, p, gs, w_s, b_s, gb, after)


def _sgu_bwd(p, gs, w_s, b_s, gb, dmix, col0, dm_col0):
    s = p.shape[0]
    n_chunks = s // TILE
    per_trip = _pick(n_chunks, (4, 2, 1))

    def body(u_ref, v_ref, gs_ref, w_ref, b_ref, gb_ref, dm_ref,
             du_ref, dv_ref, dgs_ref, dw_ref, db_ref, dgb_ref, vs_s, dvs_s):
        gsv = gs_ref[...]
        gbv = gb_ref[...]
        vg = _gelu(v_ref[...])
        vs_s[...] = (vg * _rstd(vg) * gsv).astype(BF16)
        row, col = _tile_iotas()
        causal = col <= row
        wt = jnp.where(causal, w_ref[...], 0.0).astype(BF16)
        bcol = b_ref[...]

        def chunks(c, carry):
            dw_acc, db_acc, dgb_acc = carry
            rows = [pl.ds(pl.multiple_of((c * per_trip + k) * TILE, TILE), TILE) for k in range(per_trip)]
            vss = [vs_s[r, :] for r in rows]
            mixed = [jnp.dot(wt, vs, preferred_element_type=F32) + bcol for vs in vss]
            dmbs = []
            for r, mx in zip(rows, mixed):
                u_pre = u_ref[r, :]
                u = _gelu(u_pre)
                sg = u * mx
                rs = _rstd(sg)
                sghat = sg * rs
                dm = dm_ref[r, :]
                dsg = _norm_bwd(dm, sghat, rs, gbv)
                dgb_acc = dgb_acc + jnp.sum(dm * sghat, axis=0, keepdims=True)
                du_ref[r, :] = (dsg * mx * _gelu_grad(u_pre)).astype(BF16)
                dmixed = dsg * u
                db_acc = db_acc + jnp.sum(dmixed, axis=1, keepdims=True)
                dmbs.append(dmixed.astype(BF16))
            for dmb, vs in zip(dmbs, vss):
                dw_acc = dw_acc + lax.dot_general(dmb, vs, NT_DIMS, preferred_element_type=F32)
            for r, dmb in zip(rows, dmbs):
                dvs_s[r, :] = lax.dot_general(wt, dmb, TN_DIMS, preferred_element_type=F32)
            return dw_acc, db_acc, dgb_acc

        dw_acc, db_acc, dgb_acc = lax.fori_loop(
            0, n_chunks // per_trip, chunks,
            (jnp.zeros((TILE, TILE), F32), jnp.zeros((TILE, 1), F32), jnp.zeros((1, HEAD_DIM), F32)))
        dw_ref[...] = jnp.where(causal, dw_acc, 0.0)
        db_ref[...] = db_acc
        dgb_ref[...] = dgb_acc
        v_pre = v_ref[...]
        vg = _gelu(v_pre)
        rv = _rstd(vg)
        vhat = vg * rv
        dvs = dvs_s[...]
        dgs_ref[...] = jnp.sum(dvs * vhat, axis=0, keepdims=True)
        dv_ref[...] = (_norm_bwd(dvs, vhat, rv, gsv) * _gelu_grad(v_pre)).astype(BF16)

    col_blk = lambda off: pl.BlockSpec((s, HEAD_DIM), lambda g: (0, off + g))
    gvec = pl.BlockSpec((None, 1, HEAD_DIM), lambda g: (g, 0, 0))
    wspec = pl.BlockSpec((None, TILE, TILE), lambda g: (g, 0, 0))
    bspec = pl.BlockSpec((None, TILE, 1), lambda g: (g, 0, 0))
    blk = pl.BlockSpec((s, HEAD_DIM), lambda g: (0, g))
    big = jax.ShapeDtypeStruct((s, N_GROUPS * HEAD_DIM), BF16)
    gshape = jax.ShapeDtypeStruct((N_GROUPS, 1, HEAD_DIM), F32)
    return pl.pallas_call(
        body, name="sgu_bwd", grid=(N_GROUPS,),
        in_specs=[col_blk(col0), col_blk(col0 + N_GROUPS), gvec, wspec, bspec, gvec, col_blk(dm_col0)],
        out_specs=(blk, blk, gvec, wspec, bspec, gvec),
        out_shape=(big, big, gshape, jax.ShapeDtypeStruct((N_GROUPS, TILE, TILE), F32),
                   jax.ShapeDtypeStruct((N_GROUPS, TILE, 1), F32), gshape),
        scratch_shapes=[pltpu.VMEM((s, HEAD_DIM), BF16), pltpu.VMEM((s, HEAD_DIM), F32)],
        compiler_params=_cparams(("parallel",)),
    )(p, p, gs, w_s, b_s, gb, dmix)


SUBLANES = 8


def _shift_down(x, n):
    rolled = pltpu.roll(x, n, 0)
    edge = lax.broadcasted_iota(jnp.int32, (SUBLANES, x.shape[1]), 0)
    return jnp.concatenate([jnp.where(edge >= n, rolled[:SUBLANES], 0.0), rolled[SUBLANES:]], axis=0)


def _shift_up(x, n):
    s = x.shape[0]
    rolled = pltpu.roll(x, s - n, 0)
    edge = lax.broadcasted_iota(jnp.int32, (SUBLANES, x.shape[1]), 0)
    return jnp.concatenate([rolled[:s - SUBLANES], jnp.where(edge < SUBLANES - n, rolled[s - SUBLANES:], 0.0)], axis=0)


def _conv(x, w, b):
    x1, x2 = _shift_down(x, 1), _shift_down(x, 2)
    return b + w[0:1, :] * x2 + w[1:2, :] * x1 + w[2:3, :] * x, x1, x2


def _conv_specs(s, tn):
    xspec = pl.BlockSpec((2, s, tn), lambda j: (0, 0, j))
    wspec = pl.BlockSpec((2, CONV_WIDTH, tn), lambda j: (0, 0, j))
    bspec = pl.BlockSpec((2, 1, tn), lambda j: (0, 0, j))
    return xspec, wspec, bspec


def _conv_gate_fwd(up, cw, cb):
    _, s, f = up.shape
    tn = _pick(f, (256, 128))

    def body(x_ref, w_ref, b_ref, act_ref):
        gate = _conv(x_ref[0], w_ref[0], b_ref[0])[0]
        val = _conv(x_ref[1], w_ref[1], b_ref[1])[0]
        act_ref[...] = (gate * jax.nn.sigmoid(gate) * val).astype(BF16)

    xspec, wspec, bspec = _conv_specs(s, tn)
    return pl.pallas_call(
        body, name="conv_gate_fwd", grid=(f // tn,), in_specs=[xspec, wspec, bspec],
        out_specs=pl.BlockSpec((s, tn), lambda j: (0, j)), out_shape=jax.ShapeDtypeStruct((s, f), BF16),
        compiler_params=_cparams(("parallel",)),
    )(up, cw, cb)


def _up_conv_gate_fwd(h, wb, cw, cb, after=None):
    s, k = h.shape
    nb, _, bn = wb.shape
    hb = nb // 2
    f = hb * bn
    tm = _pick(s, (512, 256, 128))
    n_in = 5 + (after is not None)

    def body(*refs):
        h_ref, wg_ref, wv_ref, w_ref, b_ref = refs[:5]
        up_ref, act_ref, halo_s = refs[n_in:]
        first = pl.program_id(1) == 0
        outs = []
        for half, wt_ref in enumerate((wg_ref, wv_ref)):
            x = jnp.dot(h_ref[...], wt_ref[...], preferred_element_type=F32)
            up_ref[half] = x
            halo = jnp.where(first, 0.0, halo_s[half])
            halo_s[half] = x[tm - SUBLANES:]
            full = jnp.concatenate([halo, x], axis=0)
            x1 = pltpu.roll(full, 1, 0)[SUBLANES:]
            x2 = pltpu.roll(full, 2, 0)[SUBLANES:]
            w = w_ref[half]
            outs.append(b_ref[half] + w[0:1, :] * x2 + w[1:2, :] * x1 + w[2:3, :] * x)
        gate, val = outs
        act_ref[...] = (gate * jax.nn.sigmoid(gate) * val).astype(BF16)

    ins = [h, wb, wb, cw, cb]
    in_specs = [pl.BlockSpec((tm, k), lambda j, i: (i, 0)),
                pl.BlockSpec((None, k, bn), lambda j, i: (j, 0, 0)),
                pl.BlockSpec((None, k, bn), lambda j, i: (j + hb, 0, 0)),
                pl.BlockSpec((2, CONV_WIDTH, bn), lambda j, i: (0, 0, j)),
                pl.BlockSpec((2, 1, bn), lambda j, i: (0, 0, j))]
    if after is not None:
        ins.append(after)
        in_specs.append(pl.BlockSpec(after.shape, lambda j, i: (0, 0)))
    return pl.pallas_call(
        body, name="up_conv_gate_fwd", grid=(hb, s // tm), in_specs=in_specs,
        out_specs=(pl.BlockSpec((2, tm, bn), lambda j, i: (0, i, j)), pl.BlockSpec((tm, bn), lambda j, i: (i, j))),
        out_shape=(jax.ShapeDtypeStruct((2, s, f), F32), jax.ShapeDtypeStruct((s, f), BF16)),
        scratch_shapes=[pltpu.VMEM((2, SUBLANES, bn), F32)],
        compiler_params=_cparams(("parallel", "arbitrary"), 56),
    )(*ins)


def _down_dx_conv_gate_bwd(up, cw, cb, dy, wdown, after):
    _, s, f = up.shape
    d = dy.shape[1]
    tn = _pick(f, (256, 128))

    def body(x_ref, w_ref, b_ref, dy_ref, wd_ref, _, dx_ref, dw_ref, db_ref):
        da = lax.dot_general(dy_ref[...], wd_ref[...], NT_DIMS, preferred_element_type=F32)
        xg, xv = x_ref[0], x_ref[1]
        wg, wv = w_ref[0], w_ref[1]
        gate, xg1, xg2 = _conv(xg, wg, b_ref[0])
        val, xv1, xv2 = _conv(xv, wv, b_ref[1])
        sig = jax.nn.sigmoid(gate)
        dval = da * (gate * sig)
        dgate = da * val * (sig * (1.0 + gate * (1.0 - sig)))
        for half, (x, x1, x2, w, dz) in enumerate(((xg, xg1, xg2, wg, dgate), (xv, xv1, xv2, wv, dval))):
            dx_ref[half] = (w[2:3, :] * dz + w[1:2, :] * _shift_up(dz, 1) + w[0:1, :] * _shift_up(dz, 2)).astype(BF16)
            dw_ref[half, 0:1, :] = jnp.sum(dz * x2, axis=0, keepdims=True)
            dw_ref[half, 1:2, :] = jnp.sum(dz * x1, axis=0, keepdims=True)
            dw_ref[half, 2:3, :] = jnp.sum(dz * x, axis=0, keepdims=True)
            db_ref[half] = jnp.sum(dz, axis=0, keepdims=True)

    xspec, wspec, bspec = _conv_specs(s, tn)
    return pl.pallas_call(
        body, name="down_dx_conv_gate_bwd", grid=(f // tn,),
        in_specs=[xspec, wspec, bspec, pl.BlockSpec((s, d), lambda j: (0, 0)), pl.BlockSpec((tn, d), lambda j: (j, 0)),
                  pl.BlockSpec(after.shape, lambda j: (0, 0))],
        out_specs=(xspec, wspec, bspec),
        out_shape=(jax.ShapeDtypeStruct((2, s, f), BF16), jax.ShapeDtypeStruct((2, CONV_WIDTH, f), F32),
                   jax.ShapeDtypeStruct((2, 1, f), F32)),
        compiler_params=_cparams(("parallel",), 56),
    )(up, cw, cb, dy, wdown, after)


def _mesh_pos():
    return lax.axis_index("x"), lax.axis_index("y"), lax.axis_index("c")


def _remote(src, dst, send_sem, recv_sem, to):
    return pltpu.make_async_remote_copy(src_ref=src, dst_ref=dst, send_sem=send_sem, recv_sem=recv_sem,
                                        device_id=to, device_id_type=pl.DeviceIdType.MESH)


HBM_SPEC = pl.BlockSpec(memory_space=pltpu.HBM)
SEM_SPEC = pl.BlockSpec(memory_space=pltpu.SEMAPHORE)
ANY_SPEC = pl.BlockSpec(memory_space=pl.ANY)
TOKEN_SPEC = pl.BlockSpec(memory_space=pltpu.VMEM)
TOKEN_SHAPE = jax.ShapeDtypeStruct((8, 128), F32)
DATAFLOW = pltpu.SideEffectType.DATAFLOW_SIDE_EFFECTING
GATHER_PLANE = (2, 4, 6)


def _slot(pos):
    return 4 * pos[0] + 2 * pos[1] + pos[2]


def _flip(pos, k):
    return (pos[0] ^ ((k >> 2) & 1), pos[1] ^ ((k >> 1) & 1), pos[2] ^ (k & 1))


def _hbm(a):
    return pltpu.with_memory_space_constraint(a, pltpu.HBM)


def _hbm_shapes(arrays):
    return tuple(pltpu.HBM(a.shape, a.dtype) for a in arrays)


class _Split:
    def __init__(self, n, outs, n_sets):
        k = 2 * n * int(n_sets)
        self.n = n
        self.sems = list(outs[:k])
        self.bufs = list(outs[k:k + 2 * n])
        self.token = outs[-1]

    def sem_set(self, i):
        return self.sems[2 * self.n * i:2 * self.n * (i + 1)]


def _split_call(name, body, bufs, sems_in, n_sets, after):
    n = len(bufs) // 2
    k = 2 * n * int(n_sets)
    m = len(sems_in)
    afters = list(after) if isinstance(after, (list, tuple)) else [after]
    na = len(afters)

    def wrapped(*refs):
        srcs, dsts = refs[:n], refs[n:2 * n]
        s_in = refs[2 * n:2 * n + m]
        s_out = refs[2 * n + m + na:2 * n + m + na + k]
        token, local_sems = refs[-2], refs[-1]
        body(srcs, dsts, s_in, s_out, local_sems)
        token[...] = jnp.zeros_like(token)

    outs = pl.pallas_call(
        wrapped, name=name,
        out_shape=(pltpu.SemaphoreType.DMA(()),) * k + _hbm_shapes(bufs) + (TOKEN_SHAPE,),
        in_specs=[HBM_SPEC] * (2 * n) + [SEM_SPEC] * m + [ANY_SPEC] * na,
        out_specs=(SEM_SPEC,) * k + (HBM_SPEC,) * (2 * n) + (TOKEN_SPEC,),
        input_output_aliases={i: k + i for i in range(2 * n)},
        scratch_shapes=[pltpu.SemaphoreType.DMA((n,))],
        compiler_params=pltpu.CompilerParams(has_side_effects=DATAFLOW),
    )(*[_hbm(b) for b in bufs], *sems_in, *afters)
    return _Split(n, outs, n_sets)


def _wait_slots(land, count, send_sem, recv_sem, me, send=False, recv=False):
    span = land.at[pl.ds(0, count)]
    cp = _remote(span, span, send_sem, recv_sem, me)
    if send:
        cp.wait_send()
    if recv:
        cp.wait_recv()


X_FLIP, Y_FLIP, DIAG_FLIP = 4, 2, 6
BF16_SUBLANES = 16


def _gather_start(name, shards, after):
    n = len(shards)
    my_slot = _slot(_mesh_pos())
    lands = [lax.dynamic_update_slice(lax.empty((N_DEV,) + w.shape, w.dtype), w[None], (my_slot, 0, 0)) for w in shards]

    def body(srcs, dsts, _, sems, local_sems):
        me = _mesh_pos()
        for a in range(n):
            for k in (1, X_FLIP, Y_FLIP):
                _remote(srcs[a], dsts[a].at[_slot(me)], sems[a], sems[n + a], _flip(me, k)).start()

    return _split_call(name, body, list(shards) + lands, [], 1, after)


def _gather_relay(name, started, after):
    n = started.n

    def body(srcs, dsts, sems_a, sems_out, local_sems):
        me = _mesh_pos()
        sibling = _flip(me, 1)
        pass_on, relay = sems_out[:2 * n], sems_out[2 * n:]
        for a in range(n):
            _wait_slots(dsts[a], 3, sems_a[a], sems_a[n + a], me, recv=True)
            from_x = dsts[a].at[_slot(_flip(me, X_FLIP))]
            from_y = dsts[a].at[_slot(_flip(me, Y_FLIP))]
            rows = srcs[a].shape[0]
            if rows % (2 * BF16_SUBLANES) == 0:
                top, bottom = pl.ds(0, rows // 2), pl.ds(rows // 2, rows // 2)
                _remote(from_y.at[top], from_y.at[top], relay[a], relay[n + a], _flip(me, X_FLIP)).start()
                _remote(from_x.at[bottom], from_x.at[bottom], relay[a], relay[n + a], _flip(me, Y_FLIP)).start()
            else:
                _remote(from_y, from_y, relay[a], relay[n + a], _flip(me, X_FLIP)).start()
            for block in (from_x, from_y):
                _remote(block, block, pass_on[a], pass_on[n + a], sibling).start()
        for a in range(n):
            _wait_slots(dsts[a], 3, sems_a[a], sems_a[n + a], me, send=True)

    return _split_call(name, body, started.bufs, started.sems, 2, after)


def _gather_relay_diagonal(name, relayed, after):
    n = relayed.n

    def body(srcs, dsts, relay, pass_on, local_sems):
        me = _mesh_pos()
        for a in range(n):
            _wait_slots(dsts[a], 1, relay[a], relay[n + a], me, recv=True)
            block = dsts[a].at[_slot(_flip(me, DIAG_FLIP))]
            _remote(block, block, pass_on[a], pass_on[n + a], _flip(me, 1)).start()
        for a in range(n):
            _wait_slots(dsts[a], 1, relay[a], relay[n + a], me, send=True)

    return _split_call(name, body, relayed.bufs, relayed.sem_set(1), 1, after)


def _gather_finish(name, relayed, diagonal, after):
    n = relayed.n

    def body(srcs, dsts, sems, _, local_sems):
        me = _mesh_pos()
        first, second = sems[:2 * n], sems[2 * n:]
        for a in range(n):
            _wait_slots(dsts[a], 2, first[a], first[n + a], me, send=True, recv=True)
            _wait_slots(dsts[a], 1, second[a], second[n + a], me, send=True, recv=True)

    return _split_call(name, body, diagonal.bufs, relayed.sem_set(0) + diagonal.sems, 0, after).bufs[n:]


def _exchange_start(name, blocked, after):
    n = len(blocked)
    my_slot = _slot(_mesh_pos())
    rows = [w.shape[-2] // (N_DEV if w.ndim == 2 else 1) for w in blocked]

    def block(ref, a, slot):
        if len(ref.shape) == 3:
            return ref.at[slot]
        return ref.at[pl.ds(pl.multiple_of(slot * rows[a], 16), rows[a])]

    lands = []
    for w, r in zip(blocked, rows):
        mine = lax.dynamic_slice_in_dim(w, my_slot, 1, 0) if w.ndim == 3 else lax.dynamic_slice_in_dim(w, my_slot * r, r, 0)[None]
        lands.append(lax.dynamic_update_slice(lax.empty((N_DEV, r, w.shape[-1]), w.dtype), mine, (my_slot, 0, 0)))

    def body(srcs, dsts, _, sems, local_sems):
        me = _mesh_pos()
        for a in range(n):
            for k in range(1, N_DEV):
                peer = _flip(me, k)
                _remote(block(srcs[a], a, _slot(peer)), dsts[a].at[_slot(me)], sems[a], sems[n + a], peer).start()

    return _split_call(name, body, list(blocked) + lands, [], True, after)


def _exchange_finish(name, started, after):
    n = started.n

    def body(srcs, dsts, sems, _, local_sems):
        me = _mesh_pos()
        for a in range(n):
            _wait_slots(dsts[a], N_DEV - 1, sems[a], sems[n + a], me, send=True, recv=True)

    return _split_call(name, body, started.bufs, started.sems, False, after).bufs[n:]


def _broadcast_start(name, arrays, after):
    n = len(arrays)
    my_slot = _slot(_mesh_pos())
    lands = [lax.dynamic_update_slice(lax.empty((N_DEV,) + w.shape, w.dtype), w[None], (my_slot, 0, 0)) for w in arrays]

    def body(srcs, dsts, _, sems, local_sems):
        me = _mesh_pos()
        for a in range(n):
            for k in range(1, N_DEV):
                _remote(srcs[a], dsts[a].at[_slot(me)], sems[a], sems[n + a], _flip(me, k)).start()

    return _split_call(name, body, list(arrays) + lands, [], True, after)


def _adamw_math(w, g, m, v):
    m = ADAM_B1 * m + (1.0 - ADAM_B1) * g
    v = ADAM_B2 * v + (1.0 - ADAM_B2) * (g * g)
    m_hat = m / (1.0 - ADAM_B1 ** ADAM_STEP)
    v_hat = v / (1.0 - ADAM_B2 ** ADAM_STEP)
    delta = -ADAM_LR * (m_hat / (jnp.sqrt(v_hat) + ADAM_EPS) + ADAM_WD * w)
    return delta, m, v


def _adamw(name, w, m, v, parts, layer, prev=None):
    _, r, c = w.shape
    tr = _pick(r, tuple(t for t in (256, 128, 64, 32, 16) if t * c <= ADAMW_TILE_ELEMS))
    n_prev = 0 if prev is None else 4

    def body(*refs):
        w_ref, m_ref, v_ref, p_ref = refs[:4]
        g_ref, d_ref, nm_ref, nv_ref = refs[4 + n_prev:]
        g = p_ref[0].astype(F32)
        for src in range(1, N_DEV):
            g = g + p_ref[src].astype(F32)
        delta, nm, nv = _adamw_math(w_ref[...], g, m_ref[...], v_ref[...])
        g_ref[...] = g
        d_ref[...] = delta
        nm_ref[...] = nm
        nv_ref[...] = nv

    wspec = pl.BlockSpec((None, tr, c), lambda i: (layer, i, 0))
    pspec = pl.BlockSpec((N_DEV, tr, c), lambda i: (0, i, 0))
    shp = jax.ShapeDtypeStruct(w.shape, F32)
    return pl.pallas_call(
        body, name=name, grid=(r // tr,), in_specs=[wspec] * 3 + [pspec] + [ANY_SPEC] * n_prev,
        out_specs=(wspec,) * 4, out_shape=(shp,) * 4, input_output_aliases={4 + j: j for j in range(n_prev)},
        compiler_params=_cparams(("parallel",)),
    )(w, m, v, parts, *([] if prev is None else prev))


PACK_TILE = 8 * 128


def _pack(arrays):
    flat = []
    for a in arrays:
        v = a.reshape(-1)
        pad = (-v.shape[0]) % PACK_TILE
        flat.append(jnp.pad(v, (0, pad)) if pad else v)
    return jnp.concatenate(flat).reshape(-1, 128)


def _unpack(buf, like):
    flat = buf.reshape(-1)
    out, off = [], 0
    for a in like:
        n = 1
        for dim in a.shape:
            n *= dim
        out.append(flat[off:off + n].reshape(a.shape))
        off += n + (-n) % PACK_TILE
    return out


def _sum_slots(name, gathered):
    _, r, c = gathered.shape

    def body(x_ref, o_ref):
        acc = x_ref[0].astype(F32)
        for src in range(1, N_DEV):
            acc = acc + x_ref[src].astype(F32)
        o_ref[...] = acc

    return pl.pallas_call(body, name=name, out_shape=jax.ShapeDtypeStruct((r, c), F32))(gathered)


def _adamw_small(w, g, m, v):
    shp = jax.ShapeDtypeStruct(w.shape, F32)

    def body(w_ref, g_ref, m_ref, v_ref, d_ref, nm_ref, nv_ref):
        delta, nm, nv = _adamw_math(w_ref[...], g_ref[...], m_ref[...], v_ref[...])
        d_ref[...] = delta
        nm_ref[...] = nm
        nv_ref[...] = nv

    return pl.pallas_call(body, name="adamw_small", out_shape=(shp,) * 3)(w, g, m, v)


def kernel(x, attn_norm_g, w_in, q_norm_g, k_norm_g, sgu_norm_g, sgu_w, sgu_b, out_norm_a_g, out_norm_b_g, w_out, ffn_norm_g, w_up, conv_w, conv_b, w_down, loss_target, m_attn_norm_g, m_w_in, m_q_norm_g, m_k_norm_g, m_sgu_norm_g, m_sgu_w, m_sgu_b, m_out_norm_a_g, m_out_norm_b_g, m_w_out, m_ffn_norm_g, m_w_up, m_conv_w, m_conv_b, m_w_down, v_attn_norm_g, v_w_in, v_q_norm_g, v_k_norm_g, v_sgu_norm_g, v_sgu_w, v_sgu_b, v_out_norm_a_g, v_out_norm_b_g, v_w_out, v_ffn_norm_g, v_w_up, v_conv_w, v_conv_b, v_w_down):
    depth = w_in.shape[0]
    s, d = x.shape[1], x.shape[2]
    n_heads = (d // 2) // HEAD_DIM
    sgu_col0 = 3 * n_heads
    f2 = w_up.shape[2] * N_DEV
    ff = f2 // 2
    my_slot = 4 * lax.axis_index("x") + 2 * lax.axis_index("y") + lax.axis_index("c")

    wb = [(w_in[l].astype(BF16), w_out[l].astype(BF16), w_up[l].astype(BF16), w_down[l].astype(BF16))
          for l in range(depth)]
    groups = {"in0": [wb[0][0]], "out0": [wb[0][1], conv_w.reshape(depth * CONV_WIDTH, -1)], "up0": [wb[0][2]],
              "down0": [wb[0][3]]}
    for l in range(1, depth):
        groups[f"in{l}"] = [wb[l][0], wb[l][1]]
        groups[f"ffn{l}"] = [wb[l][2], wb[l][3]]
    order = list(groups)
    started, relayed = {}, {}

    def start(gname, after):
        started[gname] = _gather_start(f"gather_{gname}_start", groups[gname], after)
        return started[gname].token

    def relay(gname, after):
        relayed[gname] = _gather_relay(f"gather_{gname}_relay", started[gname], after)
        token = relayed[gname].token
        k = order.index(gname)
        nxt = [k + 2] if k + 2 < len(order) - 1 else []
        if k == len(order) - 2:
            nxt = [k + 1]
        for j in nxt:
            token = start(order[j], token)
        return token

    def finish(gname, after):
        diagonal = _gather_relay_diagonal(f"gather_{gname}_diagonal", relayed[gname], after)
        return _gather_finish(f"gather_{gname}_finish", relayed[gname], diagonal, diagonal.token)

    conv_b_all = conv_b.reshape(depth, 2, 1, ff)
    sgu_b_col = sgu_b[..., None]
    token = start(order[1], start(order[0], attn_norm_g))
    token = relay("out0", relay("in0", token))
    win_g = finish("in0", token)[0]

    xs = x[0]
    saved = []
    gathered = []
    for l in range(depth):
        g1 = attn_norm_g[l][None]
        g2 = ffn_norm_g[l][None]
        gq, gk = q_norm_g[l][None], k_norm_g[l][None]
        ga = out_norm_a_g[l][:, None, :]
        gs = sgu_norm_g[l][:, None, :]
        gb = out_norm_b_g[l][:, None, :]
        h1 = _rmsnorm_fwd("attn_norm_fwd", xs, g1)
        p = _mm_nn_blocked("in_proj", h1, win_g, F32)
        att, o, rsum = _attn_fwd(p, gq, gk, ga, n_heads)
        token = relay("up0" if l == 0 else f"ffn{l}", att)
        sg = _sgu_fwd(p, gs, sgu_w[l], sgu_b_col[l], gb, sgu_col0, token)
        mix = jnp.concatenate([att, sg], axis=-1)
        if l == 0:
            wout_g, cw = finish("out0", mix)
            cw = jnp.transpose(cw.reshape(N_DEV, depth, CONV_WIDTH, -1), (1, 2, 0, 3)).reshape(depth, CONV_WIDTH, 2, ff)
            conv_w_all = jnp.transpose(cw, (0, 2, 1, 3))
        x1 = _mm_nn_res("out_proj", mix, wout_g.reshape(d, d), xs)
        h2 = _rmsnorm_fwd("ffn_norm_fwd", x1, g2)
        if l == 0:
            wup_g = finish("up0", h2)[0]
            token = relay("down0", wup_g)
            up, act = _up_conv_gate_fwd(h2, wup_g, conv_w_all[l], conv_b_all[l], after=token)
            wdown_g = finish("down0", up)[0]
        else:
            wup_g, wdown_g = finish(f"ffn{l}", h2)
            up, act = _up_conv_gate_fwd(h2, wup_g, conv_w_all[l], conv_b_all[l])
        saved.append((xs, h1, p, o, rsum, mix, x1, h2, up, act))
        gathered.append((win_g, wout_g, wup_g, wdown_g))
        if l + 1 < depth:
            token = relay(f"in{l + 1}", act)
            x2 = _mm_nn_res("down_proj", act, wdown_g.reshape(ff, d), x1, after=token)
            win_g, wout_g = finish(f"in{l + 1}", x2)
        else:
            x2 = _mm_nn_res("down_proj", act, wdown_g.reshape(ff, d), x1)
        xs = x2

    dx, dxb, loss_vec = _loss_head(xs, loss_target[0])
    loss = lax.psum(loss_vec[0, 0], MESH_AXES)

    exchanges = []
    small = [None] * depth
    small_names = ["attn_norm_g", "q_norm_g", "k_norm_g", "sgu_norm_g", "sgu_w", "sgu_b", "out_norm_a_g",
                   "out_norm_b_g", "ffn_norm_g", "conv_b", "conv_w"]
    for l in reversed(range(depth)):
        xs0, h1, p, o, rsum, mix, x1, h2, up, act = saved[l]
        win_g, wout_g, wup_g, wdown_g = gathered[l]
        wout_full = wout_g.reshape(d, d)
        wdown_full = wdown_g.reshape(ff, d)
        g1 = attn_norm_g[l][None]
        g2 = ffn_norm_g[l][None]
        gq, gk = q_norm_g[l][None], k_norm_g[l][None]
        ga = out_norm_a_g[l][:, None, :]
        gs = sgu_norm_g[l][:, None, :]
        gb = out_norm_b_g[l][:, None, :]
        d_wdown = _mm_tn_plain("down_proj_dw", act, dxb)
        exchanges.append((l, "down", ("w_down",), _exchange_start(f"grad_down{l}_start", [d_wdown], dx)))
        dup, d_cw, d_cb = _down_dx_conv_gate_bwd(up, conv_w_all[l], conv_b_all[l], dxb, wdown_full,
                                                 exchanges[-1][3].token)
        d_wup = _mm_tn_blocked("up_proj_dw", h2, dup, N_DEV, halves=True)
        exchanges.append((l, "up", ("w_up",), _exchange_start(f"grad_up{l}_start", [d_wup], d_cb)))
        dh2 = _mm_nt_blocked("up_proj_dx", dup, wup_g, halves=True, after=exchanges[-1][3].token)
        dx, dxb, d_g2 = _rmsnorm_bwd("ffn_norm_bwd", dh2, x1, g2, dx)
        d_wout = _mm_tn_plain("out_proj_dw", mix, dxb)
        exchanges.append((l, "out", ("w_out",), _exchange_start(f"grad_out{l}_start", [d_wout], d_g2)))
        dmix = _mm_nt_plain("out_proj_dx", dxb, wout_full, after=exchanges[-1][3].token)
        dq, dk, dv, d_gq, d_gk, d_ga = _attn_bwd(p, gq, gk, ga, o, rsum, dmix, n_heads)
        du, dvs, d_gs, d_sw, d_sb, d_gb = _sgu_bwd(p, gs, sgu_w[l], sgu_b_col[l], gb, dmix, sgu_col0, n_heads)
        dp = jnp.concatenate([dq, dk, dv, du, dvs], axis=-1)
        d_win = _mm_tn_blocked("in_proj_dw", h1, dp, N_DEV)
        exchanges.append((l, "in", ("w_in",), _exchange_start(f"grad_in{l}_start", [d_win], d_gq)))
        dh1 = _mm_nt_blocked("in_proj_dx", dp, win_g, after=exchanges[-1][3].token)
        dx, dxb, d_g1 = _rmsnorm_bwd("attn_norm_bwd", dh1, xs0, g1, dx)
        small[l] = dict(attn_norm_g=d_g1[0], q_norm_g=d_gq[0], k_norm_g=d_gk[0], sgu_norm_g=d_gs[:, 0], sgu_w=d_sw,
                        sgu_b=d_sb[..., 0], out_norm_a_g=d_ga[:, 0], out_norm_b_g=d_gb[:, 0], ffn_norm_g=d_g2[0],
                        conv_w=jnp.transpose(d_cw, (1, 0, 2)).reshape(CONV_WIDTH, f2), conv_b=d_cb.reshape(f2))
    grad_x = dx[None]

    f32_names = [n for n in small_names if n != "sgu_w"]
    small_g = [jnp.stack([small[l][n] for l in range(depth)]) for n in f32_names]
    sgu_w_g = jnp.stack([small[l]["sgu_w"] for l in range(depth)])
    small_sent = _broadcast_start("grad_small_start", [_pack(small_g), sgu_w_g.reshape(-1, TILE).astype(BF16)], dx)

    res = {}
    big = dict(w_in=(w_in, m_w_in, v_w_in), w_out=(w_out, m_w_out, v_w_out), w_up=(w_up, m_w_up, v_w_up),
               w_down=(w_down, m_w_down, v_w_down))
    after = [small_sent.token]
    for l, stage, names, ex in exchanges:
        landed = _exchange_finish(f"grad_{stage}{l}_finish", ex, after)
        after = []
        for name, parts in zip(names, landed):
            w, m, v = big[name]
            res[name] = _adamw(f"adamw_{name}", w, m, v, parts, l, res.get(name))
            after.append(res[name][0])
    small_all, sgu_w_all = _exchange_finish("grad_small_finish", small_sent, after)
    small_sum = _unpack(_sum_slots("small_grad_sum", small_all), small_g)
    g_small = dict(zip(f32_names, small_sum))
    g_small["sgu_w"] = _sum_slots("sgu_w_grad_sum", sgu_w_all).reshape(sgu_w.shape)
    cwn = conv_w.shape[2]
    g_small["conv_w"] = lax.dynamic_slice_in_dim(g_small["conv_w"], my_slot * cwn, cwn, axis=2)
    small_w = dict(attn_norm_g=(attn_norm_g, m_attn_norm_g, v_attn_norm_g), q_norm_g=(q_norm_g, m_q_norm_g, v_q_norm_g),
                   k_norm_g=(k_norm_g, m_k_norm_g, v_k_norm_g), sgu_norm_g=(sgu_norm_g, m_sgu_norm_g, v_sgu_norm_g),
                   sgu_w=(sgu_w, m_sgu_w, v_sgu_w), sgu_b=(sgu_b, m_sgu_b, v_sgu_b),
                   out_norm_a_g=(out_norm_a_g, m_out_norm_a_g, v_out_norm_a_g),
                   out_norm_b_g=(out_norm_b_g, m_out_norm_b_g, v_out_norm_b_g),
                   ffn_norm_g=(ffn_norm_g, m_ffn_norm_g, v_ffn_norm_g), conv_b=(conv_b, m_conv_b, v_conv_b),
                   conv_w=(conv_w, m_conv_w, v_conv_w))
    like = [small_w[n][0] for n in small_names]
    pw = _pack([small_w[n][0] for n in small_names])
    pm = _pack([small_w[n][1] for n in small_names])
    pv = _pack([small_w[n][2] for n in small_names])
    pg = _pack([g_small[n].reshape(small_w[n][0].shape) for n in small_names])
    pd, pnm, pnv = _adamw_small(pw, pg, pm, pv)
    for n, dlt, nm, nv in zip(small_names, _unpack(pd, like), _unpack(pnm, like), _unpack(pnv, like)):
        res[n] = (g_small[n].reshape(small_w[n][0].shape), dlt, nm, nv)

    order = ["attn_norm_g", "w_in", "q_norm_g", "k_norm_g", "sgu_norm_g", "sgu_w", "sgu_b", "out_norm_a_g",
             "out_norm_b_g", "w_out", "ffn_norm_g", "w_up", "conv_w", "conv_b", "w_down"]
    outs = [loss, grad_x]
    for field in range(4):
        outs += [res[n][field] for n in order]
    return tuple(outs)
```

```python
import functools

import jax
import jax.numpy as jnp
from jax import lax
from jax.experimental import pallas as pl
from jax.experimental.pallas import tpu as pltpu

F32 = jnp.float32
BF16 = jnp.bfloat16
EPS = 1e-6
HEAD_DIM = 128
TILE = 128
ATTN_VMEM_MB = 58
ATTN_HEADS_PER_STEP = 4
N_GROUPS = 8
CONV_WIDTH = 3
N_DEV = 8
MESH_AXES = ("x", "y", "c")
MIB = 1024 * 1024

ADAM_LR = 0.001
ADAM_B1 = 0.9
ADAM_B2 = 0.999
ADAM_EPS = 1e-08
ADAM_WD = 0.01
ADAM_STEP = 10
ADAMW_TILE_ELEMS = 512 * 1024

NT_DIMS = (((1,), (1,)), ((), ()))
NN_DIMS = (((1,), (0,)), ((), ()))
TN_DIMS = (((0,), (0,)), ((), ()))


def _cparams(sem, vmem_mb=48):
    return pltpu.CompilerParams(dimension_semantics=sem, vmem_limit_bytes=vmem_mb * MIB)


def _pick(n, cands):
    for c in cands:
        if n % c == 0:
            return c
    return n


def _mm(name, grid, ins, in_specs, out_shape, out_spec, dims, has_res=False, parts=None, vmem_mb=56, after=None):
    n_in = 2 + has_res + (after is not None)
    if after is not None:
        ins = tuple(ins) + (after,)
        in_specs = list(in_specs) + [pl.BlockSpec(after.shape, lambda *_: (0, 0))]

    def body(*refs):
        a_ref, b_ref = refs[:2]
        o_ref = refs[n_in]
        if parts is None:
            acc = lax.dot_general(a_ref[...], b_ref[...], dims, preferred_element_type=F32)
        else:
            acc = None
            for part in parts:
                a, b = part(a_ref, b_ref)
                prod = lax.dot_general(a, b, dims, preferred_element_type=F32)
                acc = prod if acc is None else acc + prod
        if has_res:
            acc = acc + refs[2][...]
        o_ref[...] = acc.astype(o_ref.dtype)

    return pl.pallas_call(
        body, name=name, grid=grid, in_specs=in_specs, out_specs=out_spec, out_shape=out_shape,
        compiler_params=_cparams(("parallel",) * len(grid), vmem_mb),
    )(*ins)


def _rows_for(m, row_bytes, budget):
    return _pick(m, tuple(t for t in (2048, 1024, 512, 256, 128) if t * row_bytes <= budget))


def _mm_nn_blocked(name, a, wb, out_dtype, halves=False, after=None):
    m, k = a.shape
    nb, _, bn = wb.shape
    tm = _rows_for(m, bn * jnp.dtype(out_dtype).itemsize, 6 * MIB)
    a_spec = pl.BlockSpec((tm, k), lambda j, i: (i, 0))
    b_spec = pl.BlockSpec((None, k, bn), lambda j, i: (j, 0, 0))
    if halves:
        hb = nb // 2
        out_shape = jax.ShapeDtypeStruct((2, m, hb * bn), out_dtype)
        o_spec = pl.BlockSpec((None, tm, bn), lambda j, i: (j // hb, i, j % hb))
    else:
        out_shape = jax.ShapeDtypeStruct((m, nb * bn), out_dtype)
        o_spec = pl.BlockSpec((tm, bn), lambda j, i: (i, j))
    return _mm(name, (nb, m // tm), (a, wb), [a_spec, b_spec], out_shape, o_spec, NN_DIMS, after=after)


def _mm_nn_res(name, a, w, res, after=None):
    m, k = a.shape
    n = w.shape[1]
    tm = _pick(m, (512, 256, 128))
    tn = _rows_for(n, k * 2, 12 * MIB)
    a_spec = pl.BlockSpec((tm, k), lambda j, i: (i, 0))
    b_spec = pl.BlockSpec((k, tn), lambda j, i: (0, j))
    r_spec = pl.BlockSpec((tm, tn), lambda j, i: (i, j))
    o_spec = pl.BlockSpec((tm, tn), lambda j, i: (i, j))
    return _mm(name, (n // tn, m // tm), (a, w, res), [a_spec, b_spec, r_spec], jax.ShapeDtypeStruct((m, n), F32),
               o_spec, NN_DIMS, has_res=True, after=after)


def _mm_nt_blocked(name, dy, wb, halves=False, after=None):
    nb, n, bn = wb.shape
    m = dy.shape[-2]
    tm = _pick(m, (512, 256, 128))
    tn = _rows_for(n, nb * bn * 2, 12 * MIB)
    if halves:
        hb = nb // 2
        a_spec = pl.BlockSpec((2, tm, hb * bn), lambda j, i: (0, i, 0))
        a_part = lambda kk: (lambda a_ref: a_ref[kk // hb, :, (kk % hb) * bn:(kk % hb + 1) * bn])
    else:
        a_spec = pl.BlockSpec((tm, nb * bn), lambda j, i: (i, 0))
        a_part = lambda kk: (lambda a_ref: a_ref[:, kk * bn:(kk + 1) * bn])
    parts = [(lambda a_ref, b_ref, kk=kk, sel=a_part(kk): (sel(a_ref), b_ref[kk])) for kk in range(nb)]
    b_spec = pl.BlockSpec((nb, tn, bn), lambda j, i: (0, j, 0))
    o_spec = pl.BlockSpec((tm, tn), lambda j, i: (i, j))
    return _mm(name, (n // tn, m // tm), (dy, wb), [a_spec, b_spec], jax.ShapeDtypeStruct((m, n), F32), o_spec,
               NT_DIMS, parts=parts, after=after)


def _mm_nt_plain(name, dy, w, out_dtype=F32, after=None):
    m, k = dy.shape
    n = w.shape[0]
    tm = _rows_for(m, k * 2, 8 * MIB)
    tn = _pick(n, (512, 256, 128))
    a_spec = pl.BlockSpec((tm, k), lambda j, i: (i, 0))
    b_spec = pl.BlockSpec((tn, k), lambda j, i: (j, 0))
    o_spec = pl.BlockSpec((tm, tn), lambda j, i: (i, j))
    return _mm(name, (n // tn, m // tm), (dy, w), [a_spec, b_spec], jax.ShapeDtypeStruct((m, n), out_dtype), o_spec,
               NT_DIMS, after=after)


def _mm_tn_blocked(name, a, dy, nb, halves=False):
    s, k1 = a.shape
    bn = (dy.shape[-1] * (2 if halves else 1)) // nb
    tm = _rows_for(k1, bn * 2, 6 * MIB)
    a_spec = pl.BlockSpec((s, tm), lambda j, i: (0, i))
    if halves:
        hb = nb // 2
        b_spec = pl.BlockSpec((None, s, bn), lambda j, i: (j // hb, 0, j % hb))
    else:
        b_spec = pl.BlockSpec((s, bn), lambda j, i: (0, j))
    o_spec = pl.BlockSpec((None, tm, bn), lambda j, i: (j, i, 0))
    return _mm(name, (nb, k1 // tm), (a, dy), [a_spec, b_spec], jax.ShapeDtypeStruct((nb, k1, bn), BF16), o_spec,
               TN_DIMS)


def _mm_tn_plain(name, a, dy):
    s, k1 = a.shape
    n = dy.shape[1]
    tm = _pick(k1, (512, 256, 128))
    tn = _rows_for(n, s * 2, 8 * MIB)
    a_spec = pl.BlockSpec((s, tm), lambda i, j: (0, i))
    b_spec = pl.BlockSpec((s, tn), lambda i, j: (0, j))
    o_spec = pl.BlockSpec((tm, tn), lambda i, j: (i, j))
    return _mm(name, (k1 // tm, n // tn), (a, dy), [a_spec, b_spec], jax.ShapeDtypeStruct((k1, n), BF16), o_spec,
               TN_DIMS)


def _rstd(x):
    return lax.rsqrt(jnp.mean(x * x, axis=-1, keepdims=True) + EPS)


def _norm_bwd(dy, xhat, r, g):
    dxhat = dy * g
    return r * (dxhat - xhat * jnp.mean(dxhat * xhat, axis=-1, keepdims=True))


def _rmsnorm_fwd(name, x, g):
    s, d = x.shape
    tr = _pick(s, (256, 128))

    def body(x_ref, g_ref, h_ref):
        xv = x_ref[...]
        h_ref[...] = (xv * _rstd(xv) * g_ref[...]).astype(BF16)

    return pl.pallas_call(
        body, name=name, grid=(s // tr,),
        in_specs=[pl.BlockSpec((tr, d), lambda i: (i, 0)), pl.BlockSpec((1, d), lambda i: (0, 0))],
        out_specs=pl.BlockSpec((tr, d), lambda i: (i, 0)),
        out_shape=jax.ShapeDtypeStruct((s, d), BF16), compiler_params=_cparams(("parallel",)),
    )(x, g)


def _rmsnorm_bwd(name, dh, x, g, dres):
    s, d = x.shape
    tr = _pick(s, (256, 128))

    def body(dh_ref, x_ref, g_ref, dres_ref, dx_ref, dxb_ref, dg_ref):
        xv = x_ref[...]
        r = _rstd(xv)
        xhat = xv * r
        dhv = dh_ref[...]
        dx = dres_ref[...] + _norm_bwd(dhv, xhat, r, g_ref[...])
        dx_ref[...] = dx
        dxb_ref[...] = dx.astype(BF16)
        part = jnp.sum(dhv * xhat, axis=0, keepdims=True)

        @pl.when(pl.program_id(0) == 0)
        def _():
            dg_ref[...] = part

        @pl.when(pl.program_id(0) > 0)
        def _():
            dg_ref[...] += part

    row = pl.BlockSpec((tr, d), lambda i: (i, 0))
    vec = pl.BlockSpec((1, d), lambda i: (0, 0))
    return pl.pallas_call(
        body, name=name, grid=(s // tr,), in_specs=[row, row, vec, row], out_specs=(row, row, vec),
        out_shape=(jax.ShapeDtypeStruct((s, d), F32), jax.ShapeDtypeStruct((s, d), BF16),
                   jax.ShapeDtypeStruct((1, d), F32)),
        compiler_params=_cparams(("arbitrary",)),
    )(dh, x, g, dres)


def _loss_head(y, target):
    s, d = y.shape
    tr = _pick(s, (256, 128))

    def body(y_ref, t_ref, dy_ref, dyb_ref, loss_ref):
        err = y_ref[...] - t_ref[...]
        dy = err * (1.0 / d)
        dy_ref[...] = dy
        dyb_ref[...] = dy.astype(BF16)
        part = 0.5 * jnp.sum(jnp.mean(err * err, axis=-1, keepdims=True), axis=0, keepdims=True)
        part = jnp.broadcast_to(part, (1, 128))

        @pl.when(pl.program_id(0) == 0)
        def _():
            loss_ref[...] = part

        @pl.when(pl.program_id(0) > 0)
        def _():
            loss_ref[...] += part

    row = pl.BlockSpec((tr, d), lambda i: (i, 0))
    return pl.pallas_call(
        body, name="loss_head", grid=(s // tr,), in_specs=[row, row],
        out_specs=(row, row, pl.BlockSpec((1, 128), lambda i: (0, 0))),
        out_shape=(jax.ShapeDtypeStruct((s, d), F32), jax.ShapeDtypeStruct((s, d), BF16),
                   jax.ShapeDtypeStruct((1, 128), F32)),
        compiler_params=_cparams(("arbitrary",)),
    )(y, target)


def _split_dot(x, tri):
    hi = x.astype(BF16)
    lo = (x - hi.astype(F32)).astype(BF16)
    return (jnp.dot(hi, tri, preferred_element_type=F32) + jnp.dot(lo, tri, preferred_element_type=F32))


def _tile_iotas():
    row = lax.broadcasted_iota(jnp.int32, (TILE, TILE), 0)
    col = lax.broadcasted_iota(jnp.int32, (TILE, TILE), 1)
    return row, col


def _sb_logits(qi, kb, mask):
    z = lax.dot_general(qi, kb, NT_DIMS, preferred_element_type=F32) * (HEAD_DIM ** -0.5)
    sp = jnp.log(1.0 + jnp.exp(-jnp.abs(z)))
    lb = jnp.minimum(z, 0.0) - sp
    l1m = -jnp.maximum(z, 0.0) - sp
    if mask is not None:
        l1m = jnp.where(mask, l1m, 0.0)
    return lb, l1m


def _attn_fwd(p, gq, gk, ga, n_heads):
    s = p.shape[0]
    nq = s // TILE

    hp = ATTN_HEADS_PER_STEP
    wd = hp * HEAD_DIM

    def body(q_ref, k_ref, v_ref, gq_ref, gk_ref, ga_ref, att_ref, o_ref, r_ref, qn_s, kn_s, vb_s):
        heads = [slice(hh * HEAD_DIM, (hh + 1) * HEAD_DIM) for hh in range(hp)]
        for hd in heads:
            qv = q_ref[:, hd]
            qn_s[:, hd] = (qv * _rstd(qv) * gq_ref[...]).astype(BF16)
            kv = k_ref[:, hd]
            kn_s[:, hd] = (kv * _rstd(kv) * gk_ref[...]).astype(BF16)
        vb_s[...] = v_ref[...].astype(BF16)
        row, col = _tile_iotas()
        causal = col < row
        upper_ones = jnp.concatenate([(row > col).astype(BF16), jnp.ones((TILE, TILE), BF16)], axis=1)

        def tiles(rows, key_blocks, states):
            chains = [(hi, hd, keys, mask) for hi, hd in enumerate(heads) for keys, mask in key_blocks]
            logits = [_sb_logits(qn_s[rows, hd], kn_s[keys, hd], mask) for _, hd, keys, mask in chains]
            sums = [_split_dot(l1m, upper_ones) for _, l1m in logits]
            carry = [c for _, c in states]
            probs = []
            for (hi, _, _, mask), (lb, _), sm in zip(chains, logits, sums):
                a = jnp.exp(lb + sm[:, :TILE] + carry[hi])
                carry[hi] = carry[hi] + sm[:, TILE:]
                probs.append((a if mask is None else jnp.where(mask, a, 0.0)).astype(BF16))
            outs = [jnp.dot(a, vb_s[keys, hd], preferred_element_type=F32) for a, (_, hd, keys, _) in zip(probs, chains)]
            acc = [o_acc for o_acc, _ in states]
            for (hi, _, _, _), o in zip(chains, outs):
                acc[hi] = acc[hi] + o
            return tuple(zip(acc, carry))

        def key_block(b):
            return pl.ds(pl.multiple_of(b * TILE, TILE), TILE), None

        def qblock(i, _):
            rows = pl.ds(pl.multiple_of(i * TILE, TILE), TILE)
            zero = jnp.zeros((TILE, HEAD_DIM), F32)
            states = tuple((zero, zero) for _ in heads)
            states = lax.cond(i % 2 == 1, lambda st: tiles(rows, [(rows, causal), key_block(i - 1)], st),
                              lambda st: tiles(rows, [(rows, causal)], st), states)
            top = i - i % 2

            def kblocks(jj, states):
                return tiles(rows, [key_block(top - 1 - 2 * jj), key_block(top - 2 - 2 * jj)], states)

            states = lax.fori_loop(0, i // 2, kblocks, states)
            for hh, (hd, (o_acc, c)) in enumerate(zip(heads, states)):
                o_ref[rows, hd] = o_acc
                r_ref[rows, hd] = c
                att_ref[rows, hd] = (o_acc * _rstd(o_acc) * ga_ref[hh]).astype(BF16)
            return 0

        lax.fori_loop(0, nq, qblock, 0)

    col_blk = lambda off: pl.BlockSpec((s, wd), lambda h: (0, off + h))
    vec = pl.BlockSpec((1, HEAD_DIM), lambda h: (0, 0))
    hvec = pl.BlockSpec((hp, 1, HEAD_DIM), lambda h: (h, 0, 0))
    out = pl.BlockSpec((s, wd), lambda h: (0, h))
    w = n_heads * HEAD_DIM
    steps = n_heads // hp
    return pl.pallas_call(
        body, name="attn_fwd", grid=(steps,),
        in_specs=[col_blk(0), col_blk(steps), col_blk(2 * steps), vec, vec, hvec],
        out_specs=(out, out, out),
        out_shape=(jax.ShapeDtypeStruct((s, w), BF16), jax.ShapeDtypeStruct((s, w), F32),
                   jax.ShapeDtypeStruct((s, w), F32)),
        scratch_shapes=[pltpu.VMEM((s, wd), BF16)] * 3,
        compiler_params=_cparams(("parallel",), ATTN_VMEM_MB),
    )(p, p, p, gq, gk, ga)


def _attn_bwd(p, gq, gk, ga, o, rsum, dmix, n_heads):
    s = p.shape[0]
    nq = s // TILE

    hp = ATTN_HEADS_PER_STEP
    wd = hp * HEAD_DIM
    scale = HEAD_DIM ** -0.5

    def body(q_ref, k_ref, v_ref, gq_ref, gk_ref, ga_ref, o_ref, r_ref, dm_ref,
             dq_ref, dk_ref, dv_ref, dgq_ref, dgk_ref, dga_ref,
             qn_s, kn_s, vb_s, do_s, dqn_s, dkn_s, dv_s):
        step = pl.program_id(0)
        gqv, gkv = gq_ref[...], gk_ref[...]
        heads = [slice(hh * HEAD_DIM, (hh + 1) * HEAD_DIM) for hh in range(hp)]
        for hh, hd in enumerate(heads):
            qv = q_ref[:, hd]
            qn_s[:, hd] = (qv * _rstd(qv) * gqv).astype(BF16)
            kv = k_ref[:, hd]
            kn_s[:, hd] = (kv * _rstd(kv) * gkv).astype(BF16)
            ov = o_ref[:, hd]
            ro = _rstd(ov)
            ohat = ov * ro
            dm = dm_ref[:, hd]
            dga_ref[hh] = jnp.sum(dm * ohat, axis=0, keepdims=True)
            do_s[:, hd] = _norm_bwd(dm, ohat, ro, ga_ref[hh]).astype(BF16)
        vb_s[...] = v_ref[...].astype(BF16)
        dkn_s[...] = jnp.zeros_like(dkn_s)
        dv_s[...] = jnp.zeros_like(dv_s)
        row, col = _tile_iotas()
        causal = col < row
        ones = jnp.ones((TILE, TILE), BF16)
        incl_ones = jnp.concatenate([(row <= col).astype(BF16), ones], axis=1)
        excl_ones = jnp.concatenate([(row < col).astype(BF16), ones], axis=1)

        def tiles(rows, key_blocks, states):
            chains = [(hi, hd, keys, mask) for hi, hd in enumerate(heads) for keys, mask in key_blocks]
            qis = [qn_s[rows, hd] for hd in heads]
            dois = [do_s[rows, hd] for hd in heads]
            logits = [_sb_logits(qis[hi], kn_s[keys, hd], mask) for hi, hd, keys, mask in chains]
            sums = [_split_dot(l1m, incl_ones) for _, l1m in logits]
            das = [lax.dot_general(dois[hi], vb_s[keys, hd], NT_DIMS, preferred_element_type=F32)
                   for hi, hd, keys, _ in chains]
            pfx = [st[1] for st in states]
            probs, dss = [], []
            for (hi, hd, _, mask), (lb, _), sm, da in zip(chains, logits, sums, das):
                a = jnp.exp(lb + (r_ref[rows, hd] - pfx[hi] - sm[:, :TILE]))
                pfx[hi] = pfx[hi] + sm[:, TILE:]
                a = a if mask is None else jnp.where(mask, a, 0.0)
                probs.append(a.astype(BF16))
                dss.append(da * a)
            dsums = [_split_dot(ds, excl_ones) for ds in dss]
            pc = [st[2] for st in states]
            dzs = []
            for (hi, _, _, mask), (lb, _), ds, dsm in zip(chains, logits, dss, dsums):
                dl1m = pc[hi] + dsm[:, :TILE]
                pc[hi] = pc[hi] + dsm[:, TILE:]
                dl1m = dl1m if mask is None else jnp.where(mask, dl1m, 0.0)
                beta = jnp.exp(lb)
                dzs.append(((ds * (1.0 - beta) - dl1m * beta) * scale).astype(BF16))
            dqs = [jnp.dot(dz, kn_s[keys, hd], preferred_element_type=F32) for dz, (_, hd, keys, _) in zip(dzs, chains)]
            for dz, a, (hi, hd, keys, _) in zip(dzs, probs, chains):
                dkn_s[keys, hd] += lax.dot_general(dz, qis[hi], TN_DIMS, preferred_element_type=F32)
                dv_s[keys, hd] += lax.dot_general(a, dois[hi], TN_DIMS, preferred_element_type=F32)
            dq_acc = [st[0] for st in states]
            for (hi, _, _, _), dq in zip(chains, dqs):
                dq_acc[hi] = dq_acc[hi] + dq
            return tuple(zip(dq_acc, pfx, pc))

        def key_block(b):
            return pl.ds(pl.multiple_of(b * TILE, TILE), TILE), None

        def qblock(i, _):
            rows = pl.ds(pl.multiple_of(i * TILE, TILE), TILE)
            zero = jnp.zeros((TILE, HEAD_DIM), F32)

            def kblocks(jj, states):
                return tiles(rows, [key_block(2 * jj), key_block(2 * jj + 1)], states)

            states = lax.fori_loop(0, i // 2, kblocks, tuple((zero, zero, zero) for _ in heads))
            states = lax.cond(i % 2 == 1, lambda st: tiles(rows, [key_block(i - 1), (rows, causal)], st),
                              lambda st: tiles(rows, [(rows, causal)], st), states)
            for hd, (dq_acc, _, _) in zip(heads, states):
                dqn_s[rows, hd] = dq_acc
            return 0

        lax.fori_loop(0, nq, qblock, 0)

        def norm_in_bwd(x_ref, g, dn_s, dx_ref, dg_ref):
            part = jnp.zeros((1, HEAD_DIM), F32)
            for hd in heads:
                xv = x_ref[:, hd]
                r = _rstd(xv)
                xhat = xv * r
                dn = dn_s[:, hd]
                dx_ref[:, hd] = _norm_bwd(dn, xhat, r, g).astype(BF16)
                part = part + jnp.sum(dn * xhat, axis=0, keepdims=True)

            @pl.when(step == 0)
            def _():
                dg_ref[...] = part

            @pl.when(step > 0)
            def _():
                dg_ref[...] += part

        norm_in_bwd(q_ref, gqv, dqn_s, dq_ref, dgq_ref)
        norm_in_bwd(k_ref, gkv, dkn_s, dk_ref, dgk_ref)
        dv_ref[...] = dv_s[...].astype(BF16)

    once = pl.Buffered(1)
    steps = n_heads // hp
    col_blk = lambda off: pl.BlockSpec((s, wd), lambda h: (0, off + h), pipeline_mode=once)
    vec = pl.BlockSpec((1, HEAD_DIM), lambda h: (0, 0))
    hvec = pl.BlockSpec((hp, 1, HEAD_DIM), lambda h: (h, 0, 0))
    blk = pl.BlockSpec((s, wd), lambda h: (0, h), pipeline_mode=once)
    w = n_heads * HEAD_DIM
    big = jax.ShapeDtypeStruct((s, w), BF16)
    return pl.pallas_call(
        body, name="attn_bwd", grid=(steps,),
        in_specs=[col_blk(0), col_blk(steps), col_blk(2 * steps), vec, vec, hvec, blk, blk, blk],
        out_specs=(blk, blk, blk, vec, vec, hvec),
        out_shape=(big, big, big, jax.ShapeDtypeStruct((1, HEAD_DIM), F32), jax.ShapeDtypeStruct((1, HEAD_DIM), F32),
                   jax.ShapeDtypeStruct((n_heads, 1, HEAD_DIM), F32)),
        scratch_shapes=[pltpu.VMEM((s, wd), BF16)] * 4 + [pltpu.VMEM((s, wd), F32)] * 3,
        compiler_params=_cparams(("arbitrary",), ATTN_VMEM_MB),
    )(p, p, p, gq, gk, ga, o, rsum, dmix)


_INV_SQRT2 = 0.7071067811865476
_INV_SQRT_2PI = 0.3989422804014327


def _gelu(x):
    return 0.5 * x * (1.0 + lax.erf(x * _INV_SQRT2))


def _gelu_grad(x):
    return 0.5 * (1.0 + lax.erf(x * _INV_SQRT2)) + x * (_INV_SQRT_2PI * jnp.exp(-0.5 * x * x))


def _sgu_fwd(p, gs, w_s, b_s, gb, col0, after):
    s = p.shape[0]
    n_chunks = s // TILE
    per_trip = _pick(n_chunks, (4, 2, 1))

    def body(u_ref, v_ref, gs_ref, w_ref, b_ref, gb_ref, _, out_ref, vs_s):
        vg = _gelu(v_ref[...])
        vs_s[...] = (vg * _rstd(vg) * gs_ref[...]).astype(BF16)
        row, col = _tile_iotas()
        wt = jnp.where(col <= row, w_ref[...], 0.0).astype(BF16)
        bcol = b_ref[...]
        gbv = gb_ref[...]

        def chunks(c, _):
            rows = [pl.ds(pl.multiple_of((c * per_trip + k) * TILE, TILE), TILE) for k in range(per_trip)]
            mixed = [jnp.dot(wt, vs_s[r, :], preferred_element_type=F32) + bcol for r in rows]
            sgs = [_gelu(u_ref[r, :]) * mx for r, mx in zip(rows, mixed)]
            for r, sg in zip(rows, sgs):
                out_ref[r, :] = (sg * _rstd(sg) * gbv).astype(BF16)
            return 0

        lax.fori_loop(0, n_chunks // per_trip, chunks, 0)

    col_blk = lambda off: pl.BlockSpec((s, HEAD_DIM), lambda g: (0, off + g))
    gvec = pl.BlockSpec((None, 1, HEAD_DIM), lambda g: (g, 0, 0))
    return pl.pallas_call(
        body, name="sgu_fwd", grid=(N_GROUPS,),
        in_specs=[col_blk(col0), col_blk(col0 + N_GROUPS), gvec,
                  pl.BlockSpec((None, TILE, TILE), lambda g: (g, 0, 0)),
                  pl.BlockSpec((None, TILE, 1), lambda g: (g, 0, 0)), gvec,
                  pl.BlockSpec(after.shape, lambda g: (0, 0))],
        out_specs=pl.BlockSpec((s, HEAD_DIM), lambda g: (0, g)),
        out_shape=jax.ShapeDtypeStruct((s, N_GROUPS * HEAD_DIM), BF16),
        scratch_shapes=[pltpu.VMEM((s, HEAD_DIM), BF16)],
        compiler_params=_cparams(("parallel",)),
    )(p, p, gs, w_s, b_s, gb, after)


def _sgu_bwd(p, gs, w_s, b_s, gb, dmix, col0, dm_col0):
    s = p.shape[0]
    n_chunks = s // TILE
    per_trip = _pick(n_chunks, (4, 2, 1))

    def body(u_ref, v_ref, gs_ref, w_ref, b_ref, gb_ref, dm_ref,
             du_ref, dv_ref, dgs_ref, dw_ref, db_ref, dgb_ref, vs_s, dvs_s):
        gsv = gs_ref[...]
        gbv = gb_ref[...]
        vg = _gelu(v_ref[...])
        vs_s[...] = (vg * _rstd(vg) * gsv).astype(BF16)
        row, col = _tile_iotas()
        causal = col <= row
        wt = jnp.where(causal, w_ref[...], 0.0).astype(BF16)
        bcol = b_ref[...]

        def chunks(c, carry):
            dw_acc, db_acc, dgb_acc = carry
            rows = [pl.ds(pl.multiple_of((c * per_trip + k) * TILE, TILE), TILE) for k in range(per_trip)]
            vss = [vs_s[r, :] for r in rows]
            mixed = [jnp.dot(wt, vs, preferred_element_type=F32) + bcol for vs in vss]
            dmbs = []
            for r, mx in zip(rows, mixed):
                u_pre = u_ref[r, :]
                u = _gelu(u_pre)
                sg = u * mx
                rs = _rstd(sg)
                sghat = sg * rs
                dm = dm_ref[r, :]
                dsg = _norm_bwd(dm, sghat, rs, gbv)
                dgb_acc = dgb_acc + jnp.sum(dm * sghat, axis=0, keepdims=True)
                du_ref[r, :] = (dsg * mx * _gelu_grad(u_pre)).astype(BF16)
                dmixed = dsg * u
                db_acc = db_acc + jnp.sum(dmixed, axis=1, keepdims=True)
                dmbs.append(dmixed.astype(BF16))
            for dmb, vs in zip(dmbs, vss):
                dw_acc = dw_acc + lax.dot_general(dmb, vs, NT_DIMS, preferred_element_type=F32)
            for r, dmb in zip(rows, dmbs):
                dvs_s[r, :] = lax.dot_general(wt, dmb, TN_DIMS, preferred_element_type=F32)
            return dw_acc, db_acc, dgb_acc

        dw_acc, db_acc, dgb_acc = lax.fori_loop(
            0, n_chunks // per_trip, chunks,
            (jnp.zeros((TILE, TILE), F32), jnp.zeros((TILE, 1), F32), jnp.zeros((1, HEAD_DIM), F32)))
        dw_ref[...] = jnp.where(causal, dw_acc, 0.0)
        db_ref[...] = db_acc
        dgb_ref[...] = dgb_acc
        v_pre = v_ref[...]
        vg = _gelu(v_pre)
        rv = _rstd(vg)
        vhat = vg * rv
        dvs = dvs_s[...]
        dgs_ref[...] = jnp.sum(dvs * vhat, axis=0, keepdims=True)
        dv_ref[...] = (_norm_bwd(dvs, vhat, rv, gsv) * _gelu_grad(v_pre)).astype(BF16)

    col_blk = lambda off: pl.BlockSpec((s, HEAD_DIM), lambda g: (0, off + g))
    gvec = pl.BlockSpec((None, 1, HEAD_DIM), lambda g: (g, 0, 0))
    wspec = pl.BlockSpec((None, TILE, TILE), lambda g: (g, 0, 0))
    bspec = pl.BlockSpec((None, TILE, 1), lambda g: (g, 0, 0))
    blk = pl.BlockSpec((s, HEAD_DIM), lambda g: (0, g))
    big = jax.ShapeDtypeStruct((s, N_GROUPS * HEAD_DIM), BF16)
    gshape = jax.ShapeDtypeStruct((N_GROUPS, 1, HEAD_DIM), F32)
    return pl.pallas_call(
        body, name="sgu_bwd", grid=(N_GROUPS,),
        in_specs=[col_blk(col0), col_blk(col0 + N_GROUPS), gvec, wspec, bspec, gvec, col_blk(dm_col0)],
        out_specs=(blk, blk, gvec, wspec, bspec, gvec),
        out_shape=(big, big, gshape, jax.ShapeDtypeStruct((N_GROUPS, TILE, TILE), F32),
                   jax.ShapeDtypeStruct((N_GROUPS, TILE, 1), F32), gshape),
        scratch_shapes=[pltpu.VMEM((s, HEAD_DIM), BF16), pltpu.VMEM((s, HEAD_DIM), F32)],
        compiler_params=_cparams(("parallel",)),
    )(p, p, gs, w_s, b_s, gb, dmix)


SUBLANES = 8


def _shift_down(x, n):
    rolled = pltpu.roll(x, n, 0)
    edge = lax.broadcasted_iota(jnp.int32, (SUBLANES, x.shape[1]), 0)
    return jnp.concatenate([jnp.where(edge >= n, rolled[:SUBLANES], 0.0), rolled[SUBLANES:]], axis=0)


def _shift_up(x, n):
    s = x.shape[0]
    rolled = pltpu.roll(x, s - n, 0)
    edge = lax.broadcasted_iota(jnp.int32, (SUBLANES, x.shape[1]), 0)
    return jnp.concatenate([rolled[:s - SUBLANES], jnp.where(edge < SUBLANES - n, rolled[s - SUBLANES:], 0.0)], axis=0)


def _conv(x, w, b):
    x1, x2 = _shift_down(x, 1), _shift_down(x, 2)
    return b + w[0:1, :] * x2 + w[1:2, :] * x1 + w[2:3, :] * x, x1, x2


def _conv_specs(s, tn):
    xspec = pl.BlockSpec((2, s, tn), lambda j: (0, 0, j))
    wspec = pl.BlockSpec((2, CONV_WIDTH, tn), lambda j: (0, 0, j))
    bspec = pl.BlockSpec((2, 1, tn), lambda j: (0, 0, j))
    return xspec, wspec, bspec


def _conv_gate_fwd(up, cw, cb):
    _, s, f = up.shape
    tn = _pick(f, (256, 128))

    def body(x_ref, w_ref, b_ref, act_ref):
        gate = _conv(x_ref[0], w_ref[0], b_ref[0])[0]
        val = _conv(x_ref[1], w_ref[1], b_ref[1])[0]
        act_ref[...] = (gate * jax.nn.sigmoid(gate) * val).astype(BF16)

    xspec, wspec, bspec = _conv_specs(s, tn)
    return pl.pallas_call(
        body, name="conv_gate_fwd", grid=(f // tn,), in_specs=[xspec, wspec, bspec],
        out_specs=pl.BlockSpec((s, tn), lambda j: (0, j)), out_shape=jax.ShapeDtypeStruct((s, f), BF16),
        compiler_params=_cparams(("parallel",)),
    )(up, cw, cb)


def _up_conv_gate_fwd(h, wb, cw, cb, after=None):
    s, k = h.shape
    nb, _, bn = wb.shape
    hb = nb // 2
    f = hb * bn
    tm = _pick(s, (512, 256, 128))
    n_in = 5 + (after is not None)

    def body(*refs):
        h_ref, wg_ref, wv_ref, w_ref, b_ref = refs[:5]
        up_ref, act_ref, halo_s = refs[n_in:]
        first = pl.program_id(1) == 0
        outs = []
        for half, wt_ref in enumerate((wg_ref, wv_ref)):
            x = jnp.dot(h_ref[...], wt_ref[...], preferred_element_type=F32)
            up_ref[half] = x
            halo = jnp.where(first, 0.0, halo_s[half])
            halo_s[half] = x[tm - SUBLANES:]
            full = jnp.concatenate([halo, x], axis=0)
            x1 = pltpu.roll(full, 1, 0)[SUBLANES:]
            x2 = pltpu.roll(full, 2, 0)[SUBLANES:]
            w = w_ref[half]
            outs.append(b_ref[half] + w[0:1, :] * x2 + w[1:2, :] * x1 + w[2:3, :] * x)
        gate, val = outs
        act_ref[...] = (gate * jax.nn.sigmoid(gate) * val).astype(BF16)

    ins = [h, wb, wb, cw, cb]
    in_specs = [pl.BlockSpec((tm, k), lambda j, i: (i, 0)),
                pl.BlockSpec((None, k, bn), lambda j, i: (j, 0, 0)),
                pl.BlockSpec((None, k, bn), lambda j, i: (j + hb, 0, 0)),
                pl.BlockSpec((2, CONV_WIDTH, bn), lambda j, i: (0, 0, j)),
                pl.BlockSpec((2, 1, bn), lambda j, i: (0, 0, j))]
    if after is not None:
        ins.append(after)
        in_specs.append(pl.BlockSpec(after.shape, lambda j, i: (0, 0)))
    return pl.pallas_call(
        body, name="up_conv_gate_fwd", grid=(hb, s // tm), in_specs=in_specs,
        out_specs=(pl.BlockSpec((2, tm, bn), lambda j, i: (0, i, j)), pl.BlockSpec((tm, bn), lambda j, i: (i, j))),
        out_shape=(jax.ShapeDtypeStruct((2, s, f), F32), jax.ShapeDtypeStruct((s, f), BF16)),
        scratch_shapes=[pltpu.VMEM((2, SUBLANES, bn), F32)],
        compiler_params=_cparams(("parallel", "arbitrary"), 56),
    )(*ins)


def _down_dx_conv_gate_bwd(up, cw, cb, dy, wdown, after):
    _, s, f = up.shape
    d = dy.shape[1]
    tn = _pick(f, (256, 128))

    def body(x_ref, w_ref, b_ref, dy_ref, wd_ref, _, dx_ref, dw_ref, db_ref):
        da = lax.dot_general(dy_ref[...], wd_ref[...], NT_DIMS, preferred_element_type=F32)
        xg, xv = x_ref[0], x_ref[1]
        wg, wv = w_ref[0], w_ref[1]
        gate, xg1, xg2 = _conv(xg, wg, b_ref[0])
        val, xv1, xv2 = _conv(xv, wv, b_ref[1])
        sig = jax.nn.sigmoid(gate)
        dval = da * (gate * sig)
        dgate = da * val * (sig * (1.0 + gate * (1.0 - sig)))
        for half, (x, x1, x2, w, dz) in enumerate(((xg, xg1, xg2, wg, dgate), (xv, xv1, xv2, wv, dval))):
            dx_ref[half] = (w[2:3, :] * dz + w[1:2, :] * _shift_up(dz, 1) + w[0:1, :] * _shift_up(dz, 2)).astype(BF16)
            dw_ref[half, 0:1, :] = jnp.sum(dz * x2, axis=0, keepdims=True)
            dw_ref[half, 1:2, :] = jnp.sum(dz * x1, axis=0, keepdims=True)
            dw_ref[half, 2:3, :] = jnp.sum(dz * x, axis=0, keepdims=True)
            db_ref[half] = jnp.sum(dz, axis=0, keepdims=True)

    xspec, wspec, bspec = _conv_specs(s, tn)
    return pl.pallas_call(
        body, name="down_dx_conv_gate_bwd", grid=(f // tn,),
        in_specs=[xspec, wspec, bspec, pl.BlockSpec((s, d), lambda j: (0, 0)), pl.BlockSpec((tn, d), lambda j: (j, 0)),
                  pl.BlockSpec(after.shape, lambda j: (0, 0))],
        out_specs=(xspec, wspec, bspec),
        out_shape=(jax.ShapeDtypeStruct((2, s, f), BF16), jax.ShapeDtypeStruct((2, CONV_WIDTH, f), F32),
                   jax.ShapeDtypeStruct((2, 1, f), F32)),
        compiler_params=_cparams(("parallel",), 56),
    )(up, cw, cb, dy, wdown, after)


def _mesh_pos():
    return lax.axis_index("x"), lax.axis_index("y"), lax.axis_index("c")


def _remote(src, dst, send_sem, recv_sem, to):
    return pltpu.make_async_remote_copy(src_ref=src, dst_ref=dst, send_sem=send_sem, recv_sem=recv_sem,
                                        device_id=to, device_id_type=pl.DeviceIdType.MESH)


HBM_SPEC = pl.BlockSpec(memory_space=pltpu.HBM)
SEM_SPEC = pl.BlockSpec(memory_space=pltpu.SEMAPHORE)
ANY_SPEC = pl.BlockSpec(memory_space=pl.ANY)
TOKEN_SPEC = pl.BlockSpec(memory_space=pltpu.VMEM)
TOKEN_SHAPE = jax.ShapeDtypeStruct((8, 128), F32)
DATAFLOW = pltpu.SideEffectType.DATAFLOW_SIDE_EFFECTING
GATHER_PLANE = (2, 4, 6)


def _slot(pos):
    return 4 * pos[0] + 2 * pos[1] + pos[2]


def _flip(pos, k):
    return (pos[0] ^ ((k >> 2) & 1), pos[1] ^ ((k >> 1) & 1), pos[2] ^ (k & 1))


def _hbm(a):
    return pltpu.with_memory_space_constraint(a, pltpu.HBM)


def _hbm_shapes(arrays):
    return tuple(pltpu.HBM(a.shape, a.dtype) for a in arrays)


class _Split:
    def __init__(self, n, outs, n_sets):
        k = 2 * n * int(n_sets)
        self.n = n
        self.sems = list(outs[:k])
        self.bufs = list(outs[k:k + 2 * n])
        self.token = outs[-1]

    def sem_set(self, i):
        return self.sems[2 * self.n * i:2 * self.n * (i + 1)]


def _split_call(name, body, bufs, sems_in, n_sets, after):
    n = len(bufs) // 2
    k = 2 * n * int(n_sets)
    m = len(sems_in)
    afters = list(after) if isinstance(after, (list, tuple)) else [after]
    na = len(afters)

    def wrapped(*refs):
        srcs, dsts = refs[:n], refs[n:2 * n]
        s_in = refs[2 * n:2 * n + m]
        s_out = refs[2 * n + m + na:2 * n + m + na + k]
        token, local_sems = refs[-2], refs[-1]
        body(srcs, dsts, s_in, s_out, local_sems)
        token[...] = jnp.zeros_like(token)

    outs = pl.pallas_call(
        wrapped, name=name,
        out_shape=(pltpu.SemaphoreType.DMA(()),) * k + _hbm_shapes(bufs) + (TOKEN_SHAPE,),
        in_specs=[HBM_SPEC] * (2 * n) + [SEM_SPEC] * m + [ANY_SPEC] * na,
        out_specs=(SEM_SPEC,) * k + (HBM_SPEC,) * (2 * n) + (TOKEN_SPEC,),
        input_output_aliases={i: k + i for i in range(2 * n)},
        scratch_shapes=[pltpu.SemaphoreType.DMA((n,))],
        compiler_params=pltpu.CompilerParams(has_side_effects=DATAFLOW),
    )(*[_hbm(b) for b in bufs], *sems_in, *afters)
    return _Split(n, outs, n_sets)


def _wait_slots(land, count, send_sem, recv_sem, me, send=False, recv=False):
    span = land.at[pl.ds(0, count)]
    cp = _remote(span, span, send_sem, recv_sem, me)
    if send:
        cp.wait_send()
    if recv:
        cp.wait_recv()


X_FLIP, Y_FLIP, DIAG_FLIP = 4, 2, 6
BF16_SUBLANES = 16


def _gather_start(name, shards, after):
    n = len(shards)
    my_slot = _slot(_mesh_pos())
    lands = [lax.dynamic_update_slice(lax.empty((N_DEV,) + w.shape, w.dtype), w[None], (my_slot, 0, 0)) for w in shards]

    def body(srcs, dsts, _, sems, local_sems):
        me = _mesh_pos()
        for a in range(n):
            for k in (1, X_FLIP, Y_FLIP):
                _remote(srcs[a], dsts[a].at[_slot(me)], sems[a], sems[n + a], _flip(me, k)).start()

    return _split_call(name, body, list(shards) + lands, [], 1, after)


def _gather_relay(name, started, after):
    n = started.n

    def body(srcs, dsts, sems_a, sems_out, local_sems):
        me = _mesh_pos()
        sibling = _flip(me, 1)
        pass_on, relay = sems_out[:2 * n], sems_out[2 * n:]
        for a in range(n):
            _wait_slots(dsts[a], 3, sems_a[a], sems_a[n + a], me, recv=True)
            from_x = dsts[a].at[_slot(_flip(me, X_FLIP))]
            from_y = dsts[a].at[_slot(_flip(me, Y_FLIP))]
            rows = srcs[a].shape[0]
            if rows % (2 * BF16_SUBLANES) == 0:
                top, bottom = pl.ds(0, rows // 2), pl.ds(rows // 2, rows // 2)
                _remote(from_y.at[top], from_y.at[top], relay[a], relay[n + a], _flip(me, X_FLIP)).start()
                _remote(from_x.at[bottom], from_x.at[bottom], relay[a], relay[n + a], _flip(me, Y_FLIP)).start()
            else:
                _remote(from_y, from_y, relay[a], relay[n + a], _flip(me, X_FLIP)).start()
            for block in (from_x, from_y):
                _remote(block, block, pass_on[a], pass_on[n + a], sibling).start()
        for a in range(n):
            _wait_slots(dsts[a], 3, sems_a[a], sems_a[n + a], me, send=True)

    return _split_call(name, body, started.bufs, started.sems, 2, after)


def _gather_relay_diagonal(name, relayed, after):
    n = relayed.n

    def body(srcs, dsts, relay, pass_on, local_sems):
        me = _mesh_pos()
        for a in range(n):
            _wait_slots(dsts[a], 1, relay[a], relay[n + a], me, recv=True)
            block = dsts[a].at[_slot(_flip(me, DIAG_FLIP))]
            _remote(block, block, pass_on[a], pass_on[n + a], _flip(me, 1)).start()
        for a in range(n):
            _wait_slots(dsts[a], 1, relay[a], relay[n + a], me, send=True)

    return _split_call(name, body, relayed.bufs, relayed.sem_set(1), 1, after)


def _gather_finish(name, relayed, diagonal, after):
    n = relayed.n

    def body(srcs, dsts, sems, _, local_sems):
        me = _mesh_pos()
        first, second = sems[:2 * n], sems[2 * n:]
        for a in range(n):
            _wait_slots(dsts[a], 2, first[a], first[n + a], me, send=True, recv=True)
            _wait_slots(dsts[a], 1, second[a], second[n + a], me, send=True, recv=True)

    return _split_call(name, body, diagonal.bufs, relayed.sem_set(0) + diagonal.sems, 0, after).bufs[n:]


def _exchange_start(name, blocked, after):
    n = len(blocked)
    my_slot = _slot(_mesh_pos())
    rows = [w.shape[-2] // (N_DEV if w.ndim == 2 else 1) for w in blocked]

    def block(ref, a, slot):
        if len(ref.shape) == 3:
            return ref.at[slot]
        return ref.at[pl.ds(pl.multiple_of(slot * rows[a], 16), rows[a])]

    lands = []
    for w, r in zip(blocked, rows):
        mine = lax.dynamic_slice_in_dim(w, my_slot, 1, 0) if w.ndim == 3 else lax.dynamic_slice_in_dim(w, my_slot * r, r, 0)[None]
        lands.append(lax.dynamic_update_slice(lax.empty((N_DEV, r, w.shape[-1]), w.dtype), mine, (my_slot, 0, 0)))

    def body(srcs, dsts, _, sems, local_sems):
        me = _mesh_pos()
        for a in range(n):
            for k in range(1, N_DEV):
                peer = _flip(me, k)
                _remote(block(srcs[a], a, _slot(peer)), dsts[a].at[_slot(me)], sems[a], sems[n + a], peer).start()

    return _split_call(name, body, list(blocked) + lands, [], True, after)


def _exchange_finish(name, started, after):
    group = started if isinstance(started, (list, tuple)) else [started]
    srcs_all = [b for st in group for b in st.bufs[:st.n]]
    lands_all = [b for st in group for b in st.bufs[st.n:]]
    sends = [s for st in group for s in st.sems[:st.n]]
    recvs = [s for st in group for s in st.sems[st.n:]]
    n = len(srcs_all)

    def body(srcs, dsts, sems, _, local_sems):
        me = _mesh_pos()
        for a in range(n):
            _wait_slots(dsts[a], N_DEV - 1, sems[a], sems[n + a], me, send=True, recv=True)

    return _split_call(name, body, srcs_all + lands_all, sends + recvs, 0, after).bufs[n:]


def _broadcast_start(name, arrays, after):
    n = len(arrays)
    my_slot = _slot(_mesh_pos())
    lands = [lax.dynamic_update_slice(lax.empty((N_DEV,) + w.shape, w.dtype), w[None], (my_slot, 0, 0)) for w in arrays]

    def body(srcs, dsts, _, sems, local_sems):
        me = _mesh_pos()
        for a in range(n):
            for k in range(1, N_DEV):
                _remote(srcs[a], dsts[a].at[_slot(me)], sems[a], sems[n + a], _flip(me, k)).start()

    return _split_call(name, body, list(arrays) + lands, [], True, after)


def _adamw_math(w, g, m, v):
    m = ADAM_B1 * m + (1.0 - ADAM_B1) * g
    v = ADAM_B2 * v + (1.0 - ADAM_B2) * (g * g)
    m_hat = m / (1.0 - ADAM_B1 ** ADAM_STEP)
    v_hat = v / (1.0 - ADAM_B2 ** ADAM_STEP)
    delta = -ADAM_LR * (m_hat / (jnp.sqrt(v_hat) + ADAM_EPS) + ADAM_WD * w)
    return delta, m, v


def _adamw(name, w, m, v, parts, layer, prev=None):
    _, r, c = w.shape
    tr = max(t for t in range(16, r + 1, 16) if r % t == 0 and t * c <= ADAMW_TILE_ELEMS)
    n_prev = 0 if prev is None else 4

    def body(*refs):
        w_ref, m_ref, v_ref, p_ref = refs[:4]
        g_ref, d_ref, nm_ref, nv_ref = refs[4 + n_prev:]
        g = p_ref[0].astype(F32)
        for src in range(1, N_DEV):
            g = g + p_ref[src].astype(F32)
        delta, nm, nv = _adamw_math(w_ref[...], g, m_ref[...], v_ref[...])
        g_ref[...] = g
        d_ref[...] = delta
        nm_ref[...] = nm
        nv_ref[...] = nv

    wspec = pl.BlockSpec((None, tr, c), lambda i: (layer, i, 0))
    pspec = pl.BlockSpec((N_DEV, tr, c), lambda i: (0, i, 0))
    shp = jax.ShapeDtypeStruct(w.shape, F32)
    return pl.pallas_call(
        body, name=name, grid=(r // tr,), in_specs=[wspec] * 3 + [pspec] + [ANY_SPEC] * n_prev,
        out_specs=(wspec,) * 4, out_shape=(shp,) * 4, input_output_aliases={4 + j: j for j in range(n_prev)},
        compiler_params=_cparams(("parallel",), 56),
    )(w, m, v, parts, *([] if prev is None else prev))


PACK_TILE = 8 * 128


def _pack(arrays):
    flat = []
    for a in arrays:
        v = a.reshape(-1)
        pad = (-v.shape[0]) % PACK_TILE
        flat.append(jnp.pad(v, (0, pad)) if pad else v)
    return jnp.concatenate(flat).reshape(-1, 128)


def _unpack(buf, like):
    flat = buf.reshape(-1)
    out, off = [], 0
    for a in like:
        n = 1
        for dim in a.shape:
            n *= dim
        out.append(flat[off:off + n].reshape(a.shape))
        off += n + (-n) % PACK_TILE
    return out


def _sum_slots(name, gathered):
    _, r, c = gathered.shape

    def body(x_ref, o_ref):
        acc = x_ref[0].astype(F32)
        for src in range(1, N_DEV):
            acc = acc + x_ref[src].astype(F32)
        o_ref[...] = acc

    return pl.pallas_call(body, name=name, out_shape=jax.ShapeDtypeStruct((r, c), F32))(gathered)


def _adamw_small(w, g, m, v):
    shp = jax.ShapeDtypeStruct(w.shape, F32)

    def body(w_ref, g_ref, m_ref, v_ref, d_ref, nm_ref, nv_ref):
        delta, nm, nv = _adamw_math(w_ref[...], g_ref[...], m_ref[...], v_ref[...])
        d_ref[...] = delta
        nm_ref[...] = nm
        nv_ref[...] = nv

    return pl.pallas_call(body, name="adamw_small", out_shape=(shp,) * 3)(w, g, m, v)


def kernel(x, attn_norm_g, w_in, q_norm_g, k_norm_g, sgu_norm_g, sgu_w, sgu_b, out_norm_a_g, out_norm_b_g, w_out, ffn_norm_g, w_up, conv_w, conv_b, w_down, loss_target, m_attn_norm_g, m_w_in, m_q_norm_g, m_k_norm_g, m_sgu_norm_g, m_sgu_w, m_sgu_b, m_out_norm_a_g, m_out_norm_b_g, m_w_out, m_ffn_norm_g, m_w_up, m_conv_w, m_conv_b, m_w_down, v_attn_norm_g, v_w_in, v_q_norm_g, v_k_norm_g, v_sgu_norm_g, v_sgu_w, v_sgu_b, v_out_norm_a_g, v_out_norm_b_g, v_w_out, v_ffn_norm_g, v_w_up, v_conv_w, v_conv_b, v_w_down):
    depth = w_in.shape[0]
    s, d = x.shape[1], x.shape[2]
    n_heads = (d // 2) // HEAD_DIM
    sgu_col0 = 3 * n_heads
    f2 = w_up.shape[2] * N_DEV
    ff = f2 // 2
    my_slot = 4 * lax.axis_index("x") + 2 * lax.axis_index("y") + lax.axis_index("c")

    wb = [(w_in[l].astype(BF16), w_out[l].astype(BF16), w_up[l].astype(BF16), w_down[l].astype(BF16))
          for l in range(depth)]
    groups = {"in0": [wb[0][0]], "out0": [wb[0][1], conv_w.reshape(depth * CONV_WIDTH, -1)], "up0": [wb[0][2]],
              "down0": [wb[0][3]]}
    for l in range(1, depth):
        groups[f"in{l}"] = [wb[l][0], wb[l][1]]
        groups[f"ffn{l}"] = [wb[l][2], wb[l][3]]
    order = list(groups)
    started, relayed = {}, {}

    def start(gname, after):
        started[gname] = _gather_start(f"gather_{gname}_start", groups[gname], after)
        return started[gname].token

    def relay(gname, after):
        relayed[gname] = _gather_relay(f"gather_{gname}_relay", started[gname], after)
        token = relayed[gname].token
        k = order.index(gname)
        nxt = [k + 2] if k + 2 < len(order) - 1 else []
        if k == len(order) - 2:
            nxt = [k + 1]
        for j in nxt:
            token = start(order[j], token)
        return token

    def finish(gname, after):
        diagonal = _gather_relay_diagonal(f"gather_{gname}_diagonal", relayed[gname], after)
        return _gather_finish(f"gather_{gname}_finish", relayed[gname], diagonal, diagonal.token)

    conv_b_all = conv_b.reshape(depth, 2, 1, ff)
    sgu_b_col = sgu_b[..., None]
    token = start(order[1], start(order[0], attn_norm_g))
    token = relay("out0", relay("in0", token))
    win_g = finish("in0", token)[0]

    xs = x[0]
    saved = []
    gathered = []
    for l in range(depth):
        g1 = attn_norm_g[l][None]
        g2 = ffn_norm_g[l][None]
        gq, gk = q_norm_g[l][None], k_norm_g[l][None]
        ga = out_norm_a_g[l][:, None, :]
        gs = sgu_norm_g[l][:, None, :]
        gb = out_norm_b_g[l][:, None, :]
        h1 = _rmsnorm_fwd("attn_norm_fwd", xs, g1)
        p = _mm_nn_blocked("in_proj", h1, win_g, F32)
        att, o, rsum = _attn_fwd(p, gq, gk, ga, n_heads)
        token = relay("up0" if l == 0 else f"ffn{l}", att)
        sg = _sgu_fwd(p, gs, sgu_w[l], sgu_b_col[l], gb, sgu_col0, token)
        mix = jnp.concatenate([att, sg], axis=-1)
        if l == 0:
            wout_g, cw = finish("out0", mix)
            cw = jnp.transpose(cw.reshape(N_DEV, depth, CONV_WIDTH, -1), (1, 2, 0, 3)).reshape(depth, CONV_WIDTH, 2, ff)
            conv_w_all = jnp.transpose(cw, (0, 2, 1, 3))
        x1 = _mm_nn_res("out_proj", mix, wout_g.reshape(d, d), xs)
        h2 = _rmsnorm_fwd("ffn_norm_fwd", x1, g2)
        if l == 0:
            wup_g = finish("up0", h2)[0]
            token = relay("down0", wup_g)
            up, act = _up_conv_gate_fwd(h2, wup_g, conv_w_all[l], conv_b_all[l], after=token)
            wdown_g = finish("down0", up)[0]
        else:
            wup_g, wdown_g = finish(f"ffn{l}", h2)
            up, act = _up_conv_gate_fwd(h2, wup_g, conv_w_all[l], conv_b_all[l])
        saved.append((xs, h1, p, o, rsum, mix, x1, h2, up, act))
        gathered.append((win_g, wout_g, wup_g, wdown_g))
        if l + 1 < depth:
            token = relay(f"in{l + 1}", act)
            x2 = _mm_nn_res("down_proj", act, wdown_g.reshape(ff, d), x1, after=token)
            win_g, wout_g = finish(f"in{l + 1}", x2)
        else:
            x2 = _mm_nn_res("down_proj", act, wdown_g.reshape(ff, d), x1)
        xs = x2

    dx, dxb, loss_vec = _loss_head(xs, loss_target[0])
    loss = lax.psum(loss_vec[0, 0], MESH_AXES)

    exchanges = []
    small = [None] * depth
    small_names = ["attn_norm_g", "q_norm_g", "k_norm_g", "sgu_norm_g", "sgu_w", "sgu_b", "out_norm_a_g",
                   "out_norm_b_g", "ffn_norm_g", "conv_b", "conv_w"]
    for l in reversed(range(depth)):
        xs0, h1, p, o, rsum, mix, x1, h2, up, act = saved[l]
        win_g, wout_g, wup_g, wdown_g = gathered[l]
        wout_full = wout_g.reshape(d, d)
        wdown_full = wdown_g.reshape(ff, d)
        g1 = attn_norm_g[l][None]
        g2 = ffn_norm_g[l][None]
        gq, gk = q_norm_g[l][None], k_norm_g[l][None]
        ga = out_norm_a_g[l][:, None, :]
        gs = sgu_norm_g[l][:, None, :]
        gb = out_norm_b_g[l][:, None, :]
        d_wdown = _mm_tn_plain("down_proj_dw", act, dxb)
        exchanges.append((l, "down", ("w_down",), _exchange_start(f"grad_down{l}_start", [d_wdown], dx)))
        dup, d_cw, d_cb = _down_dx_conv_gate_bwd(up, conv_w_all[l], conv_b_all[l], dxb, wdown_full,
                                                 exchanges[-1][3].token)
        d_wup = _mm_tn_blocked("up_proj_dw", h2, dup, N_DEV, halves=True)
        exchanges.append((l, "up", ("w_up",), _exchange_start(f"grad_up{l}_start", [d_wup], d_cb)))
        dh2 = _mm_nt_blocked("up_proj_dx", dup, wup_g, halves=True, after=exchanges[-1][3].token)
        dx, dxb, d_g2 = _rmsnorm_bwd("ffn_norm_bwd", dh2, x1, g2, dx)
        d_wout = _mm_tn_plain("out_proj_dw", mix, dxb)
        exchanges.append((l, "out", ("w_out",), _exchange_start(f"grad_out{l}_start", [d_wout], d_g2)))
        dmix = _mm_nt_plain("out_proj_dx", dxb, wout_full, after=exchanges[-1][3].token)
        dq, dk, dv, d_gq, d_gk, d_ga = _attn_bwd(p, gq, gk, ga, o, rsum, dmix, n_heads)
        du, dvs, d_gs, d_sw, d_sb, d_gb = _sgu_bwd(p, gs, sgu_w[l], sgu_b_col[l], gb, dmix, sgu_col0, n_heads)
        dp = jnp.concatenate([dq, dk, dv, du, dvs], axis=-1)
        d_win = _mm_tn_blocked("in_proj_dw", h1, dp, N_DEV)
        exchanges.append((l, "in", ("w_in",), _exchange_start(f"grad_in{l}_start", [d_win], d_gq)))
        dh1 = _mm_nt_blocked("in_proj_dx", dp, win_g, after=exchanges[-1][3].token)
        dx, dxb, d_g1 = _rmsnorm_bwd("attn_norm_bwd", dh1, xs0, g1, dx)
        small[l] = dict(attn_norm_g=d_g1[0], q_norm_g=d_gq[0], k_norm_g=d_gk[0], sgu_norm_g=d_gs[:, 0], sgu_w=d_sw,
                        sgu_b=d_sb[..., 0], out_norm_a_g=d_ga[:, 0], out_norm_b_g=d_gb[:, 0], ffn_norm_g=d_g2[0],
                        conv_w=jnp.transpose(d_cw, (1, 0, 2)).reshape(CONV_WIDTH, f2), conv_b=d_cb.reshape(f2))
    grad_x = dx[None]

    f32_names = [n for n in small_names if n != "sgu_w"]
    small_g = [jnp.stack([small[l][n] for l in range(depth)]) for n in f32_names]
    sgu_w_g = jnp.stack([small[l]["sgu_w"] for l in range(depth)])
    small_sent = _broadcast_start("grad_small_start", [_pack(small_g), sgu_w_g.reshape(-1, TILE).astype(BF16)], dx)

    res = {}
    big = dict(w_in=(w_in, m_w_in, v_w_in), w_out=(w_out, m_w_out, v_w_out), w_up=(w_up, m_w_up, v_w_up),
               w_down=(w_down, m_w_down, v_w_down))
    after = [small_sent.token]
    batches = [[e for e in exchanges if e[0] == l] for l in reversed(range(depth))]
    batches = batches[:-1] + [batches[-1][:-1], batches[-1][-1:]]
    for i, batch in enumerate(batches):
        landed = _exchange_finish(f"grad_batch{i}_finish", [ex for _, _, _, ex in batch], after)
        after = []
        for (layer, name), parts in zip([(k, n) for k, _, names, _ in batch for n in names], landed):
            w, m, v = big[name]
            res[name] = _adamw(f"adamw_{name}", w, m, v, parts, layer, res.get(name))
            after.append(res[name][0])
    small_all, sgu_w_all = _exchange_finish("grad_small_finish", small_sent, after)
    small_sum = _unpack(_sum_slots("small_grad_sum", small_all), small_g)
    g_small = dict(zip(f32_names, small_sum))
    g_small["sgu_w"] = _sum_slots("sgu_w_grad_sum", sgu_w_all).reshape(sgu_w.shape)
    cwn = conv_w.shape[2]
    g_small["conv_w"] = lax.dynamic_slice_in_dim(g_small["conv_w"], my_slot * cwn, cwn, axis=2)
    small_w = dict(attn_norm_g=(attn_norm_g, m_attn_norm_g, v_attn_norm_g), q_norm_g=(q_norm_g, m_q_norm_g, v_q_norm_g),
                   k_norm_g=(k_norm_g, m_k_norm_g, v_k_norm_g), sgu_norm_g=(sgu_norm_g, m_sgu_norm_g, v_sgu_norm_g),
                   sgu_w=(sgu_w, m_sgu_w, v_sgu_w), sgu_b=(sgu_b, m_sgu_b, v_sgu_b),
                   out_norm_a_g=(out_norm_a_g, m_out_norm_a_g, v_out_norm_a_g),
                   out_norm_b_g=(out_norm_b_g, m_out_norm_b_g, v_out_norm_b_g),
                   ffn_norm_g=(ffn_norm_g, m_ffn_norm_g, v_ffn_norm_g), conv_b=(conv_b, m_conv_b, v_conv_b),
                   conv_w=(conv_w, m_conv_w, v_conv_w))
    like = [small_w[n][0] for n in small_names]
    pw = _pack([small_w[n][0] for n in small_names])
    pm = _pack([small_w[n][1] for n in small_names])
    pv = _pack([small_w[n][2] for n in small_names])
    pg = _pack([g_small[n].reshape(small_w[n][0].shape) for n in small_names])
    pd, pnm, pnv = _adamw_small(pw, pg, pm, pv)
    for n, dlt, nm, nv in zip(small_names, _unpack(pd, like), _unpack(pnm, like), _unpack(pnv, like)):
        res[n] = (g_small[n].reshape(small_w[n][0].shape), dlt, nm, nv)

    order = ["attn_norm_g", "w_in", "q_norm_g", "k_norm_g", "sgu_norm_g", "sgu_w", "sgu_b", "out_norm_a_g",
             "out_norm_b_g", "w_out", "ffn_norm_g", "w_up", "conv_w", "conv_b", "w_down"]
    outs = [loss, grad_x]
    for field in range(4):
        outs += [res[n][field] for n in order]
    return tuple(outs)
```

```python
import functools

import jax
import jax.numpy as jnp
from jax import lax
from jax.experimental import pallas as pl
from jax.experimental.pallas import tpu as pltpu

F32 = jnp.float32
BF16 = jnp.bfloat16
EPS = 1e-6
HEAD_DIM = 128
TILE = 128
ATTN_VMEM_MB = 58
ATTN_HEADS_PER_STEP = 4
N_GROUPS = 8
CONV_WIDTH = 3
N_DEV = 8
MESH_AXES = ("x", "y", "c")
MIB = 1024 * 1024

ADAM_LR = 0.001
ADAM_B1 = 0.9
ADAM_B2 = 0.999
ADAM_EPS = 1e-08
ADAM_WD = 0.01
ADAM_STEP = 10
ADAMW_TILE_ELEMS = 512 * 1024

NT_DIMS = (((1,), (1,)), ((), ()))
NN_DIMS = (((1,), (0,)), ((), ()))
TN_DIMS = (((0,), (0,)), ((), ()))


def _cparams(sem, vmem_mb=48):
    return pltpu.CompilerParams(dimension_semantics=sem, vmem_limit_bytes=vmem_mb * MIB)


def _pick(n, cands):
    for c in cands:
        if n % c == 0:
            return c
    return n


MXU_WIDTH = 256


def _pairs(nb, bn):
    return nb % 2 == 0 and bn % MXU_WIDTH != 0 and (2 * bn) % MXU_WIDTH == 0


def _mm(name, grid, ins, in_specs, out_shape, out_spec, dims, has_res=False, parts=None, vmem_mb=56, after=None,
        split_out=None):
    n_in = 2 + has_res + (after is not None)
    if after is not None:
        ins = tuple(ins) + (after,)
        in_specs = list(in_specs) + [pl.BlockSpec(after.shape, lambda *_: (0, 0))]

    def body(*refs):
        a_ref, b_ref = refs[:2]
        o_ref = refs[n_in]
        if parts is None:
            acc = lax.dot_general(a_ref[...], b_ref[...], dims, preferred_element_type=F32)
        else:
            acc = None
            for part in parts:
                a, b = part(a_ref, b_ref)
                prod = lax.dot_general(a, b, dims, preferred_element_type=F32)
                acc = prod if acc is None else acc + prod
        if has_res:
            acc = acc + refs[2][...]
        if split_out is None:
            o_ref[...] = acc.astype(o_ref.dtype)
        else:
            o_ref[0] = acc[:, :split_out].astype(o_ref.dtype)
            o_ref[1] = acc[:, split_out:].astype(o_ref.dtype)

    return pl.pallas_call(
        body, name=name, grid=grid, in_specs=in_specs, out_specs=out_spec, out_shape=out_shape,
        compiler_params=_cparams(("parallel",) * len(grid), vmem_mb),
    )(*ins)


def _two_blocks(b_ref, first):
    return jnp.concatenate([b_ref[first], b_ref[first + 1]], axis=1)


def _rows_for(m, row_bytes, budget):
    return _pick(m, tuple(t for t in (2048, 1024, 512, 256, 128) if t * row_bytes <= budget))


def _mm_nn_blocked(name, a, wb, out_dtype, halves=False, after=None):
    m, k = a.shape
    nb, _, bn = wb.shape
    if _pairs(nb, bn) and not halves:
        tm = _rows_for(m, 2 * bn * jnp.dtype(out_dtype).itemsize, 6 * MIB)
        return _mm(name, (nb // 2, m // tm), (a, wb),
                   [pl.BlockSpec((tm, k), lambda j, i: (i, 0)), pl.BlockSpec((2, k, bn), lambda j, i: (j, 0, 0))],
                   jax.ShapeDtypeStruct((m, nb * bn), out_dtype), pl.BlockSpec((tm, 2 * bn), lambda j, i: (i, j)), NN_DIMS,
                   parts=[lambda a_ref, b_ref: (a_ref[...], _two_blocks(b_ref, 0))], after=after)
    tm = _rows_for(m, bn * jnp.dtype(out_dtype).itemsize, 6 * MIB)
    a_spec = pl.BlockSpec((tm, k), lambda j, i: (i, 0))
    b_spec = pl.BlockSpec((None, k, bn), lambda j, i: (j, 0, 0))
    if halves:
        hb = nb // 2
        out_shape = jax.ShapeDtypeStruct((2, m, hb * bn), out_dtype)
        o_spec = pl.BlockSpec((None, tm, bn), lambda j, i: (j // hb, i, j % hb))
    else:
        out_shape = jax.ShapeDtypeStruct((m, nb * bn), out_dtype)
        o_spec = pl.BlockSpec((tm, bn), lambda j, i: (i, j))
    return _mm(name, (nb, m // tm), (a, wb), [a_spec, b_spec], out_shape, o_spec, NN_DIMS, after=after)


def _mm_nn_res(name, a, w, res, after=None):
    m, k = a.shape
    n = w.shape[1]
    tm = _pick(m, (512, 256, 128))
    tn = _rows_for(n, k * 2, 12 * MIB)
    a_spec = pl.BlockSpec((tm, k), lambda j, i: (i, 0))
    b_spec = pl.BlockSpec((k, tn), lambda j, i: (0, j))
    r_spec = pl.BlockSpec((tm, tn), lambda j, i: (i, j))
    o_spec = pl.BlockSpec((tm, tn), lambda j, i: (i, j))
    return _mm(name, (n // tn, m // tm), (a, w, res), [a_spec, b_spec, r_spec], jax.ShapeDtypeStruct((m, n), F32),
               o_spec, NN_DIMS, has_res=True, after=after)


def _mm_nt_blocked(name, dy, wb, halves=False, after=None):
    nb, n, bn = wb.shape
    m = dy.shape[-2]
    tm = _pick(m, (512, 256, 128))
    tn = _rows_for(n, nb * bn * 2, 12 * MIB)
    if halves:
        hb = nb // 2
        a_spec = pl.BlockSpec((2, tm, hb * bn), lambda j, i: (0, i, 0))
        a_part = lambda kk: (lambda a_ref: a_ref[kk // hb, :, (kk % hb) * bn:(kk % hb + 1) * bn])
    else:
        a_spec = pl.BlockSpec((tm, nb * bn), lambda j, i: (i, 0))
        a_part = lambda kk: (lambda a_ref: a_ref[:, kk * bn:(kk + 1) * bn])
    if _pairs(nb, bn) and not halves:
        parts = [(lambda a_ref, b_ref, kk=kk: (a_ref[:, kk * bn:(kk + 2) * bn], _two_blocks(b_ref, kk)))
                 for kk in range(0, nb, 2)]
    elif _pairs(nb // 2, bn) and halves:
        parts = [(lambda a_ref, b_ref, kk=kk: (a_ref[kk // hb, :, (kk % hb) * bn:(kk % hb + 2) * bn],
                                               _two_blocks(b_ref, kk))) for kk in range(0, nb, 2)]
    else:
        parts = [(lambda a_ref, b_ref, kk=kk, sel=a_part(kk): (sel(a_ref), b_ref[kk])) for kk in range(nb)]
    b_spec = pl.BlockSpec((nb, tn, bn), lambda j, i: (0, j, 0))
    o_spec = pl.BlockSpec((tm, tn), lambda j, i: (i, j))
    return _mm(name, (n // tn, m // tm), (dy, wb), [a_spec, b_spec], jax.ShapeDtypeStruct((m, n), F32), o_spec,
               NT_DIMS, parts=parts, after=after)


def _mm_nt_plain(name, dy, w, out_dtype=F32, after=None):
    m, k = dy.shape
    n = w.shape[0]
    tm = _rows_for(m, k * 2, 8 * MIB)
    tn = _pick(n, (512, 256, 128))
    a_spec = pl.BlockSpec((tm, k), lambda j, i: (i, 0))
    b_spec = pl.BlockSpec((tn, k), lambda j, i: (j, 0))
    o_spec = pl.BlockSpec((tm, tn), lambda j, i: (i, j))
    return _mm(name, (n // tn, m // tm), (dy, w), [a_spec, b_spec], jax.ShapeDtypeStruct((m, n), out_dtype), o_spec,
               NT_DIMS, after=after)


def _mm_tn_blocked(name, a, dy, nb, halves=False):
    s, k1 = a.shape
    bn = (dy.shape[-1] * (2 if halves else 1)) // nb
    if _pairs(nb, bn) and not halves:
        tm = _rows_for(k1, 2 * bn * 2, 6 * MIB)
        return _mm(name, (nb // 2, k1 // tm), (a, dy),
                   [pl.BlockSpec((s, tm), lambda j, i: (0, i)), pl.BlockSpec((s, 2 * bn), lambda j, i: (0, j))],
                   jax.ShapeDtypeStruct((nb, k1, bn), BF16), pl.BlockSpec((2, tm, bn), lambda j, i: (j, i, 0)), TN_DIMS,
                   split_out=bn)
    if halves and _pairs(nb // 2, bn):
        tm = _rows_for(k1, 2 * bn * 2, 6 * MIB)
        per_half = nb // 4
        return _mm(name, (nb // 2, k1 // tm), (a, dy),
                   [pl.BlockSpec((s, tm), lambda j, i: (0, i)),
                    pl.BlockSpec((None, s, 2 * bn), lambda j, i: (j // per_half, 0, j % per_half))],
                   jax.ShapeDtypeStruct((nb, k1, bn), BF16), pl.BlockSpec((2, tm, bn), lambda j, i: (j, i, 0)), TN_DIMS,
                   split_out=bn)
    tm = _rows_for(k1, bn * 2, 6 * MIB)
    a_spec = pl.BlockSpec((s, tm), lambda j, i: (0, i))
    if halves:
        hb = nb // 2
        b_spec = pl.BlockSpec((None, s, bn), lambda j, i: (j // hb, 0, j % hb))
    else:
        b_spec = pl.BlockSpec((s, bn), lambda j, i: (0, j))
    o_spec = pl.BlockSpec((None, tm, bn), lambda j, i: (j, i, 0))
    return _mm(name, (nb, k1 // tm), (a, dy), [a_spec, b_spec], jax.ShapeDtypeStruct((nb, k1, bn), BF16), o_spec,
               TN_DIMS)


def _mm_tn_plain(name, a, dy):
    s, k1 = a.shape
    n = dy.shape[1]
    tm = _pick(k1, (512, 256, 128))
    tn = _rows_for(n, s * 2, 8 * MIB)
    a_spec = pl.BlockSpec((s, tm), lambda i, j: (0, i))
    b_spec = pl.BlockSpec((s, tn), lambda i, j: (0, j))
    o_spec = pl.BlockSpec((tm, tn), lambda i, j: (i, j))
    return _mm(name, (k1 // tm, n // tn), (a, dy), [a_spec, b_spec], jax.ShapeDtypeStruct((k1, n), BF16), o_spec,
               TN_DIMS)


def _rstd(x):
    return lax.rsqrt(jnp.mean(x * x, axis=-1, keepdims=True) + EPS)


def _norm_bwd(dy, xhat, r, g):
    dxhat = dy * g
    return r * (dxhat - xhat * jnp.mean(dxhat * xhat, axis=-1, keepdims=True))


def _rmsnorm_fwd(name, x, g):
    s, d = x.shape
    tr = _pick(s, (256, 128))

    def body(x_ref, g_ref, h_ref):
        xv = x_ref[...]
        h_ref[...] = (xv * _rstd(xv) * g_ref[...]).astype(BF16)

    return pl.pallas_call(
        body, name=name, grid=(s // tr,),
        in_specs=[pl.BlockSpec((tr, d), lambda i: (i, 0)), pl.BlockSpec((1, d), lambda i: (0, 0))],
        out_specs=pl.BlockSpec((tr, d), lambda i: (i, 0)),
        out_shape=jax.ShapeDtypeStruct((s, d), BF16), compiler_params=_cparams(("parallel",)),
    )(x, g)


def _rmsnorm_bwd(name, dh, x, g, dres):
    s, d = x.shape
    tr = _pick(s, (256, 128))

    def body(dh_ref, x_ref, g_ref, dres_ref, dx_ref, dxb_ref, dg_ref):
        xv = x_ref[...]
        r = _rstd(xv)
        xhat = xv * r
        dhv = dh_ref[...]
        dx = dres_ref[...] + _norm_bwd(dhv, xhat, r, g_ref[...])
        dx_ref[...] = dx
        dxb_ref[...] = dx.astype(BF16)
        part = jnp.sum(dhv * xhat, axis=0, keepdims=True)

        @pl.when(pl.program_id(0) == 0)
        def _():
            dg_ref[...] = part

        @pl.when(pl.program_id(0) > 0)
        def _():
            dg_ref[...] += part

    row = pl.BlockSpec((tr, d), lambda i: (i, 0))
    vec = pl.BlockSpec((1, d), lambda i: (0, 0))
    return pl.pallas_call(
        body, name=name, grid=(s // tr,), in_specs=[row, row, vec, row], out_specs=(row, row, vec),
        out_shape=(jax.ShapeDtypeStruct((s, d), F32), jax.ShapeDtypeStruct((s, d), BF16),
                   jax.ShapeDtypeStruct((1, d), F32)),
        compiler_params=_cparams(("arbitrary",)),
    )(dh, x, g, dres)


def _loss_head(y, target):
    s, d = y.shape
    tr = _pick(s, (256, 128))

    def body(y_ref, t_ref, dy_ref, dyb_ref, loss_ref):
        err = y_ref[...] - t_ref[...]
        dy = err * (1.0 / d)
        dy_ref[...] = dy
        dyb_ref[...] = dy.astype(BF16)
        part = 0.5 * jnp.sum(jnp.mean(err * err, axis=-1, keepdims=True), axis=0, keepdims=True)
        part = jnp.broadcast_to(part, (1, 128))

        @pl.when(pl.program_id(0) == 0)
        def _():
            loss_ref[...] = part

        @pl.when(pl.program_id(0) > 0)
        def _():
            loss_ref[...] += part

    row = pl.BlockSpec((tr, d), lambda i: (i, 0))
    return pl.pallas_call(
        body, name="loss_head", grid=(s // tr,), in_specs=[row, row],
        out_specs=(row, row, pl.BlockSpec((1, 128), lambda i: (0, 0))),
        out_shape=(jax.ShapeDtypeStruct((s, d), F32), jax.ShapeDtypeStruct((s, d), BF16),
                   jax.ShapeDtypeStruct((1, 128), F32)),
        compiler_params=_cparams(("arbitrary",)),
    )(y, target)


def _split_dot(x, tri):
    hi = x.astype(BF16)
    lo = (x - hi.astype(F32)).astype(BF16)
    return (jnp.dot(hi, tri, preferred_element_type=F32) + jnp.dot(lo, tri, preferred_element_type=F32))


def _tile_iotas():
    row = lax.broadcasted_iota(jnp.int32, (TILE, TILE), 0)
    col = lax.broadcasted_iota(jnp.int32, (TILE, TILE), 1)
    return row, col


def _sb_logits(qi, kb, mask):
    z = lax.dot_general(qi, kb, NT_DIMS, preferred_element_type=F32) * (HEAD_DIM ** -0.5)
    sp = jnp.log(1.0 + jnp.exp(-jnp.abs(z)))
    lb = jnp.minimum(z, 0.0) - sp
    l1m = -jnp.maximum(z, 0.0) - sp
    if mask is not None:
        l1m = jnp.where(mask, l1m, 0.0)
    return lb, l1m


def _attn_fwd(p, gq, gk, ga, n_heads):
    s = p.shape[0]
    nq = s // TILE

    hp = ATTN_HEADS_PER_STEP
    wd = hp * HEAD_DIM

    def body(q_ref, k_ref, v_ref, gq_ref, gk_ref, ga_ref, att_ref, o_ref, r_ref, qn_s, kn_s, vb_s):
        heads = [slice(hh * HEAD_DIM, (hh + 1) * HEAD_DIM) for hh in range(hp)]
        for hd in heads:
            qv = q_ref[:, hd]
            qn_s[:, hd] = (qv * _rstd(qv) * gq_ref[...]).astype(BF16)
            kv = k_ref[:, hd]
            kn_s[:, hd] = (kv * _rstd(kv) * gk_ref[...]).astype(BF16)
        vb_s[...] = v_ref[...].astype(BF16)
        row, col = _tile_iotas()
        causal = col < row
        upper_ones = jnp.concatenate([(row > col).astype(BF16), jnp.ones((TILE, TILE), BF16)], axis=1)

        def tiles(rows, key_blocks, states):
            chains = [(hi, hd, keys, mask) for hi, hd in enumerate(heads) for keys, mask in key_blocks]
            logits = [_sb_logits(qn_s[rows, hd], kn_s[keys, hd], mask) for _, hd, keys, mask in chains]
            sums = [_split_dot(l1m, upper_ones) for _, l1m in logits]
            carry = [c for _, c in states]
            probs = []
            for (hi, _, _, mask), (lb, _), sm in zip(chains, logits, sums):
                a = jnp.exp(lb + sm[:, :TILE] + carry[hi])
                carry[hi] = carry[hi] + sm[:, TILE:]
                probs.append((a if mask is None else jnp.where(mask, a, 0.0)).astype(BF16))
            outs = [jnp.dot(a, vb_s[keys, hd], preferred_element_type=F32) for a, (_, hd, keys, _) in zip(probs, chains)]
            acc = [o_acc for o_acc, _ in states]
            for (hi, _, _, _), o in zip(chains, outs):
                acc[hi] = acc[hi] + o
            return tuple(zip(acc, carry))

        def key_block(b):
            return pl.ds(pl.multiple_of(b * TILE, TILE), TILE), None

        def qblock(i, _):
            rows = pl.ds(pl.multiple_of(i * TILE, TILE), TILE)
            zero = jnp.zeros((TILE, HEAD_DIM), F32)
            states = tuple((zero, zero) for _ in heads)
            states = lax.cond(i % 2 == 1, lambda st: tiles(rows, [(rows, causal), key_block(i - 1)], st),
                              lambda st: tiles(rows, [(rows, causal)], st), states)
            top = i - i % 2

            def kblocks(jj, states):
                return tiles(rows, [key_block(top - 1 - 2 * jj), key_block(top - 2 - 2 * jj)], states)

            states = lax.fori_loop(0, i // 2, kblocks, states)
            for hh, (hd, (o_acc, c)) in enumerate(zip(heads, states)):
                o_ref[rows, hd] = o_acc
                r_ref[rows, hd] = c
                att_ref[rows, hd] = (o_acc * _rstd(o_acc) * ga_ref[hh]).astype(BF16)
            return 0

        lax.fori_loop(0, nq, qblock, 0)

    col_blk = lambda off: pl.BlockSpec((s, wd), lambda h: (0, off + h))
    vec = pl.BlockSpec((1, HEAD_DIM), lambda h: (0, 0))
    hvec = pl.BlockSpec((hp, 1, HEAD_DIM), lambda h: (h, 0, 0))
    out = pl.BlockSpec((s, wd), lambda h: (0, h))
    w = n_heads * HEAD_DIM
    steps = n_heads // hp
    return pl.pallas_call(
        body, name="attn_fwd", grid=(steps,),
        in_specs=[col_blk(0), col_blk(steps), col_blk(2 * steps), vec, vec, hvec],
        out_specs=(out, out, out),
        out_shape=(jax.ShapeDtypeStruct((s, w), BF16), jax.ShapeDtypeStruct((s, w), F32),
                   jax.ShapeDtypeStruct((s, w), F32)),
        scratch_shapes=[pltpu.VMEM((s, wd), BF16)] * 3,
        compiler_params=_cparams(("parallel",), ATTN_VMEM_MB),
    )(p, p, p, gq, gk, ga)


def _attn_bwd(p, gq, gk, ga, o, rsum, dmix, n_heads):
    s = p.shape[0]
    nq = s // TILE

    hp = ATTN_HEADS_PER_STEP
    wd = hp * HEAD_DIM
    scale = HEAD_DIM ** -0.5

    def body(q_ref, k_ref, v_ref, gq_ref, gk_ref, ga_ref, o_ref, r_ref, dm_ref,
             dq_ref, dk_ref, dv_ref, dgq_ref, dgk_ref, dga_ref,
             qn_s, kn_s, vb_s, do_s, dqn_s, dkn_s, dv_s):
        step = pl.program_id(0)
        gqv, gkv = gq_ref[...], gk_ref[...]
        heads = [slice(hh * HEAD_DIM, (hh + 1) * HEAD_DIM) for hh in range(hp)]
        for hh, hd in enumerate(heads):
            qv = q_ref[:, hd]
            qn_s[:, hd] = (qv * _rstd(qv) * gqv).astype(BF16)
            kv = k_ref[:, hd]
            kn_s[:, hd] = (kv * _rstd(kv) * gkv).astype(BF16)
            ov = o_ref[:, hd]
            ro = _rstd(ov)
            ohat = ov * ro
            dm = dm_ref[:, hd]
            dga_ref[hh] = jnp.sum(dm * ohat, axis=0, keepdims=True)
            do_s[:, hd] = _norm_bwd(dm, ohat, ro, ga_ref[hh]).astype(BF16)
        vb_s[...] = v_ref[...].astype(BF16)
        dkn_s[...] = jnp.zeros_like(dkn_s)
        dv_s[...] = jnp.zeros_like(dv_s)
        row, col = _tile_iotas()
        causal = col < row
        ones = jnp.ones((TILE, TILE), BF16)
        incl_ones = jnp.concatenate([(row <= col).astype(BF16), ones], axis=1)
        excl_ones = jnp.concatenate([(row < col).astype(BF16), ones], axis=1)

        def tiles(rows, key_blocks, states):
            chains = [(hi, hd, keys, mask) for hi, hd in enumerate(heads) for keys, mask in key_blocks]
            qis = [qn_s[rows, hd] for hd in heads]
            dois = [do_s[rows, hd] for hd in heads]
            logits = [_sb_logits(qis[hi], kn_s[keys, hd], mask) for hi, hd, keys, mask in chains]
            sums = [_split_dot(l1m, incl_ones) for _, l1m in logits]
            das = [lax.dot_general(dois[hi], vb_s[keys, hd], NT_DIMS, preferred_element_type=F32)
                   for hi, hd, keys, _ in chains]
            pfx = [st[1] for st in states]
            probs, dss = [], []
            for (hi, hd, _, mask), (lb, _), sm, da in zip(chains, logits, sums, das):
                a = jnp.exp(lb + (r_ref[rows, hd] - pfx[hi] - sm[:, :TILE]))
                pfx[hi] = pfx[hi] + sm[:, TILE:]
                a = a if mask is None else jnp.where(mask, a, 0.0)
                probs.append(a.astype(BF16))
                dss.append(da * a)
            dsums = [_split_dot(ds, excl_ones) for ds in dss]
            pc = [st[2] for st in states]
            dzs = []
            for (hi, _, _, mask), (lb, _), ds, dsm in zip(chains, logits, dss, dsums):
                dl1m = pc[hi] + dsm[:, :TILE]
                pc[hi] = pc[hi] + dsm[:, TILE:]
                dl1m = dl1m if mask is None else jnp.where(mask, dl1m, 0.0)
                beta = jnp.exp(lb)
                dzs.append(((ds * (1.0 - beta) - dl1m * beta) * scale).astype(BF16))
            dqs = [jnp.dot(dz, kn_s[keys, hd], preferred_element_type=F32) for dz, (_, hd, keys, _) in zip(dzs, chains)]
            for dz, a, (hi, hd, keys, _) in zip(dzs, probs, chains):
                dkn_s[keys, hd] += lax.dot_general(dz, qis[hi], TN_DIMS, preferred_element_type=F32)
                dv_s[keys, hd] += lax.dot_general(a, dois[hi], TN_DIMS, preferred_element_type=F32)
            dq_acc = [st[0] for st in states]
            for (hi, _, _, _), dq in zip(chains, dqs):
                dq_acc[hi] = dq_acc[hi] + dq
            return tuple(zip(dq_acc, pfx, pc))

        def key_block(b):
            return pl.ds(pl.multiple_of(b * TILE, TILE), TILE), None

        def qblock(i, _):
            rows = pl.ds(pl.multiple_of(i * TILE, TILE), TILE)
            zero = jnp.zeros((TILE, HEAD_DIM), F32)

            def kblocks(jj, states):
                return tiles(rows, [key_block(2 * jj), key_block(2 * jj + 1)], states)

            states = lax.fori_loop(0, i // 2, kblocks, tuple((zero, zero, zero) for _ in heads))
            states = lax.cond(i % 2 == 1, lambda st: tiles(rows, [key_block(i - 1), (rows, causal)], st),
                              lambda st: tiles(rows, [(rows, causal)], st), states)
            for hd, (dq_acc, _, _) in zip(heads, states):
                dqn_s[rows, hd] = dq_acc
            return 0

        lax.fori_loop(0, nq, qblock, 0)

        def norm_in_bwd(x_ref, g, dn_s, dx_ref, dg_ref):
            part = jnp.zeros((1, HEAD_DIM), F32)
            for hd in heads:
                xv = x_ref[:, hd]
                r = _rstd(xv)
                xhat = xv * r
                dn = dn_s[:, hd]
                dx_ref[:, hd] = _norm_bwd(dn, xhat, r, g).astype(BF16)
                part = part + jnp.sum(dn * xhat, axis=0, keepdims=True)

            @pl.when(step == 0)
            def _():
                dg_ref[...] = part

            @pl.when(step > 0)
            def _():
                dg_ref[...] += part

        norm_in_bwd(q_ref, gqv, dqn_s, dq_ref, dgq_ref)
        norm_in_bwd(k_ref, gkv, dkn_s, dk_ref, dgk_ref)
        dv_ref[...] = dv_s[...].astype(BF16)

    once = pl.Buffered(1)
    steps = n_heads // hp
    col_blk = lambda off: pl.BlockSpec((s, wd), lambda h: (0, off + h), pipeline_mode=once)
    vec = pl.BlockSpec((1, HEAD_DIM), lambda h: (0, 0))
    hvec = pl.BlockSpec((hp, 1, HEAD_DIM), lambda h: (h, 0, 0))
    blk = pl.BlockSpec((s, wd), lambda h: (0, h), pipeline_mode=once)
    w = n_heads * HEAD_DIM
    big = jax.ShapeDtypeStruct((s, w), BF16)
    return pl.pallas_call(
        body, name="attn_bwd", grid=(steps,),
        in_specs=[col_blk(0), col_blk(steps), col_blk(2 * steps), vec, vec, hvec, blk, blk, blk],
        out_specs=(blk, blk, blk, vec, vec, hvec),
        out_shape=(big, big, big, jax.ShapeDtypeStruct((1, HEAD_DIM), F32), jax.ShapeDtypeStruct((1, HEAD_DIM), F32),
                   jax.ShapeDtypeStruct((n_heads, 1, HEAD_DIM), F32)),
        scratch_shapes=[pltpu.VMEM((s, wd), BF16)] * 4 + [pltpu.VMEM((s, wd), F32)] * 3,
        compiler_params=_cparams(("arbitrary",), ATTN_VMEM_MB),
    )(p, p, p, gq, gk, ga, o, rsum, dmix)


_INV_SQRT2 = 0.7071067811865476
_INV_SQRT_2PI = 0.3989422804014327


def _gelu(x):
    return 0.5 * x * (1.0 + lax.erf(x * _INV_SQRT2))


def _gelu_grad(x):
    return 0.5 * (1.0 + lax.erf(x * _INV_SQRT2)) + x * (_INV_SQRT_2PI * jnp.exp(-0.5 * x * x))


def _sgu_fwd(p, gs, w_s, b_s, gb, col0, after):
    s = p.shape[0]
    n_chunks = s // TILE
    per_trip = _pick(n_chunks, (4, 2, 1))

    def body(u_ref, v_ref, gs_ref, w_ref, b_ref, gb_ref, _, out_ref, vs_s):
        vg = _gelu(v_ref[...])
        vs_s[...] = (vg * _rstd(vg) * gs_ref[...]).astype(BF16)
        row, col = _tile_iotas()
        wt = jnp.where(col <= row, w_ref[...], 0.0).astype(BF16)
        bcol = b_ref[...]
        gbv = gb_ref[...]

        def chunks(c, _):
            rows = [pl.ds(pl.multiple_of((c * per_trip + k) * TILE, TILE), TILE) for k in range(per_trip)]
            mixed = [jnp.dot(wt, vs_s[r, :], preferred_element_type=F32) + bcol for r in rows]
            sgs = [_gelu(u_ref[r, :]) * mx for r, mx in zip(rows, mixed)]
            for r, sg in zip(rows, sgs):
                out_ref[r, :] = (sg * _rstd(sg) * gbv).astype(BF16)
            return 0

        lax.fori_loop(0, n_chunks // per_trip, chunks, 0)

    col_blk = lambda off: pl.BlockSpec((s, HEAD_DIM), lambda g: (0, off + g))
    gvec = pl.BlockSpec((None, 1, HEAD_DIM), lambda g: (g, 0, 0))
    return pl.pallas_call(
        body, name="sgu_fwd", grid=(N_GROUPS,),
        in_specs=[col_blk(col0), col_blk(col0 + N_GROUPS), gvec,
                  pl.BlockSpec((None, TILE, TILE), lambda g: (g, 0, 0)),
                  pl.BlockSpec((None, TILE, 1), lambda g: (g, 0, 0)), gvec,
                  pl.BlockSpec(after.shape, lambda g: (0, 0))],
        out_specs=pl.BlockSpec((s, HEAD_DIM), lambda g: (0, g)),
        out_shape=jax.ShapeDtypeStruct((s, N_GROUPS * HEAD_DIM), BF16),
        scratch_shapes=[pltpu.VMEM((s, HEAD_DIM), BF16)],
        compiler_params=_cparams(("parallel",)),
    )(p, p, gs, w_s, b_s, gb, after)


def _sgu_bwd(p, gs, w_s, b_s, gb, dmix, col0, dm_col0):
    s = p.shape[0]
    n_chunks = s // TILE
    per_trip = _pick(n_chunks, (4, 2, 1))

    def body(u_ref, v_ref, gs_ref, w_ref, b_ref, gb_ref, dm_ref,
             du_ref, dv_ref, dgs_ref, dw_ref, db_ref, dgb_ref, vs_s, dvs_s):
        gsv = gs_ref[...]
        gbv = gb_ref[...]
        vg = _gelu(v_ref[...])
        vs_s[...] = (vg * _rstd(vg) * gsv).astype(BF16)
        row, col = _tile_iotas()
        causal = col <= row
        wt = jnp.where(causal, w_ref[...], 0.0).astype(BF16)
        bcol = b_ref[...]

        def chunks(c, carry):
            dw_acc, db_acc, dgb_acc = carry
            rows = [pl.ds(pl.multiple_of((c * per_trip + k) * TILE, TILE), TILE) for k in range(per_trip)]
            vss = [vs_s[r, :] for r in rows]
            mixed = [jnp.dot(wt, vs, preferred_element_type=F32) + bcol for vs in vss]
            dmbs = []
            for r, mx in zip(rows, mixed):
                u_pre = u_ref[r, :]
                u = _gelu(u_pre)
                sg = u * mx
                rs = _rstd(sg)
                sghat = sg * rs
                dm = dm_ref[r, :]
                dsg = _norm_bwd(dm, sghat, rs, gbv)
                dgb_acc = dgb_acc + jnp.sum(dm * sghat, axis=0, keepdims=True)
                du_ref[r, :] = (dsg * mx * _gelu_grad(u_pre)).astype(BF16)
                dmixed = dsg * u
                db_acc = db_acc + jnp.sum(dmixed, axis=1, keepdims=True)
                dmbs.append(dmixed.astype(BF16))
            for dmb, vs in zip(dmbs, vss):
                dw_acc = dw_acc + lax.dot_general(dmb, vs, NT_DIMS, preferred_element_type=F32)
            for r, dmb in zip(rows, dmbs):
                dvs_s[r, :] = lax.dot_general(wt, dmb, TN_DIMS, preferred_element_type=F32)
            return dw_acc, db_acc, dgb_acc

        dw_acc, db_acc, dgb_acc = lax.fori_loop(
            0, n_chunks // per_trip, chunks,
            (jnp.zeros((TILE, TILE), F32), jnp.zeros((TILE, 1), F32), jnp.zeros((1, HEAD_DIM), F32)))
        dw_ref[...] = jnp.where(causal, dw_acc, 0.0)
        db_ref[...] = db_acc
        dgb_ref[...] = dgb_acc
        v_pre = v_ref[...]
        vg = _gelu(v_pre)
        rv = _rstd(vg)
        vhat = vg * rv
        dvs = dvs_s[...]
        dgs_ref[...] = jnp.sum(dvs * vhat, axis=0, keepdims=True)
        dv_ref[...] = (_norm_bwd(dvs, vhat, rv, gsv) * _gelu_grad(v_pre)).astype(BF16)

    col_blk = lambda off: pl.BlockSpec((s, HEAD_DIM), lambda g: (0, off + g))
    gvec = pl.BlockSpec((None, 1, HEAD_DIM), lambda g: (g, 0, 0))
    wspec = pl.BlockSpec((None, TILE, TILE), lambda g: (g, 0, 0))
    bspec = pl.BlockSpec((None, TILE, 1), lambda g: (g, 0, 0))
    blk = pl.BlockSpec((s, HEAD_DIM), lambda g: (0, g))
    big = jax.ShapeDtypeStruct((s, N_GROUPS * HEAD_DIM), BF16)
    gshape = jax.ShapeDtypeStruct((N_GROUPS, 1, HEAD_DIM), F32)
    return pl.pallas_call(
        body, name="sgu_bwd", grid=(N_GROUPS,),
        in_specs=[col_blk(col0), col_blk(col0 + N_GROUPS), gvec, wspec, bspec, gvec, col_blk(dm_col0)],
        out_specs=(blk, blk, gvec, wspec, bspec, gvec),
        out_shape=(big, big, gshape, jax.ShapeDtypeStruct((N_GROUPS, TILE, TILE), F32),
                   jax.ShapeDtypeStruct((N_GROUPS, TILE, 1), F32), gshape),
        scratch_shapes=[pltpu.VMEM((s, HEAD_DIM), BF16), pltpu.VMEM((s, HEAD_DIM), F32)],
        compiler_params=_cparams(("parallel",)),
    )(p, p, gs, w_s, b_s, gb, dmix)


SUBLANES = 8


def _shift_down(x, n):
    rolled = pltpu.roll(x, n, 0)
    edge = lax.broadcasted_iota(jnp.int32, (SUBLANES, x.shape[1]), 0)
    return jnp.concatenate([jnp.where(edge >= n, rolled[:SUBLANES], 0.0), rolled[SUBLANES:]], axis=0)


def _shift_up(x, n):
    s = x.shape[0]
    rolled = pltpu.roll(x, s - n, 0)
    edge = lax.broadcasted_iota(jnp.int32, (SUBLANES, x.shape[1]), 0)
    return jnp.concatenate([rolled[:s - SUBLANES], jnp.where(edge < SUBLANES - n, rolled[s - SUBLANES:], 0.0)], axis=0)


def _conv(x, w, b):
    x1, x2 = _shift_down(x, 1), _shift_down(x, 2)
    return b + w[0:1, :] * x2 + w[1:2, :] * x1 + w[2:3, :] * x, x1, x2


def _conv_specs(s, tn):
    xspec = pl.BlockSpec((2, s, tn), lambda j: (0, 0, j))
    wspec = pl.BlockSpec((2, CONV_WIDTH, tn), lambda j: (0, 0, j))
    bspec = pl.BlockSpec((2, 1, tn), lambda j: (0, 0, j))
    return xspec, wspec, bspec


def _conv_gate_fwd(up, cw, cb):
    _, s, f = up.shape
    tn = _pick(f, (256, 128))

    def body(x_ref, w_ref, b_ref, act_ref):
        gate = _conv(x_ref[0], w_ref[0], b_ref[0])[0]
        val = _conv(x_ref[1], w_ref[1], b_ref[1])[0]
        act_ref[...] = (gate * jax.nn.sigmoid(gate) * val).astype(BF16)

    xspec, wspec, bspec = _conv_specs(s, tn)
    return pl.pallas_call(
        body, name="conv_gate_fwd", grid=(f // tn,), in_specs=[xspec, wspec, bspec],
        out_specs=pl.BlockSpec((s, tn), lambda j: (0, j)), out_shape=jax.ShapeDtypeStruct((s, f), BF16),
        compiler_params=_cparams(("parallel",)),
    )(up, cw, cb)


def _up_conv_gate_fwd(h, wb, cw, cb, after=None):
    s, k = h.shape
    nb, _, bn = wb.shape
    hb = nb // 2
    f = hb * bn
    tm = _pick(s, (512, 256, 128))
    n_in = 5 + (after is not None)

    def body(*refs):
        h_ref, wg_ref, wv_ref, w_ref, b_ref = refs[:5]
        up_ref, act_ref, halo_s = refs[n_in:]
        first = pl.program_id(1) == 0
        outs = []
        for half, wt_ref in enumerate((wg_ref, wv_ref)):
            x = jnp.dot(h_ref[...], wt_ref[...], preferred_element_type=F32)
            up_ref[half] = x
            halo = jnp.where(first, 0.0, halo_s[half])
            halo_s[half] = x[tm - SUBLANES:]
            full = jnp.concatenate([halo, x], axis=0)
            x1 = pltpu.roll(full, 1, 0)[SUBLANES:]
            x2 = pltpu.roll(full, 2, 0)[SUBLANES:]
            w = w_ref[half]
            outs.append(b_ref[half] + w[0:1, :] * x2 + w[1:2, :] * x1 + w[2:3, :] * x)
        gate, val = outs
        act_ref[...] = (gate * jax.nn.sigmoid(gate) * val).astype(BF16)

    ins = [h, wb, wb, cw, cb]
    in_specs = [pl.BlockSpec((tm, k), lambda j, i: (i, 0)),
                pl.BlockSpec((None, k, bn), lambda j, i: (j, 0, 0)),
                pl.BlockSpec((None, k, bn), lambda j, i: (j + hb, 0, 0)),
                pl.BlockSpec((2, CONV_WIDTH, bn), lambda j, i: (0, 0, j)),
                pl.BlockSpec((2, 1, bn), lambda j, i: (0, 0, j))]
    if after is not None:
        ins.append(after)
        in_specs.append(pl.BlockSpec(after.shape, lambda j, i: (0, 0)))
    return pl.pallas_call(
        body, name="up_conv_gate_fwd", grid=(hb, s // tm), in_specs=in_specs,
        out_specs=(pl.BlockSpec((2, tm, bn), lambda j, i: (0, i, j)), pl.BlockSpec((tm, bn), lambda j, i: (i, j))),
        out_shape=(jax.ShapeDtypeStruct((2, s, f), F32), jax.ShapeDtypeStruct((s, f), BF16)),
        scratch_shapes=[pltpu.VMEM((2, SUBLANES, bn), F32)],
        compiler_params=_cparams(("parallel", "arbitrary"), 56),
    )(*ins)


def _down_dx_conv_gate_bwd(up, cw, cb, dy, wdown, after):
    _, s, f = up.shape
    d = dy.shape[1]
    tn = _pick(f, (256, 128))

    def body(x_ref, w_ref, b_ref, dy_ref, wd_ref, _, dx_ref, dw_ref, db_ref):
        da = lax.dot_general(dy_ref[...], wd_ref[...], NT_DIMS, preferred_element_type=F32)
        xg, xv = x_ref[0], x_ref[1]
        wg, wv = w_ref[0], w_ref[1]
        gate, xg1, xg2 = _conv(xg, wg, b_ref[0])
        val, xv1, xv2 = _conv(xv, wv, b_ref[1])
        sig = jax.nn.sigmoid(gate)
        dval = da * (gate * sig)
        dgate = da * val * (sig * (1.0 + gate * (1.0 - sig)))
        for half, (x, x1, x2, w, dz) in enumerate(((xg, xg1, xg2, wg, dgate), (xv, xv1, xv2, wv, dval))):
            dx_ref[half] = (w[2:3, :] * dz + w[1:2, :] * _shift_up(dz, 1) + w[0:1, :] * _shift_up(dz, 2)).astype(BF16)
            dw_ref[half, 0:1, :] = jnp.sum(dz * x2, axis=0, keepdims=True)
            dw_ref[half, 1:2, :] = jnp.sum(dz * x1, axis=0, keepdims=True)
            dw_ref[half, 2:3, :] = jnp.sum(dz * x, axis=0, keepdims=True)
            db_ref[half] = jnp.sum(dz, axis=0, keepdims=True)

    xspec, wspec, bspec = _conv_specs(s, tn)
    return pl.pallas_call(
        body, name="down_dx_conv_gate_bwd", grid=(f // tn,),
        in_specs=[xspec, wspec, bspec, pl.BlockSpec((s, d), lambda j: (0, 0)), pl.BlockSpec((tn, d), lambda j: (j, 0)),
                  pl.BlockSpec(after.shape, lambda j: (0, 0))],
        out_specs=(xspec, wspec, bspec),
        out_shape=(jax.ShapeDtypeStruct((2, s, f), BF16), jax.ShapeDtypeStruct((2, CONV_WIDTH, f), F32),
                   jax.ShapeDtypeStruct((2, 1, f), F32)),
        compiler_params=_cparams(("parallel",), 56),
    )(up, cw, cb, dy, wdown, after)


def _mesh_pos():
    return lax.axis_index("x"), lax.axis_index("y"), lax.axis_index("c")


def _remote(src, dst, send_sem, recv_sem, to):
    return pltpu.make_async_remote_copy(src_ref=src, dst_ref=dst, send_sem=send_sem, recv_sem=recv_sem,
                                        device_id=to, device_id_type=pl.DeviceIdType.MESH)


HBM_SPEC = pl.BlockSpec(memory_space=pltpu.HBM)
SEM_SPEC = pl.BlockSpec(memory_space=pltpu.SEMAPHORE)
ANY_SPEC = pl.BlockSpec(memory_space=pl.ANY)
TOKEN_SPEC = pl.BlockSpec(memory_space=pltpu.VMEM)
TOKEN_SHAPE = jax.ShapeDtypeStruct((8, 128), F32)
DATAFLOW = pltpu.SideEffectType.DATAFLOW_SIDE_EFFECTING
GATHER_PLANE = (2, 4, 6)


def _slot(pos):
    return 4 * pos[0] + 2 * pos[1] + pos[2]


def _flip(pos, k):
    return (pos[0] ^ ((k >> 2) & 1), pos[1] ^ ((k >> 1) & 1), pos[2] ^ (k & 1))


def _hbm(a):
    return pltpu.with_memory_space_constraint(a, pltpu.HBM)


def _hbm_shapes(arrays):
    return tuple(pltpu.HBM(a.shape, a.dtype) for a in arrays)


class _Split:
    def __init__(self, n, outs, n_sets):
        k = 2 * n * int(n_sets)
        self.n = n
        self.sems = list(outs[:k])
        self.bufs = list(outs[k:k + 2 * n])
        self.token = outs[-1]

    def sem_set(self, i):
        return self.sems[2 * self.n * i:2 * self.n * (i + 1)]


def _split_call(name, body, bufs, sems_in, n_sets, after):
    n = len(bufs) // 2
    k = 2 * n * int(n_sets)
    m = len(sems_in)
    afters = list(after) if isinstance(after, (list, tuple)) else [after]
    na = len(afters)

    def wrapped(*refs):
        srcs, dsts = refs[:n], refs[n:2 * n]
        s_in = refs[2 * n:2 * n + m]
        s_out = refs[2 * n + m + na:2 * n + m + na + k]
        token, local_sems = refs[-2], refs[-1]
        body(srcs, dsts, s_in, s_out, local_sems)
        token[...] = jnp.zeros_like(token)

    outs = pl.pallas_call(
        wrapped, name=name,
        out_shape=(pltpu.SemaphoreType.DMA(()),) * k + _hbm_shapes(bufs) + (TOKEN_SHAPE,),
        in_specs=[HBM_SPEC] * (2 * n) + [SEM_SPEC] * m + [ANY_SPEC] * na,
        out_specs=(SEM_SPEC,) * k + (HBM_SPEC,) * (2 * n) + (TOKEN_SPEC,),
        input_output_aliases={i: k + i for i in range(2 * n)},
        scratch_shapes=[pltpu.SemaphoreType.DMA((n,))],
        compiler_params=pltpu.CompilerParams(has_side_effects=DATAFLOW),
    )(*[_hbm(b) for b in bufs], *sems_in, *afters)
    return _Split(n, outs, n_sets)


def _wait_slots(land, count, send_sem, recv_sem, me, send=False, recv=False):
    span = land.at[pl.ds(0, count)]
    cp = _remote(span, span, send_sem, recv_sem, me)
    if send:
        cp.wait_send()
    if recv:
        cp.wait_recv()


X_FLIP, Y_FLIP, DIAG_FLIP = 4, 2, 6
BF16_SUBLANES = 16


def _gather_start(name, shards, after):
    n = len(shards)
    my_slot = _slot(_mesh_pos())
    lands = [lax.dynamic_update_slice(lax.empty((N_DEV,) + w.shape, w.dtype), w[None], (my_slot, 0, 0)) for w in shards]

    def body(srcs, dsts, _, sems, local_sems):
        me = _mesh_pos()
        for a in range(n):
            for k in (1, X_FLIP, Y_FLIP):
                _remote(srcs[a], dsts[a].at[_slot(me)], sems[a], sems[n + a], _flip(me, k)).start()

    return _split_call(name, body, list(shards) + lands, [], 1, after)


def _gather_relay(name, started, after):
    n = started.n

    def body(srcs, dsts, sems_a, sems_out, local_sems):
        me = _mesh_pos()
        sibling = _flip(me, 1)
        pass_on, relay = sems_out[:2 * n], sems_out[2 * n:]
        for a in range(n):
            _wait_slots(dsts[a], 3, sems_a[a], sems_a[n + a], me, recv=True)
            from_x = dsts[a].at[_slot(_flip(me, X_FLIP))]
            from_y = dsts[a].at[_slot(_flip(me, Y_FLIP))]
            rows = srcs[a].shape[0]
            if rows % (2 * BF16_SUBLANES) == 0:
                top, bottom = pl.ds(0, rows // 2), pl.ds(rows // 2, rows // 2)
                _remote(from_y.at[top], from_y.at[top], relay[a], relay[n + a], _flip(me, X_FLIP)).start()
                _remote(from_x.at[bottom], from_x.at[bottom], relay[a], relay[n + a], _flip(me, Y_FLIP)).start()
            else:
                _remote(from_y, from_y, relay[a], relay[n + a], _flip(me, X_FLIP)).start()
            for block in (from_x, from_y):
                _remote(block, block, pass_on[a], pass_on[n + a], sibling).start()
        for a in range(n):
            _wait_slots(dsts[a], 3, sems_a[a], sems_a[n + a], me, send=True)

    return _split_call(name, body, started.bufs, started.sems, 2, after)


def _gather_relay_diagonal(name, relayed, after):
    n = relayed.n

    def body(srcs, dsts, relay, pass_on, local_sems):
        me = _mesh_pos()
        for a in range(n):
            _wait_slots(dsts[a], 1, relay[a], relay[n + a], me, recv=True)
            block = dsts[a].at[_slot(_flip(me, DIAG_FLIP))]
            _remote(block, block, pass_on[a], pass_on[n + a], _flip(me, 1)).start()
        for a in range(n):
            _wait_slots(dsts[a], 1, relay[a], relay[n + a], me, send=True)

    return _split_call(name, body, relayed.bufs, relayed.sem_set(1), 1, after)


def _gather_finish(name, relayed, diagonal, after):
    n = relayed.n

    def body(srcs, dsts, sems, _, local_sems):
        me = _mesh_pos()
        first, second = sems[:2 * n], sems[2 * n:]
        for a in range(n):
            _wait_slots(dsts[a], 2, first[a], first[n + a], me, send=True, recv=True)
            _wait_slots(dsts[a], 1, second[a], second[n + a], me, send=True, recv=True)

    return _split_call(name, body, diagonal.bufs, relayed.sem_set(0) + diagonal.sems, 0, after).bufs[n:]


def _exchange_start(name, blocked, after):
    n = len(blocked)
    my_slot = _slot(_mesh_pos())
    rows = [w.shape[-2] // (N_DEV if w.ndim == 2 else 1) for w in blocked]

    def block(ref, a, slot):
        if len(ref.shape) == 3:
            return ref.at[slot]
        return ref.at[pl.ds(pl.multiple_of(slot * rows[a], 16), rows[a])]

    lands = []
    for w, r in zip(blocked, rows):
        mine = lax.dynamic_slice_in_dim(w, my_slot, 1, 0) if w.ndim == 3 else lax.dynamic_slice_in_dim(w, my_slot * r, r, 0)[None]
        lands.append(lax.dynamic_update_slice(lax.empty((N_DEV, r, w.shape[-1]), w.dtype), mine, (my_slot, 0, 0)))

    def body(srcs, dsts, _, sems, local_sems):
        me = _mesh_pos()
        for a in range(n):
            for k in range(1, N_DEV):
                peer = _flip(me, k)
                _remote(block(srcs[a], a, _slot(peer)), dsts[a].at[_slot(me)], sems[a], sems[n + a], peer).start()

    return _split_call(name, body, list(blocked) + lands, [], True, after)


def _exchange_finish(name, started, after):
    group = started if isinstance(started, (list, tuple)) else [started]
    srcs_all = [b for st in group for b in st.bufs[:st.n]]
    lands_all = [b for st in group for b in st.bufs[st.n:]]
    sends = [s for st in group for s in st.sems[:st.n]]
    recvs = [s for st in group for s in st.sems[st.n:]]
    n = len(srcs_all)

    def body(srcs, dsts, sems, _, local_sems):
        me = _mesh_pos()
        for a in range(n):
            _wait_slots(dsts[a], N_DEV - 1, sems[a], sems[n + a], me, send=True, recv=True)

    return _split_call(name, body, srcs_all + lands_all, sends + recvs, 0, after).bufs[n:]


def _broadcast_start(name, arrays, after):
    n = len(arrays)
    my_slot = _slot(_mesh_pos())
    lands = [lax.dynamic_update_slice(lax.empty((N_DEV,) + w.shape, w.dtype), w[None], (my_slot, 0, 0)) for w in arrays]

    def body(srcs, dsts, _, sems, local_sems):
        me = _mesh_pos()
        for a in range(n):
            for k in range(1, N_DEV):
                _remote(srcs[a], dsts[a].at[_slot(me)], sems[a], sems[n + a], _flip(me, k)).start()

    return _split_call(name, body, list(arrays) + lands, [], True, after)


def _adamw_math(w, g, m, v):
    m = ADAM_B1 * m + (1.0 - ADAM_B1) * g
    v = ADAM_B2 * v + (1.0 - ADAM_B2) * (g * g)
    m_hat = m / (1.0 - ADAM_B1 ** ADAM_STEP)
    v_hat = v / (1.0 - ADAM_B2 ** ADAM_STEP)
    delta = -ADAM_LR * (m_hat / (jnp.sqrt(v_hat) + ADAM_EPS) + ADAM_WD * w)
    return delta, m, v


def _adamw(name, w, m, v, parts, layer, prev=None):
    _, r, c = w.shape
    tr = max(t for t in range(16, r + 1, 16) if r % t == 0 and t * c <= ADAMW_TILE_ELEMS)
    n_prev = 0 if prev is None else 4

    def body(*refs):
        w_ref, m_ref, v_ref, p_ref = refs[:4]
        g_ref, d_ref, nm_ref, nv_ref = refs[4 + n_prev:]
        g = p_ref[0].astype(F32)
        for src in range(1, N_DEV):
            g = g + p_ref[src].astype(F32)
        delta, nm, nv = _adamw_math(w_ref[...], g, m_ref[...], v_ref[...])
        g_ref[...] = g
        d_ref[...] = delta
        nm_ref[...] = nm
        nv_ref[...] = nv

    wspec = pl.BlockSpec((None, tr, c), lambda i: (layer, i, 0))
    pspec = pl.BlockSpec((N_DEV, tr, c), lambda i: (0, i, 0))
    shp = jax.ShapeDtypeStruct(w.shape, F32)
    return pl.pallas_call(
        body, name=name, grid=(r // tr,), in_specs=[wspec] * 3 + [pspec] + [ANY_SPEC] * n_prev,
        out_specs=(wspec,) * 4, out_shape=(shp,) * 4, input_output_aliases={4 + j: j for j in range(n_prev)},
        compiler_params=_cparams(("parallel",), 56),
    )(w, m, v, parts, *([] if prev is None else prev))


PACK_TILE = 8 * 128


def _pack(arrays):
    flat = []
    for a in arrays:
        v = a.reshape(-1)
        pad = (-v.shape[0]) % PACK_TILE
        flat.append(jnp.pad(v, (0, pad)) if pad else v)
    return jnp.concatenate(flat).reshape(-1, 128)


def _unpack(buf, like):
    flat = buf.reshape(-1)
    out, off = [], 0
    for a in like:
        n = 1
        for dim in a.shape:
            n *= dim
        out.append(flat[off:off + n].reshape(a.shape))
        off += n + (-n) % PACK_TILE
    return out


def _sum_slots(name, gathered):
    _, r, c = gathered.shape

    def body(x_ref, o_ref):
        acc = x_ref[0].astype(F32)
        for src in range(1, N_DEV):
            acc = acc + x_ref[src].astype(F32)
        o_ref[...] = acc

    return pl.pallas_call(body, name=name, out_shape=jax.ShapeDtypeStruct((r, c), F32))(gathered)


def _adamw_small(w, g, m, v):
    shp = jax.ShapeDtypeStruct(w.shape, F32)

    def body(w_ref, g_ref, m_ref, v_ref, d_ref, nm_ref, nv_ref):
        delta, nm, nv = _adamw_math(w_ref[...], g_ref[...], m_ref[...], v_ref[...])
        d_ref[...] = delta
        nm_ref[...] = nm
        nv_ref[...] = nv

    return pl.pallas_call(body, name="adamw_small", out_shape=(shp,) * 3)(w, g, m, v)


def kernel(x, attn_norm_g, w_in, q_norm_g, k_norm_g, sgu_norm_g, sgu_w, sgu_b, out_norm_a_g, out_norm_b_g, w_out, ffn_norm_g, w_up, conv_w, conv_b, w_down, loss_target, m_attn_norm_g, m_w_in, m_q_norm_g, m_k_norm_g, m_sgu_norm_g, m_sgu_w, m_sgu_b, m_out_norm_a_g, m_out_norm_b_g, m_w_out, m_ffn_norm_g, m_w_up, m_conv_w, m_conv_b, m_w_down, v_attn_norm_g, v_w_in, v_q_norm_g, v_k_norm_g, v_sgu_norm_g, v_sgu_w, v_sgu_b, v_out_norm_a_g, v_out_norm_b_g, v_w_out, v_ffn_norm_g, v_w_up, v_conv_w, v_conv_b, v_w_down):
    depth = w_in.shape[0]
    s, d = x.shape[1], x.shape[2]
    n_heads = (d // 2) // HEAD_DIM
    sgu_col0 = 3 * n_heads
    f2 = w_up.shape[2] * N_DEV
    ff = f2 // 2
    my_slot = 4 * lax.axis_index("x") + 2 * lax.axis_index("y") + lax.axis_index("c")

    wb = [(w_in[l].astype(BF16), w_out[l].astype(BF16), w_up[l].astype(BF16), w_down[l].astype(BF16))
          for l in range(depth)]
    groups = {"in0": [wb[0][0]], "out0": [wb[0][1], conv_w.reshape(depth * CONV_WIDTH, -1)], "up0": [wb[0][2]],
              "down0": [wb[0][3]]}
    for l in range(1, depth):
        groups[f"in{l}"] = [wb[l][0], wb[l][1]]
        groups[f"ffn{l}"] = [wb[l][2], wb[l][3]]
    order = list(groups)
    started, relayed = {}, {}

    def start(gname, after):
        started[gname] = _gather_start(f"gather_{gname}_start", groups[gname], after)
        return started[gname].token

    def relay(gname, after):
        relayed[gname] = _gather_relay(f"gather_{gname}_relay", started[gname], after)
        token = relayed[gname].token
        k = order.index(gname)
        nxt = [k + 2] if k + 2 < len(order) - 1 else []
        if k == len(order) - 2:
            nxt = [k + 1]
        for j in nxt:
            token = start(order[j], token)
        return token

    def finish(gname, after):
        diagonal = _gather_relay_diagonal(f"gather_{gname}_diagonal", relayed[gname], after)
        return _gather_finish(f"gather_{gname}_finish", relayed[gname], diagonal, diagonal.token)

    conv_b_all = conv_b.reshape(depth, 2, 1, ff)
    sgu_b_col = sgu_b[..., None]
    token = start(order[1], start(order[0], attn_norm_g))
    token = relay("out0", relay("in0", token))
    win_g = finish("in0", token)[0]

    xs = x[0]
    saved = []
    gathered = []
    for l in range(depth):
        g1 = attn_norm_g[l][None]
        g2 = ffn_norm_g[l][None]
        gq, gk = q_norm_g[l][None], k_norm_g[l][None]
        ga = out_norm_a_g[l][:, None, :]
        gs = sgu_norm_g[l][:, None, :]
        gb = out_norm_b_g[l][:, None, :]
        h1 = _rmsnorm_fwd("attn_norm_fwd", xs, g1)
        p = _mm_nn_blocked("in_proj", h1, win_g, F32)
        att, o, rsum = _attn_fwd(p, gq, gk, ga, n_heads)
        token = relay("up0" if l == 0 else f"ffn{l}", att)
        sg = _sgu_fwd(p, gs, sgu_w[l], sgu_b_col[l], gb, sgu_col0, token)
        mix = jnp.concatenate([att, sg], axis=-1)
        if l == 0:
            wout_g, cw = finish("out0", mix)
            cw = jnp.transpose(cw.reshape(N_DEV, depth, CONV_WIDTH, -1), (1, 2, 0, 3)).reshape(depth, CONV_WIDTH, 2, ff)
            conv_w_all = jnp.transpose(cw, (0, 2, 1, 3))
        x1 = _mm_nn_res("out_proj", mix, wout_g.reshape(d, d), xs)
        h2 = _rmsnorm_fwd("ffn_norm_fwd", x1, g2)
        if l == 0:
            wup_g = finish("up0", h2)[0]
            token = relay("down0", wup_g)
            up, act = _up_conv_gate_fwd(h2, wup_g, conv_w_all[l], conv_b_all[l], after=token)
            wdown_g = finish("down0", up)[0]
        else:
            wup_g, wdown_g = finish(f"ffn{l}", h2)
            up, act = _up_conv_gate_fwd(h2, wup_g, conv_w_all[l], conv_b_all[l])
        saved.append((xs, h1, p, o, rsum, mix, x1, h2, up, act))
        gathered.append((win_g, wout_g, wup_g, wdown_g))
        if l + 1 < depth:
            token = relay(f"in{l + 1}", act)
            x2 = _mm_nn_res("down_proj", act, wdown_g.reshape(ff, d), x1, after=token)
            win_g, wout_g = finish(f"in{l + 1}", x2)
        else:
            x2 = _mm_nn_res("down_proj", act, wdown_g.reshape(ff, d), x1)
        xs = x2

    dx, dxb, loss_vec = _loss_head(xs, loss_target[0])
    loss = lax.psum(loss_vec[0, 0], MESH_AXES)

    exchanges = []
    small = [None] * depth
    small_names = ["attn_norm_g", "q_norm_g", "k_norm_g", "sgu_norm_g", "sgu_w", "sgu_b", "out_norm_a_g",
                   "out_norm_b_g", "ffn_norm_g", "conv_b", "conv_w"]
    for l in reversed(range(depth)):
        xs0, h1, p, o, rsum, mix, x1, h2, up, act = saved[l]
        win_g, wout_g, wup_g, wdown_g = gathered[l]
        wout_full = wout_g.reshape(d, d)
        wdown_full = wdown_g.reshape(ff, d)
        g1 = attn_norm_g[l][None]
        g2 = ffn_norm_g[l][None]
        gq, gk = q_norm_g[l][None], k_norm_g[l][None]
        ga = out_norm_a_g[l][:, None, :]
        gs = sgu_norm_g[l][:, None, :]
        gb = out_norm_b_g[l][:, None, :]
        d_wdown = _mm_tn_plain("down_proj_dw", act, dxb)
        exchanges.append((l, "down", ("w_down",), _exchange_start(f"grad_down{l}_start", [d_wdown], dx)))
        dup, d_cw, d_cb = _down_dx_conv_gate_bwd(up, conv_w_all[l], conv_b_all[l], dxb, wdown_full,
                                                 exchanges[-1][3].token)
        d_wup = _mm_tn_blocked("up_proj_dw", h2, dup, N_DEV, halves=True)
        exchanges.append((l, "up", ("w_up",), _exchange_start(f"grad_up{l}_start", [d_wup], d_cb)))
        dh2 = _mm_nt_blocked("up_proj_dx", dup, wup_g, halves=True, after=exchanges[-1][3].token)
        dx, dxb, d_g2 = _rmsnorm_bwd("ffn_norm_bwd", dh2, x1, g2, dx)
        d_wout = _mm_tn_plain("out_proj_dw", mix, dxb)
        exchanges.append((l, "out", ("w_out",), _exchange_start(f"grad_out{l}_start", [d_wout], d_g2)))
        dmix = _mm_nt_plain("out_proj_dx", dxb, wout_full, after=exchanges[-1][3].token)
        dq, dk, dv, d_gq, d_gk, d_ga = _attn_bwd(p, gq, gk, ga, o, rsum, dmix, n_heads)
        du, dvs, d_gs, d_sw, d_sb, d_gb = _sgu_bwd(p, gs, sgu_w[l], sgu_b_col[l], gb, dmix, sgu_col0, n_heads)
        dp = jnp.concatenate([dq, dk, dv, du, dvs], axis=-1)
        d_win = _mm_tn_blocked("in_proj_dw", h1, dp, N_DEV)
        exchanges.append((l, "in", ("w_in",), _exchange_start(f"grad_in{l}_start", [d_win], d_gq)))
        dh1 = _mm_nt_blocked("in_proj_dx", dp, win_g, after=exchanges[-1][3].token)
        dx, dxb, d_g1 = _rmsnorm_bwd("attn_norm_bwd", dh1, xs0, g1, dx)
        small[l] = dict(attn_norm_g=d_g1[0], q_norm_g=d_gq[0], k_norm_g=d_gk[0], sgu_norm_g=d_gs[:, 0], sgu_w=d_sw,
                        sgu_b=d_sb[..., 0], out_norm_a_g=d_ga[:, 0], out_norm_b_g=d_gb[:, 0], ffn_norm_g=d_g2[0],
                        conv_w=jnp.transpose(d_cw, (1, 0, 2)).reshape(CONV_WIDTH, f2), conv_b=d_cb.reshape(f2))
    grad_x = dx[None]

    f32_names = [n for n in small_names if n != "sgu_w"]
    small_g = [jnp.stack([small[l][n] for l in range(depth)]) for n in f32_names]
    sgu_w_g = jnp.stack([small[l]["sgu_w"] for l in range(depth)])
    small_sent = _broadcast_start("grad_small_start", [_pack(small_g), sgu_w_g.reshape(-1, TILE).astype(BF16)], dx)

    res = {}
    big = dict(w_in=(w_in, m_w_in, v_w_in), w_out=(w_out, m_w_out, v_w_out), w_up=(w_up, m_w_up, v_w_up),
               w_down=(w_down, m_w_down, v_w_down))
    after = [small_sent.token]
    batches = [[e for e in exchanges if e[0] == l] for l in reversed(range(depth))]
    batches = batches[:-1] + [batches[-1][:-1], batches[-1][-1:]]
    for i, batch in enumerate(batches):
        landed = _exchange_finish(f"grad_batch{i}_finish", [ex for _, _, _, ex in batch], after)
        after = []
        for (layer, name), parts in zip([(k, n) for k, _, names, _ in batch for n in names], landed):
            w, m, v = big[name]
            res[name] = _adamw(f"adamw_{name}", w, m, v, parts, layer, res.get(name))
            after.append(res[name][0])
    small_all, sgu_w_all = _exchange_finish("grad_small_finish", small_sent, after)
    small_sum = _unpack(_sum_slots("small_grad_sum", small_all), small_g)
    g_small = dict(zip(f32_names, small_sum))
    g_small["sgu_w"] = _sum_slots("sgu_w_grad_sum", sgu_w_all).reshape(sgu_w.shape)
    cwn = conv_w.shape[2]
    g_small["conv_w"] = lax.dynamic_slice_in_dim(g_small["conv_w"], my_slot * cwn, cwn, axis=2)
    small_w = dict(attn_norm_g=(attn_norm_g, m_attn_norm_g, v_attn_norm_g), q_norm_g=(q_norm_g, m_q_norm_g, v_q_norm_g),
                   k_norm_g=(k_norm_g, m_k_norm_g, v_k_norm_g), sgu_norm_g=(sgu_norm_g, m_sgu_norm_g, v_sgu_norm_g),
                   sgu_w=(sgu_w, m_sgu_w, v_sgu_w), sgu_b=(sgu_b, m_sgu_b, v_sgu_b),
                   out_norm_a_g=(out_norm_a_g, m_out_norm_a_g, v_out_norm_a_g),
                   out_norm_b_g=(out_norm_b_g, m_out_norm_b_g, v_out_norm_b_g),
                   ffn_norm_g=(ffn_norm_g, m_ffn_norm_g, v_ffn_norm_g), conv_b=(conv_b, m_conv_b, v_conv_b),
                   conv_w=(conv_w, m_conv_w, v_conv_w))
    like = [small_w[n][0] for n in small_names]
    pw = _pack([small_w[n][0] for n in small_names])
    pm = _pack([small_w[n][1] for n in small_names])
    pv = _pack([small_w[n][2] for n in small_names])
    pg = _pack([g_small[n].reshape(small_w[n][0].shape) for n in small_names])
    pd, pnm, pnv = _adamw_small(pw, pg, pm, pv)
    for n, dlt, nm, nv in zip(small_names, _unpack(pd, like), _unpack(pnm, like), _unpack(pnv, like)):
        res[n] = (g_small[n].reshape(small_w[n][0].shape), dlt, nm, nv)

    order = ["attn_norm_g", "w_in", "q_norm_g", "k_norm_g", "sgu_norm_g", "sgu_w", "sgu_b", "out_norm_a_g",
             "out_norm_b_g", "w_out", "ffn_norm_g", "w_up", "conv_w", "conv_b", "w_down"]
    outs = [loss, grad_x]
    for field in range(4):
        outs += [res[n][field] for n in order]
    return tuple(outs)
```

```python
import functools

import jax
import jax.numpy as jnp
from jax import lax
from jax.experimental import pallas as pl
from jax.experimental.pallas import tpu as pltpu

F32 = jnp.float32
BF16 = jnp.bfloat16
EPS = 1e-6
HEAD_DIM = 128
TILE = 128
ATTN_VMEM_MB = 58
ATTN_HEADS_PER_STEP = 4
N_GROUPS = 8
CONV_WIDTH = 3
N_DEV = 8
MESH_AXES = ("x", "y", "c")
MIB = 1024 * 1024

ADAM_LR = 0.001
ADAM_B1 = 0.9
ADAM_B2 = 0.999
ADAM_EPS = 1e-08
ADAM_WD = 0.01
ADAM_STEP = 10
ADAMW_TILE_ELEMS = 512 * 1024

NT_DIMS = (((1,), (1,)), ((), ()))
NN_DIMS = (((1,), (0,)), ((), ()))
TN_DIMS = (((0,), (0,)), ((), ()))


def _cparams(sem, vmem_mb=48):
    return pltpu.CompilerParams(dimension_semantics=sem, vmem_limit_bytes=vmem_mb * MIB)


def _pick(n, cands):
    for c in cands:
        if n % c == 0:
            return c
    return n


MXU_WIDTH = 256


def _pairs(nb, bn):
    return nb % 2 == 0 and bn % MXU_WIDTH != 0 and (2 * bn) % MXU_WIDTH == 0


def _mm(name, grid, ins, in_specs, out_shape, out_spec, dims, has_res=False, parts=None, vmem_mb=56, after=None,
        split_out=None):
    n_in = 2 + has_res + (after is not None)
    if after is not None:
        ins = tuple(ins) + (after,)
        in_specs = list(in_specs) + [pl.BlockSpec(after.shape, lambda *_: (0, 0))]

    def body(*refs):
        a_ref, b_ref = refs[:2]
        o_ref = refs[n_in]
        if parts is None:
            acc = lax.dot_general(a_ref[...], b_ref[...], dims, preferred_element_type=F32)
        else:
            acc = None
            for part in parts:
                a, b = part(a_ref, b_ref)
                prod = lax.dot_general(a, b, dims, preferred_element_type=F32)
                acc = prod if acc is None else acc + prod
        if has_res:
            acc = acc + refs[2][...]
        if split_out is None:
            o_ref[...] = acc.astype(o_ref.dtype)
        else:
            o_ref[0] = acc[:, :split_out].astype(o_ref.dtype)
            o_ref[1] = acc[:, split_out:].astype(o_ref.dtype)

    return pl.pallas_call(
        body, name=name, grid=grid, in_specs=in_specs, out_specs=out_spec, out_shape=out_shape,
        compiler_params=_cparams(("parallel",) * len(grid), vmem_mb),
    )(*ins)


def _two_blocks(b_ref, first):
    return jnp.concatenate([b_ref[first], b_ref[first + 1]], axis=1)


def _rows_for(m, row_bytes, budget):
    return _pick(m, tuple(t for t in (2048, 1024, 512, 256, 128) if t * row_bytes <= budget))


def _mm_nn_blocked(name, a, wb, out_dtype, after=None):
    m, k = a.shape
    nb, _, bn = wb.shape
    out_shape = jax.ShapeDtypeStruct((m, nb * bn), out_dtype)
    if _pairs(nb, bn):
        tm = _rows_for(m, 2 * bn * jnp.dtype(out_dtype).itemsize, 6 * MIB)
        return _mm(name, (nb // 2, m // tm), (a, wb),
                   [pl.BlockSpec((tm, k), lambda j, i: (i, 0)), pl.BlockSpec((2, k, bn), lambda j, i: (j, 0, 0))],
                   out_shape, pl.BlockSpec((tm, 2 * bn), lambda j, i: (i, j)), NN_DIMS,
                   parts=[lambda a_ref, b_ref: (a_ref[...], _two_blocks(b_ref, 0))], after=after)
    tm = _rows_for(m, bn * jnp.dtype(out_dtype).itemsize, 6 * MIB)
    a_spec = pl.BlockSpec((tm, k), lambda j, i: (i, 0))
    b_spec = pl.BlockSpec((None, k, bn), lambda j, i: (j, 0, 0))
    o_spec = pl.BlockSpec((tm, bn), lambda j, i: (i, j))
    return _mm(name, (nb, m // tm), (a, wb), [a_spec, b_spec], out_shape, o_spec, NN_DIMS, after=after)


def _mm_nn_res_norm(name, a, w, res, g):
    m, k = a.shape
    n = w.shape[1]
    tm = _pick(m, (512, 256, 128))

    def body(a_ref, w_ref, r_ref, g_ref, x_ref, h_ref):
        x = jnp.dot(a_ref[...], w_ref[...], preferred_element_type=F32) + r_ref[...]
        x_ref[...] = x
        h_ref[...] = (x * _rstd(x) * g_ref[...]).astype(BF16)

    row = pl.BlockSpec((tm, n), lambda i: (i, 0))
    return pl.pallas_call(
        body, name=name, grid=(m // tm,),
        in_specs=[pl.BlockSpec((tm, k), lambda i: (i, 0)), pl.BlockSpec((k, n), lambda i: (0, 0)), row,
                  pl.BlockSpec((1, n), lambda i: (0, 0))],
        out_specs=(row, row), out_shape=(jax.ShapeDtypeStruct((m, n), F32), jax.ShapeDtypeStruct((m, n), BF16)),
        compiler_params=_cparams(("parallel",), 56),
    )(a, w, res, g)


def _mm_nn_res(name, a, w, res, after=None):
    m, k = a.shape
    n = w.shape[1]
    tm = _pick(m, (512, 256, 128))
    tn = _rows_for(n, k * 2, 12 * MIB)
    a_spec = pl.BlockSpec((tm, k), lambda j, i: (i, 0))
    b_spec = pl.BlockSpec((k, tn), lambda j, i: (0, j))
    r_spec = pl.BlockSpec((tm, tn), lambda j, i: (i, j))
    o_spec = pl.BlockSpec((tm, tn), lambda j, i: (i, j))
    return _mm(name, (n // tn, m // tm), (a, w, res), [a_spec, b_spec, r_spec], jax.ShapeDtypeStruct((m, n), F32),
               o_spec, NN_DIMS, has_res=True, after=after)


def _mm_nt_blocked(name, dy, wb, halves=False, after=None):
    nb, n, bn = wb.shape
    m = dy.shape[-2]
    tm = _pick(m, (512, 256, 128))
    tn = _rows_for(n, nb * bn * 2, 12 * MIB)
    if halves:
        hb = nb // 2
        a_spec = pl.BlockSpec((2, tm, hb * bn), lambda j, i: (0, i, 0))
        a_part = lambda kk: (lambda a_ref: a_ref[kk // hb, :, (kk % hb) * bn:(kk % hb + 1) * bn])
    else:
        a_spec = pl.BlockSpec((tm, nb * bn), lambda j, i: (i, 0))
        a_part = lambda kk: (lambda a_ref: a_ref[:, kk * bn:(kk + 1) * bn])
    if _pairs(nb, bn) and not halves:
        parts = [(lambda a_ref, b_ref, kk=kk: (a_ref[:, kk * bn:(kk + 2) * bn], _two_blocks(b_ref, kk)))
                 for kk in range(0, nb, 2)]
    elif _pairs(nb // 2, bn) and halves:
        parts = [(lambda a_ref, b_ref, kk=kk: (a_ref[kk // hb, :, (kk % hb) * bn:(kk % hb + 2) * bn],
                                               _two_blocks(b_ref, kk))) for kk in range(0, nb, 2)]
    else:
        parts = [(lambda a_ref, b_ref, kk=kk, sel=a_part(kk): (sel(a_ref), b_ref[kk])) for kk in range(nb)]
    b_spec = pl.BlockSpec((nb, tn, bn), lambda j, i: (0, j, 0))
    o_spec = pl.BlockSpec((tm, tn), lambda j, i: (i, j))
    return _mm(name, (n // tn, m // tm), (dy, wb), [a_spec, b_spec], jax.ShapeDtypeStruct((m, n), F32), o_spec,
               NT_DIMS, parts=parts, after=after)


def _mm_nt_plain(name, dy, w, out_dtype=F32, after=None):
    m, k = dy.shape
    n = w.shape[0]
    tm = _rows_for(m, k * 2, 8 * MIB)
    tn = _pick(n, (512, 256, 128))
    a_spec = pl.BlockSpec((tm, k), lambda j, i: (i, 0))
    b_spec = pl.BlockSpec((tn, k), lambda j, i: (j, 0))
    o_spec = pl.BlockSpec((tm, tn), lambda j, i: (i, j))
    return _mm(name, (n // tn, m // tm), (dy, w), [a_spec, b_spec], jax.ShapeDtypeStruct((m, n), out_dtype), o_spec,
               NT_DIMS, after=after)


def _mm_tn_blocked(name, a, dy, nb, halves=False):
    s, k1 = a.shape
    bn = (dy.shape[-1] * (2 if halves else 1)) // nb
    if _pairs(nb, bn) and not halves:
        tm = _rows_for(k1, 2 * bn * 2, 6 * MIB)
        return _mm(name, (nb // 2, k1 // tm), (a, dy),
                   [pl.BlockSpec((s, tm), lambda j, i: (0, i)), pl.BlockSpec((s, 2 * bn), lambda j, i: (0, j))],
                   jax.ShapeDtypeStruct((nb, k1, bn), BF16), pl.BlockSpec((2, tm, bn), lambda j, i: (j, i, 0)), TN_DIMS,
                   split_out=bn)
    if halves and _pairs(nb // 2, bn):
        tm = _rows_for(k1, 2 * bn * 2, 6 * MIB)
        per_half = nb // 4
        return _mm(name, (nb // 2, k1 // tm), (a, dy),
                   [pl.BlockSpec((s, tm), lambda j, i: (0, i)),
                    pl.BlockSpec((None, s, 2 * bn), lambda j, i: (j // per_half, 0, j % per_half))],
                   jax.ShapeDtypeStruct((nb, k1, bn), BF16), pl.BlockSpec((2, tm, bn), lambda j, i: (j, i, 0)), TN_DIMS,
                   split_out=bn)
    tm = _rows_for(k1, bn * 2, 6 * MIB)
    a_spec = pl.BlockSpec((s, tm), lambda j, i: (0, i))
    if halves:
        hb = nb // 2
        b_spec = pl.BlockSpec((None, s, bn), lambda j, i: (j // hb, 0, j % hb))
    else:
        b_spec = pl.BlockSpec((s, bn), lambda j, i: (0, j))
    o_spec = pl.BlockSpec((None, tm, bn), lambda j, i: (j, i, 0))
    return _mm(name, (nb, k1 // tm), (a, dy), [a_spec, b_spec], jax.ShapeDtypeStruct((nb, k1, bn), BF16), o_spec,
               TN_DIMS)


def _mm_tn_plain(name, a, dy):
    s, k1 = a.shape
    n = dy.shape[1]
    tm = _pick(k1, (512, 256, 128))
    tn = _rows_for(n, s * 2, 8 * MIB)
    a_spec = pl.BlockSpec((s, tm), lambda i, j: (0, i))
    b_spec = pl.BlockSpec((s, tn), lambda i, j: (0, j))
    o_spec = pl.BlockSpec((tm, tn), lambda i, j: (i, j))
    return _mm(name, (k1 // tm, n // tn), (a, dy), [a_spec, b_spec], jax.ShapeDtypeStruct((k1, n), BF16), o_spec,
               TN_DIMS)


def _rstd(x):
    return lax.rsqrt(jnp.mean(x * x, axis=-1, keepdims=True) + EPS)


def _norm_bwd(dy, xhat, r, g):
    dxhat = dy * g
    return r * (dxhat - xhat * jnp.mean(dxhat * xhat, axis=-1, keepdims=True))


def _rmsnorm_fwd(name, x, g):
    s, d = x.shape
    tr = _pick(s, (256, 128))

    def body(x_ref, g_ref, h_ref):
        xv = x_ref[...]
        h_ref[...] = (xv * _rstd(xv) * g_ref[...]).astype(BF16)

    return pl.pallas_call(
        body, name=name, grid=(s // tr,),
        in_specs=[pl.BlockSpec((tr, d), lambda i: (i, 0)), pl.BlockSpec((1, d), lambda i: (0, 0))],
        out_specs=pl.BlockSpec((tr, d), lambda i: (i, 0)),
        out_shape=jax.ShapeDtypeStruct((s, d), BF16), compiler_params=_cparams(("parallel",)),
    )(x, g)


def _rmsnorm_bwd(name, dh, x, g, dres):
    s, d = x.shape
    tr = _pick(s, (256, 128))

    def body(dh_ref, x_ref, g_ref, dres_ref, dx_ref, dxb_ref, dg_ref):
        xv = x_ref[...]
        r = _rstd(xv)
        xhat = xv * r
        dhv = dh_ref[...]
        dx = dres_ref[...] + _norm_bwd(dhv, xhat, r, g_ref[...])
        dx_ref[...] = dx
        dxb_ref[...] = dx.astype(BF16)
        part = jnp.sum(dhv * xhat, axis=0, keepdims=True)

        @pl.when(pl.program_id(0) == 0)
        def _():
            dg_ref[...] = part

        @pl.when(pl.program_id(0) > 0)
        def _():
            dg_ref[...] += part

    row = pl.BlockSpec((tr, d), lambda i: (i, 0))
    vec = pl.BlockSpec((1, d), lambda i: (0, 0))
    return pl.pallas_call(
        body, name=name, grid=(s // tr,), in_specs=[row, row, vec, row], out_specs=(row, row, vec),
        out_shape=(jax.ShapeDtypeStruct((s, d), F32), jax.ShapeDtypeStruct((s, d), BF16),
                   jax.ShapeDtypeStruct((1, d), F32)),
        compiler_params=_cparams(("arbitrary",)),
    )(dh, x, g, dres)


def _loss_head(y, target):
    s, d = y.shape
    tr = _pick(s, (256, 128))

    def body(y_ref, t_ref, dy_ref, dyb_ref, loss_ref):
        err = y_ref[...] - t_ref[...]
        dy = err * (1.0 / d)
        dy_ref[...] = dy
        dyb_ref[...] = dy.astype(BF16)
        part = 0.5 * jnp.sum(jnp.mean(err * err, axis=-1, keepdims=True), axis=0, keepdims=True)
        part = jnp.broadcast_to(part, (1, 128))

        @pl.when(pl.program_id(0) == 0)
        def _():
            loss_ref[...] = part

        @pl.when(pl.program_id(0) > 0)
        def _():
            loss_ref[...] += part

    row = pl.BlockSpec((tr, d), lambda i: (i, 0))
    return pl.pallas_call(
        body, name="loss_head", grid=(s // tr,), in_specs=[row, row],
        out_specs=(row, row, pl.BlockSpec((1, 128), lambda i: (0, 0))),
        out_shape=(jax.ShapeDtypeStruct((s, d), F32), jax.ShapeDtypeStruct((s, d), BF16),
                   jax.ShapeDtypeStruct((1, 128), F32)),
        compiler_params=_cparams(("arbitrary",)),
    )(y, target)


def _split_dot(x, tri):
    hi = x.astype(BF16)
    lo = (x - hi.astype(F32)).astype(BF16)
    return (jnp.dot(hi, tri, preferred_element_type=F32) + jnp.dot(lo, tri, preferred_element_type=F32))


def _tile_iotas():
    row = lax.broadcasted_iota(jnp.int32, (TILE, TILE), 0)
    col = lax.broadcasted_iota(jnp.int32, (TILE, TILE), 1)
    return row, col


def _sb_logits(qi, kb, mask):
    z = lax.dot_general(qi, kb, NT_DIMS, preferred_element_type=F32) * (HEAD_DIM ** -0.5)
    sp = jnp.log(1.0 + jnp.exp(-jnp.abs(z)))
    lb = jnp.minimum(z, 0.0) - sp
    l1m = -jnp.maximum(z, 0.0) - sp
    if mask is not None:
        l1m = jnp.where(mask, l1m, 0.0)
    return lb, l1m


def _attn_fwd(p, gq, gk, ga, n_heads):
    s = p.shape[0]
    nq = s // TILE

    hp = ATTN_HEADS_PER_STEP
    wd = hp * HEAD_DIM

    def body(q_ref, k_ref, v_ref, gq_ref, gk_ref, ga_ref, att_ref, o_ref, r_ref, qn_s, kn_s, vb_s):
        heads = [slice(hh * HEAD_DIM, (hh + 1) * HEAD_DIM) for hh in range(hp)]
        for hd in heads:
            qv = q_ref[:, hd]
            qn_s[:, hd] = (qv * _rstd(qv) * gq_ref[...]).astype(BF16)
            kv = k_ref[:, hd]
            kn_s[:, hd] = (kv * _rstd(kv) * gk_ref[...]).astype(BF16)
        vb_s[...] = v_ref[...].astype(BF16)
        row, col = _tile_iotas()
        causal = col < row
        upper_ones = jnp.concatenate([(row > col).astype(BF16), jnp.ones((TILE, TILE), BF16)], axis=1)

        def tiles(rows, key_blocks, states):
            chains = [(hi, hd, keys, mask) for hi, hd in enumerate(heads) for keys, mask in key_blocks]
            logits = [_sb_logits(qn_s[rows, hd], kn_s[keys, hd], mask) for _, hd, keys, mask in chains]
            sums = [_split_dot(l1m, upper_ones) for _, l1m in logits]
            carry = [c for _, c in states]
            probs = []
            for (hi, _, _, mask), (lb, _), sm in zip(chains, logits, sums):
                a = jnp.exp(lb + sm[:, :TILE] + carry[hi])
                carry[hi] = carry[hi] + sm[:, TILE:]
                probs.append((a if mask is None else jnp.where(mask, a, 0.0)).astype(BF16))
            outs = [jnp.dot(a, vb_s[keys, hd], preferred_element_type=F32) for a, (_, hd, keys, _) in zip(probs, chains)]
            acc = [o_acc for o_acc, _ in states]
            for (hi, _, _, _), o in zip(chains, outs):
                acc[hi] = acc[hi] + o
            return tuple(zip(acc, carry))

        def key_block(b):
            return pl.ds(pl.multiple_of(b * TILE, TILE), TILE), None

        def qblock(i, _):
            rows = pl.ds(pl.multiple_of(i * TILE, TILE), TILE)
            zero = jnp.zeros((TILE, HEAD_DIM), F32)
            states = tuple((zero, zero) for _ in heads)
            states = lax.cond(i % 2 == 1, lambda st: tiles(rows, [(rows, causal), key_block(i - 1)], st),
                              lambda st: tiles(rows, [(rows, causal)], st), states)
            top = i - i % 2

            def kblocks(jj, states):
                return tiles(rows, [key_block(top - 1 - 2 * jj), key_block(top - 2 - 2 * jj)], states)

            states = lax.fori_loop(0, i // 2, kblocks, states)
            for hh, (hd, (o_acc, c)) in enumerate(zip(heads, states)):
                o_ref[rows, hd] = o_acc
                r_ref[rows, hd] = c
                att_ref[rows, hd] = (o_acc * _rstd(o_acc) * ga_ref[hh]).astype(BF16)
            return 0

        lax.fori_loop(0, nq, qblock, 0)

    col_blk = lambda off: pl.BlockSpec((s, wd), lambda h: (0, off + h))
    vec = pl.BlockSpec((1, HEAD_DIM), lambda h: (0, 0))
    hvec = pl.BlockSpec((hp, 1, HEAD_DIM), lambda h: (h, 0, 0))
    out = pl.BlockSpec((s, wd), lambda h: (0, h))
    w = n_heads * HEAD_DIM
    steps = n_heads // hp
    return pl.pallas_call(
        body, name="attn_fwd", grid=(steps,),
        in_specs=[col_blk(0), col_blk(steps), col_blk(2 * steps), vec, vec, hvec],
        out_specs=(out, out, out),
        out_shape=(jax.ShapeDtypeStruct((s, w), BF16), jax.ShapeDtypeStruct((s, w), F32),
                   jax.ShapeDtypeStruct((s, w), F32)),
        scratch_shapes=[pltpu.VMEM((s, wd), BF16)] * 3,
        compiler_params=_cparams(("parallel",), ATTN_VMEM_MB),
    )(p, p, p, gq, gk, ga)


def _attn_bwd(p, gq, gk, ga, o, rsum, dmix, n_heads):
    s = p.shape[0]
    nq = s // TILE

    hp = ATTN_HEADS_PER_STEP
    wd = hp * HEAD_DIM
    scale = HEAD_DIM ** -0.5

    def body(q_ref, k_ref, v_ref, gq_ref, gk_ref, ga_ref, o_ref, r_ref, dm_ref,
             dq_ref, dk_ref, dv_ref, dgq_ref, dgk_ref, dga_ref,
             qn_s, kn_s, vb_s, do_s, dqn_s, dkn_s, dv_s):
        step = pl.program_id(0)
        gqv, gkv = gq_ref[...], gk_ref[...]
        heads = [slice(hh * HEAD_DIM, (hh + 1) * HEAD_DIM) for hh in range(hp)]
        for hh, hd in enumerate(heads):
            qv = q_ref[:, hd]
            qn_s[:, hd] = (qv * _rstd(qv) * gqv).astype(BF16)
            kv = k_ref[:, hd]
            kn_s[:, hd] = (kv * _rstd(kv) * gkv).astype(BF16)
            ov = o_ref[:, hd]
            ro = _rstd(ov)
            ohat = ov * ro
            dm = dm_ref[:, hd]
            dga_ref[hh] = jnp.sum(dm * ohat, axis=0, keepdims=True)
            do_s[:, hd] = _norm_bwd(dm, ohat, ro, ga_ref[hh]).astype(BF16)
        vb_s[...] = v_ref[...].astype(BF16)
        dkn_s[...] = jnp.zeros_like(dkn_s)
        dv_s[...] = jnp.zeros_like(dv_s)
        row, col = _tile_iotas()
        causal = col < row
        ones = jnp.ones((TILE, TILE), BF16)
        incl_ones = jnp.concatenate([(row <= col).astype(BF16), ones], axis=1)
        excl_ones = jnp.concatenate([(row < col).astype(BF16), ones], axis=1)

        def tiles(rows, key_blocks, states):
            chains = [(hi, hd, keys, mask) for hi, hd in enumerate(heads) for keys, mask in key_blocks]
            qis = [qn_s[rows, hd] for hd in heads]
            dois = [do_s[rows, hd] for hd in heads]
            logits = [_sb_logits(qis[hi], kn_s[keys, hd], mask) for hi, hd, keys, mask in chains]
            sums = [_split_dot(l1m, incl_ones) for _, l1m in logits]
            das = [lax.dot_general(dois[hi], vb_s[keys, hd], NT_DIMS, preferred_element_type=F32)
                   for hi, hd, keys, _ in chains]
            pfx = [st[1] for st in states]
            probs, dss = [], []
            for (hi, hd, _, mask), (lb, _), sm, da in zip(chains, logits, sums, das):
                a = jnp.exp(lb + (r_ref[rows, hd] - pfx[hi] - sm[:, :TILE]))
                pfx[hi] = pfx[hi] + sm[:, TILE:]
                a = a if mask is None else jnp.where(mask, a, 0.0)
                probs.append(a.astype(BF16))
                dss.append(da * a)
            dsums = [_split_dot(ds, excl_ones) for ds in dss]
            pc = [st[2] for st in states]
            dzs = []
            for (hi, _, _, mask), (lb, _), ds, dsm in zip(chains, logits, dss, dsums):
                dl1m = pc[hi] + dsm[:, :TILE]
                pc[hi] = pc[hi] + dsm[:, TILE:]
                dl1m = dl1m if mask is None else jnp.where(mask, dl1m, 0.0)
                beta = jnp.exp(lb)
                dzs.append(((ds * (1.0 - beta) - dl1m * beta) * scale).astype(BF16))
            dqs = [jnp.dot(dz, kn_s[keys, hd], preferred_element_type=F32) for dz, (_, hd, keys, _) in zip(dzs, chains)]
            for dz, a, (hi, hd, keys, _) in zip(dzs, probs, chains):
                dkn_s[keys, hd] += lax.dot_general(dz, qis[hi], TN_DIMS, preferred_element_type=F32)
                dv_s[keys, hd] += lax.dot_general(a, dois[hi], TN_DIMS, preferred_element_type=F32)
            dq_acc = [st[0] for st in states]
            for (hi, _, _, _), dq in zip(chains, dqs):
                dq_acc[hi] = dq_acc[hi] + dq
            return tuple(zip(dq_acc, pfx, pc))

        def key_block(b):
            return pl.ds(pl.multiple_of(b * TILE, TILE), TILE), None

        def qblock(i, _):
            rows = pl.ds(pl.multiple_of(i * TILE, TILE), TILE)
            zero = jnp.zeros((TILE, HEAD_DIM), F32)

            def kblocks(jj, states):
                return tiles(rows, [key_block(2 * jj), key_block(2 * jj + 1)], states)

            states = lax.fori_loop(0, i // 2, kblocks, tuple((zero, zero, zero) for _ in heads))
            states = lax.cond(i % 2 == 1, lambda st: tiles(rows, [key_block(i - 1), (rows, causal)], st),
                              lambda st: tiles(rows, [(rows, causal)], st), states)
            for hd, (dq_acc, _, _) in zip(heads, states):
                dqn_s[rows, hd] = dq_acc
            return 0

        lax.fori_loop(0, nq, qblock, 0)

        def norm_in_bwd(x_ref, g, dn_s, dx_ref, dg_ref):
            part = jnp.zeros((1, HEAD_DIM), F32)
            for hd in heads:
                xv = x_ref[:, hd]
                r = _rstd(xv)
                xhat = xv * r
                dn = dn_s[:, hd]
                dx_ref[:, hd] = _norm_bwd(dn, xhat, r, g).astype(BF16)
                part = part + jnp.sum(dn * xhat, axis=0, keepdims=True)

            @pl.when(step == 0)
            def _():
                dg_ref[...] = part

            @pl.when(step > 0)
            def _():
                dg_ref[...] += part

        norm_in_bwd(q_ref, gqv, dqn_s, dq_ref, dgq_ref)
        norm_in_bwd(k_ref, gkv, dkn_s, dk_ref, dgk_ref)
        dv_ref[...] = dv_s[...].astype(BF16)

    once = pl.Buffered(1)
    steps = n_heads // hp
    col_blk = lambda off: pl.BlockSpec((s, wd), lambda h: (0, off + h), pipeline_mode=once)
    vec = pl.BlockSpec((1, HEAD_DIM), lambda h: (0, 0))
    hvec = pl.BlockSpec((hp, 1, HEAD_DIM), lambda h: (h, 0, 0))
    blk = pl.BlockSpec((s, wd), lambda h: (0, h), pipeline_mode=once)
    w = n_heads * HEAD_DIM
    big = jax.ShapeDtypeStruct((s, w), BF16)
    return pl.pallas_call(
        body, name="attn_bwd", grid=(steps,),
        in_specs=[col_blk(0), col_blk(steps), col_blk(2 * steps), vec, vec, hvec, blk, blk, blk],
        out_specs=(blk, blk, blk, vec, vec, hvec),
        out_shape=(big, big, big, jax.ShapeDtypeStruct((1, HEAD_DIM), F32), jax.ShapeDtypeStruct((1, HEAD_DIM), F32),
                   jax.ShapeDtypeStruct((n_heads, 1, HEAD_DIM), F32)),
        scratch_shapes=[pltpu.VMEM((s, wd), BF16)] * 4 + [pltpu.VMEM((s, wd), F32)] * 3,
        compiler_params=_cparams(("arbitrary",), ATTN_VMEM_MB),
    )(p, p, p, gq, gk, ga, o, rsum, dmix)


_INV_SQRT2 = 0.7071067811865476
_INV_SQRT_2PI = 0.3989422804014327


def _gelu(x):
    return 0.5 * x * (1.0 + lax.erf(x * _INV_SQRT2))


def _gelu_grad(x):
    return 0.5 * (1.0 + lax.erf(x * _INV_SQRT2)) + x * (_INV_SQRT_2PI * jnp.exp(-0.5 * x * x))


def _sgu_fwd(p, gs, w_s, b_s, gb, col0, after):
    s = p.shape[0]
    n_chunks = s // TILE
    per_trip = _pick(n_chunks, (4, 2, 1))

    def body(u_ref, v_ref, gs_ref, w_ref, b_ref, gb_ref, _, out_ref, vs_s):
        vg = _gelu(v_ref[...])
        vs_s[...] = (vg * _rstd(vg) * gs_ref[...]).astype(BF16)
        row, col = _tile_iotas()
        wt = jnp.where(col <= row, w_ref[...], 0.0).astype(BF16)
        bcol = b_ref[...]
        gbv = gb_ref[...]

        def chunks(c, _):
            rows = [pl.ds(pl.multiple_of((c * per_trip + k) * TILE, TILE), TILE) for k in range(per_trip)]
            mixed = [jnp.dot(wt, vs_s[r, :], preferred_element_type=F32) + bcol for r in rows]
            sgs = [_gelu(u_ref[r, :]) * mx for r, mx in zip(rows, mixed)]
            for r, sg in zip(rows, sgs):
                out_ref[r, :] = (sg * _rstd(sg) * gbv).astype(BF16)
            return 0

        lax.fori_loop(0, n_chunks // per_trip, chunks, 0)

    col_blk = lambda off: pl.BlockSpec((s, HEAD_DIM), lambda g: (0, off + g))
    gvec = pl.BlockSpec((None, 1, HEAD_DIM), lambda g: (g, 0, 0))
    return pl.pallas_call(
        body, name="sgu_fwd", grid=(N_GROUPS,),
        in_specs=[col_blk(col0), col_blk(col0 + N_GROUPS), gvec,
                  pl.BlockSpec((None, TILE, TILE), lambda g: (g, 0, 0)),
                  pl.BlockSpec((None, TILE, 1), lambda g: (g, 0, 0)), gvec,
                  pl.BlockSpec(after.shape, lambda g: (0, 0))],
        out_specs=pl.BlockSpec((s, HEAD_DIM), lambda g: (0, g)),
        out_shape=jax.ShapeDtypeStruct((s, N_GROUPS * HEAD_DIM), BF16),
        scratch_shapes=[pltpu.VMEM((s, HEAD_DIM), BF16)],
        compiler_params=_cparams(("parallel",)),
    )(p, p, gs, w_s, b_s, gb, after)


def _sgu_bwd(p, gs, w_s, b_s, gb, dmix, col0, dm_col0):
    s = p.shape[0]
    n_chunks = s // TILE
    per_trip = _pick(n_chunks, (4, 2, 1))

    def body(u_ref, v_ref, gs_ref, w_ref, b_ref, gb_ref, dm_ref,
             du_ref, dv_ref, dgs_ref, dw_ref, db_ref, dgb_ref, vs_s, dvs_s):
        gsv = gs_ref[...]
        gbv = gb_ref[...]
        vg = _gelu(v_ref[...])
        vs_s[...] = (vg * _rstd(vg) * gsv).astype(BF16)
        row, col = _tile_iotas()
        causal = col <= row
        wt = jnp.where(causal, w_ref[...], 0.0).astype(BF16)
        bcol = b_ref[...]

        def chunks(c, carry):
            dw_acc, db_acc, dgb_acc = carry
            rows = [pl.ds(pl.multiple_of((c * per_trip + k) * TILE, TILE), TILE) for k in range(per_trip)]
            vss = [vs_s[r, :] for r in rows]
            mixed = [jnp.dot(wt, vs, preferred_element_type=F32) + bcol for vs in vss]
            dmbs = []
            for r, mx in zip(rows, mixed):
                u_pre = u_ref[r, :]
                u = _gelu(u_pre)
                sg = u * mx
                rs = _rstd(sg)
                sghat = sg * rs
                dm = dm_ref[r, :]
                dsg = _norm_bwd(dm, sghat, rs, gbv)
                dgb_acc = dgb_acc + jnp.sum(dm * sghat, axis=0, keepdims=True)
                du_ref[r, :] = (dsg * mx * _gelu_grad(u_pre)).astype(BF16)
                dmixed = dsg * u
                db_acc = db_acc + jnp.sum(dmixed, axis=1, keepdims=True)
                dmbs.append(dmixed.astype(BF16))
            for dmb, vs in zip(dmbs, vss):
                dw_acc = dw_acc + lax.dot_general(dmb, vs, NT_DIMS, preferred_element_type=F32)
            for r, dmb in zip(rows, dmbs):
                dvs_s[r, :] = lax.dot_general(wt, dmb, TN_DIMS, preferred_element_type=F32)
            return dw_acc, db_acc, dgb_acc

        dw_acc, db_acc, dgb_acc = lax.fori_loop(
            0, n_chunks // per_trip, chunks,
            (jnp.zeros((TILE, TILE), F32), jnp.zeros((TILE, 1), F32), jnp.zeros((1, HEAD_DIM), F32)))
        dw_ref[...] = jnp.where(causal, dw_acc, 0.0)
        db_ref[...] = db_acc
        dgb_ref[...] = dgb_acc
        v_pre = v_ref[...]
        vg = _gelu(v_pre)
        rv = _rstd(vg)
        vhat = vg * rv
        dvs = dvs_s[...]
        dgs_ref[...] = jnp.sum(dvs * vhat, axis=0, keepdims=True)
        dv_ref[...] = (_norm_bwd(dvs, vhat, rv, gsv) * _gelu_grad(v_pre)).astype(BF16)

    col_blk = lambda off: pl.BlockSpec((s, HEAD_DIM), lambda g: (0, off + g))
    gvec = pl.BlockSpec((None, 1, HEAD_DIM), lambda g: (g, 0, 0))
    wspec = pl.BlockSpec((None, TILE, TILE), lambda g: (g, 0, 0))
    bspec = pl.BlockSpec((None, TILE, 1), lambda g: (g, 0, 0))
    blk = pl.BlockSpec((s, HEAD_DIM), lambda g: (0, g))
    big = jax.ShapeDtypeStruct((s, N_GROUPS * HEAD_DIM), BF16)
    gshape = jax.ShapeDtypeStruct((N_GROUPS, 1, HEAD_DIM), F32)
    return pl.pallas_call(
        body, name="sgu_bwd", grid=(N_GROUPS,),
        in_specs=[col_blk(col0), col_blk(col0 + N_GROUPS), gvec, wspec, bspec, gvec, col_blk(dm_col0)],
        out_specs=(blk, blk, gvec, wspec, bspec, gvec),
        out_shape=(big, big, gshape, jax.ShapeDtypeStruct((N_GROUPS, TILE, TILE), F32),
                   jax.ShapeDtypeStruct((N_GROUPS, TILE, 1), F32), gshape),
        scratch_shapes=[pltpu.VMEM((s, HEAD_DIM), BF16), pltpu.VMEM((s, HEAD_DIM), F32)],
        compiler_params=_cparams(("parallel",)),
    )(p, p, gs, w_s, b_s, gb, dmix)


SUBLANES = 8


def _shift_down(x, n):
    rolled = pltpu.roll(x, n, 0)
    edge = lax.broadcasted_iota(jnp.int32, (SUBLANES, x.shape[1]), 0)
    return jnp.concatenate([jnp.where(edge >= n, rolled[:SUBLANES], 0.0), rolled[SUBLANES:]], axis=0)


def _shift_up(x, n):
    s = x.shape[0]
    rolled = pltpu.roll(x, s - n, 0)
    edge = lax.broadcasted_iota(jnp.int32, (SUBLANES, x.shape[1]), 0)
    return jnp.concatenate([rolled[:s - SUBLANES], jnp.where(edge < SUBLANES - n, rolled[s - SUBLANES:], 0.0)], axis=0)


def _conv(x, w, b):
    x1, x2 = _shift_down(x, 1), _shift_down(x, 2)
    return b + w[0:1, :] * x2 + w[1:2, :] * x1 + w[2:3, :] * x, x1, x2


def _conv_specs(s, tn):
    xspec = pl.BlockSpec((2, s, tn), lambda j: (0, 0, j))
    wspec = pl.BlockSpec((2, CONV_WIDTH, tn), lambda j: (0, 0, j))
    bspec = pl.BlockSpec((2, 1, tn), lambda j: (0, 0, j))
    return xspec, wspec, bspec


def _up_conv_gate_fwd(h, wb, cw, cb, after=None):
    s, k = h.shape
    nb, _, bn = wb.shape
    hb = nb // 2
    f = hb * bn
    tm = _pick(s, (512, 256, 128))
    n_in = 5 + (after is not None)

    def body(*refs):
        h_ref, wg_ref, wv_ref, w_ref, b_ref = refs[:5]
        up_ref, act_ref, halo_s = refs[n_in:]
        first = pl.program_id(1) == 0
        outs = []
        for half, wt_ref in enumerate((wg_ref, wv_ref)):
            x = jnp.dot(h_ref[...], wt_ref[...], preferred_element_type=F32)
            up_ref[half] = x
            halo = jnp.where(first, 0.0, halo_s[half])
            halo_s[half] = x[tm - SUBLANES:]
            full = jnp.concatenate([halo, x], axis=0)
            x1 = pltpu.roll(full, 1, 0)[SUBLANES:]
            x2 = pltpu.roll(full, 2, 0)[SUBLANES:]
            w = w_ref[half]
            outs.append(b_ref[half] + w[0:1, :] * x2 + w[1:2, :] * x1 + w[2:3, :] * x)
        gate, val = outs
        act_ref[...] = (gate * jax.nn.sigmoid(gate) * val).astype(BF16)

    ins = [h, wb, wb, cw, cb]
    in_specs = [pl.BlockSpec((tm, k), lambda j, i: (i, 0)),
                pl.BlockSpec((None, k, bn), lambda j, i: (j, 0, 0)),
                pl.BlockSpec((None, k, bn), lambda j, i: (j + hb, 0, 0)),
                pl.BlockSpec((2, CONV_WIDTH, bn), lambda j, i: (0, 0, j)),
                pl.BlockSpec((2, 1, bn), lambda j, i: (0, 0, j))]
    if after is not None:
        ins.append(after)
        in_specs.append(pl.BlockSpec(after.shape, lambda j, i: (0, 0)))
    return pl.pallas_call(
        body, name="up_conv_gate_fwd", grid=(hb, s // tm), in_specs=in_specs,
        out_specs=(pl.BlockSpec((2, tm, bn), lambda j, i: (0, i, j)), pl.BlockSpec((tm, bn), lambda j, i: (i, j))),
        out_shape=(jax.ShapeDtypeStruct((2, s, f), F32), jax.ShapeDtypeStruct((s, f), BF16)),
        scratch_shapes=[pltpu.VMEM((2, SUBLANES, bn), F32)],
        compiler_params=_cparams(("parallel", "arbitrary"), 56),
    )(*ins)


def _down_dx_conv_gate_bwd(up, cw, cb, dy, wdown, after):
    _, s, f = up.shape
    d = dy.shape[1]
    tn = _pick(f, (256, 128))

    def body(x_ref, w_ref, b_ref, dy_ref, wd_ref, _, dx_ref, dw_ref, db_ref):
        da = lax.dot_general(dy_ref[...], wd_ref[...], NT_DIMS, preferred_element_type=F32)
        xg, xv = x_ref[0], x_ref[1]
        wg, wv = w_ref[0], w_ref[1]
        gate, xg1, xg2 = _conv(xg, wg, b_ref[0])
        val, xv1, xv2 = _conv(xv, wv, b_ref[1])
        sig = jax.nn.sigmoid(gate)
        dval = da * (gate * sig)
        dgate = da * val * (sig * (1.0 + gate * (1.0 - sig)))
        for half, (x, x1, x2, w, dz) in enumerate(((xg, xg1, xg2, wg, dgate), (xv, xv1, xv2, wv, dval))):
            dx_ref[half] = (w[2:3, :] * dz + w[1:2, :] * _shift_up(dz, 1) + w[0:1, :] * _shift_up(dz, 2)).astype(BF16)
            dw_ref[half, 0:1, :] = jnp.sum(dz * x2, axis=0, keepdims=True)
            dw_ref[half, 1:2, :] = jnp.sum(dz * x1, axis=0, keepdims=True)
            dw_ref[half, 2:3, :] = jnp.sum(dz * x, axis=0, keepdims=True)
            db_ref[half] = jnp.sum(dz, axis=0, keepdims=True)

    xspec, wspec, bspec = _conv_specs(s, tn)
    return pl.pallas_call(
        body, name="down_dx_conv_gate_bwd", grid=(f // tn,),
        in_specs=[xspec, wspec, bspec, pl.BlockSpec((s, d), lambda j: (0, 0)), pl.BlockSpec((tn, d), lambda j: (j, 0)),
                  pl.BlockSpec(after.shape, lambda j: (0, 0))],
        out_specs=(xspec, wspec, bspec),
        out_shape=(jax.ShapeDtypeStruct((2, s, f), BF16), jax.ShapeDtypeStruct((2, CONV_WIDTH, f), F32),
                   jax.ShapeDtypeStruct((2, 1, f), F32)),
        compiler_params=_cparams(("parallel",), 56),
    )(up, cw, cb, dy, wdown, after)


def _mesh_pos():
    return lax.axis_index("x"), lax.axis_index("y"), lax.axis_index("c")


def _remote(src, dst, send_sem, recv_sem, to):
    return pltpu.make_async_remote_copy(src_ref=src, dst_ref=dst, send_sem=send_sem, recv_sem=recv_sem,
                                        device_id=to, device_id_type=pl.DeviceIdType.MESH)


HBM_SPEC = pl.BlockSpec(memory_space=pltpu.HBM)
SEM_SPEC = pl.BlockSpec(memory_space=pltpu.SEMAPHORE)
ANY_SPEC = pl.BlockSpec(memory_space=pl.ANY)
TOKEN_SPEC = pl.BlockSpec(memory_space=pltpu.VMEM)
TOKEN_SHAPE = jax.ShapeDtypeStruct((8, 128), F32)
DATAFLOW = pltpu.SideEffectType.DATAFLOW_SIDE_EFFECTING
GATHER_PLANE = (2, 4, 6)


def _slot(pos):
    return 4 * pos[0] + 2 * pos[1] + pos[2]


def _flip(pos, k):
    return (pos[0] ^ ((k >> 2) & 1), pos[1] ^ ((k >> 1) & 1), pos[2] ^ (k & 1))


def _hbm(a):
    return pltpu.with_memory_space_constraint(a, pltpu.HBM)


def _hbm_shapes(arrays):
    return tuple(pltpu.HBM(a.shape, a.dtype) for a in arrays)


class _Split:
    def __init__(self, n, outs, n_sets):
        k = 2 * n * int(n_sets)
        self.n = n
        self.sems = list(outs[:k])
        self.bufs = list(outs[k:k + 2 * n])
        self.token = outs[-1]

    def sem_set(self, i):
        return self.sems[2 * self.n * i:2 * self.n * (i + 1)]


def _split_call(name, body, bufs, sems_in, n_sets, after):
    n = len(bufs) // 2
    k = 2 * n * int(n_sets)
    m = len(sems_in)
    afters = list(after) if isinstance(after, (list, tuple)) else [after]
    na = len(afters)

    def wrapped(*refs):
        srcs, dsts = refs[:n], refs[n:2 * n]
        s_in = refs[2 * n:2 * n + m]
        s_out = refs[2 * n + m + na:2 * n + m + na + k]
        token, local_sems = refs[-2], refs[-1]
        body(srcs, dsts, s_in, s_out, local_sems)
        token[...] = jnp.zeros_like(token)

    outs = pl.pallas_call(
        wrapped, name=name,
        out_shape=(pltpu.SemaphoreType.DMA(()),) * k + _hbm_shapes(bufs) + (TOKEN_SHAPE,),
        in_specs=[HBM_SPEC] * (2 * n) + [SEM_SPEC] * m + [ANY_SPEC] * na,
        out_specs=(SEM_SPEC,) * k + (HBM_SPEC,) * (2 * n) + (TOKEN_SPEC,),
        input_output_aliases={i: k + i for i in range(2 * n)},
        scratch_shapes=[pltpu.SemaphoreType.DMA((n,))],
        compiler_params=pltpu.CompilerParams(has_side_effects=DATAFLOW),
    )(*[_hbm(b) for b in bufs], *sems_in, *afters)
    return _Split(n, outs, n_sets)


def _wait_slots(land, count, send_sem, recv_sem, me, send=False, recv=False):
    span = land.at[pl.ds(0, count)]
    cp = _remote(span, span, send_sem, recv_sem, me)
    if send:
        cp.wait_send()
    if recv:
        cp.wait_recv()


X_FLIP, Y_FLIP, DIAG_FLIP = 4, 2, 6
BF16_SUBLANES = 16


def _gather_start(name, shards, after):
    n = len(shards)
    my_slot = _slot(_mesh_pos())
    lands = [lax.dynamic_update_slice(lax.empty((N_DEV,) + w.shape, w.dtype), w[None], (my_slot, 0, 0)) for w in shards]

    def body(srcs, dsts, _, sems, local_sems):
        me = _mesh_pos()
        for a in range(n):
            for k in (1, X_FLIP, Y_FLIP):
                _remote(srcs[a], dsts[a].at[_slot(me)], sems[a], sems[n + a], _flip(me, k)).start()

    return _split_call(name, body, list(shards) + lands, [], 1, after)


def _gather_relay(name, started, after):
    n = started.n

    def body(srcs, dsts, sems_a, sems_out, local_sems):
        me = _mesh_pos()
        sibling = _flip(me, 1)
        pass_on, relay = sems_out[:2 * n], sems_out[2 * n:]
        for a in range(n):
            _wait_slots(dsts[a], 3, sems_a[a], sems_a[n + a], me, recv=True)
            from_x = dsts[a].at[_slot(_flip(me, X_FLIP))]
            from_y = dsts[a].at[_slot(_flip(me, Y_FLIP))]
            rows = srcs[a].shape[0]
            if rows % (2 * BF16_SUBLANES) == 0:
                top, bottom = pl.ds(0, rows // 2), pl.ds(rows // 2, rows // 2)
                _remote(from_y.at[top], from_y.at[top], relay[a], relay[n + a], _flip(me, X_FLIP)).start()
                _remote(from_x.at[bottom], from_x.at[bottom], relay[a], relay[n + a], _flip(me, Y_FLIP)).start()
            else:
                _remote(from_y, from_y, relay[a], relay[n + a], _flip(me, X_FLIP)).start()
            for block in (from_x, from_y):
                _remote(block, block, pass_on[a], pass_on[n + a], sibling).start()
        for a in range(n):
            _wait_slots(dsts[a], 3, sems_a[a], sems_a[n + a], me, send=True)

    return _split_call(name, body, started.bufs, started.sems, 2, after)


def _gather_relay_diagonal(name, relayed, after):
    n = relayed.n

    def body(srcs, dsts, relay, pass_on, local_sems):
        me = _mesh_pos()
        for a in range(n):
            _wait_slots(dsts[a], 1, relay[a], relay[n + a], me, recv=True)
            block = dsts[a].at[_slot(_flip(me, DIAG_FLIP))]
            _remote(block, block, pass_on[a], pass_on[n + a], _flip(me, 1)).start()
        for a in range(n):
            _wait_slots(dsts[a], 1, relay[a], relay[n + a], me, send=True)

    return _split_call(name, body, relayed.bufs, relayed.sem_set(1), 1, after)


def _gather_finish(name, relayed, diagonal, after):
    n = relayed.n

    def body(srcs, dsts, sems, _, local_sems):
        me = _mesh_pos()
        first, second = sems[:2 * n], sems[2 * n:]
        for a in range(n):
            _wait_slots(dsts[a], 2, first[a], first[n + a], me, send=True, recv=True)
            _wait_slots(dsts[a], 1, second[a], second[n + a], me, send=True, recv=True)

    return _split_call(name, body, diagonal.bufs, relayed.sem_set(0) + diagonal.sems, 0, after).bufs[n:]


def _exchange_start(name, blocked, after):
    n = len(blocked)
    my_slot = _slot(_mesh_pos())
    rows = [w.shape[-2] // (N_DEV if w.ndim == 2 else 1) for w in blocked]

    def block(ref, a, slot):
        if len(ref.shape) == 3:
            return ref.at[slot]
        return ref.at[pl.ds(pl.multiple_of(slot * rows[a], 16), rows[a])]

    lands = []
    for w, r in zip(blocked, rows):
        mine = lax.dynamic_slice_in_dim(w, my_slot, 1, 0) if w.ndim == 3 else lax.dynamic_slice_in_dim(w, my_slot * r, r, 0)[None]
        lands.append(lax.dynamic_update_slice(lax.empty((N_DEV, r, w.shape[-1]), w.dtype), mine, (my_slot, 0, 0)))

    def body(srcs, dsts, _, sems, local_sems):
        me = _mesh_pos()
        for a in range(n):
            for k in range(1, N_DEV):
                peer = _flip(me, k)
                _remote(block(srcs[a], a, _slot(peer)), dsts[a].at[_slot(me)], sems[a], sems[n + a], peer).start()

    return _split_call(name, body, list(blocked) + lands, [], True, after)


def _exchange_finish(name, started, after):
    group = started if isinstance(started, (list, tuple)) else [started]
    srcs_all = [b for st in group for b in st.bufs[:st.n]]
    lands_all = [b for st in group for b in st.bufs[st.n:]]
    sends = [s for st in group for s in st.sems[:st.n]]
    recvs = [s for st in group for s in st.sems[st.n:]]
    n = len(srcs_all)

    def body(srcs, dsts, sems, _, local_sems):
        me = _mesh_pos()
        for a in range(n):
            _wait_slots(dsts[a], N_DEV - 1, sems[a], sems[n + a], me, send=True, recv=True)

    return _split_call(name, body, srcs_all + lands_all, sends + recvs, 0, after).bufs[n:]


def _broadcast_start(name, arrays, after):
    n = len(arrays)
    my_slot = _slot(_mesh_pos())
    lands = [lax.dynamic_update_slice(lax.empty((N_DEV,) + w.shape, w.dtype), w[None], (my_slot, 0, 0)) for w in arrays]

    def body(srcs, dsts, _, sems, local_sems):
        me = _mesh_pos()
        for a in range(n):
            for k in range(1, N_DEV):
                _remote(srcs[a], dsts[a].at[_slot(me)], sems[a], sems[n + a], _flip(me, k)).start()

    return _split_call(name, body, list(arrays) + lands, [], True, after)


def _adamw_math(w, g, m, v):
    m = ADAM_B1 * m + (1.0 - ADAM_B1) * g
    v = ADAM_B2 * v + (1.0 - ADAM_B2) * (g * g)
    m_hat = m / (1.0 - ADAM_B1 ** ADAM_STEP)
    v_hat = v / (1.0 - ADAM_B2 ** ADAM_STEP)
    delta = -ADAM_LR * (m_hat / (jnp.sqrt(v_hat) + ADAM_EPS) + ADAM_WD * w)
    return delta, m, v


def _adamw(name, w, m, v, parts, layer, prev=None):
    _, r, c = w.shape
    tr = max(t for t in range(16, r + 1, 16) if r % t == 0 and t * c <= ADAMW_TILE_ELEMS)
    n_prev = 0 if prev is None else 4

    def body(*refs):
        w_ref, m_ref, v_ref, p_ref = refs[:4]
        g_ref, d_ref, nm_ref, nv_ref = refs[4 + n_prev:]
        g = p_ref[0].astype(F32)
        for src in range(1, N_DEV):
            g = g + p_ref[src].astype(F32)
        delta, nm, nv = _adamw_math(w_ref[...], g, m_ref[...], v_ref[...])
        g_ref[...] = g
        d_ref[...] = delta
        nm_ref[...] = nm
        nv_ref[...] = nv

    wspec = pl.BlockSpec((None, tr, c), lambda i: (layer, i, 0))
    pspec = pl.BlockSpec((N_DEV, tr, c), lambda i: (0, i, 0))
    shp = jax.ShapeDtypeStruct(w.shape, F32)
    return pl.pallas_call(
        body, name=name, grid=(r // tr,), in_specs=[wspec] * 3 + [pspec] + [ANY_SPEC] * n_prev,
        out_specs=(wspec,) * 4, out_shape=(shp,) * 4, input_output_aliases={4 + j: j for j in range(n_prev)},
        compiler_params=_cparams(("parallel",), 56),
    )(w, m, v, parts, *([] if prev is None else prev))


PACK_TILE = 8 * 128


def _pack(arrays):
    flat = []
    for a in arrays:
        v = a.reshape(-1)
        pad = (-v.shape[0]) % PACK_TILE
        flat.append(jnp.pad(v, (0, pad)) if pad else v)
    return jnp.concatenate(flat).reshape(-1, 128)


def _unpack(buf, like):
    flat = buf.reshape(-1)
    out, off = [], 0
    for a in like:
        n = 1
        for dim in a.shape:
            n *= dim
        out.append(flat[off:off + n].reshape(a.shape))
        off += n + (-n) % PACK_TILE
    return out


def _sum_slots(name, gathered):
    _, r, c = gathered.shape

    def body(x_ref, o_ref):
        acc = x_ref[0].astype(F32)
        for src in range(1, N_DEV):
            acc = acc + x_ref[src].astype(F32)
        o_ref[...] = acc

    return pl.pallas_call(body, name=name, out_shape=jax.ShapeDtypeStruct((r, c), F32))(gathered)


def _adamw_small(w, g, m, v):
    shp = jax.ShapeDtypeStruct(w.shape, F32)

    def body(w_ref, g_ref, m_ref, v_ref, d_ref, nm_ref, nv_ref):
        delta, nm, nv = _adamw_math(w_ref[...], g_ref[...], m_ref[...], v_ref[...])
        d_ref[...] = delta
        nm_ref[...] = nm
        nv_ref[...] = nv

    return pl.pallas_call(body, name="adamw_small", out_shape=(shp,) * 3)(w, g, m, v)


def kernel(x, attn_norm_g, w_in, q_norm_g, k_norm_g, sgu_norm_g, sgu_w, sgu_b, out_norm_a_g, out_norm_b_g, w_out, ffn_norm_g, w_up, conv_w, conv_b, w_down, loss_target, m_attn_norm_g, m_w_in, m_q_norm_g, m_k_norm_g, m_sgu_norm_g, m_sgu_w, m_sgu_b, m_out_norm_a_g, m_out_norm_b_g, m_w_out, m_ffn_norm_g, m_w_up, m_conv_w, m_conv_b, m_w_down, v_attn_norm_g, v_w_in, v_q_norm_g, v_k_norm_g, v_sgu_norm_g, v_sgu_w, v_sgu_b, v_out_norm_a_g, v_out_norm_b_g, v_w_out, v_ffn_norm_g, v_w_up, v_conv_w, v_conv_b, v_w_down):
    depth = w_in.shape[0]
    s, d = x.shape[1], x.shape[2]
    n_heads = (d // 2) // HEAD_DIM
    sgu_col0 = 3 * n_heads
    f2 = w_up.shape[2] * N_DEV
    ff = f2 // 2
    my_slot = 4 * lax.axis_index("x") + 2 * lax.axis_index("y") + lax.axis_index("c")

    wb = [(w_in[l].astype(BF16), w_out[l].astype(BF16), w_up[l].astype(BF16), w_down[l].astype(BF16))
          for l in range(depth)]
    groups = {"in0": [wb[0][0]], "out0": [wb[0][1], conv_w.reshape(depth * CONV_WIDTH, -1)], "up0": [wb[0][2]],
              "down0": [wb[0][3]]}
    for l in range(1, depth):
        groups[f"in{l}"] = [wb[l][0], wb[l][1]]
        groups[f"ffn{l}"] = [wb[l][2], wb[l][3]]
    order = list(groups)
    started, relayed = {}, {}

    def start(gname, after):
        started[gname] = _gather_start(f"gather_{gname}_start", groups[gname], after)
        return started[gname].token

    def relay(gname, after):
        relayed[gname] = _gather_relay(f"gather_{gname}_relay", started[gname], after)
        token = relayed[gname].token
        k = order.index(gname)
        nxt = [k + 2] if k + 2 < len(order) - 1 else []
        if k == len(order) - 2:
            nxt = [k + 1]
        for j in nxt:
            token = start(order[j], token)
        return token

    def finish(gname, after):
        diagonal = _gather_relay_diagonal(f"gather_{gname}_diagonal", relayed[gname], after)
        return _gather_finish(f"gather_{gname}_finish", relayed[gname], diagonal, diagonal.token)

    conv_b_all = conv_b.reshape(depth, 2, 1, ff)
    sgu_b_col = sgu_b[..., None]
    token = start(order[1], start(order[0], attn_norm_g))
    token = relay("out0", relay("in0", token))
    win_g = finish("in0", token)[0]

    xs = x[0]
    saved = []
    gathered = []
    for l in range(depth):
        g1 = attn_norm_g[l][None]
        g2 = ffn_norm_g[l][None]
        gq, gk = q_norm_g[l][None], k_norm_g[l][None]
        ga = out_norm_a_g[l][:, None, :]
        gs = sgu_norm_g[l][:, None, :]
        gb = out_norm_b_g[l][:, None, :]
        h1 = _rmsnorm_fwd("attn_norm_fwd", xs, g1)
        p = _mm_nn_blocked("in_proj", h1, win_g, F32)
        att, o, rsum = _attn_fwd(p, gq, gk, ga, n_heads)
        token = relay("up0" if l == 0 else f"ffn{l}", att)
        sg = _sgu_fwd(p, gs, sgu_w[l], sgu_b_col[l], gb, sgu_col0, token)
        mix = jnp.concatenate([att, sg], axis=-1)
        if l == 0:
            wout_g, cw = finish("out0", mix)
            cw = jnp.transpose(cw.reshape(N_DEV, depth, CONV_WIDTH, -1), (1, 2, 0, 3)).reshape(depth, CONV_WIDTH, 2, ff)
            conv_w_all = jnp.transpose(cw, (0, 2, 1, 3))
        x1, h2 = _mm_nn_res_norm("out_proj_ffn_norm", mix, wout_g.reshape(d, d), xs, g2)
        if l == 0:
            wup_g = finish("up0", h2)[0]
            token = relay("down0", wup_g)
            up, act = _up_conv_gate_fwd(h2, wup_g, conv_w_all[l], conv_b_all[l], after=token)
            wdown_g = finish("down0", up)[0]
        else:
            wup_g, wdown_g = finish(f"ffn{l}", h2)
            up, act = _up_conv_gate_fwd(h2, wup_g, conv_w_all[l], conv_b_all[l])
        saved.append((xs, h1, p, o, rsum, mix, x1, h2, up, act))
        gathered.append((win_g, wout_g, wup_g, wdown_g))
        if l + 1 < depth:
            token = relay(f"in{l + 1}", act)
            x2 = _mm_nn_res("down_proj", act, wdown_g.reshape(ff, d), x1, after=token)
            win_g, wout_g = finish(f"in{l + 1}", x2)
        else:
            x2 = _mm_nn_res("down_proj", act, wdown_g.reshape(ff, d), x1)
        xs = x2

    dx, dxb, loss_vec = _loss_head(xs, loss_target[0])
    loss = lax.psum(loss_vec[0, 0], MESH_AXES)

    exchanges = []
    small = [None] * depth
    small_names = ["attn_norm_g", "q_norm_g", "k_norm_g", "sgu_norm_g", "sgu_w", "sgu_b", "out_norm_a_g",
                   "out_norm_b_g", "ffn_norm_g", "conv_b", "conv_w"]
    for l in reversed(range(depth)):
        xs0, h1, p, o, rsum, mix, x1, h2, up, act = saved[l]
        win_g, wout_g, wup_g, wdown_g = gathered[l]
        wout_full = wout_g.reshape(d, d)
        wdown_full = wdown_g.reshape(ff, d)
        g1 = attn_norm_g[l][None]
        g2 = ffn_norm_g[l][None]
        gq, gk = q_norm_g[l][None], k_norm_g[l][None]
        ga = out_norm_a_g[l][:, None, :]
        gs = sgu_norm_g[l][:, None, :]
        gb = out_norm_b_g[l][:, None, :]
        d_wdown = _mm_tn_plain("down_proj_dw", act, dxb)
        exchanges.append((l, "down", ("w_down",), _exchange_start(f"grad_down{l}_start", [d_wdown], dx)))
        dup, d_cw, d_cb = _down_dx_conv_gate_bwd(up, conv_w_all[l], conv_b_all[l], dxb, wdown_full,
                                                 exchanges[-1][3].token)
        d_wup = _mm_tn_blocked("up_proj_dw", h2, dup, N_DEV, halves=True)
        exchanges.append((l, "up", ("w_up",), _exchange_start(f"grad_up{l}_start", [d_wup], d_cb)))
        dh2 = _mm_nt_blocked("up_proj_dx", dup, wup_g, halves=True, after=exchanges[-1][3].token)
        dx, dxb, d_g2 = _rmsnorm_bwd("ffn_norm_bwd", dh2, x1, g2, dx)
        d_wout = _mm_tn_plain("out_proj_dw", mix, dxb)
        exchanges.append((l, "out", ("w_out",), _exchange_start(f"grad_out{l}_start", [d_wout], d_g2)))
        dmix = _mm_nt_plain("out_proj_dx", dxb, wout_full, after=exchanges[-1][3].token)
        dq, dk, dv, d_gq, d_gk, d_ga = _attn_bwd(p, gq, gk, ga, o, rsum, dmix, n_heads)
        du, dvs, d_gs, d_sw, d_sb, d_gb = _sgu_bwd(p, gs, sgu_w[l], sgu_b_col[l], gb, dmix, sgu_col0, n_heads)
        dp = jnp.concatenate([dq, dk, dv, du, dvs], axis=-1)
        d_win = _mm_tn_blocked("in_proj_dw", h1, dp, N_DEV)
        exchanges.append((l, "in", ("w_in",), _exchange_start(f"grad_in{l}_start", [d_win], d_gq)))
        dh1 = _mm_nt_blocked("in_proj_dx", dp, win_g, after=exchanges[-1][3].token)
        dx, dxb, d_g1 = _rmsnorm_bwd("attn_norm_bwd", dh1, xs0, g1, dx)
        small[l] = dict(attn_norm_g=d_g1[0], q_norm_g=d_gq[0], k_norm_g=d_gk[0], sgu_norm_g=d_gs[:, 0], sgu_w=d_sw,
                        sgu_b=d_sb[..., 0], out_norm_a_g=d_ga[:, 0], out_norm_b_g=d_gb[:, 0], ffn_norm_g=d_g2[0],
                        conv_w=jnp.transpose(d_cw, (1, 0, 2)).reshape(CONV_WIDTH, f2), conv_b=d_cb.reshape(f2))
    grad_x = dx[None]

    f32_names = [n for n in small_names if n != "sgu_w"]
    small_g = [jnp.stack([small[l][n] for l in range(depth)]) for n in f32_names]
    sgu_w_g = jnp.stack([small[l]["sgu_w"] for l in range(depth)])
    small_sent = _broadcast_start("grad_small_start", [_pack(small_g), sgu_w_g.reshape(-1, TILE).astype(BF16)], dx)

    res = {}
    big = dict(w_in=(w_in, m_w_in, v_w_in), w_out=(w_out, m_w_out, v_w_out), w_up=(w_up, m_w_up, v_w_up),
               w_down=(w_down, m_w_down, v_w_down))
    after = [small_sent.token]
    batches = [[e for e in exchanges if e[0] == l] for l in reversed(range(depth))]
    batches = batches[:-1] + [batches[-1][:-1], batches[-1][-1:]]
    for i, batch in enumerate(batches):
        landed = _exchange_finish(f"grad_batch{i}_finish", [ex for _, _, _, ex in batch], after)
        after = []
        for (layer, name), parts in zip([(k, n) for k, _, names, _ in batch for n in names], landed):
            w, m, v = big[name]
            res[name] = _adamw(f"adamw_{name}", w, m, v, parts, layer, res.get(name))
            after.append(res[name][0])
    small_all, sgu_w_all = _exchange_finish("grad_small_finish", small_sent, after)
    small_sum = _unpack(_sum_slots("small_grad_sum", small_all), small_g)
    g_small = dict(zip(f32_names, small_sum))
    g_small["sgu_w"] = _sum_slots("sgu_w_grad_sum", sgu_w_all).reshape(sgu_w.shape)
    cwn = conv_w.shape[2]
    g_small["conv_w"] = lax.dynamic_slice_in_dim(g_small["conv_w"], my_slot * cwn, cwn, axis=2)
    small_w = dict(attn_norm_g=(attn_norm_g, m_attn_norm_g, v_attn_norm_g), q_norm_g=(q_norm_g, m_q_norm_g, v_q_norm_g),
                   k_norm_g=(k_norm_g, m_k_norm_g, v_k_norm_g), sgu_norm_g=(sgu_norm_g, m_sgu_norm_g, v_sgu_norm_g),
                   sgu_w=(sgu_w, m_sgu_w, v_sgu_w), sgu_b=(sgu_b, m_sgu_b, v_sgu_b),
                   out_norm_a_g=(out_norm_a_g, m_out_norm_a_g, v_out_norm_a_g),
                   out_norm_b_g=(out_norm_b_g, m_out_norm_b_g, v_out_norm_b_g),
                   ffn_norm_g=(ffn_norm_g, m_ffn_norm_g, v_ffn_norm_g), conv_b=(conv_b, m_conv_b, v_conv_b),
                   conv_w=(conv_w, m_conv_w, v_conv_w))
    like = [small_w[n][0] for n in small_names]
    pw = _pack([small_w[n][0] for n in small_names])
    pm = _pack([small_w[n][1] for n in small_names])
    pv = _pack([small_w[n][2] for n in small_names])
    pg = _pack([g_small[n].reshape(small_w[n][0].shape) for n in small_names])
    pd, pnm, pnv = _adamw_small(pw, pg, pm, pv)
    for n, dlt, nm, nv in zip(small_names, _unpack(pd, like), _unpack(pnm, like), _unpack(pnv, like)):
        res[n] = (g_small[n].reshape(small_w[n][0].shape), dlt, nm, nv)

    order = ["attn_norm_g", "w_in", "q_norm_g", "k_norm_g", "sgu_norm_g", "sgu_w", "sgu_b", "out_norm_a_g",
             "out_norm_b_g", "w_out", "ffn_norm_g", "w_up", "conv_w", "conv_b", "w_down"]
    outs = [loss, grad_x]
    for field in range(4):
        outs += [res[n][field] for n in order]
    return tuple(outs)
```

```python
import functools

import jax
import jax.numpy as jnp
from jax import lax
from jax.experimental import pallas as pl
from jax.experimental.pallas import tpu as pltpu

F32 = jnp.float32
BF16 = jnp.bfloat16
EPS = 1e-6
HEAD_DIM = 128
TILE = 128
ATTN_VMEM_MB = 58
ATTN_HEADS_PER_STEP = 4
N_GROUPS = 8
CONV_WIDTH = 3
N_DEV = 8
MESH_AXES = ("x", "y", "c")
MIB = 1024 * 1024

ADAM_LR = 0.001
ADAM_B1 = 0.9
ADAM_B2 = 0.999
ADAM_EPS = 1e-08
ADAM_WD = 0.01
ADAM_STEP = 10
ADAMW_TILE_ELEMS = 512 * 1024

NT_DIMS = (((1,), (1,)), ((), ()))
NN_DIMS = (((1,), (0,)), ((), ()))
TN_DIMS = (((0,), (0,)), ((), ()))


def _cparams(sem, vmem_mb=48):
    return pltpu.CompilerParams(dimension_semantics=sem, vmem_limit_bytes=vmem_mb * MIB)


def _pick(n, cands):
    for c in cands:
        if n % c == 0:
            return c
    return n


MXU_WIDTH = 256


def _pairs(nb, bn):
    return nb % 2 == 0 and bn % MXU_WIDTH != 0 and (2 * bn) % MXU_WIDTH == 0


def _mm(name, grid, ins, in_specs, out_shape, out_spec, dims, has_res=False, parts=None, vmem_mb=56, after=None,
        split_out=None):
    n_in = 2 + has_res + (after is not None)
    if after is not None:
        ins = tuple(ins) + (after,)
        in_specs = list(in_specs) + [pl.BlockSpec(after.shape, lambda *_: (0, 0))]

    def body(*refs):
        a_ref, b_ref = refs[:2]
        o_ref = refs[n_in]
        if parts is None:
            acc = lax.dot_general(a_ref[...], b_ref[...], dims, preferred_element_type=F32)
        else:
            acc = None
            for part in parts:
                a, b = part(a_ref, b_ref)
                prod = lax.dot_general(a, b, dims, preferred_element_type=F32)
                acc = prod if acc is None else acc + prod
        if has_res:
            acc = acc + refs[2][...]
        if split_out is None:
            o_ref[...] = acc.astype(o_ref.dtype)
        else:
            o_ref[0] = acc[:, :split_out].astype(o_ref.dtype)
            o_ref[1] = acc[:, split_out:].astype(o_ref.dtype)

    return pl.pallas_call(
        body, name=name, grid=grid, in_specs=in_specs, out_specs=out_spec, out_shape=out_shape,
        compiler_params=_cparams(("parallel",) * len(grid), vmem_mb),
    )(*ins)


def _two_blocks(b_ref, first):
    return jnp.concatenate([b_ref[first], b_ref[first + 1]], axis=1)


def _rows_for(m, row_bytes, budget):
    return _pick(m, tuple(t for t in (2048, 1024, 512, 256, 128) if t * row_bytes <= budget))


def _mm_nn_blocked(name, a, wb, out_dtype, after=None):
    m, k = a.shape
    nb, _, bn = wb.shape
    out_shape = jax.ShapeDtypeStruct((m, nb * bn), out_dtype)
    if _pairs(nb, bn):
        tm = _rows_for(m, 2 * bn * jnp.dtype(out_dtype).itemsize, 6 * MIB)
        return _mm(name, (nb // 2, m // tm), (a, wb),
                   [pl.BlockSpec((tm, k), lambda j, i: (i, 0)), pl.BlockSpec((2, k, bn), lambda j, i: (j, 0, 0))],
                   out_shape, pl.BlockSpec((tm, 2 * bn), lambda j, i: (i, j)), NN_DIMS,
                   parts=[lambda a_ref, b_ref: (a_ref[...], _two_blocks(b_ref, 0))], after=after)
    tm = _rows_for(m, bn * jnp.dtype(out_dtype).itemsize, 6 * MIB)
    a_spec = pl.BlockSpec((tm, k), lambda j, i: (i, 0))
    b_spec = pl.BlockSpec((None, k, bn), lambda j, i: (j, 0, 0))
    o_spec = pl.BlockSpec((tm, bn), lambda j, i: (i, j))
    return _mm(name, (nb, m // tm), (a, wb), [a_spec, b_spec], out_shape, o_spec, NN_DIMS, after=after)


def _mm_nn_res_norm(name, a, w, res, g):
    m, k = a.shape
    n = w.shape[1]
    tm = _pick(m, (512, 256, 128))

    def body(a_ref, w_ref, r_ref, g_ref, x_ref, h_ref):
        x = jnp.dot(a_ref[...], w_ref[...], preferred_element_type=F32) + r_ref[...]
        x_ref[...] = x
        h_ref[...] = (x * _rstd(x) * g_ref[...]).astype(BF16)

    row = pl.BlockSpec((tm, n), lambda i: (i, 0))
    return pl.pallas_call(
        body, name=name, grid=(m // tm,),
        in_specs=[pl.BlockSpec((tm, k), lambda i: (i, 0)), pl.BlockSpec((k, n), lambda i: (0, 0)), row,
                  pl.BlockSpec((1, n), lambda i: (0, 0))],
        out_specs=(row, row), out_shape=(jax.ShapeDtypeStruct((m, n), F32), jax.ShapeDtypeStruct((m, n), BF16)),
        compiler_params=_cparams(("parallel",), 56),
    )(a, w, res, g)


def _mm_nn_res(name, a, w, res, after=None):
    m, k = a.shape
    n = w.shape[1]
    tm = _pick(m, (512, 256, 128))
    tn = _rows_for(n, k * 2, 12 * MIB)
    a_spec = pl.BlockSpec((tm, k), lambda j, i: (i, 0))
    b_spec = pl.BlockSpec((k, tn), lambda j, i: (0, j))
    r_spec = pl.BlockSpec((tm, tn), lambda j, i: (i, j))
    o_spec = pl.BlockSpec((tm, tn), lambda j, i: (i, j))
    return _mm(name, (n // tn, m // tm), (a, w, res), [a_spec, b_spec, r_spec], jax.ShapeDtypeStruct((m, n), F32),
               o_spec, NN_DIMS, has_res=True, after=after)


def _mm_nt_blocked(name, dy, wb, halves=False, after=None):
    nb, n, bn = wb.shape
    m = dy.shape[-2]
    tm = _pick(m, (512, 256, 128))
    tn = _rows_for(n, nb * bn * 2, 12 * MIB)
    if halves:
        hb = nb // 2
        a_spec = pl.BlockSpec((2, tm, hb * bn), lambda j, i: (0, i, 0))
        a_part = lambda kk: (lambda a_ref: a_ref[kk // hb, :, (kk % hb) * bn:(kk % hb + 1) * bn])
    else:
        a_spec = pl.BlockSpec((tm, nb * bn), lambda j, i: (i, 0))
        a_part = lambda kk: (lambda a_ref: a_ref[:, kk * bn:(kk + 1) * bn])
    if _pairs(nb, bn) and not halves:
        parts = [(lambda a_ref, b_ref, kk=kk: (a_ref[:, kk * bn:(kk + 2) * bn], _two_blocks(b_ref, kk)))
                 for kk in range(0, nb, 2)]
    elif _pairs(nb // 2, bn) and halves:
        parts = [(lambda a_ref, b_ref, kk=kk: (a_ref[kk // hb, :, (kk % hb) * bn:(kk % hb + 2) * bn],
                                               _two_blocks(b_ref, kk))) for kk in range(0, nb, 2)]
    else:
        parts = [(lambda a_ref, b_ref, kk=kk, sel=a_part(kk): (sel(a_ref), b_ref[kk])) for kk in range(nb)]
    b_spec = pl.BlockSpec((nb, tn, bn), lambda j, i: (0, j, 0))
    o_spec = pl.BlockSpec((tm, tn), lambda j, i: (i, j))
    return _mm(name, (n // tn, m // tm), (dy, wb), [a_spec, b_spec], jax.ShapeDtypeStruct((m, n), F32), o_spec,
               NT_DIMS, parts=parts, after=after)


def _mm_nt_plain(name, dy, w, out_dtype=F32, after=None):
    m, k = dy.shape
    n = w.shape[0]
    tm = _rows_for(m, k * 2, 8 * MIB)
    tn = _pick(n, (512, 256, 128))
    a_spec = pl.BlockSpec((tm, k), lambda j, i: (i, 0))
    b_spec = pl.BlockSpec((tn, k), lambda j, i: (j, 0))
    o_spec = pl.BlockSpec((tm, tn), lambda j, i: (i, j))
    return _mm(name, (n // tn, m // tm), (dy, w), [a_spec, b_spec], jax.ShapeDtypeStruct((m, n), out_dtype), o_spec,
               NT_DIMS, after=after)


def _mm_tn_blocked(name, a, dy, nb, halves=False):
    s, k1 = a.shape
    bn = (dy.shape[-1] * (2 if halves else 1)) // nb
    if _pairs(nb, bn) and not halves:
        tm = _rows_for(k1, 2 * bn * 2, 6 * MIB)
        return _mm(name, (nb // 2, k1 // tm), (a, dy),
                   [pl.BlockSpec((s, tm), lambda j, i: (0, i)), pl.BlockSpec((s, 2 * bn), lambda j, i: (0, j))],
                   jax.ShapeDtypeStruct((nb, k1, bn), BF16), pl.BlockSpec((2, tm, bn), lambda j, i: (j, i, 0)), TN_DIMS,
                   split_out=bn)
    if halves and _pairs(nb // 2, bn):
        tm = _rows_for(k1, 2 * bn * 2, 6 * MIB)
        per_half = nb // 4
        return _mm(name, (nb // 2, k1 // tm), (a, dy),
                   [pl.BlockSpec((s, tm), lambda j, i: (0, i)),
                    pl.BlockSpec((None, s, 2 * bn), lambda j, i: (j // per_half, 0, j % per_half))],
                   jax.ShapeDtypeStruct((nb, k1, bn), BF16), pl.BlockSpec((2, tm, bn), lambda j, i: (j, i, 0)), TN_DIMS,
                   split_out=bn)
    tm = _rows_for(k1, bn * 2, 6 * MIB)
    a_spec = pl.BlockSpec((s, tm), lambda j, i: (0, i))
    if halves:
        hb = nb // 2
        b_spec = pl.BlockSpec((None, s, bn), lambda j, i: (j // hb, 0, j % hb))
    else:
        b_spec = pl.BlockSpec((s, bn), lambda j, i: (0, j))
    o_spec = pl.BlockSpec((None, tm, bn), lambda j, i: (j, i, 0))
    return _mm(name, (nb, k1 // tm), (a, dy), [a_spec, b_spec], jax.ShapeDtypeStruct((nb, k1, bn), BF16), o_spec,
               TN_DIMS)


def _mm_tn_plain(name, a, dy):
    s, k1 = a.shape
    n = dy.shape[1]
    tm = _pick(k1, (512, 256, 128))
    tn = _rows_for(n, s * 2, 8 * MIB)
    a_spec = pl.BlockSpec((s, tm), lambda i, j: (0, i))
    b_spec = pl.BlockSpec((s, tn), lambda i, j: (0, j))
    o_spec = pl.BlockSpec((tm, tn), lambda i, j: (i, j))
    return _mm(name, (k1 // tm, n // tn), (a, dy), [a_spec, b_spec], jax.ShapeDtypeStruct((k1, n), BF16), o_spec,
               TN_DIMS)


def _rstd(x):
    return lax.rsqrt(jnp.mean(x * x, axis=-1, keepdims=True) + EPS)


def _norm_bwd(dy, xhat, r, g):
    dxhat = dy * g
    return r * (dxhat - xhat * jnp.mean(dxhat * xhat, axis=-1, keepdims=True))


def _rmsnorm_fwd(name, x, g):
    s, d = x.shape
    tr = _pick(s, (256, 128))

    def body(x_ref, g_ref, h_ref):
        xv = x_ref[...]
        h_ref[...] = (xv * _rstd(xv) * g_ref[...]).astype(BF16)

    return pl.pallas_call(
        body, name=name, grid=(s // tr,),
        in_specs=[pl.BlockSpec((tr, d), lambda i: (i, 0)), pl.BlockSpec((1, d), lambda i: (0, 0))],
        out_specs=pl.BlockSpec((tr, d), lambda i: (i, 0)),
        out_shape=jax.ShapeDtypeStruct((s, d), BF16), compiler_params=_cparams(("parallel",)),
    )(x, g)


def _rmsnorm_bwd(name, dh, x, g, dres):
    s, d = x.shape
    tr = _pick(s, (256, 128))

    def body(dh_ref, x_ref, g_ref, dres_ref, dx_ref, dxb_ref, dg_ref):
        xv = x_ref[...]
        r = _rstd(xv)
        xhat = xv * r
        dhv = dh_ref[...]
        dx = dres_ref[...] + _norm_bwd(dhv, xhat, r, g_ref[...])
        dx_ref[...] = dx
        dxb_ref[...] = dx.astype(BF16)
        part = jnp.sum(dhv * xhat, axis=0, keepdims=True)

        @pl.when(pl.program_id(0) == 0)
        def _():
            dg_ref[...] = part

        @pl.when(pl.program_id(0) > 0)
        def _():
            dg_ref[...] += part

    row = pl.BlockSpec((tr, d), lambda i: (i, 0))
    vec = pl.BlockSpec((1, d), lambda i: (0, 0))
    return pl.pallas_call(
        body, name=name, grid=(s // tr,), in_specs=[row, row, vec, row], out_specs=(row, row, vec),
        out_shape=(jax.ShapeDtypeStruct((s, d), F32), jax.ShapeDtypeStruct((s, d), BF16),
                   jax.ShapeDtypeStruct((1, d), F32)),
        compiler_params=_cparams(("arbitrary",)),
    )(dh, x, g, dres)


def _loss_head(y, target):
    s, d = y.shape
    tr = _pick(s, (256, 128))

    def body(y_ref, t_ref, dy_ref, dyb_ref, loss_ref):
        err = y_ref[...] - t_ref[...]
        dy = err * (1.0 / d)
        dy_ref[...] = dy
        dyb_ref[...] = dy.astype(BF16)
        part = 0.5 * jnp.sum(jnp.mean(err * err, axis=-1, keepdims=True), axis=0, keepdims=True)
        part = jnp.broadcast_to(part, (1, 128))

        @pl.when(pl.program_id(0) == 0)
        def _():
            loss_ref[...] = part

        @pl.when(pl.program_id(0) > 0)
        def _():
            loss_ref[...] += part

    row = pl.BlockSpec((tr, d), lambda i: (i, 0))
    return pl.pallas_call(
        body, name="loss_head", grid=(s // tr,), in_specs=[row, row],
        out_specs=(row, row, pl.BlockSpec((1, 128), lambda i: (0, 0))),
        out_shape=(jax.ShapeDtypeStruct((s, d), F32), jax.ShapeDtypeStruct((s, d), BF16),
                   jax.ShapeDtypeStruct((1, 128), F32)),
        compiler_params=_cparams(("arbitrary",)),
    )(y, target)


def _split_dot(x, tri):
    hi = x.astype(BF16)
    lo = (x - hi.astype(F32)).astype(BF16)
    return (jnp.dot(hi, tri, preferred_element_type=F32) + jnp.dot(lo, tri, preferred_element_type=F32))


def _tile_iotas():
    row = lax.broadcasted_iota(jnp.int32, (TILE, TILE), 0)
    col = lax.broadcasted_iota(jnp.int32, (TILE, TILE), 1)
    return row, col


def _sb_logits(qi, kb, mask):
    z = lax.dot_general(qi, kb, NT_DIMS, preferred_element_type=F32) * (HEAD_DIM ** -0.5)
    sp = jnp.log(1.0 + jnp.exp(-jnp.abs(z)))
    lb = jnp.minimum(z, 0.0) - sp
    l1m = -jnp.maximum(z, 0.0) - sp
    if mask is not None:
        l1m = jnp.where(mask, l1m, 0.0)
    return lb, l1m


def _attn_fwd(p, gq, gk, ga, n_heads):
    s = p.shape[0]
    nq = s // TILE

    hp = ATTN_HEADS_PER_STEP
    wd = hp * HEAD_DIM

    def body(q_ref, k_ref, v_ref, gq_ref, gk_ref, ga_ref, att_ref, o_ref, r_ref, qn_s, kn_s, vb_s):
        heads = [slice(hh * HEAD_DIM, (hh + 1) * HEAD_DIM) for hh in range(hp)]
        for hd in heads:
            qv = q_ref[:, hd]
            qn_s[:, hd] = (qv * _rstd(qv) * gq_ref[...]).astype(BF16)
            kv = k_ref[:, hd]
            kn_s[:, hd] = (kv * _rstd(kv) * gk_ref[...]).astype(BF16)
        vb_s[...] = v_ref[...].astype(BF16)
        row, col = _tile_iotas()
        causal = col < row
        upper_ones = jnp.concatenate([(row > col).astype(BF16), jnp.ones((TILE, TILE), BF16)], axis=1)

        def tiles(rows, key_blocks, states):
            chains = [(hi, hd, keys, mask) for hi, hd in enumerate(heads) for keys, mask in key_blocks]
            logits = [_sb_logits(qn_s[rows, hd], kn_s[keys, hd], mask) for _, hd, keys, mask in chains]
            sums = [_split_dot(l1m, upper_ones) for _, l1m in logits]
            carry = [c for _, c in states]
            probs = []
            for (hi, _, _, mask), (lb, _), sm in zip(chains, logits, sums):
                a = jnp.exp(lb + sm[:, :TILE] + carry[hi])
                carry[hi] = carry[hi] + sm[:, TILE:]
                probs.append((a if mask is None else jnp.where(mask, a, 0.0)).astype(BF16))
            outs = [jnp.dot(a, vb_s[keys, hd], preferred_element_type=F32) for a, (_, hd, keys, _) in zip(probs, chains)]
            acc = [o_acc for o_acc, _ in states]
            for (hi, _, _, _), o in zip(chains, outs):
                acc[hi] = acc[hi] + o
            return tuple(zip(acc, carry))

        def key_block(b):
            return pl.ds(pl.multiple_of(b * TILE, TILE), TILE), None

        def qblock(i, _):
            rows = pl.ds(pl.multiple_of(i * TILE, TILE), TILE)
            zero = jnp.zeros((TILE, HEAD_DIM), F32)
            states = tuple((zero, zero) for _ in heads)
            states = lax.cond(i % 2 == 1, lambda st: tiles(rows, [(rows, causal), key_block(i - 1)], st),
                              lambda st: tiles(rows, [(rows, causal)], st), states)
            top = i - i % 2

            def kblocks(jj, states):
                return tiles(rows, [key_block(top - 1 - 2 * jj), key_block(top - 2 - 2 * jj)], states)

            states = lax.fori_loop(0, i // 2, kblocks, states)
            for hh, (hd, (o_acc, c)) in enumerate(zip(heads, states)):
                o_ref[rows, hd] = o_acc
                r_ref[rows, hd] = c
                att_ref[rows, hd] = (o_acc * _rstd(o_acc) * ga_ref[hh]).astype(BF16)
            return 0

        lax.fori_loop(0, nq, qblock, 0)

    col_blk = lambda off: pl.BlockSpec((s, wd), lambda h: (0, off + h))
    vec = pl.BlockSpec((1, HEAD_DIM), lambda h: (0, 0))
    hvec = pl.BlockSpec((hp, 1, HEAD_DIM), lambda h: (h, 0, 0))
    out = pl.BlockSpec((s, wd), lambda h: (0, h))
    w = n_heads * HEAD_DIM
    steps = n_heads // hp
    return pl.pallas_call(
        body, name="attn_fwd", grid=(steps,),
        in_specs=[col_blk(0), col_blk(steps), col_blk(2 * steps), vec, vec, hvec],
        out_specs=(out, out, out),
        out_shape=(jax.ShapeDtypeStruct((s, w), BF16), jax.ShapeDtypeStruct((s, w), F32),
                   jax.ShapeDtypeStruct((s, w), F32)),
        scratch_shapes=[pltpu.VMEM((s, wd), BF16)] * 3,
        compiler_params=_cparams(("parallel",), ATTN_VMEM_MB),
    )(p, p, p, gq, gk, ga)


def _attn_bwd(p, gq, gk, ga, o, rsum, dmix, n_heads):
    s = p.shape[0]
    nq = s // TILE

    hp = ATTN_HEADS_PER_STEP
    wd = hp * HEAD_DIM
    scale = HEAD_DIM ** -0.5

    def body(q_ref, k_ref, v_ref, gq_ref, gk_ref, ga_ref, o_ref, r_ref, dm_ref,
             dq_ref, dk_ref, dv_ref, dgq_ref, dgk_ref, dga_ref,
             qn_s, kn_s, vb_s, do_s, dqn_s, dkn_s, dv_s):
        step = pl.program_id(0)
        gqv, gkv = gq_ref[...], gk_ref[...]
        heads = [slice(hh * HEAD_DIM, (hh + 1) * HEAD_DIM) for hh in range(hp)]
        chunk = _pick(s, (256, TILE))

        def prepare(c, dga):
            r = pl.ds(pl.multiple_of(c * chunk, chunk), chunk)
            out = []
            for hh, hd in enumerate(heads):
                qv = q_ref[r, hd]
                qn_s[r, hd] = (qv * _rstd(qv) * gqv).astype(BF16)
                kv = k_ref[r, hd]
                kn_s[r, hd] = (kv * _rstd(kv) * gkv).astype(BF16)
                ov = o_ref[r, hd]
                ro = _rstd(ov)
                ohat = ov * ro
                dm = dm_ref[r, hd]
                out.append(dga[hh] + jnp.sum(dm * ohat, axis=0, keepdims=True))
                do_s[r, hd] = _norm_bwd(dm, ohat, ro, ga_ref[hh]).astype(BF16)
            vb_s[r, :] = v_ref[r, :].astype(BF16)
            dkn_s[r, :] = jnp.zeros((chunk, wd), F32)
            dv_s[r, :] = jnp.zeros((chunk, wd), F32)
            return tuple(out)

        dga = lax.fori_loop(0, s // chunk, prepare, tuple(jnp.zeros((1, HEAD_DIM), F32) for _ in heads))
        for hh in range(hp):
            dga_ref[hh] = dga[hh]
        row, col = _tile_iotas()
        causal = col < row
        ones = jnp.ones((TILE, TILE), BF16)
        incl_ones = jnp.concatenate([(row <= col).astype(BF16), ones], axis=1)
        excl_ones = jnp.concatenate([(row < col).astype(BF16), ones], axis=1)

        def tiles(rows, key_blocks, states):
            chains = [(hi, hd, keys, mask) for hi, hd in enumerate(heads) for keys, mask in key_blocks]
            qis = [qn_s[rows, hd] for hd in heads]
            dois = [do_s[rows, hd] for hd in heads]
            logits = [_sb_logits(qis[hi], kn_s[keys, hd], mask) for hi, hd, keys, mask in chains]
            sums = [_split_dot(l1m, incl_ones) for _, l1m in logits]
            das = [lax.dot_general(dois[hi], vb_s[keys, hd], NT_DIMS, preferred_element_type=F32)
                   for hi, hd, keys, _ in chains]
            pfx = [st[1] for st in states]
            probs, dss = [], []
            for (hi, hd, _, mask), (lb, _), sm, da in zip(chains, logits, sums, das):
                a = jnp.exp(lb + (r_ref[rows, hd] - pfx[hi] - sm[:, :TILE]))
                pfx[hi] = pfx[hi] + sm[:, TILE:]
                a = a if mask is None else jnp.where(mask, a, 0.0)
                probs.append(a.astype(BF16))
                dss.append(da * a)
            dsums = [_split_dot(ds, excl_ones) for ds in dss]
            pc = [st[2] for st in states]
            dzs = []
            for (hi, _, _, mask), (lb, _), ds, dsm in zip(chains, logits, dss, dsums):
                dl1m = pc[hi] + dsm[:, :TILE]
                pc[hi] = pc[hi] + dsm[:, TILE:]
                dl1m = dl1m if mask is None else jnp.where(mask, dl1m, 0.0)
                beta = jnp.exp(lb)
                dzs.append(((ds * (1.0 - beta) - dl1m * beta) * scale).astype(BF16))
            dqs = [jnp.dot(dz, kn_s[keys, hd], preferred_element_type=F32) for dz, (_, hd, keys, _) in zip(dzs, chains)]
            for dz, a, (hi, hd, keys, _) in zip(dzs, probs, chains):
                dkn_s[keys, hd] += lax.dot_general(dz, qis[hi], TN_DIMS, preferred_element_type=F32)
                dv_s[keys, hd] += lax.dot_general(a, dois[hi], TN_DIMS, preferred_element_type=F32)
            dq_acc = [st[0] for st in states]
            for (hi, _, _, _), dq in zip(chains, dqs):
                dq_acc[hi] = dq_acc[hi] + dq
            return tuple(zip(dq_acc, pfx, pc))

        def key_block(b):
            return pl.ds(pl.multiple_of(b * TILE, TILE), TILE), None

        def qblock(i, _):
            rows = pl.ds(pl.multiple_of(i * TILE, TILE), TILE)
            zero = jnp.zeros((TILE, HEAD_DIM), F32)

            def kblocks(jj, states):
                return tiles(rows, [key_block(2 * jj), key_block(2 * jj + 1)], states)

            states = lax.fori_loop(0, i // 2, kblocks, tuple((zero, zero, zero) for _ in heads))
            states = lax.cond(i % 2 == 1, lambda st: tiles(rows, [key_block(i - 1), (rows, causal)], st),
                              lambda st: tiles(rows, [(rows, causal)], st), states)
            for hd, (dq_acc, _, _) in zip(heads, states):
                dqn_s[rows, hd] = dq_acc
            return 0

        lax.fori_loop(0, nq, qblock, 0)

        def norm_in_bwd(x_ref, g, dn_s, dx_ref, rows, part):
            for hd in heads:
                xv = x_ref[rows, hd]
                r = _rstd(xv)
                xhat = xv * r
                dn = dn_s[rows, hd]
                dx_ref[rows, hd] = _norm_bwd(dn, xhat, r, g).astype(BF16)
                part = part + jnp.sum(dn * xhat, axis=0, keepdims=True)
            return part

        def finish(c, parts):
            rows = pl.ds(pl.multiple_of(c * chunk, chunk), chunk)
            dv_ref[rows, :] = dv_s[rows, :].astype(BF16)
            return (norm_in_bwd(q_ref, gqv, dqn_s, dq_ref, rows, parts[0]),
                    norm_in_bwd(k_ref, gkv, dkn_s, dk_ref, rows, parts[1]))

        zero_row = jnp.zeros((1, HEAD_DIM), F32)
        for dg_ref, part in zip((dgq_ref, dgk_ref), lax.fori_loop(0, s // chunk, finish, (zero_row, zero_row))):
            @pl.when(step == 0)
            def _(dg_ref=dg_ref, part=part):
                dg_ref[...] = part

            @pl.when(step > 0)
            def _(dg_ref=dg_ref, part=part):
                dg_ref[...] += part

    once = pl.Buffered(1)
    steps = n_heads // hp
    col_blk = lambda off: pl.BlockSpec((s, wd), lambda h: (0, off + h), pipeline_mode=once)
    vec = pl.BlockSpec((1, HEAD_DIM), lambda h: (0, 0))
    hvec = pl.BlockSpec((hp, 1, HEAD_DIM), lambda h: (h, 0, 0))
    blk = pl.BlockSpec((s, wd), lambda h: (0, h), pipeline_mode=once)
    w = n_heads * HEAD_DIM
    big = jax.ShapeDtypeStruct((s, w), BF16)
    return pl.pallas_call(
        body, name="attn_bwd", grid=(steps,),
        in_specs=[col_blk(0), col_blk(steps), col_blk(2 * steps), vec, vec, hvec, blk, blk, blk],
        out_specs=(blk, blk, blk, vec, vec, hvec),
        out_shape=(big, big, big, jax.ShapeDtypeStruct((1, HEAD_DIM), F32), jax.ShapeDtypeStruct((1, HEAD_DIM), F32),
                   jax.ShapeDtypeStruct((n_heads, 1, HEAD_DIM), F32)),
        scratch_shapes=[pltpu.VMEM((s, wd), BF16)] * 4 + [pltpu.VMEM((s, wd), F32)] * 3,
        compiler_params=_cparams(("arbitrary",), ATTN_VMEM_MB),
    )(p, p, p, gq, gk, ga, o, rsum, dmix)


_INV_SQRT2 = 0.7071067811865476
_INV_SQRT_2PI = 0.3989422804014327


def _gelu(x):
    return 0.5 * x * (1.0 + lax.erf(x * _INV_SQRT2))


def _gelu_grad(x):
    return 0.5 * (1.0 + lax.erf(x * _INV_SQRT2)) + x * (_INV_SQRT_2PI * jnp.exp(-0.5 * x * x))


def _sgu_fwd(p, gs, w_s, b_s, gb, col0, after):
    s = p.shape[0]
    n_chunks = s // TILE
    per_trip = _pick(n_chunks, (4, 2, 1))

    def body(u_ref, v_ref, gs_ref, w_ref, b_ref, gb_ref, _, out_ref, vs_s):
        vg = _gelu(v_ref[...])
        vs_s[...] = (vg * _rstd(vg) * gs_ref[...]).astype(BF16)
        row, col = _tile_iotas()
        wt = jnp.where(col <= row, w_ref[...], 0.0).astype(BF16)
        bcol = b_ref[...]
        gbv = gb_ref[...]

        def chunks(c, _):
            rows = [pl.ds(pl.multiple_of((c * per_trip + k) * TILE, TILE), TILE) for k in range(per_trip)]
            mixed = [jnp.dot(wt, vs_s[r, :], preferred_element_type=F32) + bcol for r in rows]
            sgs = [_gelu(u_ref[r, :]) * mx for r, mx in zip(rows, mixed)]
            for r, sg in zip(rows, sgs):
                out_ref[r, :] = (sg * _rstd(sg) * gbv).astype(BF16)
            return 0

        lax.fori_loop(0, n_chunks // per_trip, chunks, 0)

    col_blk = lambda off: pl.BlockSpec((s, HEAD_DIM), lambda g: (0, off + g))
    gvec = pl.BlockSpec((None, 1, HEAD_DIM), lambda g: (g, 0, 0))
    return pl.pallas_call(
        body, name="sgu_fwd", grid=(N_GROUPS,),
        in_specs=[col_blk(col0), col_blk(col0 + N_GROUPS), gvec,
                  pl.BlockSpec((None, TILE, TILE), lambda g: (g, 0, 0)),
                  pl.BlockSpec((None, TILE, 1), lambda g: (g, 0, 0)), gvec,
                  pl.BlockSpec(after.shape, lambda g: (0, 0))],
        out_specs=pl.BlockSpec((s, HEAD_DIM), lambda g: (0, g)),
        out_shape=jax.ShapeDtypeStruct((s, N_GROUPS * HEAD_DIM), BF16),
        scratch_shapes=[pltpu.VMEM((s, HEAD_DIM), BF16)],
        compiler_params=_cparams(("parallel",)),
    )(p, p, gs, w_s, b_s, gb, after)


def _sgu_bwd(p, gs, w_s, b_s, gb, dmix, col0, dm_col0):
    s = p.shape[0]
    n_chunks = s // TILE
    per_trip = _pick(n_chunks, (4, 2, 1))

    def body(u_ref, v_ref, gs_ref, w_ref, b_ref, gb_ref, dm_ref,
             du_ref, dv_ref, dgs_ref, dw_ref, db_ref, dgb_ref, vs_s, dvs_s):
        gsv = gs_ref[...]
        gbv = gb_ref[...]
        vg = _gelu(v_ref[...])
        vs_s[...] = (vg * _rstd(vg) * gsv).astype(BF16)
        row, col = _tile_iotas()
        causal = col <= row
        wt = jnp.where(causal, w_ref[...], 0.0).astype(BF16)
        bcol = b_ref[...]

        def chunks(c, carry):
            dw_acc, db_acc, dgb_acc = carry
            rows = [pl.ds(pl.multiple_of((c * per_trip + k) * TILE, TILE), TILE) for k in range(per_trip)]
            vss = [vs_s[r, :] for r in rows]
            mixed = [jnp.dot(wt, vs, preferred_element_type=F32) + bcol for vs in vss]
            dmbs = []
            for r, mx in zip(rows, mixed):
                u_pre = u_ref[r, :]
                u = _gelu(u_pre)
                sg = u * mx
                rs = _rstd(sg)
                sghat = sg * rs
                dm = dm_ref[r, :]
                dsg = _norm_bwd(dm, sghat, rs, gbv)
                dgb_acc = dgb_acc + jnp.sum(dm * sghat, axis=0, keepdims=True)
                du_ref[r, :] = (dsg * mx * _gelu_grad(u_pre)).astype(BF16)
                dmixed = dsg * u
                db_acc = db_acc + jnp.sum(dmixed, axis=1, keepdims=True)
                dmbs.append(dmixed.astype(BF16))
            for dmb, vs in zip(dmbs, vss):
                dw_acc = dw_acc + lax.dot_general(dmb, vs, NT_DIMS, preferred_element_type=F32)
            for r, dmb in zip(rows, dmbs):
                dvs_s[r, :] = lax.dot_general(wt, dmb, TN_DIMS, preferred_element_type=F32)
            return dw_acc, db_acc, dgb_acc

        dw_acc, db_acc, dgb_acc = lax.fori_loop(
            0, n_chunks // per_trip, chunks,
            (jnp.zeros((TILE, TILE), F32), jnp.zeros((TILE, 1), F32), jnp.zeros((1, HEAD_DIM), F32)))
        dw_ref[...] = jnp.where(causal, dw_acc, 0.0)
        db_ref[...] = db_acc
        dgb_ref[...] = dgb_acc
        v_pre = v_ref[...]
        vg = _gelu(v_pre)
        rv = _rstd(vg)
        vhat = vg * rv
        dvs = dvs_s[...]
        dgs_ref[...] = jnp.sum(dvs * vhat, axis=0, keepdims=True)
        dv_ref[...] = (_norm_bwd(dvs, vhat, rv, gsv) * _gelu_grad(v_pre)).astype(BF16)

    col_blk = lambda off: pl.BlockSpec((s, HEAD_DIM), lambda g: (0, off + g))
    gvec = pl.BlockSpec((None, 1, HEAD_DIM), lambda g: (g, 0, 0))
    wspec = pl.BlockSpec((None, TILE, TILE), lambda g: (g, 0, 0))
    bspec = pl.BlockSpec((None, TILE, 1), lambda g: (g, 0, 0))
    blk = pl.BlockSpec((s, HEAD_DIM), lambda g: (0, g))
    big = jax.ShapeDtypeStruct((s, N_GROUPS * HEAD_DIM), BF16)
    gshape = jax.ShapeDtypeStruct((N_GROUPS, 1, HEAD_DIM), F32)
    return pl.pallas_call(
        body, name="sgu_bwd", grid=(N_GROUPS,),
        in_specs=[col_blk(col0), col_blk(col0 + N_GROUPS), gvec, wspec, bspec, gvec, col_blk(dm_col0)],
        out_specs=(blk, blk, gvec, wspec, bspec, gvec),
        out_shape=(big, big, gshape, jax.ShapeDtypeStruct((N_GROUPS, TILE, TILE), F32),
                   jax.ShapeDtypeStruct((N_GROUPS, TILE, 1), F32), gshape),
        scratch_shapes=[pltpu.VMEM((s, HEAD_DIM), BF16), pltpu.VMEM((s, HEAD_DIM), F32)],
        compiler_params=_cparams(("parallel",)),
    )(p, p, gs, w_s, b_s, gb, dmix)


SUBLANES = 8


def _shift_down(x, n):
    rolled = pltpu.roll(x, n, 0)
    edge = lax.broadcasted_iota(jnp.int32, (SUBLANES, x.shape[1]), 0)
    return jnp.concatenate([jnp.where(edge >= n, rolled[:SUBLANES], 0.0), rolled[SUBLANES:]], axis=0)


def _shift_up(x, n):
    s = x.shape[0]
    rolled = pltpu.roll(x, s - n, 0)
    edge = lax.broadcasted_iota(jnp.int32, (SUBLANES, x.shape[1]), 0)
    return jnp.concatenate([rolled[:s - SUBLANES], jnp.where(edge < SUBLANES - n, rolled[s - SUBLANES:], 0.0)], axis=0)


def _conv(x, w, b):
    x1, x2 = _shift_down(x, 1), _shift_down(x, 2)
    return b + w[0:1, :] * x2 + w[1:2, :] * x1 + w[2:3, :] * x, x1, x2


def _conv_specs(s, tn):
    xspec = pl.BlockSpec((2, s, tn), lambda j: (0, 0, j))
    wspec = pl.BlockSpec((2, CONV_WIDTH, tn), lambda j: (0, 0, j))
    bspec = pl.BlockSpec((2, 1, tn), lambda j: (0, 0, j))
    return xspec, wspec, bspec


def _up_conv_gate_fwd(h, wb, cw, cb, after=None):
    s, k = h.shape
    nb, _, bn = wb.shape
    hb = nb // 2
    f = hb * bn
    tm = _pick(s, (512, 256, 128))
    n_in = 5 + (after is not None)

    def body(*refs):
        h_ref, wg_ref, wv_ref, w_ref, b_ref = refs[:5]
        up_ref, act_ref, halo_s = refs[n_in:]
        first = pl.program_id(1) == 0
        outs = []
        for half, wt_ref in enumerate((wg_ref, wv_ref)):
            x = jnp.dot(h_ref[...], wt_ref[...], preferred_element_type=F32)
            up_ref[half] = x
            halo = jnp.where(first, 0.0, halo_s[half])
            halo_s[half] = x[tm - SUBLANES:]
            full = jnp.concatenate([halo, x], axis=0)
            x1 = pltpu.roll(full, 1, 0)[SUBLANES:]
            x2 = pltpu.roll(full, 2, 0)[SUBLANES:]
            w = w_ref[half]
            outs.append(b_ref[half] + w[0:1, :] * x2 + w[1:2, :] * x1 + w[2:3, :] * x)
        gate, val = outs
        act_ref[...] = (gate * jax.nn.sigmoid(gate) * val).astype(BF16)

    ins = [h, wb, wb, cw, cb]
    in_specs = [pl.BlockSpec((tm, k), lambda j, i: (i, 0)),
                pl.BlockSpec((None, k, bn), lambda j, i: (j, 0, 0)),
                pl.BlockSpec((None, k, bn), lambda j, i: (j + hb, 0, 0)),
                pl.BlockSpec((2, CONV_WIDTH, bn), lambda j, i: (0, 0, j)),
                pl.BlockSpec((2, 1, bn), lambda j, i: (0, 0, j))]
    if after is not None:
        ins.append(after)
        in_specs.append(pl.BlockSpec(after.shape, lambda j, i: (0, 0)))
    return pl.pallas_call(
        body, name="up_conv_gate_fwd", grid=(hb, s // tm), in_specs=in_specs,
        out_specs=(pl.BlockSpec((2, tm, bn), lambda j, i: (0, i, j)), pl.BlockSpec((tm, bn), lambda j, i: (i, j))),
        out_shape=(jax.ShapeDtypeStruct((2, s, f), F32), jax.ShapeDtypeStruct((s, f), BF16)),
        scratch_shapes=[pltpu.VMEM((2, SUBLANES, bn), F32)],
        compiler_params=_cparams(("parallel", "arbitrary"), 56),
    )(*ins)


def _down_dx_conv_gate_bwd(up, cw, cb, dy, wdown, after):
    _, s, f = up.shape
    d = dy.shape[1]
    tn = _pick(f, (256, 128))

    def body(x_ref, w_ref, b_ref, dy_ref, wd_ref, _, dx_ref, dw_ref, db_ref):
        da = lax.dot_general(dy_ref[...], wd_ref[...], NT_DIMS, preferred_element_type=F32)
        xg, xv = x_ref[0], x_ref[1]
        wg, wv = w_ref[0], w_ref[1]
        gate, xg1, xg2 = _conv(xg, wg, b_ref[0])
        val, xv1, xv2 = _conv(xv, wv, b_ref[1])
        sig = jax.nn.sigmoid(gate)
        dval = da * (gate * sig)
        dgate = da * val * (sig * (1.0 + gate * (1.0 - sig)))
        for half, (x, x1, x2, w, dz) in enumerate(((xg, xg1, xg2, wg, dgate), (xv, xv1, xv2, wv, dval))):
            dx_ref[half] = (w[2:3, :] * dz + w[1:2, :] * _shift_up(dz, 1) + w[0:1, :] * _shift_up(dz, 2)).astype(BF16)
            dw_ref[half, 0:1, :] = jnp.sum(dz * x2, axis=0, keepdims=True)
            dw_ref[half, 1:2, :] = jnp.sum(dz * x1, axis=0, keepdims=True)
            dw_ref[half, 2:3, :] = jnp.sum(dz * x, axis=0, keepdims=True)
            db_ref[half] = jnp.sum(dz, axis=0, keepdims=True)

    xspec, wspec, bspec = _conv_specs(s, tn)
    return pl.pallas_call(
        body, name="down_dx_conv_gate_bwd", grid=(f // tn,),
        in_specs=[xspec, wspec, bspec, pl.BlockSpec((s, d), lambda j: (0, 0)), pl.BlockSpec((tn, d), lambda j: (j, 0)),
                  pl.BlockSpec(after.shape, lambda j: (0, 0))],
        out_specs=(xspec, wspec, bspec),
        out_shape=(jax.ShapeDtypeStruct((2, s, f), BF16), jax.ShapeDtypeStruct((2, CONV_WIDTH, f), F32),
                   jax.ShapeDtypeStruct((2, 1, f), F32)),
        compiler_params=_cparams(("parallel",), 56),
    )(up, cw, cb, dy, wdown, after)


def _mesh_pos():
    return lax.axis_index("x"), lax.axis_index("y"), lax.axis_index("c")


def _remote(src, dst, send_sem, recv_sem, to):
    return pltpu.make_async_remote_copy(src_ref=src, dst_ref=dst, send_sem=send_sem, recv_sem=recv_sem,
                                        device_id=to, device_id_type=pl.DeviceIdType.MESH)


HBM_SPEC = pl.BlockSpec(memory_space=pltpu.HBM)
SEM_SPEC = pl.BlockSpec(memory_space=pltpu.SEMAPHORE)
ANY_SPEC = pl.BlockSpec(memory_space=pl.ANY)
TOKEN_SPEC = pl.BlockSpec(memory_space=pltpu.VMEM)
TOKEN_SHAPE = jax.ShapeDtypeStruct((8, 128), F32)
DATAFLOW = pltpu.SideEffectType.DATAFLOW_SIDE_EFFECTING
GATHER_PLANE = (2, 4, 6)


def _slot(pos):
    return 4 * pos[0] + 2 * pos[1] + pos[2]


def _flip(pos, k):
    return (pos[0] ^ ((k >> 2) & 1), pos[1] ^ ((k >> 1) & 1), pos[2] ^ (k & 1))


def _hbm(a):
    return pltpu.with_memory_space_constraint(a, pltpu.HBM)


def _hbm_shapes(arrays):
    return tuple(pltpu.HBM(a.shape, a.dtype) for a in arrays)


class _Split:
    def __init__(self, n, outs, n_sets):
        k = 2 * n * int(n_sets)
        self.n = n
        self.sems = list(outs[:k])
        self.bufs = list(outs[k:k + 2 * n])
        self.token = outs[-1]

    def sem_set(self, i):
        return self.sems[2 * self.n * i:2 * self.n * (i + 1)]


def _split_call(name, body, bufs, sems_in, n_sets, after):
    n = len(bufs) // 2
    k = 2 * n * int(n_sets)
    m = len(sems_in)
    afters = list(after) if isinstance(after, (list, tuple)) else [after]
    na = len(afters)

    def wrapped(*refs):
        srcs, dsts = refs[:n], refs[n:2 * n]
        s_in = refs[2 * n:2 * n + m]
        s_out = refs[2 * n + m + na:2 * n + m + na + k]
        token, local_sems = refs[-2], refs[-1]
        body(srcs, dsts, s_in, s_out, local_sems)
        token[...] = jnp.zeros_like(token)

    outs = pl.pallas_call(
        wrapped, name=name,
        out_shape=(pltpu.SemaphoreType.DMA(()),) * k + _hbm_shapes(bufs) + (TOKEN_SHAPE,),
        in_specs=[HBM_SPEC] * (2 * n) + [SEM_SPEC] * m + [ANY_SPEC] * na,
        out_specs=(SEM_SPEC,) * k + (HBM_SPEC,) * (2 * n) + (TOKEN_SPEC,),
        input_output_aliases={i: k + i for i in range(2 * n)},
        scratch_shapes=[pltpu.SemaphoreType.DMA((n,))],
        compiler_params=pltpu.CompilerParams(has_side_effects=DATAFLOW),
    )(*[_hbm(b) for b in bufs], *sems_in, *afters)
    return _Split(n, outs, n_sets)


def _wait_slots(land, count, send_sem, recv_sem, me, send=False, recv=False):
    span = land.at[pl.ds(0, count)]
    cp = _remote(span, span, send_sem, recv_sem, me)
    if send:
        cp.wait_send()
    if recv:
        cp.wait_recv()


X_FLIP, Y_FLIP, DIAG_FLIP = 4, 2, 6
BF16_SUBLANES = 16


def _gather_start(name, shards, after):
    n = len(shards)
    my_slot = _slot(_mesh_pos())
    lands = [lax.dynamic_update_slice(lax.empty((N_DEV,) + w.shape, w.dtype), w[None], (my_slot, 0, 0)) for w in shards]

    def body(srcs, dsts, _, sems, local_sems):
        me = _mesh_pos()
        for a in range(n):
            for k in (1, X_FLIP, Y_FLIP):
                _remote(srcs[a], dsts[a].at[_slot(me)], sems[a], sems[n + a], _flip(me, k)).start()

    return _split_call(name, body, list(shards) + lands, [], 1, after)


def _gather_relay(name, started, after):
    n = started.n

    def body(srcs, dsts, sems_a, sems_out, local_sems):
        me = _mesh_pos()
        sibling = _flip(me, 1)
        pass_on, relay = sems_out[:2 * n], sems_out[2 * n:]
        for a in range(n):
            _wait_slots(dsts[a], 3, sems_a[a], sems_a[n + a], me, recv=True)
            from_x = dsts[a].at[_slot(_flip(me, X_FLIP))]
            from_y = dsts[a].at[_slot(_flip(me, Y_FLIP))]
            rows = srcs[a].shape[0]
            if rows % (2 * BF16_SUBLANES) == 0:
                top, bottom = pl.ds(0, rows // 2), pl.ds(rows // 2, rows // 2)
                _remote(from_y.at[top], from_y.at[top], relay[a], relay[n + a], _flip(me, X_FLIP)).start()
                _remote(from_x.at[bottom], from_x.at[bottom], relay[a], relay[n + a], _flip(me, Y_FLIP)).start()
            else:
                _remote(from_y, from_y, relay[a], relay[n + a], _flip(me, X_FLIP)).start()
            for block in (from_x, from_y):
                _remote(block, block, pass_on[a], pass_on[n + a], sibling).start()
        for a in range(n):
            _wait_slots(dsts[a], 3, sems_a[a], sems_a[n + a], me, send=True)

    return _split_call(name, body, started.bufs, started.sems, 2, after)


def _gather_relay_diagonal(name, relayed, after):
    n = relayed.n

    def body(srcs, dsts, relay, pass_on, local_sems):
        me = _mesh_pos()
        for a in range(n):
            _wait_slots(dsts[a], 1, relay[a], relay[n + a], me, recv=True)
            block = dsts[a].at[_slot(_flip(me, DIAG_FLIP))]
            _remote(block, block, pass_on[a], pass_on[n + a], _flip(me, 1)).start()
        for a in range(n):
            _wait_slots(dsts[a], 1, relay[a], relay[n + a], me, send=True)

    return _split_call(name, body, relayed.bufs, relayed.sem_set(1), 1, after)


def _gather_finish(name, relayed, diagonal, after):
    n = relayed.n

    def body(srcs, dsts, sems, _, local_sems):
        me = _mesh_pos()
        first, second = sems[:2 * n], sems[2 * n:]
        for a in range(n):
            _wait_slots(dsts[a], 2, first[a], first[n + a], me, send=True, recv=True)
            _wait_slots(dsts[a], 1, second[a], second[n + a], me, send=True, recv=True)

    return _split_call(name, body, diagonal.bufs, relayed.sem_set(0) + diagonal.sems, 0, after).bufs[n:]


def _exchange_start(name, blocked, after):
    n = len(blocked)
    my_slot = _slot(_mesh_pos())
    rows = [w.shape[-2] // (N_DEV if w.ndim == 2 else 1) for w in blocked]

    def block(ref, a, slot):
        if len(ref.shape) == 3:
            return ref.at[slot]
        return ref.at[pl.ds(pl.multiple_of(slot * rows[a], 16), rows[a])]

    lands = []
    for w, r in zip(blocked, rows):
        mine = lax.dynamic_slice_in_dim(w, my_slot, 1, 0) if w.ndim == 3 else lax.dynamic_slice_in_dim(w, my_slot * r, r, 0)[None]
        lands.append(lax.dynamic_update_slice(lax.empty((N_DEV, r, w.shape[-1]), w.dtype), mine, (my_slot, 0, 0)))

    def body(srcs, dsts, _, sems, local_sems):
        me = _mesh_pos()
        for a in range(n):
            for k in range(1, N_DEV):
                peer = _flip(me, k)
                _remote(block(srcs[a], a, _slot(peer)), dsts[a].at[_slot(me)], sems[a], sems[n + a], peer).start()

    return _split_call(name, body, list(blocked) + lands, [], True, after)


def _exchange_finish(name, started, after):
    group = started if isinstance(started, (list, tuple)) else [started]
    srcs_all = [b for st in group for b in st.bufs[:st.n]]
    lands_all = [b for st in group for b in st.bufs[st.n:]]
    sends = [s for st in group for s in st.sems[:st.n]]
    recvs = [s for st in group for s in st.sems[st.n:]]
    n = len(srcs_all)

    def body(srcs, dsts, sems, _, local_sems):
        me = _mesh_pos()
        for a in range(n):
            _wait_slots(dsts[a], N_DEV - 1, sems[a], sems[n + a], me, send=True, recv=True)

    return _split_call(name, body, srcs_all + lands_all, sends + recvs, 0, after).bufs[n:]


def _broadcast_start(name, arrays, after):
    n = len(arrays)
    my_slot = _slot(_mesh_pos())
    lands = [lax.dynamic_update_slice(lax.empty((N_DEV,) + w.shape, w.dtype), w[None], (my_slot, 0, 0)) for w in arrays]

    def body(srcs, dsts, _, sems, local_sems):
        me = _mesh_pos()
        for a in range(n):
            for k in range(1, N_DEV):
                _remote(srcs[a], dsts[a].at[_slot(me)], sems[a], sems[n + a], _flip(me, k)).start()

    return _split_call(name, body, list(arrays) + lands, [], True, after)


def _adamw_math(w, g, m, v):
    m = ADAM_B1 * m + (1.0 - ADAM_B1) * g
    v = ADAM_B2 * v + (1.0 - ADAM_B2) * (g * g)
    m_hat = m / (1.0 - ADAM_B1 ** ADAM_STEP)
    v_hat = v / (1.0 - ADAM_B2 ** ADAM_STEP)
    delta = -ADAM_LR * (m_hat / (jnp.sqrt(v_hat) + ADAM_EPS) + ADAM_WD * w)
    return delta, m, v


def _adamw(name, w, m, v, parts, layer, prev=None):
    _, r, c = w.shape
    tr = max(t for t in range(16, r + 1, 16) if r % t == 0 and t * c <= ADAMW_TILE_ELEMS)
    n_prev = 0 if prev is None else 4

    def body(*refs):
        w_ref, m_ref, v_ref, p_ref = refs[:4]
        g_ref, d_ref, nm_ref, nv_ref = refs[4 + n_prev:]
        g = p_ref[0].astype(F32)
        for src in range(1, N_DEV):
            g = g + p_ref[src].astype(F32)
        delta, nm, nv = _adamw_math(w_ref[...], g, m_ref[...], v_ref[...])
        g_ref[...] = g
        d_ref[...] = delta
        nm_ref[...] = nm
        nv_ref[...] = nv

    wspec = pl.BlockSpec((None, tr, c), lambda i: (layer, i, 0))
    pspec = pl.BlockSpec((N_DEV, tr, c), lambda i: (0, i, 0))
    shp = jax.ShapeDtypeStruct(w.shape, F32)
    return pl.pallas_call(
        body, name=name, grid=(r // tr,), in_specs=[wspec] * 3 + [pspec] + [ANY_SPEC] * n_prev,
        out_specs=(wspec,) * 4, out_shape=(shp,) * 4, input_output_aliases={4 + j: j for j in range(n_prev)},
        compiler_params=_cparams(("parallel",), 56),
    )(w, m, v, parts, *([] if prev is None else prev))


PACK_TILE = 8 * 128


def _pack(arrays):
    flat = []
    for a in arrays:
        v = a.reshape(-1)
        pad = (-v.shape[0]) % PACK_TILE
        flat.append(jnp.pad(v, (0, pad)) if pad else v)
    return jnp.concatenate(flat).reshape(-1, 128)


def _unpack(buf, like):
    flat = buf.reshape(-1)
    out, off = [], 0
    for a in like:
        n = 1
        for dim in a.shape:
            n *= dim
        out.append(flat[off:off + n].reshape(a.shape))
        off += n + (-n) % PACK_TILE
    return out


def _sum_slots(name, gathered):
    _, r, c = gathered.shape

    def body(x_ref, o_ref):
        acc = x_ref[0].astype(F32)
        for src in range(1, N_DEV):
            acc = acc + x_ref[src].astype(F32)
        o_ref[...] = acc

    return pl.pallas_call(body, name=name, out_shape=jax.ShapeDtypeStruct((r, c), F32))(gathered)


def _adamw_small(w, g, m, v):
    shp = jax.ShapeDtypeStruct(w.shape, F32)

    def body(w_ref, g_ref, m_ref, v_ref, d_ref, nm_ref, nv_ref):
        delta, nm, nv = _adamw_math(w_ref[...], g_ref[...], m_ref[...], v_ref[...])
        d_ref[...] = delta
        nm_ref[...] = nm
        nv_ref[...] = nv

    return pl.pallas_call(body, name="adamw_small", out_shape=(shp,) * 3)(w, g, m, v)


def kernel(x, attn_norm_g, w_in, q_norm_g, k_norm_g, sgu_norm_g, sgu_w, sgu_b, out_norm_a_g, out_norm_b_g, w_out, ffn_norm_g, w_up, conv_w, conv_b, w_down, loss_target, m_attn_norm_g, m_w_in, m_q_norm_g, m_k_norm_g, m_sgu_norm_g, m_sgu_w, m_sgu_b, m_out_norm_a_g, m_out_norm_b_g, m_w_out, m_ffn_norm_g, m_w_up, m_conv_w, m_conv_b, m_w_down, v_attn_norm_g, v_w_in, v_q_norm_g, v_k_norm_g, v_sgu_norm_g, v_sgu_w, v_sgu_b, v_out_norm_a_g, v_out_norm_b_g, v_w_out, v_ffn_norm_g, v_w_up, v_conv_w, v_conv_b, v_w_down):
    depth = w_in.shape[0]
    s, d = x.shape[1], x.shape[2]
    n_heads = (d // 2) // HEAD_DIM
    sgu_col0 = 3 * n_heads
    f2 = w_up.shape[2] * N_DEV
    ff = f2 // 2
    my_slot = 4 * lax.axis_index("x") + 2 * lax.axis_index("y") + lax.axis_index("c")

    wb = [(w_in[l].astype(BF16), w_out[l].astype(BF16), w_up[l].astype(BF16), w_down[l].astype(BF16))
          for l in range(depth)]
    groups = {"in0": [wb[0][0]], "out0": [wb[0][1], conv_w.reshape(depth * CONV_WIDTH, -1)], "up0": [wb[0][2]],
              "down0": [wb[0][3]]}
    for l in range(1, depth):
        groups[f"in{l}"] = [wb[l][0], wb[l][1]]
        groups[f"ffn{l}"] = [wb[l][2], wb[l][3]]
    order = list(groups)
    started, relayed = {}, {}

    def start(gname, after):
        started[gname] = _gather_start(f"gather_{gname}_start", groups[gname], after)
        return started[gname].token

    def relay(gname, after):
        relayed[gname] = _gather_relay(f"gather_{gname}_relay", started[gname], after)
        token = relayed[gname].token
        k = order.index(gname)
        nxt = [k + 2] if k + 2 < len(order) - 1 else []
        if k == len(order) - 2:
            nxt = [k + 1]
        for j in nxt:
            token = start(order[j], token)
        return token

    def finish(gname, after):
        diagonal = _gather_relay_diagonal(f"gather_{gname}_diagonal", relayed[gname], after)
        return _gather_finish(f"gather_{gname}_finish", relayed[gname], diagonal, diagonal.token)

    conv_b_all = conv_b.reshape(depth, 2, 1, ff)
    sgu_b_col = sgu_b[..., None]
    token = start(order[1], start(order[0], attn_norm_g))
    token = relay("out0", relay("in0", token))
    win_g = finish("in0", token)[0]

    xs = x[0]
    saved = []
    gathered = []
    for l in range(depth):
        g1 = attn_norm_g[l][None]
        g2 = ffn_norm_g[l][None]
        gq, gk = q_norm_g[l][None], k_norm_g[l][None]
        ga = out_norm_a_g[l][:, None, :]
        gs = sgu_norm_g[l][:, None, :]
        gb = out_norm_b_g[l][:, None, :]
        h1 = _rmsnorm_fwd("attn_norm_fwd", xs, g1)
        p = _mm_nn_blocked("in_proj", h1, win_g, F32)
        att, o, rsum = _attn_fwd(p, gq, gk, ga, n_heads)
        token = relay("up0" if l == 0 else f"ffn{l}", att)
        sg = _sgu_fwd(p, gs, sgu_w[l], sgu_b_col[l], gb, sgu_col0, token)
        mix = jnp.concatenate([att, sg], axis=-1)
        if l == 0:
            wout_g, cw = finish("out0", mix)
            cw = jnp.transpose(cw.reshape(N_DEV, depth, CONV_WIDTH, -1), (1, 2, 0, 3)).reshape(depth, CONV_WIDTH, 2, ff)
            conv_w_all = jnp.transpose(cw, (0, 2, 1, 3))
        x1, h2 = _mm_nn_res_norm("out_proj_ffn_norm", mix, wout_g.reshape(d, d), xs, g2)
        if l == 0:
            wup_g = finish("up0", h2)[0]
            token = relay("down0", wup_g)
            up, act = _up_conv_gate_fwd(h2, wup_g, conv_w_all[l], conv_b_all[l], after=token)
            wdown_g = finish("down0", up)[0]
        else:
            wup_g, wdown_g = finish(f"ffn{l}", h2)
            up, act = _up_conv_gate_fwd(h2, wup_g, conv_w_all[l], conv_b_all[l])
        saved.append((xs, h1, p, o, rsum, mix, x1, h2, up, act))
        gathered.append((win_g, wout_g, wup_g, wdown_g))
        if l + 1 < depth:
            token = relay(f"in{l + 1}", act)
            x2 = _mm_nn_res("down_proj", act, wdown_g.reshape(ff, d), x1, after=token)
            win_g, wout_g = finish(f"in{l + 1}", x2)
        else:
            x2 = _mm_nn_res("down_proj", act, wdown_g.reshape(ff, d), x1)
        xs = x2

    dx, dxb, loss_vec = _loss_head(xs, loss_target[0])
    loss = lax.psum(loss_vec[0, 0], MESH_AXES)

    exchanges = []
    small = [None] * depth
    small_names = ["attn_norm_g", "q_norm_g", "k_norm_g", "sgu_norm_g", "sgu_w", "sgu_b", "out_norm_a_g",
                   "out_norm_b_g", "ffn_norm_g", "conv_b", "conv_w"]
    for l in reversed(range(depth)):
        xs0, h1, p, o, rsum, mix, x1, h2, up, act = saved[l]
        win_g, wout_g, wup_g, wdown_g = gathered[l]
        wout_full = wout_g.reshape(d, d)
        wdown_full = wdown_g.reshape(ff, d)
        g1 = attn_norm_g[l][None]
        g2 = ffn_norm_g[l][None]
        gq, gk = q_norm_g[l][None], k_norm_g[l][None]
        ga = out_norm_a_g[l][:, None, :]
        gs = sgu_norm_g[l][:, None, :]
        gb = out_norm_b_g[l][:, None, :]
        d_wdown = _mm_tn_plain("down_proj_dw", act, dxb)
        exchanges.append((l, "down", ("w_down",), _exchange_start(f"grad_down{l}_start", [d_wdown], dx)))
        dup, d_cw, d_cb = _down_dx_conv_gate_bwd(up, conv_w_all[l], conv_b_all[l], dxb, wdown_full,
                                                 exchanges[-1][3].token)
        d_wup = _mm_tn_blocked("up_proj_dw", h2, dup, N_DEV, halves=True)
        exchanges.append((l, "up", ("w_up",), _exchange_start(f"grad_up{l}_start", [d_wup], d_cb)))
        dh2 = _mm_nt_blocked("up_proj_dx", dup, wup_g, halves=True, after=exchanges[-1][3].token)
        dx, dxb, d_g2 = _rmsnorm_bwd("ffn_norm_bwd", dh2, x1, g2, dx)
        d_wout = _mm_tn_plain("out_proj_dw", mix, dxb)
        exchanges.append((l, "out", ("w_out",), _exchange_start(f"grad_out{l}_start", [d_wout], d_g2)))
        dmix = _mm_nt_plain("out_proj_dx", dxb, wout_full, after=exchanges[-1][3].token)
        dq, dk, dv, d_gq, d_gk, d_ga = _attn_bwd(p, gq, gk, ga, o, rsum, dmix, n_heads)
        du, dvs, d_gs, d_sw, d_sb, d_gb = _sgu_bwd(p, gs, sgu_w[l], sgu_b_col[l], gb, dmix, sgu_col0, n_heads)
        dp = jnp.concatenate([dq, dk, dv, du, dvs], axis=-1)
        d_win = _mm_tn_blocked("in_proj_dw", h1, dp, N_DEV)
        exchanges.append((l, "in", ("w_in",), _exchange_start(f"grad_in{l}_start", [d_win], d_gq)))
        dh1 = _mm_nt_blocked("in_proj_dx", dp, win_g, after=exchanges[-1][3].token)
        dx, dxb, d_g1 = _rmsnorm_bwd("attn_norm_bwd", dh1, xs0, g1, dx)
        small[l] = dict(attn_norm_g=d_g1[0], q_norm_g=d_gq[0], k_norm_g=d_gk[0], sgu_norm_g=d_gs[:, 0], sgu_w=d_sw,
                        sgu_b=d_sb[..., 0], out_norm_a_g=d_ga[:, 0], out_norm_b_g=d_gb[:, 0], ffn_norm_g=d_g2[0],
                        conv_w=jnp.transpose(d_cw, (1, 0, 2)).reshape(CONV_WIDTH, f2), conv_b=d_cb.reshape(f2))
    grad_x = dx[None]

    f32_names = [n for n in small_names if n != "sgu_w"]
    small_g = [jnp.stack([small[l][n] for l in range(depth)]) for n in f32_names]
    sgu_w_g = jnp.stack([small[l]["sgu_w"] for l in range(depth)])
    small_sent = _broadcast_start("grad_small_start", [_pack(small_g), sgu_w_g.reshape(-1, TILE).astype(BF16)], dx)

    res = {}
    big = dict(w_in=(w_in, m_w_in, v_w_in), w_out=(w_out, m_w_out, v_w_out), w_up=(w_up, m_w_up, v_w_up),
               w_down=(w_down, m_w_down, v_w_down))
    after = [small_sent.token]
    batches = [[e for e in exchanges if e[0] == l] for l in reversed(range(depth))]
    batches = batches[:-1] + [batches[-1][:-1], batches[-1][-1:]]
    for i, batch in enumerate(batches):
        landed = _exchange_finish(f"grad_batch{i}_finish", [ex for _, _, _, ex in batch], after)
        after = []
        for (layer, name), parts in zip([(k, n) for k, _, names, _ in batch for n in names], landed):
            w, m, v = big[name]
            res[name] = _adamw(f"adamw_{name}", w, m, v, parts, layer, res.get(name))
            after.append(res[name][0])
    small_all, sgu_w_all = _exchange_finish("grad_small_finish", small_sent, after)
    small_sum = _unpack(_sum_slots("small_grad_sum", small_all), small_g)
    g_small = dict(zip(f32_names, small_sum))
    g_small["sgu_w"] = _sum_slots("sgu_w_grad_sum", sgu_w_all).reshape(sgu_w.shape)
    cwn = conv_w.shape[2]
    g_small["conv_w"] = lax.dynamic_slice_in_dim(g_small["conv_w"], my_slot * cwn, cwn, axis=2)
    small_w = dict(attn_norm_g=(attn_norm_g, m_attn_norm_g, v_attn_norm_g), q_norm_g=(q_norm_g, m_q_norm_g, v_q_norm_g),
                   k_norm_g=(k_norm_g, m_k_norm_g, v_k_norm_g), sgu_norm_g=(sgu_norm_g, m_sgu_norm_g, v_sgu_norm_g),
                   sgu_w=(sgu_w, m_sgu_w, v_sgu_w), sgu_b=(sgu_b, m_sgu_b, v_sgu_b),
                   out_norm_a_g=(out_norm_a_g, m_out_norm_a_g, v_out_norm_a_g),
                   out_norm_b_g=(out_norm_b_g, m_out_norm_b_g, v_out_norm_b_g),
                   ffn_norm_g=(ffn_norm_g, m_ffn_norm_g, v_ffn_norm_g), conv_b=(conv_b, m_conv_b, v_conv_b),
                   conv_w=(conv_w, m_conv_w, v_conv_w))
    like = [small_w[n][0] for n in small_names]
    pw = _pack([small_w[n][0] for n in small_names])
    pm = _pack([small_w[n][1] for n in small_names])
    pv = _pack([small_w[n][2] for n in small_names])
    pg = _pack([g_small[n].reshape(small_w[n][0].shape) for n in small_names])
    pd, pnm, pnv = _adamw_small(pw, pg, pm, pv)
    for n, dlt, nm, nv in zip(small_names, _unpack(pd, like), _unpack(pnm, like), _unpack(pnv, like)):
        res[n] = (g_small[n].reshape(small_w[n][0].shape), dlt, nm, nv)

    order = ["attn_norm_g", "w_in", "q_norm_g", "k_norm_g", "sgu_norm_g", "sgu_w", "sgu_b", "out_norm_a_g",
             "out_norm_b_g", "w_out", "ffn_norm_g", "w_up", "conv_w", "conv_b", "w_down"]
    outs = [loss, grad_x]
    for field in range(4):
        outs += [res[n][field] for n in order]
    return tuple(outs)
```

```python
import functools

import jax
import jax.numpy as jnp
from jax import lax
from jax.experimental import pallas as pl
from jax.experimental.pallas import tpu as pltpu

F32 = jnp.float32
BF16 = jnp.bfloat16
EPS = 1e-6
HEAD_DIM = 128
TILE = 128
ATTN_VMEM_MB = 58
ATTN_HEADS_PER_STEP = 4
N_GROUPS = 8
CONV_WIDTH = 3
N_DEV = 8
MESH_AXES = ("x", "y", "c")
MIB = 1024 * 1024

ADAM_LR = 0.001
ADAM_B1 = 0.9
ADAM_B2 = 0.999
ADAM_EPS = 1e-08
ADAM_WD = 0.01
ADAM_STEP = 10
ADAMW_TILE_ELEMS = 512 * 1024

NT_DIMS = (((1,), (1,)), ((), ()))
NN_DIMS = (((1,), (0,)), ((), ()))
TN_DIMS = (((0,), (0,)), ((), ()))


def _cparams(sem, vmem_mb=48):
    return pltpu.CompilerParams(dimension_semantics=sem, vmem_limit_bytes=vmem_mb * MIB)


def _pick(n, cands):
    for c in cands:
        if n % c == 0:
            return c
    return n


MXU_WIDTH = 256


def _pairs(nb, bn):
    return nb % 2 == 0 and bn % MXU_WIDTH != 0 and (2 * bn) % MXU_WIDTH == 0


def _mm(name, grid, ins, in_specs, out_shape, out_spec, dims, has_res=False, parts=None, vmem_mb=56, after=None,
        split_out=None):
    n_in = 2 + has_res + (after is not None)
    if after is not None:
        ins = tuple(ins) + (after,)
        in_specs = list(in_specs) + [pl.BlockSpec(after.shape, lambda *_: (0, 0))]

    def body(*refs):
        a_ref, b_ref = refs[:2]
        o_ref = refs[n_in]
        if parts is None:
            acc = lax.dot_general(a_ref[...], b_ref[...], dims, preferred_element_type=F32)
        else:
            acc = None
            for part in parts:
                a, b = part(a_ref, b_ref)
                prod = lax.dot_general(a, b, dims, preferred_element_type=F32)
                acc = prod if acc is None else acc + prod
        if has_res:
            acc = acc + refs[2][...]
        if split_out is None:
            o_ref[...] = acc.astype(o_ref.dtype)
        else:
            o_ref[0] = acc[:, :split_out].astype(o_ref.dtype)
            o_ref[1] = acc[:, split_out:].astype(o_ref.dtype)

    return pl.pallas_call(
        body, name=name, grid=grid, in_specs=in_specs, out_specs=out_spec, out_shape=out_shape,
        compiler_params=_cparams(("parallel",) * len(grid), vmem_mb),
    )(*ins)


def _two_blocks(b_ref, first):
    return jnp.concatenate([b_ref[first], b_ref[first + 1]], axis=1)


def _rows_for(m, row_bytes, budget):
    return _pick(m, tuple(t for t in (2048, 1024, 512, 256, 128) if t * row_bytes <= budget))


def _mm_nn_blocked(name, a, wb, out_dtype, after=None):
    m, k = a.shape
    nb, _, bn = wb.shape
    out_shape = jax.ShapeDtypeStruct((m, nb * bn), out_dtype)
    if _pairs(nb, bn):
        tm = _rows_for(m, 2 * bn * jnp.dtype(out_dtype).itemsize, 6 * MIB)
        return _mm(name, (nb // 2, m // tm), (a, wb),
                   [pl.BlockSpec((tm, k), lambda j, i: (i, 0)), pl.BlockSpec((2, k, bn), lambda j, i: (j, 0, 0))],
                   out_shape, pl.BlockSpec((tm, 2 * bn), lambda j, i: (i, j)), NN_DIMS,
                   parts=[lambda a_ref, b_ref: (a_ref[...], _two_blocks(b_ref, 0))], after=after)
    tm = _rows_for(m, bn * jnp.dtype(out_dtype).itemsize, 6 * MIB)
    a_spec = pl.BlockSpec((tm, k), lambda j, i: (i, 0))
    b_spec = pl.BlockSpec((None, k, bn), lambda j, i: (j, 0, 0))
    o_spec = pl.BlockSpec((tm, bn), lambda j, i: (i, j))
    return _mm(name, (nb, m // tm), (a, wb), [a_spec, b_spec], out_shape, o_spec, NN_DIMS, after=after)


def _mm_nn_res_norm(name, a, w, res, g):
    m, k = a.shape
    n = w.shape[1]
    tm = _pick(m, (512, 256, 128))

    def body(a_ref, w_ref, r_ref, g_ref, x_ref, h_ref):
        x = jnp.dot(a_ref[...], w_ref[...], preferred_element_type=F32) + r_ref[...]
        x_ref[...] = x
        h_ref[...] = (x * _rstd(x) * g_ref[...]).astype(BF16)

    row = pl.BlockSpec((tm, n), lambda i: (i, 0))
    return pl.pallas_call(
        body, name=name, grid=(m // tm,),
        in_specs=[pl.BlockSpec((tm, k), lambda i: (i, 0)), pl.BlockSpec((k, n), lambda i: (0, 0)), row,
                  pl.BlockSpec((1, n), lambda i: (0, 0))],
        out_specs=(row, row), out_shape=(jax.ShapeDtypeStruct((m, n), F32), jax.ShapeDtypeStruct((m, n), BF16)),
        compiler_params=_cparams(("parallel",), 56),
    )(a, w, res, g)


def _mm_nn_res(name, a, w, res, after=None):
    m, k = a.shape
    n = w.shape[1]
    tm = _pick(m, (512, 256, 128))
    tn = _rows_for(n, k * 2, 12 * MIB)
    a_spec = pl.BlockSpec((tm, k), lambda j, i: (i, 0))
    b_spec = pl.BlockSpec((k, tn), lambda j, i: (0, j))
    r_spec = pl.BlockSpec((tm, tn), lambda j, i: (i, j))
    o_spec = pl.BlockSpec((tm, tn), lambda j, i: (i, j))
    return _mm(name, (n // tn, m // tm), (a, w, res), [a_spec, b_spec, r_spec], jax.ShapeDtypeStruct((m, n), F32),
               o_spec, NN_DIMS, has_res=True, after=after)


def _mm_nt_blocked(name, dy, wb, halves=False, after=None):
    nb, n, bn = wb.shape
    m = dy.shape[-2]
    tm = _pick(m, (512, 256, 128))
    tn = _rows_for(n, nb * bn * 2, 12 * MIB)
    if halves:
        hb = nb // 2
        a_spec = pl.BlockSpec((2, tm, hb * bn), lambda j, i: (0, i, 0))
        a_part = lambda kk: (lambda a_ref: a_ref[kk // hb, :, (kk % hb) * bn:(kk % hb + 1) * bn])
    else:
        a_spec = pl.BlockSpec((tm, nb * bn), lambda j, i: (i, 0))
        a_part = lambda kk: (lambda a_ref: a_ref[:, kk * bn:(kk + 1) * bn])
    if _pairs(nb, bn) and not halves:
        parts = [(lambda a_ref, b_ref, kk=kk: (a_ref[:, kk * bn:(kk + 2) * bn], _two_blocks(b_ref, kk)))
                 for kk in range(0, nb, 2)]
    elif _pairs(nb // 2, bn) and halves:
        parts = [(lambda a_ref, b_ref, kk=kk: (a_ref[kk // hb, :, (kk % hb) * bn:(kk % hb + 2) * bn],
                                               _two_blocks(b_ref, kk))) for kk in range(0, nb, 2)]
    else:
        parts = [(lambda a_ref, b_ref, kk=kk, sel=a_part(kk): (sel(a_ref), b_ref[kk])) for kk in range(nb)]
    b_spec = pl.BlockSpec((nb, tn, bn), lambda j, i: (0, j, 0))
    o_spec = pl.BlockSpec((tm, tn), lambda j, i: (i, j))
    return _mm(name, (n // tn, m // tm), (dy, wb), [a_spec, b_spec], jax.ShapeDtypeStruct((m, n), F32), o_spec,
               NT_DIMS, parts=parts, after=after)


def _mm_nt_plain(name, dy, w, out_dtype=F32, after=None):
    m, k = dy.shape
    n = w.shape[0]
    tm = _rows_for(m, k * 2, 8 * MIB)
    tn = _pick(n, (512, 256, 128))
    a_spec = pl.BlockSpec((tm, k), lambda j, i: (i, 0))
    b_spec = pl.BlockSpec((tn, k), lambda j, i: (j, 0))
    o_spec = pl.BlockSpec((tm, tn), lambda j, i: (i, j))
    return _mm(name, (n // tn, m // tm), (dy, w), [a_spec, b_spec], jax.ShapeDtypeStruct((m, n), out_dtype), o_spec,
               NT_DIMS, after=after)


def _mm_tn_blocked(name, a, dy, nb, halves=False):
    s, k1 = a.shape
    bn = (dy.shape[-1] * (2 if halves else 1)) // nb
    if _pairs(nb, bn) and not halves:
        tm = _rows_for(k1, 2 * bn * 2, 6 * MIB)
        return _mm(name, (nb // 2, k1 // tm), (a, dy),
                   [pl.BlockSpec((s, tm), lambda j, i: (0, i)), pl.BlockSpec((s, 2 * bn), lambda j, i: (0, j))],
                   jax.ShapeDtypeStruct((nb, k1, bn), BF16), pl.BlockSpec((2, tm, bn), lambda j, i: (j, i, 0)), TN_DIMS,
                   split_out=bn)
    if halves and _pairs(nb // 2, bn):
        tm = _rows_for(k1, 2 * bn * 2, 6 * MIB)
        per_half = nb // 4
        return _mm(name, (nb // 2, k1 // tm), (a, dy),
                   [pl.BlockSpec((s, tm), lambda j, i: (0, i)),
                    pl.BlockSpec((None, s, 2 * bn), lambda j, i: (j // per_half, 0, j % per_half))],
                   jax.ShapeDtypeStruct((nb, k1, bn), BF16), pl.BlockSpec((2, tm, bn), lambda j, i: (j, i, 0)), TN_DIMS,
                   split_out=bn)
    tm = _rows_for(k1, bn * 2, 6 * MIB)
    a_spec = pl.BlockSpec((s, tm), lambda j, i: (0, i))
    if halves:
        hb = nb // 2
        b_spec = pl.BlockSpec((None, s, bn), lambda j, i: (j // hb, 0, j % hb))
    else:
        b_spec = pl.BlockSpec((s, bn), lambda j, i: (0, j))
    o_spec = pl.BlockSpec((None, tm, bn), lambda j, i: (j, i, 0))
    return _mm(name, (nb, k1 // tm), (a, dy), [a_spec, b_spec], jax.ShapeDtypeStruct((nb, k1, bn), BF16), o_spec,
               TN_DIMS)


def _mm_tn_plain(name, a, dy):
    s, k1 = a.shape
    n = dy.shape[1]
    tm = _pick(k1, (512, 256, 128))
    tn = _rows_for(n, s * 2, 8 * MIB)
    a_spec = pl.BlockSpec((s, tm), lambda i, j: (0, i))
    b_spec = pl.BlockSpec((s, tn), lambda i, j: (0, j))
    o_spec = pl.BlockSpec((tm, tn), lambda i, j: (i, j))
    return _mm(name, (k1 // tm, n // tn), (a, dy), [a_spec, b_spec], jax.ShapeDtypeStruct((k1, n), BF16), o_spec,
               TN_DIMS)


def _rstd(x):
    return lax.rsqrt(jnp.mean(x * x, axis=-1, keepdims=True) + EPS)


def _norm_bwd(dy, xhat, r, g):
    dxhat = dy * g
    return r * (dxhat - xhat * jnp.mean(dxhat * xhat, axis=-1, keepdims=True))


def _rmsnorm_fwd(name, x, g):
    s, d = x.shape
    tr = _pick(s, (256, 128))

    def body(x_ref, g_ref, h_ref):
        xv = x_ref[...]
        h_ref[...] = (xv * _rstd(xv) * g_ref[...]).astype(BF16)

    return pl.pallas_call(
        body, name=name, grid=(s // tr,),
        in_specs=[pl.BlockSpec((tr, d), lambda i: (i, 0)), pl.BlockSpec((1, d), lambda i: (0, 0))],
        out_specs=pl.BlockSpec((tr, d), lambda i: (i, 0)),
        out_shape=jax.ShapeDtypeStruct((s, d), BF16), compiler_params=_cparams(("parallel",)),
    )(x, g)


def _rmsnorm_bwd(name, dh, x, g, dres):
    s, d = x.shape
    tr = _pick(s, (256, 128))

    def body(dh_ref, x_ref, g_ref, dres_ref, dx_ref, dxb_ref, dg_ref):
        xv = x_ref[...]
        r = _rstd(xv)
        xhat = xv * r
        dhv = dh_ref[...]
        dx = dres_ref[...] + _norm_bwd(dhv, xhat, r, g_ref[...])
        dx_ref[...] = dx
        dxb_ref[...] = dx.astype(BF16)
        part = jnp.sum(dhv * xhat, axis=0, keepdims=True)

        @pl.when(pl.program_id(0) == 0)
        def _():
            dg_ref[...] = part

        @pl.when(pl.program_id(0) > 0)
        def _():
            dg_ref[...] += part

    row = pl.BlockSpec((tr, d), lambda i: (i, 0))
    vec = pl.BlockSpec((1, d), lambda i: (0, 0))
    return pl.pallas_call(
        body, name=name, grid=(s // tr,), in_specs=[row, row, vec, row], out_specs=(row, row, vec),
        out_shape=(jax.ShapeDtypeStruct((s, d), F32), jax.ShapeDtypeStruct((s, d), BF16),
                   jax.ShapeDtypeStruct((1, d), F32)),
        compiler_params=_cparams(("arbitrary",)),
    )(dh, x, g, dres)


def _loss_head(y, target):
    s, d = y.shape
    tr = _pick(s, (256, 128))

    def body(y_ref, t_ref, dy_ref, dyb_ref, loss_ref):
        err = y_ref[...] - t_ref[...]
        dy = err * (1.0 / d)
        dy_ref[...] = dy
        dyb_ref[...] = dy.astype(BF16)
        part = 0.5 * jnp.sum(jnp.mean(err * err, axis=-1, keepdims=True), axis=0, keepdims=True)
        part = jnp.broadcast_to(part, (1, 128))

        @pl.when(pl.program_id(0) == 0)
        def _():
            loss_ref[...] = part

        @pl.when(pl.program_id(0) > 0)
        def _():
            loss_ref[...] += part

    row = pl.BlockSpec((tr, d), lambda i: (i, 0))
    return pl.pallas_call(
        body, name="loss_head", grid=(s // tr,), in_specs=[row, row],
        out_specs=(row, row, pl.BlockSpec((1, 128), lambda i: (0, 0))),
        out_shape=(jax.ShapeDtypeStruct((s, d), F32), jax.ShapeDtypeStruct((s, d), BF16),
                   jax.ShapeDtypeStruct((1, 128), F32)),
        compiler_params=_cparams(("arbitrary",)),
    )(y, target)


def _split_dot(x, tri):
    hi = x.astype(BF16)
    lo = (x - hi.astype(F32)).astype(BF16)
    return (jnp.dot(hi, tri, preferred_element_type=F32) + jnp.dot(lo, tri, preferred_element_type=F32))


def _tile_iotas():
    row = lax.broadcasted_iota(jnp.int32, (TILE, TILE), 0)
    col = lax.broadcasted_iota(jnp.int32, (TILE, TILE), 1)
    return row, col


def _sb_logits(qi, kb, mask):
    z = lax.dot_general(qi, kb, NT_DIMS, preferred_element_type=F32) * (HEAD_DIM ** -0.5)
    sp = jnp.log(1.0 + jnp.exp(-jnp.abs(z)))
    lb = jnp.minimum(z, 0.0) - sp
    l1m = -jnp.maximum(z, 0.0) - sp
    if mask is not None:
        l1m = jnp.where(mask, l1m, 0.0)
    return lb, l1m


def _attn_fwd(p, gq, gk, ga, n_heads):
    s = p.shape[0]
    nq = s // TILE

    hp = ATTN_HEADS_PER_STEP
    wd = hp * HEAD_DIM

    def body(q_ref, k_ref, v_ref, gq_ref, gk_ref, ga_ref, att_ref, o_ref, r_ref, qn_s, kn_s, vb_s):
        heads = [slice(hh * HEAD_DIM, (hh + 1) * HEAD_DIM) for hh in range(hp)]
        for hd in heads:
            qv = q_ref[:, hd]
            qn_s[:, hd] = (qv * _rstd(qv) * gq_ref[...]).astype(BF16)
            kv = k_ref[:, hd]
            kn_s[:, hd] = (kv * _rstd(kv) * gk_ref[...]).astype(BF16)
        vb_s[...] = v_ref[...].astype(BF16)
        row, col = _tile_iotas()
        causal = col < row
        upper_ones = jnp.concatenate([(row > col).astype(BF16), jnp.ones((TILE, TILE), BF16)], axis=1)

        def tiles(rows, key_blocks, states):
            chains = [(hi, hd, keys, mask) for hi, hd in enumerate(heads) for keys, mask in key_blocks]
            logits = [_sb_logits(qn_s[rows, hd], kn_s[keys, hd], mask) for _, hd, keys, mask in chains]
            sums = [_split_dot(l1m, upper_ones) for _, l1m in logits]
            carry = [c for _, c in states]
            probs = []
            for (hi, _, _, mask), (lb, _), sm in zip(chains, logits, sums):
                a = jnp.exp(lb + sm[:, :TILE] + carry[hi])
                carry[hi] = carry[hi] + sm[:, TILE:]
                probs.append((a if mask is None else jnp.where(mask, a, 0.0)).astype(BF16))
            outs = [jnp.dot(a, vb_s[keys, hd], preferred_element_type=F32) for a, (_, hd, keys, _) in zip(probs, chains)]
            acc = [o_acc for o_acc, _ in states]
            for (hi, _, _, _), o in zip(chains, outs):
                acc[hi] = acc[hi] + o
            return tuple(zip(acc, carry))

        def key_block(b):
            return pl.ds(pl.multiple_of(b * TILE, TILE), TILE), None

        def qblock(i, _):
            rows = pl.ds(pl.multiple_of(i * TILE, TILE), TILE)
            zero = jnp.zeros((TILE, HEAD_DIM), F32)
            states = tuple((zero, zero) for _ in heads)
            states = lax.cond(i % 2 == 1, lambda st: tiles(rows, [(rows, causal), key_block(i - 1)], st),
                              lambda st: tiles(rows, [(rows, causal)], st), states)
            top = i - i % 2

            def kblocks(jj, states):
                return tiles(rows, [key_block(top - 1 - 2 * jj), key_block(top - 2 - 2 * jj)], states)

            states = lax.fori_loop(0, i // 2, kblocks, states)
            for hh, (hd, (o_acc, c)) in enumerate(zip(heads, states)):
                o_ref[rows, hd] = o_acc
                r_ref[rows, hd] = c
                att_ref[rows, hd] = (o_acc * _rstd(o_acc) * ga_ref[hh]).astype(BF16)
            return 0

        lax.fori_loop(0, nq, qblock, 0)

    col_blk = lambda off: pl.BlockSpec((s, wd), lambda h: (0, off + h))
    vec = pl.BlockSpec((1, HEAD_DIM), lambda h: (0, 0))
    hvec = pl.BlockSpec((hp, 1, HEAD_DIM), lambda h: (h, 0, 0))
    out = pl.BlockSpec((s, wd), lambda h: (0, h))
    w = n_heads * HEAD_DIM
    steps = n_heads // hp
    return pl.pallas_call(
        body, name="attn_fwd", grid=(steps,),
        in_specs=[col_blk(0), col_blk(steps), col_blk(2 * steps), vec, vec, hvec],
        out_specs=(out, out, out),
        out_shape=(jax.ShapeDtypeStruct((s, w), BF16), jax.ShapeDtypeStruct((s, w), F32),
                   jax.ShapeDtypeStruct((s, w), F32)),
        scratch_shapes=[pltpu.VMEM((s, wd), BF16)] * 3,
        compiler_params=_cparams(("parallel",), ATTN_VMEM_MB),
    )(p, p, p, gq, gk, ga)


def _attn_bwd(p, gq, gk, ga, o, rsum, dmix, n_heads):
    s = p.shape[0]
    nq = s // TILE

    hp = ATTN_HEADS_PER_STEP
    wd = hp * HEAD_DIM
    scale = HEAD_DIM ** -0.5

    def body(q_ref, k_ref, v_ref, gq_ref, gk_ref, ga_ref, o_ref, r_ref, dm_ref,
             dq_ref, dk_ref, dv_ref, dgq_ref, dgk_ref, dga_ref,
             qn_s, kn_s, vb_s, do_s, dqn_s, dkn_s, dv_s):
        step = pl.program_id(0)
        gqv, gkv = gq_ref[...], gk_ref[...]
        heads = [slice(hh * HEAD_DIM, (hh + 1) * HEAD_DIM) for hh in range(hp)]
        for hh, hd in enumerate(heads):
            qv = q_ref[:, hd]
            qn_s[:, hd] = (qv * _rstd(qv) * gqv).astype(BF16)
            kv = k_ref[:, hd]
            kn_s[:, hd] = (kv * _rstd(kv) * gkv).astype(BF16)
            ov = o_ref[:, hd]
            ro = _rstd(ov)
            ohat = ov * ro
            dm = dm_ref[:, hd]
            dga_ref[hh] = jnp.sum(dm * ohat, axis=0, keepdims=True)
            do_s[:, hd] = _norm_bwd(dm, ohat, ro, ga_ref[hh]).astype(BF16)
        vb_s[...] = v_ref[...].astype(BF16)
        dkn_s[...] = jnp.zeros_like(dkn_s)
        dv_s[...] = jnp.zeros_like(dv_s)
        row, col = _tile_iotas()
        causal = col < row
        ones = jnp.ones((TILE, TILE), BF16)
        incl_ones = jnp.concatenate([(row <= col).astype(BF16), ones], axis=1)
        excl_ones = jnp.concatenate([(row < col).astype(BF16), ones], axis=1)

        def tiles(rows, key_blocks, states):
            chains = [(hi, hd, keys, mask) for hi, hd in enumerate(heads) for keys, mask in key_blocks]
            qis = [qn_s[rows, hd] for hd in heads]
            dois = [do_s[rows, hd] for hd in heads]
            logits = [_sb_logits(qis[hi], kn_s[keys, hd], mask) for hi, hd, keys, mask in chains]
            sums = [_split_dot(l1m, incl_ones) for _, l1m in logits]
            das = [lax.dot_general(dois[hi], vb_s[keys, hd], NT_DIMS, preferred_element_type=F32)
                   for hi, hd, keys, _ in chains]
            pfx = [st[1] for st in states]
            probs, dss = [], []
            for (hi, hd, _, mask), (lb, _), sm, da in zip(chains, logits, sums, das):
                a = jnp.exp(lb + (r_ref[rows, hd] - pfx[hi] - sm[:, :TILE]))
                pfx[hi] = pfx[hi] + sm[:, TILE:]
                a = a if mask is None else jnp.where(mask, a, 0.0)
                probs.append(a.astype(BF16))
                dss.append(da * a)
            dsums = [_split_dot(ds, excl_ones) for ds in dss]
            pc = [st[2] for st in states]
            dzs = []
            for (hi, _, _, mask), (lb, _), ds, dsm in zip(chains, logits, dss, dsums):
                dl1m = pc[hi] + dsm[:, :TILE]
                pc[hi] = pc[hi] + dsm[:, TILE:]
                dl1m = dl1m if mask is None else jnp.where(mask, dl1m, 0.0)
                beta = jnp.exp(lb)
                dzs.append(((ds * (1.0 - beta) - dl1m * beta) * scale).astype(BF16))
            dqs = [jnp.dot(dz, kn_s[keys, hd], preferred_element_type=F32) for dz, (_, hd, keys, _) in zip(dzs, chains)]
            for dz, a, (hi, hd, keys, _) in zip(dzs, probs, chains):
                dkn_s[keys, hd] += lax.dot_general(dz, qis[hi], TN_DIMS, preferred_element_type=F32)
                dv_s[keys, hd] += lax.dot_general(a, dois[hi], TN_DIMS, preferred_element_type=F32)
            dq_acc = [st[0] for st in states]
            for (hi, _, _, _), dq in zip(chains, dqs):
                dq_acc[hi] = dq_acc[hi] + dq
            return tuple(zip(dq_acc, pfx, pc))

        def key_block(b):
            return pl.ds(pl.multiple_of(b * TILE, TILE), TILE), None

        def qblock(i, _):
            rows = pl.ds(pl.multiple_of(i * TILE, TILE), TILE)
            zero = jnp.zeros((TILE, HEAD_DIM), F32)

            def kblocks(jj, states):
                return tiles(rows, [key_block(2 * jj), key_block(2 * jj + 1)], states)

            states = lax.fori_loop(0, i // 2, kblocks, tuple((zero, zero, zero) for _ in heads))
            states = lax.cond(i % 2 == 1, lambda st: tiles(rows, [key_block(i - 1), (rows, causal)], st),
                              lambda st: tiles(rows, [(rows, causal)], st), states)
            for hd, (dq_acc, _, _) in zip(heads, states):
                dqn_s[rows, hd] = dq_acc
            return 0

        lax.fori_loop(0, nq, qblock, 0)

        def norm_in_bwd(x_ref, g, dn_s, dx_ref, dg_ref):
            part = jnp.zeros((1, HEAD_DIM), F32)
            for hd in heads:
                xv = x_ref[:, hd]
                r = _rstd(xv)
                xhat = xv * r
                dn = dn_s[:, hd]
                dx_ref[:, hd] = _norm_bwd(dn, xhat, r, g).astype(BF16)
                part = part + jnp.sum(dn * xhat, axis=0, keepdims=True)

            @pl.when(step == 0)
            def _():
                dg_ref[...] = part

            @pl.when(step > 0)
            def _():
                dg_ref[...] += part

        norm_in_bwd(q_ref, gqv, dqn_s, dq_ref, dgq_ref)
        norm_in_bwd(k_ref, gkv, dkn_s, dk_ref, dgk_ref)
        dv_ref[...] = dv_s[...].astype(BF16)

    once = pl.Buffered(1)
    steps = n_heads // hp
    col_blk = lambda off: pl.BlockSpec((s, wd), lambda h: (0, off + h), pipeline_mode=once)
    vec = pl.BlockSpec((1, HEAD_DIM), lambda h: (0, 0))
    hvec = pl.BlockSpec((hp, 1, HEAD_DIM), lambda h: (h, 0, 0))
    blk = pl.BlockSpec((s, wd), lambda h: (0, h), pipeline_mode=once)
    w = n_heads * HEAD_DIM
    big = jax.ShapeDtypeStruct((s, w), BF16)
    return pl.pallas_call(
        body, name="attn_bwd", grid=(steps,),
        in_specs=[col_blk(0), col_blk(steps), col_blk(2 * steps), vec, vec, hvec, blk, blk, blk],
        out_specs=(blk, blk, blk, vec, vec, hvec),
        out_shape=(big, big, big, jax.ShapeDtypeStruct((1, HEAD_DIM), F32), jax.ShapeDtypeStruct((1, HEAD_DIM), F32),
                   jax.ShapeDtypeStruct((n_heads, 1, HEAD_DIM), F32)),
        scratch_shapes=[pltpu.VMEM((s, wd), BF16)] * 4 + [pltpu.VMEM((s, wd), F32)] * 3,
        compiler_params=_cparams(("arbitrary",), ATTN_VMEM_MB),
    )(p, p, p, gq, gk, ga, o, rsum, dmix)


_INV_SQRT2 = 0.7071067811865476
_INV_SQRT_2PI = 0.3989422804014327


def _gelu(x):
    return 0.5 * x * (1.0 + lax.erf(x * _INV_SQRT2))


def _gelu_grad(x):
    return 0.5 * (1.0 + lax.erf(x * _INV_SQRT2)) + x * (_INV_SQRT_2PI * jnp.exp(-0.5 * x * x))


def _sgu_fwd(p, gs, w_s, b_s, gb, col0, after):
    s = p.shape[0]
    n_chunks = s // TILE
    per_trip = _pick(n_chunks, (4, 2, 1))

    def body(u_ref, v_ref, gs_ref, w_ref, b_ref, gb_ref, _, out_ref, vs_s):
        vg = _gelu(v_ref[...])
        vs_s[...] = (vg * _rstd(vg) * gs_ref[...]).astype(BF16)
        row, col = _tile_iotas()
        wt = jnp.where(col <= row, w_ref[...], 0.0).astype(BF16)
        bcol = b_ref[...]
        gbv = gb_ref[...]

        def chunks(c, _):
            rows = [pl.ds(pl.multiple_of((c * per_trip + k) * TILE, TILE), TILE) for k in range(per_trip)]
            mixed = [jnp.dot(wt, vs_s[r, :], preferred_element_type=F32) + bcol for r in rows]
            sgs = [_gelu(u_ref[r, :]) * mx for r, mx in zip(rows, mixed)]
            for r, sg in zip(rows, sgs):
                out_ref[r, :] = (sg * _rstd(sg) * gbv).astype(BF16)
            return 0

        lax.fori_loop(0, n_chunks // per_trip, chunks, 0)

    col_blk = lambda off: pl.BlockSpec((s, HEAD_DIM), lambda g: (0, off + g))
    gvec = pl.BlockSpec((None, 1, HEAD_DIM), lambda g: (g, 0, 0))
    return pl.pallas_call(
        body, name="sgu_fwd", grid=(N_GROUPS,),
        in_specs=[col_blk(col0), col_blk(col0 + N_GROUPS), gvec,
                  pl.BlockSpec((None, TILE, TILE), lambda g: (g, 0, 0)),
                  pl.BlockSpec((None, TILE, 1), lambda g: (g, 0, 0)), gvec,
                  pl.BlockSpec(after.shape, lambda g: (0, 0))],
        out_specs=pl.BlockSpec((s, HEAD_DIM), lambda g: (0, g)),
        out_shape=jax.ShapeDtypeStruct((s, N_GROUPS * HEAD_DIM), BF16),
        scratch_shapes=[pltpu.VMEM((s, HEAD_DIM), BF16)],
        compiler_params=_cparams(("parallel",)),
    )(p, p, gs, w_s, b_s, gb, after)


def _sgu_bwd(p, gs, w_s, b_s, gb, dmix, col0, dm_col0):
    s = p.shape[0]
    n_chunks = s // TILE
    per_trip = _pick(n_chunks, (4, 2, 1))

    def body(u_ref, v_ref, gs_ref, w_ref, b_ref, gb_ref, dm_ref,
             du_ref, dv_ref, dgs_ref, dw_ref, db_ref, dgb_ref, vs_s, dvs_s):
        gsv = gs_ref[...]
        gbv = gb_ref[...]
        vg = _gelu(v_ref[...])
        vs_s[...] = (vg * _rstd(vg) * gsv).astype(BF16)
        row, col = _tile_iotas()
        causal = col <= row
        wt = jnp.where(causal, w_ref[...], 0.0).astype(BF16)
        bcol = b_ref[...]

        def chunks(c, carry):
            dw_acc, db_acc, dgb_acc = carry
            rows = [pl.ds(pl.multiple_of((c * per_trip + k) * TILE, TILE), TILE) for k in range(per_trip)]
            vss = [vs_s[r, :] for r in rows]
            mixed = [jnp.dot(wt, vs, preferred_element_type=F32) + bcol for vs in vss]
            dmbs = []
            for r, mx in zip(rows, mixed):
                u_pre = u_ref[r, :]
                u = _gelu(u_pre)
                sg = u * mx
                rs = _rstd(sg)
                sghat = sg * rs
                dm = dm_ref[r, :]
                dsg = _norm_bwd(dm, sghat, rs, gbv)
                dgb_acc = dgb_acc + jnp.sum(dm * sghat, axis=0, keepdims=True)
                du_ref[r, :] = (dsg * mx * _gelu_grad(u_pre)).astype(BF16)
                dmixed = dsg * u
                db_acc = db_acc + jnp.sum(dmixed, axis=1, keepdims=True)
                dmbs.append(dmixed.astype(BF16))
            for dmb, vs in zip(dmbs, vss):
                dw_acc = dw_acc + lax.dot_general(dmb, vs, NT_DIMS, preferred_element_type=F32)
            for r, dmb in zip(rows, dmbs):
                dvs_s[r, :] = lax.dot_general(wt, dmb, TN_DIMS, preferred_element_type=F32)
            return dw_acc, db_acc, dgb_acc

        dw_acc, db_acc, dgb_acc = lax.fori_loop(
            0, n_chunks // per_trip, chunks,
            (jnp.zeros((TILE, TILE), F32), jnp.zeros((TILE, 1), F32), jnp.zeros((1, HEAD_DIM), F32)))
        dw_ref[...] = jnp.where(causal, dw_acc, 0.0)
        db_ref[...] = db_acc
        dgb_ref[...] = dgb_acc
        v_pre = v_ref[...]
        vg = _gelu(v_pre)
        rv = _rstd(vg)
        vhat = vg * rv
        dvs = dvs_s[...]
        dgs_ref[...] = jnp.sum(dvs * vhat, axis=0, keepdims=True)
        dv_ref[...] = (_norm_bwd(dvs, vhat, rv, gsv) * _gelu_grad(v_pre)).astype(BF16)

    col_blk = lambda off: pl.BlockSpec((s, HEAD_DIM), lambda g: (0, off + g))
    gvec = pl.BlockSpec((None, 1, HEAD_DIM), lambda g: (g, 0, 0))
    wspec = pl.BlockSpec((None, TILE, TILE), lambda g: (g, 0, 0))
    bspec = pl.BlockSpec((None, TILE, 1), lambda g: (g, 0, 0))
    blk = pl.BlockSpec((s, HEAD_DIM), lambda g: (0, g))
    big = jax.ShapeDtypeStruct((s, N_GROUPS * HEAD_DIM), BF16)
    gshape = jax.ShapeDtypeStruct((N_GROUPS, 1, HEAD_DIM), F32)
    return pl.pallas_call(
        body, name="sgu_bwd", grid=(N_GROUPS,),
        in_specs=[col_blk(col0), col_blk(col0 + N_GROUPS), gvec, wspec, bspec, gvec, col_blk(dm_col0)],
        out_specs=(blk, blk, gvec, wspec, bspec, gvec),
        out_shape=(big, big, gshape, jax.ShapeDtypeStruct((N_GROUPS, TILE, TILE), F32),
                   jax.ShapeDtypeStruct((N_GROUPS, TILE, 1), F32), gshape),
        scratch_shapes=[pltpu.VMEM((s, HEAD_DIM), BF16), pltpu.VMEM((s, HEAD_DIM), F32)],
        compiler_params=_cparams(("parallel",)),
    )(p, p, gs, w_s, b_s, gb, dmix)


SUBLANES = 8


def _shift_down(x, n):
    rolled = pltpu.roll(x, n, 0)
    edge = lax.broadcasted_iota(jnp.int32, (SUBLANES, x.shape[1]), 0)
    return jnp.concatenate([jnp.where(edge >= n, rolled[:SUBLANES], 0.0), rolled[SUBLANES:]], axis=0)


def _shift_up(x, n):
    s = x.shape[0]
    rolled = pltpu.roll(x, s - n, 0)
    edge = lax.broadcasted_iota(jnp.int32, (SUBLANES, x.shape[1]), 0)
    return jnp.concatenate([rolled[:s - SUBLANES], jnp.where(edge < SUBLANES - n, rolled[s - SUBLANES:], 0.0)], axis=0)


def _conv(x, w, b):
    x1, x2 = _shift_down(x, 1), _shift_down(x, 2)
    return b + w[0:1, :] * x2 + w[1:2, :] * x1 + w[2:3, :] * x, x1, x2


def _conv_specs(s, tn):
    xspec = pl.BlockSpec((2, s, tn), lambda j: (0, 0, j))
    wspec = pl.BlockSpec((2, CONV_WIDTH, tn), lambda j: (0, 0, j))
    bspec = pl.BlockSpec((2, 1, tn), lambda j: (0, 0, j))
    return xspec, wspec, bspec


def _up_conv_gate_fwd(h, wb, cw, cb, after=None):
    s, k = h.shape
    nb, _, bn = wb.shape
    hb = nb // 2
    f = hb * bn
    tm = _pick(s, (512, 256, 128))
    n_in = 5 + (after is not None)

    def body(*refs):
        h_ref, wg_ref, wv_ref, w_ref, b_ref = refs[:5]
        up_ref, act_ref, halo_s = refs[n_in:]
        first = pl.program_id(1) == 0
        outs = []
        for half, wt_ref in enumerate((wg_ref, wv_ref)):
            x = jnp.dot(h_ref[...], wt_ref[...], preferred_element_type=F32)
            up_ref[half] = x
            halo = jnp.where(first, 0.0, halo_s[half])
            halo_s[half] = x[tm - SUBLANES:]
            full = jnp.concatenate([halo, x], axis=0)
            x1 = pltpu.roll(full, 1, 0)[SUBLANES:]
            x2 = pltpu.roll(full, 2, 0)[SUBLANES:]
            w = w_ref[half]
            outs.append(b_ref[half] + w[0:1, :] * x2 + w[1:2, :] * x1 + w[2:3, :] * x)
        gate, val = outs
        act_ref[...] = (gate * jax.nn.sigmoid(gate) * val).astype(BF16)

    ins = [h, wb, wb, cw, cb]
    in_specs = [pl.BlockSpec((tm, k), lambda j, i: (i, 0)),
                pl.BlockSpec((None, k, bn), lambda j, i: (j, 0, 0)),
                pl.BlockSpec((None, k, bn), lambda j, i: (j + hb, 0, 0)),
                pl.BlockSpec((2, CONV_WIDTH, bn), lambda j, i: (0, 0, j)),
                pl.BlockSpec((2, 1, bn), lambda j, i: (0, 0, j))]
    if after is not None:
        ins.append(after)
        in_specs.append(pl.BlockSpec(after.shape, lambda j, i: (0, 0)))
    return pl.pallas_call(
        body, name="up_conv_gate_fwd", grid=(hb, s // tm), in_specs=in_specs,
        out_specs=(pl.BlockSpec((2, tm, bn), lambda j, i: (0, i, j)), pl.BlockSpec((tm, bn), lambda j, i: (i, j))),
        out_shape=(jax.ShapeDtypeStruct((2, s, f), F32), jax.ShapeDtypeStruct((s, f), BF16)),
        scratch_shapes=[pltpu.VMEM((2, SUBLANES, bn), F32)],
        compiler_params=_cparams(("parallel", "arbitrary"), 56),
    )(*ins)


def _down_dx_conv_gate_bwd(up, cw, cb, dy, wdown, after):
    _, s, f = up.shape
    d = dy.shape[1]
    tn = _pick(f, (256, 128))

    def body(x_ref, w_ref, b_ref, dy_ref, wd_ref, _, dx_ref, dw_ref, db_ref):
        da = lax.dot_general(dy_ref[...], wd_ref[...], NT_DIMS, preferred_element_type=F32)
        xg, xv = x_ref[0], x_ref[1]
        wg, wv = w_ref[0], w_ref[1]
        gate, xg1, xg2 = _conv(xg, wg, b_ref[0])
        val, xv1, xv2 = _conv(xv, wv, b_ref[1])
        sig = jax.nn.sigmoid(gate)
        dval = da * (gate * sig)
        dgate = da * val * (sig * (1.0 + gate * (1.0 - sig)))
        for half, (x, x1, x2, w, dz) in enumerate(((xg, xg1, xg2, wg, dgate), (xv, xv1, xv2, wv, dval))):
            dx_ref[half] = (w[2:3, :] * dz + w[1:2, :] * _shift_up(dz, 1) + w[0:1, :] * _shift_up(dz, 2)).astype(BF16)
            dw_ref[half, 0:1, :] = jnp.sum(dz * x2, axis=0, keepdims=True)
            dw_ref[half, 1:2, :] = jnp.sum(dz * x1, axis=0, keepdims=True)
            dw_ref[half, 2:3, :] = jnp.sum(dz * x, axis=0, keepdims=True)
            db_ref[half] = jnp.sum(dz, axis=0, keepdims=True)

    xspec, wspec, bspec = _conv_specs(s, tn)
    return pl.pallas_call(
        body, name="down_dx_conv_gate_bwd", grid=(f // tn,),
        in_specs=[xspec, wspec, bspec, pl.BlockSpec((s, d), lambda j: (0, 0)), pl.BlockSpec((tn, d), lambda j: (j, 0)),
                  pl.BlockSpec(after.shape, lambda j: (0, 0))],
        out_specs=(xspec, wspec, bspec),
        out_shape=(jax.ShapeDtypeStruct((2, s, f), BF16), jax.ShapeDtypeStruct((2, CONV_WIDTH, f), F32),
                   jax.ShapeDtypeStruct((2, 1, f), F32)),
        compiler_params=_cparams(("parallel",), 56),
    )(up, cw, cb, dy, wdown, after)


def _mesh_pos():
    return lax.axis_index("x"), lax.axis_index("y"), lax.axis_index("c")


def _remote(src, dst, send_sem, recv_sem, to):
    return pltpu.make_async_remote_copy(src_ref=src, dst_ref=dst, send_sem=send_sem, recv_sem=recv_sem,
                                        device_id=to, device_id_type=pl.DeviceIdType.MESH)


HBM_SPEC = pl.BlockSpec(memory_space=pltpu.HBM)
SEM_SPEC = pl.BlockSpec(memory_space=pltpu.SEMAPHORE)
ANY_SPEC = pl.BlockSpec(memory_space=pl.ANY)
TOKEN_SPEC = pl.BlockSpec(memory_space=pltpu.VMEM)
TOKEN_SHAPE = jax.ShapeDtypeStruct((8, 128), F32)
DATAFLOW = pltpu.SideEffectType.DATAFLOW_SIDE_EFFECTING
GATHER_PLANE = (2, 4, 6)


def _slot(pos):
    return 4 * pos[0] + 2 * pos[1] + pos[2]


def _flip(pos, k):
    return (pos[0] ^ ((k >> 2) & 1), pos[1] ^ ((k >> 1) & 1), pos[2] ^ (k & 1))


def _hbm(a):
    return pltpu.with_memory_space_constraint(a, pltpu.HBM)


def _hbm_shapes(arrays):
    return tuple(pltpu.HBM(a.shape, a.dtype) for a in arrays)


class _Split:
    def __init__(self, n, outs, n_sets):
        k = 2 * n * int(n_sets)
        self.n = n
        self.sems = list(outs[:k])
        self.bufs = list(outs[k:k + 2 * n])
        self.token = outs[-1]

    def sem_set(self, i):
        return self.sems[2 * self.n * i:2 * self.n * (i + 1)]


def _split_call(name, body, bufs, sems_in, n_sets, after):
    n = len(bufs) // 2
    k = 2 * n * int(n_sets)
    m = len(sems_in)
    afters = list(after) if isinstance(after, (list, tuple)) else [after]
    na = len(afters)

    def wrapped(*refs):
        srcs, dsts = refs[:n], refs[n:2 * n]
        s_in = refs[2 * n:2 * n + m]
        s_out = refs[2 * n + m + na:2 * n + m + na + k]
        token, local_sems = refs[-2], refs[-1]
        body(srcs, dsts, s_in, s_out, local_sems)
        token[...] = jnp.zeros_like(token)

    outs = pl.pallas_call(
        wrapped, name=name,
        out_shape=(pltpu.SemaphoreType.DMA(()),) * k + _hbm_shapes(bufs) + (TOKEN_SHAPE,),
        in_specs=[HBM_SPEC] * (2 * n) + [SEM_SPEC] * m + [ANY_SPEC] * na,
        out_specs=(SEM_SPEC,) * k + (HBM_SPEC,) * (2 * n) + (TOKEN_SPEC,),
        input_output_aliases={i: k + i for i in range(2 * n)},
        scratch_shapes=[pltpu.SemaphoreType.DMA((n,))],
        compiler_params=pltpu.CompilerParams(has_side_effects=DATAFLOW),
    )(*[_hbm(b) for b in bufs], *sems_in, *afters)
    return _Split(n, outs, n_sets)


def _wait_slots(land, count, send_sem, recv_sem, me, send=False, recv=False):
    span = land.at[pl.ds(0, count)]
    cp = _remote(span, span, send_sem, recv_sem, me)
    if send:
        cp.wait_send()
    if recv:
        cp.wait_recv()


X_FLIP, Y_FLIP, DIAG_FLIP = 4, 2, 6
BF16_SUBLANES = 16


def _gather_start(name, shards, after):
    n = len(shards)
    my_slot = _slot(_mesh_pos())
    lands = [lax.dynamic_update_slice(lax.empty((N_DEV,) + w.shape, w.dtype), w[None], (my_slot, 0, 0)) for w in shards]

    def body(srcs, dsts, _, sems, local_sems):
        me = _mesh_pos()
        for a in range(n):
            for k in (1, X_FLIP, Y_FLIP):
                _remote(srcs[a], dsts[a].at[_slot(me)], sems[a], sems[n + a], _flip(me, k)).start()

    return _split_call(name, body, list(shards) + lands, [], 1, after)


def _gather_relay(name, started, after):
    n = started.n

    def body(srcs, dsts, sems_a, sems_out, local_sems):
        me = _mesh_pos()
        sibling = _flip(me, 1)
        pass_on, relay = sems_out[:2 * n], sems_out[2 * n:]
        for a in range(n):
            _wait_slots(dsts[a], 3, sems_a[a], sems_a[n + a], me, recv=True)
            from_x = dsts[a].at[_slot(_flip(me, X_FLIP))]
            from_y = dsts[a].at[_slot(_flip(me, Y_FLIP))]
            rows = srcs[a].shape[0]
            if rows % (2 * BF16_SUBLANES) == 0:
                top, bottom = pl.ds(0, rows // 2), pl.ds(rows // 2, rows // 2)
                _remote(from_y.at[top], from_y.at[top], relay[a], relay[n + a], _flip(me, X_FLIP)).start()
                _remote(from_x.at[bottom], from_x.at[bottom], relay[a], relay[n + a], _flip(me, Y_FLIP)).start()
            else:
                _remote(from_y, from_y, relay[a], relay[n + a], _flip(me, X_FLIP)).start()
            for block in (from_x, from_y):
                _remote(block, block, pass_on[a], pass_on[n + a], sibling).start()
        for a in range(n):
            _wait_slots(dsts[a], 3, sems_a[a], sems_a[n + a], me, send=True)

    return _split_call(name, body, started.bufs, started.sems, 2, after)


def _gather_relay_diagonal(name, relayed, after):
    n = relayed.n

    def body(srcs, dsts, relay, pass_on, local_sems):
        me = _mesh_pos()
        for a in range(n):
            _wait_slots(dsts[a], 1, relay[a], relay[n + a], me, recv=True)
            block = dsts[a].at[_slot(_flip(me, DIAG_FLIP))]
            _remote(block, block, pass_on[a], pass_on[n + a], _flip(me, 1)).start()
        for a in range(n):
            _wait_slots(dsts[a], 1, relay[a], relay[n + a], me, send=True)

    return _split_call(name, body, relayed.bufs, relayed.sem_set(1), 1, after)


def _gather_finish(name, relayed, diagonal, after):
    n = relayed.n

    def body(srcs, dsts, sems, _, local_sems):
        me = _mesh_pos()
        first, second = sems[:2 * n], sems[2 * n:]
        for a in range(n):
            _wait_slots(dsts[a], 2, first[a], first[n + a], me, send=True, recv=True)
            _wait_slots(dsts[a], 1, second[a], second[n + a], me, send=True, recv=True)

    return _split_call(name, body, diagonal.bufs, relayed.sem_set(0) + diagonal.sems, 0, after).bufs[n:]


def _exchange_start(name, blocked, after):
    n = len(blocked)
    my_slot = _slot(_mesh_pos())
    rows = [w.shape[-2] // (N_DEV if w.ndim == 2 else 1) for w in blocked]

    def block(ref, a, slot):
        if len(ref.shape) == 3:
            return ref.at[slot]
        return ref.at[pl.ds(pl.multiple_of(slot * rows[a], 16), rows[a])]

    lands = []
    for w, r in zip(blocked, rows):
        mine = lax.dynamic_slice_in_dim(w, my_slot, 1, 0) if w.ndim == 3 else lax.dynamic_slice_in_dim(w, my_slot * r, r, 0)[None]
        lands.append(lax.dynamic_update_slice(lax.empty((N_DEV, r, w.shape[-1]), w.dtype), mine, (my_slot, 0, 0)))

    def body(srcs, dsts, _, sems, local_sems):
        me = _mesh_pos()
        for a in range(n):
            for k in range(1, N_DEV):
                peer = _flip(me, k)
                _remote(block(srcs[a], a, _slot(peer)), dsts[a].at[_slot(me)], sems[a], sems[n + a], peer).start()

    return _split_call(name, body, list(blocked) + lands, [], True, after)


def _exchange_finish(name, started, after):
    group = started if isinstance(started, (list, tuple)) else [started]
    srcs_all = [b for st in group for b in st.bufs[:st.n]]
    lands_all = [b for st in group for b in st.bufs[st.n:]]
    sends = [s for st in group for s in st.sems[:st.n]]
    recvs = [s for st in group for s in st.sems[st.n:]]
    n = len(srcs_all)

    def body(srcs, dsts, sems, _, local_sems):
        me = _mesh_pos()
        for a in range(n):
            _wait_slots(dsts[a], N_DEV - 1, sems[a], sems[n + a], me, send=True, recv=True)

    return _split_call(name, body, srcs_all + lands_all, sends + recvs, 0, after).bufs[n:]


def _broadcast_start(name, arrays, after):
    n = len(arrays)
    my_slot = _slot(_mesh_pos())
    lands = [lax.dynamic_update_slice(lax.empty((N_DEV,) + w.shape, w.dtype), w[None], (my_slot, 0, 0)) for w in arrays]

    def body(srcs, dsts, _, sems, local_sems):
        me = _mesh_pos()
        for a in range(n):
            for k in range(1, N_DEV):
                _remote(srcs[a], dsts[a].at[_slot(me)], sems[a], sems[n + a], _flip(me, k)).start()

    return _split_call(name, body, list(arrays) + lands, [], True, after)


def _adamw_math(w, g, m, v):
    m = ADAM_B1 * m + (1.0 - ADAM_B1) * g
    v = ADAM_B2 * v + (1.0 - ADAM_B2) * (g * g)
    m_hat = m / (1.0 - ADAM_B1 ** ADAM_STEP)
    v_hat = v / (1.0 - ADAM_B2 ** ADAM_STEP)
    delta = -ADAM_LR * (m_hat / (jnp.sqrt(v_hat) + ADAM_EPS) + ADAM_WD * w)
    return delta, m, v


def _adamw(name, w, m, v, parts, layer, prev=None):
    _, r, c = w.shape
    tr = max(t for t in range(16, r + 1, 16) if r % t == 0 and t * c <= ADAMW_TILE_ELEMS)
    n_prev = 0 if prev is None else 4

    def body(*refs):
        w_ref, m_ref, v_ref, p_ref = refs[:4]
        g_ref, d_ref, nm_ref, nv_ref = refs[4 + n_prev:]
        g = p_ref[0].astype(F32)
        for src in range(1, N_DEV):
            g = g + p_ref[src].astype(F32)
        delta, nm, nv = _adamw_math(w_ref[...], g, m_ref[...], v_ref[...])
        g_ref[...] = g
        d_ref[...] = delta
        nm_ref[...] = nm
        nv_ref[...] = nv

    wspec = pl.BlockSpec((None, tr, c), lambda i: (layer, i, 0))
    pspec = pl.BlockSpec((N_DEV, tr, c), lambda i: (0, i, 0))
    shp = jax.ShapeDtypeStruct(w.shape, F32)
    return pl.pallas_call(
        body, name=name, grid=(r // tr,), in_specs=[wspec] * 3 + [pspec] + [ANY_SPEC] * n_prev,
        out_specs=(wspec,) * 4, out_shape=(shp,) * 4, input_output_aliases={4 + j: j for j in range(n_prev)},
        compiler_params=_cparams(("parallel",), 56),
    )(w, m, v, parts, *([] if prev is None else prev))


PACK_TILE = 8 * 128


def _pack(arrays):
    flat = []
    for a in arrays:
        v = a.reshape(-1)
        pad = (-v.shape[0]) % PACK_TILE
        flat.append(jnp.pad(v, (0, pad)) if pad else v)
    return jnp.concatenate(flat).reshape(-1, 128)


def _unpack(buf, like):
    flat = buf.reshape(-1)
    out, off = [], 0
    for a in like:
        n = 1
        for dim in a.shape:
            n *= dim
        out.append(flat[off:off + n].reshape(a.shape))
        off += n + (-n) % PACK_TILE
    return out


def _sum_slots(name, gathered):
    _, r, c = gathered.shape

    def body(x_ref, o_ref):
        acc = x_ref[0].astype(F32)
        for src in range(1, N_DEV):
            acc = acc + x_ref[src].astype(F32)
        o_ref[...] = acc

    return pl.pallas_call(body, name=name, out_shape=jax.ShapeDtypeStruct((r, c), F32))(gathered)


def _adamw_small(ws, gs, ms, vs):
    n = len(ws)

    def body(*refs):
        ins, outs = refs[:4 * n], refs[4 * n:]
        for i in range(n):
            delta, nm, nv = _adamw_math(ins[i][...], ins[n + i][...], ins[2 * n + i][...], ins[3 * n + i][...])
            outs[i][...] = delta
            outs[n + i][...] = nm
            outs[2 * n + i][...] = nv

    shapes = tuple(jax.ShapeDtypeStruct(w.shape, F32) for w in ws) * 3
    out = pl.pallas_call(body, name="adamw_small", out_shape=shapes)(*ws, *gs, *ms, *vs)
    return out[:n], out[n:2 * n], out[2 * n:]


def kernel(x, attn_norm_g, w_in, q_norm_g, k_norm_g, sgu_norm_g, sgu_w, sgu_b, out_norm_a_g, out_norm_b_g, w_out, ffn_norm_g, w_up, conv_w, conv_b, w_down, loss_target, m_attn_norm_g, m_w_in, m_q_norm_g, m_k_norm_g, m_sgu_norm_g, m_sgu_w, m_sgu_b, m_out_norm_a_g, m_out_norm_b_g, m_w_out, m_ffn_norm_g, m_w_up, m_conv_w, m_conv_b, m_w_down, v_attn_norm_g, v_w_in, v_q_norm_g, v_k_norm_g, v_sgu_norm_g, v_sgu_w, v_sgu_b, v_out_norm_a_g, v_out_norm_b_g, v_w_out, v_ffn_norm_g, v_w_up, v_conv_w, v_conv_b, v_w_down):
    depth = w_in.shape[0]
    s, d = x.shape[1], x.shape[2]
    n_heads = (d // 2) // HEAD_DIM
    sgu_col0 = 3 * n_heads
    f2 = w_up.shape[2] * N_DEV
    ff = f2 // 2
    my_slot = 4 * lax.axis_index("x") + 2 * lax.axis_index("y") + lax.axis_index("c")

    wb = [(w_in[l].astype(BF16), w_out[l].astype(BF16), w_up[l].astype(BF16), w_down[l].astype(BF16))
          for l in range(depth)]
    groups = {"in0": [wb[0][0]], "out0": [wb[0][1], conv_w.reshape(depth * CONV_WIDTH, -1)], "up0": [wb[0][2]],
              "down0": [wb[0][3]]}
    for l in range(1, depth):
        groups[f"in{l}"] = [wb[l][0], wb[l][1]]
        groups[f"ffn{l}"] = [wb[l][2], wb[l][3]]
    order = list(groups)
    started, relayed = {}, {}

    def start(gname, after):
        started[gname] = _gather_start(f"gather_{gname}_start", groups[gname], after)
        return started[gname].token

    def relay(gname, after):
        relayed[gname] = _gather_relay(f"gather_{gname}_relay", started[gname], after)
        token = relayed[gname].token
        k = order.index(gname)
        nxt = [k + 2] if k + 2 < len(order) - 1 else []
        if k == len(order) - 2:
            nxt = [k + 1]
        for j in nxt:
            token = start(order[j], token)
        return token

    def finish(gname, after):
        diagonal = _gather_relay_diagonal(f"gather_{gname}_diagonal", relayed[gname], after)
        return _gather_finish(f"gather_{gname}_finish", relayed[gname], diagonal, diagonal.token)

    conv_b_all = conv_b.reshape(depth, 2, 1, ff)
    sgu_b_col = sgu_b[..., None]
    token = start(order[1], start(order[0], attn_norm_g))
    token = relay("out0", relay("in0", token))
    win_g = finish("in0", token)[0]

    xs = x[0]
    saved = []
    gathered = []
    for l in range(depth):
        g1 = attn_norm_g[l][None]
        g2 = ffn_norm_g[l][None]
        gq, gk = q_norm_g[l][None], k_norm_g[l][None]
        ga = out_norm_a_g[l][:, None, :]
        gs = sgu_norm_g[l][:, None, :]
        gb = out_norm_b_g[l][:, None, :]
        h1 = _rmsnorm_fwd("attn_norm_fwd", xs, g1)
        p = _mm_nn_blocked("in_proj", h1, win_g, F32)
        att, o, rsum = _attn_fwd(p, gq, gk, ga, n_heads)
        token = relay("up0" if l == 0 else f"ffn{l}", att)
        sg = _sgu_fwd(p, gs, sgu_w[l], sgu_b_col[l], gb, sgu_col0, token)
        mix = jnp.concatenate([att, sg], axis=-1)
        if l == 0:
            wout_g, cw = finish("out0", mix)
            cw = jnp.transpose(cw.reshape(N_DEV, depth, CONV_WIDTH, -1), (1, 2, 0, 3)).reshape(depth, CONV_WIDTH, 2, ff)
            conv_w_all = jnp.transpose(cw, (0, 2, 1, 3))
        x1, h2 = _mm_nn_res_norm("out_proj_ffn_norm", mix, wout_g.reshape(d, d), xs, g2)
        if l == 0:
            wup_g = finish("up0", h2)[0]
            token = relay("down0", wup_g)
            up, act = _up_conv_gate_fwd(h2, wup_g, conv_w_all[l], conv_b_all[l], after=token)
            wdown_g = finish("down0", up)[0]
        else:
            wup_g, wdown_g = finish(f"ffn{l}", h2)
            up, act = _up_conv_gate_fwd(h2, wup_g, conv_w_all[l], conv_b_all[l])
        saved.append((xs, h1, p, o, rsum, mix, x1, h2, up, act))
        gathered.append((win_g, wout_g, wup_g, wdown_g))
        if l + 1 < depth:
            token = relay(f"in{l + 1}", act)
            x2 = _mm_nn_res("down_proj", act, wdown_g.reshape(ff, d), x1, after=token)
            win_g, wout_g = finish(f"in{l + 1}", x2)
        else:
            x2 = _mm_nn_res("down_proj", act, wdown_g.reshape(ff, d), x1)
        xs = x2

    dx, dxb, loss_vec = _loss_head(xs, loss_target[0])
    loss = lax.psum(loss_vec[0, 0], MESH_AXES)

    exchanges = []
    small = [None] * depth
    small_names = ["attn_norm_g", "q_norm_g", "k_norm_g", "sgu_norm_g", "sgu_w", "sgu_b", "out_norm_a_g",
                   "out_norm_b_g", "ffn_norm_g", "conv_b", "conv_w"]
    for l in reversed(range(depth)):
        xs0, h1, p, o, rsum, mix, x1, h2, up, act = saved[l]
        win_g, wout_g, wup_g, wdown_g = gathered[l]
        wout_full = wout_g.reshape(d, d)
        wdown_full = wdown_g.reshape(ff, d)
        g1 = attn_norm_g[l][None]
        g2 = ffn_norm_g[l][None]
        gq, gk = q_norm_g[l][None], k_norm_g[l][None]
        ga = out_norm_a_g[l][:, None, :]
        gs = sgu_norm_g[l][:, None, :]
        gb = out_norm_b_g[l][:, None, :]
        d_wdown = _mm_tn_plain("down_proj_dw", act, dxb)
        exchanges.append((l, "down", ("w_down",), _exchange_start(f"grad_down{l}_start", [d_wdown], dx)))
        dup, d_cw, d_cb = _down_dx_conv_gate_bwd(up, conv_w_all[l], conv_b_all[l], dxb, wdown_full,
                                                 exchanges[-1][3].token)
        d_wup = _mm_tn_blocked("up_proj_dw", h2, dup, N_DEV, halves=True)
        exchanges.append((l, "up", ("w_up",), _exchange_start(f"grad_up{l}_start", [d_wup], d_cb)))
        dh2 = _mm_nt_blocked("up_proj_dx", dup, wup_g, halves=True, after=exchanges[-1][3].token)
        dx, dxb, d_g2 = _rmsnorm_bwd("ffn_norm_bwd", dh2, x1, g2, dx)
        d_wout = _mm_tn_plain("out_proj_dw", mix, dxb)
        exchanges.append((l, "out", ("w_out",), _exchange_start(f"grad_out{l}_start", [d_wout], d_g2)))
        dmix = _mm_nt_plain("out_proj_dx", dxb, wout_full, after=exchanges[-1][3].token)
        dq, dk, dv, d_gq, d_gk, d_ga = _attn_bwd(p, gq, gk, ga, o, rsum, dmix, n_heads)
        du, dvs, d_gs, d_sw, d_sb, d_gb = _sgu_bwd(p, gs, sgu_w[l], sgu_b_col[l], gb, dmix, sgu_col0, n_heads)
        dp = jnp.concatenate([dq, dk, dv, du, dvs], axis=-1)
        d_win = _mm_tn_blocked("in_proj_dw", h1, dp, N_DEV)
        exchanges.append((l, "in", ("w_in",), _exchange_start(f"grad_in{l}_start", [d_win], d_gq)))
        dh1 = _mm_nt_blocked("in_proj_dx", dp, win_g, after=exchanges[-1][3].token)
        dx, dxb, d_g1 = _rmsnorm_bwd("attn_norm_bwd", dh1, xs0, g1, dx)
        small[l] = dict(attn_norm_g=d_g1[0], q_norm_g=d_gq[0], k_norm_g=d_gk[0], sgu_norm_g=d_gs[:, 0], sgu_w=d_sw,
                        sgu_b=d_sb[..., 0], out_norm_a_g=d_ga[:, 0], out_norm_b_g=d_gb[:, 0], ffn_norm_g=d_g2[0],
                        conv_w=jnp.transpose(d_cw, (1, 0, 2)).reshape(CONV_WIDTH, f2), conv_b=d_cb.reshape(f2))
    grad_x = dx[None]

    f32_names = [n for n in small_names if n != "sgu_w"]
    small_g = [jnp.stack([small[l][n] for l in range(depth)]) for n in f32_names]
    sgu_w_g = jnp.stack([small[l]["sgu_w"] for l in range(depth)])
    small_sent = _broadcast_start("grad_small_start", [_pack(small_g), sgu_w_g.reshape(-1, TILE).astype(BF16)], dx)

    res = {}
    big = dict(w_in=(w_in, m_w_in, v_w_in), w_out=(w_out, m_w_out, v_w_out), w_up=(w_up, m_w_up, v_w_up),
               w_down=(w_down, m_w_down, v_w_down))
    after = [small_sent.token]
    batches = [[e for e in exchanges if e[0] == l] for l in reversed(range(depth))]
    batches = batches[:-1] + [batches[-1][:-1], batches[-1][-1:]]
    for i, batch in enumerate(batches):
        landed = _exchange_finish(f"grad_batch{i}_finish", [ex for _, _, _, ex in batch], after)
        after = []
        for (layer, name), parts in zip([(k, n) for k, _, names, _ in batch for n in names], landed):
            w, m, v = big[name]
            res[name] = _adamw(f"adamw_{name}", w, m, v, parts, layer, res.get(name))
            after.append(res[name][0])
    small_all, sgu_w_all = _exchange_finish("grad_small_finish", small_sent, after)
    small_sum = _unpack(_sum_slots("small_grad_sum", small_all), small_g)
    g_small = dict(zip(f32_names, small_sum))
    g_small["sgu_w"] = _sum_slots("sgu_w_grad_sum", sgu_w_all).reshape(sgu_w.shape)
    cwn = conv_w.shape[2]
    g_small["conv_w"] = lax.dynamic_slice_in_dim(g_small["conv_w"], my_slot * cwn, cwn, axis=2)
    small_w = dict(attn_norm_g=(attn_norm_g, m_attn_norm_g, v_attn_norm_g), q_norm_g=(q_norm_g, m_q_norm_g, v_q_norm_g),
                   k_norm_g=(k_norm_g, m_k_norm_g, v_k_norm_g), sgu_norm_g=(sgu_norm_g, m_sgu_norm_g, v_sgu_norm_g),
                   sgu_w=(sgu_w, m_sgu_w, v_sgu_w), sgu_b=(sgu_b, m_sgu_b, v_sgu_b),
                   out_norm_a_g=(out_norm_a_g, m_out_norm_a_g, v_out_norm_a_g),
                   out_norm_b_g=(out_norm_b_g, m_out_norm_b_g, v_out_norm_b_g),
                   ffn_norm_g=(ffn_norm_g, m_ffn_norm_g, v_ffn_norm_g), conv_b=(conv_b, m_conv_b, v_conv_b),
                   conv_w=(conv_w, m_conv_w, v_conv_w))
    grads = [g_small[n].reshape(small_w[n][0].shape) for n in small_names]
    deltas, new_ms, new_vs = _adamw_small([small_w[n][0] for n in small_names], grads,
                                          [small_w[n][1] for n in small_names], [small_w[n][2] for n in small_names])
    for n, g, dlt, nm, nv in zip(small_names, grads, deltas, new_ms, new_vs):
        res[n] = (g, dlt, nm, nv)

    order = ["attn_norm_g", "w_in", "q_norm_g", "k_norm_g", "sgu_norm_g", "sgu_w", "sgu_b", "out_norm_a_g",
             "out_norm_b_g", "w_out", "ffn_norm_g", "w_up", "conv_w", "conv_b", "w_down"]
    outs = [loss, grad_x]
    for field in range(4):
        outs += [res[n][field] for n in order]
    return tuple(outs)
```

```python
import functools

import jax
import jax.numpy as jnp
from jax import lax
from jax.experimental import pallas as pl
from jax.experimental.pallas import tpu as pltpu

F32 = jnp.float32
BF16 = jnp.bfloat16
EPS = 1e-6
HEAD_DIM = 128
TILE = 128
ATTN_VMEM_MB = 58
ATTN_HEADS_PER_STEP = 4
N_GROUPS = 8
CONV_WIDTH = 3
N_DEV = 8
MESH_AXES = ("x", "y", "c")
MIB = 1024 * 1024

ADAM_LR = 0.001
ADAM_B1 = 0.9
ADAM_B2 = 0.999
ADAM_EPS = 1e-08
ADAM_WD = 0.01
ADAM_STEP = 10
ADAMW_TILE_ELEMS = 512 * 1024

NT_DIMS = (((1,), (1,)), ((), ()))
NN_DIMS = (((1,), (0,)), ((), ()))
TN_DIMS = (((0,), (0,)), ((), ()))


def _cparams(sem, vmem_mb=48):
    return pltpu.CompilerParams(dimension_semantics=sem, vmem_limit_bytes=vmem_mb * MIB)


def _pick(n, cands):
    for c in cands:
        if n % c == 0:
            return c
    return n


MXU_WIDTH = 256


def _pairs(nb, bn):
    return nb % 2 == 0 and bn % MXU_WIDTH != 0 and (2 * bn) % MXU_WIDTH == 0


def _mm(name, grid, ins, in_specs, out_shape, out_spec, dims, has_res=False, parts=None, vmem_mb=56, after=None,
        split_out=None):
    n_in = 2 + has_res + (after is not None)
    if after is not None:
        ins = tuple(ins) + (after,)
        in_specs = list(in_specs) + [pl.BlockSpec(after.shape, lambda *_: (0, 0))]

    def body(*refs):
        a_ref, b_ref = refs[:2]
        o_ref = refs[n_in]
        if parts is None:
            acc = lax.dot_general(a_ref[...], b_ref[...], dims, preferred_element_type=F32)
        else:
            acc = None
            for part in parts:
                a, b = part(a_ref, b_ref)
                prod = lax.dot_general(a, b, dims, preferred_element_type=F32)
                acc = prod if acc is None else acc + prod
        if has_res:
            acc = acc + refs[2][...]
        if split_out is None:
            o_ref[...] = acc.astype(o_ref.dtype)
        else:
            o_ref[0] = acc[:, :split_out].astype(o_ref.dtype)
            o_ref[1] = acc[:, split_out:].astype(o_ref.dtype)

    return pl.pallas_call(
        body, name=name, grid=grid, in_specs=in_specs, out_specs=out_spec, out_shape=out_shape,
        compiler_params=_cparams(("parallel",) * len(grid), vmem_mb),
    )(*ins)


def _two_blocks(b_ref, first):
    return jnp.concatenate([b_ref[first], b_ref[first + 1]], axis=1)


def _rows_for(m, row_bytes, budget):
    return _pick(m, tuple(t for t in (2048, 1024, 512, 256, 128) if t * row_bytes <= budget))


def _mm_nn_blocked(name, a, wb, out_dtype, after=None):
    m, k = a.shape
    nb, _, bn = wb.shape
    out_shape = jax.ShapeDtypeStruct((m, nb * bn), out_dtype)
    if _pairs(nb, bn):
        tm = _rows_for(m, 2 * bn * jnp.dtype(out_dtype).itemsize, 6 * MIB)
        return _mm(name, (nb // 2, m // tm), (a, wb),
                   [pl.BlockSpec((tm, k), lambda j, i: (i, 0)), pl.BlockSpec((2, k, bn), lambda j, i: (j, 0, 0))],
                   out_shape, pl.BlockSpec((tm, 2 * bn), lambda j, i: (i, j)), NN_DIMS,
                   parts=[lambda a_ref, b_ref: (a_ref[...], _two_blocks(b_ref, 0))], after=after)
    tm = _rows_for(m, bn * jnp.dtype(out_dtype).itemsize, 6 * MIB)
    a_spec = pl.BlockSpec((tm, k), lambda j, i: (i, 0))
    b_spec = pl.BlockSpec((None, k, bn), lambda j, i: (j, 0, 0))
    o_spec = pl.BlockSpec((tm, bn), lambda j, i: (i, j))
    return _mm(name, (nb, m // tm), (a, wb), [a_spec, b_spec], out_shape, o_spec, NN_DIMS, after=after)


def _mm_nn_res_norm(name, a, w, res, g):
    m, k = a.shape
    n = w.shape[1]
    tm = _pick(m, (512, 256, 128))

    def body(a_ref, w_ref, r_ref, g_ref, x_ref, h_ref):
        x = jnp.dot(a_ref[...], w_ref[...], preferred_element_type=F32) + r_ref[...]
        x_ref[...] = x
        h_ref[...] = (x * _rstd(x) * g_ref[...]).astype(BF16)

    row = pl.BlockSpec((tm, n), lambda i: (i, 0))
    return pl.pallas_call(
        body, name=name, grid=(m // tm,),
        in_specs=[pl.BlockSpec((tm, k), lambda i: (i, 0)), pl.BlockSpec((k, n), lambda i: (0, 0)), row,
                  pl.BlockSpec((1, n), lambda i: (0, 0))],
        out_specs=(row, row), out_shape=(jax.ShapeDtypeStruct((m, n), F32), jax.ShapeDtypeStruct((m, n), BF16)),
        compiler_params=_cparams(("parallel",), 56),
    )(a, w, res, g)


def _mm_nn_res(name, a, w, res, after=None):
    m, k = a.shape
    n = w.shape[1]
    tm = _pick(m, (512, 256, 128))
    tn = _rows_for(n, k * 2, 12 * MIB)
    a_spec = pl.BlockSpec((tm, k), lambda j, i: (i, 0))
    b_spec = pl.BlockSpec((k, tn), lambda j, i: (0, j))
    r_spec = pl.BlockSpec((tm, tn), lambda j, i: (i, j))
    o_spec = pl.BlockSpec((tm, tn), lambda j, i: (i, j))
    return _mm(name, (n // tn, m // tm), (a, w, res), [a_spec, b_spec, r_spec], jax.ShapeDtypeStruct((m, n), F32),
               o_spec, NN_DIMS, has_res=True, after=after)


def _mm_nt_blocked(name, dy, wb, halves=False, after=None):
    nb, n, bn = wb.shape
    m = dy.shape[-2]
    tm = _pick(m, (512, 256, 128))
    tn = _rows_for(n, nb * bn * 2, 12 * MIB)
    if halves:
        hb = nb // 2
        a_spec = pl.BlockSpec((2, tm, hb * bn), lambda j, i: (0, i, 0))
        a_part = lambda kk: (lambda a_ref: a_ref[kk // hb, :, (kk % hb) * bn:(kk % hb + 1) * bn])
    else:
        a_spec = pl.BlockSpec((tm, nb * bn), lambda j, i: (i, 0))
        a_part = lambda kk: (lambda a_ref: a_ref[:, kk * bn:(kk + 1) * bn])
    if _pairs(nb, bn) and not halves:
        parts = [(lambda a_ref, b_ref, kk=kk: (a_ref[:, kk * bn:(kk + 2) * bn], _two_blocks(b_ref, kk)))
                 for kk in range(0, nb, 2)]
    elif _pairs(nb // 2, bn) and halves:
        parts = [(lambda a_ref, b_ref, kk=kk: (a_ref[kk // hb, :, (kk % hb) * bn:(kk % hb + 2) * bn],
                                               _two_blocks(b_ref, kk))) for kk in range(0, nb, 2)]
    else:
        parts = [(lambda a_ref, b_ref, kk=kk, sel=a_part(kk): (sel(a_ref), b_ref[kk])) for kk in range(nb)]
    b_spec = pl.BlockSpec((nb, tn, bn), lambda j, i: (0, j, 0))
    o_spec = pl.BlockSpec((tm, tn), lambda j, i: (i, j))
    return _mm(name, (n // tn, m // tm), (dy, wb), [a_spec, b_spec], jax.ShapeDtypeStruct((m, n), F32), o_spec,
               NT_DIMS, parts=parts, after=after)


def _mm_nt_plain(name, dy, w, out_dtype=F32, after=None):
    m, k = dy.shape
    n = w.shape[0]
    tm = _rows_for(m, k * 2, 8 * MIB)
    tn = _pick(n, (512, 256, 128))
    a_spec = pl.BlockSpec((tm, k), lambda j, i: (i, 0))
    b_spec = pl.BlockSpec((tn, k), lambda j, i: (j, 0))
    o_spec = pl.BlockSpec((tm, tn), lambda j, i: (i, j))
    return _mm(name, (n // tn, m // tm), (dy, w), [a_spec, b_spec], jax.ShapeDtypeStruct((m, n), out_dtype), o_spec,
               NT_DIMS, after=after)


def _mm_tn_blocked(name, a, dy, nb, halves=False):
    s, k1 = a.shape
    bn = (dy.shape[-1] * (2 if halves else 1)) // nb
    if _pairs(nb, bn) and not halves:
        tm = _rows_for(k1, 2 * bn * 2, 6 * MIB)
        return _mm(name, (nb // 2, k1 // tm), (a, dy),
                   [pl.BlockSpec((s, tm), lambda j, i: (0, i)), pl.BlockSpec((s, 2 * bn), lambda j, i: (0, j))],
                   jax.ShapeDtypeStruct((nb, k1, bn), BF16), pl.BlockSpec((2, tm, bn), lambda j, i: (j, i, 0)), TN_DIMS,
                   split_out=bn)
    if halves and _pairs(nb // 2, bn):
        tm = _rows_for(k1, 2 * bn * 2, 6 * MIB)
        per_half = nb // 4
        return _mm(name, (nb // 2, k1 // tm), (a, dy),
                   [pl.BlockSpec((s, tm), lambda j, i: (0, i)),
                    pl.BlockSpec((None, s, 2 * bn), lambda j, i: (j // per_half, 0, j % per_half))],
                   jax.ShapeDtypeStruct((nb, k1, bn), BF16), pl.BlockSpec((2, tm, bn), lambda j, i: (j, i, 0)), TN_DIMS,
                   split_out=bn)
    tm = _rows_for(k1, bn * 2, 6 * MIB)
    a_spec = pl.BlockSpec((s, tm), lambda j, i: (0, i))
    if halves:
        hb = nb // 2
        b_spec = pl.BlockSpec((None, s, bn), lambda j, i: (j // hb, 0, j % hb))
    else:
        b_spec = pl.BlockSpec((s, bn), lambda j, i: (0, j))
    o_spec = pl.BlockSpec((None, tm, bn), lambda j, i: (j, i, 0))
    return _mm(name, (nb, k1 // tm), (a, dy), [a_spec, b_spec], jax.ShapeDtypeStruct((nb, k1, bn), BF16), o_spec,
               TN_DIMS)


def _mm_tn_plain(name, a, dy):
    s, k1 = a.shape
    n = dy.shape[1]
    tm = _pick(k1, (512, 256, 128))
    tn = _rows_for(n, s * 2, 8 * MIB)
    a_spec = pl.BlockSpec((s, tm), lambda i, j: (0, i))
    b_spec = pl.BlockSpec((s, tn), lambda i, j: (0, j))
    o_spec = pl.BlockSpec((tm, tn), lambda i, j: (i, j))
    return _mm(name, (k1 // tm, n // tn), (a, dy), [a_spec, b_spec], jax.ShapeDtypeStruct((k1, n), BF16), o_spec,
               TN_DIMS)


def _rstd(x):
    return lax.rsqrt(jnp.mean(x * x, axis=-1, keepdims=True) + EPS)


def _norm_bwd(dy, xhat, r, g):
    dxhat = dy * g
    return r * (dxhat - xhat * jnp.mean(dxhat * xhat, axis=-1, keepdims=True))


def _rmsnorm_fwd(name, x, g):
    s, d = x.shape
    tr = _pick(s, (256, 128))

    def body(x_ref, g_ref, h_ref):
        xv = x_ref[...]
        h_ref[...] = (xv * _rstd(xv) * g_ref[...]).astype(BF16)

    return pl.pallas_call(
        body, name=name, grid=(s // tr,),
        in_specs=[pl.BlockSpec((tr, d), lambda i: (i, 0)), pl.BlockSpec((1, d), lambda i: (0, 0))],
        out_specs=pl.BlockSpec((tr, d), lambda i: (i, 0)),
        out_shape=jax.ShapeDtypeStruct((s, d), BF16), compiler_params=_cparams(("parallel",)),
    )(x, g)


def _rmsnorm_bwd(name, dh, x, g, dres):
    s, d = x.shape
    tr = _pick(s, (256, 128))

    def body(dh_ref, x_ref, g_ref, dres_ref, dx_ref, dxb_ref, dg_ref):
        xv = x_ref[...]
        r = _rstd(xv)
        xhat = xv * r
        dhv = dh_ref[...]
        dx = dres_ref[...] + _norm_bwd(dhv, xhat, r, g_ref[...])
        dx_ref[...] = dx
        dxb_ref[...] = dx.astype(BF16)
        part = jnp.sum(dhv * xhat, axis=0, keepdims=True)

        @pl.when(pl.program_id(0) == 0)
        def _():
            dg_ref[...] = part

        @pl.when(pl.program_id(0) > 0)
        def _():
            dg_ref[...] += part

    row = pl.BlockSpec((tr, d), lambda i: (i, 0))
    vec = pl.BlockSpec((1, d), lambda i: (0, 0))
    return pl.pallas_call(
        body, name=name, grid=(s // tr,), in_specs=[row, row, vec, row], out_specs=(row, row, vec),
        out_shape=(jax.ShapeDtypeStruct((s, d), F32), jax.ShapeDtypeStruct((s, d), BF16),
                   jax.ShapeDtypeStruct((1, d), F32)),
        compiler_params=_cparams(("arbitrary",)),
    )(dh, x, g, dres)


def _loss_head(y, target):
    s, d = y.shape
    tr = _pick(s, (256, 128))

    def body(y_ref, t_ref, dy_ref, dyb_ref, loss_ref):
        err = y_ref[...] - t_ref[...]
        dy = err * (1.0 / d)
        dy_ref[...] = dy
        dyb_ref[...] = dy.astype(BF16)
        part = 0.5 * jnp.sum(jnp.mean(err * err, axis=-1, keepdims=True), axis=0, keepdims=True)
        part = jnp.broadcast_to(part, (1, 128))

        @pl.when(pl.program_id(0) == 0)
        def _():
            loss_ref[...] = part

        @pl.when(pl.program_id(0) > 0)
        def _():
            loss_ref[...] += part

    row = pl.BlockSpec((tr, d), lambda i: (i, 0))
    return pl.pallas_call(
        body, name="loss_head", grid=(s // tr,), in_specs=[row, row],
        out_specs=(row, row, pl.BlockSpec((1, 128), lambda i: (0, 0))),
        out_shape=(jax.ShapeDtypeStruct((s, d), F32), jax.ShapeDtypeStruct((s, d), BF16),
                   jax.ShapeDtypeStruct((1, 128), F32)),
        compiler_params=_cparams(("arbitrary",)),
    )(y, target)


def _split_dot(x, tri):
    hi = x.astype(BF16)
    lo = (x - hi.astype(F32)).astype(BF16)
    return (jnp.dot(hi, tri, preferred_element_type=F32) + jnp.dot(lo, tri, preferred_element_type=F32))


def _tile_iotas():
    row = lax.broadcasted_iota(jnp.int32, (TILE, TILE), 0)
    col = lax.broadcasted_iota(jnp.int32, (TILE, TILE), 1)
    return row, col


def _sb_logits(qi, kb, mask):
    z = lax.dot_general(qi, kb, NT_DIMS, preferred_element_type=F32) * (HEAD_DIM ** -0.5)
    sp = jnp.log(1.0 + jnp.exp(-jnp.abs(z)))
    lb = jnp.minimum(z, 0.0) - sp
    l1m = -jnp.maximum(z, 0.0) - sp
    if mask is not None:
        l1m = jnp.where(mask, l1m, 0.0)
    return lb, l1m


def _attn_fwd(p, gq, gk, ga, n_heads, first=0, count=None, prev=None, after=None):
    s = p.shape[0]
    nq = s // TILE

    hp = ATTN_HEADS_PER_STEP
    wd = hp * HEAD_DIM

    def body(q_ref, k_ref, v_ref, gq_ref, gk_ref, ga_ref, *rest):
        att_ref, o_ref, r_ref, qn_s, kn_s, vb_s = rest[-6:]
        heads = [slice(hh * HEAD_DIM, (hh + 1) * HEAD_DIM) for hh in range(hp)]
        for hd in heads:
            qv = q_ref[:, hd]
            qn_s[:, hd] = (qv * _rstd(qv) * gq_ref[...]).astype(BF16)
            kv = k_ref[:, hd]
            kn_s[:, hd] = (kv * _rstd(kv) * gk_ref[...]).astype(BF16)
        vb_s[...] = v_ref[...].astype(BF16)
        row, col = _tile_iotas()
        causal = col < row
        upper_ones = jnp.concatenate([(row > col).astype(BF16), jnp.ones((TILE, TILE), BF16)], axis=1)

        def tiles(rows, key_blocks, states):
            chains = [(hi, hd, keys, mask) for hi, hd in enumerate(heads) for keys, mask in key_blocks]
            logits = [_sb_logits(qn_s[rows, hd], kn_s[keys, hd], mask) for _, hd, keys, mask in chains]
            sums = [_split_dot(l1m, upper_ones) for _, l1m in logits]
            carry = [c for _, c in states]
            probs = []
            for (hi, _, _, mask), (lb, _), sm in zip(chains, logits, sums):
                a = jnp.exp(lb + sm[:, :TILE] + carry[hi])
                carry[hi] = carry[hi] + sm[:, TILE:]
                probs.append((a if mask is None else jnp.where(mask, a, 0.0)).astype(BF16))
            outs = [jnp.dot(a, vb_s[keys, hd], preferred_element_type=F32) for a, (_, hd, keys, _) in zip(probs, chains)]
            acc = [o_acc for o_acc, _ in states]
            for (hi, _, _, _), o in zip(chains, outs):
                acc[hi] = acc[hi] + o
            return tuple(zip(acc, carry))

        def key_block(b):
            return pl.ds(pl.multiple_of(b * TILE, TILE), TILE), None

        def qblock(i, _):
            rows = pl.ds(pl.multiple_of(i * TILE, TILE), TILE)
            zero = jnp.zeros((TILE, HEAD_DIM), F32)
            states = tuple((zero, zero) for _ in heads)
            states = lax.cond(i % 2 == 1, lambda st: tiles(rows, [(rows, causal), key_block(i - 1)], st),
                              lambda st: tiles(rows, [(rows, causal)], st), states)
            top = i - i % 2

            def kblocks(jj, states):
                return tiles(rows, [key_block(top - 1 - 2 * jj), key_block(top - 2 - 2 * jj)], states)

            states = lax.fori_loop(0, i // 2, kblocks, states)
            for hh, (hd, (o_acc, c)) in enumerate(zip(heads, states)):
                o_ref[rows, hd] = o_acc
                r_ref[rows, hd] = c
                att_ref[rows, hd] = (o_acc * _rstd(o_acc) * ga_ref[hh]).astype(BF16)
            return 0

        lax.fori_loop(0, nq, qblock, 0)

    steps = n_heads // hp
    count = steps if count is None else count
    col_blk = lambda off: pl.BlockSpec((s, wd), lambda h: (0, off + first + h))
    vec = pl.BlockSpec((1, HEAD_DIM), lambda h: (0, 0))
    hvec = pl.BlockSpec((hp, 1, HEAD_DIM), lambda h: (first + h, 0, 0))
    out = pl.BlockSpec((s, wd), lambda h: (0, first + h))
    w = n_heads * HEAD_DIM
    n_prev = 0 if prev is None else 3
    extra = [] if after is None else [after]
    return pl.pallas_call(
        body, name="attn_fwd", grid=(count,),
        in_specs=[col_blk(0), col_blk(steps), col_blk(2 * steps), vec, vec, hvec] + [ANY_SPEC] * n_prev
        + [pl.BlockSpec(a.shape, lambda h: (0, 0)) for a in extra],
        out_specs=(out, out, out),
        out_shape=(jax.ShapeDtypeStruct((s, w), BF16), jax.ShapeDtypeStruct((s, w), F32),
                   jax.ShapeDtypeStruct((s, w), F32)),
        input_output_aliases={6 + j: j for j in range(n_prev)},
        scratch_shapes=[pltpu.VMEM((s, wd), BF16)] * 3,
        compiler_params=_cparams(("parallel",), ATTN_VMEM_MB),
    )(p, p, p, gq, gk, ga, *([] if prev is None else prev), *extra)


def _attn_bwd(p, gq, gk, ga, o, rsum, dmix, n_heads):
    s = p.shape[0]
    nq = s // TILE

    hp = ATTN_HEADS_PER_STEP
    wd = hp * HEAD_DIM
    scale = HEAD_DIM ** -0.5

    def body(q_ref, k_ref, v_ref, gq_ref, gk_ref, ga_ref, o_ref, r_ref, dm_ref,
             dq_ref, dk_ref, dv_ref, dgq_ref, dgk_ref, dga_ref,
             qn_s, kn_s, vb_s, do_s, dqn_s, dkn_s, dv_s):
        step = pl.program_id(0)
        gqv, gkv = gq_ref[...], gk_ref[...]
        heads = [slice(hh * HEAD_DIM, (hh + 1) * HEAD_DIM) for hh in range(hp)]
        for hh, hd in enumerate(heads):
            qv = q_ref[:, hd]
            qn_s[:, hd] = (qv * _rstd(qv) * gqv).astype(BF16)
            kv = k_ref[:, hd]
            kn_s[:, hd] = (kv * _rstd(kv) * gkv).astype(BF16)
            ov = o_ref[:, hd]
            ro = _rstd(ov)
            ohat = ov * ro
            dm = dm_ref[:, hd]
            dga_ref[hh] = jnp.sum(dm * ohat, axis=0, keepdims=True)
            do_s[:, hd] = _norm_bwd(dm, ohat, ro, ga_ref[hh]).astype(BF16)
        vb_s[...] = v_ref[...].astype(BF16)
        dkn_s[...] = jnp.zeros_like(dkn_s)
        dv_s[...] = jnp.zeros_like(dv_s)
        row, col = _tile_iotas()
        causal = col < row
        ones = jnp.ones((TILE, TILE), BF16)
        incl_ones = jnp.concatenate([(row <= col).astype(BF16), ones], axis=1)
        excl_ones = jnp.concatenate([(row < col).astype(BF16), ones], axis=1)

        def tiles(rows, key_blocks, states):
            chains = [(hi, hd, keys, mask) for hi, hd in enumerate(heads) for keys, mask in key_blocks]
            qis = [qn_s[rows, hd] for hd in heads]
            dois = [do_s[rows, hd] for hd in heads]
            logits = [_sb_logits(qis[hi], kn_s[keys, hd], mask) for hi, hd, keys, mask in chains]
            sums = [_split_dot(l1m, incl_ones) for _, l1m in logits]
            das = [lax.dot_general(dois[hi], vb_s[keys, hd], NT_DIMS, preferred_element_type=F32)
                   for hi, hd, keys, _ in chains]
            pfx = [st[1] for st in states]
            probs, dss = [], []
            for (hi, hd, _, mask), (lb, _), sm, da in zip(chains, logits, sums, das):
                a = jnp.exp(lb + (r_ref[rows, hd] - pfx[hi] - sm[:, :TILE]))
                pfx[hi] = pfx[hi] + sm[:, TILE:]
                a = a if mask is None else jnp.where(mask, a, 0.0)
                probs.append(a.astype(BF16))
                dss.append(da * a)
            dsums = [_split_dot(ds, excl_ones) for ds in dss]
            pc = [st[2] for st in states]
            dzs = []
            for (hi, _, _, mask), (lb, _), ds, dsm in zip(chains, logits, dss, dsums):
                dl1m = pc[hi] + dsm[:, :TILE]
                pc[hi] = pc[hi] + dsm[:, TILE:]
                dl1m = dl1m if mask is None else jnp.where(mask, dl1m, 0.0)
                beta = jnp.exp(lb)
                dzs.append(((ds * (1.0 - beta) - dl1m * beta) * scale).astype(BF16))
            dqs = [jnp.dot(dz, kn_s[keys, hd], preferred_element_type=F32) for dz, (_, hd, keys, _) in zip(dzs, chains)]
            for dz, a, (hi, hd, keys, _) in zip(dzs, probs, chains):
                dkn_s[keys, hd] += lax.dot_general(dz, qis[hi], TN_DIMS, preferred_element_type=F32)
                dv_s[keys, hd] += lax.dot_general(a, dois[hi], TN_DIMS, preferred_element_type=F32)
            dq_acc = [st[0] for st in states]
            for (hi, _, _, _), dq in zip(chains, dqs):
                dq_acc[hi] = dq_acc[hi] + dq
            return tuple(zip(dq_acc, pfx, pc))

        def key_block(b):
            return pl.ds(pl.multiple_of(b * TILE, TILE), TILE), None

        def qblock(i, _):
            rows = pl.ds(pl.multiple_of(i * TILE, TILE), TILE)
            zero = jnp.zeros((TILE, HEAD_DIM), F32)

            def kblocks(jj, states):
                return tiles(rows, [key_block(2 * jj), key_block(2 * jj + 1)], states)

            states = lax.fori_loop(0, i // 2, kblocks, tuple((zero, zero, zero) for _ in heads))
            states = lax.cond(i % 2 == 1, lambda st: tiles(rows, [key_block(i - 1), (rows, causal)], st),
                              lambda st: tiles(rows, [(rows, causal)], st), states)
            for hd, (dq_acc, _, _) in zip(heads, states):
                dqn_s[rows, hd] = dq_acc
            return 0

        lax.fori_loop(0, nq, qblock, 0)

        def norm_in_bwd(x_ref, g, dn_s, dx_ref, dg_ref):
            part = jnp.zeros((1, HEAD_DIM), F32)
            for hd in heads:
                xv = x_ref[:, hd]
                r = _rstd(xv)
                xhat = xv * r
                dn = dn_s[:, hd]
                dx_ref[:, hd] = _norm_bwd(dn, xhat, r, g).astype(BF16)
                part = part + jnp.sum(dn * xhat, axis=0, keepdims=True)

            @pl.when(step == 0)
            def _():
                dg_ref[...] = part

            @pl.when(step > 0)
            def _():
                dg_ref[...] += part

        norm_in_bwd(q_ref, gqv, dqn_s, dq_ref, dgq_ref)
        norm_in_bwd(k_ref, gkv, dkn_s, dk_ref, dgk_ref)
        dv_ref[...] = dv_s[...].astype(BF16)

    once = pl.Buffered(1)
    steps = n_heads // hp
    col_blk = lambda off: pl.BlockSpec((s, wd), lambda h: (0, off + h), pipeline_mode=once)
    vec = pl.BlockSpec((1, HEAD_DIM), lambda h: (0, 0))
    hvec = pl.BlockSpec((hp, 1, HEAD_DIM), lambda h: (h, 0, 0))
    blk = pl.BlockSpec((s, wd), lambda h: (0, h), pipeline_mode=once)
    w = n_heads * HEAD_DIM
    big = jax.ShapeDtypeStruct((s, w), BF16)
    return pl.pallas_call(
        body, name="attn_bwd", grid=(steps,),
        in_specs=[col_blk(0), col_blk(steps), col_blk(2 * steps), vec, vec, hvec, blk, blk, blk],
        out_specs=(blk, blk, blk, vec, vec, hvec),
        out_shape=(big, big, big, jax.ShapeDtypeStruct((1, HEAD_DIM), F32), jax.ShapeDtypeStruct((1, HEAD_DIM), F32),
                   jax.ShapeDtypeStruct((n_heads, 1, HEAD_DIM), F32)),
        scratch_shapes=[pltpu.VMEM((s, wd), BF16)] * 4 + [pltpu.VMEM((s, wd), F32)] * 3,
        compiler_params=_cparams(("arbitrary",), ATTN_VMEM_MB),
    )(p, p, p, gq, gk, ga, o, rsum, dmix)


_INV_SQRT2 = 0.7071067811865476
_INV_SQRT_2PI = 0.3989422804014327


def _gelu(x):
    return 0.5 * x * (1.0 + lax.erf(x * _INV_SQRT2))


def _gelu_grad(x):
    return 0.5 * (1.0 + lax.erf(x * _INV_SQRT2)) + x * (_INV_SQRT_2PI * jnp.exp(-0.5 * x * x))


def _sgu_fwd(p, gs, w_s, b_s, gb, col0, after):
    s = p.shape[0]
    n_chunks = s // TILE
    per_trip = _pick(n_chunks, (4, 2, 1))

    def body(u_ref, v_ref, gs_ref, w_ref, b_ref, gb_ref, _, out_ref, vs_s):
        vg = _gelu(v_ref[...])
        vs_s[...] = (vg * _rstd(vg) * gs_ref[...]).astype(BF16)
        row, col = _tile_iotas()
        wt = jnp.where(col <= row, w_ref[...], 0.0).astype(BF16)
        bcol = b_ref[...]
        gbv = gb_ref[...]

        def chunks(c, _):
            rows = [pl.ds(pl.multiple_of((c * per_trip + k) * TILE, TILE), TILE) for k in range(per_trip)]
            mixed = [jnp.dot(wt, vs_s[r, :], preferred_element_type=F32) + bcol for r in rows]
            sgs = [_gelu(u_ref[r, :]) * mx for r, mx in zip(rows, mixed)]
            for r, sg in zip(rows, sgs):
                out_ref[r, :] = (sg * _rstd(sg) * gbv).astype(BF16)
            return 0

        lax.fori_loop(0, n_chunks // per_trip, chunks, 0)

    col_blk = lambda off: pl.BlockSpec((s, HEAD_DIM), lambda g: (0, off + g))
    gvec = pl.BlockSpec((None, 1, HEAD_DIM), lambda g: (g, 0, 0))
    return pl.pallas_call(
        body, name="sgu_fwd", grid=(N_GROUPS,),
        in_specs=[col_blk(col0), col_blk(col0 + N_GROUPS), gvec,
                  pl.BlockSpec((None, TILE, TILE), lambda g: (g, 0, 0)),
                  pl.BlockSpec((None, TILE, 1), lambda g: (g, 0, 0)), gvec,
                  pl.BlockSpec(after.shape, lambda g: (0, 0))],
        out_specs=pl.BlockSpec((s, HEAD_DIM), lambda g: (0, g)),
        out_shape=jax.ShapeDtypeStruct((s, N_GROUPS * HEAD_DIM), BF16),
        scratch_shapes=[pltpu.VMEM((s, HEAD_DIM), BF16)],
        compiler_params=_cparams(("parallel",)),
    )(p, p, gs, w_s, b_s, gb, after)


def _sgu_bwd(p, gs, w_s, b_s, gb, dmix, col0, dm_col0):
    s = p.shape[0]
    n_chunks = s // TILE
    per_trip = _pick(n_chunks, (4, 2, 1))

    def body(u_ref, v_ref, gs_ref, w_ref, b_ref, gb_ref, dm_ref,
             du_ref, dv_ref, dgs_ref, dw_ref, db_ref, dgb_ref, vs_s, dvs_s):
        gsv = gs_ref[...]
        gbv = gb_ref[...]
        vg = _gelu(v_ref[...])
        vs_s[...] = (vg * _rstd(vg) * gsv).astype(BF16)
        row, col = _tile_iotas()
        causal = col <= row
        wt = jnp.where(causal, w_ref[...], 0.0).astype(BF16)
        bcol = b_ref[...]

        def chunks(c, carry):
            dw_acc, db_acc, dgb_acc = carry
            rows = [pl.ds(pl.multiple_of((c * per_trip + k) * TILE, TILE), TILE) for k in range(per_trip)]
            vss = [vs_s[r, :] for r in rows]
            mixed = [jnp.dot(wt, vs, preferred_element_type=F32) + bcol for vs in vss]
            dmbs = []
            for r, mx in zip(rows, mixed):
                u_pre = u_ref[r, :]
                u = _gelu(u_pre)
                sg = u * mx
                rs = _rstd(sg)
                sghat = sg * rs
                dm = dm_ref[r, :]
                dsg = _norm_bwd(dm, sghat, rs, gbv)
                dgb_acc = dgb_acc + jnp.sum(dm * sghat, axis=0, keepdims=True)
                du_ref[r, :] = (dsg * mx * _gelu_grad(u_pre)).astype(BF16)
                dmixed = dsg * u
                db_acc = db_acc + jnp.sum(dmixed, axis=1, keepdims=True)
                dmbs.append(dmixed.astype(BF16))
            for dmb, vs in zip(dmbs, vss):
                dw_acc = dw_acc + lax.dot_general(dmb, vs, NT_DIMS, preferred_element_type=F32)
            for r, dmb in zip(rows, dmbs):
                dvs_s[r, :] = lax.dot_general(wt, dmb, TN_DIMS, preferred_element_type=F32)
            return dw_acc, db_acc, dgb_acc

        dw_acc, db_acc, dgb_acc = lax.fori_loop(
            0, n_chunks // per_trip, chunks,
            (jnp.zeros((TILE, TILE), F32), jnp.zeros((TILE, 1), F32), jnp.zeros((1, HEAD_DIM), F32)))
        dw_ref[...] = jnp.where(causal, dw_acc, 0.0)
        db_ref[...] = db_acc
        dgb_ref[...] = dgb_acc
        v_pre = v_ref[...]
        vg = _gelu(v_pre)
        rv = _rstd(vg)
        vhat = vg * rv
        dvs = dvs_s[...]
        dgs_ref[...] = jnp.sum(dvs * vhat, axis=0, keepdims=True)
        dv_ref[...] = (_norm_bwd(dvs, vhat, rv, gsv) * _gelu_grad(v_pre)).astype(BF16)

    col_blk = lambda off: pl.BlockSpec((s, HEAD_DIM), lambda g: (0, off + g))
    gvec = pl.BlockSpec((None, 1, HEAD_DIM), lambda g: (g, 0, 0))
    wspec = pl.BlockSpec((None, TILE, TILE), lambda g: (g, 0, 0))
    bspec = pl.BlockSpec((None, TILE, 1), lambda g: (g, 0, 0))
    blk = pl.BlockSpec((s, HEAD_DIM), lambda g: (0, g))
    big = jax.ShapeDtypeStruct((s, N_GROUPS * HEAD_DIM), BF16)
    gshape = jax.ShapeDtypeStruct((N_GROUPS, 1, HEAD_DIM), F32)
    return pl.pallas_call(
        body, name="sgu_bwd", grid=(N_GROUPS,),
        in_specs=[col_blk(col0), col_blk(col0 + N_GROUPS), gvec, wspec, bspec, gvec, col_blk(dm_col0)],
        out_specs=(blk, blk, gvec, wspec, bspec, gvec),
        out_shape=(big, big, gshape, jax.ShapeDtypeStruct((N_GROUPS, TILE, TILE), F32),
                   jax.ShapeDtypeStruct((N_GROUPS, TILE, 1), F32), gshape),
        scratch_shapes=[pltpu.VMEM((s, HEAD_DIM), BF16), pltpu.VMEM((s, HEAD_DIM), F32)],
        compiler_params=_cparams(("parallel",)),
    )(p, p, gs, w_s, b_s, gb, dmix)


SUBLANES = 8


def _shift_down(x, n):
    rolled = pltpu.roll(x, n, 0)
    edge = lax.broadcasted_iota(jnp.int32, (SUBLANES, x.shape[1]), 0)
    return jnp.concatenate([jnp.where(edge >= n, rolled[:SUBLANES], 0.0), rolled[SUBLANES:]], axis=0)


def _shift_up(x, n):
    s = x.shape[0]
    rolled = pltpu.roll(x, s - n, 0)
    edge = lax.broadcasted_iota(jnp.int32, (SUBLANES, x.shape[1]), 0)
    return jnp.concatenate([rolled[:s - SUBLANES], jnp.where(edge < SUBLANES - n, rolled[s - SUBLANES:], 0.0)], axis=0)


def _conv(x, w, b):
    x1, x2 = _shift_down(x, 1), _shift_down(x, 2)
    return b + w[0:1, :] * x2 + w[1:2, :] * x1 + w[2:3, :] * x, x1, x2


def _conv_specs(s, tn):
    xspec = pl.BlockSpec((2, s, tn), lambda j: (0, 0, j))
    wspec = pl.BlockSpec((2, CONV_WIDTH, tn), lambda j: (0, 0, j))
    bspec = pl.BlockSpec((2, 1, tn), lambda j: (0, 0, j))
    return xspec, wspec, bspec


def _up_conv_gate_fwd(h, wb, cw, cb, after=None):
    s, k = h.shape
    nb, _, bn = wb.shape
    hb = nb // 2
    f = hb * bn
    tm = _pick(s, (512, 256, 128))
    n_in = 5 + (after is not None)

    def body(*refs):
        h_ref, wg_ref, wv_ref, w_ref, b_ref = refs[:5]
        up_ref, act_ref, halo_s = refs[n_in:]
        first = pl.program_id(1) == 0
        outs = []
        for half, wt_ref in enumerate((wg_ref, wv_ref)):
            x = jnp.dot(h_ref[...], wt_ref[...], preferred_element_type=F32)
            up_ref[half] = x
            halo = jnp.where(first, 0.0, halo_s[half])
            halo_s[half] = x[tm - SUBLANES:]
            full = jnp.concatenate([halo, x], axis=0)
            x1 = pltpu.roll(full, 1, 0)[SUBLANES:]
            x2 = pltpu.roll(full, 2, 0)[SUBLANES:]
            w = w_ref[half]
            outs.append(b_ref[half] + w[0:1, :] * x2 + w[1:2, :] * x1 + w[2:3, :] * x)
        gate, val = outs
        act_ref[...] = (gate * jax.nn.sigmoid(gate) * val).astype(BF16)

    ins = [h, wb, wb, cw, cb]
    in_specs = [pl.BlockSpec((tm, k), lambda j, i: (i, 0)),
                pl.BlockSpec((None, k, bn), lambda j, i: (j, 0, 0)),
                pl.BlockSpec((None, k, bn), lambda j, i: (j + hb, 0, 0)),
                pl.BlockSpec((2, CONV_WIDTH, bn), lambda j, i: (0, 0, j)),
                pl.BlockSpec((2, 1, bn), lambda j, i: (0, 0, j))]
    if after is not None:
        ins.append(after)
        in_specs.append(pl.BlockSpec(after.shape, lambda j, i: (0, 0)))
    return pl.pallas_call(
        body, name="up_conv_gate_fwd", grid=(hb, s // tm), in_specs=in_specs,
        out_specs=(pl.BlockSpec((2, tm, bn), lambda j, i: (0, i, j)), pl.BlockSpec((tm, bn), lambda j, i: (i, j))),
        out_shape=(jax.ShapeDtypeStruct((2, s, f), F32), jax.ShapeDtypeStruct((s, f), BF16)),
        scratch_shapes=[pltpu.VMEM((2, SUBLANES, bn), F32)],
        compiler_params=_cparams(("parallel", "arbitrary"), 56),
    )(*ins)


def _down_dx_conv_gate_bwd(up, cw, cb, dy, wdown, after):
    _, s, f = up.shape
    d = dy.shape[1]
    tn = _pick(f, (256, 128))

    def body(x_ref, w_ref, b_ref, dy_ref, wd_ref, _, dx_ref, dw_ref, db_ref):
        da = lax.dot_general(dy_ref[...], wd_ref[...], NT_DIMS, preferred_element_type=F32)
        xg, xv = x_ref[0], x_ref[1]
        wg, wv = w_ref[0], w_ref[1]
        gate, xg1, xg2 = _conv(xg, wg, b_ref[0])
        val, xv1, xv2 = _conv(xv, wv, b_ref[1])
        sig = jax.nn.sigmoid(gate)
        dval = da * (gate * sig)
        dgate = da * val * (sig * (1.0 + gate * (1.0 - sig)))
        for half, (x, x1, x2, w, dz) in enumerate(((xg, xg1, xg2, wg, dgate), (xv, xv1, xv2, wv, dval))):
            dx_ref[half] = (w[2:3, :] * dz + w[1:2, :] * _shift_up(dz, 1) + w[0:1, :] * _shift_up(dz, 2)).astype(BF16)
            dw_ref[half, 0:1, :] = jnp.sum(dz * x2, axis=0, keepdims=True)
            dw_ref[half, 1:2, :] = jnp.sum(dz * x1, axis=0, keepdims=True)
            dw_ref[half, 2:3, :] = jnp.sum(dz * x, axis=0, keepdims=True)
            db_ref[half] = jnp.sum(dz, axis=0, keepdims=True)

    xspec, wspec, bspec = _conv_specs(s, tn)
    return pl.pallas_call(
        body, name="down_dx_conv_gate_bwd", grid=(f // tn,),
        in_specs=[xspec, wspec, bspec, pl.BlockSpec((s, d), lambda j: (0, 0)), pl.BlockSpec((tn, d), lambda j: (j, 0)),
                  pl.BlockSpec(after.shape, lambda j: (0, 0))],
        out_specs=(xspec, wspec, bspec),
        out_shape=(jax.ShapeDtypeStruct((2, s, f), BF16), jax.ShapeDtypeStruct((2, CONV_WIDTH, f), F32),
                   jax.ShapeDtypeStruct((2, 1, f), F32)),
        compiler_params=_cparams(("parallel",), 56),
    )(up, cw, cb, dy, wdown, after)


def _mesh_pos():
    return lax.axis_index("x"), lax.axis_index("y"), lax.axis_index("c")


def _remote(src, dst, send_sem, recv_sem, to):
    return pltpu.make_async_remote_copy(src_ref=src, dst_ref=dst, send_sem=send_sem, recv_sem=recv_sem,
                                        device_id=to, device_id_type=pl.DeviceIdType.MESH)


HBM_SPEC = pl.BlockSpec(memory_space=pltpu.HBM)
SEM_SPEC = pl.BlockSpec(memory_space=pltpu.SEMAPHORE)
ANY_SPEC = pl.BlockSpec(memory_space=pl.ANY)
TOKEN_SPEC = pl.BlockSpec(memory_space=pltpu.VMEM)
TOKEN_SHAPE = jax.ShapeDtypeStruct((8, 128), F32)
DATAFLOW = pltpu.SideEffectType.DATAFLOW_SIDE_EFFECTING
GATHER_PLANE = (2, 4, 6)


def _slot(pos):
    return 4 * pos[0] + 2 * pos[1] + pos[2]


def _flip(pos, k):
    return (pos[0] ^ ((k >> 2) & 1), pos[1] ^ ((k >> 1) & 1), pos[2] ^ (k & 1))


def _hbm(a):
    return pltpu.with_memory_space_constraint(a, pltpu.HBM)


def _hbm_shapes(arrays):
    return tuple(pltpu.HBM(a.shape, a.dtype) for a in arrays)


class _Split:
    def __init__(self, n, outs, n_sets):
        k = 2 * n * int(n_sets)
        self.n = n
        self.sems = list(outs[:k])
        self.bufs = list(outs[k:k + 2 * n])
        self.token = outs[-1]

    def sem_set(self, i):
        return self.sems[2 * self.n * i:2 * self.n * (i + 1)]


def _split_call(name, body, bufs, sems_in, n_sets, after):
    n = len(bufs) // 2
    k = 2 * n * int(n_sets)
    m = len(sems_in)
    afters = list(after) if isinstance(after, (list, tuple)) else [after]
    na = len(afters)

    def wrapped(*refs):
        srcs, dsts = refs[:n], refs[n:2 * n]
        s_in = refs[2 * n:2 * n + m]
        s_out = refs[2 * n + m + na:2 * n + m + na + k]
        token, local_sems = refs[-2], refs[-1]
        body(srcs, dsts, s_in, s_out, local_sems)
        token[...] = jnp.zeros_like(token)

    outs = pl.pallas_call(
        wrapped, name=name,
        out_shape=(pltpu.SemaphoreType.DMA(()),) * k + _hbm_shapes(bufs) + (TOKEN_SHAPE,),
        in_specs=[HBM_SPEC] * (2 * n) + [SEM_SPEC] * m + [ANY_SPEC] * na,
        out_specs=(SEM_SPEC,) * k + (HBM_SPEC,) * (2 * n) + (TOKEN_SPEC,),
        input_output_aliases={i: k + i for i in range(2 * n)},
        scratch_shapes=[pltpu.SemaphoreType.DMA((n,))],
        compiler_params=pltpu.CompilerParams(has_side_effects=DATAFLOW),
    )(*[_hbm(b) for b in bufs], *sems_in, *afters)
    return _Split(n, outs, n_sets)


def _wait_slots(land, count, send_sem, recv_sem, me, send=False, recv=False):
    span = land.at[pl.ds(0, count)]
    cp = _remote(span, span, send_sem, recv_sem, me)
    if send:
        cp.wait_send()
    if recv:
        cp.wait_recv()


X_FLIP, Y_FLIP, DIAG_FLIP = 4, 2, 6
BF16_SUBLANES = 16


def _gather_start(name, shards, after):
    n = len(shards)
    my_slot = _slot(_mesh_pos())
    lands = [lax.dynamic_update_slice(lax.empty((N_DEV,) + w.shape, w.dtype), w[None], (my_slot, 0, 0)) for w in shards]

    def body(srcs, dsts, _, sems, local_sems):
        me = _mesh_pos()
        for a in range(n):
            for k in (1, X_FLIP, Y_FLIP):
                _remote(srcs[a], dsts[a].at[_slot(me)], sems[a], sems[n + a], _flip(me, k)).start()

    return _split_call(name, body, list(shards) + lands, [], 1, after)


def _gather_relay(name, started, after):
    n = started.n

    def body(srcs, dsts, sems_a, sems_out, local_sems):
        me = _mesh_pos()
        sibling = _flip(me, 1)
        pass_on, relay = sems_out[:2 * n], sems_out[2 * n:]
        for a in range(n):
            _wait_slots(dsts[a], 3, sems_a[a], sems_a[n + a], me, recv=True)
            from_x = dsts[a].at[_slot(_flip(me, X_FLIP))]
            from_y = dsts[a].at[_slot(_flip(me, Y_FLIP))]
            rows = srcs[a].shape[0]
            if rows % (2 * BF16_SUBLANES) == 0:
                top, bottom = pl.ds(0, rows // 2), pl.ds(rows // 2, rows // 2)
                _remote(from_y.at[top], from_y.at[top], relay[a], relay[n + a], _flip(me, X_FLIP)).start()
                _remote(from_x.at[bottom], from_x.at[bottom], relay[a], relay[n + a], _flip(me, Y_FLIP)).start()
            else:
                _remote(from_y, from_y, relay[a], relay[n + a], _flip(me, X_FLIP)).start()
            for block in (from_x, from_y):
                _remote(block, block, pass_on[a], pass_on[n + a], sibling).start()
        for a in range(n):
            _wait_slots(dsts[a], 3, sems_a[a], sems_a[n + a], me, send=True)

    return _split_call(name, body, started.bufs, started.sems, 2, after)


def _gather_relay_diagonal(name, relayed, after):
    n = relayed.n

    def body(srcs, dsts, relay, pass_on, local_sems):
        me = _mesh_pos()
        for a in range(n):
            _wait_slots(dsts[a], 1, relay[a], relay[n + a], me, recv=True)
            block = dsts[a].at[_slot(_flip(me, DIAG_FLIP))]
            _remote(block, block, pass_on[a], pass_on[n + a], _flip(me, 1)).start()
        for a in range(n):
            _wait_slots(dsts[a], 1, relay[a], relay[n + a], me, send=True)

    return _split_call(name, body, relayed.bufs, relayed.sem_set(1), 1, after)


def _gather_finish(name, relayed, diagonal, after):
    n = relayed.n

    def body(srcs, dsts, sems, _, local_sems):
        me = _mesh_pos()
        first, second = sems[:2 * n], sems[2 * n:]
        for a in range(n):
            _wait_slots(dsts[a], 2, first[a], first[n + a], me, send=True, recv=True)
            _wait_slots(dsts[a], 1, second[a], second[n + a], me, send=True, recv=True)

    return _split_call(name, body, diagonal.bufs, relayed.sem_set(0) + diagonal.sems, 0, after).bufs[n:]


def _exchange_start(name, blocked, after):
    n = len(blocked)
    my_slot = _slot(_mesh_pos())
    rows = [w.shape[-2] // (N_DEV if w.ndim == 2 else 1) for w in blocked]

    def block(ref, a, slot):
        if len(ref.shape) == 3:
            return ref.at[slot]
        return ref.at[pl.ds(pl.multiple_of(slot * rows[a], 16), rows[a])]

    lands = []
    for w, r in zip(blocked, rows):
        mine = lax.dynamic_slice_in_dim(w, my_slot, 1, 0) if w.ndim == 3 else lax.dynamic_slice_in_dim(w, my_slot * r, r, 0)[None]
        lands.append(lax.dynamic_update_slice(lax.empty((N_DEV, r, w.shape[-1]), w.dtype), mine, (my_slot, 0, 0)))

    def body(srcs, dsts, _, sems, local_sems):
        me = _mesh_pos()
        for a in range(n):
            for k in range(1, N_DEV):
                peer = _flip(me, k)
                _remote(block(srcs[a], a, _slot(peer)), dsts[a].at[_slot(me)], sems[a], sems[n + a], peer).start()

    return _split_call(name, body, list(blocked) + lands, [], True, after)


def _exchange_finish(name, started, after):
    group = started if isinstance(started, (list, tuple)) else [started]
    srcs_all = [b for st in group for b in st.bufs[:st.n]]
    lands_all = [b for st in group for b in st.bufs[st.n:]]
    sends = [s for st in group for s in st.sems[:st.n]]
    recvs = [s for st in group for s in st.sems[st.n:]]
    n = len(srcs_all)

    def body(srcs, dsts, sems, _, local_sems):
        me = _mesh_pos()
        for a in range(n):
            _wait_slots(dsts[a], N_DEV - 1, sems[a], sems[n + a], me, send=True, recv=True)

    return _split_call(name, body, srcs_all + lands_all, sends + recvs, 0, after).bufs[n:]


def _broadcast_start(name, arrays, after):
    n = len(arrays)
    my_slot = _slot(_mesh_pos())
    lands = [lax.dynamic_update_slice(lax.empty((N_DEV,) + w.shape, w.dtype), w[None], (my_slot, 0, 0)) for w in arrays]

    def body(srcs, dsts, _, sems, local_sems):
        me = _mesh_pos()
        for a in range(n):
            for k in range(1, N_DEV):
                _remote(srcs[a], dsts[a].at[_slot(me)], sems[a], sems[n + a], _flip(me, k)).start()

    return _split_call(name, body, list(arrays) + lands, [], True, after)


def _adamw_math(w, g, m, v):
    m = ADAM_B1 * m + (1.0 - ADAM_B1) * g
    v = ADAM_B2 * v + (1.0 - ADAM_B2) * (g * g)
    m_hat = m / (1.0 - ADAM_B1 ** ADAM_STEP)
    v_hat = v / (1.0 - ADAM_B2 ** ADAM_STEP)
    delta = -ADAM_LR * (m_hat / (jnp.sqrt(v_hat) + ADAM_EPS) + ADAM_WD * w)
    return delta, m, v


def _adamw(name, w, m, v, parts, layer, prev=None):
    _, r, c = w.shape
    tr = max(t for t in range(16, r + 1, 16) if r % t == 0 and t * c <= ADAMW_TILE_ELEMS)
    n_prev = 0 if prev is None else 4

    def body(*refs):
        w_ref, m_ref, v_ref, p_ref = refs[:4]
        g_ref, d_ref, nm_ref, nv_ref = refs[4 + n_prev:]
        g = p_ref[0].astype(F32)
        for src in range(1, N_DEV):
            g = g + p_ref[src].astype(F32)
        delta, nm, nv = _adamw_math(w_ref[...], g, m_ref[...], v_ref[...])
        g_ref[...] = g
        d_ref[...] = delta
        nm_ref[...] = nm
        nv_ref[...] = nv

    wspec = pl.BlockSpec((None, tr, c), lambda i: (layer, i, 0))
    pspec = pl.BlockSpec((N_DEV, tr, c), lambda i: (0, i, 0))
    shp = jax.ShapeDtypeStruct(w.shape, F32)
    return pl.pallas_call(
        body, name=name, grid=(r // tr,), in_specs=[wspec] * 3 + [pspec] + [ANY_SPEC] * n_prev,
        out_specs=(wspec,) * 4, out_shape=(shp,) * 4, input_output_aliases={4 + j: j for j in range(n_prev)},
        compiler_params=_cparams(("parallel",), 56),
    )(w, m, v, parts, *([] if prev is None else prev))


PACK_TILE = 8 * 128


def _pack(arrays):
    flat = []
    for a in arrays:
        v = a.reshape(-1)
        pad = (-v.shape[0]) % PACK_TILE
        flat.append(jnp.pad(v, (0, pad)) if pad else v)
    return jnp.concatenate(flat).reshape(-1, 128)


def _unpack(buf, like):
    flat = buf.reshape(-1)
    out, off = [], 0
    for a in like:
        n = 1
        for dim in a.shape:
            n *= dim
        out.append(flat[off:off + n].reshape(a.shape))
        off += n + (-n) % PACK_TILE
    return out


def _sum_slots(name, gathered):
    _, r, c = gathered.shape

    def body(x_ref, o_ref):
        acc = x_ref[0].astype(F32)
        for src in range(1, N_DEV):
            acc = acc + x_ref[src].astype(F32)
        o_ref[...] = acc

    return pl.pallas_call(body, name=name, out_shape=jax.ShapeDtypeStruct((r, c), F32))(gathered)


def _adamw_small(ws, gs, ms, vs):
    n = len(ws)

    def body(*refs):
        ins, outs = refs[:4 * n], refs[4 * n:]
        for i in range(n):
            delta, nm, nv = _adamw_math(ins[i][...], ins[n + i][...], ins[2 * n + i][...], ins[3 * n + i][...])
            outs[i][...] = delta
            outs[n + i][...] = nm
            outs[2 * n + i][...] = nv

    shapes = tuple(jax.ShapeDtypeStruct(w.shape, F32) for w in ws) * 3
    out = pl.pallas_call(body, name="adamw_small", out_shape=shapes)(*ws, *gs, *ms, *vs)
    return out[:n], out[n:2 * n], out[2 * n:]


def kernel(x, attn_norm_g, w_in, q_norm_g, k_norm_g, sgu_norm_g, sgu_w, sgu_b, out_norm_a_g, out_norm_b_g, w_out, ffn_norm_g, w_up, conv_w, conv_b, w_down, loss_target, m_attn_norm_g, m_w_in, m_q_norm_g, m_k_norm_g, m_sgu_norm_g, m_sgu_w, m_sgu_b, m_out_norm_a_g, m_out_norm_b_g, m_w_out, m_ffn_norm_g, m_w_up, m_conv_w, m_conv_b, m_w_down, v_attn_norm_g, v_w_in, v_q_norm_g, v_k_norm_g, v_sgu_norm_g, v_sgu_w, v_sgu_b, v_out_norm_a_g, v_out_norm_b_g, v_w_out, v_ffn_norm_g, v_w_up, v_conv_w, v_conv_b, v_w_down):
    depth = w_in.shape[0]
    s, d = x.shape[1], x.shape[2]
    n_heads = (d // 2) // HEAD_DIM
    sgu_col0 = 3 * n_heads
    f2 = w_up.shape[2] * N_DEV
    ff = f2 // 2
    my_slot = 4 * lax.axis_index("x") + 2 * lax.axis_index("y") + lax.axis_index("c")

    wb = [(w_in[l].astype(BF16), w_out[l].astype(BF16), w_up[l].astype(BF16), w_down[l].astype(BF16))
          for l in range(depth)]
    groups = {"in0": [wb[0][0]], "out0": [wb[0][1], conv_w.reshape(depth * CONV_WIDTH, -1)], "up0": [wb[0][2]],
              "down0": [wb[0][3]]}
    for l in range(1, depth):
        groups[f"in{l}"] = [wb[l][0], wb[l][1]]
        groups[f"ffn{l}"] = [wb[l][2], wb[l][3]]
    order = list(groups)
    started, relayed = {}, {}

    def start(gname, after):
        started[gname] = _gather_start(f"gather_{gname}_start", groups[gname], after)
        return started[gname].token

    def relay(gname, after):
        relayed[gname] = _gather_relay(f"gather_{gname}_relay", started[gname], after)
        token = relayed[gname].token
        k = order.index(gname)
        nxt = [k + 2] if k + 2 < len(order) - 1 else []
        if k == len(order) - 2:
            nxt = [k + 1]
        for j in nxt:
            token = start(order[j], token)
        return token

    def finish(gname, after):
        diagonal = _gather_relay_diagonal(f"gather_{gname}_diagonal", relayed[gname], after)
        return _gather_finish(f"gather_{gname}_finish", relayed[gname], diagonal, diagonal.token)

    conv_b_all = conv_b.reshape(depth, 2, 1, ff)
    sgu_b_col = sgu_b[..., None]
    token = start(order[1], start(order[0], attn_norm_g))
    token = relay("out0", relay("in0", token))
    win_g = finish("in0", token)[0]

    xs = x[0]
    saved = []
    gathered = []
    for l in range(depth):
        g1 = attn_norm_g[l][None]
        g2 = ffn_norm_g[l][None]
        gq, gk = q_norm_g[l][None], k_norm_g[l][None]
        ga = out_norm_a_g[l][:, None, :]
        gs = sgu_norm_g[l][:, None, :]
        gb = out_norm_b_g[l][:, None, :]
        h1 = _rmsnorm_fwd("attn_norm_fwd", xs, g1)
        p = _mm_nn_blocked("in_proj", h1, win_g, F32)
        groups_of_heads = n_heads // ATTN_HEADS_PER_STEP
        if l == 0 or groups_of_heads < 2:
            att, o, rsum = _attn_fwd(p, gq, gk, ga, n_heads)
            token = relay("up0" if l == 0 else f"ffn{l}", att)
        else:
            half = groups_of_heads // 2
            part = _attn_fwd(p, gq, gk, ga, n_heads, 0, half)
            token = relay(f"ffn{l}", part[0])
            att, o, rsum = _attn_fwd(p, gq, gk, ga, n_heads, half, groups_of_heads - half, part, after=token)
        sg = _sgu_fwd(p, gs, sgu_w[l], sgu_b_col[l], gb, sgu_col0, token)
        mix = jnp.concatenate([att, sg], axis=-1)
        if l == 0:
            wout_g, cw = finish("out0", mix)
            cw = jnp.transpose(cw.reshape(N_DEV, depth, CONV_WIDTH, -1), (1, 2, 0, 3)).reshape(depth, CONV_WIDTH, 2, ff)
            conv_w_all = jnp.transpose(cw, (0, 2, 1, 3))
        x1, h2 = _mm_nn_res_norm("out_proj_ffn_norm", mix, wout_g.reshape(d, d), xs, g2)
        if l == 0:
            wup_g = finish("up0", h2)[0]
            token = relay("down0", wup_g)
            up, act = _up_conv_gate_fwd(h2, wup_g, conv_w_all[l], conv_b_all[l], after=token)
            wdown_g = finish("down0", up)[0]
        else:
            wup_g, wdown_g = finish(f"ffn{l}", h2)
            up, act = _up_conv_gate_fwd(h2, wup_g, conv_w_all[l], conv_b_all[l])
        saved.append((xs, h1, p, o, rsum, mix, x1, h2, up, act))
        gathered.append((win_g, wout_g, wup_g, wdown_g))
        if l + 1 < depth:
            token = relay(f"in{l + 1}", act)
            x2 = _mm_nn_res("down_proj", act, wdown_g.reshape(ff, d), x1, after=token)
            win_g, wout_g = finish(f"in{l + 1}", x2)
        else:
            x2 = _mm_nn_res("down_proj", act, wdown_g.reshape(ff, d), x1)
        xs = x2

    dx, dxb, loss_vec = _loss_head(xs, loss_target[0])
    loss = lax.psum(loss_vec[0, 0], MESH_AXES)

    exchanges = []
    small = [None] * depth
    small_names = ["attn_norm_g", "q_norm_g", "k_norm_g", "sgu_norm_g", "sgu_w", "sgu_b", "out_norm_a_g",
                   "out_norm_b_g", "ffn_norm_g", "conv_b", "conv_w"]
    for l in reversed(range(depth)):
        xs0, h1, p, o, rsum, mix, x1, h2, up, act = saved[l]
        win_g, wout_g, wup_g, wdown_g = gathered[l]
        wout_full = wout_g.reshape(d, d)
        wdown_full = wdown_g.reshape(ff, d)
        g1 = attn_norm_g[l][None]
        g2 = ffn_norm_g[l][None]
        gq, gk = q_norm_g[l][None], k_norm_g[l][None]
        ga = out_norm_a_g[l][:, None, :]
        gs = sgu_norm_g[l][:, None, :]
        gb = out_norm_b_g[l][:, None, :]
        d_wdown = _mm_tn_plain("down_proj_dw", act, dxb)
        exchanges.append((l, "down", ("w_down",), _exchange_start(f"grad_down{l}_start", [d_wdown], dx)))
        dup, d_cw, d_cb = _down_dx_conv_gate_bwd(up, conv_w_all[l], conv_b_all[l], dxb, wdown_full,
                                                 exchanges[-1][3].token)
        d_wup = _mm_tn_blocked("up_proj_dw", h2, dup, N_DEV, halves=True)
        exchanges.append((l, "up", ("w_up",), _exchange_start(f"grad_up{l}_start", [d_wup], d_cb)))
        dh2 = _mm_nt_blocked("up_proj_dx", dup, wup_g, halves=True, after=exchanges[-1][3].token)
        dx, dxb, d_g2 = _rmsnorm_bwd("ffn_norm_bwd", dh2, x1, g2, dx)
        d_wout = _mm_tn_plain("out_proj_dw", mix, dxb)
        exchanges.append((l, "out", ("w_out",), _exchange_start(f"grad_out{l}_start", [d_wout], d_g2)))
        dmix = _mm_nt_plain("out_proj_dx", dxb, wout_full, after=exchanges[-1][3].token)
        dq, dk, dv, d_gq, d_gk, d_ga = _attn_bwd(p, gq, gk, ga, o, rsum, dmix, n_heads)
        du, dvs, d_gs, d_sw, d_sb, d_gb = _sgu_bwd(p, gs, sgu_w[l], sgu_b_col[l], gb, dmix, sgu_col0, n_heads)
        dp = jnp.concatenate([dq, dk, dv, du, dvs], axis=-1)
        d_win = _mm_tn_blocked("in_proj_dw", h1, dp, N_DEV)
        exchanges.append((l, "in", ("w_in",), _exchange_start(f"grad_in{l}_start", [d_win], d_gq)))
        dh1 = _mm_nt_blocked("in_proj_dx", dp, win_g, after=exchanges[-1][3].token)
        dx, dxb, d_g1 = _rmsnorm_bwd("attn_norm_bwd", dh1, xs0, g1, dx)
        small[l] = dict(attn_norm_g=d_g1[0], q_norm_g=d_gq[0], k_norm_g=d_gk[0], sgu_norm_g=d_gs[:, 0], sgu_w=d_sw,
                        sgu_b=d_sb[..., 0], out_norm_a_g=d_ga[:, 0], out_norm_b_g=d_gb[:, 0], ffn_norm_g=d_g2[0],
                        conv_w=jnp.transpose(d_cw, (1, 0, 2)).reshape(CONV_WIDTH, f2), conv_b=d_cb.reshape(f2))
    grad_x = dx[None]

    f32_names = [n for n in small_names if n != "sgu_w"]
    small_g = [jnp.stack([small[l][n] for l in range(depth)]) for n in f32_names]
    sgu_w_g = jnp.stack([small[l]["sgu_w"] for l in range(depth)])
    small_sent = _broadcast_start("grad_small_start", [_pack(small_g), sgu_w_g.reshape(-1, TILE).astype(BF16)], dx)

    res = {}
    big = dict(w_in=(w_in, m_w_in, v_w_in), w_out=(w_out, m_w_out, v_w_out), w_up=(w_up, m_w_up, v_w_up),
               w_down=(w_down, m_w_down, v_w_down))
    after = [small_sent.token]
    batches = [[e for e in exchanges if e[0] == l] for l in reversed(range(depth))]
    batches = batches[:-1] + [batches[-1][:-1], batches[-1][-1:]]
    for i, batch in enumerate(batches):
        landed = _exchange_finish(f"grad_batch{i}_finish", [ex for _, _, _, ex in batch], after)
        after = []
        for (layer, name), parts in zip([(k, n) for k, _, names, _ in batch for n in names], landed):
            w, m, v = big[name]
            res[name] = _adamw(f"adamw_{name}", w, m, v, parts, layer, res.get(name))
            after.append(res[name][0])
    small_all, sgu_w_all = _exchange_finish("grad_small_finish", small_sent, after)
    small_sum = _unpack(_sum_slots("small_grad_sum", small_all), small_g)
    g_small = dict(zip(f32_names, small_sum))
    g_small["sgu_w"] = _sum_slots("sgu_w_grad_sum", sgu_w_all).reshape(sgu_w.shape)
    cwn = conv_w.shape[2]
    g_small["conv_w"] = lax.dynamic_slice_in_dim(g_small["conv_w"], my_slot * cwn, cwn, axis=2)
    small_w = dict(attn_norm_g=(attn_norm_g, m_attn_norm_g, v_attn_norm_g), q_norm_g=(q_norm_g, m_q_norm_g, v_q_norm_g),
                   k_norm_g=(k_norm_g, m_k_norm_g, v_k_norm_g), sgu_norm_g=(sgu_norm_g, m_sgu_norm_g, v_sgu_norm_g),
                   sgu_w=(sgu_w, m_sgu_w, v_sgu_w), sgu_b=(sgu_b, m_sgu_b, v_sgu_b),
                   out_norm_a_g=(out_norm_a_g, m_out_norm_a_g, v_out_norm_a_g),
                   out_norm_b_g=(out_norm_b_g, m_out_norm_b_g, v_out_norm_b_g),
                   ffn_norm_g=(ffn_norm_g, m_ffn_norm_g, v_ffn_norm_g), conv_b=(conv_b, m_conv_b, v_conv_b),
                   conv_w=(conv_w, m_conv_w, v_conv_w))
    grads = [g_small[n].reshape(small_w[n][0].shape) for n in small_names]
    deltas, new_ms, new_vs = _adamw_small([small_w[n][0] for n in small_names], grads,
                                          [small_w[n][1] for n in small_names], [small_w[n][2] for n in small_names])
    for n, g, dlt, nm, nv in zip(small_names, grads, deltas, new_ms, new_vs):
        res[n] = (g, dlt, nm, nv)

    order = ["attn_norm_g", "w_in", "q_norm_g", "k_norm_g", "sgu_norm_g", "sgu_w", "sgu_b", "out_norm_a_g",
             "out_norm_b_g", "w_out", "ffn_norm_g", "w_up", "conv_w", "conv_b", "w_down"]
    outs = [loss, grad_x]
    for field in range(4):
        outs += [res[n][field] for n in order]
    return tuple(outs)
```

```python
import functools

import jax
import jax.numpy as jnp
from jax import lax
from jax.experimental import pallas as pl
from jax.experimental.pallas import tpu as pltpu

F32 = jnp.float32
BF16 = jnp.bfloat16
EPS = 1e-6
HEAD_DIM = 128
TILE = 128
ATTN_VMEM_MB = 58
ATTN_HEADS_PER_STEP = 4
N_GROUPS = 8
CONV_WIDTH = 3
N_DEV = 8
MESH_AXES = ("x", "y", "c")
MIB = 1024 * 1024

ADAM_LR = 0.001
ADAM_B1 = 0.9
ADAM_B2 = 0.999
ADAM_EPS = 1e-08
ADAM_WD = 0.01
ADAM_STEP = 10
ADAMW_TILE_ELEMS = 512 * 1024

NT_DIMS = (((1,), (1,)), ((), ()))
NN_DIMS = (((1,), (0,)), ((), ()))
TN_DIMS = (((0,), (0,)), ((), ()))


def _cparams(sem, vmem_mb=48):
    return pltpu.CompilerParams(dimension_semantics=sem, vmem_limit_bytes=vmem_mb * MIB)


def _pick(n, cands):
    for c in cands:
        if n % c == 0:
            return c
    return n


MXU_WIDTH = 256


def _pairs(nb, bn):
    return nb % 2 == 0 and bn % MXU_WIDTH != 0 and (2 * bn) % MXU_WIDTH == 0


def _mm(name, grid, ins, in_specs, out_shape, out_spec, dims, has_res=False, parts=None, vmem_mb=56, after=None,
        split_out=None):
    n_in = 2 + has_res + (after is not None)
    if after is not None:
        ins = tuple(ins) + (after,)
        in_specs = list(in_specs) + [pl.BlockSpec(after.shape, lambda *_: (0, 0))]

    def body(*refs):
        a_ref, b_ref = refs[:2]
        o_ref = refs[n_in]
        if parts is None:
            acc = lax.dot_general(a_ref[...], b_ref[...], dims, preferred_element_type=F32)
        else:
            acc = None
            for part in parts:
                a, b = part(a_ref, b_ref)
                prod = lax.dot_general(a, b, dims, preferred_element_type=F32)
                acc = prod if acc is None else acc + prod
        if has_res:
            acc = acc + refs[2][...]
        if split_out is None:
            o_ref[...] = acc.astype(o_ref.dtype)
        else:
            o_ref[0] = acc[:, :split_out].astype(o_ref.dtype)
            o_ref[1] = acc[:, split_out:].astype(o_ref.dtype)

    return pl.pallas_call(
        body, name=name, grid=grid, in_specs=in_specs, out_specs=out_spec, out_shape=out_shape,
        compiler_params=_cparams(("parallel",) * len(grid), vmem_mb),
    )(*ins)


def _two_blocks(b_ref, first):
    return jnp.concatenate([b_ref[first], b_ref[first + 1]], axis=1)


def _rows_for(m, row_bytes, budget):
    return _pick(m, tuple(t for t in (2048, 1024, 512, 256, 128) if t * row_bytes <= budget))


def _mm_nn_blocked(name, a, wb, out_dtype, after=None):
    m, k = a.shape
    nb, _, bn = wb.shape
    out_shape = jax.ShapeDtypeStruct((m, nb * bn), out_dtype)
    if _pairs(nb, bn):
        tm = _rows_for(m, 2 * bn * jnp.dtype(out_dtype).itemsize, 6 * MIB)
        return _mm(name, (nb // 2, m // tm), (a, wb),
                   [pl.BlockSpec((tm, k), lambda j, i: (i, 0)), pl.BlockSpec((2, k, bn), lambda j, i: (j, 0, 0))],
                   out_shape, pl.BlockSpec((tm, 2 * bn), lambda j, i: (i, j)), NN_DIMS,
                   parts=[lambda a_ref, b_ref: (a_ref[...], _two_blocks(b_ref, 0))], after=after)
    tm = _rows_for(m, bn * jnp.dtype(out_dtype).itemsize, 6 * MIB)
    a_spec = pl.BlockSpec((tm, k), lambda j, i: (i, 0))
    b_spec = pl.BlockSpec((None, k, bn), lambda j, i: (j, 0, 0))
    o_spec = pl.BlockSpec((tm, bn), lambda j, i: (i, j))
    return _mm(name, (nb, m // tm), (a, wb), [a_spec, b_spec], out_shape, o_spec, NN_DIMS, after=after)


def _mm_nn_res_norm(name, a, w, res, g):
    m, k = a.shape
    n = w.shape[1]
    tm = _pick(m, (512, 256, 128))

    def body(a_ref, w_ref, r_ref, g_ref, x_ref, h_ref):
        x = jnp.dot(a_ref[...], w_ref[...], preferred_element_type=F32) + r_ref[...]
        x_ref[...] = x
        h_ref[...] = (x * _rstd(x) * g_ref[...]).astype(BF16)

    row = pl.BlockSpec((tm, n), lambda i: (i, 0))
    return pl.pallas_call(
        body, name=name, grid=(m // tm,),
        in_specs=[pl.BlockSpec((tm, k), lambda i: (i, 0)), pl.BlockSpec((k, n), lambda i: (0, 0)), row,
                  pl.BlockSpec((1, n), lambda i: (0, 0))],
        out_specs=(row, row), out_shape=(jax.ShapeDtypeStruct((m, n), F32), jax.ShapeDtypeStruct((m, n), BF16)),
        compiler_params=_cparams(("parallel",), 56),
    )(a, w, res, g)


def _mm_nn_res(name, a, w, res, after=None):
    m, k = a.shape
    n = w.shape[1]
    tm = _pick(m, (512, 256, 128))
    tn = _rows_for(n, k * 2, 12 * MIB)
    a_spec = pl.BlockSpec((tm, k), lambda j, i: (i, 0))
    b_spec = pl.BlockSpec((k, tn), lambda j, i: (0, j))
    r_spec = pl.BlockSpec((tm, tn), lambda j, i: (i, j))
    o_spec = pl.BlockSpec((tm, tn), lambda j, i: (i, j))
    return _mm(name, (n // tn, m // tm), (a, w, res), [a_spec, b_spec, r_spec], jax.ShapeDtypeStruct((m, n), F32),
               o_spec, NN_DIMS, has_res=True, after=after)


def _mm_nt_blocked(name, dy, wb, halves=False, after=None):
    nb, n, bn = wb.shape
    m = dy.shape[-2]
    tm = _pick(m, (512, 256, 128))
    tn = _rows_for(n, nb * bn * 2, 12 * MIB)
    if halves:
        hb = nb // 2
        a_spec = pl.BlockSpec((2, tm, hb * bn), lambda j, i: (0, i, 0))
        a_part = lambda kk: (lambda a_ref: a_ref[kk // hb, :, (kk % hb) * bn:(kk % hb + 1) * bn])
    else:
        a_spec = pl.BlockSpec((tm, nb * bn), lambda j, i: (i, 0))
        a_part = lambda kk: (lambda a_ref: a_ref[:, kk * bn:(kk + 1) * bn])
    if _pairs(nb, bn) and not halves:
        parts = [(lambda a_ref, b_ref, kk=kk: (a_ref[:, kk * bn:(kk + 2) * bn], _two_blocks(b_ref, kk)))
                 for kk in range(0, nb, 2)]
    elif _pairs(nb // 2, bn) and halves:
        parts = [(lambda a_ref, b_ref, kk=kk: (a_ref[kk // hb, :, (kk % hb) * bn:(kk % hb + 2) * bn],
                                               _two_blocks(b_ref, kk))) for kk in range(0, nb, 2)]
    else:
        parts = [(lambda a_ref, b_ref, kk=kk, sel=a_part(kk): (sel(a_ref), b_ref[kk])) for kk in range(nb)]
    b_spec = pl.BlockSpec((nb, tn, bn), lambda j, i: (0, j, 0))
    o_spec = pl.BlockSpec((tm, tn), lambda j, i: (i, j))
    return _mm(name, (n // tn, m // tm), (dy, wb), [a_spec, b_spec], jax.ShapeDtypeStruct((m, n), F32), o_spec,
               NT_DIMS, parts=parts, after=after)


def _mm_nt_plain(name, dy, w, out_dtype=F32, after=None):
    m, k = dy.shape
    n = w.shape[0]
    tm = _rows_for(m, k * 2, 8 * MIB)
    tn = _pick(n, (512, 256, 128))
    a_spec = pl.BlockSpec((tm, k), lambda j, i: (i, 0))
    b_spec = pl.BlockSpec((tn, k), lambda j, i: (j, 0))
    o_spec = pl.BlockSpec((tm, tn), lambda j, i: (i, j))
    return _mm(name, (n // tn, m // tm), (dy, w), [a_spec, b_spec], jax.ShapeDtypeStruct((m, n), out_dtype), o_spec,
               NT_DIMS, after=after)


def _mm_tn_blocked(name, a, dy, nb, halves=False):
    s, k1 = a.shape
    bn = (dy.shape[-1] * (2 if halves else 1)) // nb
    if _pairs(nb, bn) and not halves:
        tm = _rows_for(k1, 2 * bn * 2, 6 * MIB)
        return _mm(name, (nb // 2, k1 // tm), (a, dy),
                   [pl.BlockSpec((s, tm), lambda j, i: (0, i)), pl.BlockSpec((s, 2 * bn), lambda j, i: (0, j))],
                   jax.ShapeDtypeStruct((nb, k1, bn), BF16), pl.BlockSpec((2, tm, bn), lambda j, i: (j, i, 0)), TN_DIMS,
                   split_out=bn)
    if halves and _pairs(nb // 2, bn):
        tm = _rows_for(k1, 2 * bn * 2, 6 * MIB)
        per_half = nb // 4
        return _mm(name, (nb // 2, k1 // tm), (a, dy),
                   [pl.BlockSpec((s, tm), lambda j, i: (0, i)),
                    pl.BlockSpec((None, s, 2 * bn), lambda j, i: (j // per_half, 0, j % per_half))],
                   jax.ShapeDtypeStruct((nb, k1, bn), BF16), pl.BlockSpec((2, tm, bn), lambda j, i: (j, i, 0)), TN_DIMS,
                   split_out=bn)
    tm = _rows_for(k1, bn * 2, 6 * MIB)
    a_spec = pl.BlockSpec((s, tm), lambda j, i: (0, i))
    if halves:
        hb = nb // 2
        b_spec = pl.BlockSpec((None, s, bn), lambda j, i: (j // hb, 0, j % hb))
    else:
        b_spec = pl.BlockSpec((s, bn), lambda j, i: (0, j))
    o_spec = pl.BlockSpec((None, tm, bn), lambda j, i: (j, i, 0))
    return _mm(name, (nb, k1 // tm), (a, dy), [a_spec, b_spec], jax.ShapeDtypeStruct((nb, k1, bn), BF16), o_spec,
               TN_DIMS)


def _mm_tn_plain(name, a, dy):
    s, k1 = a.shape
    n = dy.shape[1]
    tm = _pick(k1, (512, 256, 128))
    tn = _rows_for(n, s * 2, 8 * MIB)
    a_spec = pl.BlockSpec((s, tm), lambda i, j: (0, i))
    b_spec = pl.BlockSpec((s, tn), lambda i, j: (0, j))
    o_spec = pl.BlockSpec((tm, tn), lambda i, j: (i, j))
    return _mm(name, (k1 // tm, n // tn), (a, dy), [a_spec, b_spec], jax.ShapeDtypeStruct((k1, n), BF16), o_spec,
               TN_DIMS)


def _rstd(x):
    return lax.rsqrt(jnp.mean(x * x, axis=-1, keepdims=True) + EPS)


def _norm_bwd(dy, xhat, r, g):
    dxhat = dy * g
    return r * (dxhat - xhat * jnp.mean(dxhat * xhat, axis=-1, keepdims=True))


def _rmsnorm_fwd(name, x, g):
    s, d = x.shape
    tr = _pick(s, (256, 128))

    def body(x_ref, g_ref, h_ref):
        xv = x_ref[...]
        h_ref[...] = (xv * _rstd(xv) * g_ref[...]).astype(BF16)

    return pl.pallas_call(
        body, name=name, grid=(s // tr,),
        in_specs=[pl.BlockSpec((tr, d), lambda i: (i, 0)), pl.BlockSpec((1, d), lambda i: (0, 0))],
        out_specs=pl.BlockSpec((tr, d), lambda i: (i, 0)),
        out_shape=jax.ShapeDtypeStruct((s, d), BF16), compiler_params=_cparams(("parallel",)),
    )(x, g)


def _rmsnorm_bwd(name, dh, x, g, dres):
    s, d = x.shape
    tr = _pick(s, (256, 128))

    def body(dh_ref, x_ref, g_ref, dres_ref, dx_ref, dxb_ref, dg_ref):
        xv = x_ref[...]
        r = _rstd(xv)
        xhat = xv * r
        dhv = dh_ref[...]
        dx = dres_ref[...] + _norm_bwd(dhv, xhat, r, g_ref[...])
        dx_ref[...] = dx
        dxb_ref[...] = dx.astype(BF16)
        part = jnp.sum(dhv * xhat, axis=0, keepdims=True)

        @pl.when(pl.program_id(0) == 0)
        def _():
            dg_ref[...] = part

        @pl.when(pl.program_id(0) > 0)
        def _():
            dg_ref[...] += part

    row = pl.BlockSpec((tr, d), lambda i: (i, 0))
    vec = pl.BlockSpec((1, d), lambda i: (0, 0))
    return pl.pallas_call(
        body, name=name, grid=(s // tr,), in_specs=[row, row, vec, row], out_specs=(row, row, vec),
        out_shape=(jax.ShapeDtypeStruct((s, d), F32), jax.ShapeDtypeStruct((s, d), BF16),
                   jax.ShapeDtypeStruct((1, d), F32)),
        compiler_params=_cparams(("arbitrary",)),
    )(dh, x, g, dres)


def _loss_head(y, target):
    s, d = y.shape
    tr = _pick(s, (256, 128))

    def body(y_ref, t_ref, dy_ref, dyb_ref, loss_ref):
        err = y_ref[...] - t_ref[...]
        dy = err * (1.0 / d)
        dy_ref[...] = dy
        dyb_ref[...] = dy.astype(BF16)
        part = 0.5 * jnp.sum(jnp.mean(err * err, axis=-1, keepdims=True), axis=0, keepdims=True)
        part = jnp.broadcast_to(part, (1, 128))

        @pl.when(pl.program_id(0) == 0)
        def _():
            loss_ref[...] = part

        @pl.when(pl.program_id(0) > 0)
        def _():
            loss_ref[...] += part

    row = pl.BlockSpec((tr, d), lambda i: (i, 0))
    return pl.pallas_call(
        body, name="loss_head", grid=(s // tr,), in_specs=[row, row],
        out_specs=(row, row, pl.BlockSpec((1, 128), lambda i: (0, 0))),
        out_shape=(jax.ShapeDtypeStruct((s, d), F32), jax.ShapeDtypeStruct((s, d), BF16),
                   jax.ShapeDtypeStruct((1, 128), F32)),
        compiler_params=_cparams(("arbitrary",)),
    )(y, target)


def _split_dot(x, tri):
    hi = x.astype(BF16)
    lo = (x - hi.astype(F32)).astype(BF16)
    return (jnp.dot(hi, tri, preferred_element_type=F32) + jnp.dot(lo, tri, preferred_element_type=F32))


def _tile_iotas():
    row = lax.broadcasted_iota(jnp.int32, (TILE, TILE), 0)
    col = lax.broadcasted_iota(jnp.int32, (TILE, TILE), 1)
    return row, col


def _sb_logits(qi, kb, mask):
    z = lax.dot_general(qi, kb, NT_DIMS, preferred_element_type=F32) * (HEAD_DIM ** -0.5)
    sp = jnp.log(1.0 + jnp.exp(-jnp.abs(z)))
    lb = jnp.minimum(z, 0.0) - sp
    l1m = -jnp.maximum(z, 0.0) - sp
    if mask is not None:
        l1m = jnp.where(mask, l1m, 0.0)
    return lb, l1m


def _attn_fwd(p, gq, gk, ga, n_heads, first=0, count=None, prev=None, after=None):
    s = p.shape[0]
    nq = s // TILE

    hp = ATTN_HEADS_PER_STEP
    wd = hp * HEAD_DIM

    def body(q_ref, k_ref, v_ref, gq_ref, gk_ref, ga_ref, *rest):
        att_ref, o_ref, r_ref, qn_s, kn_s, vb_s = rest[-6:]
        heads = [slice(hh * HEAD_DIM, (hh + 1) * HEAD_DIM) for hh in range(hp)]
        for hd in heads:
            qv = q_ref[:, hd]
            qn_s[:, hd] = (qv * _rstd(qv) * gq_ref[...]).astype(BF16)
            kv = k_ref[:, hd]
            kn_s[:, hd] = (kv * _rstd(kv) * gk_ref[...]).astype(BF16)
        vb_s[...] = v_ref[...].astype(BF16)
        row, col = _tile_iotas()
        causal = col < row
        upper_ones = jnp.concatenate([(row > col).astype(BF16), jnp.ones((TILE, TILE), BF16)], axis=1)

        def tiles(rows, key_blocks, states):
            chains = [(hi, hd, keys, mask) for hi, hd in enumerate(heads) for keys, mask in key_blocks]
            logits = [_sb_logits(qn_s[rows, hd], kn_s[keys, hd], mask) for _, hd, keys, mask in chains]
            sums = [_split_dot(l1m, upper_ones) for _, l1m in logits]
            carry = [c for _, c in states]
            probs = []
            for (hi, _, _, mask), (lb, _), sm in zip(chains, logits, sums):
                a = jnp.exp(lb + sm[:, :TILE] + carry[hi])
                carry[hi] = carry[hi] + sm[:, TILE:]
                probs.append((a if mask is None else jnp.where(mask, a, 0.0)).astype(BF16))
            outs = [jnp.dot(a, vb_s[keys, hd], preferred_element_type=F32) for a, (_, hd, keys, _) in zip(probs, chains)]
            acc = [o_acc for o_acc, _ in states]
            for (hi, _, _, _), o in zip(chains, outs):
                acc[hi] = acc[hi] + o
            return tuple(zip(acc, carry))

        def key_block(b):
            return pl.ds(pl.multiple_of(b * TILE, TILE), TILE), None

        def qblock(i, _):
            rows = pl.ds(pl.multiple_of(i * TILE, TILE), TILE)
            zero = jnp.zeros((TILE, HEAD_DIM), F32)
            states = tuple((zero, zero) for _ in heads)
            states = lax.cond(i % 2 == 1, lambda st: tiles(rows, [(rows, causal), key_block(i - 1)], st),
                              lambda st: tiles(rows, [(rows, causal)], st), states)
            top = i - i % 2

            def kblocks(jj, states):
                return tiles(rows, [key_block(top - 1 - 2 * jj), key_block(top - 2 - 2 * jj)], states)

            states = lax.fori_loop(0, i // 2, kblocks, states)
            for hh, (hd, (o_acc, c)) in enumerate(zip(heads, states)):
                o_ref[rows, hd] = o_acc
                r_ref[rows, hd] = c
                att_ref[rows, hd] = (o_acc * _rstd(o_acc) * ga_ref[hh]).astype(BF16)
            return 0

        lax.fori_loop(0, nq, qblock, 0)

    steps = n_heads // hp
    count = steps if count is None else count
    col_blk = lambda off: pl.BlockSpec((s, wd), lambda h: (0, off + first + h))
    vec = pl.BlockSpec((1, HEAD_DIM), lambda h: (0, 0))
    hvec = pl.BlockSpec((hp, 1, HEAD_DIM), lambda h: (first + h, 0, 0))
    out = pl.BlockSpec((s, wd), lambda h: (0, first + h))
    w = n_heads * HEAD_DIM
    n_prev = 0 if prev is None else 3
    extra = [] if after is None else [after]
    return pl.pallas_call(
        body, name="attn_fwd", grid=(count,),
        in_specs=[col_blk(0), col_blk(steps), col_blk(2 * steps), vec, vec, hvec] + [ANY_SPEC] * n_prev
        + [pl.BlockSpec(a.shape, lambda h: (0, 0)) for a in extra],
        out_specs=(out, out, out),
        out_shape=(jax.ShapeDtypeStruct((s, w), BF16), jax.ShapeDtypeStruct((s, w), F32),
                   jax.ShapeDtypeStruct((s, w), F32)),
        input_output_aliases={6 + j: j for j in range(n_prev)},
        scratch_shapes=[pltpu.VMEM((s, wd), BF16)] * 3,
        compiler_params=_cparams(("parallel",), ATTN_VMEM_MB),
    )(p, p, p, gq, gk, ga, *([] if prev is None else prev), *extra)


def _attn_bwd(p, gq, gk, ga, o, rsum, dmix, n_heads):
    s = p.shape[0]
    nq = s // TILE

    hp = ATTN_HEADS_PER_STEP
    wd = hp * HEAD_DIM
    scale = HEAD_DIM ** -0.5

    def body(q_ref, k_ref, v_ref, gq_ref, gk_ref, ga_ref, o_ref, r_ref, dm_ref,
             dq_ref, dk_ref, dv_ref, dgq_ref, dgk_ref, dga_ref,
             qn_s, kn_s, vb_s, do_s, dqn_s, dkn_s, dv_s):
        step = pl.program_id(0)
        gqv, gkv = gq_ref[...], gk_ref[...]
        heads = [slice(hh * HEAD_DIM, (hh + 1) * HEAD_DIM) for hh in range(hp)]
        for hh, hd in enumerate(heads):
            qv = q_ref[:, hd]
            qn_s[:, hd] = (qv * _rstd(qv) * gqv).astype(BF16)
            kv = k_ref[:, hd]
            kn_s[:, hd] = (kv * _rstd(kv) * gkv).astype(BF16)
            ov = o_ref[:, hd]
            ro = _rstd(ov)
            ohat = ov * ro
            dm = dm_ref[:, hd]
            dga_ref[hh] = jnp.sum(dm * ohat, axis=0, keepdims=True)
            do_s[:, hd] = _norm_bwd(dm, ohat, ro, ga_ref[hh]).astype(BF16)
        vb_s[...] = v_ref[...].astype(BF16)
        dkn_s[...] = jnp.zeros_like(dkn_s)
        dv_s[...] = jnp.zeros_like(dv_s)
        row, col = _tile_iotas()
        causal = col < row
        ones = jnp.ones((TILE, TILE), BF16)
        incl_ones = jnp.concatenate([(row <= col).astype(BF16), ones], axis=1)
        excl_ones = jnp.concatenate([(row < col).astype(BF16), ones], axis=1)

        def tiles(rows, key_blocks, states):
            chains = [(hi, hd, keys, mask) for hi, hd in enumerate(heads) for keys, mask in key_blocks]
            qis = [qn_s[rows, hd] for hd in heads]
            dois = [do_s[rows, hd] for hd in heads]
            logits = [_sb_logits(qis[hi], kn_s[keys, hd], mask) for hi, hd, keys, mask in chains]
            sums = [_split_dot(l1m, incl_ones) for _, l1m in logits]
            das = [lax.dot_general(dois[hi], vb_s[keys, hd], NT_DIMS, preferred_element_type=F32)
                   for hi, hd, keys, _ in chains]
            pfx = [st[1] for st in states]
            probs, dss = [], []
            for (hi, hd, _, mask), (lb, _), sm, da in zip(chains, logits, sums, das):
                a = jnp.exp(lb + (r_ref[rows, hd] - pfx[hi] - sm[:, :TILE]))
                pfx[hi] = pfx[hi] + sm[:, TILE:]
                a = a if mask is None else jnp.where(mask, a, 0.0)
                probs.append(a.astype(BF16))
                dss.append(da * a)
            dsums = [_split_dot(ds, excl_ones) for ds in dss]
            pc = [st[2] for st in states]
            dzs = []
            for (hi, _, _, mask), (lb, _), ds, dsm in zip(chains, logits, dss, dsums):
                dl1m = pc[hi] + dsm[:, :TILE]
                pc[hi] = pc[hi] + dsm[:, TILE:]
                dl1m = dl1m if mask is None else jnp.where(mask, dl1m, 0.0)
                beta = jnp.exp(lb)
                dzs.append(((ds * (1.0 - beta) - dl1m * beta) * scale).astype(BF16))
            dqs = [jnp.dot(dz, kn_s[keys, hd], preferred_element_type=F32) for dz, (_, hd, keys, _) in zip(dzs, chains)]
            for dz, a, (hi, hd, keys, _) in zip(dzs, probs, chains):
                dkn_s[keys, hd] += lax.dot_general(dz, qis[hi], TN_DIMS, preferred_element_type=F32)
                dv_s[keys, hd] += lax.dot_general(a, dois[hi], TN_DIMS, preferred_element_type=F32)
            dq_acc = [st[0] for st in states]
            for (hi, _, _, _), dq in zip(chains, dqs):
                dq_acc[hi] = dq_acc[hi] + dq
            return tuple(zip(dq_acc, pfx, pc))

        def key_block(b):
            return pl.ds(pl.multiple_of(b * TILE, TILE), TILE), None

        def qblock(i, _):
            rows = pl.ds(pl.multiple_of(i * TILE, TILE), TILE)
            zero = jnp.zeros((TILE, HEAD_DIM), F32)

            def kblocks(jj, states):
                return tiles(rows, [key_block(2 * jj), key_block(2 * jj + 1)], states)

            states = lax.fori_loop(0, i // 2, kblocks, tuple((zero, zero, zero) for _ in heads))
            states = lax.cond(i % 2 == 1, lambda st: tiles(rows, [key_block(i - 1), (rows, causal)], st),
                              lambda st: tiles(rows, [(rows, causal)], st), states)
            for hd, (dq_acc, _, _) in zip(heads, states):
                dqn_s[rows, hd] = dq_acc
            return 0

        lax.fori_loop(0, nq, qblock, 0)

        def norm_in_bwd(x_ref, g, dn_s, dx_ref, dg_ref):
            part = jnp.zeros((1, HEAD_DIM), F32)
            for hd in heads:
                xv = x_ref[:, hd]
                r = _rstd(xv)
                xhat = xv * r
                dn = dn_s[:, hd]
                dx_ref[:, hd] = _norm_bwd(dn, xhat, r, g).astype(BF16)
                part = part + jnp.sum(dn * xhat, axis=0, keepdims=True)

            @pl.when(step == 0)
            def _():
                dg_ref[...] = part

            @pl.when(step > 0)
            def _():
                dg_ref[...] += part

        norm_in_bwd(q_ref, gqv, dqn_s, dq_ref, dgq_ref)
        norm_in_bwd(k_ref, gkv, dkn_s, dk_ref, dgk_ref)
        dv_ref[...] = dv_s[...].astype(BF16)

    once = pl.Buffered(1)
    steps = n_heads // hp
    col_blk = lambda off: pl.BlockSpec((s, wd), lambda h: (0, off + h), pipeline_mode=once)
    vec = pl.BlockSpec((1, HEAD_DIM), lambda h: (0, 0))
    hvec = pl.BlockSpec((hp, 1, HEAD_DIM), lambda h: (h, 0, 0))
    blk = pl.BlockSpec((s, wd), lambda h: (0, h), pipeline_mode=once)
    w = n_heads * HEAD_DIM
    big = jax.ShapeDtypeStruct((s, w), BF16)
    return pl.pallas_call(
        body, name="attn_bwd", grid=(steps,),
        in_specs=[col_blk(0), col_blk(steps), col_blk(2 * steps), vec, vec, hvec, blk, blk, blk],
        out_specs=(blk, blk, blk, vec, vec, hvec),
        out_shape=(big, big, big, jax.ShapeDtypeStruct((1, HEAD_DIM), F32), jax.ShapeDtypeStruct((1, HEAD_DIM), F32),
                   jax.ShapeDtypeStruct((n_heads, 1, HEAD_DIM), F32)),
        scratch_shapes=[pltpu.VMEM((s, wd), BF16)] * 4 + [pltpu.VMEM((s, wd), F32)] * 3,
        compiler_params=_cparams(("arbitrary",), ATTN_VMEM_MB),
    )(p, p, p, gq, gk, ga, o, rsum, dmix)


_INV_SQRT2 = 0.7071067811865476
_INV_SQRT_2PI = 0.3989422804014327


def _gelu(x):
    return 0.5 * x * (1.0 + lax.erf(x * _INV_SQRT2))


def _gelu_grad(x):
    return 0.5 * (1.0 + lax.erf(x * _INV_SQRT2)) + x * (_INV_SQRT_2PI * jnp.exp(-0.5 * x * x))


def _sgu_fwd(p, gs, w_s, b_s, gb, col0, after):
    s = p.shape[0]
    n_chunks = s // TILE
    per_trip = _pick(n_chunks, (8, 4, 2, 1))

    def body(u_ref, v_ref, gs_ref, w_ref, b_ref, gb_ref, _, out_ref, vs_s):
        vg = _gelu(v_ref[...])
        vs_s[...] = (vg * _rstd(vg) * gs_ref[...]).astype(BF16)
        row, col = _tile_iotas()
        wt = jnp.where(col <= row, w_ref[...], 0.0).astype(BF16)
        bcol = b_ref[...]
        gbv = gb_ref[...]

        def chunks(c, _):
            rows = [pl.ds(pl.multiple_of((c * per_trip + k) * TILE, TILE), TILE) for k in range(per_trip)]
            mixed = [jnp.dot(wt, vs_s[r, :], preferred_element_type=F32) + bcol for r in rows]
            sgs = [_gelu(u_ref[r, :]) * mx for r, mx in zip(rows, mixed)]
            for r, sg in zip(rows, sgs):
                out_ref[r, :] = (sg * _rstd(sg) * gbv).astype(BF16)
            return 0

        lax.fori_loop(0, n_chunks // per_trip, chunks, 0)

    col_blk = lambda off: pl.BlockSpec((s, HEAD_DIM), lambda g: (0, off + g))
    gvec = pl.BlockSpec((None, 1, HEAD_DIM), lambda g: (g, 0, 0))
    return pl.pallas_call(
        body, name="sgu_fwd", grid=(N_GROUPS,),
        in_specs=[col_blk(col0), col_blk(col0 + N_GROUPS), gvec,
                  pl.BlockSpec((None, TILE, TILE), lambda g: (g, 0, 0)),
                  pl.BlockSpec((None, TILE, 1), lambda g: (g, 0, 0)), gvec,
                  pl.BlockSpec(after.shape, lambda g: (0, 0))],
        out_specs=pl.BlockSpec((s, HEAD_DIM), lambda g: (0, g)),
        out_shape=jax.ShapeDtypeStruct((s, N_GROUPS * HEAD_DIM), BF16),
        scratch_shapes=[pltpu.VMEM((s, HEAD_DIM), BF16)],
        compiler_params=_cparams(("parallel",)),
    )(p, p, gs, w_s, b_s, gb, after)


def _sgu_bwd(p, gs, w_s, b_s, gb, dmix, col0, dm_col0):
    s = p.shape[0]
    n_chunks = s // TILE
    per_trip = _pick(n_chunks, (8, 4, 2, 1))

    def body(u_ref, v_ref, gs_ref, w_ref, b_ref, gb_ref, dm_ref,
             du_ref, dv_ref, dgs_ref, dw_ref, db_ref, dgb_ref, vs_s, dvs_s):
        gsv = gs_ref[...]
        gbv = gb_ref[...]
        vg = _gelu(v_ref[...])
        vs_s[...] = (vg * _rstd(vg) * gsv).astype(BF16)
        row, col = _tile_iotas()
        causal = col <= row
        wt = jnp.where(causal, w_ref[...], 0.0).astype(BF16)
        bcol = b_ref[...]

        def chunks(c, carry):
            dw_acc, db_acc, dgb_acc = carry
            rows = [pl.ds(pl.multiple_of((c * per_trip + k) * TILE, TILE), TILE) for k in range(per_trip)]
            vss = [vs_s[r, :] for r in rows]
            mixed = [jnp.dot(wt, vs, preferred_element_type=F32) + bcol for vs in vss]
            dmbs = []
            for r, mx in zip(rows, mixed):
                u_pre = u_ref[r, :]
                u = _gelu(u_pre)
                sg = u * mx
                rs = _rstd(sg)
                sghat = sg * rs
                dm = dm_ref[r, :]
                dsg = _norm_bwd(dm, sghat, rs, gbv)
                dgb_acc = dgb_acc + jnp.sum(dm * sghat, axis=0, keepdims=True)
                du_ref[r, :] = (dsg * mx * _gelu_grad(u_pre)).astype(BF16)
                dmixed = dsg * u
                db_acc = db_acc + jnp.sum(dmixed, axis=1, keepdims=True)
                dmbs.append(dmixed.astype(BF16))
            for dmb, vs in zip(dmbs, vss):
                dw_acc = dw_acc + lax.dot_general(dmb, vs, NT_DIMS, preferred_element_type=F32)
            for r, dmb in zip(rows, dmbs):
                dvs_s[r, :] = lax.dot_general(wt, dmb, TN_DIMS, preferred_element_type=F32)
            return dw_acc, db_acc, dgb_acc

        dw_acc, db_acc, dgb_acc = lax.fori_loop(
            0, n_chunks // per_trip, chunks,
            (jnp.zeros((TILE, TILE), F32), jnp.zeros((TILE, 1), F32), jnp.zeros((1, HEAD_DIM), F32)))
        dw_ref[...] = jnp.where(causal, dw_acc, 0.0)
        db_ref[...] = db_acc
        dgb_ref[...] = dgb_acc
        v_pre = v_ref[...]
        vg = _gelu(v_pre)
        rv = _rstd(vg)
        vhat = vg * rv
        dvs = dvs_s[...]
        dgs_ref[...] = jnp.sum(dvs * vhat, axis=0, keepdims=True)
        dv_ref[...] = (_norm_bwd(dvs, vhat, rv, gsv) * _gelu_grad(v_pre)).astype(BF16)

    col_blk = lambda off: pl.BlockSpec((s, HEAD_DIM), lambda g: (0, off + g))
    gvec = pl.BlockSpec((None, 1, HEAD_DIM), lambda g: (g, 0, 0))
    wspec = pl.BlockSpec((None, TILE, TILE), lambda g: (g, 0, 0))
    bspec = pl.BlockSpec((None, TILE, 1), lambda g: (g, 0, 0))
    blk = pl.BlockSpec((s, HEAD_DIM), lambda g: (0, g))
    big = jax.ShapeDtypeStruct((s, N_GROUPS * HEAD_DIM), BF16)
    gshape = jax.ShapeDtypeStruct((N_GROUPS, 1, HEAD_DIM), F32)
    return pl.pallas_call(
        body, name="sgu_bwd", grid=(N_GROUPS,),
        in_specs=[col_blk(col0), col_blk(col0 + N_GROUPS), gvec, wspec, bspec, gvec, col_blk(dm_col0)],
        out_specs=(blk, blk, gvec, wspec, bspec, gvec),
        out_shape=(big, big, gshape, jax.ShapeDtypeStruct((N_GROUPS, TILE, TILE), F32),
                   jax.ShapeDtypeStruct((N_GROUPS, TILE, 1), F32), gshape),
        scratch_shapes=[pltpu.VMEM((s, HEAD_DIM), BF16), pltpu.VMEM((s, HEAD_DIM), F32)],
        compiler_params=_cparams(("parallel",)),
    )(p, p, gs, w_s, b_s, gb, dmix)


SUBLANES = 8


def _shift_down(x, n):
    rolled = pltpu.roll(x, n, 0)
    edge = lax.broadcasted_iota(jnp.int32, (SUBLANES, x.shape[1]), 0)
    return jnp.concatenate([jnp.where(edge >= n, rolled[:SUBLANES], 0.0), rolled[SUBLANES:]], axis=0)


def _shift_up(x, n):
    s = x.shape[0]
    rolled = pltpu.roll(x, s - n, 0)
    edge = lax.broadcasted_iota(jnp.int32, (SUBLANES, x.shape[1]), 0)
    return jnp.concatenate([rolled[:s - SUBLANES], jnp.where(edge < SUBLANES - n, rolled[s - SUBLANES:], 0.0)], axis=0)


def _conv(x, w, b):
    x1, x2 = _shift_down(x, 1), _shift_down(x, 2)
    return b + w[0:1, :] * x2 + w[1:2, :] * x1 + w[2:3, :] * x, x1, x2


def _conv_specs(s, tn):
    xspec = pl.BlockSpec((2, s, tn), lambda j: (0, 0, j))
    wspec = pl.BlockSpec((2, CONV_WIDTH, tn), lambda j: (0, 0, j))
    bspec = pl.BlockSpec((2, 1, tn), lambda j: (0, 0, j))
    return xspec, wspec, bspec


def _up_conv_gate_fwd(h, wb, cw, cb, after=None):
    s, k = h.shape
    nb, _, bn = wb.shape
    hb = nb // 2
    f = hb * bn
    tm = _pick(s, (512, 256, 128))
    n_in = 5 + (after is not None)

    def body(*refs):
        h_ref, wg_ref, wv_ref, w_ref, b_ref = refs[:5]
        up_ref, act_ref, halo_s = refs[n_in:]
        first = pl.program_id(1) == 0
        outs = []
        for half, wt_ref in enumerate((wg_ref, wv_ref)):
            x = jnp.dot(h_ref[...], wt_ref[...], preferred_element_type=F32)
            up_ref[half] = x
            halo = jnp.where(first, 0.0, halo_s[half])
            halo_s[half] = x[tm - SUBLANES:]
            full = jnp.concatenate([halo, x], axis=0)
            x1 = pltpu.roll(full, 1, 0)[SUBLANES:]
            x2 = pltpu.roll(full, 2, 0)[SUBLANES:]
            w = w_ref[half]
            outs.append(b_ref[half] + w[0:1, :] * x2 + w[1:2, :] * x1 + w[2:3, :] * x)
        gate, val = outs
        act_ref[...] = (gate * jax.nn.sigmoid(gate) * val).astype(BF16)

    ins = [h, wb, wb, cw, cb]
    in_specs = [pl.BlockSpec((tm, k), lambda j, i: (i, 0)),
                pl.BlockSpec((None, k, bn), lambda j, i: (j, 0, 0)),
                pl.BlockSpec((None, k, bn), lambda j, i: (j + hb, 0, 0)),
                pl.BlockSpec((2, CONV_WIDTH, bn), lambda j, i: (0, 0, j)),
                pl.BlockSpec((2, 1, bn), lambda j, i: (0, 0, j))]
    if after is not None:
        ins.append(after)
        in_specs.append(pl.BlockSpec(after.shape, lambda j, i: (0, 0)))
    return pl.pallas_call(
        body, name="up_conv_gate_fwd", grid=(hb, s // tm), in_specs=in_specs,
        out_specs=(pl.BlockSpec((2, tm, bn), lambda j, i: (0, i, j)), pl.BlockSpec((tm, bn), lambda j, i: (i, j))),
        out_shape=(jax.ShapeDtypeStruct((2, s, f), F32), jax.ShapeDtypeStruct((s, f), BF16)),
        scratch_shapes=[pltpu.VMEM((2, SUBLANES, bn), F32)],
        compiler_params=_cparams(("parallel", "arbitrary"), 56),
    )(*ins)


def _down_dx_conv_gate_bwd(up, cw, cb, dy, wdown, after):
    _, s, f = up.shape
    d = dy.shape[1]
    tn = _pick(f, (256, 128))

    def body(x_ref, w_ref, b_ref, dy_ref, wd_ref, _, dx_ref, dw_ref, db_ref):
        da = lax.dot_general(dy_ref[...], wd_ref[...], NT_DIMS, preferred_element_type=F32)
        xg, xv = x_ref[0], x_ref[1]
        wg, wv = w_ref[0], w_ref[1]
        gate, xg1, xg2 = _conv(xg, wg, b_ref[0])
        val, xv1, xv2 = _conv(xv, wv, b_ref[1])
        sig = jax.nn.sigmoid(gate)
        dval = da * (gate * sig)
        dgate = da * val * (sig * (1.0 + gate * (1.0 - sig)))
        for half, (x, x1, x2, w, dz) in enumerate(((xg, xg1, xg2, wg, dgate), (xv, xv1, xv2, wv, dval))):
            dx_ref[half] = (w[2:3, :] * dz + w[1:2, :] * _shift_up(dz, 1) + w[0:1, :] * _shift_up(dz, 2)).astype(BF16)
            dw_ref[half, 0:1, :] = jnp.sum(dz * x2, axis=0, keepdims=True)
            dw_ref[half, 1:2, :] = jnp.sum(dz * x1, axis=0, keepdims=True)
            dw_ref[half, 2:3, :] = jnp.sum(dz * x, axis=0, keepdims=True)
            db_ref[half] = jnp.sum(dz, axis=0, keepdims=True)

    xspec, wspec, bspec = _conv_specs(s, tn)
    return pl.pallas_call(
        body, name="down_dx_conv_gate_bwd", grid=(f // tn,),
        in_specs=[xspec, wspec, bspec, pl.BlockSpec((s, d), lambda j: (0, 0)), pl.BlockSpec((tn, d), lambda j: (j, 0)),
                  pl.BlockSpec(after.shape, lambda j: (0, 0))],
        out_specs=(xspec, wspec, bspec),
        out_shape=(jax.ShapeDtypeStruct((2, s, f), BF16), jax.ShapeDtypeStruct((2, CONV_WIDTH, f), F32),
                   jax.ShapeDtypeStruct((2, 1, f), F32)),
        compiler_params=_cparams(("parallel",), 56),
    )(up, cw, cb, dy, wdown, after)


def _mesh_pos():
    return lax.axis_index("x"), lax.axis_index("y"), lax.axis_index("c")


def _remote(src, dst, send_sem, recv_sem, to):
    return pltpu.make_async_remote_copy(src_ref=src, dst_ref=dst, send_sem=send_sem, recv_sem=recv_sem,
                                        device_id=to, device_id_type=pl.DeviceIdType.MESH)


HBM_SPEC = pl.BlockSpec(memory_space=pltpu.HBM)
SEM_SPEC = pl.BlockSpec(memory_space=pltpu.SEMAPHORE)
ANY_SPEC = pl.BlockSpec(memory_space=pl.ANY)
TOKEN_SPEC = pl.BlockSpec(memory_space=pltpu.VMEM)
TOKEN_SHAPE = jax.ShapeDtypeStruct((8, 128), F32)
DATAFLOW = pltpu.SideEffectType.DATAFLOW_SIDE_EFFECTING
GATHER_PLANE = (2, 4, 6)


def _slot(pos):
    return 4 * pos[0] + 2 * pos[1] + pos[2]


def _flip(pos, k):
    return (pos[0] ^ ((k >> 2) & 1), pos[1] ^ ((k >> 1) & 1), pos[2] ^ (k & 1))


def _hbm(a):
    return pltpu.with_memory_space_constraint(a, pltpu.HBM)


def _hbm_shapes(arrays):
    return tuple(pltpu.HBM(a.shape, a.dtype) for a in arrays)


class _Split:
    def __init__(self, n, outs, n_sets):
        k = 2 * n * int(n_sets)
        self.n = n
        self.sems = list(outs[:k])
        self.bufs = list(outs[k:k + 2 * n])
        self.token = outs[-1]

    def sem_set(self, i):
        return self.sems[2 * self.n * i:2 * self.n * (i + 1)]


def _split_call(name, body, bufs, sems_in, n_sets, after):
    n = len(bufs) // 2
    k = 2 * n * int(n_sets)
    m = len(sems_in)
    afters = list(after) if isinstance(after, (list, tuple)) else [after]
    na = len(afters)

    def wrapped(*refs):
        srcs, dsts = refs[:n], refs[n:2 * n]
        s_in = refs[2 * n:2 * n + m]
        s_out = refs[2 * n + m + na:2 * n + m + na + k]
        token, local_sems = refs[-2], refs[-1]
        body(srcs, dsts, s_in, s_out, local_sems)
        token[...] = jnp.zeros_like(token)

    outs = pl.pallas_call(
        wrapped, name=name,
        out_shape=(pltpu.SemaphoreType.DMA(()),) * k + _hbm_shapes(bufs) + (TOKEN_SHAPE,),
        in_specs=[HBM_SPEC] * (2 * n) + [SEM_SPEC] * m + [ANY_SPEC] * na,
        out_specs=(SEM_SPEC,) * k + (HBM_SPEC,) * (2 * n) + (TOKEN_SPEC,),
        input_output_aliases={i: k + i for i in range(2 * n)},
        scratch_shapes=[pltpu.SemaphoreType.DMA((n,))],
        compiler_params=pltpu.CompilerParams(has_side_effects=DATAFLOW),
    )(*[_hbm(b) for b in bufs], *sems_in, *afters)
    return _Split(n, outs, n_sets)


def _wait_slots(land, count, send_sem, recv_sem, me, send=False, recv=False):
    span = land.at[pl.ds(0, count)]
    cp = _remote(span, span, send_sem, recv_sem, me)
    if send:
        cp.wait_send()
    if recv:
        cp.wait_recv()


X_FLIP, Y_FLIP, DIAG_FLIP = 4, 2, 6
BF16_SUBLANES = 16


def _gather_start(name, shards, after):
    n = len(shards)
    my_slot = _slot(_mesh_pos())
    lands = [lax.dynamic_update_slice(lax.empty((N_DEV,) + w.shape, w.dtype), w[None], (my_slot, 0, 0)) for w in shards]

    def body(srcs, dsts, _, sems, local_sems):
        me = _mesh_pos()
        for a in range(n):
            for k in (1, X_FLIP, Y_FLIP):
                _remote(srcs[a], dsts[a].at[_slot(me)], sems[a], sems[n + a], _flip(me, k)).start()

    return _split_call(name, body, list(shards) + lands, [], 1, after)


def _gather_relay(name, started, after):
    n = started.n

    def body(srcs, dsts, sems_a, sems_out, local_sems):
        me = _mesh_pos()
        sibling = _flip(me, 1)
        pass_on, relay = sems_out[:2 * n], sems_out[2 * n:]
        for a in range(n):
            _wait_slots(dsts[a], 3, sems_a[a], sems_a[n + a], me, recv=True)
            from_x = dsts[a].at[_slot(_flip(me, X_FLIP))]
            from_y = dsts[a].at[_slot(_flip(me, Y_FLIP))]
            rows = srcs[a].shape[0]
            if rows % (2 * BF16_SUBLANES) == 0:
                top, bottom = pl.ds(0, rows // 2), pl.ds(rows // 2, rows // 2)
                _remote(from_y.at[top], from_y.at[top], relay[a], relay[n + a], _flip(me, X_FLIP)).start()
                _remote(from_x.at[bottom], from_x.at[bottom], relay[a], relay[n + a], _flip(me, Y_FLIP)).start()
            else:
                _remote(from_y, from_y, relay[a], relay[n + a], _flip(me, X_FLIP)).start()
            for block in (from_x, from_y):
                _remote(block, block, pass_on[a], pass_on[n + a], sibling).start()
        for a in range(n):
            _wait_slots(dsts[a], 3, sems_a[a], sems_a[n + a], me, send=True)

    return _split_call(name, body, started.bufs, started.sems, 2, after)


def _gather_relay_diagonal(name, relayed, after):
    n = relayed.n

    def body(srcs, dsts, relay, pass_on, local_sems):
        me = _mesh_pos()
        for a in range(n):
            _wait_slots(dsts[a], 1, relay[a], relay[n + a], me, recv=True)
            block = dsts[a].at[_slot(_flip(me, DIAG_FLIP))]
            _remote(block, block, pass_on[a], pass_on[n + a], _flip(me, 1)).start()
        for a in range(n):
            _wait_slots(dsts[a], 1, relay[a], relay[n + a], me, send=True)

    return _split_call(name, body, relayed.bufs, relayed.sem_set(1), 1, after)


def _gather_finish(name, relayed, diagonal, after):
    n = relayed.n

    def body(srcs, dsts, sems, _, local_sems):
        me = _mesh_pos()
        first, second = sems[:2 * n], sems[2 * n:]
        for a in range(n):
            _wait_slots(dsts[a], 2, first[a], first[n + a], me, send=True, recv=True)
            _wait_slots(dsts[a], 1, second[a], second[n + a], me, send=True, recv=True)

    return _split_call(name, body, diagonal.bufs, relayed.sem_set(0) + diagonal.sems, 0, after).bufs[n:]


def _exchange_start(name, blocked, after):
    n = len(blocked)
    my_slot = _slot(_mesh_pos())
    rows = [w.shape[-2] // (N_DEV if w.ndim == 2 else 1) for w in blocked]

    def block(ref, a, slot):
        if len(ref.shape) == 3:
            return ref.at[slot]
        return ref.at[pl.ds(pl.multiple_of(slot * rows[a], 16), rows[a])]

    lands = []
    for w, r in zip(blocked, rows):
        mine = lax.dynamic_slice_in_dim(w, my_slot, 1, 0) if w.ndim == 3 else lax.dynamic_slice_in_dim(w, my_slot * r, r, 0)[None]
        lands.append(lax.dynamic_update_slice(lax.empty((N_DEV, r, w.shape[-1]), w.dtype), mine, (my_slot, 0, 0)))

    def body(srcs, dsts, _, sems, local_sems):
        me = _mesh_pos()
        for a in range(n):
            for k in range(1, N_DEV):
                peer = _flip(me, k)
                _remote(block(srcs[a], a, _slot(peer)), dsts[a].at[_slot(me)], sems[a], sems[n + a], peer).start()

    return _split_call(name, body, list(blocked) + lands, [], True, after)


def _exchange_finish(name, started, after):
    group = started if isinstance(started, (list, tuple)) else [started]
    srcs_all = [b for st in group for b in st.bufs[:st.n]]
    lands_all = [b for st in group for b in st.bufs[st.n:]]
    sends = [s for st in group for s in st.sems[:st.n]]
    recvs = [s for st in group for s in st.sems[st.n:]]
    n = len(srcs_all)

    def body(srcs, dsts, sems, _, local_sems):
        me = _mesh_pos()
        for a in range(n):
            _wait_slots(dsts[a], N_DEV - 1, sems[a], sems[n + a], me, send=True, recv=True)

    return _split_call(name, body, srcs_all + lands_all, sends + recvs, 0, after).bufs[n:]


def _broadcast_start(name, arrays, after):
    n = len(arrays)
    my_slot = _slot(_mesh_pos())
    lands = [lax.dynamic_update_slice(lax.empty((N_DEV,) + w.shape, w.dtype), w[None], (my_slot, 0, 0)) for w in arrays]

    def body(srcs, dsts, _, sems, local_sems):
        me = _mesh_pos()
        for a in range(n):
            for k in range(1, N_DEV):
                _remote(srcs[a], dsts[a].at[_slot(me)], sems[a], sems[n + a], _flip(me, k)).start()

    return _split_call(name, body, list(arrays) + lands, [], True, after)


def _adamw_math(w, g, m, v):
    m = ADAM_B1 * m + (1.0 - ADAM_B1) * g
    v = ADAM_B2 * v + (1.0 - ADAM_B2) * (g * g)
    m_hat = m / (1.0 - ADAM_B1 ** ADAM_STEP)
    v_hat = v / (1.0 - ADAM_B2 ** ADAM_STEP)
    delta = -ADAM_LR * (m_hat / (jnp.sqrt(v_hat) + ADAM_EPS) + ADAM_WD * w)
    return delta, m, v


def _adamw(name, w, m, v, parts, layer, prev=None):
    _, r, c = w.shape
    tr = max(t for t in range(16, r + 1, 16) if r % t == 0 and t * c <= ADAMW_TILE_ELEMS)
    n_prev = 0 if prev is None else 4

    def body(*refs):
        w_ref, m_ref, v_ref, p_ref = refs[:4]
        g_ref, d_ref, nm_ref, nv_ref = refs[4 + n_prev:]
        g = p_ref[0].astype(F32)
        for src in range(1, N_DEV):
            g = g + p_ref[src].astype(F32)
        delta, nm, nv = _adamw_math(w_ref[...], g, m_ref[...], v_ref[...])
        g_ref[...] = g
        d_ref[...] = delta
        nm_ref[...] = nm
        nv_ref[...] = nv

    wspec = pl.BlockSpec((None, tr, c), lambda i: (layer, i, 0))
    pspec = pl.BlockSpec((N_DEV, tr, c), lambda i: (0, i, 0))
    shp = jax.ShapeDtypeStruct(w.shape, F32)
    return pl.pallas_call(
        body, name=name, grid=(r // tr,), in_specs=[wspec] * 3 + [pspec] + [ANY_SPEC] * n_prev,
        out_specs=(wspec,) * 4, out_shape=(shp,) * 4, input_output_aliases={4 + j: j for j in range(n_prev)},
        compiler_params=_cparams(("parallel",), 56),
    )(w, m, v, parts, *([] if prev is None else prev))


PACK_TILE = 8 * 128


def _pack(arrays):
    flat = []
    for a in arrays:
        v = a.reshape(-1)
        pad = (-v.shape[0]) % PACK_TILE
        flat.append(jnp.pad(v, (0, pad)) if pad else v)
    return jnp.concatenate(flat).reshape(-1, 128)


def _unpack(buf, like):
    flat = buf.reshape(-1)
    out, off = [], 0
    for a in like:
        n = 1
        for dim in a.shape:
            n *= dim
        out.append(flat[off:off + n].reshape(a.shape))
        off += n + (-n) % PACK_TILE
    return out


def _sum_slots(name, gathered):
    _, r, c = gathered.shape

    def body(x_ref, o_ref):
        acc = x_ref[0].astype(F32)
        for src in range(1, N_DEV):
            acc = acc + x_ref[src].astype(F32)
        o_ref[...] = acc

    return pl.pallas_call(body, name=name, out_shape=jax.ShapeDtypeStruct((r, c), F32))(gathered)


def _adamw_small(ws, gs, ms, vs):
    n = len(ws)

    def body(*refs):
        ins, outs = refs[:4 * n], refs[4 * n:]
        for i in range(n):
            delta, nm, nv = _adamw_math(ins[i][...], ins[n + i][...], ins[2 * n + i][...], ins[3 * n + i][...])
            outs[i][...] = delta
            outs[n + i][...] = nm
            outs[2 * n + i][...] = nv

    shapes = tuple(jax.ShapeDtypeStruct(w.shape, F32) for w in ws) * 3
    out = pl.pallas_call(body, name="adamw_small", out_shape=shapes)(*ws, *gs, *ms, *vs)
    return out[:n], out[n:2 * n], out[2 * n:]


def kernel(x, attn_norm_g, w_in, q_norm_g, k_norm_g, sgu_norm_g, sgu_w, sgu_b, out_norm_a_g, out_norm_b_g, w_out, ffn_norm_g, w_up, conv_w, conv_b, w_down, loss_target, m_attn_norm_g, m_w_in, m_q_norm_g, m_k_norm_g, m_sgu_norm_g, m_sgu_w, m_sgu_b, m_out_norm_a_g, m_out_norm_b_g, m_w_out, m_ffn_norm_g, m_w_up, m_conv_w, m_conv_b, m_w_down, v_attn_norm_g, v_w_in, v_q_norm_g, v_k_norm_g, v_sgu_norm_g, v_sgu_w, v_sgu_b, v_out_norm_a_g, v_out_norm_b_g, v_w_out, v_ffn_norm_g, v_w_up, v_conv_w, v_conv_b, v_w_down):
    depth = w_in.shape[0]
    s, d = x.shape[1], x.shape[2]
    n_heads = (d // 2) // HEAD_DIM
    sgu_col0 = 3 * n_heads
    f2 = w_up.shape[2] * N_DEV
    ff = f2 // 2
    my_slot = 4 * lax.axis_index("x") + 2 * lax.axis_index("y") + lax.axis_index("c")

    wb = [(w_in[l].astype(BF16), w_out[l].astype(BF16), w_up[l].astype(BF16), w_down[l].astype(BF16))
          for l in range(depth)]
    groups = {"in0": [wb[0][0]], "out0": [wb[0][1], conv_w.reshape(depth * CONV_WIDTH, -1)], "up0": [wb[0][2]],
              "down0": [wb[0][3]]}
    for l in range(1, depth):
        groups[f"in{l}"] = [wb[l][0], wb[l][1]]
        groups[f"ffn{l}"] = [wb[l][2], wb[l][3]]
    order = list(groups)
    started, relayed = {}, {}

    def start(gname, after):
        started[gname] = _gather_start(f"gather_{gname}_start", groups[gname], after)
        return started[gname].token

    def relay(gname, after):
        relayed[gname] = _gather_relay(f"gather_{gname}_relay", started[gname], after)
        token = relayed[gname].token
        k = order.index(gname)
        nxt = [k + 2] if k + 2 < len(order) - 1 else []
        if k == len(order) - 2:
            nxt = [k + 1]
        for j in nxt:
            token = start(order[j], token)
        return token

    def finish(gname, after):
        diagonal = _gather_relay_diagonal(f"gather_{gname}_diagonal", relayed[gname], after)
        return _gather_finish(f"gather_{gname}_finish", relayed[gname], diagonal, diagonal.token)

    conv_b_all = conv_b.reshape(depth, 2, 1, ff)
    sgu_b_col = sgu_b[..., None]
    token = start(order[1], start(order[0], attn_norm_g))
    token = relay("out0", relay("in0", token))
    win_g = finish("in0", token)[0]

    xs = x[0]
    saved = []
    gathered = []
    for l in range(depth):
        g1 = attn_norm_g[l][None]
        g2 = ffn_norm_g[l][None]
        gq, gk = q_norm_g[l][None], k_norm_g[l][None]
        ga = out_norm_a_g[l][:, None, :]
        gs = sgu_norm_g[l][:, None, :]
        gb = out_norm_b_g[l][:, None, :]
        h1 = _rmsnorm_fwd("attn_norm_fwd", xs, g1)
        p = _mm_nn_blocked("in_proj", h1, win_g, F32)
        groups_of_heads = n_heads // ATTN_HEADS_PER_STEP
        if l == 0 or groups_of_heads < 2:
            att, o, rsum = _attn_fwd(p, gq, gk, ga, n_heads)
            token = relay("up0" if l == 0 else f"ffn{l}", att)
        else:
            half = groups_of_heads // 2
            part = _attn_fwd(p, gq, gk, ga, n_heads, 0, half)
            token = relay(f"ffn{l}", part[0])
            att, o, rsum = _attn_fwd(p, gq, gk, ga, n_heads, half, groups_of_heads - half, part, after=token)
        sg = _sgu_fwd(p, gs, sgu_w[l], sgu_b_col[l], gb, sgu_col0, token)
        mix = jnp.concatenate([att, sg], axis=-1)
        if l == 0:
            wout_g, cw = finish("out0", mix)
            cw = jnp.transpose(cw.reshape(N_DEV, depth, CONV_WIDTH, -1), (1, 2, 0, 3)).reshape(depth, CONV_WIDTH, 2, ff)
            conv_w_all = jnp.transpose(cw, (0, 2, 1, 3))
        x1, h2 = _mm_nn_res_norm("out_proj_ffn_norm", mix, wout_g.reshape(d, d), xs, g2)
        if l == 0:
            wup_g = finish("up0", h2)[0]
            token = relay("down0", wup_g)
            up, act = _up_conv_gate_fwd(h2, wup_g, conv_w_all[l], conv_b_all[l], after=token)
            wdown_g = finish("down0", up)[0]
        else:
            wup_g, wdown_g = finish(f"ffn{l}", h2)
            up, act = _up_conv_gate_fwd(h2, wup_g, conv_w_all[l], conv_b_all[l])
        saved.append((xs, h1, p, o, rsum, mix, x1, h2, up, act))
        gathered.append((win_g, wout_g, wup_g, wdown_g))
        if l + 1 < depth:
            token = relay(f"in{l + 1}", act)
            x2 = _mm_nn_res("down_proj", act, wdown_g.reshape(ff, d), x1, after=token)
            win_g, wout_g = finish(f"in{l + 1}", x2)
        else:
            x2 = _mm_nn_res("down_proj", act, wdown_g.reshape(ff, d), x1)
        xs = x2

    dx, dxb, loss_vec = _loss_head(xs, loss_target[0])
    loss = lax.psum(loss_vec[0, 0], MESH_AXES)

    exchanges = []
    small = [None] * depth
    small_names = ["attn_norm_g", "q_norm_g", "k_norm_g", "sgu_norm_g", "sgu_w", "sgu_b", "out_norm_a_g",
                   "out_norm_b_g", "ffn_norm_g", "conv_b", "conv_w"]
    for l in reversed(range(depth)):
        xs0, h1, p, o, rsum, mix, x1, h2, up, act = saved[l]
        win_g, wout_g, wup_g, wdown_g = gathered[l]
        wout_full = wout_g.reshape(d, d)
        wdown_full = wdown_g.reshape(ff, d)
        g1 = attn_norm_g[l][None]
        g2 = ffn_norm_g[l][None]
        gq, gk = q_norm_g[l][None], k_norm_g[l][None]
        ga = out_norm_a_g[l][:, None, :]
        gs = sgu_norm_g[l][:, None, :]
        gb = out_norm_b_g[l][:, None, :]
        d_wdown = _mm_tn_plain("down_proj_dw", act, dxb)
        exchanges.append((l, "down", ("w_down",), _exchange_start(f"grad_down{l}_start", [d_wdown], dx)))
        dup, d_cw, d_cb = _down_dx_conv_gate_bwd(up, conv_w_all[l], conv_b_all[l], dxb, wdown_full,
                                                 exchanges[-1][3].token)
        d_wup = _mm_tn_blocked("up_proj_dw", h2, dup, N_DEV, halves=True)
        exchanges.append((l, "up", ("w_up",), _exchange_start(f"grad_up{l}_start", [d_wup], d_cb)))
        dh2 = _mm_nt_blocked("up_proj_dx", dup, wup_g, halves=True, after=exchanges[-1][3].token)
        dx, dxb, d_g2 = _rmsnorm_bwd("ffn_norm_bwd", dh2, x1, g2, dx)
        d_wout = _mm_tn_plain("out_proj_dw", mix, dxb)
        exchanges.append((l, "out", ("w_out",), _exchange_start(f"grad_out{l}_start", [d_wout], d_g2)))
        dmix = _mm_nt_plain("out_proj_dx", dxb, wout_full, after=exchanges[-1][3].token)
        dq, dk, dv, d_gq, d_gk, d_ga = _attn_bwd(p, gq, gk, ga, o, rsum, dmix, n_heads)
        du, dvs, d_gs, d_sw, d_sb, d_gb = _sgu_bwd(p, gs, sgu_w[l], sgu_b_col[l], gb, dmix, sgu_col0, n_heads)
        dp = jnp.concatenate([dq, dk, dv, du, dvs], axis=-1)
        d_win = _mm_tn_blocked("in_proj_dw", h1, dp, N_DEV)
        exchanges.append((l, "in", ("w_in",), _exchange_start(f"grad_in{l}_start", [d_win], d_gq)))
        dh1 = _mm_nt_blocked("in_proj_dx", dp, win_g, after=exchanges[-1][3].token)
        dx, dxb, d_g1 = _rmsnorm_bwd("attn_norm_bwd", dh1, xs0, g1, dx)
        small[l] = dict(attn_norm_g=d_g1[0], q_norm_g=d_gq[0], k_norm_g=d_gk[0], sgu_norm_g=d_gs[:, 0], sgu_w=d_sw,
                        sgu_b=d_sb[..., 0], out_norm_a_g=d_ga[:, 0], out_norm_b_g=d_gb[:, 0], ffn_norm_g=d_g2[0],
                        conv_w=jnp.transpose(d_cw, (1, 0, 2)).reshape(CONV_WIDTH, f2), conv_b=d_cb.reshape(f2))
    grad_x = dx[None]

    f32_names = [n for n in small_names if n != "sgu_w"]
    small_g = [jnp.stack([small[l][n] for l in range(depth)]) for n in f32_names]
    sgu_w_g = jnp.stack([small[l]["sgu_w"] for l in range(depth)])
    small_sent = _broadcast_start("grad_small_start", [_pack(small_g), sgu_w_g.reshape(-1, TILE).astype(BF16)], dx)

    res = {}
    big = dict(w_in=(w_in, m_w_in, v_w_in), w_out=(w_out, m_w_out, v_w_out), w_up=(w_up, m_w_up, v_w_up),
               w_down=(w_down, m_w_down, v_w_down))
    after = [small_sent.token]
    batches = [[e for e in exchanges if e[0] == l] for l in reversed(range(depth))]
    batches = batches[:-1] + [batches[-1][:-1], batches[-1][-1:]]
    for i, batch in enumerate(batches):
        landed = _exchange_finish(f"grad_batch{i}_finish", [ex for _, _, _, ex in batch], after)
        after = []
        for (layer, name), parts in zip([(k, n) for k, _, names, _ in batch for n in names], landed):
            w, m, v = big[name]
            res[name] = _adamw(f"adamw_{name}", w, m, v, parts, layer, res.get(name))
            after.append(res[name][0])
    small_all, sgu_w_all = _exchange_finish("grad_small_finish", small_sent, after)
    small_sum = _unpack(_sum_slots("small_grad_sum", small_all), small_g)
    g_small = dict(zip(f32_names, small_sum))
    g_small["sgu_w"] = _sum_slots("sgu_w_grad_sum", sgu_w_all).reshape(sgu_w.shape)
    cwn = conv_w.shape[2]
    g_small["conv_w"] = lax.dynamic_slice_in_dim(g_small["conv_w"], my_slot * cwn, cwn, axis=2)
    small_w = dict(attn_norm_g=(attn_norm_g, m_attn_norm_g, v_attn_norm_g), q_norm_g=(q_norm_g, m_q_norm_g, v_q_norm_g),
                   k_norm_g=(k_norm_g, m_k_norm_g, v_k_norm_g), sgu_norm_g=(sgu_norm_g, m_sgu_norm_g, v_sgu_norm_g),
                   sgu_w=(sgu_w, m_sgu_w, v_sgu_w), sgu_b=(sgu_b, m_sgu_b, v_sgu_b),
                   out_norm_a_g=(out_norm_a_g, m_out_norm_a_g, v_out_norm_a_g),
                   out_norm_b_g=(out_norm_b_g, m_out_norm_b_g, v_out_norm_b_g),
                   ffn_norm_g=(ffn_norm_g, m_ffn_norm_g, v_ffn_norm_g), conv_b=(conv_b, m_conv_b, v_conv_b),
                   conv_w=(conv_w, m_conv_w, v_conv_w))
    grads = [g_small[n].reshape(small_w[n][0].shape) for n in small_names]
    deltas, new_ms, new_vs = _adamw_small([small_w[n][0] for n in small_names], grads,
                                          [small_w[n][1] for n in small_names], [small_w[n][2] for n in small_names])
    for n, g, dlt, nm, nv in zip(small_names, grads, deltas, new_ms, new_vs):
        res[n] = (g, dlt, nm, nv)

    order = ["attn_norm_g", "w_in", "q_norm_g", "k_norm_g", "sgu_norm_g", "sgu_w", "sgu_b", "out_norm_a_g",
             "out_norm_b_g", "w_out", "ffn_norm_g", "w_up", "conv_w", "conv_b", "w_down"]
    outs = [loss, grad_x]
    for field in range(4):
        outs += [res[n][field] for n in order]
    return tuple(outs)
```
